```python
import math
import jax, jax.numpy as jnp
from jax import lax
import numpy as np

D_MODEL = 1024
BATCH = 16
SEQ = 2048
DEPTH = 4

N_MIXERS = 2
N_FOX = (DEPTH + 1) // 2
N_MLA = DEPTH // 2
FOX_HEADS = 16
FOX_HEAD_DIM = D_MODEL // FOX_HEADS
MLA_HEADS = 16
MLA_NOPE_DIM = D_MODEL // MLA_HEADS
MLA_ROPE_DIM = MLA_NOPE_DIM // 2
MLA_V_DIM = D_MODEL // MLA_HEADS
MLA_Q_RANK = D_MODEL // 4
MLA_KV_RANK = D_MODEL // 8
D_FF = 4 * D_MODEL
Q_BLOCK = 128
ROPE_THETA = 10000.0
NORM_EPS = 1e-6
N_MOD = 6

kernel_name = "hybrid_fox_mla_adaln_trunk"


def rms_norm(x, g):
    x32 = x.astype(jnp.float32)
    y = x32 * lax.rsqrt(jnp.mean(x32 * x32, axis=-1, keepdims=True) + NORM_EPS)
    return y.astype(x.dtype) * g


def causal_block_attention(logits_fn, v):
    S = v.shape[1]
    outs = []
    for qb in range(S // Q_BLOCK):
        q0, q1 = qb * Q_BLOCK, (qb + 1) * Q_BLOCK
        logits = logits_fn(q0, q1)
        allowed = jnp.arange(q1)[None, :] <= jnp.arange(q0, q1)[:, None]
        logits = jnp.where(allowed, logits, -jnp.inf)
        p = jax.nn.softmax(logits, axis=-1).astype(v.dtype)
        outs.append(jnp.einsum('bhqk,bkhd->bqhd', p, v[:, :q1]))
    return jnp.concatenate(outs, axis=1)


def rope_cos_sin(positions, dim):
    inv_freq = ROPE_THETA ** (-jnp.arange(0, dim, 2, dtype=jnp.float32) / dim)
    ang = positions.astype(jnp.float32)[..., None] * inv_freq
    return jnp.cos(ang), jnp.sin(ang)


def apply_rope(x, cos, sin):
    half = x.shape[-1] // 2
    x1, x2 = x[..., :half], x[..., half:]
    cos = cos.astype(x.dtype)
    sin = sin.astype(x.dtype)
    return jnp.concatenate([x1 * cos - x2 * sin, x2 * cos + x1 * sin], axis=-1)


def fox_mixer(h, w_in, b_f, w_out):
    B, S, _ = h.shape
    proj = h @ w_in
    q = proj[..., :D_MODEL].reshape(B, S, FOX_HEADS, FOX_HEAD_DIM)
    k = proj[..., D_MODEL:2 * D_MODEL].reshape(B, S, FOX_HEADS, FOX_HEAD_DIM)
    v = proj[..., 2 * D_MODEL:3 * D_MODEL].reshape(B, S, FOX_HEADS, FOX_HEAD_DIM)
    log_f = jax.nn.log_sigmoid((proj[..., 3 * D_MODEL:] + b_f).astype(jnp.float32))
    F = jnp.cumsum(log_f, axis=1).transpose(0, 2, 1)
    scale = FOX_HEAD_DIM ** -0.5

    def logits_fn(q0, q1):
        s = jnp.einsum('bqhd,bkhd->bhqk', q[:, q0:q1], k[:, :q1],
                       preferred_element_type=jnp.float32)
        return s * scale + (F[:, :, q0:q1, None] - F[:, :, None, :q1])

    o = causal_block_attention(logits_fn, v)
    return o.reshape(B, S, D_MODEL) @ w_out


def mla_mixer(h, cos, sin, w_dq, q_norm_g, w_uq, w_dkv, kv_norm_g, w_ukv, w_out):
    B, S, _ = h.shape
    cq = rms_norm(h @ w_dq, q_norm_g)
    q = (cq @ w_uq).reshape(B, S, MLA_HEADS, MLA_NOPE_DIM + MLA_ROPE_DIM)
    q_nope = q[..., :MLA_NOPE_DIM]
    q_rope = apply_rope(q[..., MLA_NOPE_DIM:], cos[:, :, None, :], sin[:, :, None, :])
    dkv = h @ w_dkv
    ckv = rms_norm(dkv[..., :MLA_KV_RANK], kv_norm_g)
    k_rope = apply_rope(dkv[..., MLA_KV_RANK:], cos, sin)
    kv = (ckv @ w_ukv).reshape(B, S, MLA_HEADS, MLA_NOPE_DIM + MLA_V_DIM)
    k_nope = kv[..., :MLA_NOPE_DIM]
    v = kv[..., MLA_NOPE_DIM:]
    scale = (MLA_NOPE_DIM + MLA_ROPE_DIM) ** -0.5

    def logits_fn(q0, q1):
        s = jnp.einsum('bqhd,bkhd->bhqk', q_nope[:, q0:q1], k_nope[:, :q1],
                       preferred_element_type=jnp.float32)
        s = s + jnp.einsum('bqhd,bkd->bhqk', q_rope[:, q0:q1], k_rope[:, :q1],
                           preferred_element_type=jnp.float32)
        return s * scale

    o = causal_block_attention(logits_fn, v)
    return o.reshape(B, S, MLA_HEADS * MLA_V_DIM) @ w_out


def sq_relu_mlp(h, w1, w2):
    a = jax.nn.relu(h @ w1)
    return (a * a) @ w2


def _fwd_setup_inputs(seed: int = 0) -> dict:
    key = jax.random.key(seed)
    ks = iter(jax.random.split(key, 32))
    D = D_MODEL

    def nrm(shape, fan_in, mult=1.0):
        return jax.random.normal(next(ks), shape, jnp.float32) * (mult * fan_in ** -0.5)

    def gain(shape):
        return 1.0 + 0.02 * jax.random.normal(next(ks), shape, jnp.float32)

    x = jax.random.normal(next(ks), (BATCH, SEQ, D), jnp.float32)
    c = jax.random.normal(next(ks), (BATCH, D), jnp.float32)
    offs = jax.random.randint(next(ks), (BATCH, 1), 0, 4096, dtype=jnp.int32)
    positions = (jnp.arange(SEQ, dtype=jnp.int32)[None, :] + offs).astype(jnp.int32)
    return {
        "x": x,
        "c": c,
        "positions": positions,
        "ada_w": nrm((DEPTH, D, N_MOD * D), D),
        "ada_b": 0.02 * jax.random.normal(next(ks), (DEPTH, N_MOD * D), jnp.float32),
        "norm_mix_g": gain((DEPTH, D)),
        "norm_mlp_g": gain((DEPTH, D)),
        "fox_w_in": nrm((N_FOX, D, 3 * D + FOX_HEADS), D),
        "fox_b_f": jax.random.uniform(next(ks), (N_FOX, FOX_HEADS), jnp.float32, 2.0, 6.0),
        "fox_w_out": nrm((N_FOX, D, D), D),
        "mla_w_dq": nrm((N_MLA, D, MLA_Q_RANK), D),
        "mla_q_norm_g": gain((N_MLA, MLA_Q_RANK)),
        "mla_w_uq": nrm((N_MLA, MLA_Q_RANK, MLA_HEADS * (MLA_NOPE_DIM + MLA_ROPE_DIM)), MLA_Q_RANK),
        "mla_w_dkv": nrm((N_MLA, D, MLA_KV_RANK + MLA_ROPE_DIM), D),
        "mla_kv_norm_g": gain((N_MLA, MLA_KV_RANK)),
        "mla_w_ukv": nrm((N_MLA, MLA_KV_RANK, MLA_HEADS * (MLA_NOPE_DIM + MLA_V_DIM)), MLA_KV_RANK),
        "mla_w_out": nrm((N_MLA, MLA_HEADS * MLA_V_DIM, D), MLA_HEADS * MLA_V_DIM),
        "mlp_w1": nrm((DEPTH, D, D_FF), D),
        "mlp_w2": nrm((DEPTH, D_FF, D), D_FF),
        "final_norm_g": gain((D,)),
    }


def _fwd_reference(x, c, positions, ada_w, ada_b, norm_mix_g, norm_mlp_g,
              fox_w_in, fox_b_f, fox_w_out,
              mla_w_dq, mla_q_norm_g, mla_w_uq, mla_w_dkv, mla_kv_norm_g, mla_w_ukv, mla_w_out,
              mlp_w1, mlp_w2, final_norm_g):
    cos, sin = rope_cos_sin(positions, MLA_ROPE_DIM)
    c_act = jax.nn.silu(c)
    for i in range(DEPTH):
        mod = (c_act @ ada_w[i] + ada_b[i])[:, None, :]
        sh_m, sc_m, g_m, sh_f, sc_f, g_f = jnp.split(mod, N_MOD, axis=-1)
        h = rms_norm(x, norm_mix_g[i]) * (1 + sc_m) + sh_m
        j = i // N_MIXERS
        if i % N_MIXERS == 0:
            y = fox_mixer(h, fox_w_in[j], fox_b_f[j], fox_w_out[j])
        else:
            y = mla_mixer(h, cos, sin, mla_w_dq[j], mla_q_norm_g[j], mla_w_uq[j],
                          mla_w_dkv[j], mla_kv_norm_g[j], mla_w_ukv[j], mla_w_out[j])
        x = x + g_m * y
        h = rms_norm(x, norm_mlp_g[i]) * (1 + sc_f) + sh_f
        x = x + g_f * sq_relu_mlp(h, mlp_w1[i], mlp_w2[i])
    return rms_norm(x, final_norm_g)


import jax as _jax
import jax.numpy as _jnp

TWIN_FORMAT = 'train_step'
FWD_PARAMS = ['x', 'c', 'positions', 'ada_w', 'ada_b', 'norm_mix_g', 'norm_mlp_g', 'fox_w_in', 'fox_b_f', 'fox_w_out', 'mla_w_dq', 'mla_q_norm_g', 'mla_w_uq', 'mla_w_dkv', 'mla_kv_norm_g', 'mla_w_ukv', 'mla_w_out', 'mlp_w1', 'mlp_w2', 'final_norm_g']
TWIN_WEIGHTS = ['ada_w', 'ada_b', 'norm_mix_g', 'norm_mlp_g', 'fox_w_in', 'fox_b_f', 'fox_w_out', 'mla_w_dq', 'mla_q_norm_g', 'mla_w_uq', 'mla_w_dkv', 'mla_kv_norm_g', 'mla_w_ukv', 'mla_w_out', 'mlp_w1', 'mlp_w2', 'final_norm_g']
TWIN_DIFF_INPUT = 'x'
TWIN_INPUTS = ['x', 'c', 'positions', 'ada_w', 'ada_b', 'norm_mix_g', 'norm_mlp_g', 'fox_w_in', 'fox_b_f', 'fox_w_out', 'mla_w_dq', 'mla_q_norm_g', 'mla_w_uq', 'mla_w_dkv', 'mla_kv_norm_g', 'mla_w_ukv', 'mla_w_out', 'mlp_w1', 'mlp_w2', 'final_norm_g', 'loss_target', 'm_ada_w', 'm_ada_b', 'm_norm_mix_g', 'm_norm_mlp_g', 'm_fox_w_in', 'm_fox_b_f', 'm_fox_w_out', 'm_mla_w_dq', 'm_mla_q_norm_g', 'm_mla_w_uq', 'm_mla_w_dkv', 'm_mla_kv_norm_g', 'm_mla_w_ukv', 'm_mla_w_out', 'm_mlp_w1', 'm_mlp_w2', 'm_final_norm_g', 'v_ada_w', 'v_ada_b', 'v_norm_mix_g', 'v_norm_mlp_g', 'v_fox_w_in', 'v_fox_b_f', 'v_fox_w_out', 'v_mla_w_dq', 'v_mla_q_norm_g', 'v_mla_w_uq', 'v_mla_w_dkv', 'v_mla_kv_norm_g', 'v_mla_w_ukv', 'v_mla_w_out', 'v_mlp_w1', 'v_mlp_w2', 'v_final_norm_g']
TWIN_OUTPUTS = ['loss', 'grad_x', 'grad_ada_w', 'grad_ada_b', 'grad_norm_mix_g', 'grad_norm_mlp_g', 'grad_fox_w_in', 'grad_fox_b_f', 'grad_fox_w_out', 'grad_mla_w_dq', 'grad_mla_q_norm_g', 'grad_mla_w_uq', 'grad_mla_w_dkv', 'grad_mla_kv_norm_g', 'grad_mla_w_ukv', 'grad_mla_w_out', 'grad_mlp_w1', 'grad_mlp_w2', 'grad_final_norm_g', 'delta_ada_w', 'delta_ada_b', 'delta_norm_mix_g', 'delta_norm_mlp_g', 'delta_fox_w_in', 'delta_fox_b_f', 'delta_fox_w_out', 'delta_mla_w_dq', 'delta_mla_q_norm_g', 'delta_mla_w_uq', 'delta_mla_w_dkv', 'delta_mla_kv_norm_g', 'delta_mla_w_ukv', 'delta_mla_w_out', 'delta_mlp_w1', 'delta_mlp_w2', 'delta_final_norm_g', 'new_m_ada_w', 'new_m_ada_b', 'new_m_norm_mix_g', 'new_m_norm_mlp_g', 'new_m_fox_w_in', 'new_m_fox_b_f', 'new_m_fox_w_out', 'new_m_mla_w_dq', 'new_m_mla_q_norm_g', 'new_m_mla_w_uq', 'new_m_mla_w_dkv', 'new_m_mla_kv_norm_g', 'new_m_mla_w_ukv', 'new_m_mla_w_out', 'new_m_mlp_w1', 'new_m_mlp_w2', 'new_m_final_norm_g', 'new_v_ada_w', 'new_v_ada_b', 'new_v_norm_mix_g', 'new_v_norm_mlp_g', 'new_v_fox_w_in', 'new_v_fox_b_f', 'new_v_fox_w_out', 'new_v_mla_w_dq', 'new_v_mla_q_norm_g', 'new_v_mla_w_uq', 'new_v_mla_w_dkv', 'new_v_mla_kv_norm_g', 'new_v_mla_w_ukv', 'new_v_mla_w_out', 'new_v_mlp_w1', 'new_v_mlp_w2', 'new_v_final_norm_g']
TWIN_LEAF_KINDS = {'loss': 'loss', 'grad_x': 'grad_x', 'grad_ada_w': 'grad_w', 'grad_ada_b': 'grad_w', 'grad_norm_mix_g': 'grad_w', 'grad_norm_mlp_g': 'grad_w', 'grad_fox_w_in': 'grad_w', 'grad_fox_b_f': 'grad_w', 'grad_fox_w_out': 'grad_w', 'grad_mla_w_dq': 'grad_w', 'grad_mla_q_norm_g': 'grad_w', 'grad_mla_w_uq': 'grad_w', 'grad_mla_w_dkv': 'grad_w', 'grad_mla_kv_norm_g': 'grad_w', 'grad_mla_w_ukv': 'grad_w', 'grad_mla_w_out': 'grad_w', 'grad_mlp_w1': 'grad_w', 'grad_mlp_w2': 'grad_w', 'grad_final_norm_g': 'grad_w', 'delta_ada_w': 'delta_w', 'delta_ada_b': 'delta_w', 'delta_norm_mix_g': 'delta_w', 'delta_norm_mlp_g': 'delta_w', 'delta_fox_w_in': 'delta_w', 'delta_fox_b_f': 'delta_w', 'delta_fox_w_out': 'delta_w', 'delta_mla_w_dq': 'delta_w', 'delta_mla_q_norm_g': 'delta_w', 'delta_mla_w_uq': 'delta_w', 'delta_mla_w_dkv': 'delta_w', 'delta_mla_kv_norm_g': 'delta_w', 'delta_mla_w_ukv': 'delta_w', 'delta_mla_w_out': 'delta_w', 'delta_mlp_w1': 'delta_w', 'delta_mlp_w2': 'delta_w', 'delta_final_norm_g': 'delta_w', 'new_m_ada_w': 'new_m', 'new_m_ada_b': 'new_m', 'new_m_norm_mix_g': 'new_m', 'new_m_norm_mlp_g': 'new_m', 'new_m_fox_w_in': 'new_m', 'new_m_fox_b_f': 'new_m', 'new_m_fox_w_out': 'new_m', 'new_m_mla_w_dq': 'new_m', 'new_m_mla_q_norm_g': 'new_m', 'new_m_mla_w_uq': 'new_m', 'new_m_mla_w_dkv': 'new_m', 'new_m_mla_kv_norm_g': 'new_m', 'new_m_mla_w_ukv': 'new_m', 'new_m_mla_w_out': 'new_m', 'new_m_mlp_w1': 'new_m', 'new_m_mlp_w2': 'new_m', 'new_m_final_norm_g': 'new_m', 'new_v_ada_w': 'new_v', 'new_v_ada_b': 'new_v', 'new_v_norm_mix_g': 'new_v', 'new_v_norm_mlp_g': 'new_v', 'new_v_fox_w_in': 'new_v', 'new_v_fox_b_f': 'new_v', 'new_v_fox_w_out': 'new_v', 'new_v_mla_w_dq': 'new_v', 'new_v_mla_q_norm_g': 'new_v', 'new_v_mla_w_uq': 'new_v', 'new_v_mla_w_dkv': 'new_v', 'new_v_mla_kv_norm_g': 'new_v', 'new_v_mla_w_ukv': 'new_v', 'new_v_mla_w_out': 'new_v', 'new_v_mlp_w1': 'new_v', 'new_v_mlp_w2': 'new_v', 'new_v_final_norm_g': 'new_v'}


def _forward(args):
    return _fwd_reference(*[args[k] for k in FWD_PARAMS])


def _output_shape():
    out = _jax.eval_shape(lambda: _forward(_fwd_setup_inputs(0)))
    return out.shape, out.dtype

N_MICROBATCH = 1
ADAM_LR = 0.001
ADAM_B1 = 0.9
ADAM_B2 = 0.999
ADAM_EPS = 1e-08
ADAM_WD = 0.01
ADAM_STEP = 10
PER_EXAMPLE_BATCH_AXIS = {'x': 0, 'c': 0, 'positions': 0, 'loss_target': 0}
SHARED_INPUTS = []
_WEIGHT_DTYPES = {'ada_w': _jnp.float32, 'ada_b': _jnp.float32, 'norm_mix_g': _jnp.float32, 'norm_mlp_g': _jnp.float32, 'fox_w_in': _jnp.float32, 'fox_b_f': _jnp.float32, 'fox_w_out': _jnp.float32, 'mla_w_dq': _jnp.float32, 'mla_q_norm_g': _jnp.float32, 'mla_w_uq': _jnp.float32, 'mla_w_dkv': _jnp.float32, 'mla_kv_norm_g': _jnp.float32, 'mla_w_ukv': _jnp.float32, 'mla_w_out': _jnp.float32, 'mlp_w1': _jnp.float32, 'mlp_w2': _jnp.float32, 'final_norm_g': _jnp.float32}
MOMENT_SCALE = {'ada_w': 8.406224e-02, 'ada_b': 1.420003e-01, 'norm_mix_g': 5.317092e-02, 'norm_mlp_g': 1.183146e-01, 'fox_w_in': 5.686259e-02, 'fox_b_f': 1.323384e-01, 'fox_w_out': 8.472550e-02, 'mla_w_dq': 1.276956e-02, 'mla_q_norm_g': 1.333140e-02, 'mla_w_uq': 5.152231e-03, 'mla_w_dkv': 1.415652e-01, 'mla_kv_norm_g': 1.670908e-01, 'mla_w_ukv': 3.799369e-02, 'mla_w_out': 5.574292e-02, 'mlp_w1': 7.154538e-02, 'mlp_w2': 1.407458e-01, 'final_norm_g': 3.475129e+01}


def _to_microbatches(a, axis):
    t = _jnp.moveaxis(a, axis, 0)
    t = t.reshape((N_MICROBATCH, t.shape[0] // N_MICROBATCH) + t.shape[1:])
    return _jnp.moveaxis(t, 1, axis + 1)


def setup_inputs(seed: int = 0) -> dict:
    inp = _fwd_setup_inputs(seed)
    key = _jax.random.fold_in(_jax.random.key(seed), 7919)
    shape, _ = _output_shape()
    out = dict(inp)
    out["loss_target"] = _jax.random.normal(_jax.random.fold_in(key, 0), shape, _jnp.float32)
    for i, name in enumerate(TWIN_WEIGHTS):
        w = inp[name].astype(_jnp.float32)
        if MOMENT_SCALE is None:
            s = _jnp.sqrt(_jnp.mean(_jnp.square(w)) + 1e-30)
        else:
            s = MOMENT_SCALE[name]
        km, kv = _jax.random.split(_jax.random.fold_in(key, i + 1))
        out[name] = w
        out["m_" + name] = s * _jax.random.normal(km, w.shape, _jnp.float32)
        out["v_" + name] = (s * s) * _jax.random.uniform(kv, w.shape, _jnp.float32, 0.5, 1.5)
    if N_MICROBATCH > 1:
        for name, axis in PER_EXAMPLE_BATCH_AXIS.items():
            out[name] = _to_microbatches(out[name], axis)
    return {'x': out['x'], 'c': out['c'], 'positions': out['positions'], 'ada_w': out['ada_w'], 'ada_b': out['ada_b'], 'norm_mix_g': out['norm_mix_g'], 'norm_mlp_g': out['norm_mlp_g'], 'fox_w_in': out['fox_w_in'], 'fox_b_f': out['fox_b_f'], 'fox_w_out': out['fox_w_out'], 'mla_w_dq': out['mla_w_dq'], 'mla_q_norm_g': out['mla_q_norm_g'], 'mla_w_uq': out['mla_w_uq'], 'mla_w_dkv': out['mla_w_dkv'], 'mla_kv_norm_g': out['mla_kv_norm_g'], 'mla_w_ukv': out['mla_w_ukv'], 'mla_w_out': out['mla_w_out'], 'mlp_w1': out['mlp_w1'], 'mlp_w2': out['mlp_w2'], 'final_norm_g': out['final_norm_g'], 'loss_target': out['loss_target'], 'm_ada_w': out['m_ada_w'], 'm_ada_b': out['m_ada_b'], 'm_norm_mix_g': out['m_norm_mix_g'], 'm_norm_mlp_g': out['m_norm_mlp_g'], 'm_fox_w_in': out['m_fox_w_in'], 'm_fox_b_f': out['m_fox_b_f'], 'm_fox_w_out': out['m_fox_w_out'], 'm_mla_w_dq': out['m_mla_w_dq'], 'm_mla_q_norm_g': out['m_mla_q_norm_g'], 'm_mla_w_uq': out['m_mla_w_uq'], 'm_mla_w_dkv': out['m_mla_w_dkv'], 'm_mla_kv_norm_g': out['m_mla_kv_norm_g'], 'm_mla_w_ukv': out['m_mla_w_ukv'], 'm_mla_w_out': out['m_mla_w_out'], 'm_mlp_w1': out['m_mlp_w1'], 'm_mlp_w2': out['m_mlp_w2'], 'm_final_norm_g': out['m_final_norm_g'], 'v_ada_w': out['v_ada_w'], 'v_ada_b': out['v_ada_b'], 'v_norm_mix_g': out['v_norm_mix_g'], 'v_norm_mlp_g': out['v_norm_mlp_g'], 'v_fox_w_in': out['v_fox_w_in'], 'v_fox_b_f': out['v_fox_b_f'], 'v_fox_w_out': out['v_fox_w_out'], 'v_mla_w_dq': out['v_mla_w_dq'], 'v_mla_q_norm_g': out['v_mla_q_norm_g'], 'v_mla_w_uq': out['v_mla_w_uq'], 'v_mla_w_dkv': out['v_mla_w_dkv'], 'v_mla_kv_norm_g': out['v_mla_kv_norm_g'], 'v_mla_w_ukv': out['v_mla_w_ukv'], 'v_mla_w_out': out['v_mla_w_out'], 'v_mlp_w1': out['v_mlp_w1'], 'v_mlp_w2': out['v_mlp_w2'], 'v_final_norm_g': out['v_final_norm_g']}


def _loss(weights, diff, rest, loss_target):
    with _jax.named_scope("forward"):
        args = {**rest, TWIN_DIFF_INPUT: diff, **{k: w.astype(_WEIGHT_DTYPES[k]) for k, w in weights.items()}}
        y = _forward(args)
    with _jax.named_scope("loss_head"):
        err = _jnp.square(y.astype(_jnp.float32) - loss_target)
        return 0.5 * _jnp.sum(_jnp.mean(err, axis=-1)) if err.ndim else 0.5 * err


def _adamw(w, g, m, v):
    m = ADAM_B1 * m + (1.0 - ADAM_B1) * g
    v = ADAM_B2 * v + (1.0 - ADAM_B2) * _jnp.square(g)
    m_hat = m / (1.0 - ADAM_B1 ** ADAM_STEP)
    v_hat = v / (1.0 - ADAM_B2 ** ADAM_STEP)
    delta = -ADAM_LR * (m_hat / (_jnp.sqrt(v_hat) + ADAM_EPS) + ADAM_WD * w)
    return delta, m, v


def reference(x, c, positions, ada_w, ada_b, norm_mix_g, norm_mlp_g, fox_w_in, fox_b_f, fox_w_out, mla_w_dq, mla_q_norm_g, mla_w_uq, mla_w_dkv, mla_kv_norm_g, mla_w_ukv, mla_w_out, mlp_w1, mlp_w2, final_norm_g, loss_target, m_ada_w, m_ada_b, m_norm_mix_g, m_norm_mlp_g, m_fox_w_in, m_fox_b_f, m_fox_w_out, m_mla_w_dq, m_mla_q_norm_g, m_mla_w_uq, m_mla_w_dkv, m_mla_kv_norm_g, m_mla_w_ukv, m_mla_w_out, m_mlp_w1, m_mlp_w2, m_final_norm_g, v_ada_w, v_ada_b, v_norm_mix_g, v_norm_mlp_g, v_fox_w_in, v_fox_b_f, v_fox_w_out, v_mla_w_dq, v_mla_q_norm_g, v_mla_w_uq, v_mla_w_dkv, v_mla_kv_norm_g, v_mla_w_ukv, v_mla_w_out, v_mlp_w1, v_mlp_w2, v_final_norm_g):
    given = dict(x=x, c=c, positions=positions, ada_w=ada_w, ada_b=ada_b, norm_mix_g=norm_mix_g, norm_mlp_g=norm_mlp_g, fox_w_in=fox_w_in, fox_b_f=fox_b_f, fox_w_out=fox_w_out, mla_w_dq=mla_w_dq, mla_q_norm_g=mla_q_norm_g, mla_w_uq=mla_w_uq, mla_w_dkv=mla_w_dkv, mla_kv_norm_g=mla_kv_norm_g, mla_w_ukv=mla_w_ukv, mla_w_out=mla_w_out, mlp_w1=mlp_w1, mlp_w2=mlp_w2, final_norm_g=final_norm_g, loss_target=loss_target, m_ada_w=m_ada_w, m_ada_b=m_ada_b, m_norm_mix_g=m_norm_mix_g, m_norm_mlp_g=m_norm_mlp_g, m_fox_w_in=m_fox_w_in, m_fox_b_f=m_fox_b_f, m_fox_w_out=m_fox_w_out, m_mla_w_dq=m_mla_w_dq, m_mla_q_norm_g=m_mla_q_norm_g, m_mla_w_uq=m_mla_w_uq, m_mla_w_dkv=m_mla_w_dkv, m_mla_kv_norm_g=m_mla_kv_norm_g, m_mla_w_ukv=m_mla_w_ukv, m_mla_w_out=m_mla_w_out, m_mlp_w1=m_mlp_w1, m_mlp_w2=m_mlp_w2, m_final_norm_g=m_final_norm_g, v_ada_w=v_ada_w, v_ada_b=v_ada_b, v_norm_mix_g=v_norm_mix_g, v_norm_mlp_g=v_norm_mlp_g, v_fox_w_in=v_fox_w_in, v_fox_b_f=v_fox_b_f, v_fox_w_out=v_fox_w_out, v_mla_w_dq=v_mla_w_dq, v_mla_q_norm_g=v_mla_q_norm_g, v_mla_w_uq=v_mla_w_uq, v_mla_w_dkv=v_mla_w_dkv, v_mla_kv_norm_g=v_mla_kv_norm_g, v_mla_w_ukv=v_mla_w_ukv, v_mla_w_out=v_mla_w_out, v_mlp_w1=v_mlp_w1, v_mlp_w2=v_mlp_w2, v_final_norm_g=v_final_norm_g)
    weights = {n: given[n] for n in TWIN_WEIGHTS}
    shared = {n: given[n] for n in SHARED_INPUTS}
    per_example = {n: given[n] for n in ['x', 'c', 'positions']}
    grad_fn = _jax.value_and_grad(_loss, argnums=(0, 1))

    def one_microbatch(ex, loss_target):
        ex = dict(ex)
        diff = ex.pop(TWIN_DIFF_INPUT)
        return grad_fn(weights, diff, {**shared, **ex}, loss_target)

    if N_MICROBATCH == 1:
        loss, (grad_w, grad_x) = one_microbatch(per_example, given["loss_target"])
    else:
        def body(carry, xs):
            loss_sum, grad_sum = carry
            l_k, (gw_k, gx_k) = one_microbatch(xs[0], xs[1])
            with _jax.named_scope("update"):
                return (loss_sum + l_k, _jax.tree.map(_jnp.add, grad_sum, gw_k)), gx_k

        init = (_jnp.zeros((), _jnp.float32), _jax.tree.map(_jnp.zeros_like, weights))
        (loss, grad_w), grad_x = _jax.lax.scan(body, init, (per_example, given["loss_target"]))
    with _jax.named_scope("update"):
        delta_w, new_m, new_v = {}, {}, {}
        for n in TWIN_WEIGHTS:
            delta_w[n], new_m[n], new_v[n] = _adamw(weights[n], grad_w[n], given["m_" + n], given["v_" + n])
    return (loss, grad_x, *[grad_w[n] for n in TWIN_WEIGHTS], *[delta_w[n] for n in TWIN_WEIGHTS],
            *[new_m[n] for n in TWIN_WEIGHTS], *[new_v[n] for n in TWIN_WEIGHTS])
```

```python
import functools
import math

import jax
import jax.numpy as jnp
import numpy as np
from jax import lax
from jax.experimental import pallas as pl
from jax.experimental.pallas import tpu as pltpu

F32 = jnp.float32
BF = jnp.bfloat16

N_DEV = 8
HEADS = 16
HEAD_PAIRS = HEADS // 2
HEAD_DIM = 64
LANES = 128
ROPE_HALF = 16
NORM_EPS = 1e-6
MLA_SCALE = 96.0 ** -0.5
FOX_SCALE = 0.125
ATTN_BLOCK = 256
ROW_BLOCK = 256
VMEM_LIMIT = 56 * 1024 * 1024
MESH = pl.DeviceIdType.MESH

ADAM_LR = 0.001
ADAM_B1 = 0.9
ADAM_B2 = 0.999
ADAM_EPS = 1e-08
ADAM_WD = 0.01
ADAM_STEP = 10

PACKED = (("fox_w_in", "col"), ("fox_w_out", "row"), ("mla_w_dq", "row"), ("mla_q_norm_g", "vec"),
          ("mla_w_uq", "col"), ("mla_w_dkv", "row"), ("mla_w_ukv", "col"), ("mla_w_out", "row"),
          ("mlp_w1", "col"), ("mlp_w2", "row"))
WEIGHTS = ("ada_w", "ada_b", "norm_mix_g", "norm_mlp_g", "fox_w_in", "fox_b_f", "fox_w_out", "mla_w_dq",
           "mla_q_norm_g", "mla_w_uq", "mla_w_dkv", "mla_kv_norm_g", "mla_w_ukv", "mla_w_out", "mlp_w1",
           "mlp_w2", "final_norm_g")
PACK_ALIGN = 128 * 1024


def _params(sem=None):
    return pltpu.CompilerParams(dimension_semantics=sem, vmem_limit_bytes=VMEM_LIMIT)


def _pick(n, target):
    if n <= target:
        return n
    for t in range(target, 127, -128):
        if n % t == 0:
            return t
    return n


def _rows(n, target=512):
    if n <= target:
        return n
    for t in range(target, 7, -8):
        if n % t == 0:
            return t
    return n


def _place():
    x, y, c = lax.axis_index("x"), lax.axis_index("y"), lax.axis_index("c")
    return x, y, c


def _all_gather(block, name):
    shape = block.shape

    def body(x_ref, out_ref, send_sems, recv_sems, local_sem):
        x, y, c = _place()
        me, sibling = (x, y, c), (x, y, 1 - c)
        chips = [(1 - x, y), (x, 1 - y), (1 - x, 1 - y)]

        def slot(px, py, pc):
            return out_ref.at[4 * px + 2 * py + pc]

        def copy(k, blk, to, src=None):
            return pltpu.make_async_remote_copy(
                src_ref=slot(*blk) if src is None else src, dst_ref=slot(*blk),
                send_sem=send_sems.at[k], recv_sem=recv_sems.at[k], device_id=to, device_id_type=MESH)

        mine = pltpu.make_async_copy(x_ref, slot(*me), local_sem)
        mine.start()
        first = [copy(0, me, sibling, src=x_ref)]
        first += [copy(1 + j, me, (*chip, c), src=x_ref) for j, chip in enumerate(chips)]
        for cp in first:
            cp.start()
        passed = [copy(4 + j, (*chip, c), sibling) for j, chip in enumerate(chips)]
        for j, chip in enumerate(chips):
            copy(1 + j, (*chip, c), me).wait_recv()
            passed[j].start()
        copy(0, sibling, me).wait_recv()
        for j, chip in enumerate(chips):
            copy(4 + j, (*chip, 1 - c), me).wait_recv()
        for cp in first + passed:
            cp.wait_send()
        mine.wait()

    return pl.pallas_call(
        body, name=name,
        out_shape=jax.ShapeDtypeStruct((N_DEV,) + shape, block.dtype),
        in_specs=[pl.BlockSpec(memory_space=pl.ANY)],
        out_specs=pl.BlockSpec(memory_space=pl.ANY),
        scratch_shapes=[pltpu.SemaphoreType.DMA((7,)), pltpu.SemaphoreType.DMA((7,)), pltpu.SemaphoreType.DMA],
    )(block)


def _swap_with_sibling(packed, name):
    _, r, cdim = packed.shape

    def body(src_ref, dst_ref, send_sem, recv_sem):
        x, y, c = _place()
        cp = pltpu.make_async_remote_copy(
            src_ref=src_ref.at[pl.ds(4, 4)], dst_ref=dst_ref,
            send_sem=send_sem, recv_sem=recv_sem, device_id=(x, y, 1 - c), device_id_type=MESH)
        cp.start()
        cp.wait()

    return pl.pallas_call(
        body, name=name,
        out_shape=jax.ShapeDtypeStruct((4, r, cdim), packed.dtype),
        in_specs=[pl.BlockSpec(memory_space=pl.ANY)],
        out_specs=pl.BlockSpec(memory_space=pl.ANY),
        scratch_shapes=[pltpu.SemaphoreType.DMA, pltpu.SemaphoreType.DMA],
    )(packed)


def _swap_with_chips(parts, name):
    def body(src_ref, dst_ref, send_sems, recv_sems):
        x, y, c = _place()
        chips = [(1 - x, y), (x, 1 - y), (1 - x, 1 - y)]
        cps = [pltpu.make_async_remote_copy(
            src_ref=src_ref.at[k], dst_ref=dst_ref.at[k], send_sem=send_sems.at[k], recv_sem=recv_sems.at[k],
            device_id=(*chip, c), device_id_type=MESH) for k, chip in enumerate(chips)]
        for cp in cps:
            cp.start()
        for cp in cps:
            cp.wait()

    return pl.pallas_call(
        body, name=name,
        out_shape=jax.ShapeDtypeStruct(parts.shape, parts.dtype),
        in_specs=[pl.BlockSpec(memory_space=pl.ANY)],
        out_specs=pl.BlockSpec(memory_space=pl.ANY),
        scratch_shapes=[pltpu.SemaphoreType.DMA((3,)), pltpu.SemaphoreType.DMA((3,))],
    )(parts)


def _chip_partial(packed, from_sibling, name):
    _, r, cdim = packed.shape
    tr = _rows(r, 256)

    def body(mine_ref, sib_ref, own_ref, parts_ref):
        own_ref[...] = mine_ref[0].astype(F32) + sib_ref[0].astype(F32)
        for k in range(1, 4):
            parts_ref[k - 1] = (mine_ref[k].astype(F32) + sib_ref[k].astype(F32)).astype(parts_ref.dtype)

    four = pl.BlockSpec((4, tr, cdim), lambda i: (0, i, 0))
    return pl.pallas_call(
        body, name=name, grid=(r // tr,),
        out_shape=(jax.ShapeDtypeStruct((r, cdim), F32), jax.ShapeDtypeStruct((3, r, cdim), packed.dtype)),
        in_specs=[four, four],
        out_specs=(pl.BlockSpec((tr, cdim), lambda i: (i, 0)), pl.BlockSpec((3, tr, cdim), lambda i: (0, i, 0))),
        compiler_params=_params(("parallel",)),
    )(packed, from_sibling)


def _add_parts(own, parts, name):
    r, cdim = own.shape
    tr = _rows(r, 512)

    def body(own_ref, parts_ref, out_ref):
        acc = own_ref[...]
        for k in range(parts_ref.shape[0]):
            acc = acc + parts_ref[k].astype(F32)
        out_ref[...] = acc

    return pl.pallas_call(
        body, name=name, grid=(r // tr,),
        out_shape=jax.ShapeDtypeStruct((r, cdim), F32),
        in_specs=[pl.BlockSpec((tr, cdim), lambda i: (i, 0)),
                  pl.BlockSpec((parts.shape[0], tr, cdim), lambda i: (0, i, 0))],
        out_specs=pl.BlockSpec((tr, cdim), lambda i: (i, 0)),
        compiler_params=_params(("parallel",)),
    )(own, parts)


def _sum_leading(stack, name):
    n, r, cdim = stack.shape
    tr = _rows(r, 512)

    def body(in_ref, out_ref):
        acc = in_ref[0]
        for k in range(1, n):
            acc = acc + in_ref[k]
        out_ref[...] = acc

    return pl.pallas_call(
        body, name=name, grid=(r // tr,),
        out_shape=jax.ShapeDtypeStruct((r, cdim), F32),
        in_specs=[pl.BlockSpec((n, tr, cdim), lambda i: (0, i, 0))],
        out_specs=pl.BlockSpec((tr, cdim), lambda i: (i, 0)),
        compiler_params=_params(("parallel",)),
    )(stack)


_DIMS = {"nn": (((1,), (0,)), ((), ())), "nt": (((1,), (1,)), ((), ())), "tn": (((0,), (0,)), ((), ()))}


def _matmul(a, b, *, mode, name, out_dtype=BF, a_act=None, epi=None, extras=(), seq=None):
    if mode == "nn":
        (m, kdim), (_, n) = a.shape, b.shape
    elif mode == "nt":
        (m, kdim), (n, _) = a.shape, b.shape
    else:
        (kdim, m), (_, n) = a.shape, b.shape
    tm, tn, tk = _pick(m, 512), _pick(n, 1024), _pick(kdim, 1024)
    nk = kdim // tk
    a_spec = (pl.BlockSpec((tk, tm), lambda i, j, k: (k, i)) if mode == "tn"
              else pl.BlockSpec((tm, tk), lambda i, j, k: (i, k)))
    b_spec = (pl.BlockSpec((tn, tk), lambda i, j, k: (j, k)) if mode == "nt"
              else pl.BlockSpec((tk, tn), lambda i, j, k: (k, j)))
    tile = pl.BlockSpec((tm, tn), lambda i, j, k: (i, j))
    in_specs, out_specs = [a_spec, b_spec], [tile]
    out_shape = [jax.ShapeDtypeStruct((m, n), out_dtype)]
    if epi == "resid_gate":
        in_specs += [tile, pl.BlockSpec((None, 1, tn), lambda i, j, k: ((i * tm) // seq, 0, j))]
        out_shape = [jax.ShapeDtypeStruct((m, n), F32), jax.ShapeDtypeStruct((m, n), BF)]
        out_specs = [tile, tile]
    elif epi in ("mul_drelu", "add"):
        in_specs += [tile]
    elif epi == "bias":
        in_specs += [pl.BlockSpec((1, tn), lambda i, j, k: (0, j))]
    n_extra, n_out = len(in_specs) - 2, len(out_specs)
    dims = _DIMS[mode]

    def body(*refs):
        a_ref, b_ref = refs[:2]
        ex = refs[2:2 + n_extra]
        outs = refs[2 + n_extra:2 + n_extra + n_out]
        av = a_ref[...]
        if a_act == "relu2":
            t = jnp.maximum(av.astype(F32), 0.0)
            av = t * t
        elif a_act == "silu":
            t = av.astype(F32)
            av = t / (1.0 + jnp.exp(-t))
        part = lax.dot_general(av.astype(BF), b_ref[...].astype(BF), dims, preferred_element_type=F32)

        def finish(acc):
            if epi == "resid_gate":
                outs[0][...] = ex[0][...] + ex[1][...] * acc
                outs[1][...] = acc.astype(BF)
            elif epi == "mul_drelu":
                outs[0][...] = (acc * (2.0 * jnp.maximum(ex[0][...].astype(F32), 0.0))).astype(out_dtype)
            elif epi == "add":
                outs[0][...] = (acc + ex[0][...].astype(F32)).astype(out_dtype)
            elif epi == "bias":
                outs[0][...] = (acc + ex[0][...]).astype(out_dtype)
            else:
                outs[0][...] = acc.astype(out_dtype)

        if nk == 1:
            finish(part)
        else:
            acc_ref = refs[-1]
            k = pl.program_id(2)

            @pl.when(k == 0)
            def _():
                acc_ref[...] = part

            @pl.when(k > 0)
            def _():
                acc_ref[...] += part

            @pl.when(k == nk - 1)
            def _():
                finish(acc_ref[...])

    res = pl.pallas_call(
        body, name=name, grid=(m // tm, n // tn, nk),
        out_shape=out_shape, in_specs=in_specs, out_specs=out_specs,
        scratch_shapes=[pltpu.VMEM((tm, tn), F32)] if nk > 1 else [],
        compiler_params=_params(("parallel", "parallel", "arbitrary")),
    )(a, b, *extras)
    return res if n_out > 1 else res[0]


def _norm_mod(x, gain, scale, shift, seq, name):
    t, w = x.shape
    tr = ROW_BLOCK

    def body(x_ref, g_ref, sc_ref, sh_ref, out_ref):
        xv = x_ref[...]
        rstd = lax.rsqrt(jnp.mean(xv * xv, axis=-1, keepdims=True) + NORM_EPS)
        y = xv * rstd * g_ref[...]
        out_ref[...] = (y * (1.0 + sc_ref[...]) + sh_ref[...]).astype(BF)

    per_b = pl.BlockSpec((None, 1, w), lambda i: ((i * tr) // seq, 0, 0))
    return pl.pallas_call(
        body, name=name, grid=(t // tr,),
        out_shape=jax.ShapeDtypeStruct((t, w), BF),
        in_specs=[pl.BlockSpec((tr, w), lambda i: (i, 0)), pl.BlockSpec((1, w), lambda i: (0, 0)), per_b, per_b],
        out_specs=pl.BlockSpec((tr, w), lambda i: (i, 0)),
        compiler_params=_params(("parallel",)),
    )(x, gain, scale, shift)


def _norm_mod_bwd(dh, x, gain, scale, dres, seq, name):
    t, w = x.shape
    tr = ROW_BLOCK
    steps_per_seq = seq // tr
    nb = t // seq

    def body(dh_ref, x_ref, g_ref, sc_ref, dres_ref, dx_ref, dg_ref, dsc_ref, dsh_ref):
        i = pl.program_id(0)
        xv = x_ref[...]
        dhv = dh_ref[...].astype(F32)
        rstd = lax.rsqrt(jnp.mean(xv * xv, axis=-1, keepdims=True) + NORM_EPS)
        xhat = xv * rstd
        one_sc = 1.0 + sc_ref[...]
        g = g_ref[...]
        dxhat = dhv * (g * one_sc)
        proj = jnp.mean(dxhat * xhat, axis=-1, keepdims=True)
        dx_ref[...] = dres_ref[...] + rstd * (dxhat - xhat * proj)
        dhx = dhv * xhat
        dg_part = jnp.sum(dhx * one_sc, axis=0, keepdims=True)
        dsc_part = jnp.sum(dhx * g, axis=0, keepdims=True)
        dsh_part = jnp.sum(dhv, axis=0, keepdims=True)

        @pl.when(i == 0)
        def _():
            dg_ref[...] = dg_part

        @pl.when(i > 0)
        def _():
            dg_ref[...] += dg_part

        @pl.when(i % steps_per_seq == 0)
        def _():
            dsc_ref[...] = dsc_part
            dsh_ref[...] = dsh_part

        @pl.when(i % steps_per_seq != 0)
        def _():
            dsc_ref[...] += dsc_part
            dsh_ref[...] += dsh_part

    row = pl.BlockSpec((tr, w), lambda i: (i, 0))
    per_b = pl.BlockSpec((None, 1, w), lambda i: ((i * tr) // seq, 0, 0))
    vec = pl.BlockSpec((1, w), lambda i: (0, 0))
    return pl.pallas_call(
        body, name=name, grid=(t // tr,),
        out_shape=(jax.ShapeDtypeStruct((t, w), F32), jax.ShapeDtypeStruct((1, w), F32),
                   jax.ShapeDtypeStruct((nb, 1, w), F32), jax.ShapeDtypeStruct((nb, 1, w), F32)),
        in_specs=[row, row, vec, per_b, row],
        out_specs=(row, vec, per_b, per_b),
        compiler_params=_params(("arbitrary",)),
    )(dh, x, gain, scale, dres)


def _gate_bwd(dx, y, gate, seq, name):
    t, w = dx.shape
    tr = ROW_BLOCK
    steps_per_seq = seq // tr
    nb = t // seq

    def body(dx_ref, y_ref, g_ref, dy_ref, dg_ref):
        i = pl.program_id(0)
        dxv = dx_ref[...]
        dy_ref[...] = (dxv * g_ref[...]).astype(BF)
        part = jnp.sum(dxv * y_ref[...].astype(F32), axis=0, keepdims=True)

        @pl.when(i % steps_per_seq == 0)
        def _():
            dg_ref[...] = part

        @pl.when(i % steps_per_seq != 0)
        def _():
            dg_ref[...] += part

    row = pl.BlockSpec((tr, w), lambda i: (i, 0))
    per_b = pl.BlockSpec((None, 1, w), lambda i: ((i * tr) // seq, 0, 0))
    return pl.pallas_call(
        body, name=name, grid=(t // tr,),
        out_shape=(jax.ShapeDtypeStruct((t, w), BF), jax.ShapeDtypeStruct((nb, 1, w), F32)),
        in_specs=[row, row, per_b], out_specs=(row, per_b),
        compiler_params=_params(("arbitrary",)),
    )(dx, y, gate)


def _loss_head(x, gain, target, name):
    t, w = x.shape
    tr = ROW_BLOCK

    def body(x_ref, g_ref, t_ref, loss_ref, dx_ref, dg_ref):
        i = pl.program_id(0)
        xv = x_ref[...]
        g = g_ref[...]
        rstd = lax.rsqrt(jnp.mean(xv * xv, axis=-1, keepdims=True) + NORM_EPS)
        xhat = xv * rstd
        err = xhat * g - t_ref[...]
        row_loss = jnp.sum(err * err, axis=-1, keepdims=True) * (0.5 / w)
        loss_part = jnp.broadcast_to(jnp.sum(row_loss, axis=0, keepdims=True), (1, LANES))
        dy = err * (1.0 / w)
        dg_part = jnp.sum(dy * xhat, axis=0, keepdims=True)
        dxhat = dy * g
        proj = jnp.mean(dxhat * xhat, axis=-1, keepdims=True)
        dx_ref[...] = rstd * (dxhat - xhat * proj)

        @pl.when(i == 0)
        def _():
            loss_ref[...] = loss_part
            dg_ref[...] = dg_part

        @pl.when(i > 0)
        def _():
            loss_ref[...] += loss_part
            dg_ref[...] += dg_part

    row = pl.BlockSpec((tr, w), lambda i: (i, 0))
    vec = pl.BlockSpec((1, w), lambda i: (0, 0))
    return pl.pallas_call(
        body, name=name, grid=(t // tr,),
        out_shape=(jax.ShapeDtypeStruct((1, LANES), F32), jax.ShapeDtypeStruct((t, w), F32),
                   jax.ShapeDtypeStruct((1, w), F32)),
        in_specs=[row, vec, row],
        out_specs=(pl.BlockSpec((1, LANES), lambda i: (0, 0)), row, vec),
        compiler_params=_params(("arbitrary",)),
    )(x, gain, target)


def _rope_group(xg, cos_p, sin_a, sin_b):
    return (xg * cos_p + pltpu.roll(xg, LANES - ROPE_HALF, axis=1) * sin_a
            + pltpu.roll(xg, ROPE_HALF, axis=1) * sin_b)


def _rope(x, tables, name, out_dtype=BF):
    t, w = x.shape
    tr = ROW_BLOCK
    groups = w // LANES

    def body(x_ref, c_ref, a_ref, b_ref, out_ref):
        cos_p, sin_a, sin_b = c_ref[...], a_ref[...], b_ref[...]
        for g in range(groups):
            sl = slice(g * LANES, (g + 1) * LANES)
            out_ref[:, sl] = _rope_group(x_ref[:, sl].astype(F32), cos_p, sin_a, sin_b).astype(out_dtype)

    row = pl.BlockSpec((tr, w), lambda i: (i, 0))
    tab = pl.BlockSpec((tr, LANES), lambda i: (i, 0))
    return pl.pallas_call(
        body, name=name, grid=(t // tr,),
        out_shape=jax.ShapeDtypeStruct((t, w), out_dtype),
        in_specs=[row, tab, tab, tab], out_specs=row,
        compiler_params=_params(("parallel",)),
    )(x, *tables)


def _mla_mid(down, gq, gkv, tables, name):
    t = down.shape[0]
    tr = ROW_BLOCK

    def body(d_ref, gq_ref, gkv_ref, c_ref, a_ref, b_ref, cq_ref, ckr_ref):
        q = d_ref[:, 0:256]
        cq_ref[...] = (q * lax.rsqrt(jnp.mean(q * q, axis=-1, keepdims=True) + NORM_EPS) * gq_ref[...]).astype(BF)
        kv = d_ref[:, 256:384]
        ckr_ref[:, 0:128] = (kv * lax.rsqrt(jnp.mean(kv * kv, axis=-1, keepdims=True) + NORM_EPS)
                             * gkv_ref[...]).astype(BF)
        ckr_ref[:, 128:256] = _rope_group(d_ref[:, 384:512], c_ref[...], a_ref[...], b_ref[...]).astype(BF)

    tab = pl.BlockSpec((tr, LANES), lambda i: (i, 0))
    return pl.pallas_call(
        body, name=name, grid=(t // tr,),
        out_shape=(jax.ShapeDtypeStruct((t, 256), BF), jax.ShapeDtypeStruct((t, 256), BF)),
        in_specs=[pl.BlockSpec((tr, 512), lambda i: (i, 0)), pl.BlockSpec((1, 256), lambda i: (0, 0)),
                  pl.BlockSpec((1, 128), lambda i: (0, 0)), tab, tab, tab],
        out_specs=(pl.BlockSpec((tr, 256), lambda i: (i, 0)), pl.BlockSpec((tr, 256), lambda i: (i, 0))),
        compiler_params=_params(("parallel",)),
    )(down, gq, gkv, *tables)


def _mla_mid_bwd(down, dcq, dckr, gq, gkv, tables_t, name):
    t = down.shape[0]
    tr = ROW_BLOCK

    def norm_bwd(xv, g, dy):
        rstd = lax.rsqrt(jnp.mean(xv * xv, axis=-1, keepdims=True) + NORM_EPS)
        xhat = xv * rstd
        dxhat = dy * g
        proj = jnp.mean(dxhat * xhat, axis=-1, keepdims=True)
        return rstd * (dxhat - xhat * proj), jnp.sum(dy * xhat, axis=0, keepdims=True)

    def body(d_ref, dcq_ref, dckr_ref, gq_ref, gkv_ref, c_ref, a_ref, b_ref, dd_ref, dgq_ref, dgkv_ref):
        i = pl.program_id(0)
        dq, dgq_part = norm_bwd(d_ref[:, 0:256], gq_ref[...], dcq_ref[...].astype(F32))
        dd_ref[:, 0:256] = dq.astype(BF)
        dkv, dgkv_part = norm_bwd(d_ref[:, 256:384], gkv_ref[...], dckr_ref[:, 0:128].astype(F32))
        dd_ref[:, 256:384] = dkv.astype(BF)
        dd_ref[:, 384:512] = _rope_group(dckr_ref[:, 128:256].astype(F32), c_ref[...], a_ref[...],
                                         b_ref[...]).astype(BF)

        @pl.when(i == 0)
        def _():
            dgq_ref[...] = dgq_part
            dgkv_ref[...] = dgkv_part

        @pl.when(i > 0)
        def _():
            dgq_ref[...] += dgq_part
            dgkv_ref[...] += dgkv_part

    tab = pl.BlockSpec((tr, LANES), lambda i: (i, 0))
    r256 = pl.BlockSpec((tr, 256), lambda i: (i, 0))
    return pl.pallas_call(
        body, name=name, grid=(t // tr,),
        out_shape=(jax.ShapeDtypeStruct((t, 512), BF), jax.ShapeDtypeStruct((1, 256), F32),
                   jax.ShapeDtypeStruct((1, 128), F32)),
        in_specs=[pl.BlockSpec((tr, 512), lambda i: (i, 0)), r256, r256, pl.BlockSpec((1, 256), lambda i: (0, 0)),
                  pl.BlockSpec((1, 128), lambda i: (0, 0)), tab, tab, tab],
        out_specs=(pl.BlockSpec((tr, 512), lambda i: (i, 0)), pl.BlockSpec((1, 256), lambda i: (0, 0)),
                   pl.BlockSpec((1, 128), lambda i: (0, 0))),
        compiler_params=_params(("arbitrary",)),
    )(down, dcq, dckr, gq, gkv, *tables_t)


def _scan_rows(x, reverse):
    s = x.shape[0]
    row = lax.broadcasted_iota(jnp.int32, x.shape, 0)
    step = 1
    while step < s:
        if reverse:
            x = x + jnp.where(row < s - step, pltpu.roll(x, s - step, axis=0), 0.0)
        else:
            x = x + jnp.where(row >= step, pltpu.roll(x, step, axis=0), 0.0)
        step *= 2
    return x


def _fox_gate(fg, b_f, seq, name):
    t = fg.shape[0]

    def body(fg_ref, b_ref, out_ref):
        z = fg_ref[...] + b_ref[...]
        log_f = jnp.minimum(z, 0.0) - jnp.log(1.0 + jnp.exp(-jnp.abs(z)))
        out_ref[...] = _scan_rows(log_f, reverse=False)

    blk = pl.BlockSpec((seq, LANES), lambda b: (b, 0))
    return pl.pallas_call(
        body, name=name, grid=(t // seq,),
        out_shape=jax.ShapeDtypeStruct((t, LANES), F32),
        in_specs=[blk, pl.BlockSpec((1, LANES), lambda b: (0, 0))], out_specs=blk,
        compiler_params=_params(("parallel",)),
    )(fg, b_f)


def _fox_gate_bwd(d_cum, fg, b_f, seq, name):
    t = fg.shape[0]

    def body(dc_ref, fg_ref, b_ref, dfg_ref, db_ref):
        b = pl.program_id(0)
        z = fg_ref[...] + b_ref[...]
        d_log_f = _scan_rows(dc_ref[...], reverse=True)
        dz = d_log_f / (1.0 + jnp.exp(z))
        dfg_ref[...] = dz
        part = jnp.sum(dz, axis=0, keepdims=True)

        @pl.when(b == 0)
        def _():
            db_ref[...] = part

        @pl.when(b > 0)
        def _():
            db_ref[...] += part

    blk = pl.BlockSpec((seq, LANES), lambda b: (b, 0))
    vec = pl.BlockSpec((1, LANES), lambda b: (0, 0))
    return pl.pallas_call(
        body, name=name, grid=(t // seq,),
        out_shape=(jax.ShapeDtypeStruct((t, LANES), F32), jax.ShapeDtypeStruct((1, LANES), F32)),
        in_specs=[blk, blk, vec], out_specs=(blk, vec),
        compiler_params=_params(("arbitrary",)),
    )(d_cum, fg, b_f)


def _head_masks():
    lane = lax.broadcasted_iota(jnp.int32, (1, LANES), 1)
    return lane < HEAD_DIM, lane >= HEAD_DIM


def _attn_fwd(q_arr, q_off, k_arr, k_off, v_arr, v_off, bias, seq, name):
    t = q_arr.shape[0]
    nb = t // seq
    blk = min(ATTN_BLOCK, seq)
    nq = seq // blk
    has_bias = bias is not None

    def body(*refs):
        if has_bias:
            q_ref, k_ref, v_ref, bias_ref, o_ref, lse_ref, o32_ref = refs
        else:
            q_ref, k_ref, v_ref, o_ref, lse_ref = refs
        lo, hi = _head_masks()
        rows = lax.broadcasted_iota(jnp.int32, (blk, blk), 0)
        cols = lax.broadcasted_iota(jnp.int32, (blk, blk), 1)
        causal = cols <= rows

        def q_block(iq, _):
            q0 = pl.multiple_of(iq * blk, blk)
            qs = [q_ref[pl.ds(q0, blk), h * LANES:(h + 1) * LANES] for h in range(2)]

            def kv_block(j, carry, diag):
                k0 = pl.multiple_of(j * blk, blk)
                vv = v_ref[pl.ds(k0, blk), :]
                vs = [jnp.where(lo, vv, jnp.zeros_like(vv)), jnp.where(hi, vv, jnp.zeros_like(vv))]
                acc = carry[0]
                new = []
                pv = None
                alphas = []
                for h in range(2):
                    m, l = carry[1 + 2 * h], carry[2 + 2 * h]
                    kk = k_ref[pl.ds(k0, blk), h * LANES:(h + 1) * LANES]
                    s = lax.dot_general(qs[h], kk, _DIMS["nt"], preferred_element_type=F32)
                    if has_bias:
                        s = s + bias_ref[h, 0:1, pl.ds(k0, blk)]
                    if diag:
                        s = jnp.where(causal, s, -jnp.inf)
                    m_new = jnp.maximum(m, jnp.max(s, axis=-1, keepdims=True))
                    p = jnp.exp(s - m_new)
                    alpha = jnp.exp(m - m_new)
                    l_new = alpha * l + jnp.sum(p, axis=-1, keepdims=True)
                    p_hi = p.astype(BF)
                    d = jnp.dot(p_hi, vs[h], preferred_element_type=F32)
                    if has_bias:
                        p_lo = (p - p_hi.astype(F32)).astype(BF)
                        d = d + jnp.dot(p_lo, vs[h], preferred_element_type=F32)
                    pv = d if pv is None else pv + d
                    alphas.append(alpha)
                    new += [m_new, l_new]
                acc = acc * jnp.where(lo, alphas[0], alphas[1]) + pv
                return (acc, *new)

            init = (jnp.zeros((blk, LANES), F32),
                    jnp.full((blk, 1), -jnp.inf, F32), jnp.zeros((blk, 1), F32),
                    jnp.full((blk, 1), -jnp.inf, F32), jnp.zeros((blk, 1), F32))
            carry = lax.fori_loop(0, iq, functools.partial(kv_block, diag=False), init)
            acc, m0, l0, m1, l1 = kv_block(iq, carry, diag=True)
            o_val = acc / jnp.where(lo, l0, l1)
            o_ref[pl.ds(q0, blk), :] = o_val.astype(BF)
            if has_bias:
                o32_ref[pl.ds(q0, blk), :] = o_val
            lse_ref[pl.ds(q0, blk), :] = jnp.where(lo, m0 + jnp.log(l0), m1 + jnp.log(l1))
            return 0

        lax.fori_loop(0, nq, q_block, 0)

    in_specs = [pl.BlockSpec((seq, 2 * LANES), lambda b, p: (b, q_off + p)),
                pl.BlockSpec((seq, 2 * LANES), lambda b, p: (b, k_off + p)),
                pl.BlockSpec((seq, LANES), lambda b, p: (b, v_off + p))]
    args = [q_arr, k_arr, v_arr]
    if has_bias:
        in_specs.append(pl.BlockSpec((None, 2, 8, seq), lambda b, p: (b, p, 0, 0)))
        args.append(bias)
    out_blk = pl.BlockSpec((seq, LANES), lambda b, p: (b, p))
    out_shape = [jax.ShapeDtypeStruct((t, HEAD_PAIRS * LANES), BF), jax.ShapeDtypeStruct((t, HEAD_PAIRS * LANES), F32)]
    if has_bias:
        out_shape.append(jax.ShapeDtypeStruct((t, HEAD_PAIRS * LANES), F32))
    return pl.pallas_call(
        body, name=name, grid=(nb, HEAD_PAIRS),
        out_shape=out_shape, in_specs=in_specs, out_specs=[out_blk] * len(out_shape),
        compiler_params=_params(("parallel", "parallel")),
    )(*args)


def _attn_bwd(q_arr, q_off, k_arr, k_off, v_arr, v_off, bias, o, do, lse, seq, name):
    t = q_arr.shape[0]
    nb = t // seq
    blk = min(ATTN_BLOCK, seq)
    nq = seq // blk
    has_bias = bias is not None

    def body(*refs):
        if has_bias:
            (q_ref, k_ref, v_ref, bias_ref, o_ref, do_ref, lse_ref,
             dq_ref, dk_ref, dv_ref, dbias_ref, dq_acc, dsum) = refs
        else:
            (q_ref, k_ref, v_ref, o_ref, do_ref, lse_ref, dq_ref, dk_ref, dv_ref, dq_acc, dsum) = refs
        lo, hi = _head_masks()
        rows = lax.broadcasted_iota(jnp.int32, (blk, blk), 0)
        cols = lax.broadcasted_iota(jnp.int32, (blk, blk), 1)
        causal = cols <= rows
        dq_acc[...] = jnp.zeros_like(dq_acc)

        def prep(iq, _):
            q0 = pl.multiple_of(iq * blk, blk)
            prod = do_ref[pl.ds(q0, blk), :].astype(F32) * o_ref[pl.ds(q0, blk), :].astype(F32)
            d0 = jnp.sum(jnp.where(lo, prod, 0.0), axis=-1, keepdims=True)
            d1 = jnp.sum(jnp.where(hi, prod, 0.0), axis=-1, keepdims=True)
            dsum[pl.ds(q0, blk), :] = jnp.where(lo, d0, d1)
            return 0

        lax.fori_loop(0, nq, prep, 0)

        def kv_block(j, _):
            k0 = pl.multiple_of(j * blk, blk)
            vv = v_ref[pl.ds(k0, blk), :]
            vs = [jnp.where(lo, vv, jnp.zeros_like(vv)), jnp.where(hi, vv, jnp.zeros_like(vv))]
            ks = [k_ref[pl.ds(k0, blk), h * LANES:(h + 1) * LANES] for h in range(2)]

            def q_block(iq, carry, diag):
                q0 = pl.multiple_of(iq * blk, blk)
                dov = do_ref[pl.ds(q0, blk), :]
                lse_v = lse_ref[pl.ds(q0, blk), :]
                dsum_v = dsum[pl.ds(q0, blk), :]
                dv_acc = carry[0]
                out = []
                for h in range(2):
                    dk_acc, db_acc = carry[1 + 2 * h], carry[2 + 2 * h]
                    qq = q_ref[pl.ds(q0, blk), h * LANES:(h + 1) * LANES]
                    s = lax.dot_general(qq, ks[h], _DIMS["nt"], preferred_element_type=F32)
                    if has_bias:
                        s = s + bias_ref[h, 0:1, pl.ds(k0, blk)]
                    p = jnp.exp(s - lse_v[:, h * HEAD_DIM:h * HEAD_DIM + 1])
                    if diag:
                        p = jnp.where(causal, p, 0.0)
                    dp = lax.dot_general(dov, vs[h], _DIMS["nt"], preferred_element_type=F32)
                    ds = p * (dp - dsum_v[:, h * HEAD_DIM:h * HEAD_DIM + 1])
                    ds_bf = ds.astype(BF)
                    pt_do = lax.dot_general(p.astype(BF), dov, _DIMS["tn"], preferred_element_type=F32)
                    dv_acc = dv_acc + jnp.where(hi if h else lo, pt_do, 0.0)
                    dk_acc = dk_acc + lax.dot_general(ds_bf, qq, _DIMS["tn"], preferred_element_type=F32)
                    dq_acc[pl.ds(q0, blk), h * LANES:(h + 1) * LANES] += jnp.dot(
                        ds_bf, ks[h], preferred_element_type=F32)
                    if has_bias:
                        db_acc = db_acc + jnp.sum(ds, axis=0, keepdims=True)
                    out += [dk_acc, db_acc]
                return (dv_acc, *out)

            init = (jnp.zeros((blk, LANES), F32),
                    jnp.zeros((blk, LANES), F32), jnp.zeros((1, blk), F32),
                    jnp.zeros((blk, LANES), F32), jnp.zeros((1, blk), F32))
            carry = q_block(j, init, diag=True)
            carry = lax.fori_loop(j + 1, nq, functools.partial(q_block, diag=False), carry)
            dv_ref[pl.ds(k0, blk), :] = carry[0].astype(BF)
            for h in range(2):
                dk_ref[pl.ds(k0, blk), h * LANES:(h + 1) * LANES] = carry[1 + 2 * h].astype(BF)
                if has_bias:
                    dbias_ref[h, :, pl.ds(k0, blk)] = jnp.broadcast_to(carry[2 + 2 * h], (8, blk))
            return 0

        lax.fori_loop(0, nq, kv_block, 0)
        dq_ref[...] = dq_acc[...].astype(BF)

    pair256 = lambda off: pl.BlockSpec((seq, 2 * LANES), lambda b, p: (b, off + p))
    pair128 = lambda off: pl.BlockSpec((seq, LANES), lambda b, p: (b, off + p))
    bias_spec = pl.BlockSpec((None, 2, 8, seq), lambda b, p: (b, p, 0, 0))
    in_specs = [pair256(q_off), pair256(k_off), pair128(v_off)]
    args = [q_arr, k_arr, v_arr]
    if has_bias:
        in_specs.append(bias_spec)
        args.append(bias)
    in_specs += [pair128(0), pair128(0), pair128(0)]
    args += [o, do, lse]
    out_shape = [jax.ShapeDtypeStruct((t, HEAD_PAIRS * 2 * LANES), BF),
                 jax.ShapeDtypeStruct((t, HEAD_PAIRS * 2 * LANES), BF),
                 jax.ShapeDtypeStruct((t, HEAD_PAIRS * LANES), BF)]
    out_specs = [pair256(0), pair256(0), pair128(0)]
    if has_bias:
        out_shape.append(jax.ShapeDtypeStruct((nb, HEADS, 8, seq), F32))
        out_specs.append(bias_spec)
    return pl.pallas_call(
        body, name=name, grid=(nb, HEAD_PAIRS),
        out_shape=out_shape, in_specs=in_specs, out_specs=out_specs,
        scratch_shapes=[pltpu.VMEM((seq, 2 * LANES), F32), pltpu.VMEM((seq, LANES), F32)],
        compiler_params=_params(("parallel", "parallel")),
    )(*args)


def _adamw(w, g, m, v, name):
    shape = w.shape
    last = shape[-1]
    rows = int(np.prod(shape[:-1])) if len(shape) > 1 else 1
    tr = _rows(rows, 512)
    c1 = 1.0 / (1.0 - ADAM_B1 ** ADAM_STEP)
    c2 = 1.0 / (1.0 - ADAM_B2 ** ADAM_STEP)

    def body(w_ref, g_ref, m_ref, v_ref, d_ref, nm_ref, nv_ref):
        gv = g_ref[...]
        nm = ADAM_B1 * m_ref[...] + (1.0 - ADAM_B1) * gv
        nv = ADAM_B2 * v_ref[...] + (1.0 - ADAM_B2) * (gv * gv)
        nm_ref[...] = nm
        nv_ref[...] = nv
        d_ref[...] = -ADAM_LR * ((nm * c1) / (jnp.sqrt(nv * c2) + ADAM_EPS) + ADAM_WD * w_ref[...])

    blk = pl.BlockSpec((tr, last), lambda i: (i, 0))
    sds = jax.ShapeDtypeStruct((rows, last), F32)
    outs = pl.pallas_call(
        body, name=name, grid=(rows // tr,),
        out_shape=(sds, sds, sds), in_specs=[blk] * 4, out_specs=(blk,) * 3,
        compiler_params=_params(("parallel",)),
    )(*[a.reshape(rows, last) for a in (w, g, m, v)])
    return tuple(a.reshape(shape) for a in outs)


def _pack_local(shards):
    flat = jnp.concatenate([shards[n].astype(BF).reshape(-1) for n, _ in PACKED])
    pad = (-flat.shape[0]) % PACK_ALIGN
    return jnp.pad(flat, (0, pad)).reshape(-1, 1024)


def _unpack_gathered(gathered, shards):
    flat = gathered.reshape(N_DEV, -1)
    full, off = {}, 0
    for name, kind in PACKED:
        shp = shards[name].shape
        n = int(np.prod(shp))
        seg = flat[:, off:off + n].reshape((N_DEV,) + shp)
        off += n
        if kind == "row":
            full[name] = jnp.transpose(seg, (1, 0, 2, 3)).reshape(shp[0], N_DEV * shp[1], shp[2])
        elif kind == "col":
            full[name] = jnp.transpose(seg, (1, 2, 0, 3)).reshape(shp[0], shp[1], N_DEV * shp[2])
        else:
            full[name] = jnp.transpose(seg, (1, 0, 2)).reshape(shp[0], N_DEV * shp[1])
    return full


def _pack_grads(grads, shards):
    parts = []
    for name, kind in PACKED:
        shp = shards[name].shape
        g = grads[name]
        if kind == "row":
            seg = jnp.transpose(g.reshape(shp[0], N_DEV, shp[1], shp[2]), (1, 0, 2, 3))
        elif kind == "col":
            seg = jnp.transpose(g.reshape(shp[0], shp[1], N_DEV, shp[2]), (2, 0, 1, 3))
        else:
            seg = jnp.transpose(g.reshape(shp[0], N_DEV, shp[1]), (1, 0, 2))
        parts.append(seg.reshape(N_DEV, -1).astype(BF))
    flat = jnp.concatenate(parts, axis=1)
    pad = (-flat.shape[1]) % PACK_ALIGN
    flat = jnp.pad(flat, ((0, 0), (0, pad)))
    x, y, c = _place()
    rel = np.arange(N_DEV)
    dest = 4 * (x ^ (rel & 1)) + 2 * (y ^ ((rel >> 1) & 1)) + (c ^ (rel >> 2))
    return jnp.take(flat, dest, axis=0).reshape(N_DEV, -1, 1024)


def _unpack_shard_grads(flat2d, shards):
    flat = flat2d.reshape(-1)
    out, off = {}, 0
    for name, _ in PACKED:
        shp = shards[name].shape
        n = int(np.prod(shp))
        out[name] = flat[off:off + n].reshape(shp)
        off += n
    return out


def _pad_heads(w, width):
    k = w.shape[0]
    return jnp.pad(w.reshape(k, HEADS, width), ((0, 0), (0, 0), (0, LANES - width))).reshape(k, HEADS * LANES)


def _unpad_heads(w, width):
    k = w.shape[0]
    return w.reshape(k, HEADS, LANES)[:, :, :width].reshape(k, HEADS * width)


def _rope_tables(positions, scale):
    inv_freq = 10000.0 ** (-jnp.arange(0, 2 * ROPE_HALF, 2, dtype=F32) / (2 * ROPE_HALF))
    ang = positions.astype(F32)[:, None] * inv_freq
    cos, sin = jnp.cos(ang) * scale, jnp.sin(ang) * scale
    t = positions.shape[0]
    z = lambda n: jnp.zeros((t, n), F32)
    cos_p = jnp.concatenate([jnp.full((t, HEAD_DIM), scale, F32), cos, cos, z(32)], axis=1)
    sin_a = jnp.concatenate([z(64), -sin, z(48)], axis=1)
    sin_b = jnp.concatenate([z(80), sin, z(32)], axis=1)
    fwd = (cos_p, sin_a, sin_b)
    bwd = (cos_p, jnp.roll(sin_b, -ROPE_HALF, axis=1), jnp.roll(sin_a, ROPE_HALF, axis=1))
    return fwd, bwd


def _key_rows(cum, nb, seq):
    v = -cum.reshape(nb, seq, LANES)[:, :, :HEADS]
    return jnp.broadcast_to(jnp.transpose(v, (0, 2, 1))[:, :, None, :], (nb, HEADS, 8, seq))


def kernel(x, c, positions, ada_w, ada_b, norm_mix_g, norm_mlp_g, fox_w_in, fox_b_f, fox_w_out, mla_w_dq, mla_q_norm_g, mla_w_uq, mla_w_dkv, mla_kv_norm_g, mla_w_ukv, mla_w_out, mlp_w1, mlp_w2, final_norm_g, loss_target, m_ada_w, m_ada_b, m_norm_mix_g, m_norm_mlp_g, m_fox_w_in, m_fox_b_f, m_fox_w_out, m_mla_w_dq, m_mla_q_norm_g, m_mla_w_uq, m_mla_w_dkv, m_mla_kv_norm_g, m_mla_w_ukv, m_mla_w_out, m_mlp_w1, m_mlp_w2, m_final_norm_g, v_ada_w, v_ada_b, v_norm_mix_g, v_norm_mlp_g, v_fox_w_in, v_fox_b_f, v_fox_w_out, v_mla_w_dq, v_mla_q_norm_g, v_mla_w_uq, v_mla_w_dkv, v_mla_kv_norm_g, v_mla_w_ukv, v_mla_w_out, v_mlp_w1, v_mlp_w2, v_final_norm_g):
    args = dict(locals())
    weights = {n: args[n] for n in WEIGHTS}
    nb, seq, d = x.shape
    t = nb * seq
    depth = ada_w.shape[0]
    dev = 4 * lax.axis_index("x") + 2 * lax.axis_index("y") + lax.axis_index("c")
    n_mod_local = ada_w.shape[2]
    shards = {n: weights[n] for n, _ in PACKED}

    gathered = _all_gather(_pack_local(shards), "gather_weights")
    full = _unpack_gathered(gathered, shards)
    n_qg = mla_q_norm_g.shape[1]
    cond = jnp.concatenate([c, jnp.pad(mla_q_norm_g.reshape(1, -1), ((0, 7), (0, d - 2 * n_qg)))], axis=0)
    cond_all = _all_gather(cond, "gather_cond")
    c_all = cond_all[:, :nb].reshape(N_DEV * nb, d)
    q_gain = jnp.transpose(cond_all[:, nb, :2 * n_qg].reshape(N_DEV, 2, n_qg), (1, 0, 2)).reshape(2, N_DEV * n_qg)
    mod_local = jnp.stack([
        _matmul(c_all, ada_w[i], mode="nn", name="ada_mod", out_dtype=F32, a_act="silu", epi="bias",
                extras=(lax.dynamic_slice_in_dim(ada_b[i], dev * n_mod_local, n_mod_local)[None, :],))
        for i in range(depth)])
    mod_all = _all_gather(mod_local.reshape(depth * N_DEV * nb, n_mod_local), "gather_mod")
    mod_all = jnp.transpose(mod_all.reshape(N_DEV, depth, N_DEV * nb, n_mod_local), (1, 2, 0, 3))
    mod_all = mod_all.reshape(depth, N_DEV * nb, N_DEV * n_mod_local)
    mod = lax.dynamic_slice_in_dim(mod_all, dev * nb, nb, axis=1)
    mod = mod.reshape(depth, nb, 6, 1, d)

    pos = positions.reshape(t)
    rope_q, rope_q_t = _rope_tables(pos, MLA_SCALE)
    rope_k, rope_k_t = _rope_tables(pos, 1.0)

    def fox_weights(j):
        w_in = full["fox_w_in"][j]
        wq = _pad_heads(w_in[:, :d] * FOX_SCALE, HEAD_DIM)
        wk = _pad_heads(w_in[:, d:2 * d], HEAD_DIM)
        w_qkv = jnp.concatenate([wq, wk, w_in[:, 2 * d:3 * d]], axis=1)
        w_f = jnp.pad(w_in[:, 3 * d:], ((0, 0), (0, LANES - HEADS)))
        return w_qkv, w_f

    def mla_weights(j):
        w_dkv = full["mla_w_dkv"][j]
        w_down = jnp.concatenate([full["mla_w_dq"][j], w_dkv[:, :128], jnp.zeros((d, 64), BF),
                                  w_dkv[:, 128:160], jnp.zeros((d, 32), BF)], axis=1)
        w_uq = _pad_heads(full["mla_w_uq"][j], 96)
        w_ukv = full["mla_w_ukv"][j].reshape(128, HEADS, 2, HEAD_DIM)
        w_uk = jnp.pad(w_ukv[:, :, 0, :], ((0, 0), (0, 0), (0, 64))).reshape(128, HEADS * LANES)
        w_uv = w_ukv[:, :, 1, :].reshape(128, HEADS * HEAD_DIM)
        place = np.zeros((128, HEADS, LANES), np.float32)
        for i in range(2 * ROPE_HALF):
            place[64 + i, :, 64 + i] = 1.0
        bottom = jnp.concatenate([jnp.asarray(place.reshape(128, HEADS * LANES), BF),
                                  jnp.zeros((128, HEADS * HEAD_DIM), BF)], axis=1)
        w_kv = jnp.concatenate([jnp.concatenate([w_uk, w_uv], axis=1), bottom], axis=0)
        return w_down, w_uq, w_kv

    xs = x.reshape(t, d)
    saved = []
    for i in range(depth):
        j = i // 2
        sh_m, sc_m, g_m, sh_f, sc_f, g_f = (mod[i, :, q] for q in range(6))
        gain_mix = norm_mix_g[i][None, :]
        gain_mlp = norm_mlp_g[i][None, :]
        s = dict(x_in=xs)
        h = _norm_mod(xs, gain_mix, sc_m, sh_m, seq, "norm_mix")
        s["h"] = h
        if i % 2 == 0:
            w_qkv, w_f = fox_weights(j)
            qkv = _matmul(h, w_qkv, mode="nn", name="fox_qkv")
            fg = _matmul(h, w_f, mode="nn", name="fox_gate_logits", out_dtype=F32)
            b_f = jnp.pad(fox_b_f[j], (0, LANES - HEADS))[None, :]
            cum = _fox_gate(fg, b_f, seq, "fox_gate")
            bias = _key_rows(cum, nb, seq)
            o, lse, o32 = _attn_fwd(qkv, 0, qkv, 8, qkv, 32, bias, seq, "fox_attn")
            s.update(qkv=qkv, fg=fg, b_f=b_f, bias=bias, w_qkv=w_qkv, w_f=w_f, o32=o32)
            w_out = full["fox_w_out"][j]
        else:
            w_down, w_uq, w_kv = mla_weights(j)
            down = _matmul(h, w_down, mode="nn", name="mla_down", out_dtype=F32)
            gq, gkv = q_gain[j][None, :], mla_kv_norm_g[j][None, :]
            cq, ckr = _mla_mid(down, gq, gkv, rope_k, "mla_mid")
            q_raw = _matmul(cq, w_uq, mode="nn", name="mla_uq", out_dtype=F32)
            q_rot = _rope(q_raw, rope_q, "mla_rope_q")
            kv = _matmul(ckr, w_kv, mode="nn", name="mla_ukv")
            o, lse = _attn_fwd(q_rot, 0, kv, 0, kv, 16, None, seq, "mla_attn")
            s.update(down=down, gq=gq, gkv=gkv, cq=cq, ckr=ckr, q_rot=q_rot, kv=kv,
                     w_down=w_down, w_uq=w_uq, w_kv=w_kv)
            w_out = full["mla_w_out"][j]
        xs, y = _matmul(o, w_out, mode="nn", name="attn_out", epi="resid_gate", extras=(xs, g_m), seq=seq)
        s.update(o=o, lse=lse, y=y, w_out=w_out, x_mid=xs)
        h2 = _norm_mod(xs, gain_mlp, sc_f, sh_f, seq, "norm_mlp")
        a_pre = _matmul(h2, full["mlp_w1"][i], mode="nn", name="mlp_up")
        xs, y2 = _matmul(a_pre, full["mlp_w2"][i], mode="nn", name="mlp_down", a_act="relu2",
                         epi="resid_gate", extras=(xs, g_f), seq=seq)
        s.update(h2=h2, a_pre=a_pre, y2=y2)
        saved.append(s)

    loss_part, dx, dg_final = _loss_head(xs, final_norm_g[None, :], loss_target.reshape(t, d), "loss_head")

    grads = {n: [None] * weights[n].shape[0] for n, _ in PACKED}
    dg_mix, dg_mlp, db_f, dg_kv, dg_q = [None] * depth, [None] * depth, [None] * 2, [None] * 2, [None] * 2
    dmod = [None] * depth
    for i in reversed(range(depth)):
        j = i // 2
        s = saved[i]
        sh_m, sc_m, g_m, sh_f, sc_f, g_f = (mod[i, :, q] for q in range(6))
        dy2, dg_f = _gate_bwd(dx, s["y2"], g_f, seq, "gate_bwd")
        da_pre = _matmul(dy2, full["mlp_w2"][i], mode="nt", name="mlp_down_dx", epi="mul_drelu",
                         extras=(s["a_pre"],))
        grads["mlp_w2"][i] = _matmul(s["a_pre"], dy2, mode="tn", name="mlp_down_dw", out_dtype=F32, a_act="relu2")
        dh2 = _matmul(da_pre, full["mlp_w1"][i], mode="nt", name="mlp_up_dx")
        grads["mlp_w1"][i] = _matmul(s["h2"], da_pre, mode="tn", name="mlp_up_dw", out_dtype=F32)
        dx, dg_mlp[i], dsc_f, dsh_f = _norm_mod_bwd(dh2, s["x_mid"], norm_mlp_g[i][None, :], sc_f, dx, seq,
                                                    "norm_bwd")
        dy, dg_m = _gate_bwd(dx, s["y"], g_m, seq, "gate_bwd")
        do = _matmul(dy, s["w_out"], mode="nt", name="attn_out_dx")
        dw_out = _matmul(s["o"], dy, mode="tn", name="attn_out_dw", out_dtype=F32)
        if i % 2 == 0:
            qkv = s["qkv"]
            dq, dk, dv, dbias = _attn_bwd(qkv, 0, qkv, 8, qkv, 32, s["bias"], s["o32"], do, s["lse"], seq,
                                          "fox_attn_bwd")
            dqkv = jnp.concatenate([dq, dk, dv], axis=1)
            d_cum = -jnp.transpose(dbias[:, :, 0, :], (0, 2, 1)).reshape(t, HEADS)
            d_cum = jnp.pad(d_cum, ((0, 0), (0, LANES - HEADS)))
            dfg, db = _fox_gate_bwd(d_cum, s["fg"], s["b_f"], seq, "fox_gate_bwd")
            db_f[j] = db
            dh = _matmul(dfg, s["w_f"], mode="nt", name="fox_gate_dx", out_dtype=F32)
            dh = _matmul(dqkv, s["w_qkv"], mode="nt", name="fox_qkv_dx", epi="add", extras=(dh,))
            dw_qkv = _matmul(s["h"], dqkv, mode="tn", name="fox_qkv_dw", out_dtype=F32)
            dw_f = _matmul(s["h"], dfg, mode="tn", name="fox_gate_dw", out_dtype=F32)
            grads["fox_w_in"][j] = jnp.concatenate(
                [_unpad_heads(dw_qkv[:, :2048], HEAD_DIM) * FOX_SCALE, _unpad_heads(dw_qkv[:, 2048:4096], HEAD_DIM),
                 dw_qkv[:, 4096:], dw_f[:, :HEADS]], axis=1)
            grads["fox_w_out"][j] = dw_out
        else:
            kv = s["kv"]
            dq, dk, dv = _attn_bwd(s["q_rot"], 0, kv, 0, kv, 16, None, s["o"], do, s["lse"], seq, "mla_attn_bwd")
            dq_raw = _rope(dq, rope_q_t, "mla_rope_q_bwd")
            dcq = _matmul(dq_raw, s["w_uq"], mode="nt", name="mla_uq_dx")
            dw_uq = _matmul(s["cq"], dq_raw, mode="tn", name="mla_uq_dw", out_dtype=F32)
            dkv = jnp.concatenate([dk, dv], axis=1)
            dckr = _matmul(dkv, s["w_kv"], mode="nt", name="mla_ukv_dx")
            dw_kv = _matmul(s["ckr"], dkv, mode="tn", name="mla_ukv_dw", out_dtype=F32)
            d_down, dgq, dgkv = _mla_mid_bwd(s["down"], dcq, dckr, s["gq"], s["gkv"], rope_k_t, "mla_mid_bwd")
            dg_q[j], dg_kv[j] = dgq, dgkv
            dh = _matmul(d_down, s["w_down"], mode="nt", name="mla_down_dx")
            dw_down = _matmul(s["h"], d_down, mode="tn", name="mla_down_dw", out_dtype=F32)
            grads["mla_w_dq"][j] = dw_down[:, :256]
            grads["mla_w_dkv"][j] = jnp.concatenate([dw_down[:, 256:384], dw_down[:, 448:480]], axis=1)
            grads["mla_w_uq"][j] = _unpad_heads(dw_uq, 96)
            dk_nope = dw_kv[:128, :HEADS * LANES].reshape(128, HEADS, LANES)[:, :, :HEAD_DIM]
            dv_w = dw_kv[:128, HEADS * LANES:].reshape(128, HEADS, HEAD_DIM)
            grads["mla_w_ukv"][j] = jnp.concatenate([dk_nope, dv_w], axis=2).reshape(128, HEADS * LANES)
            grads["mla_w_out"][j] = dw_out
        dx, dg_mix[i], dsc_m, dsh_m = _norm_mod_bwd(dh, s["x_in"], norm_mix_g[i][None, :], sc_m, dx, seq,
                                                    "norm_bwd")
        dmod[i] = jnp.stack([dsh_m, dsc_m, dg_m, dsh_f, dsc_f, dg_f], axis=1).reshape(nb, 6 * d)

    grad_x = dx.reshape(nb, seq, d)

    full_grads = {n: jnp.stack(grads[n]) for n, _ in PACKED if n != "mla_q_norm_g"}
    full_grads["mla_q_norm_g"] = jnp.concatenate(dg_q, axis=0)
    packed = _pack_grads(full_grads, shards)
    from_sibling = _swap_with_sibling(packed, "grads_to_sibling")
    own, parts = _chip_partial(packed, from_sibling, "grads_chip_sum")
    from_chips = _swap_with_chips(parts, "grads_to_chips")
    shard_grads = _unpack_shard_grads(_add_parts(own, from_chips, "grads_total"), shards)

    dmod_arr = jnp.stack(dmod)
    small = jnp.concatenate([
        loss_part.reshape(-1), jnp.concatenate(dg_mix, axis=0).reshape(-1), jnp.concatenate(dg_mlp, axis=0).reshape(-1),
        jnp.concatenate(db_f, axis=0).reshape(-1), jnp.concatenate(dg_kv, axis=0).reshape(-1), dg_final.reshape(-1),
        jnp.sum(dmod_arr, axis=1).reshape(-1)])
    n_small = small.shape[0]
    both = jnp.concatenate([small, dmod_arr.reshape(-1)])
    pad = (-both.shape[0]) % (8 * 1024)
    both = jnp.pad(both, (0, pad)).reshape(-1, 1024)
    both_all = _all_gather(both, "gather_small")
    total = _sum_leading(both_all, "sum_small").reshape(-1)
    off = 0

    def take(n, shape):
        nonlocal off
        out = total[off:off + n].reshape(shape)
        off += n
        return out

    loss = take(LANES, (LANES,))[0]
    g_small = dict(
        norm_mix_g=take(depth * d, (depth, d)), norm_mlp_g=take(depth * d, (depth, d)),
        fox_b_f=take(2 * LANES, (2, LANES))[:, :HEADS], mla_kv_norm_g=take(2 * 128, (2, 128)),
        final_norm_g=take(d, (d,)), ada_b=take(depth * 6 * d, (depth, 6 * d)))
    dmod_all = both_all.reshape(N_DEV, -1)[:, n_small:n_small + depth * nb * 6 * d]
    dmod_all = jnp.transpose(dmod_all.reshape(N_DEV, depth, nb, 6 * d), (1, 0, 2, 3)).reshape(depth, N_DEV * nb, 6 * d)
    dmod_cols = lax.dynamic_slice_in_dim(dmod_all, dev * n_mod_local, n_mod_local, axis=2)
    g_ada_w = jnp.stack([_matmul(c_all, dmod_cols[i], mode="tn", name="ada_dw", out_dtype=F32, a_act="silu")
                         for i in range(depth)])

    all_grads = dict(shard_grads)
    all_grads.update(g_small)
    all_grads["ada_w"] = g_ada_w

    deltas, new_m, new_v = {}, {}, {}
    for n in WEIGHTS:
        deltas[n], new_m[n], new_v[n] = _adamw(weights[n], all_grads[n], args["m_" + n], args["v_" + n], "adamw")

    return (loss, grad_x, *[all_grads[n] for n in WEIGHTS], *[deltas[n] for n in WEIGHTS],
            *[new_m[n] for n in WEIGHTS], *[new_v[n] for n in WEIGHTS])
```

```python
import functools
import math

import jax
import jax.numpy as jnp
import numpy as np
from jax import lax
from jax.experimental import pallas as pl
from jax.experimental.pallas import tpu as pltpu

F32 = jnp.float32
BF = jnp.bfloat16

N_DEV = 8
HEADS = 16
HEAD_PAIRS = HEADS // 2
HEAD_DIM = 64
LANES = 128
ROPE_HALF = 16
NORM_EPS = 1e-6
MLA_SCALE = 96.0 ** -0.5
FOX_SCALE = 0.125
ATTN_BLOCK = 256
ROW_BLOCK = 256
VMEM_LIMIT = 56 * 1024 * 1024
MESH = pl.DeviceIdType.MESH

ADAM_LR = 0.001
ADAM_B1 = 0.9
ADAM_B2 = 0.999
ADAM_EPS = 1e-08
ADAM_WD = 0.01
ADAM_STEP = 10

PACKED = (("fox_w_in", "col"), ("fox_w_out", "row"), ("mla_w_dq", "row"), ("mla_q_norm_g", "vec"),
          ("mla_w_uq", "col"), ("mla_w_dkv", "row"), ("mla_w_ukv", "col"), ("mla_w_out", "row"))
WEIGHTS = ("ada_w", "ada_b", "norm_mix_g", "norm_mlp_g", "fox_w_in", "fox_b_f", "fox_w_out", "mla_w_dq",
           "mla_q_norm_g", "mla_w_uq", "mla_w_dkv", "mla_kv_norm_g", "mla_w_ukv", "mla_w_out", "mlp_w1",
           "mlp_w2", "final_norm_g")
PACK_ALIGN = 256 * 1024


def _params(sem=None):
    return pltpu.CompilerParams(dimension_semantics=sem, vmem_limit_bytes=VMEM_LIMIT)


def _pick(n, target):
    if n <= target:
        return n
    for t in range(target, 127, -128):
        if n % t == 0:
            return t
    return n


def _rows(n, target=512):
    if n <= target:
        return n
    for t in range(target, 7, -8):
        if n % t == 0:
            return t
    return n


def _place():
    x, y, c = lax.axis_index("x"), lax.axis_index("y"), lax.axis_index("c")
    return x, y, c


def _adamw_math(w, g, m, v):
    nm = ADAM_B1 * m + (1.0 - ADAM_B1) * g
    nv = ADAM_B2 * v + (1.0 - ADAM_B2) * (g * g)
    m_hat = nm * (1.0 / (1.0 - ADAM_B1 ** ADAM_STEP))
    v_hat = nv * (1.0 / (1.0 - ADAM_B2 ** ADAM_STEP))
    return -ADAM_LR * (m_hat / (jnp.sqrt(v_hat) + ADAM_EPS) + ADAM_WD * w), nm, nv


def _all_gather(blocks, name):
    n = len(blocks)

    def body(*refs):
        x_refs, out_refs = refs[:n], refs[n:2 * n]
        send_sems, recv_sems, local_sems = refs[2 * n:]
        x, y, c = _place()
        me, sibling = (x, y, c), (x, y, 1 - c)
        chips = [(1 - x, y), (x, 1 - y), (1 - x, 1 - y)]

        def slot(a, px, py, pc):
            return out_refs[a].at[4 * px + 2 * py + pc]

        def copy(a, k, blk, to, src=None):
            return pltpu.make_async_remote_copy(
                src_ref=slot(a, *blk) if src is None else src, dst_ref=slot(a, *blk),
                send_sem=send_sems.at[7 * a + k], recv_sem=recv_sems.at[7 * a + k], device_id=to, device_id_type=MESH)

        mine = [pltpu.make_async_copy(x_refs[a], slot(a, *me), local_sems.at[a]) for a in range(n)]
        for cp in mine:
            cp.start()
        first = []
        for j, chip in enumerate(chips):
            first += [copy(a, 1 + j, me, (*chip, c), src=x_refs[a]) for a in range(n)]
        first += [copy(a, 0, me, sibling, src=x_refs[a]) for a in range(n)]
        for cp in first:
            cp.start()
        passed = []
        for j, chip in enumerate(chips):
            for a in range(n):
                copy(a, 1 + j, (*chip, c), me).wait_recv()
                passed.append(copy(a, 4 + j, (*chip, c), sibling))
                passed[-1].start()
        for a in range(n):
            copy(a, 0, sibling, me).wait_recv()
        for j, chip in enumerate(chips):
            for a in range(n):
                copy(a, 4 + j, (*chip, 1 - c), me).wait_recv()
        for cp in first + passed:
            cp.wait_send()
        for cp in mine:
            cp.wait()

    any_spec = pl.BlockSpec(memory_space=pl.ANY)
    return pl.pallas_call(
        body, name=name,
        out_shape=[jax.ShapeDtypeStruct((N_DEV,) + b.shape, b.dtype) for b in blocks],
        in_specs=[any_spec] * n, out_specs=[any_spec] * n,
        scratch_shapes=[pltpu.SemaphoreType.DMA((7 * n,)), pltpu.SemaphoreType.DMA((7 * n,)),
                        pltpu.SemaphoreType.DMA((n,))],
    )(*blocks)


def _swap_with_sibling(bufs, name):
    n = len(bufs)

    def body(*refs):
        src, dst = refs[:n], refs[n:2 * n]
        send_sems, recv_sems = refs[2 * n:]
        x, y, c = _place()
        cps = []
        for a in range(n):
            for k in range(4):
                cps.append(pltpu.make_async_remote_copy(
                    src_ref=src[a].at[2 * k + (1 - c)], dst_ref=dst[a].at[k],
                    send_sem=send_sems.at[4 * a + k], recv_sem=recv_sems.at[4 * a + k],
                    device_id=(x, y, 1 - c), device_id_type=MESH))
        for cp in cps:
            cp.start()
        for cp in cps:
            cp.wait()

    any_spec = pl.BlockSpec(memory_space=pl.ANY)
    return pl.pallas_call(
        body, name=name,
        out_shape=[jax.ShapeDtypeStruct((4,) + b.shape[1:], b.dtype) for b in bufs],
        in_specs=[any_spec] * n, out_specs=[any_spec] * n,
        scratch_shapes=[pltpu.SemaphoreType.DMA((4 * n,)), pltpu.SemaphoreType.DMA((4 * n,))],
    )(*bufs)


def _swap_with_chips(parts, name):
    n = len(parts)

    def body(*refs):
        src, dst = refs[:n], refs[n:2 * n]
        send_sems, recv_sems = refs[2 * n:]
        x, y, c = _place()
        chips = [(1 - x, y), (x, 1 - y), (1 - x, 1 - y)]
        cps = []
        for k, chip in enumerate(chips):
            for a in range(n):
                cps.append(pltpu.make_async_remote_copy(
                    src_ref=src[a].at[k], dst_ref=dst[a].at[k],
                    send_sem=send_sems.at[3 * a + k], recv_sem=recv_sems.at[3 * a + k],
                    device_id=(*chip, c), device_id_type=MESH))
        for cp in cps:
            cp.start()
        for cp in cps:
            cp.wait()

    any_spec = pl.BlockSpec(memory_space=pl.ANY)
    return pl.pallas_call(
        body, name=name,
        out_shape=[jax.ShapeDtypeStruct(p.shape, p.dtype) for p in parts],
        in_specs=[any_spec] * n, out_specs=[any_spec] * n,
        scratch_shapes=[pltpu.SemaphoreType.DMA((3 * n,)), pltpu.SemaphoreType.DMA((3 * n,))],
    )(*parts)


def _relative_blocks():
    x, y, c = _place()
    flips = ((0, 0), (1, 0), (0, 1), (1, 1))
    mine = [4 * (x ^ fx) + 2 * (y ^ fy) + c for fx, fy in flips]
    sib = [2 * (x ^ fx) + (y ^ fy) for fx, fy in flips]
    return jnp.stack(mine + sib).astype(jnp.int32)


def _chip_partial(buf, from_sibling, rel, name):
    _, r, cdim = buf.shape
    tr = _rows(r, 256)

    def body(rel_ref, m0, m1, m2, m3, s0, s1, s2, s3, own_ref, parts_ref):
        del rel_ref
        own_ref[...] = m0[...].astype(F32) + s0[...].astype(F32)
        for k, (m, s) in enumerate(((m1, s1), (m2, s2), (m3, s3))):
            parts_ref[k] = (m[...].astype(F32) + s[...].astype(F32)).astype(parts_ref.dtype)

    def pick(k):
        return pl.BlockSpec((None, tr, cdim), lambda i, rel_ref: (rel_ref[k], i, 0))

    return pl.pallas_call(
        body, name=name,
        grid_spec=pltpu.PrefetchScalarGridSpec(
            num_scalar_prefetch=1, grid=(r // tr,),
            in_specs=[pick(k) for k in range(8)],
            out_specs=(pl.BlockSpec((tr, cdim), lambda i, rel_ref: (i, 0)),
                       pl.BlockSpec((3, tr, cdim), lambda i, rel_ref: (0, i, 0)))),
        out_shape=(jax.ShapeDtypeStruct((r, cdim), F32), jax.ShapeDtypeStruct((3, r, cdim), buf.dtype)),
        compiler_params=_params(("parallel",)),
    )(rel, buf, buf, buf, buf, from_sibling, from_sibling, from_sibling, from_sibling)


def _total_adamw(own, parts, w, m, v, name):
    r, cdim = own.shape
    tr = _rows(r, 256)

    def body(own_ref, parts_ref, w_ref, m_ref, v_ref, g_ref, d_ref, nm_ref, nv_ref):
        g = own_ref[...]
        for k in range(3):
            g = g + parts_ref[k].astype(F32)
        g_ref[...] = g
        d_ref[...], nm_ref[...], nv_ref[...] = _adamw_math(w_ref[...], g, m_ref[...], v_ref[...])

    blk = pl.BlockSpec((tr, cdim), lambda i: (i, 0))
    sds = jax.ShapeDtypeStruct((r, cdim), F32)
    outs = pl.pallas_call(
        body, name=name, grid=(r // tr,),
        out_shape=(sds,) * 4,
        in_specs=[blk, pl.BlockSpec((3, tr, cdim), lambda i: (0, i, 0)), blk, blk, blk], out_specs=(blk,) * 4,
        compiler_params=_params(("parallel",)),
    )(own, parts, *[a.reshape(r, cdim) for a in (w, m, v)])
    return tuple(a.reshape(w.shape) for a in outs)


def _add_parts(own, parts, name):
    r, cdim = own.shape
    tr = _rows(r, 512)

    def body(own_ref, parts_ref, out_ref):
        acc = own_ref[...]
        for k in range(parts_ref.shape[0]):
            acc = acc + parts_ref[k].astype(F32)
        out_ref[...] = acc

    return pl.pallas_call(
        body, name=name, grid=(r // tr,),
        out_shape=jax.ShapeDtypeStruct((r, cdim), F32),
        in_specs=[pl.BlockSpec((tr, cdim), lambda i: (i, 0)),
                  pl.BlockSpec((parts.shape[0], tr, cdim), lambda i: (0, i, 0))],
        out_specs=pl.BlockSpec((tr, cdim), lambda i: (i, 0)),
        compiler_params=_params(("parallel",)),
    )(own, parts)


def _sum_leading(stack, name):
    n, r, cdim = stack.shape
    tr = _rows(r, 512)

    def body(in_ref, out_ref):
        acc = in_ref[0]
        for k in range(1, n):
            acc = acc + in_ref[k]
        out_ref[...] = acc

    return pl.pallas_call(
        body, name=name, grid=(r // tr,),
        out_shape=jax.ShapeDtypeStruct((r, cdim), F32),
        in_specs=[pl.BlockSpec((n, tr, cdim), lambda i: (0, i, 0))],
        out_specs=pl.BlockSpec((tr, cdim), lambda i: (i, 0)),
        compiler_params=_params(("parallel",)),
    )(stack)


_DIMS = {"nn": (((1,), (0,)), ((), ())), "nt": (((1,), (1,)), ((), ())), "tn": (((0,), (0,)), ((), ()))}


def _stack_spec(shape, mode, layer):
    cut, l, rows = layer
    cols = shape[2]
    by_n = pl.BlockSpec((None, rows, cols), lambda i, j, k: (j, l, 0))
    by_k = pl.BlockSpec((None, rows, cols), lambda i, j, k: (k, l, 0))
    if cut == "col":
        return (by_n, N_DEV * cols, cols, rows) if mode == "nn" else (by_k, rows, rows, cols)
    return (by_k, cols, cols, rows) if mode == "nn" else (by_n, N_DEV * rows, rows, cols)


def _matmul(a, b, *, mode, name, out_dtype=BF, a_act=None, epi=None, extras=(), seq=None, layer=None, tn=None,
            into=None):
    if mode == "tn":
        kdim, m = a.shape
    else:
        m, kdim = a.shape
    tm, tk = _pick(m, 512), _pick(kdim, 1024)
    b_spec = None
    if layer is not None:
        b_spec, n, tn, tk = _stack_spec(b.shape, mode, layer)
    else:
        n = b.shape[0] if mode == "nt" else b.shape[1]
        tn = _pick(n, 1024) if tn is None else tn
    nk = kdim // tk
    a_spec = (pl.BlockSpec((tk, tm), lambda i, j, k: (k, i)) if mode == "tn"
              else pl.BlockSpec((tm, tk), lambda i, j, k: (i, k)))
    if b_spec is None:
        b_spec = (pl.BlockSpec((tn, tk), lambda i, j, k: (j, k)) if mode == "nt"
                  else pl.BlockSpec((tk, tn), lambda i, j, k: (k, j)))
    tile = pl.BlockSpec((tm, tn), lambda i, j, k: (i, j))
    in_specs, out_specs = [a_spec, b_spec], [tile]
    out_shape = [jax.ShapeDtypeStruct((m, n), out_dtype)]
    if epi == "resid_gate":
        in_specs += [tile, pl.BlockSpec((None, 1, tn), lambda i, j, k: ((i * tm) // seq, 0, j))]
        out_shape = [jax.ShapeDtypeStruct((m, n), F32), jax.ShapeDtypeStruct((m, n), BF)]
        out_specs = [tile, tile]
    elif epi in ("mul_drelu", "add"):
        in_specs += [tile]
    elif epi == "bias":
        in_specs += [pl.BlockSpec((1, tn), lambda i, j, k: (0, j))]
    n_extra, n_out = len(in_specs) - 2, len(out_specs)
    aliases, n_kept = {}, 0
    if into is not None:
        buffer, block, index_map = into
        out_dtype = buffer.dtype
        in_specs.append(pl.BlockSpec(memory_space=pl.ANY))
        extras = tuple(extras) + (buffer,)
        aliases, n_kept = {len(in_specs) - 1: 0}, 1
        out_shape = [jax.ShapeDtypeStruct(buffer.shape, buffer.dtype)]
        out_specs = [pl.BlockSpec(block, index_map)]
    dims = _DIMS[mode]

    def body(*refs):
        a_ref, b_ref = refs[:2]
        ex = refs[2:2 + n_extra]
        outs = refs[2 + n_extra + n_kept:2 + n_extra + n_kept + n_out]
        av = a_ref[...]
        if a_act == "relu2":
            t = jnp.maximum(av.astype(F32), 0.0)
            av = t * t
        elif a_act == "silu":
            t = av.astype(F32)
            av = t / (1.0 + jnp.exp(-t))
        part = lax.dot_general(av.astype(BF), b_ref[...].astype(BF), dims, preferred_element_type=F32)

        def finish(acc):
            if epi == "resid_gate":
                outs[0][...] = ex[0][...] + ex[1][...] * acc
                outs[1][...] = acc.astype(BF)
            elif epi == "mul_drelu":
                outs[0][...] = (acc * (2.0 * jnp.maximum(ex[0][...].astype(F32), 0.0))).astype(out_dtype)
            elif epi == "add":
                outs[0][...] = (acc + ex[0][...].astype(F32)).astype(out_dtype)
            elif epi == "bias":
                outs[0][...] = (acc + ex[0][...]).astype(out_dtype)
            else:
                outs[0][...] = acc.astype(out_dtype)

        if nk == 1:
            finish(part)
        else:
            acc_ref = refs[-1]
            k = pl.program_id(2)

            @pl.when(k == 0)
            def _():
                acc_ref[...] = part

            @pl.when(k > 0)
            def _():
                acc_ref[...] += part

            @pl.when(k == nk - 1)
            def _():
                finish(acc_ref[...])

    res = pl.pallas_call(
        body, name=name, grid=(m // tm, n // tn, nk),
        out_shape=out_shape, in_specs=in_specs, out_specs=out_specs,
        scratch_shapes=[pltpu.VMEM((tm, tn), F32)] if nk > 1 else [],
        input_output_aliases=aliases,
        compiler_params=_params(("parallel", "parallel", "arbitrary")),
    )(a, b, *extras)
    return res if n_out > 1 else res[0]


def _norm_mod(x, gain, scale, shift, seq, name):
    t, w = x.shape
    tr = ROW_BLOCK

    def body(x_ref, g_ref, sc_ref, sh_ref, out_ref):
        xv = x_ref[...]
        rstd = lax.rsqrt(jnp.mean(xv * xv, axis=-1, keepdims=True) + NORM_EPS)
        y = xv * rstd * g_ref[...]
        out_ref[...] = (y * (1.0 + sc_ref[...]) + sh_ref[...]).astype(BF)

    per_b = pl.BlockSpec((None, 1, w), lambda i: ((i * tr) // seq, 0, 0))
    return pl.pallas_call(
        body, name=name, grid=(t // tr,),
        out_shape=jax.ShapeDtypeStruct((t, w), BF),
        in_specs=[pl.BlockSpec((tr, w), lambda i: (i, 0)), pl.BlockSpec((1, w), lambda i: (0, 0)), per_b, per_b],
        out_specs=pl.BlockSpec((tr, w), lambda i: (i, 0)),
        compiler_params=_params(("parallel",)),
    )(x, gain, scale, shift)


def _norm_mod_bwd(dh, x, gain, scale, dres, seq, name):
    t, w = x.shape
    tr = ROW_BLOCK
    steps_per_seq = seq // tr
    nb = t // seq

    def body(dh_ref, x_ref, g_ref, sc_ref, dres_ref, dx_ref, dg_ref, dsc_ref, dsh_ref):
        i = pl.program_id(0)
        xv = x_ref[...]
        dhv = dh_ref[...].astype(F32)
        rstd = lax.rsqrt(jnp.mean(xv * xv, axis=-1, keepdims=True) + NORM_EPS)
        xhat = xv * rstd
        one_sc = 1.0 + sc_ref[...]
        g = g_ref[...]
        dxhat = dhv * (g * one_sc)
        proj = jnp.mean(dxhat * xhat, axis=-1, keepdims=True)
        dx_ref[...] = dres_ref[...] + rstd * (dxhat - xhat * proj)
        dhx = dhv * xhat
        dg_part = jnp.sum(dhx * one_sc, axis=0, keepdims=True)
        dsc_part = jnp.sum(dhx * g, axis=0, keepdims=True)
        dsh_part = jnp.sum(dhv, axis=0, keepdims=True)

        @pl.when(i == 0)
        def _():
            dg_ref[...] = dg_part

        @pl.when(i > 0)
        def _():
            dg_ref[...] += dg_part

        @pl.when(i % steps_per_seq == 0)
        def _():
            dsc_ref[...] = dsc_part
            dsh_ref[...] = dsh_part

        @pl.when(i % steps_per_seq != 0)
        def _():
            dsc_ref[...] += dsc_part
            dsh_ref[...] += dsh_part

    row = pl.BlockSpec((tr, w), lambda i: (i, 0))
    per_b = pl.BlockSpec((None, 1, w), lambda i: ((i * tr) // seq, 0, 0))
    vec = pl.BlockSpec((1, w), lambda i: (0, 0))
    return pl.pallas_call(
        body, name=name, grid=(t // tr,),
        out_shape=(jax.ShapeDtypeStruct((t, w), F32), jax.ShapeDtypeStruct((1, w), F32),
                   jax.ShapeDtypeStruct((nb, 1, w), F32), jax.ShapeDtypeStruct((nb, 1, w), F32)),
        in_specs=[row, row, vec, per_b, row],
        out_specs=(row, vec, per_b, per_b),
        compiler_params=_params(("arbitrary",)),
    )(dh, x, gain, scale, dres)


def _gate_bwd(dx, y, gate, seq, name):
    t, w = dx.shape
    tr = ROW_BLOCK
    steps_per_seq = seq // tr
    nb = t // seq

    def body(dx_ref, y_ref, g_ref, dy_ref, dg_ref):
        i = pl.program_id(0)
        dxv = dx_ref[...]
        dy_ref[...] = (dxv * g_ref[...]).astype(BF)
        part = jnp.sum(dxv * y_ref[...].astype(F32), axis=0, keepdims=True)

        @pl.when(i % steps_per_seq == 0)
        def _():
            dg_ref[...] = part

        @pl.when(i % steps_per_seq != 0)
        def _():
            dg_ref[...] += part

    row = pl.BlockSpec((tr, w), lambda i: (i, 0))
    per_b = pl.BlockSpec((None, 1, w), lambda i: ((i * tr) // seq, 0, 0))
    return pl.pallas_call(
        body, name=name, grid=(t // tr,),
        out_shape=(jax.ShapeDtypeStruct((t, w), BF), jax.ShapeDtypeStruct((nb, 1, w), F32)),
        in_specs=[row, row, per_b], out_specs=(row, per_b),
        compiler_params=_params(("arbitrary",)),
    )(dx, y, gate)


def _loss_head(x, gain, target, name):
    t, w = x.shape
    tr = ROW_BLOCK

    def body(x_ref, g_ref, t_ref, loss_ref, dx_ref, dg_ref):
        i = pl.program_id(0)
        xv = x_ref[...]
        g = g_ref[...]
        rstd = lax.rsqrt(jnp.mean(xv * xv, axis=-1, keepdims=True) + NORM_EPS)
        xhat = xv * rstd
        err = xhat * g - t_ref[...]
        row_loss = jnp.sum(err * err, axis=-1, keepdims=True) * (0.5 / w)
        loss_part = jnp.broadcast_to(jnp.sum(row_loss, axis=0, keepdims=True), (1, LANES))
        dy = err * (1.0 / w)
        dg_part = jnp.sum(dy * xhat, axis=0, keepdims=True)
        dxhat = dy * g
        proj = jnp.mean(dxhat * xhat, axis=-1, keepdims=True)
        dx_ref[...] = rstd * (dxhat - xhat * proj)

        @pl.when(i == 0)
        def _():
            loss_ref[...] = loss_part
            dg_ref[...] = dg_part

        @pl.when(i > 0)
        def _():
            loss_ref[...] += loss_part
            dg_ref[...] += dg_part

    row = pl.BlockSpec((tr, w), lambda i: (i, 0))
    vec = pl.BlockSpec((1, w), lambda i: (0, 0))
    return pl.pallas_call(
        body, name=name, grid=(t // tr,),
        out_shape=(jax.ShapeDtypeStruct((1, LANES), F32), jax.ShapeDtypeStruct((t, w), F32),
                   jax.ShapeDtypeStruct((1, w), F32)),
        in_specs=[row, vec, row],
        out_specs=(pl.BlockSpec((1, LANES), lambda i: (0, 0)), row, vec),
        compiler_params=_params(("arbitrary",)),
    )(x, gain, target)


def _rope_group(xg, cos_p, sin_a, sin_b):
    return (xg * cos_p + pltpu.roll(xg, LANES - ROPE_HALF, axis=1) * sin_a
            + pltpu.roll(xg, ROPE_HALF, axis=1) * sin_b)


def _rope(x, tables, name, out_dtype=BF):
    t, w = x.shape
    tr = ROW_BLOCK
    groups = w // LANES

    def body(x_ref, c_ref, a_ref, b_ref, out_ref):
        cos_p, sin_a, sin_b = c_ref[...], a_ref[...], b_ref[...]
        for g in range(groups):
            sl = slice(g * LANES, (g + 1) * LANES)
            out_ref[:, sl] = _rope_group(x_ref[:, sl].astype(F32), cos_p, sin_a, sin_b).astype(out_dtype)

    row = pl.BlockSpec((tr, w), lambda i: (i, 0))
    tab = pl.BlockSpec((tr, LANES), lambda i: (i, 0))
    return pl.pallas_call(
        body, name=name, grid=(t // tr,),
        out_shape=jax.ShapeDtypeStruct((t, w), out_dtype),
        in_specs=[row, tab, tab, tab], out_specs=row,
        compiler_params=_params(("parallel",)),
    )(x, *tables)


def _mla_mid(down, gq, gkv, tables, name):
    t = down.shape[0]
    tr = ROW_BLOCK

    def body(d_ref, gq_ref, gkv_ref, c_ref, a_ref, b_ref, cq_ref, ckr_ref):
        q = d_ref[:, 0:256]
        cq_ref[...] = (q * lax.rsqrt(jnp.mean(q * q, axis=-1, keepdims=True) + NORM_EPS) * gq_ref[...]).astype(BF)
        kv = d_ref[:, 256:384]
        ckr_ref[:, 0:128] = (kv * lax.rsqrt(jnp.mean(kv * kv, axis=-1, keepdims=True) + NORM_EPS)
                             * gkv_ref[...]).astype(BF)
        ckr_ref[:, 128:256] = _rope_group(d_ref[:, 384:512], c_ref[...], a_ref[...], b_ref[...]).astype(BF)

    tab = pl.BlockSpec((tr, LANES), lambda i: (i, 0))
    return pl.pallas_call(
        body, name=name, grid=(t // tr,),
        out_shape=(jax.ShapeDtypeStruct((t, 256), BF), jax.ShapeDtypeStruct((t, 256), BF)),
        in_specs=[pl.BlockSpec((tr, 512), lambda i: (i, 0)), pl.BlockSpec((1, 256), lambda i: (0, 0)),
                  pl.BlockSpec((1, 128), lambda i: (0, 0)), tab, tab, tab],
        out_specs=(pl.BlockSpec((tr, 256), lambda i: (i, 0)), pl.BlockSpec((tr, 256), lambda i: (i, 0))),
        compiler_params=_params(("parallel",)),
    )(down, gq, gkv, *tables)


def _mla_mid_bwd(down, dcq, dckr, gq, gkv, tables_t, name):
    t = down.shape[0]
    tr = ROW_BLOCK

    def norm_bwd(xv, g, dy):
        rstd = lax.rsqrt(jnp.mean(xv * xv, axis=-1, keepdims=True) + NORM_EPS)
        xhat = xv * rstd
        dxhat = dy * g
        proj = jnp.mean(dxhat * xhat, axis=-1, keepdims=True)
        return rstd * (dxhat - xhat * proj), jnp.sum(dy * xhat, axis=0, keepdims=True)

    def body(d_ref, dcq_ref, dckr_ref, gq_ref, gkv_ref, c_ref, a_ref, b_ref, dd_ref, dgq_ref, dgkv_ref):
        i = pl.program_id(0)
        dq, dgq_part = norm_bwd(d_ref[:, 0:256], gq_ref[...], dcq_ref[...].astype(F32))
        dd_ref[:, 0:256] = dq.astype(BF)
        dkv, dgkv_part = norm_bwd(d_ref[:, 256:384], gkv_ref[...], dckr_ref[:, 0:128].astype(F32))
        dd_ref[:, 256:384] = dkv.astype(BF)
        dd_ref[:, 384:512] = _rope_group(dckr_ref[:, 128:256].astype(F32), c_ref[...], a_ref[...],
                                         b_ref[...]).astype(BF)

        @pl.when(i == 0)
        def _():
            dgq_ref[...] = dgq_part
            dgkv_ref[...] = dgkv_part

        @pl.when(i > 0)
        def _():
            dgq_ref[...] += dgq_part
            dgkv_ref[...] += dgkv_part

    tab = pl.BlockSpec((tr, LANES), lambda i: (i, 0))
    r256 = pl.BlockSpec((tr, 256), lambda i: (i, 0))
    return pl.pallas_call(
        body, name=name, grid=(t // tr,),
        out_shape=(jax.ShapeDtypeStruct((t, 512), BF), jax.ShapeDtypeStruct((1, 256), F32),
                   jax.ShapeDtypeStruct((1, 128), F32)),
        in_specs=[pl.BlockSpec((tr, 512), lambda i: (i, 0)), r256, r256, pl.BlockSpec((1, 256), lambda i: (0, 0)),
                  pl.BlockSpec((1, 128), lambda i: (0, 0)), tab, tab, tab],
        out_specs=(pl.BlockSpec((tr, 512), lambda i: (i, 0)), pl.BlockSpec((1, 256), lambda i: (0, 0)),
                   pl.BlockSpec((1, 128), lambda i: (0, 0))),
        compiler_params=_params(("arbitrary",)),
    )(down, dcq, dckr, gq, gkv, *tables_t)


def _scan_rows(x, reverse):
    s = x.shape[0]
    row = lax.broadcasted_iota(jnp.int32, x.shape, 0)
    step = 1
    while step < s:
        if reverse:
            x = x + jnp.where(row < s - step, pltpu.roll(x, s - step, axis=0), 0.0)
        else:
            x = x + jnp.where(row >= step, pltpu.roll(x, step, axis=0), 0.0)
        step *= 2
    return x


def _fox_gate(fg, b_f, seq, name):
    t = fg.shape[0]

    def body(fg_ref, b_ref, out_ref):
        z = fg_ref[...] + b_ref[...]
        log_f = jnp.minimum(z, 0.0) - jnp.log(1.0 + jnp.exp(-jnp.abs(z)))
        out_ref[...] = _scan_rows(log_f, reverse=False)

    blk = pl.BlockSpec((seq, LANES), lambda b: (b, 0))
    return pl.pallas_call(
        body, name=name, grid=(t // seq,),
        out_shape=jax.ShapeDtypeStruct((t, LANES), F32),
        in_specs=[blk, pl.BlockSpec((1, LANES), lambda b: (0, 0))], out_specs=blk,
        compiler_params=_params(("parallel",)),
    )(fg, b_f)


def _fox_gate_bwd(d_cum, fg, b_f, seq, name):
    t = fg.shape[0]

    def body(dc_ref, fg_ref, b_ref, dfg_ref, db_ref):
        b = pl.program_id(0)
        z = fg_ref[...] + b_ref[...]
        d_log_f = _scan_rows(dc_ref[...], reverse=True)
        dz = d_log_f / (1.0 + jnp.exp(z))
        dfg_ref[...] = dz
        part = jnp.sum(dz, axis=0, keepdims=True)

        @pl.when(b == 0)
        def _():
            db_ref[...] = part

        @pl.when(b > 0)
        def _():
            db_ref[...] += part

    blk = pl.BlockSpec((seq, LANES), lambda b: (b, 0))
    vec = pl.BlockSpec((1, LANES), lambda b: (0, 0))
    return pl.pallas_call(
        body, name=name, grid=(t // seq,),
        out_shape=(jax.ShapeDtypeStruct((t, LANES), F32), jax.ShapeDtypeStruct((1, LANES), F32)),
        in_specs=[blk, blk, vec], out_specs=(blk, vec),
        compiler_params=_params(("arbitrary",)),
    )(d_cum, fg, b_f)


def _head_masks():
    lane = lax.broadcasted_iota(jnp.int32, (1, LANES), 1)
    return lane < HEAD_DIM, lane >= HEAD_DIM


def _attn_fwd(q_arr, q_off, k_arr, k_off, v_arr, v_off, bias, seq, name):
    t = q_arr.shape[0]
    nb = t // seq
    blk = min(ATTN_BLOCK, seq)
    nq = seq // blk
    has_bias = bias is not None

    def body(*refs):
        if has_bias:
            q_ref, k_ref, v_ref, bias_ref, o_ref, lse_ref, o32_ref = refs
        else:
            q_ref, k_ref, v_ref, o_ref, lse_ref = refs
        lo, hi = _head_masks()
        rows = lax.broadcasted_iota(jnp.int32, (blk, blk), 0)
        cols = lax.broadcasted_iota(jnp.int32, (blk, blk), 1)
        causal = cols <= rows

        def q_block(iq, _):
            q0 = pl.multiple_of(iq * blk, blk)
            qs = [q_ref[pl.ds(q0, blk), h * LANES:(h + 1) * LANES] for h in range(2)]

            def kv_block(j, carry, diag):
                k0 = pl.multiple_of(j * blk, blk)
                vv = v_ref[pl.ds(k0, blk), :]
                vs = [jnp.where(lo, vv, jnp.zeros_like(vv)), jnp.where(hi, vv, jnp.zeros_like(vv))]
                acc = carry[0]
                new = []
                pv = None
                alphas = []
                for h in range(2):
                    m, l = carry[1 + 2 * h], carry[2 + 2 * h]
                    kk = k_ref[pl.ds(k0, blk), h * LANES:(h + 1) * LANES]
                    s = lax.dot_general(qs[h], kk, _DIMS["nt"], preferred_element_type=F32)
                    if has_bias:
                        s = s + bias_ref[h, 0:1, pl.ds(k0, blk)]
                    if diag:
                        s = jnp.where(causal, s, -jnp.inf)
                    m_new = jnp.maximum(m, jnp.max(s, axis=-1, keepdims=True))
                    p = jnp.exp(s - m_new)
                    alpha = jnp.exp(m - m_new)
                    l_new = alpha * l + jnp.sum(p, axis=-1, keepdims=True)
                    p_hi = p.astype(BF)
                    d = jnp.dot(p_hi, vs[h], preferred_element_type=F32)
                    if has_bias:
                        p_lo = (p - p_hi.astype(F32)).astype(BF)
                        d = d + jnp.dot(p_lo, vs[h], preferred_element_type=F32)
                    pv = d if pv is None else pv + d
                    alphas.append(alpha)
                    new += [m_new, l_new]
                acc = acc * jnp.where(lo, alphas[0], alphas[1]) + pv
                return (acc, *new)

            init = (jnp.zeros((blk, LANES), F32),
                    jnp.full((blk, 1), -jnp.inf, F32), jnp.zeros((blk, 1), F32),
                    jnp.full((blk, 1), -jnp.inf, F32), jnp.zeros((blk, 1), F32))
            carry = lax.fori_loop(0, iq, functools.partial(kv_block, diag=False), init)
            acc, m0, l0, m1, l1 = kv_block(iq, carry, diag=True)
            o_val = acc / jnp.where(lo, l0, l1)
            o_ref[pl.ds(q0, blk), :] = o_val.astype(BF)
            if has_bias:
                o32_ref[pl.ds(q0, blk), :] = o_val
            lse_ref[pl.ds(q0, blk), :] = jnp.where(lo, m0 + jnp.log(l0), m1 + jnp.log(l1))
            return 0

        lax.fori_loop(0, nq, q_block, 0)

    in_specs = [pl.BlockSpec((seq, 2 * LANES), lambda b, p: (b, q_off + p)),
                pl.BlockSpec((seq, 2 * LANES), lambda b, p: (b, k_off + p)),
                pl.BlockSpec((seq, LANES), lambda b, p: (b, v_off + p))]
    args = [q_arr, k_arr, v_arr]
    if has_bias:
        in_specs.append(pl.BlockSpec((None, 2, 8, seq), lambda b, p: (b, p, 0, 0)))
        args.append(bias)
    out_blk = pl.BlockSpec((seq, LANES), lambda b, p: (b, p))
    out_shape = [jax.ShapeDtypeStruct((t, HEAD_PAIRS * LANES), BF), jax.ShapeDtypeStruct((t, HEAD_PAIRS * LANES), F32)]
    if has_bias:
        out_shape.append(jax.ShapeDtypeStruct((t, HEAD_PAIRS * LANES), F32))
    return pl.pallas_call(
        body, name=name, grid=(nb, HEAD_PAIRS),
        out_shape=out_shape, in_specs=in_specs, out_specs=[out_blk] * len(out_shape),
        compiler_params=_params(("parallel", "parallel")),
    )(*args)


def _attn_bwd(q_arr, q_off, k_arr, k_off, v_arr, v_off, bias, o, do, lse, seq, name):
    t = q_arr.shape[0]
    nb = t // seq
    blk = min(ATTN_BLOCK, seq)
    nq = seq // blk
    has_bias = bias is not None

    def body(*refs):
        if has_bias:
            (q_ref, k_ref, v_ref, bias_ref, o_ref, do_ref, lse_ref,
             dq_ref, dk_ref, dv_ref, dbias_ref, dq_acc, dsum) = refs
        else:
            (q_ref, k_ref, v_ref, o_ref, do_ref, lse_ref, dq_ref, dk_ref, dv_ref, dq_acc, dsum) = refs
        lo, hi = _head_masks()
        rows = lax.broadcasted_iota(jnp.int32, (blk, blk), 0)
        cols = lax.broadcasted_iota(jnp.int32, (blk, blk), 1)
        causal = cols <= rows
        dq_acc[...] = jnp.zeros_like(dq_acc)

        def prep(iq, _):
            q0 = pl.multiple_of(iq * blk, blk)
            prod = do_ref[pl.ds(q0, blk), :].astype(F32) * o_ref[pl.ds(q0, blk), :].astype(F32)
            d0 = jnp.sum(jnp.where(lo, prod, 0.0), axis=-1, keepdims=True)
            d1 = jnp.sum(jnp.where(hi, prod, 0.0), axis=-1, keepdims=True)
            dsum[pl.ds(q0, blk), :] = jnp.where(lo, d0, d1)
            return 0

        lax.fori_loop(0, nq, prep, 0)

        def kv_block(j, _):
            k0 = pl.multiple_of(j * blk, blk)
            vv = v_ref[pl.ds(k0, blk), :]
            vs = [jnp.where(lo, vv, jnp.zeros_like(vv)), jnp.where(hi, vv, jnp.zeros_like(vv))]
            ks = [k_ref[pl.ds(k0, blk), h * LANES:(h + 1) * LANES] for h in range(2)]

            def q_block(iq, carry, diag):
                q0 = pl.multiple_of(iq * blk, blk)
                dov = do_ref[pl.ds(q0, blk), :]
                lse_v = lse_ref[pl.ds(q0, blk), :]
                dsum_v = dsum[pl.ds(q0, blk), :]
                dv_acc = carry[0]
                out = []
                for h in range(2):
                    dk_acc, db_acc = carry[1 + 2 * h], carry[2 + 2 * h]
                    qq = q_ref[pl.ds(q0, blk), h * LANES:(h + 1) * LANES]
                    s = lax.dot_general(qq, ks[h], _DIMS["nt"], preferred_element_type=F32)
                    if has_bias:
                        s = s + bias_ref[h, 0:1, pl.ds(k0, blk)]
                    p = jnp.exp(s - lse_v[:, h * HEAD_DIM:h * HEAD_DIM + 1])
                    if diag:
                        p = jnp.where(causal, p, 0.0)
                    dp = lax.dot_general(dov, vs[h], _DIMS["nt"], preferred_element_type=F32)
                    ds = p * (dp - dsum_v[:, h * HEAD_DIM:h * HEAD_DIM + 1])
                    ds_bf = ds.astype(BF)
                    pt_do = lax.dot_general(p.astype(BF), dov, _DIMS["tn"], preferred_element_type=F32)
                    dv_acc = dv_acc + jnp.where(hi if h else lo, pt_do, 0.0)
                    dk_acc = dk_acc + lax.dot_general(ds_bf, qq, _DIMS["tn"], preferred_element_type=F32)
                    dq_acc[pl.ds(q0, blk), h * LANES:(h + 1) * LANES] += jnp.dot(
                        ds_bf, ks[h], preferred_element_type=F32)
                    if has_bias:
                        db_acc = db_acc + jnp.sum(ds, axis=0, keepdims=True)
                    out += [dk_acc, db_acc]
                return (dv_acc, *out)

            init = (jnp.zeros((blk, LANES), F32),
                    jnp.zeros((blk, LANES), F32), jnp.zeros((1, blk), F32),
                    jnp.zeros((blk, LANES), F32), jnp.zeros((1, blk), F32))
            carry = q_block(j, init, diag=True)
            carry = lax.fori_loop(j + 1, nq, functools.partial(q_block, diag=False), carry)
            dv_ref[pl.ds(k0, blk), :] = carry[0].astype(BF)
            for h in range(2):
                dk_ref[pl.ds(k0, blk), h * LANES:(h + 1) * LANES] = carry[1 + 2 * h].astype(BF)
                if has_bias:
                    dbias_ref[h, :, pl.ds(k0, blk)] = jnp.broadcast_to(carry[2 + 2 * h], (8, blk))
            return 0

        lax.fori_loop(0, nq, kv_block, 0)
        dq_ref[...] = dq_acc[...].astype(BF)

    pair256 = lambda off: pl.BlockSpec((seq, 2 * LANES), lambda b, p: (b, off + p))
    pair128 = lambda off: pl.BlockSpec((seq, LANES), lambda b, p: (b, off + p))
    bias_spec = pl.BlockSpec((None, 2, 8, seq), lambda b, p: (b, p, 0, 0))
    in_specs = [pair256(q_off), pair256(k_off), pair128(v_off)]
    args = [q_arr, k_arr, v_arr]
    if has_bias:
        in_specs.append(bias_spec)
        args.append(bias)
    in_specs += [pair128(0), pair128(0), pair128(0)]
    args += [o, do, lse]
    out_shape = [jax.ShapeDtypeStruct((t, HEAD_PAIRS * 2 * LANES), BF),
                 jax.ShapeDtypeStruct((t, HEAD_PAIRS * 2 * LANES), BF),
                 jax.ShapeDtypeStruct((t, HEAD_PAIRS * LANES), BF)]
    out_specs = [pair256(0), pair256(0), pair128(0)]
    if has_bias:
        out_shape.append(jax.ShapeDtypeStruct((nb, HEADS, 8, seq), F32))
        out_specs.append(bias_spec)
    return pl.pallas_call(
        body, name=name, grid=(nb, HEAD_PAIRS),
        out_shape=out_shape, in_specs=in_specs, out_specs=out_specs,
        scratch_shapes=[pltpu.VMEM((seq, 2 * LANES), F32), pltpu.VMEM((seq, LANES), F32)],
        compiler_params=_params(("parallel", "parallel")),
    )(*args)


def _adamw(w, g, m, v, name):
    shape = w.shape
    last = shape[-1]
    rows = int(np.prod(shape[:-1])) if len(shape) > 1 else 1
    tr = _rows(rows, 512)

    def body(w_ref, g_ref, m_ref, v_ref, d_ref, nm_ref, nv_ref):
        d_ref[...], nm_ref[...], nv_ref[...] = _adamw_math(w_ref[...], g_ref[...], m_ref[...], v_ref[...])

    blk = pl.BlockSpec((tr, last), lambda i: (i, 0))
    sds = jax.ShapeDtypeStruct((rows, last), F32)
    outs = pl.pallas_call(
        body, name=name, grid=(rows // tr,),
        out_shape=(sds, sds, sds), in_specs=[blk] * 4, out_specs=(blk,) * 3,
        compiler_params=_params(("parallel",)),
    )(*[a.reshape(rows, last) for a in (w, g, m, v)])
    return tuple(a.reshape(shape) for a in outs)


def _pack_local(shards):
    flat = jnp.concatenate([shards[n].astype(BF).reshape(-1) for n, _ in PACKED])
    pad = (-flat.shape[0]) % PACK_ALIGN
    return jnp.pad(flat, (0, pad)).reshape(-1, 1024)


def _unpack_gathered(gathered, shards):
    flat = gathered.reshape(N_DEV, -1)
    full, off = {}, 0
    for name, kind in PACKED:
        shp = shards[name].shape
        n = int(np.prod(shp))
        seg = flat[:, off:off + n].reshape((N_DEV,) + shp)
        off += n
        if kind == "row":
            full[name] = jnp.transpose(seg, (1, 0, 2, 3)).reshape(shp[0], N_DEV * shp[1], shp[2])
        elif kind == "col":
            full[name] = jnp.transpose(seg, (1, 2, 0, 3)).reshape(shp[0], shp[1], N_DEV * shp[2])
        else:
            full[name] = jnp.transpose(seg, (1, 0, 2)).reshape(shp[0], N_DEV * shp[1])
    return full


def _pack_grads(grads, shards):
    parts = []
    for name, kind in PACKED:
        shp = shards[name].shape
        g = grads[name]
        if kind == "row":
            seg = jnp.transpose(g.reshape(shp[0], N_DEV, shp[1], shp[2]), (1, 0, 2, 3))
        elif kind == "col":
            seg = jnp.transpose(g.reshape(shp[0], shp[1], N_DEV, shp[2]), (2, 0, 1, 3))
        else:
            seg = jnp.transpose(g.reshape(shp[0], N_DEV, shp[1]), (1, 0, 2))
        parts.append(seg.reshape(N_DEV, -1).astype(BF))
    flat = jnp.concatenate(parts, axis=1)
    pad = (-flat.shape[1]) % PACK_ALIGN
    return jnp.pad(flat, ((0, 0), (0, pad))).reshape(N_DEV, -1, 1024)


def _unpack_shard_grads(flat2d, shards):
    flat = flat2d.reshape(-1)
    out, off = {}, 0
    for name, _ in PACKED:
        shp = shards[name].shape
        n = int(np.prod(shp))
        out[name] = flat[off:off + n].reshape(shp)
        off += n
    return out


def _pad_heads(w, width):
    k = w.shape[0]
    return jnp.pad(w.reshape(k, HEADS, width), ((0, 0), (0, 0), (0, LANES - width))).reshape(k, HEADS * LANES)


def _unpad_heads(w, width):
    k = w.shape[0]
    return w.reshape(k, HEADS, LANES)[:, :, :width].reshape(k, HEADS * width)


def _rope_tables(positions, scale):
    inv_freq = 10000.0 ** (-jnp.arange(0, 2 * ROPE_HALF, 2, dtype=F32) / (2 * ROPE_HALF))
    ang = positions.astype(F32)[:, None] * inv_freq
    cos, sin = jnp.cos(ang) * scale, jnp.sin(ang) * scale
    t = positions.shape[0]
    z = lambda n: jnp.zeros((t, n), F32)
    cos_p = jnp.concatenate([jnp.full((t, HEAD_DIM), scale, F32), cos, cos, z(32)], axis=1)
    sin_a = jnp.concatenate([z(64), -sin, z(48)], axis=1)
    sin_b = jnp.concatenate([z(80), sin, z(32)], axis=1)
    fwd = (cos_p, sin_a, sin_b)
    bwd = (cos_p, jnp.roll(sin_b, -ROPE_HALF, axis=1), jnp.roll(sin_a, ROPE_HALF, axis=1))
    return fwd, bwd


def _key_rows(cum, nb, seq):
    v = -cum.reshape(nb, seq, LANES)[:, :, :HEADS]
    return jnp.broadcast_to(jnp.transpose(v, (0, 2, 1))[:, :, None, :], (nb, HEADS, 8, seq))


def kernel(x, c, positions, ada_w, ada_b, norm_mix_g, norm_mlp_g, fox_w_in, fox_b_f, fox_w_out, mla_w_dq, mla_q_norm_g, mla_w_uq, mla_w_dkv, mla_kv_norm_g, mla_w_ukv, mla_w_out, mlp_w1, mlp_w2, final_norm_g, loss_target, m_ada_w, m_ada_b, m_norm_mix_g, m_norm_mlp_g, m_fox_w_in, m_fox_b_f, m_fox_w_out, m_mla_w_dq, m_mla_q_norm_g, m_mla_w_uq, m_mla_w_dkv, m_mla_kv_norm_g, m_mla_w_ukv, m_mla_w_out, m_mlp_w1, m_mlp_w2, m_final_norm_g, v_ada_w, v_ada_b, v_norm_mix_g, v_norm_mlp_g, v_fox_w_in, v_fox_b_f, v_fox_w_out, v_mla_w_dq, v_mla_q_norm_g, v_mla_w_uq, v_mla_w_dkv, v_mla_kv_norm_g, v_mla_w_ukv, v_mla_w_out, v_mlp_w1, v_mlp_w2, v_final_norm_g):
    args = dict(locals())
    weights = {n: args[n] for n in WEIGHTS}
    nb, seq, d = x.shape
    t = nb * seq
    depth = ada_w.shape[0]
    dev = 4 * lax.axis_index("x") + 2 * lax.axis_index("y") + lax.axis_index("c")
    n_mod_local = ada_w.shape[2]
    shards = {n: weights[n] for n, _ in PACKED}

    n_qg = mla_q_norm_g.shape[1]
    cond = jnp.concatenate([c, jnp.pad(mla_q_norm_g.reshape(1, -1), ((0, 7), (0, d - 2 * n_qg)))], axis=0)
    w1_rows, w2_rows = mlp_w1.shape[1], mlp_w2.shape[1]
    gathered, w1_all, w2_all, cond_all = _all_gather(
        [_pack_local(shards), mlp_w1.astype(BF).reshape(depth * w1_rows, -1),
         mlp_w2.astype(BF).reshape(depth * w2_rows, -1), cond], "gather_weights")
    full = _unpack_gathered(gathered, shards)
    c_all = cond_all[:, :nb].reshape(N_DEV * nb, d)
    q_gain = jnp.transpose(cond_all[:, nb, :2 * n_qg].reshape(N_DEV, 2, n_qg), (1, 0, 2)).reshape(2, N_DEV * n_qg)
    mod_local = jnp.stack([
        _matmul(c_all, ada_w[i], mode="nn", name="ada_mod", out_dtype=F32, a_act="silu", epi="bias",
                extras=(lax.dynamic_slice_in_dim(ada_b[i], dev * n_mod_local, n_mod_local)[None, :],))
        for i in range(depth)])
    mod_all, = _all_gather([mod_local.reshape(depth * N_DEV * nb, n_mod_local)], "gather_mod")
    mod_all = jnp.transpose(mod_all.reshape(N_DEV, depth, N_DEV * nb, n_mod_local), (1, 2, 0, 3))
    mod_all = mod_all.reshape(depth, N_DEV * nb, N_DEV * n_mod_local)
    mod = lax.dynamic_slice_in_dim(mod_all, dev * nb, nb, axis=1)
    mod = mod.reshape(depth, nb, 6, 1, d)

    pos = positions.reshape(t)
    rope_q, rope_q_t = _rope_tables(pos, MLA_SCALE)
    rope_k, rope_k_t = _rope_tables(pos, 1.0)

    def fox_weights(j):
        w_in = full["fox_w_in"][j]
        wq = _pad_heads(w_in[:, :d] * FOX_SCALE, HEAD_DIM)
        wk = _pad_heads(w_in[:, d:2 * d], HEAD_DIM)
        w_qkv = jnp.concatenate([wq, wk, w_in[:, 2 * d:3 * d]], axis=1)
        w_f = jnp.pad(w_in[:, 3 * d:], ((0, 0), (0, LANES - HEADS)))
        return w_qkv, w_f

    def mla_weights(j):
        w_dkv = full["mla_w_dkv"][j]
        w_down = jnp.concatenate([full["mla_w_dq"][j], w_dkv[:, :128], jnp.zeros((d, 64), BF),
                                  w_dkv[:, 128:160], jnp.zeros((d, 32), BF)], axis=1)
        w_uq = _pad_heads(full["mla_w_uq"][j], 96)
        w_ukv = full["mla_w_ukv"][j].reshape(128, HEADS, 2, HEAD_DIM)
        w_uk = jnp.pad(w_ukv[:, :, 0, :], ((0, 0), (0, 0), (0, 64))).reshape(128, HEADS * LANES)
        w_uv = w_ukv[:, :, 1, :].reshape(128, HEADS * HEAD_DIM)
        place = np.zeros((128, HEADS, LANES), np.float32)
        for i in range(2 * ROPE_HALF):
            place[64 + i, :, 64 + i] = 1.0
        bottom = jnp.concatenate([jnp.asarray(place.reshape(128, HEADS * LANES), BF),
                                  jnp.zeros((128, HEADS * HEAD_DIM), BF)], axis=1)
        w_kv = jnp.concatenate([jnp.concatenate([w_uk, w_uv], axis=1), bottom], axis=0)
        return w_down, w_uq, w_kv

    xs = x.reshape(t, d)
    saved = []
    for i in range(depth):
        j = i // 2
        sh_m, sc_m, g_m, sh_f, sc_f, g_f = (mod[i, :, q] for q in range(6))
        gain_mix = norm_mix_g[i][None, :]
        gain_mlp = norm_mlp_g[i][None, :]
        s = dict(x_in=xs)
        h = _norm_mod(xs, gain_mix, sc_m, sh_m, seq, "norm_mix")
        s["h"] = h
        if i % 2 == 0:
            w_qkv, w_f = fox_weights(j)
            qkv = _matmul(h, w_qkv, mode="nn", name="fox_qkv")
            fg = _matmul(h, w_f, mode="nn", name="fox_gate_logits", out_dtype=F32)
            b_f = jnp.pad(fox_b_f[j], (0, LANES - HEADS))[None, :]
            cum = _fox_gate(fg, b_f, seq, "fox_gate")
            bias = _key_rows(cum, nb, seq)
            o, lse, o32 = _attn_fwd(qkv, 0, qkv, 8, qkv, 32, bias, seq, "fox_attn")
            s.update(qkv=qkv, fg=fg, b_f=b_f, bias=bias, w_qkv=w_qkv, w_f=w_f, o32=o32)
            w_out = full["fox_w_out"][j]
        else:
            w_down, w_uq, w_kv = mla_weights(j)
            down = _matmul(h, w_down, mode="nn", name="mla_down", out_dtype=F32)
            gq, gkv = q_gain[j][None, :], mla_kv_norm_g[j][None, :]
            cq, ckr = _mla_mid(down, gq, gkv, rope_k, "mla_mid")
            q_raw = _matmul(cq, w_uq, mode="nn", name="mla_uq", out_dtype=F32)
            q_rot = _rope(q_raw, rope_q, "mla_rope_q")
            kv = _matmul(ckr, w_kv, mode="nn", name="mla_ukv")
            o, lse = _attn_fwd(q_rot, 0, kv, 0, kv, 16, None, seq, "mla_attn")
            s.update(down=down, gq=gq, gkv=gkv, cq=cq, ckr=ckr, q_rot=q_rot, kv=kv,
                     w_down=w_down, w_uq=w_uq, w_kv=w_kv)
            w_out = full["mla_w_out"][j]
        xs, y = _matmul(o, w_out, mode="nn", name="attn_out", epi="resid_gate", extras=(xs, g_m), seq=seq)
        s.update(o=o, lse=lse, y=y, w_out=w_out, x_mid=xs)
        h2 = _norm_mod(xs, gain_mlp, sc_f, sh_f, seq, "norm_mlp")
        a_pre = _matmul(h2, w1_all, mode="nn", name="mlp_up", layer=("col", i, w1_rows))
        xs, y2 = _matmul(a_pre, w2_all, mode="nn", name="mlp_down", layer=("row", i, w2_rows), a_act="relu2",
                         epi="resid_gate", extras=(xs, g_f), seq=seq)
        s.update(h2=h2, a_pre=a_pre, y2=y2)
        saved.append(s)

    loss_part, dx, dg_final = _loss_head(xs, final_norm_g[None, :], loss_target.reshape(t, d), "loss_head")

    grads = {n: [None] * weights[n].shape[0] for n, _ in PACKED}
    w1_cols, w1_tm = mlp_w1.shape[2], _pick(w1_rows, 512)
    assert _pick(N_DEV * w2_rows, 512) == w2_rows
    g_w1 = jnp.zeros((N_DEV, depth * w1_rows, w1_cols), BF)
    g_w2 = jnp.zeros((N_DEV, depth * w2_rows, d), BF)
    dg_mix, dg_mlp, db_f, dg_kv, dg_q = [None] * depth, [None] * depth, [None] * 2, [None] * 2, [None] * 2
    dmod = [None] * depth
    for i in reversed(range(depth)):
        j = i // 2
        s = saved[i]
        sh_m, sc_m, g_m, sh_f, sc_f, g_f = (mod[i, :, q] for q in range(6))
        dy2, dg_f = _gate_bwd(dx, s["y2"], g_f, seq, "gate_bwd")
        da_pre = _matmul(dy2, w2_all, mode="nt", name="mlp_down_dx", layer=("row", i, w2_rows), epi="mul_drelu",
                         extras=(s["a_pre"],))
        g_w2 = _matmul(s["a_pre"], dy2, mode="tn", name="mlp_down_dw", a_act="relu2",
                       into=(g_w2, (None, w2_rows, d), lambda r, j, k, li=i: (r, li, 0)))
        dh2 = _matmul(da_pre, w1_all, mode="nt", name="mlp_up_dx", layer=("col", i, w1_rows))
        g_w1 = _matmul(s["h2"], da_pre, mode="tn", name="mlp_up_dw", tn=w1_cols,
                       into=(g_w1, (None, w1_tm, w1_cols), lambda r, j, k, li=i: (j, (w1_rows // w1_tm) * li + r, 0)))
        dx, dg_mlp[i], dsc_f, dsh_f = _norm_mod_bwd(dh2, s["x_mid"], norm_mlp_g[i][None, :], sc_f, dx, seq,
                                                    "norm_bwd")
        dy, dg_m = _gate_bwd(dx, s["y"], g_m, seq, "gate_bwd")
        do = _matmul(dy, s["w_out"], mode="nt", name="attn_out_dx")
        dw_out = _matmul(s["o"], dy, mode="tn", name="attn_out_dw", out_dtype=F32)
        if i % 2 == 0:
            qkv = s["qkv"]
            dq, dk, dv, dbias = _attn_bwd(qkv, 0, qkv, 8, qkv, 32, s["bias"], s["o32"], do, s["lse"], seq,
                                          "fox_attn_bwd")
            dqkv = jnp.concatenate([dq, dk, dv], axis=1)
            d_cum = -jnp.transpose(dbias[:, :, 0, :], (0, 2, 1)).reshape(t, HEADS)
            d_cum = jnp.pad(d_cum, ((0, 0), (0, LANES - HEADS)))
            dfg, db = _fox_gate_bwd(d_cum, s["fg"], s["b_f"], seq, "fox_gate_bwd")
            db_f[j] = db
            dh = _matmul(dfg, s["w_f"], mode="nt", name="fox_gate_dx", out_dtype=F32)
            dh = _matmul(dqkv, s["w_qkv"], mode="nt", name="fox_qkv_dx", epi="add", extras=(dh,))
            dw_qkv = _matmul(s["h"], dqkv, mode="tn", name="fox_qkv_dw", out_dtype=F32)
            dw_f = _matmul(s["h"], dfg, mode="tn", name="fox_gate_dw", out_dtype=F32)
            grads["fox_w_in"][j] = jnp.concatenate(
                [_unpad_heads(dw_qkv[:, :2048], HEAD_DIM) * FOX_SCALE, _unpad_heads(dw_qkv[:, 2048:4096], HEAD_DIM),
                 dw_qkv[:, 4096:], dw_f[:, :HEADS]], axis=1)
            grads["fox_w_out"][j] = dw_out
        else:
            kv = s["kv"]
            dq, dk, dv = _attn_bwd(s["q_rot"], 0, kv, 0, kv, 16, None, s["o"], do, s["lse"], seq, "mla_attn_bwd")
            dq_raw = _rope(dq, rope_q_t, "mla_rope_q_bwd")
            dcq = _matmul(dq_raw, s["w_uq"], mode="nt", name="mla_uq_dx")
            dw_uq = _matmul(s["cq"], dq_raw, mode="tn", name="mla_uq_dw", out_dtype=F32)
            dkv = jnp.concatenate([dk, dv], axis=1)
            dckr = _matmul(dkv, s["w_kv"], mode="nt", name="mla_ukv_dx")
            dw_kv = _matmul(s["ckr"], dkv, mode="tn", name="mla_ukv_dw", out_dtype=F32)
            d_down, dgq, dgkv = _mla_mid_bwd(s["down"], dcq, dckr, s["gq"], s["gkv"], rope_k_t, "mla_mid_bwd")
            dg_q[j], dg_kv[j] = dgq, dgkv
            dh = _matmul(d_down, s["w_down"], mode="nt", name="mla_down_dx")
            dw_down = _matmul(s["h"], d_down, mode="tn", name="mla_down_dw", out_dtype=F32)
            grads["mla_w_dq"][j] = dw_down[:, :256]
            grads["mla_w_dkv"][j] = jnp.concatenate([dw_down[:, 256:384], dw_down[:, 448:480]], axis=1)
            grads["mla_w_uq"][j] = _unpad_heads(dw_uq, 96)
            dk_nope = dw_kv[:128, :HEADS * LANES].reshape(128, HEADS, LANES)[:, :, :HEAD_DIM]
            dv_w = dw_kv[:128, HEADS * LANES:].reshape(128, HEADS, HEAD_DIM)
            grads["mla_w_ukv"][j] = jnp.concatenate([dk_nope, dv_w], axis=2).reshape(128, HEADS * LANES)
            grads["mla_w_out"][j] = dw_out
        dx, dg_mix[i], dsc_m, dsh_m = _norm_mod_bwd(dh, s["x_in"], norm_mix_g[i][None, :], sc_m, dx, seq,
                                                    "norm_bwd")
        dmod[i] = jnp.stack([dsh_m, dsc_m, dg_m, dsh_f, dsc_f, dg_f], axis=1).reshape(nb, 6 * d)

    grad_x = dx.reshape(nb, seq, d)

    full_grads = {n: jnp.stack(grads[n]) for n, _ in PACKED if n != "mla_q_norm_g"}
    full_grads["mla_q_norm_g"] = jnp.concatenate(dg_q, axis=0)
    bufs = [_pack_grads(full_grads, shards), g_w1, g_w2]
    from_sibling = _swap_with_sibling(bufs, "grads_to_sibling")
    rel = _relative_blocks()
    partial = [_chip_partial(b, fs, rel, "grads_chip_sum") for b, fs in zip(bufs, from_sibling)]
    from_chips = _swap_with_chips([p[1] for p in partial], "grads_to_chips")
    shard_grads = _unpack_shard_grads(_add_parts(partial[0][0], from_chips[0], "grads_total"), shards)
    done = {"mlp_w1": _total_adamw(partial[1][0], from_chips[1], mlp_w1, m_mlp_w1, v_mlp_w1, "adamw_w1"),
            "mlp_w2": _total_adamw(partial[2][0], from_chips[2], mlp_w2, m_mlp_w2, v_mlp_w2, "adamw_w2")}

    dmod_arr = jnp.stack(dmod)
    small = jnp.concatenate([
        loss_part.reshape(-1), jnp.concatenate(dg_mix, axis=0).reshape(-1), jnp.concatenate(dg_mlp, axis=0).reshape(-1),
        jnp.concatenate(db_f, axis=0).reshape(-1), jnp.concatenate(dg_kv, axis=0).reshape(-1), dg_final.reshape(-1),
        jnp.sum(dmod_arr, axis=1).reshape(-1)])
    n_small = small.shape[0]
    both = jnp.concatenate([small, dmod_arr.reshape(-1)])
    pad = (-both.shape[0]) % (8 * 1024)
    both = jnp.pad(both, (0, pad)).reshape(-1, 1024)
    both_all, = _all_gather([both], "gather_small")
    total = _sum_leading(both_all, "sum_small").reshape(-1)
    off = 0

    def take(n, shape):
        nonlocal off
        out = total[off:off + n].reshape(shape)
        off += n
        return out

    loss = take(LANES, (LANES,))[0]
    g_small = dict(
        norm_mix_g=take(depth * d, (depth, d)), norm_mlp_g=take(depth * d, (depth, d)),
        fox_b_f=take(2 * LANES, (2, LANES))[:, :HEADS], mla_kv_norm_g=take(2 * 128, (2, 128)),
        final_norm_g=take(d, (d,)), ada_b=take(depth * 6 * d, (depth, 6 * d)))
    dmod_all = both_all.reshape(N_DEV, -1)[:, n_small:n_small + depth * nb * 6 * d]
    dmod_all = jnp.transpose(dmod_all.reshape(N_DEV, depth, nb, 6 * d), (1, 0, 2, 3)).reshape(depth, N_DEV * nb, 6 * d)
    dmod_cols = lax.dynamic_slice_in_dim(dmod_all, dev * n_mod_local, n_mod_local, axis=2)
    g_ada_w = jnp.stack([_matmul(c_all, dmod_cols[i], mode="tn", name="ada_dw", out_dtype=F32, a_act="silu")
                         for i in range(depth)])

    all_grads = dict(shard_grads)
    all_grads.update(g_small)
    all_grads["ada_w"] = g_ada_w

    deltas, new_m, new_v = {}, {}, {}
    for n in WEIGHTS:
        if n in done:
            all_grads[n], deltas[n], new_m[n], new_v[n] = done[n]
        else:
            deltas[n], new_m[n], new_v[n] = _adamw(weights[n], all_grads[n], args["m_" + n], args["v_" + n], "adamw")

    return (loss, grad_x, *[all_grads[n] for n in WEIGHTS], *[deltas[n] for n in WEIGHTS],
            *[new_m[n] for n in WEIGHTS], *[new_v[n] for n in WEIGHTS])
```

```python
import functools
import math

import jax
import jax.numpy as jnp
import numpy as np
from jax import lax
from jax.experimental import pallas as pl
from jax.experimental.pallas import tpu as pltpu

F32 = jnp.float32
BF = jnp.bfloat16

N_DEV = 8
HEADS = 16
HEAD_PAIRS = HEADS // 2
HEAD_DIM = 64
LANES = 128
ROPE_HALF = 16
NORM_EPS = 1e-6
MLA_SCALE = 96.0 ** -0.5
FOX_SCALE = 0.125
ATTN_BLOCK = 256
ROW_BLOCK = 256
VMEM_LIMIT = 56 * 1024 * 1024
MESH = pl.DeviceIdType.MESH

ADAM_LR = 0.001
ADAM_B1 = 0.9
ADAM_B2 = 0.999
ADAM_EPS = 1e-08
ADAM_WD = 0.01
ADAM_STEP = 10

MIXER_WEIGHTS = ("fox_w_in", "fox_w_out", "mla_w_dq", "mla_w_uq", "mla_w_dkv", "mla_w_ukv", "mla_w_out")
WEIGHTS = ("ada_w", "ada_b", "norm_mix_g", "norm_mlp_g", "fox_w_in", "fox_b_f", "fox_w_out", "mla_w_dq",
           "mla_q_norm_g", "mla_w_uq", "mla_w_dkv", "mla_kv_norm_g", "mla_w_ukv", "mla_w_out", "mlp_w1",
           "mlp_w2", "final_norm_g")


def _params(sem=None):
    return pltpu.CompilerParams(dimension_semantics=sem, vmem_limit_bytes=VMEM_LIMIT)


def _pick(n, target):
    if n <= target:
        return n
    for t in range(target, 127, -128):
        if n % t == 0:
            return t
    return n


def _rows(n, target=512):
    if n <= target:
        return n
    for t in range(target, 7, -8):
        if n % t == 0:
            return t
    return n


def _place():
    x, y, c = lax.axis_index("x"), lax.axis_index("y"), lax.axis_index("c")
    return x, y, c


def _adamw_math(w, g, m, v):
    nm = ADAM_B1 * m + (1.0 - ADAM_B1) * g
    nv = ADAM_B2 * v + (1.0 - ADAM_B2) * (g * g)
    m_hat = nm * (1.0 / (1.0 - ADAM_B1 ** ADAM_STEP))
    v_hat = nv * (1.0 / (1.0 - ADAM_B2 ** ADAM_STEP))
    return -ADAM_LR * (m_hat / (jnp.sqrt(v_hat) + ADAM_EPS) + ADAM_WD * w), nm, nv


def _all_gather(blocks, name):
    n = len(blocks)

    def body(*refs):
        x_refs, out_refs = refs[:n], refs[n:2 * n]
        send_sems, recv_sems, local_sems = refs[2 * n:]
        x, y, c = _place()
        me, sibling = (x, y, c), (x, y, 1 - c)
        chips = [(1 - x, y), (x, 1 - y), (1 - x, 1 - y)]

        def slot(a, px, py, pc):
            return out_refs[a].at[4 * px + 2 * py + pc]

        def copy(a, k, blk, to, src=None):
            return pltpu.make_async_remote_copy(
                src_ref=slot(a, *blk) if src is None else src, dst_ref=slot(a, *blk),
                send_sem=send_sems.at[7 * a + k], recv_sem=recv_sems.at[7 * a + k], device_id=to, device_id_type=MESH)

        mine = [pltpu.make_async_copy(x_refs[a], slot(a, *me), local_sems.at[a]) for a in range(n)]
        for cp in mine:
            cp.start()
        first = []
        for j, chip in enumerate(chips):
            first += [copy(a, 1 + j, me, (*chip, c), src=x_refs[a]) for a in range(n)]
        first += [copy(a, 0, me, sibling, src=x_refs[a]) for a in range(n)]
        for cp in first:
            cp.start()
        passed = []
        for j, chip in enumerate(chips):
            for a in range(n):
                copy(a, 1 + j, (*chip, c), me).wait_recv()
                passed.append(copy(a, 4 + j, (*chip, c), sibling))
                passed[-1].start()
        for a in range(n):
            copy(a, 0, sibling, me).wait_recv()
        for j, chip in enumerate(chips):
            for a in range(n):
                copy(a, 4 + j, (*chip, 1 - c), me).wait_recv()
        for cp in first + passed:
            cp.wait_send()
        for cp in mine:
            cp.wait()

    any_spec = pl.BlockSpec(memory_space=pl.ANY)
    return pl.pallas_call(
        body, name=name,
        out_shape=[jax.ShapeDtypeStruct((N_DEV,) + b.shape, b.dtype) for b in blocks],
        in_specs=[any_spec] * n, out_specs=[any_spec] * n,
        scratch_shapes=[pltpu.SemaphoreType.DMA((7 * n,)), pltpu.SemaphoreType.DMA((7 * n,)),
                        pltpu.SemaphoreType.DMA((n,))],
    )(*blocks)


def _swap_with_sibling(bufs, name):
    n = len(bufs)

    def body(*refs):
        src, dst = refs[:n], refs[n:2 * n]
        send_sems, recv_sems = refs[2 * n:]
        x, y, c = _place()
        cps = []
        for a in range(n):
            for k in range(4):
                cps.append(pltpu.make_async_remote_copy(
                    src_ref=src[a].at[2 * k + (1 - c)], dst_ref=dst[a].at[k],
                    send_sem=send_sems.at[4 * a + k], recv_sem=recv_sems.at[4 * a + k],
                    device_id=(x, y, 1 - c), device_id_type=MESH))
        for cp in cps:
            cp.start()
        for cp in cps:
            cp.wait()

    any_spec = pl.BlockSpec(memory_space=pl.ANY)
    return pl.pallas_call(
        body, name=name,
        out_shape=[jax.ShapeDtypeStruct((4,) + b.shape[1:], b.dtype) for b in bufs],
        in_specs=[any_spec] * n, out_specs=[any_spec] * n,
        scratch_shapes=[pltpu.SemaphoreType.DMA((4 * n,)), pltpu.SemaphoreType.DMA((4 * n,))],
    )(*bufs)


def _swap_with_chips(parts, name):
    n = len(parts)

    def body(*refs):
        src, dst = refs[:n], refs[n:2 * n]
        send_sems, recv_sems = refs[2 * n:]
        x, y, c = _place()
        chips = [(1 - x, y), (x, 1 - y), (1 - x, 1 - y)]
        cps = []
        for k, chip in enumerate(chips):
            for a in range(n):
                cps.append(pltpu.make_async_remote_copy(
                    src_ref=src[a].at[k], dst_ref=dst[a].at[k],
                    send_sem=send_sems.at[3 * a + k], recv_sem=recv_sems.at[3 * a + k],
                    device_id=(*chip, c), device_id_type=MESH))
        for cp in cps:
            cp.start()
        for cp in cps:
            cp.wait()

    any_spec = pl.BlockSpec(memory_space=pl.ANY)
    return pl.pallas_call(
        body, name=name,
        out_shape=[jax.ShapeDtypeStruct(p.shape, p.dtype) for p in parts],
        in_specs=[any_spec] * n, out_specs=[any_spec] * n,
        scratch_shapes=[pltpu.SemaphoreType.DMA((3 * n,)), pltpu.SemaphoreType.DMA((3 * n,))],
    )(*parts)


def _relative_blocks():
    x, y, c = _place()
    flips = ((0, 0), (1, 0), (0, 1), (1, 1))
    mine = [4 * (x ^ fx) + 2 * (y ^ fy) + c for fx, fy in flips]
    sib = [2 * (x ^ fx) + (y ^ fy) for fx, fy in flips]
    return jnp.stack(mine + sib).astype(jnp.int32)


def _chip_partial(buf, from_sibling, rel, name):
    _, r, cdim = buf.shape
    tr = _rows(r, 256)

    def body(rel_ref, m0, m1, m2, m3, s0, s1, s2, s3, own_ref, parts_ref):
        del rel_ref
        own_ref[...] = m0[...].astype(F32) + s0[...].astype(F32)
        for k, (m, s) in enumerate(((m1, s1), (m2, s2), (m3, s3))):
            parts_ref[k] = (m[...].astype(F32) + s[...].astype(F32)).astype(parts_ref.dtype)

    def pick(k):
        return pl.BlockSpec((None, tr, cdim), lambda i, rel_ref: (rel_ref[k], i, 0))

    return pl.pallas_call(
        body, name=name,
        grid_spec=pltpu.PrefetchScalarGridSpec(
            num_scalar_prefetch=1, grid=(r // tr,),
            in_specs=[pick(k) for k in range(8)],
            out_specs=(pl.BlockSpec((tr, cdim), lambda i, rel_ref: (i, 0)),
                       pl.BlockSpec((3, tr, cdim), lambda i, rel_ref: (0, i, 0)))),
        out_shape=(jax.ShapeDtypeStruct((r, cdim), F32), jax.ShapeDtypeStruct((3, r, cdim), buf.dtype)),
        compiler_params=_params(("parallel",)),
    )(rel, buf, buf, buf, buf, from_sibling, from_sibling, from_sibling, from_sibling)


def _total_adamw(own, parts, w, m, v, name):
    r, cdim = own.shape
    tr = _rows(r, 256)

    def body(own_ref, parts_ref, w_ref, m_ref, v_ref, g_ref, d_ref, nm_ref, nv_ref):
        g = own_ref[...]
        for k in range(3):
            g = g + parts_ref[k].astype(F32)
        g_ref[...] = g
        d_ref[...], nm_ref[...], nv_ref[...] = _adamw_math(w_ref[...], g, m_ref[...], v_ref[...])

    blk = pl.BlockSpec((tr, cdim), lambda i: (i, 0))
    sds = jax.ShapeDtypeStruct((r, cdim), F32)
    outs = pl.pallas_call(
        body, name=name, grid=(r // tr,),
        out_shape=(sds,) * 4,
        in_specs=[blk, pl.BlockSpec((3, tr, cdim), lambda i: (0, i, 0)), blk, blk, blk], out_specs=(blk,) * 4,
        compiler_params=_params(("parallel",)),
    )(own, parts, *[a.reshape(r, cdim) for a in (w, m, v)])
    return tuple(a.reshape(w.shape) for a in outs)


def _add_parts(own, parts, name):
    r, cdim = own.shape
    tr = _rows(r, 512)

    def body(own_ref, parts_ref, out_ref):
        acc = own_ref[...]
        for k in range(parts_ref.shape[0]):
            acc = acc + parts_ref[k].astype(F32)
        out_ref[...] = acc

    return pl.pallas_call(
        body, name=name, grid=(r // tr,),
        out_shape=jax.ShapeDtypeStruct((r, cdim), F32),
        in_specs=[pl.BlockSpec((tr, cdim), lambda i: (i, 0)),
                  pl.BlockSpec((parts.shape[0], tr, cdim), lambda i: (0, i, 0))],
        out_specs=pl.BlockSpec((tr, cdim), lambda i: (i, 0)),
        compiler_params=_params(("parallel",)),
    )(own, parts)


def _sum_leading(stack, name):
    n, r, cdim = stack.shape
    tr = _rows(r, 512)

    def body(in_ref, out_ref):
        acc = in_ref[0]
        for k in range(1, n):
            acc = acc + in_ref[k]
        out_ref[...] = acc

    return pl.pallas_call(
        body, name=name, grid=(r // tr,),
        out_shape=jax.ShapeDtypeStruct((r, cdim), F32),
        in_specs=[pl.BlockSpec((n, tr, cdim), lambda i: (0, i, 0))],
        out_specs=pl.BlockSpec((tr, cdim), lambda i: (i, 0)),
        compiler_params=_params(("parallel",)),
    )(stack)


_DIMS = {"nn": (((1,), (0,)), ((), ())), "nt": (((1,), (1,)), ((), ())), "tn": (((0,), (0,)), ((), ()))}


def _stack_spec(shape, mode, layer):
    cut, l, rows = layer
    cols = shape[2]
    by_n = pl.BlockSpec((None, rows, cols), lambda i, j, k: (j, l, 0))
    by_k = pl.BlockSpec((None, rows, cols), lambda i, j, k: (k, l, 0))
    if cut == "col":
        return (by_n, N_DEV * cols, cols, rows) if mode == "nn" else (by_k, rows, rows, cols)
    return (by_k, cols, cols, rows) if mode == "nn" else (by_n, N_DEV * rows, rows, cols)


def _matmul(a, b, *, mode, name, out_dtype=BF, a_act=None, epi=None, extras=(), seq=None, layer=None, tm=None,
            tn=None, into=None):
    if mode == "tn":
        kdim, m = a.shape
    else:
        m, kdim = a.shape
    tm = _pick(m, 512 if epi == "resid_gate" else 1024) if tm is None else tm
    tk = _pick(kdim, 2048 if mode == "tn" else 1024)
    b_spec = None
    if layer is not None:
        b_spec, n, tn, tk = _stack_spec(b.shape, mode, layer)
    else:
        n = b.shape[0] if mode == "nt" else b.shape[1]
        tn = _pick(n, 1024) if tn is None else tn
    nk = kdim // tk
    a_spec = (pl.BlockSpec((tk, tm), lambda i, j, k: (k, i)) if mode == "tn"
              else pl.BlockSpec((tm, tk), lambda i, j, k: (i, k)))
    if b_spec is None:
        b_spec = (pl.BlockSpec((tn, tk), lambda i, j, k: (j, k)) if mode == "nt"
                  else pl.BlockSpec((tk, tn), lambda i, j, k: (k, j)))
    tile = pl.BlockSpec((tm, tn), lambda i, j, k: (i, j))
    in_specs, out_specs = [a_spec, b_spec], [tile]
    out_shape = [jax.ShapeDtypeStruct((m, n), out_dtype)]
    if epi == "resid_gate":
        in_specs += [tile, pl.BlockSpec((None, 1, tn), lambda i, j, k: ((i * tm) // seq, 0, j))]
        out_shape = [jax.ShapeDtypeStruct((m, n), F32), jax.ShapeDtypeStruct((m, n), BF)]
        out_specs = [tile, tile]
    elif epi in ("mul_drelu", "add"):
        in_specs += [tile]
    elif epi == "bias":
        in_specs += [pl.BlockSpec((1, tn), lambda i, j, k: (0, j))]
    n_extra, n_out = len(in_specs) - 2, len(out_specs)
    aliases, n_kept = {}, 0
    if into is not None:
        buffer, block, index_map = into
        out_dtype = buffer.dtype
        in_specs.append(pl.BlockSpec(memory_space=pl.ANY))
        extras = tuple(extras) + (buffer,)
        aliases, n_kept = {len(in_specs) - 1: 0}, 1
        out_shape = [jax.ShapeDtypeStruct(buffer.shape, buffer.dtype)]
        out_specs = [pl.BlockSpec(block, index_map)]
    dims = _DIMS[mode]

    def body(*refs):
        a_ref, b_ref = refs[:2]
        ex = refs[2:2 + n_extra]
        outs = refs[2 + n_extra + n_kept:2 + n_extra + n_kept + n_out]
        av = a_ref[...]
        if a_act == "relu2":
            t = jnp.maximum(av.astype(F32), 0.0)
            av = t * t
        elif a_act == "silu":
            t = av.astype(F32)
            av = t / (1.0 + jnp.exp(-t))
        part = lax.dot_general(av.astype(BF), b_ref[...].astype(BF), dims, preferred_element_type=F32)

        def finish(acc):
            if epi == "resid_gate":
                outs[0][...] = ex[0][...] + ex[1][...] * acc
                outs[1][...] = acc.astype(BF)
            elif epi == "mul_drelu":
                outs[0][...] = (acc * (2.0 * jnp.maximum(ex[0][...].astype(F32), 0.0))).astype(out_dtype)
            elif epi == "add":
                outs[0][...] = (acc + ex[0][...].astype(F32)).astype(out_dtype)
            elif epi == "bias":
                outs[0][...] = (acc + ex[0][...]).astype(out_dtype)
            else:
                outs[0][...] = acc.astype(out_dtype)

        if nk == 1:
            finish(part)
        else:
            acc_ref = refs[-1]
            k = pl.program_id(2)

            @pl.when(k == 0)
            def _():
                acc_ref[...] = part

            @pl.when(k > 0)
            def _():
                acc_ref[...] += part

            @pl.when(k == nk - 1)
            def _():
                finish(acc_ref[...])

    res = pl.pallas_call(
        body, name=name, grid=(m // tm, n // tn, nk),
        out_shape=out_shape, in_specs=in_specs, out_specs=out_specs,
        scratch_shapes=[pltpu.VMEM((tm, tn), F32)] if nk > 1 else [],
        input_output_aliases=aliases,
        compiler_params=_params(("parallel", "parallel", "arbitrary")),
    )(a, b, *extras)
    return res if n_out > 1 else res[0]


def _norm_mod(x, gain, scale, shift, seq, name):
    t, w = x.shape
    tr = ROW_BLOCK

    def body(x_ref, g_ref, sc_ref, sh_ref, out_ref):
        xv = x_ref[...]
        rstd = lax.rsqrt(jnp.mean(xv * xv, axis=-1, keepdims=True) + NORM_EPS)
        y = xv * rstd * g_ref[...]
        out_ref[...] = (y * (1.0 + sc_ref[...]) + sh_ref[...]).astype(BF)

    per_b = pl.BlockSpec((None, 1, w), lambda i: ((i * tr) // seq, 0, 0))
    return pl.pallas_call(
        body, name=name, grid=(t // tr,),
        out_shape=jax.ShapeDtypeStruct((t, w), BF),
        in_specs=[pl.BlockSpec((tr, w), lambda i: (i, 0)), pl.BlockSpec((1, w), lambda i: (0, 0)), per_b, per_b],
        out_specs=pl.BlockSpec((tr, w), lambda i: (i, 0)),
        compiler_params=_params(("parallel",)),
    )(x, gain, scale, shift)


def _norm_mod_bwd(dh, x, gain, scale, dres, seq, name):
    t, w = x.shape
    tr = ROW_BLOCK
    steps_per_seq = seq // tr
    nb = t // seq

    def body(dh_ref, x_ref, g_ref, sc_ref, dres_ref, dx_ref, dg_ref, dsc_ref, dsh_ref):
        i = pl.program_id(0)
        xv = x_ref[...]
        dhv = dh_ref[...].astype(F32)
        rstd = lax.rsqrt(jnp.mean(xv * xv, axis=-1, keepdims=True) + NORM_EPS)
        xhat = xv * rstd
        one_sc = 1.0 + sc_ref[...]
        g = g_ref[...]
        dxhat = dhv * (g * one_sc)
        proj = jnp.mean(dxhat * xhat, axis=-1, keepdims=True)
        dx_ref[...] = dres_ref[...] + rstd * (dxhat - xhat * proj)
        dhx = dhv * xhat
        dg_part = jnp.sum(dhx * one_sc, axis=0, keepdims=True)
        dsc_part = jnp.sum(dhx * g, axis=0, keepdims=True)
        dsh_part = jnp.sum(dhv, axis=0, keepdims=True)

        @pl.when(i == 0)
        def _():
            dg_ref[...] = dg_part

        @pl.when(i > 0)
        def _():
            dg_ref[...] += dg_part

        @pl.when(i % steps_per_seq == 0)
        def _():
            dsc_ref[...] = dsc_part
            dsh_ref[...] = dsh_part

        @pl.when(i % steps_per_seq != 0)
        def _():
            dsc_ref[...] += dsc_part
            dsh_ref[...] += dsh_part

    row = pl.BlockSpec((tr, w), lambda i: (i, 0))
    per_b = pl.BlockSpec((None, 1, w), lambda i: ((i * tr) // seq, 0, 0))
    vec = pl.BlockSpec((1, w), lambda i: (0, 0))
    return pl.pallas_call(
        body, name=name, grid=(t // tr,),
        out_shape=(jax.ShapeDtypeStruct((t, w), F32), jax.ShapeDtypeStruct((1, w), F32),
                   jax.ShapeDtypeStruct((nb, 1, w), F32), jax.ShapeDtypeStruct((nb, 1, w), F32)),
        in_specs=[row, row, vec, per_b, row],
        out_specs=(row, vec, per_b, per_b),
        compiler_params=_params(("arbitrary",)),
    )(dh, x, gain, scale, dres)


def _gate_bwd(dx, y, gate, seq, name):
    t, w = dx.shape
    tr = ROW_BLOCK
    steps_per_seq = seq // tr
    nb = t // seq

    def body(dx_ref, y_ref, g_ref, dy_ref, dg_ref):
        i = pl.program_id(0)
        dxv = dx_ref[...]
        dy_ref[...] = (dxv * g_ref[...]).astype(BF)
        part = jnp.sum(dxv * y_ref[...].astype(F32), axis=0, keepdims=True)

        @pl.when(i % steps_per_seq == 0)
        def _():
            dg_ref[...] = part

        @pl.when(i % steps_per_seq != 0)
        def _():
            dg_ref[...] += part

    row = pl.BlockSpec((tr, w), lambda i: (i, 0))
    per_b = pl.BlockSpec((None, 1, w), lambda i: ((i * tr) // seq, 0, 0))
    return pl.pallas_call(
        body, name=name, grid=(t // tr,),
        out_shape=(jax.ShapeDtypeStruct((t, w), BF), jax.ShapeDtypeStruct((nb, 1, w), F32)),
        in_specs=[row, row, per_b], out_specs=(row, per_b),
        compiler_params=_params(("arbitrary",)),
    )(dx, y, gate)


def _loss_head(x, gain, target, name):
    t, w = x.shape
    tr = ROW_BLOCK

    def body(x_ref, g_ref, t_ref, loss_ref, dx_ref, dg_ref):
        i = pl.program_id(0)
        xv = x_ref[...]
        g = g_ref[...]
        rstd = lax.rsqrt(jnp.mean(xv * xv, axis=-1, keepdims=True) + NORM_EPS)
        xhat = xv * rstd
        err = xhat * g - t_ref[...]
        row_loss = jnp.sum(err * err, axis=-1, keepdims=True) * (0.5 / w)
        loss_part = jnp.broadcast_to(jnp.sum(row_loss, axis=0, keepdims=True), (1, LANES))
        dy = err * (1.0 / w)
        dg_part = jnp.sum(dy * xhat, axis=0, keepdims=True)
        dxhat = dy * g
        proj = jnp.mean(dxhat * xhat, axis=-1, keepdims=True)
        dx_ref[...] = rstd * (dxhat - xhat * proj)

        @pl.when(i == 0)
        def _():
            loss_ref[...] = loss_part
            dg_ref[...] = dg_part

        @pl.when(i > 0)
        def _():
            loss_ref[...] += loss_part
            dg_ref[...] += dg_part

    row = pl.BlockSpec((tr, w), lambda i: (i, 0))
    vec = pl.BlockSpec((1, w), lambda i: (0, 0))
    return pl.pallas_call(
        body, name=name, grid=(t // tr,),
        out_shape=(jax.ShapeDtypeStruct((1, LANES), F32), jax.ShapeDtypeStruct((t, w), F32),
                   jax.ShapeDtypeStruct((1, w), F32)),
        in_specs=[row, vec, row],
        out_specs=(pl.BlockSpec((1, LANES), lambda i: (0, 0)), row, vec),
        compiler_params=_params(("arbitrary",)),
    )(x, gain, target)


def _rope_group(xg, cos_p, sin_a, sin_b):
    return (xg * cos_p + pltpu.roll(xg, LANES - ROPE_HALF, axis=1) * sin_a
            + pltpu.roll(xg, ROPE_HALF, axis=1) * sin_b)


def _rope(x, tables, name, out_dtype=BF):
    t, w = x.shape
    tr = ROW_BLOCK
    groups = w // LANES

    def body(x_ref, c_ref, a_ref, b_ref, out_ref):
        cos_p, sin_a, sin_b = c_ref[...], a_ref[...], b_ref[...]
        for g in range(groups):
            sl = slice(g * LANES, (g + 1) * LANES)
            out_ref[:, sl] = _rope_group(x_ref[:, sl].astype(F32), cos_p, sin_a, sin_b).astype(out_dtype)

    row = pl.BlockSpec((tr, w), lambda i: (i, 0))
    tab = pl.BlockSpec((tr, LANES), lambda i: (i, 0))
    return pl.pallas_call(
        body, name=name, grid=(t // tr,),
        out_shape=jax.ShapeDtypeStruct((t, w), out_dtype),
        in_specs=[row, tab, tab, tab], out_specs=row,
        compiler_params=_params(("parallel",)),
    )(x, *tables)


def _mla_mid(down, gq, gkv, tables, name):
    t = down.shape[0]
    tr = ROW_BLOCK

    def body(d_ref, gq_ref, gkv_ref, c_ref, a_ref, b_ref, cq_ref, ckr_ref):
        q = d_ref[:, 0:256]
        cq_ref[...] = (q * lax.rsqrt(jnp.mean(q * q, axis=-1, keepdims=True) + NORM_EPS) * gq_ref[...]).astype(BF)
        kv = d_ref[:, 256:384]
        ckr_ref[:, 0:128] = (kv * lax.rsqrt(jnp.mean(kv * kv, axis=-1, keepdims=True) + NORM_EPS)
                             * gkv_ref[...]).astype(BF)
        ckr_ref[:, 128:256] = _rope_group(d_ref[:, 384:512], c_ref[...], a_ref[...], b_ref[...]).astype(BF)

    tab = pl.BlockSpec((tr, LANES), lambda i: (i, 0))
    return pl.pallas_call(
        body, name=name, grid=(t // tr,),
        out_shape=(jax.ShapeDtypeStruct((t, 256), BF), jax.ShapeDtypeStruct((t, 256), BF)),
        in_specs=[pl.BlockSpec((tr, 512), lambda i: (i, 0)), pl.BlockSpec((1, 256), lambda i: (0, 0)),
                  pl.BlockSpec((1, 128), lambda i: (0, 0)), tab, tab, tab],
        out_specs=(pl.BlockSpec((tr, 256), lambda i: (i, 0)), pl.BlockSpec((tr, 256), lambda i: (i, 0))),
        compiler_params=_params(("parallel",)),
    )(down, gq, gkv, *tables)


def _mla_mid_bwd(down, dcq, dckr, gq, gkv, tables_t, name):
    t = down.shape[0]
    tr = ROW_BLOCK

    def norm_bwd(xv, g, dy):
        rstd = lax.rsqrt(jnp.mean(xv * xv, axis=-1, keepdims=True) + NORM_EPS)
        xhat = xv * rstd
        dxhat = dy * g
        proj = jnp.mean(dxhat * xhat, axis=-1, keepdims=True)
        return rstd * (dxhat - xhat * proj), jnp.sum(dy * xhat, axis=0, keepdims=True)

    def body(d_ref, dcq_ref, dckr_ref, gq_ref, gkv_ref, c_ref, a_ref, b_ref, dd_ref, dgq_ref, dgkv_ref):
        i = pl.program_id(0)
        dq, dgq_part = norm_bwd(d_ref[:, 0:256], gq_ref[...], dcq_ref[...].astype(F32))
        dd_ref[:, 0:256] = dq.astype(BF)
        dkv, dgkv_part = norm_bwd(d_ref[:, 256:384], gkv_ref[...], dckr_ref[:, 0:128].astype(F32))
        dd_ref[:, 256:384] = dkv.astype(BF)
        dd_ref[:, 384:512] = _rope_group(dckr_ref[:, 128:256].astype(F32), c_ref[...], a_ref[...],
                                         b_ref[...]).astype(BF)

        @pl.when(i == 0)
        def _():
            dgq_ref[...] = dgq_part
            dgkv_ref[...] = dgkv_part

        @pl.when(i > 0)
        def _():
            dgq_ref[...] += dgq_part
            dgkv_ref[...] += dgkv_part

    tab = pl.BlockSpec((tr, LANES), lambda i: (i, 0))
    r256 = pl.BlockSpec((tr, 256), lambda i: (i, 0))
    return pl.pallas_call(
        body, name=name, grid=(t // tr,),
        out_shape=(jax.ShapeDtypeStruct((t, 512), BF), jax.ShapeDtypeStruct((1, 256), F32),
                   jax.ShapeDtypeStruct((1, 128), F32)),
        in_specs=[pl.BlockSpec((tr, 512), lambda i: (i, 0)), r256, r256, pl.BlockSpec((1, 256), lambda i: (0, 0)),
                  pl.BlockSpec((1, 128), lambda i: (0, 0)), tab, tab, tab],
        out_specs=(pl.BlockSpec((tr, 512), lambda i: (i, 0)), pl.BlockSpec((1, 256), lambda i: (0, 0)),
                   pl.BlockSpec((1, 128), lambda i: (0, 0))),
        compiler_params=_params(("arbitrary",)),
    )(down, dcq, dckr, gq, gkv, *tables_t)


def _scan_rows(x, reverse):
    s = x.shape[0]
    row = lax.broadcasted_iota(jnp.int32, x.shape, 0)
    step = 1
    while step < s:
        if reverse:
            x = x + jnp.where(row < s - step, pltpu.roll(x, s - step, axis=0), 0.0)
        else:
            x = x + jnp.where(row >= step, pltpu.roll(x, step, axis=0), 0.0)
        step *= 2
    return x


def _fox_gate(fg, b_f, seq, name):
    t = fg.shape[0]

    def body(fg_ref, b_ref, out_ref):
        z = fg_ref[...] + b_ref[...]
        log_f = jnp.minimum(z, 0.0) - jnp.log(1.0 + jnp.exp(-jnp.abs(z)))
        out_ref[...] = _scan_rows(log_f, reverse=False)

    blk = pl.BlockSpec((seq, LANES), lambda b: (b, 0))
    return pl.pallas_call(
        body, name=name, grid=(t // seq,),
        out_shape=jax.ShapeDtypeStruct((t, LANES), F32),
        in_specs=[blk, pl.BlockSpec((1, LANES), lambda b: (0, 0))], out_specs=blk,
        compiler_params=_params(("parallel",)),
    )(fg, b_f)


def _fox_gate_bwd(d_cum, fg, b_f, seq, name):
    t = fg.shape[0]

    def body(dc_ref, fg_ref, b_ref, dfg_ref, db_ref):
        b = pl.program_id(0)
        z = fg_ref[...] + b_ref[...]
        d_log_f = _scan_rows(dc_ref[...], reverse=True)
        dz = d_log_f / (1.0 + jnp.exp(z))
        dfg_ref[...] = dz
        part = jnp.sum(dz, axis=0, keepdims=True)

        @pl.when(b == 0)
        def _():
            db_ref[...] = part

        @pl.when(b > 0)
        def _():
            db_ref[...] += part

    blk = pl.BlockSpec((seq, LANES), lambda b: (b, 0))
    vec = pl.BlockSpec((1, LANES), lambda b: (0, 0))
    return pl.pallas_call(
        body, name=name, grid=(t // seq,),
        out_shape=(jax.ShapeDtypeStruct((t, LANES), F32), jax.ShapeDtypeStruct((1, LANES), F32)),
        in_specs=[blk, blk, vec], out_specs=(blk, vec),
        compiler_params=_params(("arbitrary",)),
    )(d_cum, fg, b_f)


def _head_masks():
    lane = lax.broadcasted_iota(jnp.int32, (1, LANES), 1)
    return lane < HEAD_DIM, lane >= HEAD_DIM


def _attn_fwd(q_arr, q_off, k_arr, k_off, v_arr, v_off, bias, seq, name):
    t = q_arr.shape[0]
    nb = t // seq
    blk = min(ATTN_BLOCK, seq)
    nq = seq // blk
    has_bias = bias is not None

    def body(*refs):
        if has_bias:
            q_ref, k_ref, v_ref, bias_ref, o_ref, lse_ref, o32_ref = refs
        else:
            q_ref, k_ref, v_ref, o_ref, lse_ref = refs
        lo, hi = _head_masks()
        rows = lax.broadcasted_iota(jnp.int32, (blk, blk), 0)
        cols = lax.broadcasted_iota(jnp.int32, (blk, blk), 1)
        causal = cols <= rows

        def q_block(iq, _):
            q0 = pl.multiple_of(iq * blk, blk)
            qs = [q_ref[pl.ds(q0, blk), h * LANES:(h + 1) * LANES] for h in range(2)]

            def kv_block(j, carry, diag):
                k0 = pl.multiple_of(j * blk, blk)
                vv = v_ref[pl.ds(k0, blk), :]
                vs = [jnp.where(lo, vv, jnp.zeros_like(vv)), jnp.where(hi, vv, jnp.zeros_like(vv))]
                acc = carry[0]
                new = []
                pv = None
                alphas = []
                for h in range(2):
                    m, l = carry[1 + 2 * h], carry[2 + 2 * h]
                    kk = k_ref[pl.ds(k0, blk), h * LANES:(h + 1) * LANES]
                    s = lax.dot_general(qs[h], kk, _DIMS["nt"], preferred_element_type=F32)
                    if has_bias:
                        s = s + bias_ref[h, 0:1, pl.ds(k0, blk)]
                    if diag:
                        s = jnp.where(causal, s, -jnp.inf)
                    m_new = jnp.maximum(m, jnp.max(s, axis=-1, keepdims=True))
                    p = jnp.exp(s - m_new)
                    alpha = jnp.exp(m - m_new)
                    l_new = alpha * l + jnp.sum(p, axis=-1, keepdims=True)
                    p_hi = p.astype(BF)
                    d = jnp.dot(p_hi, vs[h], preferred_element_type=F32)
                    if has_bias:
                        p_lo = (p - p_hi.astype(F32)).astype(BF)
                        d = d + jnp.dot(p_lo, vs[h], preferred_element_type=F32)
                    pv = d if pv is None else pv + d
                    alphas.append(alpha)
                    new += [m_new, l_new]
                acc = acc * jnp.where(lo, alphas[0], alphas[1]) + pv
                return (acc, *new)

            init = (jnp.zeros((blk, LANES), F32),
                    jnp.full((blk, 1), -jnp.inf, F32), jnp.zeros((blk, 1), F32),
                    jnp.full((blk, 1), -jnp.inf, F32), jnp.zeros((blk, 1), F32))
            carry = lax.fori_loop(0, iq, functools.partial(kv_block, diag=False), init)
            acc, m0, l0, m1, l1 = kv_block(iq, carry, diag=True)
            o_val = acc / jnp.where(lo, l0, l1)
            o_ref[pl.ds(q0, blk), :] = o_val.astype(BF)
            if has_bias:
                o32_ref[pl.ds(q0, blk), :] = o_val
            lse_ref[pl.ds(q0, blk), :] = jnp.where(lo, m0 + jnp.log(l0), m1 + jnp.log(l1))
            return 0

        lax.fori_loop(0, nq, q_block, 0)

    in_specs = [pl.BlockSpec((seq, 2 * LANES), lambda b, p: (b, q_off + p)),
                pl.BlockSpec((seq, 2 * LANES), lambda b, p: (b, k_off + p)),
                pl.BlockSpec((seq, LANES), lambda b, p: (b, v_off + p))]
    args = [q_arr, k_arr, v_arr]
    if has_bias:
        in_specs.append(pl.BlockSpec((None, 2, 8, seq), lambda b, p: (b, p, 0, 0)))
        args.append(bias)
    out_blk = pl.BlockSpec((seq, LANES), lambda b, p: (b, p))
    out_shape = [jax.ShapeDtypeStruct((t, HEAD_PAIRS * LANES), BF), jax.ShapeDtypeStruct((t, HEAD_PAIRS * LANES), F32)]
    if has_bias:
        out_shape.append(jax.ShapeDtypeStruct((t, HEAD_PAIRS * LANES), F32))
    return pl.pallas_call(
        body, name=name, grid=(nb, HEAD_PAIRS),
        out_shape=out_shape, in_specs=in_specs, out_specs=[out_blk] * len(out_shape),
        compiler_params=_params(("parallel", "parallel")),
    )(*args)


def _attn_bwd(q_arr, q_off, k_arr, k_off, v_arr, v_off, bias, o, do, lse, seq, name):
    t = q_arr.shape[0]
    nb = t // seq
    blk = min(ATTN_BLOCK, seq)
    nq = seq // blk
    has_bias = bias is not None

    def body(*refs):
        if has_bias:
            (q_ref, k_ref, v_ref, bias_ref, o_ref, do_ref, lse_ref,
             dq_ref, dk_ref, dv_ref, dbias_ref, dq_acc, dsum) = refs
        else:
            (q_ref, k_ref, v_ref, o_ref, do_ref, lse_ref, dq_ref, dk_ref, dv_ref, dq_acc, dsum) = refs
        lo, hi = _head_masks()
        rows = lax.broadcasted_iota(jnp.int32, (blk, blk), 0)
        cols = lax.broadcasted_iota(jnp.int32, (blk, blk), 1)
        causal = cols <= rows
        dq_acc[...] = jnp.zeros_like(dq_acc)

        def prep(iq, _):
            q0 = pl.multiple_of(iq * blk, blk)
            prod = do_ref[pl.ds(q0, blk), :].astype(F32) * o_ref[pl.ds(q0, blk), :].astype(F32)
            d0 = jnp.sum(jnp.where(lo, prod, 0.0), axis=-1, keepdims=True)
            d1 = jnp.sum(jnp.where(hi, prod, 0.0), axis=-1, keepdims=True)
            dsum[pl.ds(q0, blk), :] = jnp.where(lo, d0, d1)
            return 0

        lax.fori_loop(0, nq, prep, 0)

        def kv_block(j, _):
            k0 = pl.multiple_of(j * blk, blk)
            vv = v_ref[pl.ds(k0, blk), :]
            vs = [jnp.where(lo, vv, jnp.zeros_like(vv)), jnp.where(hi, vv, jnp.zeros_like(vv))]
            ks = [k_ref[pl.ds(k0, blk), h * LANES:(h + 1) * LANES] for h in range(2)]

            def q_block(iq, carry, diag):
                q0 = pl.multiple_of(iq * blk, blk)
                dov = do_ref[pl.ds(q0, blk), :]
                lse_v = lse_ref[pl.ds(q0, blk), :]
                dsum_v = dsum[pl.ds(q0, blk), :]
                dv_acc = carry[0]
                out = []
                for h in range(2):
                    dk_acc, db_acc = carry[1 + 2 * h], carry[2 + 2 * h]
                    qq = q_ref[pl.ds(q0, blk), h * LANES:(h + 1) * LANES]
                    s = lax.dot_general(qq, ks[h], _DIMS["nt"], preferred_element_type=F32)
                    if has_bias:
                        s = s + bias_ref[h, 0:1, pl.ds(k0, blk)]
                    p = jnp.exp(s - lse_v[:, h * HEAD_DIM:h * HEAD_DIM + 1])
                    if diag:
                        p = jnp.where(causal, p, 0.0)
                    dp = lax.dot_general(dov, vs[h], _DIMS["nt"], preferred_element_type=F32)
                    ds = p * (dp - dsum_v[:, h * HEAD_DIM:h * HEAD_DIM + 1])
                    ds_bf = ds.astype(BF)
                    pt_do = lax.dot_general(p.astype(BF), dov, _DIMS["tn"], preferred_element_type=F32)
                    dv_acc = dv_acc + jnp.where(hi if h else lo, pt_do, 0.0)
                    dk_acc = dk_acc + lax.dot_general(ds_bf, qq, _DIMS["tn"], preferred_element_type=F32)
                    dq_acc[pl.ds(q0, blk), h * LANES:(h + 1) * LANES] += jnp.dot(
                        ds_bf, ks[h], preferred_element_type=F32)
                    if has_bias:
                        db_acc = db_acc + jnp.sum(ds, axis=0, keepdims=True)
                    out += [dk_acc, db_acc]
                return (dv_acc, *out)

            init = (jnp.zeros((blk, LANES), F32),
                    jnp.zeros((blk, LANES), F32), jnp.zeros((1, blk), F32),
                    jnp.zeros((blk, LANES), F32), jnp.zeros((1, blk), F32))
            carry = q_block(j, init, diag=True)
            carry = lax.fori_loop(j + 1, nq, functools.partial(q_block, diag=False), carry)
            dv_ref[pl.ds(k0, blk), :] = carry[0].astype(BF)
            for h in range(2):
                dk_ref[pl.ds(k0, blk), h * LANES:(h + 1) * LANES] = carry[1 + 2 * h].astype(BF)
                if has_bias:
                    dbias_ref[h, :, pl.ds(k0, blk)] = jnp.broadcast_to(carry[2 + 2 * h], (8, blk))
            return 0

        lax.fori_loop(0, nq, kv_block, 0)
        dq_ref[...] = dq_acc[...].astype(BF)

    pair256 = lambda off: pl.BlockSpec((seq, 2 * LANES), lambda b, p: (b, off + p))
    pair128 = lambda off: pl.BlockSpec((seq, LANES), lambda b, p: (b, off + p))
    bias_spec = pl.BlockSpec((None, 2, 8, seq), lambda b, p: (b, p, 0, 0))
    in_specs = [pair256(q_off), pair256(k_off), pair128(v_off)]
    args = [q_arr, k_arr, v_arr]
    if has_bias:
        in_specs.append(bias_spec)
        args.append(bias)
    in_specs += [pair128(0), pair128(0), pair128(0)]
    args += [o, do, lse]
    out_shape = [jax.ShapeDtypeStruct((t, HEAD_PAIRS * 2 * LANES), BF),
                 jax.ShapeDtypeStruct((t, HEAD_PAIRS * 2 * LANES), BF),
                 jax.ShapeDtypeStruct((t, HEAD_PAIRS * LANES), BF)]
    out_specs = [pair256(0), pair256(0), pair128(0)]
    if has_bias:
        out_shape.append(jax.ShapeDtypeStruct((nb, HEADS, 8, seq), F32))
        out_specs.append(bias_spec)
    return pl.pallas_call(
        body, name=name, grid=(nb, HEAD_PAIRS),
        out_shape=out_shape, in_specs=in_specs, out_specs=out_specs,
        scratch_shapes=[pltpu.VMEM((seq, 2 * LANES), F32), pltpu.VMEM((seq, LANES), F32)],
        compiler_params=_params(("parallel", "parallel")),
    )(*args)


def _adamw(w, g, m, v, name):
    shape = w.shape
    last = shape[-1]
    rows = int(np.prod(shape[:-1])) if len(shape) > 1 else 1
    tr = _rows(rows, 512)

    def body(w_ref, g_ref, m_ref, v_ref, d_ref, nm_ref, nv_ref):
        d_ref[...], nm_ref[...], nv_ref[...] = _adamw_math(w_ref[...], g_ref[...], m_ref[...], v_ref[...])

    blk = pl.BlockSpec((tr, last), lambda i: (i, 0))
    sds = jax.ShapeDtypeStruct((rows, last), F32)
    outs = pl.pallas_call(
        body, name=name, grid=(rows // tr,),
        out_shape=(sds, sds, sds), in_specs=[blk] * 4, out_specs=(blk,) * 3,
        compiler_params=_params(("parallel",)),
    )(*[a.reshape(rows, last) for a in (w, g, m, v)])
    return tuple(a.reshape(shape) for a in outs)


LOW_COLS = 256


def _low_pad(a):
    return jnp.pad(a, ((0, 0),) * (a.ndim - 1) + ((0, LOW_COLS - a.shape[-1]),))


def _pack_mixer_shards(w):
    rows = lambda a: a.astype(BF).reshape(-1, a.shape[-1])
    out = jnp.concatenate([rows(w["fox_w_out"]), rows(w["mla_w_out"])], axis=0)
    low = jnp.concatenate([rows(w["mla_w_dq"]), rows(w["mla_w_ukv"]), _low_pad(rows(w["mla_w_uq"])),
                           _low_pad(rows(w["mla_w_dkv"]))], axis=0)
    return out, rows(w["fox_w_in"]), low


def _side_by_side(stack, r0, rows, cols=None):
    return jnp.concatenate([stack[dd, r0:r0 + rows, :cols] for dd in range(N_DEV)], axis=1)


def _stacked(stack, r0, rows, cols=None):
    part = stack[:, r0:r0 + rows, :cols]
    return part.reshape(N_DEV * rows, part.shape[2])


def _unpack_mixer_weights(out_all, in_all, low_all, j):
    return dict(
        fox_w_out=_stacked(out_all, 128 * j, 128), mla_w_out=_stacked(out_all, 256 + 128 * j, 128),
        fox_w_in=_side_by_side(in_all, 1024 * j, 1024),
        mla_w_dq=_stacked(low_all, 128 * j, 128), mla_w_ukv=_side_by_side(low_all, 256 + 128 * j, 128),
        mla_w_uq=_side_by_side(low_all, 512 + 256 * j, 256, 192), mla_w_dkv=_stacked(low_all, 1024 + 128 * j, 128, 160))


def _by_dest_rows(g):
    return g.reshape(N_DEV, g.shape[0] // N_DEV, g.shape[1]).astype(BF)


def _by_dest_cols(g):
    n = g.shape[1] // N_DEV
    return jnp.stack([g[:, dd * n:(dd + 1) * n] for dd in range(N_DEV)]).astype(BF)


def _pack_mixer_grads(g):
    cat = lambda parts: jnp.concatenate(parts, axis=1)
    out = cat([_by_dest_rows(a) for a in g["fox_w_out"] + g["mla_w_out"]])
    inn = cat([_by_dest_cols(a) for a in g["fox_w_in"]])
    low = cat([_by_dest_rows(a) for a in g["mla_w_dq"]] + [_by_dest_cols(a) for a in g["mla_w_ukv"]]
              + [_low_pad(_by_dest_cols(a)) for a in g["mla_w_uq"]] + [_low_pad(_by_dest_rows(a)) for a in g["mla_w_dkv"]])
    return out, inn, low


def _unpack_mixer_shard_grads(out, low):
    return dict(
        fox_w_out=out[:256].reshape(2, 128, 1024), mla_w_out=out[256:].reshape(2, 128, 1024),
        mla_w_dq=low[:256].reshape(2, 128, 256), mla_w_ukv=low[256:512].reshape(2, 128, 256),
        mla_w_uq=low[512:1024, :192].reshape(2, 256, 192), mla_w_dkv=low[1024:, :160].reshape(2, 128, 160))


def _pad_heads(w, width):
    k = w.shape[0]
    return jnp.pad(w.reshape(k, HEADS, width), ((0, 0), (0, 0), (0, LANES - width))).reshape(k, HEADS * LANES)


def _unpad_heads(w, width):
    k = w.shape[0]
    return w.reshape(k, HEADS, LANES)[:, :, :width].reshape(k, HEADS * width)


def _rope_tables(positions, scale):
    inv_freq = 10000.0 ** (-jnp.arange(0, 2 * ROPE_HALF, 2, dtype=F32) / (2 * ROPE_HALF))
    ang = positions.astype(F32)[:, None] * inv_freq
    cos, sin = jnp.cos(ang) * scale, jnp.sin(ang) * scale
    t = positions.shape[0]
    z = lambda n: jnp.zeros((t, n), F32)
    cos_p = jnp.concatenate([jnp.full((t, HEAD_DIM), scale, F32), cos, cos, z(32)], axis=1)
    sin_a = jnp.concatenate([z(64), -sin, z(48)], axis=1)
    sin_b = jnp.concatenate([z(80), sin, z(32)], axis=1)
    fwd = (cos_p, sin_a, sin_b)
    bwd = (cos_p, jnp.roll(sin_b, -ROPE_HALF, axis=1), jnp.roll(sin_a, ROPE_HALF, axis=1))
    return fwd, bwd


def _key_rows(cum, nb, seq):
    v = -cum.reshape(nb, seq, LANES)[:, :, :HEADS]
    return jnp.broadcast_to(jnp.transpose(v, (0, 2, 1))[:, :, None, :], (nb, HEADS, 8, seq))


def kernel(x, c, positions, ada_w, ada_b, norm_mix_g, norm_mlp_g, fox_w_in, fox_b_f, fox_w_out, mla_w_dq, mla_q_norm_g, mla_w_uq, mla_w_dkv, mla_kv_norm_g, mla_w_ukv, mla_w_out, mlp_w1, mlp_w2, final_norm_g, loss_target, m_ada_w, m_ada_b, m_norm_mix_g, m_norm_mlp_g, m_fox_w_in, m_fox_b_f, m_fox_w_out, m_mla_w_dq, m_mla_q_norm_g, m_mla_w_uq, m_mla_w_dkv, m_mla_kv_norm_g, m_mla_w_ukv, m_mla_w_out, m_mlp_w1, m_mlp_w2, m_final_norm_g, v_ada_w, v_ada_b, v_norm_mix_g, v_norm_mlp_g, v_fox_w_in, v_fox_b_f, v_fox_w_out, v_mla_w_dq, v_mla_q_norm_g, v_mla_w_uq, v_mla_w_dkv, v_mla_kv_norm_g, v_mla_w_ukv, v_mla_w_out, v_mlp_w1, v_mlp_w2, v_final_norm_g):
    args = dict(locals())
    weights = {n: args[n] for n in WEIGHTS}
    nb, seq, d = x.shape
    t = nb * seq
    depth = ada_w.shape[0]
    dev = 4 * lax.axis_index("x") + 2 * lax.axis_index("y") + lax.axis_index("c")
    n_mod_local = ada_w.shape[2]

    n_qg = mla_q_norm_g.shape[1]
    cond = jnp.concatenate([c, jnp.pad(mla_q_norm_g.reshape(1, -1), ((0, 7), (0, d - 2 * n_qg)))], axis=0)
    w1_rows, w2_rows = mlp_w1.shape[1], mlp_w2.shape[1]
    out_all, in_all, low_all, w1_all, w2_all, cond_all = _all_gather(
        [*_pack_mixer_shards(weights), mlp_w1.astype(BF).reshape(depth * w1_rows, -1),
         mlp_w2.astype(BF).reshape(depth * w2_rows, -1), cond], "gather_weights")
    full = [_unpack_mixer_weights(out_all, in_all, low_all, j) for j in range(2)]
    c_all = cond_all[:, :nb].reshape(N_DEV * nb, d)
    q_gain = jnp.transpose(cond_all[:, nb, :2 * n_qg].reshape(N_DEV, 2, n_qg), (1, 0, 2)).reshape(2, N_DEV * n_qg)
    mod_local = jnp.stack([
        _matmul(c_all, ada_w[i], mode="nn", name="ada_mod", out_dtype=F32, a_act="silu", epi="bias",
                extras=(lax.dynamic_slice_in_dim(ada_b[i], dev * n_mod_local, n_mod_local)[None, :],))
        for i in range(depth)])
    mod_all, = _all_gather([mod_local.reshape(depth * N_DEV * nb, n_mod_local)], "gather_mod")
    mod_all = jnp.transpose(mod_all.reshape(N_DEV, depth, N_DEV * nb, n_mod_local), (1, 2, 0, 3))
    mod_all = mod_all.reshape(depth, N_DEV * nb, N_DEV * n_mod_local)
    mod = lax.dynamic_slice_in_dim(mod_all, dev * nb, nb, axis=1)
    mod = mod.reshape(depth, nb, 6, 1, d)

    pos = positions.reshape(t)
    rope_q, rope_q_t = _rope_tables(pos, MLA_SCALE)
    rope_k, rope_k_t = _rope_tables(pos, 1.0)

    def fox_weights(j):
        w_in = full[j]["fox_w_in"]
        wq = _pad_heads(w_in[:, :d] * FOX_SCALE, HEAD_DIM)
        wk = _pad_heads(w_in[:, d:2 * d], HEAD_DIM)
        w_qkv = jnp.concatenate([wq, wk, w_in[:, 2 * d:3 * d]], axis=1)
        w_f = jnp.pad(w_in[:, 3 * d:], ((0, 0), (0, LANES - HEADS)))
        return w_qkv, w_f

    def mla_weights(j):
        w_dkv = full[j]["mla_w_dkv"]
        w_down = jnp.concatenate([full[j]["mla_w_dq"], w_dkv[:, :128], jnp.zeros((d, 64), BF),
                                  w_dkv[:, 128:160], jnp.zeros((d, 32), BF)], axis=1)
        w_uq = _pad_heads(full[j]["mla_w_uq"], 96)
        w_ukv = full[j]["mla_w_ukv"].reshape(128, HEADS, 2, HEAD_DIM)
        w_uk = jnp.pad(w_ukv[:, :, 0, :], ((0, 0), (0, 0), (0, 64))).reshape(128, HEADS * LANES)
        w_uv = w_ukv[:, :, 1, :].reshape(128, HEADS * HEAD_DIM)
        place = np.zeros((128, HEADS, LANES), np.float32)
        for i in range(2 * ROPE_HALF):
            place[64 + i, :, 64 + i] = 1.0
        bottom = jnp.concatenate([jnp.asarray(place.reshape(128, HEADS * LANES), BF),
                                  jnp.zeros((128, HEADS * HEAD_DIM), BF)], axis=1)
        w_kv = jnp.concatenate([jnp.concatenate([w_uk, w_uv], axis=1), bottom], axis=0)
        return w_down, w_uq, w_kv

    xs = x.reshape(t, d)
    saved = []
    for i in range(depth):
        j = i // 2
        sh_m, sc_m, g_m, sh_f, sc_f, g_f = (mod[i, :, q] for q in range(6))
        gain_mix = norm_mix_g[i][None, :]
        gain_mlp = norm_mlp_g[i][None, :]
        s = dict(x_in=xs)
        h = _norm_mod(xs, gain_mix, sc_m, sh_m, seq, "norm_mix")
        s["h"] = h
        if i % 2 == 0:
            w_qkv, w_f = fox_weights(j)
            qkv = _matmul(h, w_qkv, mode="nn", name="fox_qkv")
            fg = _matmul(h, w_f, mode="nn", name="fox_gate_logits", out_dtype=F32)
            b_f = jnp.pad(fox_b_f[j], (0, LANES - HEADS))[None, :]
            cum = _fox_gate(fg, b_f, seq, "fox_gate")
            bias = _key_rows(cum, nb, seq)
            o, lse, o32 = _attn_fwd(qkv, 0, qkv, 8, qkv, 32, bias, seq, "fox_attn")
            s.update(qkv=qkv, fg=fg, b_f=b_f, bias=bias, w_qkv=w_qkv, w_f=w_f, o32=o32)
            w_out = full[j]["fox_w_out"]
        else:
            w_down, w_uq, w_kv = mla_weights(j)
            down = _matmul(h, w_down, mode="nn", name="mla_down", out_dtype=F32)
            gq, gkv = q_gain[j][None, :], mla_kv_norm_g[j][None, :]
            cq, ckr = _mla_mid(down, gq, gkv, rope_k, "mla_mid")
            q_raw = _matmul(cq, w_uq, mode="nn", name="mla_uq", out_dtype=F32)
            q_rot = _rope(q_raw, rope_q, "mla_rope_q")
            kv = _matmul(ckr, w_kv, mode="nn", name="mla_ukv")
            o, lse = _attn_fwd(q_rot, 0, kv, 0, kv, 16, None, seq, "mla_attn")
            s.update(down=down, gq=gq, gkv=gkv, cq=cq, ckr=ckr, q_rot=q_rot, kv=kv,
                     w_down=w_down, w_uq=w_uq, w_kv=w_kv)
            w_out = full[j]["mla_w_out"]
        xs, y = _matmul(o, w_out, mode="nn", name="attn_out", epi="resid_gate", extras=(xs, g_m), seq=seq)
        s.update(o=o, lse=lse, y=y, w_out=w_out, x_mid=xs)
        h2 = _norm_mod(xs, gain_mlp, sc_f, sh_f, seq, "norm_mlp")
        a_pre = _matmul(h2, w1_all, mode="nn", name="mlp_up", layer=("col", i, w1_rows))
        xs, y2 = _matmul(a_pre, w2_all, mode="nn", name="mlp_down", layer=("row", i, w2_rows), a_act="relu2",
                         epi="resid_gate", extras=(xs, g_f), seq=seq)
        s.update(h2=h2, a_pre=a_pre, y2=y2)
        saved.append(s)

    loss_part, dx, dg_final = _loss_head(xs, final_norm_g[None, :], loss_target.reshape(t, d), "loss_head")

    grads = {n: [None] * weights[n].shape[0] for n in MIXER_WEIGHTS}
    w1_cols, w1_tm = mlp_w1.shape[2], _pick(w1_rows, 1024)
    g_w1 = jnp.zeros((N_DEV, depth * w1_rows, w1_cols), BF)
    g_w2 = jnp.zeros((N_DEV, depth * w2_rows, d), BF)
    dg_mix, dg_mlp, db_f, dg_kv, dg_q = [None] * depth, [None] * depth, [None] * 2, [None] * 2, [None] * 2
    dmod = [None] * depth
    for i in reversed(range(depth)):
        j = i // 2
        s = saved[i]
        sh_m, sc_m, g_m, sh_f, sc_f, g_f = (mod[i, :, q] for q in range(6))
        dy2, dg_f = _gate_bwd(dx, s["y2"], g_f, seq, "gate_bwd")
        da_pre = _matmul(dy2, w2_all, mode="nt", name="mlp_down_dx", layer=("row", i, w2_rows), epi="mul_drelu",
                         extras=(s["a_pre"],))
        g_w2 = _matmul(s["a_pre"], dy2, mode="tn", name="mlp_down_dw", a_act="relu2", tm=w2_rows,
                       into=(g_w2, (None, w2_rows, d), lambda r, j, k, li=i: (r, li, 0)))
        dh2 = _matmul(da_pre, w1_all, mode="nt", name="mlp_up_dx", layer=("col", i, w1_rows))
        g_w1 = _matmul(s["h2"], da_pre, mode="tn", name="mlp_up_dw", tm=w1_tm, tn=w1_cols,
                       into=(g_w1, (None, w1_tm, w1_cols), lambda r, j, k, li=i: (j, (w1_rows // w1_tm) * li + r, 0)))
        dx, dg_mlp[i], dsc_f, dsh_f = _norm_mod_bwd(dh2, s["x_mid"], norm_mlp_g[i][None, :], sc_f, dx, seq,
                                                    "norm_bwd")
        dy, dg_m = _gate_bwd(dx, s["y"], g_m, seq, "gate_bwd")
        do = _matmul(dy, s["w_out"], mode="nt", name="attn_out_dx")
        dw_out = _matmul(s["o"], dy, mode="tn", name="attn_out_dw", out_dtype=F32)
        if i % 2 == 0:
            qkv = s["qkv"]
            dq, dk, dv, dbias = _attn_bwd(qkv, 0, qkv, 8, qkv, 32, s["bias"], s["o32"], do, s["lse"], seq,
                                          "fox_attn_bwd")
            dqkv = jnp.concatenate([dq, dk, dv], axis=1)
            d_cum = -jnp.transpose(dbias[:, :, 0, :], (0, 2, 1)).reshape(t, HEADS)
            d_cum = jnp.pad(d_cum, ((0, 0), (0, LANES - HEADS)))
            dfg, db = _fox_gate_bwd(d_cum, s["fg"], s["b_f"], seq, "fox_gate_bwd")
            db_f[j] = db
            dh = _matmul(dfg, s["w_f"], mode="nt", name="fox_gate_dx", out_dtype=F32)
            dh = _matmul(dqkv, s["w_qkv"], mode="nt", name="fox_qkv_dx", epi="add", extras=(dh,))
            dw_qkv = _matmul(s["h"], dqkv, mode="tn", name="fox_qkv_dw", out_dtype=F32)
            dw_f = _matmul(s["h"], dfg, mode="tn", name="fox_gate_dw", out_dtype=F32)
            grads["fox_w_in"][j] = jnp.concatenate(
                [_unpad_heads(dw_qkv[:, :2048], HEAD_DIM) * FOX_SCALE, _unpad_heads(dw_qkv[:, 2048:4096], HEAD_DIM),
                 dw_qkv[:, 4096:], dw_f[:, :HEADS]], axis=1)
            grads["fox_w_out"][j] = dw_out
        else:
            kv = s["kv"]
            dq, dk, dv = _attn_bwd(s["q_rot"], 0, kv, 0, kv, 16, None, s["o"], do, s["lse"], seq, "mla_attn_bwd")
            dq_raw = _rope(dq, rope_q_t, "mla_rope_q_bwd")
            dcq = _matmul(dq_raw, s["w_uq"], mode="nt", name="mla_uq_dx")
            dw_uq = _matmul(s["cq"], dq_raw, mode="tn", name="mla_uq_dw", out_dtype=F32)
            dkv = jnp.concatenate([dk, dv], axis=1)
            dckr = _matmul(dkv, s["w_kv"], mode="nt", name="mla_ukv_dx")
            dw_kv = _matmul(s["ckr"], dkv, mode="tn", name="mla_ukv_dw", out_dtype=F32)
            d_down, dgq, dgkv = _mla_mid_bwd(s["down"], dcq, dckr, s["gq"], s["gkv"], rope_k_t, "mla_mid_bwd")
            dg_q[j], dg_kv[j] = dgq, dgkv
            dh = _matmul(d_down, s["w_down"], mode="nt", name="mla_down_dx")
            dw_down = _matmul(s["h"], d_down, mode="tn", name="mla_down_dw", out_dtype=F32)
            grads["mla_w_dq"][j] = dw_down[:, :256]
            grads["mla_w_dkv"][j] = jnp.concatenate([dw_down[:, 256:384], dw_down[:, 448:480]], axis=1)
            grads["mla_w_uq"][j] = _unpad_heads(dw_uq, 96)
            dk_nope = dw_kv[:128, :HEADS * LANES].reshape(128, HEADS, LANES)[:, :, :HEAD_DIM]
            dv_w = dw_kv[:128, HEADS * LANES:].reshape(128, HEADS, HEAD_DIM)
            grads["mla_w_ukv"][j] = jnp.concatenate([dk_nope, dv_w], axis=2).reshape(128, HEADS * LANES)
            grads["mla_w_out"][j] = dw_out
        dx, dg_mix[i], dsc_m, dsh_m = _norm_mod_bwd(dh, s["x_in"], norm_mix_g[i][None, :], sc_m, dx, seq,
                                                    "norm_bwd")
        dmod[i] = jnp.stack([dsh_m, dsc_m, dg_m, dsh_f, dsc_f, dg_f], axis=1).reshape(nb, 6 * d)

    grad_x = dx.reshape(nb, seq, d)

    bufs = [*_pack_mixer_grads(grads), g_w1, g_w2]
    from_sibling = _swap_with_sibling(bufs, "grads_to_sibling")
    rel = _relative_blocks()
    partial = [_chip_partial(b, fs, rel, "grads_chip_sum") for b, fs in zip(bufs, from_sibling)]
    from_chips = _swap_with_chips([p[1] for p in partial], "grads_to_chips")
    shard_grads = _unpack_mixer_shard_grads(_add_parts(partial[0][0], from_chips[0], "grads_total"),
                                            _add_parts(partial[2][0], from_chips[2], "grads_total"))
    done = {"fox_w_in": _total_adamw(partial[1][0], from_chips[1], fox_w_in, m_fox_w_in, v_fox_w_in, "adamw_in"),
            "mlp_w1": _total_adamw(partial[3][0], from_chips[3], mlp_w1, m_mlp_w1, v_mlp_w1, "adamw_w1"),
            "mlp_w2": _total_adamw(partial[4][0], from_chips[4], mlp_w2, m_mlp_w2, v_mlp_w2, "adamw_w2")}

    dmod_arr = jnp.stack(dmod)
    wide = lambda a: jnp.pad(a, ((0, 0), (0, d - a.shape[1])))
    pieces = [wide(loss_part), *dg_mix, *dg_mlp, *[wide(a) for a in db_f], *[wide(a) for a in dg_kv], dg_final,
              *[wide(a) for a in dg_q], jnp.sum(dmod_arr, axis=1).reshape(depth * 6, d)]
    n_small = sum(p.shape[0] for p in pieces)
    both = jnp.concatenate(pieces + [dmod_arr.reshape(depth * nb * 6, d)], axis=0)
    both = jnp.pad(both, ((0, (-both.shape[0]) % 8), (0, 0)))
    both_all, = _all_gather([both], "gather_small")
    total = _sum_leading(both_all, "sum_small")
    off = 0

    def take(rows):
        nonlocal off
        out = total[off:off + rows]
        off += rows
        return out

    loss = take(1)[0, 0]
    g_small = dict(
        norm_mix_g=take(depth), norm_mlp_g=take(depth), fox_b_f=take(2)[:, :HEADS], mla_kv_norm_g=take(2)[:, :128],
        final_norm_g=take(1)[0],
        mla_q_norm_g=lax.dynamic_slice_in_dim(take(2)[:, :N_DEV * n_qg], dev * n_qg, n_qg, axis=1),
        ada_b=take(depth * 6).reshape(depth, 6 * d))
    dmod_all = both_all[:, n_small:n_small + depth * nb * 6]
    dmod_all = jnp.transpose(dmod_all.reshape(N_DEV, depth, nb, 6 * d), (1, 0, 2, 3)).reshape(depth, N_DEV * nb, 6 * d)
    dmod_cols = lax.dynamic_slice_in_dim(dmod_all, dev * n_mod_local, n_mod_local, axis=2)
    g_ada_w = jnp.stack([_matmul(c_all, dmod_cols[i], mode="tn", name="ada_dw", out_dtype=F32, a_act="silu")
                         for i in range(depth)])

    all_grads = dict(shard_grads)
    all_grads.update(g_small)
    all_grads["ada_w"] = g_ada_w

    deltas, new_m, new_v = {}, {}, {}
    for n in WEIGHTS:
        if n in done:
            all_grads[n], deltas[n], new_m[n], new_v[n] = done[n]
        else:
            deltas[n], new_m[n], new_v[n] = _adamw(weights[n], all_grads[n], args["m_" + n], args["v_" + n], "adamw")

    return (loss, grad_x, *[all_grads[n] for n in WEIGHTS], *[deltas[n] for n in WEIGHTS],
            *[new_m[n] for n in WEIGHTS], *[new_v[n] for n in WEIGHTS])
```

```python
import functools
import math

import jax
import jax.numpy as jnp
import numpy as np
from jax import lax
from jax.experimental import pallas as pl
from jax.experimental.pallas import tpu as pltpu

F32 = jnp.float32
BF = jnp.bfloat16

N_DEV = 8
HEADS = 16
HEAD_PAIRS = HEADS // 2
HEAD_DIM = 64
LANES = 128
ROPE_HALF = 16
NORM_EPS = 1e-6
MLA_SCALE = 96.0 ** -0.5
FOX_SCALE = 0.125
ATTN_BLOCK = 512
ROW_BLOCK = 512
VMEM_LIMIT = 56 * 1024 * 1024
MESH = pl.DeviceIdType.MESH

ADAM_LR = 0.001
ADAM_B1 = 0.9
ADAM_B2 = 0.999
ADAM_EPS = 1e-08
ADAM_WD = 0.01
ADAM_STEP = 10

MIXER_WEIGHTS = ("fox_w_in", "fox_w_out", "mla_w_dq", "mla_w_uq", "mla_w_dkv", "mla_w_ukv", "mla_w_out")
WEIGHTS = ("ada_w", "ada_b", "norm_mix_g", "norm_mlp_g", "fox_w_in", "fox_b_f", "fox_w_out", "mla_w_dq",
           "mla_q_norm_g", "mla_w_uq", "mla_w_dkv", "mla_kv_norm_g", "mla_w_ukv", "mla_w_out", "mlp_w1",
           "mlp_w2", "final_norm_g")


def _params(sem=None):
    return pltpu.CompilerParams(dimension_semantics=sem, vmem_limit_bytes=VMEM_LIMIT)


def _pick(n, target):
    if n <= target:
        return n
    for t in range(target, 127, -128):
        if n % t == 0:
            return t
    return n


def _rows(n, target=512):
    if n <= target:
        return n
    for t in range(target, 7, -8):
        if n % t == 0:
            return t
    return n


def _place():
    x, y, c = lax.axis_index("x"), lax.axis_index("y"), lax.axis_index("c")
    return x, y, c


def _adamw_math(w, g, m, v):
    nm = ADAM_B1 * m + (1.0 - ADAM_B1) * g
    nv = ADAM_B2 * v + (1.0 - ADAM_B2) * (g * g)
    m_hat = nm * (1.0 / (1.0 - ADAM_B1 ** ADAM_STEP))
    v_hat = nv * (1.0 / (1.0 - ADAM_B2 ** ADAM_STEP))
    return -ADAM_LR * (m_hat / (jnp.sqrt(v_hat) + ADAM_EPS) + ADAM_WD * w), nm, nv


def _all_gather(blocks, name):
    n = len(blocks)

    def body(*refs):
        x_refs, out_refs = refs[:n], refs[n:2 * n]
        send_sems, recv_sems, local_sems = refs[2 * n:]
        x, y, c = _place()
        me, sibling = (x, y, c), (x, y, 1 - c)
        chips = [(1 - x, y), (x, 1 - y), (1 - x, 1 - y)]

        def slot(a, px, py, pc):
            return out_refs[a].at[4 * px + 2 * py + pc]

        def copy(a, k, blk, to, src=None):
            return pltpu.make_async_remote_copy(
                src_ref=slot(a, *blk) if src is None else src, dst_ref=slot(a, *blk),
                send_sem=send_sems.at[7 * a + k], recv_sem=recv_sems.at[7 * a + k], device_id=to, device_id_type=MESH)

        mine = [pltpu.make_async_copy(x_refs[a], slot(a, *me), local_sems.at[a]) for a in range(n)]
        for cp in mine:
            cp.start()
        first = []
        for j, chip in enumerate(chips):
            first += [copy(a, 1 + j, me, (*chip, c), src=x_refs[a]) for a in range(n)]
        first += [copy(a, 0, me, sibling, src=x_refs[a]) for a in range(n)]
        for cp in first:
            cp.start()
        passed = []
        for j, chip in enumerate(chips):
            for a in range(n):
                copy(a, 1 + j, (*chip, c), me).wait_recv()
                passed.append(copy(a, 4 + j, (*chip, c), sibling))
                passed[-1].start()
        for a in range(n):
            copy(a, 0, sibling, me).wait_recv()
        for j, chip in enumerate(chips):
            for a in range(n):
                copy(a, 4 + j, (*chip, 1 - c), me).wait_recv()
        for cp in first + passed:
            cp.wait_send()
        for cp in mine:
            cp.wait()

    any_spec = pl.BlockSpec(memory_space=pl.ANY)
    return pl.pallas_call(
        body, name=name,
        out_shape=[jax.ShapeDtypeStruct((N_DEV,) + b.shape, b.dtype) for b in blocks],
        in_specs=[any_spec] * n, out_specs=[any_spec] * n,
        scratch_shapes=[pltpu.SemaphoreType.DMA((7 * n,)), pltpu.SemaphoreType.DMA((7 * n,)),
                        pltpu.SemaphoreType.DMA((n,))],
    )(*blocks)


def _swap_with_sibling(bufs, name):
    n = len(bufs)

    def body(*refs):
        src, dst = refs[:n], refs[n:2 * n]
        send_sems, recv_sems = refs[2 * n:]
        x, y, c = _place()
        cps = []
        for a in range(n):
            for k in range(4):
                cps.append(pltpu.make_async_remote_copy(
                    src_ref=src[a].at[2 * k + (1 - c)], dst_ref=dst[a].at[k],
                    send_sem=send_sems.at[4 * a + k], recv_sem=recv_sems.at[4 * a + k],
                    device_id=(x, y, 1 - c), device_id_type=MESH))
        for cp in cps:
            cp.start()
        for cp in cps:
            cp.wait()

    any_spec = pl.BlockSpec(memory_space=pl.ANY)
    return pl.pallas_call(
        body, name=name,
        out_shape=[jax.ShapeDtypeStruct((4,) + b.shape[1:], b.dtype) for b in bufs],
        in_specs=[any_spec] * n, out_specs=[any_spec] * n,
        scratch_shapes=[pltpu.SemaphoreType.DMA((4 * n,)), pltpu.SemaphoreType.DMA((4 * n,))],
    )(*bufs)


def _swap_with_chips(parts, name):
    n = len(parts)

    def body(*refs):
        src, dst = refs[:n], refs[n:2 * n]
        send_sems, recv_sems = refs[2 * n:]
        x, y, c = _place()
        chips = [(1 - x, y), (x, 1 - y), (1 - x, 1 - y)]
        cps = []
        for k, chip in enumerate(chips):
            for a in range(n):
                cps.append(pltpu.make_async_remote_copy(
                    src_ref=src[a].at[k], dst_ref=dst[a].at[k],
                    send_sem=send_sems.at[3 * a + k], recv_sem=recv_sems.at[3 * a + k],
                    device_id=(*chip, c), device_id_type=MESH))
        for cp in cps:
            cp.start()
        for cp in cps:
            cp.wait()

    any_spec = pl.BlockSpec(memory_space=pl.ANY)
    return pl.pallas_call(
        body, name=name,
        out_shape=[jax.ShapeDtypeStruct(p.shape, p.dtype) for p in parts],
        in_specs=[any_spec] * n, out_specs=[any_spec] * n,
        scratch_shapes=[pltpu.SemaphoreType.DMA((3 * n,)), pltpu.SemaphoreType.DMA((3 * n,))],
    )(*parts)


def _relative_blocks():
    x, y, c = _place()
    flips = ((0, 0), (1, 0), (0, 1), (1, 1))
    mine = [4 * (x ^ fx) + 2 * (y ^ fy) + c for fx, fy in flips]
    sib = [2 * (x ^ fx) + (y ^ fy) for fx, fy in flips]
    return jnp.stack(mine + sib).astype(jnp.int32)


def _chip_partial(buf, from_sibling, rel, name):
    _, r, cdim = buf.shape
    tr = _rows(r, 256)

    def body(rel_ref, m0, m1, m2, m3, s0, s1, s2, s3, own_ref, parts_ref):
        del rel_ref
        own_ref[...] = m0[...].astype(F32) + s0[...].astype(F32)
        for k, (m, s) in enumerate(((m1, s1), (m2, s2), (m3, s3))):
            parts_ref[k] = (m[...].astype(F32) + s[...].astype(F32)).astype(parts_ref.dtype)

    def pick(k):
        return pl.BlockSpec((None, tr, cdim), lambda i, rel_ref: (rel_ref[k], i, 0))

    return pl.pallas_call(
        body, name=name,
        grid_spec=pltpu.PrefetchScalarGridSpec(
            num_scalar_prefetch=1, grid=(r // tr,),
            in_specs=[pick(k) for k in range(8)],
            out_specs=(pl.BlockSpec((tr, cdim), lambda i, rel_ref: (i, 0)),
                       pl.BlockSpec((3, tr, cdim), lambda i, rel_ref: (0, i, 0)))),
        out_shape=(jax.ShapeDtypeStruct((r, cdim), F32), jax.ShapeDtypeStruct((3, r, cdim), buf.dtype)),
        compiler_params=_params(("parallel",)),
    )(rel, buf, buf, buf, buf, from_sibling, from_sibling, from_sibling, from_sibling)


def _total_adamw(own, parts, w, m, v, name):
    r, cdim = own.shape
    tr = _rows(r, 256)

    def body(own_ref, parts_ref, w_ref, m_ref, v_ref, g_ref, d_ref, nm_ref, nv_ref):
        g = own_ref[...]
        for k in range(3):
            g = g + parts_ref[k].astype(F32)
        g_ref[...] = g
        d_ref[...], nm_ref[...], nv_ref[...] = _adamw_math(w_ref[...], g, m_ref[...], v_ref[...])

    blk = pl.BlockSpec((tr, cdim), lambda i: (i, 0))
    sds = jax.ShapeDtypeStruct((r, cdim), F32)
    outs = pl.pallas_call(
        body, name=name, grid=(r // tr,),
        out_shape=(sds,) * 4,
        in_specs=[blk, pl.BlockSpec((3, tr, cdim), lambda i: (0, i, 0)), blk, blk, blk], out_specs=(blk,) * 4,
        compiler_params=_params(("parallel",)),
    )(own, parts, *[a.reshape(r, cdim) for a in (w, m, v)])
    return tuple(a.reshape(w.shape) for a in outs)


def _add_parts(own, parts, name):
    r, cdim = own.shape
    tr = _rows(r, 512)

    def body(own_ref, parts_ref, out_ref):
        acc = own_ref[...]
        for k in range(parts_ref.shape[0]):
            acc = acc + parts_ref[k].astype(F32)
        out_ref[...] = acc

    return pl.pallas_call(
        body, name=name, grid=(r // tr,),
        out_shape=jax.ShapeDtypeStruct((r, cdim), F32),
        in_specs=[pl.BlockSpec((tr, cdim), lambda i: (i, 0)),
                  pl.BlockSpec((parts.shape[0], tr, cdim), lambda i: (0, i, 0))],
        out_specs=pl.BlockSpec((tr, cdim), lambda i: (i, 0)),
        compiler_params=_params(("parallel",)),
    )(own, parts)


def _sum_leading(stack, name):
    n, r, cdim = stack.shape
    tr = _rows(r, 512)

    def body(in_ref, out_ref):
        acc = in_ref[0]
        for k in range(1, n):
            acc = acc + in_ref[k]
        out_ref[...] = acc

    return pl.pallas_call(
        body, name=name, grid=(r // tr,),
        out_shape=jax.ShapeDtypeStruct((r, cdim), F32),
        in_specs=[pl.BlockSpec((n, tr, cdim), lambda i: (0, i, 0))],
        out_specs=pl.BlockSpec((tr, cdim), lambda i: (i, 0)),
        compiler_params=_params(("parallel",)),
    )(stack)


_DIMS = {"nn": (((1,), (0,)), ((), ())), "nt": (((1,), (1,)), ((), ())), "tn": (((0,), (0,)), ((), ()))}


def _stack_spec(shape, mode, layer):
    cut, l, rows = layer
    cols = shape[2]
    by_n = pl.BlockSpec((None, rows, cols), lambda i, j, k: (j, l, 0))
    by_k = pl.BlockSpec((None, rows, cols), lambda i, j, k: (k, l, 0))
    if cut == "col":
        return (by_n, N_DEV * cols, cols, rows) if mode == "nn" else (by_k, rows, rows, cols)
    return (by_k, cols, cols, rows) if mode == "nn" else (by_n, N_DEV * rows, rows, cols)


def _matmul(a, b, *, mode, name, out_dtype=BF, a_act=None, epi=None, extras=(), seq=None, layer=None, tm=None,
            tn=None, into=None):
    if mode == "tn":
        kdim, m = a.shape
    else:
        m, kdim = a.shape
    tm = _pick(m, 1024) if tm is None else tm
    tk = _pick(kdim, 2048 if mode == "tn" else 1024)
    b_spec = None
    if layer is not None:
        b_spec, n, tn, tk = _stack_spec(b.shape, mode, layer)
    else:
        n = b.shape[0] if mode == "nt" else b.shape[1]
        tn = _pick(n, 1024) if tn is None else tn
    nk = kdim // tk
    a_spec = (pl.BlockSpec((tk, tm), lambda i, j, k: (k, i)) if mode == "tn"
              else pl.BlockSpec((tm, tk), lambda i, j, k: (i, k)))
    if b_spec is None:
        b_spec = (pl.BlockSpec((tn, tk), lambda i, j, k: (j, k)) if mode == "nt"
                  else pl.BlockSpec((tk, tn), lambda i, j, k: (k, j)))
    tile = pl.BlockSpec((tm, tn), lambda i, j, k: (i, j))
    in_specs, out_specs = [a_spec, b_spec], [tile]
    out_shape = [jax.ShapeDtypeStruct((m, n), out_dtype)]
    if epi == "resid_gate":
        in_specs += [tile, pl.BlockSpec((None, 1, tn), lambda i, j, k: ((i * tm) // seq, 0, j))]
        out_shape = [jax.ShapeDtypeStruct((m, n), F32), jax.ShapeDtypeStruct((m, n), BF)]
        out_specs = [tile, tile]
    elif epi in ("mul_drelu", "add"):
        in_specs += [tile]
    elif epi == "bias":
        in_specs += [pl.BlockSpec((1, tn), lambda i, j, k: (0, j))]
    n_extra, n_out = len(in_specs) - 2, len(out_specs)
    aliases, n_kept = {}, 0
    if into is not None:
        buffer, block, index_map = into
        out_dtype = buffer.dtype
        in_specs.append(pl.BlockSpec(memory_space=pl.ANY))
        extras = tuple(extras) + (buffer,)
        aliases, n_kept = {len(in_specs) - 1: 0}, 1
        out_shape = [jax.ShapeDtypeStruct(buffer.shape, buffer.dtype)]
        out_specs = [pl.BlockSpec(block, index_map)]
    dims = _DIMS[mode]

    def body(*refs):
        a_ref, b_ref = refs[:2]
        ex = refs[2:2 + n_extra]
        outs = refs[2 + n_extra + n_kept:2 + n_extra + n_kept + n_out]
        av = a_ref[...]
        if a_act == "relu2":
            t = jnp.maximum(av.astype(F32), 0.0)
            av = t * t
        elif a_act == "silu":
            t = av.astype(F32)
            av = t / (1.0 + jnp.exp(-t))
        part = lax.dot_general(av.astype(BF), b_ref[...].astype(BF), dims, preferred_element_type=F32)

        def finish(acc):
            if epi == "resid_gate":
                outs[0][...] = ex[0][...] + ex[1][...] * acc
                outs[1][...] = acc.astype(BF)
            elif epi == "mul_drelu":
                outs[0][...] = (acc * (2.0 * jnp.maximum(ex[0][...].astype(F32), 0.0))).astype(out_dtype)
            elif epi == "add":
                outs[0][...] = (acc + ex[0][...].astype(F32)).astype(out_dtype)
            elif epi == "bias":
                outs[0][...] = (acc + ex[0][...]).astype(out_dtype)
            else:
                outs[0][...] = acc.astype(out_dtype)

        if nk == 1:
            finish(part)
        else:
            acc_ref = refs[-1]
            k = pl.program_id(2)

            @pl.when(k == 0)
            def _():
                acc_ref[...] = part

            @pl.when(k > 0)
            def _():
                acc_ref[...] += part

            @pl.when(k == nk - 1)
            def _():
                finish(acc_ref[...])

    res = pl.pallas_call(
        body, name=name, grid=(m // tm, n // tn, nk),
        out_shape=out_shape, in_specs=in_specs, out_specs=out_specs,
        scratch_shapes=[pltpu.VMEM((tm, tn), F32)] if nk > 1 else [],
        input_output_aliases=aliases,
        compiler_params=_params(("parallel", "parallel", "arbitrary")),
    )(a, b, *extras)
    return res if n_out > 1 else res[0]


def _norm_mod(x, gain, scale, shift, seq, name):
    t, w = x.shape
    tr = ROW_BLOCK

    def body(x_ref, g_ref, sc_ref, sh_ref, out_ref):
        xv = x_ref[...]
        rstd = lax.rsqrt(jnp.mean(xv * xv, axis=-1, keepdims=True) + NORM_EPS)
        y = xv * rstd * g_ref[...]
        out_ref[...] = (y * (1.0 + sc_ref[...]) + sh_ref[...]).astype(BF)

    per_b = pl.BlockSpec((None, 1, w), lambda i: ((i * tr) // seq, 0, 0))
    return pl.pallas_call(
        body, name=name, grid=(t // tr,),
        out_shape=jax.ShapeDtypeStruct((t, w), BF),
        in_specs=[pl.BlockSpec((tr, w), lambda i: (i, 0)), pl.BlockSpec((1, w), lambda i: (0, 0)), per_b, per_b],
        out_specs=pl.BlockSpec((tr, w), lambda i: (i, 0)),
        compiler_params=_params(("parallel",)),
    )(x, gain, scale, shift)


def _norm_mod_bwd(dh, x, gain, scale, dres, seq, name):
    t, w = x.shape
    tr = ROW_BLOCK
    steps_per_seq = seq // tr
    nb = t // seq

    def body(dh_ref, x_ref, g_ref, sc_ref, dres_ref, dx_ref, dg_ref, dsc_ref, dsh_ref):
        i = pl.program_id(0)
        xv = x_ref[...]
        dhv = dh_ref[...].astype(F32)
        rstd = lax.rsqrt(jnp.mean(xv * xv, axis=-1, keepdims=True) + NORM_EPS)
        xhat = xv * rstd
        one_sc = 1.0 + sc_ref[...]
        g = g_ref[...]
        dxhat = dhv * (g * one_sc)
        proj = jnp.mean(dxhat * xhat, axis=-1, keepdims=True)
        dx_ref[...] = dres_ref[...] + rstd * (dxhat - xhat * proj)
        dhx = dhv * xhat
        dg_part = jnp.sum(dhx * one_sc, axis=0, keepdims=True)
        dsc_part = jnp.sum(dhx * g, axis=0, keepdims=True)
        dsh_part = jnp.sum(dhv, axis=0, keepdims=True)

        @pl.when(i == 0)
        def _():
            dg_ref[...] = dg_part

        @pl.when(i > 0)
        def _():
            dg_ref[...] += dg_part

        @pl.when(i % steps_per_seq == 0)
        def _():
            dsc_ref[...] = dsc_part
            dsh_ref[...] = dsh_part

        @pl.when(i % steps_per_seq != 0)
        def _():
            dsc_ref[...] += dsc_part
            dsh_ref[...] += dsh_part

    row = pl.BlockSpec((tr, w), lambda i: (i, 0))
    per_b = pl.BlockSpec((None, 1, w), lambda i: ((i * tr) // seq, 0, 0))
    vec = pl.BlockSpec((1, w), lambda i: (0, 0))
    return pl.pallas_call(
        body, name=name, grid=(t // tr,),
        out_shape=(jax.ShapeDtypeStruct((t, w), F32), jax.ShapeDtypeStruct((1, w), F32),
                   jax.ShapeDtypeStruct((nb, 1, w), F32), jax.ShapeDtypeStruct((nb, 1, w), F32)),
        in_specs=[row, row, vec, per_b, row],
        out_specs=(row, vec, per_b, per_b),
        compiler_params=_params(("arbitrary",)),
    )(dh, x, gain, scale, dres)


def _gate_bwd(dx, y, gate, seq, name):
    t, w = dx.shape
    tr = ROW_BLOCK
    steps_per_seq = seq // tr
    nb = t // seq

    def body(dx_ref, y_ref, g_ref, dy_ref, dg_ref):
        i = pl.program_id(0)
        dxv = dx_ref[...]
        dy_ref[...] = (dxv * g_ref[...]).astype(BF)
        part = jnp.sum(dxv * y_ref[...].astype(F32), axis=0, keepdims=True)

        @pl.when(i % steps_per_seq == 0)
        def _():
            dg_ref[...] = part

        @pl.when(i % steps_per_seq != 0)
        def _():
            dg_ref[...] += part

    row = pl.BlockSpec((tr, w), lambda i: (i, 0))
    per_b = pl.BlockSpec((None, 1, w), lambda i: ((i * tr) // seq, 0, 0))
    return pl.pallas_call(
        body, name=name, grid=(t // tr,),
        out_shape=(jax.ShapeDtypeStruct((t, w), BF), jax.ShapeDtypeStruct((nb, 1, w), F32)),
        in_specs=[row, row, per_b], out_specs=(row, per_b),
        compiler_params=_params(("arbitrary",)),
    )(dx, y, gate)


def _loss_head(x, gain, target, name):
    t, w = x.shape
    tr = ROW_BLOCK

    def body(x_ref, g_ref, t_ref, loss_ref, dx_ref, dg_ref):
        i = pl.program_id(0)
        xv = x_ref[...]
        g = g_ref[...]
        rstd = lax.rsqrt(jnp.mean(xv * xv, axis=-1, keepdims=True) + NORM_EPS)
        xhat = xv * rstd
        err = xhat * g - t_ref[...]
        row_loss = jnp.sum(err * err, axis=-1, keepdims=True) * (0.5 / w)
        loss_part = jnp.broadcast_to(jnp.sum(row_loss, axis=0, keepdims=True), (1, LANES))
        dy = err * (1.0 / w)
        dg_part = jnp.sum(dy * xhat, axis=0, keepdims=True)
        dxhat = dy * g
        proj = jnp.mean(dxhat * xhat, axis=-1, keepdims=True)
        dx_ref[...] = rstd * (dxhat - xhat * proj)

        @pl.when(i == 0)
        def _():
            loss_ref[...] = loss_part
            dg_ref[...] = dg_part

        @pl.when(i > 0)
        def _():
            loss_ref[...] += loss_part
            dg_ref[...] += dg_part

    row = pl.BlockSpec((tr, w), lambda i: (i, 0))
    vec = pl.BlockSpec((1, w), lambda i: (0, 0))
    return pl.pallas_call(
        body, name=name, grid=(t // tr,),
        out_shape=(jax.ShapeDtypeStruct((1, LANES), F32), jax.ShapeDtypeStruct((t, w), F32),
                   jax.ShapeDtypeStruct((1, w), F32)),
        in_specs=[row, vec, row],
        out_specs=(pl.BlockSpec((1, LANES), lambda i: (0, 0)), row, vec),
        compiler_params=_params(("arbitrary",)),
    )(x, gain, target)


def _rope_group(xg, cos_p, sin_a, sin_b):
    return (xg * cos_p + pltpu.roll(xg, LANES - ROPE_HALF, axis=1) * sin_a
            + pltpu.roll(xg, ROPE_HALF, axis=1) * sin_b)


def _rope(x, tables, name, out_dtype=BF):
    t, w = x.shape
    tr = ROW_BLOCK
    groups = w // LANES

    def body(x_ref, c_ref, a_ref, b_ref, out_ref):
        cos_p, sin_a, sin_b = c_ref[...], a_ref[...], b_ref[...]
        for g in range(groups):
            sl = slice(g * LANES, (g + 1) * LANES)
            out_ref[:, sl] = _rope_group(x_ref[:, sl].astype(F32), cos_p, sin_a, sin_b).astype(out_dtype)

    row = pl.BlockSpec((tr, w), lambda i: (i, 0))
    tab = pl.BlockSpec((tr, LANES), lambda i: (i, 0))
    return pl.pallas_call(
        body, name=name, grid=(t // tr,),
        out_shape=jax.ShapeDtypeStruct((t, w), out_dtype),
        in_specs=[row, tab, tab, tab], out_specs=row,
        compiler_params=_params(("parallel",)),
    )(x, *tables)


def _mla_mid(down, gq, gkv, tables, name):
    t = down.shape[0]
    tr = ROW_BLOCK

    def body(d_ref, gq_ref, gkv_ref, c_ref, a_ref, b_ref, cq_ref, ckr_ref):
        q = d_ref[:, 0:256]
        cq_ref[...] = (q * lax.rsqrt(jnp.mean(q * q, axis=-1, keepdims=True) + NORM_EPS) * gq_ref[...]).astype(BF)
        kv = d_ref[:, 256:384]
        ckr_ref[:, 0:128] = (kv * lax.rsqrt(jnp.mean(kv * kv, axis=-1, keepdims=True) + NORM_EPS)
                             * gkv_ref[...]).astype(BF)
        ckr_ref[:, 128:256] = _rope_group(d_ref[:, 384:512], c_ref[...], a_ref[...], b_ref[...]).astype(BF)

    tab = pl.BlockSpec((tr, LANES), lambda i: (i, 0))
    return pl.pallas_call(
        body, name=name, grid=(t // tr,),
        out_shape=(jax.ShapeDtypeStruct((t, 256), BF), jax.ShapeDtypeStruct((t, 256), BF)),
        in_specs=[pl.BlockSpec((tr, 512), lambda i: (i, 0)), pl.BlockSpec((1, 256), lambda i: (0, 0)),
                  pl.BlockSpec((1, 128), lambda i: (0, 0)), tab, tab, tab],
        out_specs=(pl.BlockSpec((tr, 256), lambda i: (i, 0)), pl.BlockSpec((tr, 256), lambda i: (i, 0))),
        compiler_params=_params(("parallel",)),
    )(down, gq, gkv, *tables)


def _mla_mid_bwd(down, dcq, dckr, gq, gkv, tables_t, name):
    t = down.shape[0]
    tr = ROW_BLOCK

    def norm_bwd(xv, g, dy):
        rstd = lax.rsqrt(jnp.mean(xv * xv, axis=-1, keepdims=True) + NORM_EPS)
        xhat = xv * rstd
        dxhat = dy * g
        proj = jnp.mean(dxhat * xhat, axis=-1, keepdims=True)
        return rstd * (dxhat - xhat * proj), jnp.sum(dy * xhat, axis=0, keepdims=True)

    def body(d_ref, dcq_ref, dckr_ref, gq_ref, gkv_ref, c_ref, a_ref, b_ref, dd_ref, dgq_ref, dgkv_ref):
        i = pl.program_id(0)
        dq, dgq_part = norm_bwd(d_ref[:, 0:256], gq_ref[...], dcq_ref[...].astype(F32))
        dd_ref[:, 0:256] = dq.astype(BF)
        dkv, dgkv_part = norm_bwd(d_ref[:, 256:384], gkv_ref[...], dckr_ref[:, 0:128].astype(F32))
        dd_ref[:, 256:384] = dkv.astype(BF)
        dd_ref[:, 384:512] = _rope_group(dckr_ref[:, 128:256].astype(F32), c_ref[...], a_ref[...],
                                         b_ref[...]).astype(BF)

        @pl.when(i == 0)
        def _():
            dgq_ref[...] = dgq_part
            dgkv_ref[...] = dgkv_part

        @pl.when(i > 0)
        def _():
            dgq_ref[...] += dgq_part
            dgkv_ref[...] += dgkv_part

    tab = pl.BlockSpec((tr, LANES), lambda i: (i, 0))
    r256 = pl.BlockSpec((tr, 256), lambda i: (i, 0))
    return pl.pallas_call(
        body, name=name, grid=(t // tr,),
        out_shape=(jax.ShapeDtypeStruct((t, 512), BF), jax.ShapeDtypeStruct((1, 256), F32),
                   jax.ShapeDtypeStruct((1, 128), F32)),
        in_specs=[pl.BlockSpec((tr, 512), lambda i: (i, 0)), r256, r256, pl.BlockSpec((1, 256), lambda i: (0, 0)),
                  pl.BlockSpec((1, 128), lambda i: (0, 0)), tab, tab, tab],
        out_specs=(pl.BlockSpec((tr, 512), lambda i: (i, 0)), pl.BlockSpec((1, 256), lambda i: (0, 0)),
                   pl.BlockSpec((1, 128), lambda i: (0, 0))),
        compiler_params=_params(("arbitrary",)),
    )(down, dcq, dckr, gq, gkv, *tables_t)


def _scan_rows(x, reverse):
    s = x.shape[0]
    row = lax.broadcasted_iota(jnp.int32, x.shape, 0)
    step = 1
    while step < s:
        if reverse:
            x = x + jnp.where(row < s - step, pltpu.roll(x, s - step, axis=0), 0.0)
        else:
            x = x + jnp.where(row >= step, pltpu.roll(x, step, axis=0), 0.0)
        step *= 2
    return x


def _fox_gate(fg, b_f, seq, name):
    t = fg.shape[0]

    def body(fg_ref, b_ref, out_ref):
        z = fg_ref[...] + b_ref[...]
        log_f = jnp.minimum(z, 0.0) - jnp.log(1.0 + jnp.exp(-jnp.abs(z)))
        out_ref[...] = _scan_rows(log_f, reverse=False)

    blk = pl.BlockSpec((seq, LANES), lambda b: (b, 0))
    return pl.pallas_call(
        body, name=name, grid=(t // seq,),
        out_shape=jax.ShapeDtypeStruct((t, LANES), F32),
        in_specs=[blk, pl.BlockSpec((1, LANES), lambda b: (0, 0))], out_specs=blk,
        compiler_params=_params(("parallel",)),
    )(fg, b_f)


def _fox_gate_bwd(d_cum, fg, b_f, seq, name):
    t = fg.shape[0]

    def body(dc_ref, fg_ref, b_ref, dfg_ref, db_ref):
        b = pl.program_id(0)
        z = fg_ref[...] + b_ref[...]
        d_log_f = _scan_rows(dc_ref[...], reverse=True)
        dz = d_log_f / (1.0 + jnp.exp(z))
        dfg_ref[...] = dz
        part = jnp.sum(dz, axis=0, keepdims=True)

        @pl.when(b == 0)
        def _():
            db_ref[...] = part

        @pl.when(b > 0)
        def _():
            db_ref[...] += part

    blk = pl.BlockSpec((seq, LANES), lambda b: (b, 0))
    vec = pl.BlockSpec((1, LANES), lambda b: (0, 0))
    return pl.pallas_call(
        body, name=name, grid=(t // seq,),
        out_shape=(jax.ShapeDtypeStruct((t, LANES), F32), jax.ShapeDtypeStruct((1, LANES), F32)),
        in_specs=[blk, blk, vec], out_specs=(blk, vec),
        compiler_params=_params(("arbitrary",)),
    )(d_cum, fg, b_f)


def _head_masks():
    lane = lax.broadcasted_iota(jnp.int32, (1, LANES), 1)
    return lane < HEAD_DIM, lane >= HEAD_DIM


def _attn_fwd(q_arr, q_off, k_arr, k_off, v_arr, v_off, bias, seq, name):
    t = q_arr.shape[0]
    nb = t // seq
    blk = min(ATTN_BLOCK, seq)
    nq = seq // blk
    has_bias = bias is not None

    def body(*refs):
        if has_bias:
            q_ref, k_ref, v_ref, bias_ref, o_ref, lse_ref, o32_ref = refs
        else:
            q_ref, k_ref, v_ref, o_ref, lse_ref = refs
        lo, hi = _head_masks()
        rows = lax.broadcasted_iota(jnp.int32, (blk, blk), 0)
        cols = lax.broadcasted_iota(jnp.int32, (blk, blk), 1)
        causal = cols <= rows

        def q_block(iq, _):
            q0 = pl.multiple_of(iq * blk, blk)
            qs = [q_ref[pl.ds(q0, blk), h * LANES:(h + 1) * LANES] for h in range(2)]

            def kv_block(j, carry, diag):
                k0 = pl.multiple_of(j * blk, blk)
                vv = v_ref[pl.ds(k0, blk), :]
                vs = [jnp.where(lo, vv, jnp.zeros_like(vv)), jnp.where(hi, vv, jnp.zeros_like(vv))]
                acc = carry[0]
                new = []
                pv = None
                alphas = []
                for h in range(2):
                    m, l = carry[1 + 2 * h], carry[2 + 2 * h]
                    kk = k_ref[pl.ds(k0, blk), h * LANES:(h + 1) * LANES]
                    s = lax.dot_general(qs[h], kk, _DIMS["nt"], preferred_element_type=F32)
                    if has_bias:
                        s = s + bias_ref[h, 0:1, pl.ds(k0, blk)]
                    if diag:
                        s = jnp.where(causal, s, -jnp.inf)
                    m_new = jnp.maximum(m, jnp.max(s, axis=-1, keepdims=True))
                    p = jnp.exp(s - m_new)
                    alpha = jnp.exp(m - m_new)
                    l_new = alpha * l + jnp.sum(p, axis=-1, keepdims=True)
                    p_hi = p.astype(BF)
                    d = jnp.dot(p_hi, vs[h], preferred_element_type=F32)
                    if has_bias:
                        p_lo = (p - p_hi.astype(F32)).astype(BF)
                        d = d + jnp.dot(p_lo, vs[h], preferred_element_type=F32)
                    pv = d if pv is None else pv + d
                    alphas.append(alpha)
                    new += [m_new, l_new]
                acc = acc * jnp.where(lo, alphas[0], alphas[1]) + pv
                return (acc, *new)

            init = (jnp.zeros((blk, LANES), F32),
                    jnp.full((blk, 1), -jnp.inf, F32), jnp.zeros((blk, 1), F32),
                    jnp.full((blk, 1), -jnp.inf, F32), jnp.zeros((blk, 1), F32))
            carry = lax.fori_loop(0, iq, functools.partial(kv_block, diag=False), init)
            acc, m0, l0, m1, l1 = kv_block(iq, carry, diag=True)
            o_val = acc / jnp.where(lo, l0, l1)
            o_ref[pl.ds(q0, blk), :] = o_val.astype(BF)
            if has_bias:
                o32_ref[pl.ds(q0, blk), :] = o_val
            lse_ref[pl.ds(q0, blk), :] = jnp.where(lo, m0 + jnp.log(l0), m1 + jnp.log(l1))
            return 0

        lax.fori_loop(0, nq, q_block, 0)

    in_specs = [pl.BlockSpec((seq, 2 * LANES), lambda b, p: (b, q_off + p)),
                pl.BlockSpec((seq, 2 * LANES), lambda b, p: (b, k_off + p)),
                pl.BlockSpec((seq, LANES), lambda b, p: (b, v_off + p))]
    args = [q_arr, k_arr, v_arr]
    if has_bias:
        in_specs.append(pl.BlockSpec((None, 2, 8, seq), lambda b, p: (b, p, 0, 0)))
        args.append(bias)
    out_blk = pl.BlockSpec((seq, LANES), lambda b, p: (b, p))
    out_shape = [jax.ShapeDtypeStruct((t, HEAD_PAIRS * LANES), BF), jax.ShapeDtypeStruct((t, HEAD_PAIRS * LANES), F32)]
    if has_bias:
        out_shape.append(jax.ShapeDtypeStruct((t, HEAD_PAIRS * LANES), F32))
    return pl.pallas_call(
        body, name=name, grid=(nb, HEAD_PAIRS),
        out_shape=out_shape, in_specs=in_specs, out_specs=[out_blk] * len(out_shape),
        compiler_params=_params(("parallel", "parallel")),
    )(*args)


def _attn_bwd(q_arr, q_off, k_arr, k_off, v_arr, v_off, bias, o, do, lse, seq, name):
    t = q_arr.shape[0]
    nb = t // seq
    blk = min(ATTN_BLOCK, seq)
    nq = seq // blk
    has_bias = bias is not None

    def body(*refs):
        if has_bias:
            (q_ref, k_ref, v_ref, bias_ref, o_ref, do_ref, lse_ref,
             dq_ref, dk_ref, dv_ref, dbias_ref, dq_acc, dsum) = refs
        else:
            (q_ref, k_ref, v_ref, o_ref, do_ref, lse_ref, dq_ref, dk_ref, dv_ref, dq_acc, dsum) = refs
        lo, hi = _head_masks()
        rows = lax.broadcasted_iota(jnp.int32, (blk, blk), 0)
        cols = lax.broadcasted_iota(jnp.int32, (blk, blk), 1)
        causal = cols <= rows
        dq_acc[...] = jnp.zeros_like(dq_acc)

        def prep(iq, _):
            q0 = pl.multiple_of(iq * blk, blk)
            prod = do_ref[pl.ds(q0, blk), :].astype(F32) * o_ref[pl.ds(q0, blk), :].astype(F32)
            d0 = jnp.sum(jnp.where(lo, prod, 0.0), axis=-1, keepdims=True)
            d1 = jnp.sum(jnp.where(hi, prod, 0.0), axis=-1, keepdims=True)
            dsum[pl.ds(q0, blk), :] = jnp.where(lo, d0, d1)
            return 0

        lax.fori_loop(0, nq, prep, 0)

        def kv_block(j, _):
            k0 = pl.multiple_of(j * blk, blk)
            vv = v_ref[pl.ds(k0, blk), :]
            vs = [jnp.where(lo, vv, jnp.zeros_like(vv)), jnp.where(hi, vv, jnp.zeros_like(vv))]
            ks = [k_ref[pl.ds(k0, blk), h * LANES:(h + 1) * LANES] for h in range(2)]

            def q_block(iq, carry, diag):
                q0 = pl.multiple_of(iq * blk, blk)
                dov = do_ref[pl.ds(q0, blk), :]
                lse_v = lse_ref[pl.ds(q0, blk), :]
                dsum_v = dsum[pl.ds(q0, blk), :]
                dv_acc = carry[0]
                out = []
                for h in range(2):
                    dk_acc, db_acc = carry[1 + 2 * h], carry[2 + 2 * h]
                    qq = q_ref[pl.ds(q0, blk), h * LANES:(h + 1) * LANES]
                    s = lax.dot_general(qq, ks[h], _DIMS["nt"], preferred_element_type=F32)
                    if has_bias:
                        s = s + bias_ref[h, 0:1, pl.ds(k0, blk)]
                    p = jnp.exp(s - lse_v[:, h * HEAD_DIM:h * HEAD_DIM + 1])
                    if diag:
                        p = jnp.where(causal, p, 0.0)
                    dp = lax.dot_general(dov, vs[h], _DIMS["nt"], preferred_element_type=F32)
                    ds = p * (dp - dsum_v[:, h * HEAD_DIM:h * HEAD_DIM + 1])
                    ds_bf = ds.astype(BF)
                    pt_do = lax.dot_general(p.astype(BF), dov, _DIMS["tn"], preferred_element_type=F32)
                    dv_acc = dv_acc + jnp.where(hi if h else lo, pt_do, 0.0)
                    dk_acc = dk_acc + lax.dot_general(ds_bf, qq, _DIMS["tn"], preferred_element_type=F32)
                    dq_acc[pl.ds(q0, blk), h * LANES:(h + 1) * LANES] += jnp.dot(
                        ds_bf, ks[h], preferred_element_type=F32)
                    if has_bias:
                        db_acc = db_acc + jnp.sum(ds, axis=0, keepdims=True)
                    out += [dk_acc, db_acc]
                return (dv_acc, *out)

            init = (jnp.zeros((blk, LANES), F32),
                    jnp.zeros((blk, LANES), F32), jnp.zeros((1, blk), F32),
                    jnp.zeros((blk, LANES), F32), jnp.zeros((1, blk), F32))
            carry = q_block(j, init, diag=True)
            carry = lax.fori_loop(j + 1, nq, functools.partial(q_block, diag=False), carry)
            dv_ref[pl.ds(k0, blk), :] = carry[0].astype(BF)
            for h in range(2):
                dk_ref[pl.ds(k0, blk), h * LANES:(h + 1) * LANES] = carry[1 + 2 * h].astype(BF)
                if has_bias:
                    dbias_ref[h, :, pl.ds(k0, blk)] = jnp.broadcast_to(carry[2 + 2 * h], (8, blk))
            return 0

        lax.fori_loop(0, nq, kv_block, 0)
        dq_ref[...] = dq_acc[...].astype(BF)

    pair256 = lambda off: pl.BlockSpec((seq, 2 * LANES), lambda b, p: (b, off + p))
    pair128 = lambda off: pl.BlockSpec((seq, LANES), lambda b, p: (b, off + p))
    bias_spec = pl.BlockSpec((None, 2, 8, seq), lambda b, p: (b, p, 0, 0))
    in_specs = [pair256(q_off), pair256(k_off), pair128(v_off)]
    args = [q_arr, k_arr, v_arr]
    if has_bias:
        in_specs.append(bias_spec)
        args.append(bias)
    in_specs += [pair128(0), pair128(0), pair128(0)]
    args += [o, do, lse]
    out_shape = [jax.ShapeDtypeStruct((t, HEAD_PAIRS * 2 * LANES), BF),
                 jax.ShapeDtypeStruct((t, HEAD_PAIRS * 2 * LANES), BF),
                 jax.ShapeDtypeStruct((t, HEAD_PAIRS * LANES), BF)]
    out_specs = [pair256(0), pair256(0), pair128(0)]
    if has_bias:
        out_shape.append(jax.ShapeDtypeStruct((nb, HEADS, 8, seq), F32))
        out_specs.append(bias_spec)
    return pl.pallas_call(
        body, name=name, grid=(nb, HEAD_PAIRS),
        out_shape=out_shape, in_specs=in_specs, out_specs=out_specs,
        scratch_shapes=[pltpu.VMEM((seq, 2 * LANES), F32), pltpu.VMEM((seq, LANES), F32)],
        compiler_params=_params(("parallel", "parallel")),
    )(*args)


def _adamw(w, g, m, v, name):
    shape = w.shape
    last = shape[-1]
    rows = int(np.prod(shape[:-1])) if len(shape) > 1 else 1
    tr = _rows(rows, 512)

    def body(w_ref, g_ref, m_ref, v_ref, d_ref, nm_ref, nv_ref):
        d_ref[...], nm_ref[...], nv_ref[...] = _adamw_math(w_ref[...], g_ref[...], m_ref[...], v_ref[...])

    blk = pl.BlockSpec((tr, last), lambda i: (i, 0))
    sds = jax.ShapeDtypeStruct((rows, last), F32)
    outs = pl.pallas_call(
        body, name=name, grid=(rows // tr,),
        out_shape=(sds, sds, sds), in_specs=[blk] * 4, out_specs=(blk,) * 3,
        compiler_params=_params(("parallel",)),
    )(*[a.reshape(rows, last) for a in (w, g, m, v)])
    return tuple(a.reshape(shape) for a in outs)


LOW_COLS = 256


def _low_pad(a):
    return jnp.pad(a, ((0, 0),) * (a.ndim - 1) + ((0, LOW_COLS - a.shape[-1]),))


def _pack_mixer_shards(w):
    rows = lambda a: a.astype(BF).reshape(-1, a.shape[-1])
    out = jnp.concatenate([rows(w["fox_w_out"]), rows(w["mla_w_out"])], axis=0)
    low = jnp.concatenate([rows(w["mla_w_dq"]), rows(w["mla_w_ukv"]), _low_pad(rows(w["mla_w_uq"])),
                           _low_pad(rows(w["mla_w_dkv"]))], axis=0)
    return out, rows(w["fox_w_in"]), low


def _side_by_side(stack, r0, rows, cols=None):
    return jnp.concatenate([stack[dd, r0:r0 + rows, :cols] for dd in range(N_DEV)], axis=1)


def _stacked(stack, r0, rows, cols=None):
    part = stack[:, r0:r0 + rows, :cols]
    return part.reshape(N_DEV * rows, part.shape[2])


def _unpack_mixer_weights(out_all, in_all, low_all, j):
    return dict(
        fox_w_out=_stacked(out_all, 128 * j, 128), mla_w_out=_stacked(out_all, 256 + 128 * j, 128),
        fox_w_in=_side_by_side(in_all, 1024 * j, 1024),
        mla_w_dq=_stacked(low_all, 128 * j, 128), mla_w_ukv=_side_by_side(low_all, 256 + 128 * j, 128),
        mla_w_uq=_side_by_side(low_all, 512 + 256 * j, 256, 192), mla_w_dkv=_stacked(low_all, 1024 + 128 * j, 128, 160))


def _by_dest_rows(g):
    return g.reshape(N_DEV, g.shape[0] // N_DEV, g.shape[1]).astype(BF)


def _by_dest_cols(g):
    n = g.shape[1] // N_DEV
    return jnp.stack([g[:, dd * n:(dd + 1) * n] for dd in range(N_DEV)]).astype(BF)


def _pack_mixer_grads(g):
    cat = lambda parts: jnp.concatenate(parts, axis=1)
    out = cat([_by_dest_rows(a) for a in g["fox_w_out"] + g["mla_w_out"]])
    inn = cat([_by_dest_cols(a) for a in g["fox_w_in"]])
    low = cat([_by_dest_rows(a) for a in g["mla_w_dq"]] + [_by_dest_cols(a) for a in g["mla_w_ukv"]]
              + [_low_pad(_by_dest_cols(a)) for a in g["mla_w_uq"]] + [_low_pad(_by_dest_rows(a)) for a in g["mla_w_dkv"]])
    return out, inn, low


def _unpack_mixer_shard_grads(out, low):
    return dict(
        fox_w_out=out[:256].reshape(2, 128, 1024), mla_w_out=out[256:].reshape(2, 128, 1024),
        mla_w_dq=low[:256].reshape(2, 128, 256), mla_w_ukv=low[256:512].reshape(2, 128, 256),
        mla_w_uq=low[512:1024, :192].reshape(2, 256, 192), mla_w_dkv=low[1024:, :160].reshape(2, 128, 160))


def _pad_heads(w, width):
    k = w.shape[0]
    return jnp.pad(w.reshape(k, HEADS, width), ((0, 0), (0, 0), (0, LANES - width))).reshape(k, HEADS * LANES)


def _unpad_heads(w, width):
    k = w.shape[0]
    return w.reshape(k, HEADS, LANES)[:, :, :width].reshape(k, HEADS * width)


def _rope_tables(positions, scale):
    inv_freq = 10000.0 ** (-jnp.arange(0, 2 * ROPE_HALF, 2, dtype=F32) / (2 * ROPE_HALF))
    ang = positions.astype(F32)[:, None] * inv_freq
    cos, sin = jnp.cos(ang) * scale, jnp.sin(ang) * scale
    t = positions.shape[0]
    z = lambda n: jnp.zeros((t, n), F32)
    cos_p = jnp.concatenate([jnp.full((t, HEAD_DIM), scale, F32), cos, cos, z(32)], axis=1)
    sin_a = jnp.concatenate([z(64), -sin, z(48)], axis=1)
    sin_b = jnp.concatenate([z(80), sin, z(32)], axis=1)
    fwd = (cos_p, sin_a, sin_b)
    bwd = (cos_p, jnp.roll(sin_b, -ROPE_HALF, axis=1), jnp.roll(sin_a, ROPE_HALF, axis=1))
    return fwd, bwd


def _key_rows(cum, nb, seq):
    v = -cum.reshape(nb, seq, LANES)[:, :, :HEADS]
    return jnp.broadcast_to(jnp.transpose(v, (0, 2, 1))[:, :, None, :], (nb, HEADS, 8, seq))


def kernel(x, c, positions, ada_w, ada_b, norm_mix_g, norm_mlp_g, fox_w_in, fox_b_f, fox_w_out, mla_w_dq, mla_q_norm_g, mla_w_uq, mla_w_dkv, mla_kv_norm_g, mla_w_ukv, mla_w_out, mlp_w1, mlp_w2, final_norm_g, loss_target, m_ada_w, m_ada_b, m_norm_mix_g, m_norm_mlp_g, m_fox_w_in, m_fox_b_f, m_fox_w_out, m_mla_w_dq, m_mla_q_norm_g, m_mla_w_uq, m_mla_w_dkv, m_mla_kv_norm_g, m_mla_w_ukv, m_mla_w_out, m_mlp_w1, m_mlp_w2, m_final_norm_g, v_ada_w, v_ada_b, v_norm_mix_g, v_norm_mlp_g, v_fox_w_in, v_fox_b_f, v_fox_w_out, v_mla_w_dq, v_mla_q_norm_g, v_mla_w_uq, v_mla_w_dkv, v_mla_kv_norm_g, v_mla_w_ukv, v_mla_w_out, v_mlp_w1, v_mlp_w2, v_final_norm_g):
    args = dict(locals())
    weights = {n: args[n] for n in WEIGHTS}
    nb, seq, d = x.shape
    t = nb * seq
    depth = ada_w.shape[0]
    dev = 4 * lax.axis_index("x") + 2 * lax.axis_index("y") + lax.axis_index("c")
    n_mod_local = ada_w.shape[2]

    n_qg = mla_q_norm_g.shape[1]
    cond = jnp.concatenate([c, jnp.pad(mla_q_norm_g.reshape(1, -1), ((0, 7), (0, d - 2 * n_qg)))], axis=0)
    w1_rows, w2_rows = mlp_w1.shape[1], mlp_w2.shape[1]
    out_all, in_all, low_all, w1_all, w2_all, cond_all = _all_gather(
        [*_pack_mixer_shards(weights), mlp_w1.astype(BF).reshape(depth * w1_rows, -1),
         mlp_w2.astype(BF).reshape(depth * w2_rows, -1), cond], "gather_weights")
    full = [_unpack_mixer_weights(out_all, in_all, low_all, j) for j in range(2)]
    c_all = cond_all[:, :nb].reshape(N_DEV * nb, d)
    q_gain = jnp.transpose(cond_all[:, nb, :2 * n_qg].reshape(N_DEV, 2, n_qg), (1, 0, 2)).reshape(2, N_DEV * n_qg)
    mod_local = jnp.stack([
        _matmul(c_all, ada_w[i], mode="nn", name="ada_mod", out_dtype=F32, a_act="silu", epi="bias",
                extras=(lax.dynamic_slice_in_dim(ada_b[i], dev * n_mod_local, n_mod_local)[None, :],))
        for i in range(depth)])
    mod_all, = _all_gather([mod_local.reshape(depth * N_DEV * nb, n_mod_local)], "gather_mod")
    mod_all = jnp.transpose(mod_all.reshape(N_DEV, depth, N_DEV * nb, n_mod_local), (1, 2, 0, 3))
    mod_all = mod_all.reshape(depth, N_DEV * nb, N_DEV * n_mod_local)
    mod = lax.dynamic_slice_in_dim(mod_all, dev * nb, nb, axis=1)
    mod = mod.reshape(depth, nb, 6, 1, d)

    pos = positions.reshape(t)
    rope_q, rope_q_t = _rope_tables(pos, MLA_SCALE)
    rope_k, rope_k_t = _rope_tables(pos, 1.0)

    def fox_weights(j):
        w_in = full[j]["fox_w_in"]
        wq = _pad_heads(w_in[:, :d] * FOX_SCALE, HEAD_DIM)
        wk = _pad_heads(w_in[:, d:2 * d], HEAD_DIM)
        w_qkv = jnp.concatenate([wq, wk, w_in[:, 2 * d:3 * d]], axis=1)
        w_f = jnp.pad(w_in[:, 3 * d:], ((0, 0), (0, LANES - HEADS)))
        return w_qkv, w_f

    def mla_weights(j):
        w_dkv = full[j]["mla_w_dkv"]
        w_down = jnp.concatenate([full[j]["mla_w_dq"], w_dkv[:, :128], jnp.zeros((d, 64), BF),
                                  w_dkv[:, 128:160], jnp.zeros((d, 32), BF)], axis=1)
        w_uq = _pad_heads(full[j]["mla_w_uq"], 96)
        w_ukv = full[j]["mla_w_ukv"].reshape(128, HEADS, 2, HEAD_DIM)
        w_uk = jnp.pad(w_ukv[:, :, 0, :], ((0, 0), (0, 0), (0, 64))).reshape(128, HEADS * LANES)
        w_uv = w_ukv[:, :, 1, :].reshape(128, HEADS * HEAD_DIM)
        place = np.zeros((128, HEADS, LANES), np.float32)
        for i in range(2 * ROPE_HALF):
            place[64 + i, :, 64 + i] = 1.0
        bottom = jnp.concatenate([jnp.asarray(place.reshape(128, HEADS * LANES), BF),
                                  jnp.zeros((128, HEADS * HEAD_DIM), BF)], axis=1)
        w_kv = jnp.concatenate([jnp.concatenate([w_uk, w_uv], axis=1), bottom], axis=0)
        return w_down, w_uq, w_kv

    xs = x.reshape(t, d)
    saved = []
    for i in range(depth):
        j = i // 2
        sh_m, sc_m, g_m, sh_f, sc_f, g_f = (mod[i, :, q] for q in range(6))
        gain_mix = norm_mix_g[i][None, :]
        gain_mlp = norm_mlp_g[i][None, :]
        s = dict(x_in=xs)
        h = _norm_mod(xs, gain_mix, sc_m, sh_m, seq, "norm_mix")
        s["h"] = h
        if i % 2 == 0:
            w_qkv, w_f = fox_weights(j)
            qkv = _matmul(h, w_qkv, mode="nn", name="fox_qkv")
            fg = _matmul(h, w_f, mode="nn", name="fox_gate_logits", out_dtype=F32)
            b_f = jnp.pad(fox_b_f[j], (0, LANES - HEADS))[None, :]
            cum = _fox_gate(fg, b_f, seq, "fox_gate")
            bias = _key_rows(cum, nb, seq)
            o, lse, o32 = _attn_fwd(qkv, 0, qkv, 8, qkv, 32, bias, seq, "fox_attn")
            s.update(qkv=qkv, fg=fg, b_f=b_f, bias=bias, w_qkv=w_qkv, w_f=w_f, o32=o32)
            w_out = full[j]["fox_w_out"]
        else:
            w_down, w_uq, w_kv = mla_weights(j)
            down = _matmul(h, w_down, mode="nn", name="mla_down", out_dtype=F32)
            gq, gkv = q_gain[j][None, :], mla_kv_norm_g[j][None, :]
            cq, ckr = _mla_mid(down, gq, gkv, rope_k, "mla_mid")
            q_raw = _matmul(cq, w_uq, mode="nn", name="mla_uq", out_dtype=F32)
            q_rot = _rope(q_raw, rope_q, "mla_rope_q")
            kv = _matmul(ckr, w_kv, mode="nn", name="mla_ukv")
            o, lse = _attn_fwd(q_rot, 0, kv, 0, kv, 16, None, seq, "mla_attn")
            s.update(down=down, gq=gq, gkv=gkv, cq=cq, ckr=ckr, q_rot=q_rot, kv=kv,
                     w_down=w_down, w_uq=w_uq, w_kv=w_kv)
            w_out = full[j]["mla_w_out"]
        xs, y = _matmul(o, w_out, mode="nn", name="attn_out", epi="resid_gate", extras=(xs, g_m), seq=seq)
        s.update(o=o, lse=lse, y=y, w_out=w_out, x_mid=xs)
        h2 = _norm_mod(xs, gain_mlp, sc_f, sh_f, seq, "norm_mlp")
        a_pre = _matmul(h2, w1_all, mode="nn", name="mlp_up", layer=("col", i, w1_rows))
        xs, y2 = _matmul(a_pre, w2_all, mode="nn", name="mlp_down", layer=("row", i, w2_rows), a_act="relu2",
                         epi="resid_gate", extras=(xs, g_f), seq=seq)
        s.update(h2=h2, a_pre=a_pre, y2=y2)
        saved.append(s)

    loss_part, dx, dg_final = _loss_head(xs, final_norm_g[None, :], loss_target.reshape(t, d), "loss_head")

    grads = {n: [None] * weights[n].shape[0] for n in MIXER_WEIGHTS}
    w1_cols, w1_tm = mlp_w1.shape[2], _pick(w1_rows, 1024)
    g_w1 = jnp.zeros((N_DEV, depth * w1_rows, w1_cols), BF)
    g_w2 = jnp.zeros((N_DEV, depth * w2_rows, d), BF)
    dg_mix, dg_mlp, db_f, dg_kv, dg_q = [None] * depth, [None] * depth, [None] * 2, [None] * 2, [None] * 2
    dmod = [None] * depth
    for i in reversed(range(depth)):
        j = i // 2
        s = saved[i]
        sh_m, sc_m, g_m, sh_f, sc_f, g_f = (mod[i, :, q] for q in range(6))
        dy2, dg_f = _gate_bwd(dx, s["y2"], g_f, seq, "gate_bwd")
        da_pre = _matmul(dy2, w2_all, mode="nt", name="mlp_down_dx", layer=("row", i, w2_rows), epi="mul_drelu",
                         extras=(s["a_pre"],))
        g_w2 = _matmul(s["a_pre"], dy2, mode="tn", name="mlp_down_dw", a_act="relu2", tm=w2_rows,
                       into=(g_w2, (None, w2_rows, d), lambda r, j, k, li=i: (r, li, 0)))
        dh2 = _matmul(da_pre, w1_all, mode="nt", name="mlp_up_dx", layer=("col", i, w1_rows))
        g_w1 = _matmul(s["h2"], da_pre, mode="tn", name="mlp_up_dw", tm=w1_tm, tn=w1_cols,
                       into=(g_w1, (None, w1_tm, w1_cols), lambda r, j, k, li=i: (j, (w1_rows // w1_tm) * li + r, 0)))
        dx, dg_mlp[i], dsc_f, dsh_f = _norm_mod_bwd(dh2, s["x_mid"], norm_mlp_g[i][None, :], sc_f, dx, seq,
                                                    "norm_bwd")
        dy, dg_m = _gate_bwd(dx, s["y"], g_m, seq, "gate_bwd")
        do = _matmul(dy, s["w_out"], mode="nt", name="attn_out_dx")
        dw_out = _matmul(s["o"], dy, mode="tn", name="attn_out_dw", out_dtype=F32)
        if i % 2 == 0:
            qkv = s["qkv"]
            dq, dk, dv, dbias = _attn_bwd(qkv, 0, qkv, 8, qkv, 32, s["bias"], s["o32"], do, s["lse"], seq,
                                          "fox_attn_bwd")
            dqkv = jnp.concatenate([dq, dk, dv], axis=1)
            d_cum = -jnp.transpose(dbias[:, :, 0, :], (0, 2, 1)).reshape(t, HEADS)
            d_cum = jnp.pad(d_cum, ((0, 0), (0, LANES - HEADS)))
            dfg, db = _fox_gate_bwd(d_cum, s["fg"], s["b_f"], seq, "fox_gate_bwd")
            db_f[j] = db
            dh = _matmul(dfg, s["w_f"], mode="nt", name="fox_gate_dx", out_dtype=F32)
            dh = _matmul(dqkv, s["w_qkv"], mode="nt", name="fox_qkv_dx", epi="add", extras=(dh,))
            dw_qkv = _matmul(s["h"], dqkv, mode="tn", name="fox_qkv_dw", out_dtype=F32)
            dw_f = _matmul(s["h"], dfg, mode="tn", name="fox_gate_dw", out_dtype=F32)
            grads["fox_w_in"][j] = jnp.concatenate(
                [_unpad_heads(dw_qkv[:, :2048], HEAD_DIM) * FOX_SCALE, _unpad_heads(dw_qkv[:, 2048:4096], HEAD_DIM),
                 dw_qkv[:, 4096:], dw_f[:, :HEADS]], axis=1)
            grads["fox_w_out"][j] = dw_out
        else:
            kv = s["kv"]
            dq, dk, dv = _attn_bwd(s["q_rot"], 0, kv, 0, kv, 16, None, s["o"], do, s["lse"], seq, "mla_attn_bwd")
            dq_raw = _rope(dq, rope_q_t, "mla_rope_q_bwd")
            dcq = _matmul(dq_raw, s["w_uq"], mode="nt", name="mla_uq_dx")
            dw_uq = _matmul(s["cq"], dq_raw, mode="tn", name="mla_uq_dw", out_dtype=F32)
            dkv = jnp.concatenate([dk, dv], axis=1)
            dckr = _matmul(dkv, s["w_kv"], mode="nt", name="mla_ukv_dx")
            dw_kv = _matmul(s["ckr"], dkv, mode="tn", name="mla_ukv_dw", out_dtype=F32)
            d_down, dgq, dgkv = _mla_mid_bwd(s["down"], dcq, dckr, s["gq"], s["gkv"], rope_k_t, "mla_mid_bwd")
            dg_q[j], dg_kv[j] = dgq, dgkv
            dh = _matmul(d_down, s["w_down"], mode="nt", name="mla_down_dx")
            dw_down = _matmul(s["h"], d_down, mode="tn", name="mla_down_dw", out_dtype=F32)
            grads["mla_w_dq"][j] = dw_down[:, :256]
            grads["mla_w_dkv"][j] = jnp.concatenate([dw_down[:, 256:384], dw_down[:, 448:480]], axis=1)
            grads["mla_w_uq"][j] = _unpad_heads(dw_uq, 96)
            dk_nope = dw_kv[:128, :HEADS * LANES].reshape(128, HEADS, LANES)[:, :, :HEAD_DIM]
            dv_w = dw_kv[:128, HEADS * LANES:].reshape(128, HEADS, HEAD_DIM)
            grads["mla_w_ukv"][j] = jnp.concatenate([dk_nope, dv_w], axis=2).reshape(128, HEADS * LANES)
            grads["mla_w_out"][j] = dw_out
        dx, dg_mix[i], dsc_m, dsh_m = _norm_mod_bwd(dh, s["x_in"], norm_mix_g[i][None, :], sc_m, dx, seq,
                                                    "norm_bwd")
        dmod[i] = jnp.stack([dsh_m, dsc_m, dg_m, dsh_f, dsc_f, dg_f], axis=1).reshape(nb, 6 * d)

    grad_x = dx.reshape(nb, seq, d)

    bufs = [*_pack_mixer_grads(grads), g_w1, g_w2]
    from_sibling = _swap_with_sibling(bufs, "grads_to_sibling")
    rel = _relative_blocks()
    partial = [_chip_partial(b, fs, rel, "grads_chip_sum") for b, fs in zip(bufs, from_sibling)]
    from_chips = _swap_with_chips([p[1] for p in partial], "grads_to_chips")
    shard_grads = _unpack_mixer_shard_grads(_add_parts(partial[0][0], from_chips[0], "grads_total"),
                                            _add_parts(partial[2][0], from_chips[2], "grads_total"))
    done = {"fox_w_in": _total_adamw(partial[1][0], from_chips[1], fox_w_in, m_fox_w_in, v_fox_w_in, "adamw_in"),
            "mlp_w1": _total_adamw(partial[3][0], from_chips[3], mlp_w1, m_mlp_w1, v_mlp_w1, "adamw_w1"),
            "mlp_w2": _total_adamw(partial[4][0], from_chips[4], mlp_w2, m_mlp_w2, v_mlp_w2, "adamw_w2")}

    dmod_arr = jnp.stack(dmod)
    wide = lambda a: jnp.pad(a, ((0, 0), (0, d - a.shape[1])))
    pieces = [wide(loss_part), *dg_mix, *dg_mlp, *[wide(a) for a in db_f], *[wide(a) for a in dg_kv], dg_final,
              *[wide(a) for a in dg_q], jnp.sum(dmod_arr, axis=1).reshape(depth * 6, d)]
    n_small = sum(p.shape[0] for p in pieces)
    both = jnp.concatenate(pieces + [dmod_arr.reshape(depth * nb * 6, d)], axis=0)
    both = jnp.pad(both, ((0, (-both.shape[0]) % 8), (0, 0)))
    both_all, = _all_gather([both], "gather_small")
    total = _sum_leading(both_all, "sum_small")
    off = 0

    def take(rows):
        nonlocal off
        out = total[off:off + rows]
        off += rows
        return out

    loss = take(1)[0, 0]
    g_small = dict(
        norm_mix_g=take(depth), norm_mlp_g=take(depth), fox_b_f=take(2)[:, :HEADS], mla_kv_norm_g=take(2)[:, :128],
        final_norm_g=take(1)[0],
        mla_q_norm_g=lax.dynamic_slice_in_dim(take(2)[:, :N_DEV * n_qg], dev * n_qg, n_qg, axis=1),
        ada_b=take(depth * 6).reshape(depth, 6 * d))
    dmod_all = both_all[:, n_small:n_small + depth * nb * 6]
    dmod_all = jnp.transpose(dmod_all.reshape(N_DEV, depth, nb, 6 * d), (1, 0, 2, 3)).reshape(depth, N_DEV * nb, 6 * d)
    dmod_cols = lax.dynamic_slice_in_dim(dmod_all, dev * n_mod_local, n_mod_local, axis=2)
    g_ada_w = jnp.stack([_matmul(c_all, dmod_cols[i], mode="tn", name="ada_dw", out_dtype=F32, a_act="silu")
                         for i in range(depth)])

    all_grads = dict(shard_grads)
    all_grads.update(g_small)
    all_grads["ada_w"] = g_ada_w

    deltas, new_m, new_v = {}, {}, {}
    for n in WEIGHTS:
        if n in done:
            all_grads[n], deltas[n], new_m[n], new_v[n] = done[n]
        else:
            deltas[n], new_m[n], new_v[n] = _adamw(weights[n], all_grads[n], args["m_" + n], args["v_" + n], "adamw")

    return (loss, grad_x, *[all_grads[n] for n in WEIGHTS], *[deltas[n] for n in WEIGHTS],
            *[new_m[n] for n in WEIGHTS], *[new_v[n] for n in WEIGHTS])
```

```python
import functools
import math

import jax
import jax.numpy as jnp
import numpy as np
from jax import lax
from jax.experimental import pallas as pl
from jax.experimental.pallas import tpu as pltpu

F32 = jnp.float32
BF = jnp.bfloat16

N_DEV = 8
HEADS = 16
HEAD_PAIRS = HEADS // 2
HEAD_DIM = 64
LANES = 128
ROPE_HALF = 16
NORM_EPS = 1e-6
MLA_SCALE = 96.0 ** -0.5
FOX_SCALE = 0.125
ATTN_BLOCK = 512
ROW_BLOCK = 512
VMEM_LIMIT = 56 * 1024 * 1024
MESH = pl.DeviceIdType.MESH

ADAM_LR = 0.001
ADAM_B1 = 0.9
ADAM_B2 = 0.999
ADAM_EPS = 1e-08
ADAM_WD = 0.01
ADAM_STEP = 10

WEIGHTS = ("ada_w", "ada_b", "norm_mix_g", "norm_mlp_g", "fox_w_in", "fox_b_f", "fox_w_out", "mla_w_dq",
           "mla_q_norm_g", "mla_w_uq", "mla_w_dkv", "mla_kv_norm_g", "mla_w_ukv", "mla_w_out", "mlp_w1",
           "mlp_w2", "final_norm_g")


def _params(sem=None):
    return pltpu.CompilerParams(dimension_semantics=sem, vmem_limit_bytes=VMEM_LIMIT)


def _pick(n, target):
    if n <= target:
        return n
    for t in range(target, 127, -128):
        if n % t == 0:
            return t
    return n


def _rows(n, target=512):
    if n <= target:
        return n
    for t in range(target, 7, -8):
        if n % t == 0:
            return t
    return n


def _place():
    x, y, c = lax.axis_index("x"), lax.axis_index("y"), lax.axis_index("c")
    return x, y, c


def _adamw_math(w, g, m, v):
    nm = ADAM_B1 * m + (1.0 - ADAM_B1) * g
    nv = ADAM_B2 * v + (1.0 - ADAM_B2) * (g * g)
    m_hat = nm * (1.0 / (1.0 - ADAM_B1 ** ADAM_STEP))
    v_hat = nv * (1.0 / (1.0 - ADAM_B2 ** ADAM_STEP))
    return -ADAM_LR * (m_hat / (jnp.sqrt(v_hat) + ADAM_EPS) + ADAM_WD * w), nm, nv


def _all_gather(blocks, name):
    ride = ("gather", blocks)

    def body(*refs):
        start, mid, finish = _ride_phases(ride, *_ride_split(ride, refs, 0, 0)[:3])
        start()
        mid()
        finish()

    in_specs, out_shape, out_specs, scratch = _ride_specs(ride)
    return pl.pallas_call(
        body, name=name, out_shape=out_shape, in_specs=in_specs, out_specs=out_specs, scratch_shapes=scratch,
    )(*blocks)


def _ride_specs(ride):
    kind, arrays = ride
    n = len(arrays)
    any_spec = pl.BlockSpec(memory_space=pl.ANY)
    if kind == "gather":
        out_shape = [jax.ShapeDtypeStruct((N_DEV,) + b.shape, b.dtype) for b in arrays]
        scratch = [pltpu.SemaphoreType.DMA((7 * n,)), pltpu.SemaphoreType.DMA((7 * n,)), pltpu.SemaphoreType.DMA((n,))]
    else:
        out_shape = [jax.ShapeDtypeStruct(p.shape, p.dtype) for p in arrays]
        scratch = [pltpu.SemaphoreType.DMA((3 * n,)), pltpu.SemaphoreType.DMA((3 * n,))]
    return [any_spec] * n, out_shape, [any_spec] * n, scratch


def _ride_split(ride, refs, n_in, n_out):
    n = len(ride[1])
    n_sem = 3 if ride[0] == "gather" else 2
    src = refs[n_in:n_in + n]
    dst = refs[n_in + n + n_out:n_in + 2 * n + n_out]
    own = refs[:n_in] + refs[n_in + n:n_in + n + n_out] + refs[n_in + 2 * n + n_out:len(refs) - n_sem]
    return src, dst, refs[len(refs) - n_sem:], own


def _ride_phases(ride, src, dst, sems):
    n = len(src)
    x, y, c = _place()
    chips = [(1 - x, y), (x, 1 - y), (1 - x, 1 - y)]
    if ride[0] == "swap":
        send_sems, recv_sems = sems

        def copies():
            return [pltpu.make_async_remote_copy(
                src_ref=src[a].at[k], dst_ref=dst[a].at[k], send_sem=send_sems.at[3 * a + k],
                recv_sem=recv_sems.at[3 * a + k], device_id=(*chip, c), device_id_type=MESH)
                for k, chip in enumerate(chips) for a in range(n)]

        def start():
            for cp in copies():
                cp.start()

        def finish():
            for cp in copies():
                cp.wait()

        return start, lambda: None, finish

    send_sems, recv_sems, local_sems = sems
    me, sibling = (x, y, c), (x, y, 1 - c)

    def slot(a, px, py, pc):
        return dst[a].at[4 * px + 2 * py + pc]

    def copy(a, k, blk, to, from_src=False):
        return pltpu.make_async_remote_copy(
            src_ref=src[a] if from_src else slot(a, *blk), dst_ref=slot(a, *blk),
            send_sem=send_sems.at[7 * a + k], recv_sem=recv_sems.at[7 * a + k], device_id=to, device_id_type=MESH)

    def mine():
        return [pltpu.make_async_copy(src[a], slot(a, *me), local_sems.at[a]) for a in range(n)]

    def first():
        out = []
        for j, chip in enumerate(chips):
            out += [copy(a, 1 + j, me, (*chip, c), from_src=True) for a in range(n)]
        return out + [copy(a, 0, me, sibling, from_src=True) for a in range(n)]

    def passed():
        return [copy(a, 4 + j, (*chip, c), sibling) for j, chip in enumerate(chips) for a in range(n)]

    def start():
        for cp in mine() + first():
            cp.start()

    def mid():
        for j, chip in enumerate(chips):
            for a in range(n):
                copy(a, 1 + j, (*chip, c), me).wait_recv()
        for cp in passed():
            cp.start()

    def finish():
        for a in range(n):
            copy(a, 0, sibling, me).wait_recv()
        for j, chip in enumerate(chips):
            for a in range(n):
                copy(a, 4 + j, (*chip, 1 - c), me).wait_recv()
        for cp in first() + passed():
            cp.wait_send()
        for cp in mine():
            cp.wait()

    return start, mid, finish


def _ride_steps(ride, refs, n_in, n_out, step, n_steps):
    if ride is None:
        return refs, lambda: None
    src, dst, sems, own = _ride_split(ride, refs, n_in, n_out)
    start, mid, finish = _ride_phases(ride, src, dst, sems)
    pl.when(step == 0)(start)
    pl.when(step == (3 * n_steps) // 4)(mid)
    return own, lambda: pl.when(step == n_steps - 1)(finish)


def _swap_with_sibling(bufs, name):
    n = len(bufs)

    def body(*refs):
        src, dst = refs[:n], refs[n:2 * n]
        send_sems, recv_sems = refs[2 * n:]
        x, y, c = _place()
        cps = []
        for a in range(n):
            for k in range(4):
                cps.append(pltpu.make_async_remote_copy(
                    src_ref=src[a].at[2 * k + (1 - c)], dst_ref=dst[a].at[k],
                    send_sem=send_sems.at[4 * a + k], recv_sem=recv_sems.at[4 * a + k],
                    device_id=(x, y, 1 - c), device_id_type=MESH))
        for cp in cps:
            cp.start()
        for cp in cps:
            cp.wait()

    any_spec = pl.BlockSpec(memory_space=pl.ANY)
    return pl.pallas_call(
        body, name=name,
        out_shape=[jax.ShapeDtypeStruct((4,) + b.shape[1:], b.dtype) for b in bufs],
        in_specs=[any_spec] * n, out_specs=[any_spec] * n,
        scratch_shapes=[pltpu.SemaphoreType.DMA((4 * n,)), pltpu.SemaphoreType.DMA((4 * n,))],
    )(*bufs)


def _swap_with_chips(parts, name):
    ride = ("swap", parts)

    def body(*refs):
        start, _, finish = _ride_phases(ride, *_ride_split(ride, refs, 0, 0)[:3])
        start()
        finish()

    in_specs, out_shape, out_specs, scratch = _ride_specs(ride)
    return pl.pallas_call(
        body, name=name, out_shape=out_shape, in_specs=in_specs, out_specs=out_specs, scratch_shapes=scratch,
    )(*parts)


def _relative_blocks():
    x, y, c = _place()
    flips = ((0, 0), (1, 0), (0, 1), (1, 1))
    mine = [4 * (x ^ fx) + 2 * (y ^ fy) + c for fx, fy in flips]
    sib = [2 * (x ^ fx) + (y ^ fy) for fx, fy in flips]
    return jnp.stack(mine + sib).astype(jnp.int32)


def _chip_partial(buf, from_sibling, rel, name):
    _, r, cdim = buf.shape
    tr = _rows(r, 256)

    def body(rel_ref, m0, m1, m2, m3, s0, s1, s2, s3, own_ref, parts_ref):
        del rel_ref
        own_ref[...] = m0[...].astype(F32) + s0[...].astype(F32)
        for k, (m, s) in enumerate(((m1, s1), (m2, s2), (m3, s3))):
            parts_ref[k] = (m[...].astype(F32) + s[...].astype(F32)).astype(parts_ref.dtype)

    def pick(k):
        return pl.BlockSpec((None, tr, cdim), lambda i, rel_ref: (rel_ref[k], i, 0))

    return pl.pallas_call(
        body, name=name,
        grid_spec=pltpu.PrefetchScalarGridSpec(
            num_scalar_prefetch=1, grid=(r // tr,),
            in_specs=[pick(k) for k in range(8)],
            out_specs=(pl.BlockSpec((tr, cdim), lambda i, rel_ref: (i, 0)),
                       pl.BlockSpec((3, tr, cdim), lambda i, rel_ref: (0, i, 0)))),
        out_shape=(jax.ShapeDtypeStruct((r, cdim), F32), jax.ShapeDtypeStruct((3, r, cdim), buf.dtype)),
        compiler_params=_params(("parallel",)),
    )(rel, buf, buf, buf, buf, from_sibling, from_sibling, from_sibling, from_sibling)


def _total_adamw(own, parts, w, m, v, layer, carry, name):
    r, cdim = own.shape
    n_layers = w.shape[0]
    tr = _rows(r, 256)
    steps = r // tr

    def body(own_ref, parts_ref, w_ref, m_ref, v_ref, *rest):
        g_ref, d_ref, nm_ref, nv_ref = rest[-4:]
        g = own_ref[...]
        for k in range(3):
            g = g + parts_ref[k].astype(F32)
        g_ref[...] = g
        d_ref[...], nm_ref[...], nv_ref[...] = _adamw_math(w_ref[...], g, m_ref[...], v_ref[...])

    blk = pl.BlockSpec((tr, cdim), lambda i: (i, 0))
    lay = pl.BlockSpec((tr, cdim), lambda i: (layer * steps + i, 0))
    in_specs = [blk, pl.BlockSpec((3, tr, cdim), lambda i: (0, i, 0)), lay, lay, lay]
    operands = [own, parts, *[a.reshape(n_layers * r, cdim) for a in (w, m, v)]]
    aliases = {}
    if carry is not None:
        in_specs += [pl.BlockSpec(memory_space=pl.ANY)] * 4
        operands += list(carry)
        aliases = {5 + k: k for k in range(4)}
    sds = jax.ShapeDtypeStruct((n_layers * r, cdim), F32)
    return pl.pallas_call(
        body, name=name, grid=(steps,),
        out_shape=(sds,) * 4, in_specs=in_specs, out_specs=(lay,) * 4, input_output_aliases=aliases,
        compiler_params=_params(("parallel",)),
    )(*operands)


def _add_parts(own, parts, name):
    r, cdim = own.shape
    tr = _rows(r, 512)

    def body(own_ref, parts_ref, out_ref):
        acc = own_ref[...]
        for k in range(parts_ref.shape[0]):
            acc = acc + parts_ref[k].astype(F32)
        out_ref[...] = acc

    return pl.pallas_call(
        body, name=name, grid=(r // tr,),
        out_shape=jax.ShapeDtypeStruct((r, cdim), F32),
        in_specs=[pl.BlockSpec((tr, cdim), lambda i: (i, 0)),
                  pl.BlockSpec((parts.shape[0], tr, cdim), lambda i: (0, i, 0))],
        out_specs=pl.BlockSpec((tr, cdim), lambda i: (i, 0)),
        compiler_params=_params(("parallel",)),
    )(own, parts)


def _sum_leading(stack, name):
    n, r, cdim = stack.shape
    tr = _rows(r, 512)

    def body(in_ref, out_ref):
        acc = in_ref[0]
        for k in range(1, n):
            acc = acc + in_ref[k]
        out_ref[...] = acc

    return pl.pallas_call(
        body, name=name, grid=(r // tr,),
        out_shape=jax.ShapeDtypeStruct((r, cdim), F32),
        in_specs=[pl.BlockSpec((n, tr, cdim), lambda i: (0, i, 0))],
        out_specs=pl.BlockSpec((tr, cdim), lambda i: (i, 0)),
        compiler_params=_params(("parallel",)),
    )(stack)


_DIMS = {"nn": (((1,), (0,)), ((), ())), "nt": (((1,), (1,)), ((), ())), "tn": (((0,), (0,)), ((), ()))}


def _stack_spec(shape, mode, layer):
    cut, l, rows = layer
    cols = shape[2]
    by_n = pl.BlockSpec((None, rows, cols), lambda i, j, k: (j, l, 0))
    by_k = pl.BlockSpec((None, rows, cols), lambda i, j, k: (k, l, 0))
    if cut == "col":
        return (by_n, N_DEV * cols, cols, rows) if mode == "nn" else (by_k, rows, rows, cols)
    return (by_k, cols, cols, rows) if mode == "nn" else (by_n, N_DEV * rows, rows, cols)


def _matmul(a, b, *, mode, name, out_dtype=BF, a_act=None, epi=None, extras=(), seq=None, layer=None, tm=None,
            tn=None, into=None):
    if mode == "tn":
        kdim, m = a.shape
    else:
        m, kdim = a.shape
    if tm is None:
        tm = _pick(m, 1024 if epi != "resid_gate" else min(1024, seq))
    tk = _pick(kdim, 2048 if mode == "tn" else 1024)
    b_spec = None
    if layer is not None:
        b_spec, n, tn, tk = _stack_spec(b.shape, mode, layer)
    else:
        n = b.shape[0] if mode == "nt" else b.shape[1]
        tn = _pick(n, 1024) if tn is None else tn
    nk = kdim // tk
    a_spec = (pl.BlockSpec((tk, tm), lambda i, j, k: (k, i)) if mode == "tn"
              else pl.BlockSpec((tm, tk), lambda i, j, k: (i, k)))
    if b_spec is None:
        b_spec = (pl.BlockSpec((tn, tk), lambda i, j, k: (j, k)) if mode == "nt"
                  else pl.BlockSpec((tk, tn), lambda i, j, k: (k, j)))
    tile = pl.BlockSpec((tm, tn), lambda i, j, k: (i, j))
    in_specs, out_specs = [a_spec, b_spec], [tile]
    out_shape = [jax.ShapeDtypeStruct((m, n), out_dtype)]
    if epi == "resid_gate":
        in_specs += [tile, pl.BlockSpec((None, 1, tn), lambda i, j, k: ((i * tm) // seq, 0, j))]
        out_shape = [jax.ShapeDtypeStruct((m, n), F32), jax.ShapeDtypeStruct((m, n), BF)]
        out_specs = [tile, tile]
    elif epi in ("mul_drelu", "add"):
        in_specs += [tile]
    elif epi == "bias":
        in_specs += [pl.BlockSpec((1, tn), lambda i, j, k: (0, j))]
    n_extra, n_out = len(in_specs) - 2, len(out_specs)
    aliases, n_kept = {}, 0
    if into is not None:
        buffer, block, index_map = into
        out_dtype = buffer.dtype
        if not isinstance(buffer, jax.ShapeDtypeStruct):
            in_specs.append(pl.BlockSpec(memory_space=pl.ANY))
            extras = tuple(extras) + (buffer,)
            aliases, n_kept = {len(in_specs) - 1: 0}, 1
        out_shape = [jax.ShapeDtypeStruct(buffer.shape, buffer.dtype)]
        out_specs = [pl.BlockSpec(block, index_map)]
    dims = _DIMS[mode]

    def body(*refs):
        a_ref, b_ref = refs[:2]
        ex = refs[2:2 + n_extra]
        outs = refs[2 + n_extra + n_kept:2 + n_extra + n_kept + n_out]
        av = a_ref[...]
        if a_act == "relu2":
            t = jnp.maximum(av.astype(F32), 0.0)
            av = t * t
        elif a_act == "silu":
            t = av.astype(F32)
            av = t / (1.0 + jnp.exp(-t))
        part = lax.dot_general(av.astype(BF), b_ref[...].astype(BF), dims, preferred_element_type=F32)

        def finish(acc):
            if epi == "resid_gate":
                outs[0][...] = ex[0][...] + ex[1][...] * acc
                outs[1][...] = acc.astype(BF)
            elif epi == "mul_drelu":
                outs[0][...] = (acc * (2.0 * jnp.maximum(ex[0][...].astype(F32), 0.0))).astype(out_dtype)
            elif epi == "add":
                outs[0][...] = (acc + ex[0][...].astype(F32)).astype(out_dtype)
            elif epi == "bias":
                outs[0][...] = (acc + ex[0][...]).astype(out_dtype)
            else:
                outs[0][...] = acc.astype(out_dtype)

        if nk == 1:
            finish(part)
        else:
            acc_ref = refs[-1]
            k = pl.program_id(2)

            @pl.when(k == 0)
            def _():
                acc_ref[...] = part

            @pl.when(k > 0)
            def _():
                acc_ref[...] += part

            @pl.when(k == nk - 1)
            def _():
                finish(acc_ref[...])

    res = pl.pallas_call(
        body, name=name, grid=(m // tm, n // tn, nk),
        out_shape=out_shape, in_specs=in_specs, out_specs=out_specs,
        scratch_shapes=[pltpu.VMEM((tm, tn), F32)] if nk > 1 else [],
        input_output_aliases=aliases,
        compiler_params=_params(("parallel", "parallel", "arbitrary")),
    )(a, b, *extras)
    return res if n_out > 1 else res[0]


def _norm_mod(x, gain, scale, shift, seq, name):
    t, w = x.shape
    tr = ROW_BLOCK

    def body(x_ref, g_ref, sc_ref, sh_ref, out_ref):
        xv = x_ref[...]
        rstd = lax.rsqrt(jnp.mean(xv * xv, axis=-1, keepdims=True) + NORM_EPS)
        y = xv * rstd * g_ref[...]
        out_ref[...] = (y * (1.0 + sc_ref[...]) + sh_ref[...]).astype(BF)

    per_b = pl.BlockSpec((None, 1, w), lambda i: ((i * tr) // seq, 0, 0))
    return pl.pallas_call(
        body, name=name, grid=(t // tr,),
        out_shape=jax.ShapeDtypeStruct((t, w), BF),
        in_specs=[pl.BlockSpec((tr, w), lambda i: (i, 0)), pl.BlockSpec((1, w), lambda i: (0, 0)), per_b, per_b],
        out_specs=pl.BlockSpec((tr, w), lambda i: (i, 0)),
        compiler_params=_params(("parallel",)),
    )(x, gain, scale, shift)


def _norm_mod_bwd(dh, x, gain, scale, dres, seq, name):
    t, w = x.shape
    tr = ROW_BLOCK
    steps_per_seq = seq // tr
    nb = t // seq

    def body(dh_ref, x_ref, g_ref, sc_ref, dres_ref, dx_ref, dg_ref, dsc_ref, dsh_ref):
        i = pl.program_id(0)
        xv = x_ref[...]
        dhv = dh_ref[...].astype(F32)
        rstd = lax.rsqrt(jnp.mean(xv * xv, axis=-1, keepdims=True) + NORM_EPS)
        xhat = xv * rstd
        one_sc = 1.0 + sc_ref[...]
        g = g_ref[...]
        dxhat = dhv * (g * one_sc)
        proj = jnp.mean(dxhat * xhat, axis=-1, keepdims=True)
        dx_ref[...] = dres_ref[...] + rstd * (dxhat - xhat * proj)
        dhx = dhv * xhat
        dg_part = jnp.sum(dhx * one_sc, axis=0, keepdims=True)
        dsc_part = jnp.sum(dhx * g, axis=0, keepdims=True)
        dsh_part = jnp.sum(dhv, axis=0, keepdims=True)

        @pl.when(i == 0)
        def _():
            dg_ref[...] = dg_part

        @pl.when(i > 0)
        def _():
            dg_ref[...] += dg_part

        @pl.when(i % steps_per_seq == 0)
        def _():
            dsc_ref[...] = dsc_part
            dsh_ref[...] = dsh_part

        @pl.when(i % steps_per_seq != 0)
        def _():
            dsc_ref[...] += dsc_part
            dsh_ref[...] += dsh_part

    row = pl.BlockSpec((tr, w), lambda i: (i, 0))
    per_b = pl.BlockSpec((None, 1, w), lambda i: ((i * tr) // seq, 0, 0))
    vec = pl.BlockSpec((1, w), lambda i: (0, 0))
    return pl.pallas_call(
        body, name=name, grid=(t // tr,),
        out_shape=(jax.ShapeDtypeStruct((t, w), F32), jax.ShapeDtypeStruct((1, w), F32),
                   jax.ShapeDtypeStruct((nb, 1, w), F32), jax.ShapeDtypeStruct((nb, 1, w), F32)),
        in_specs=[row, row, vec, per_b, row],
        out_specs=(row, vec, per_b, per_b),
        compiler_params=_params(("arbitrary",)),
    )(dh, x, gain, scale, dres)


def _gate_bwd(dx, y, gate, seq, name):
    t, w = dx.shape
    tr = ROW_BLOCK
    steps_per_seq = seq // tr
    nb = t // seq

    def body(dx_ref, y_ref, g_ref, dy_ref, dg_ref):
        i = pl.program_id(0)
        dxv = dx_ref[...]
        dy_ref[...] = (dxv * g_ref[...]).astype(BF)
        part = jnp.sum(dxv * y_ref[...].astype(F32), axis=0, keepdims=True)

        @pl.when(i % steps_per_seq == 0)
        def _():
            dg_ref[...] = part

        @pl.when(i % steps_per_seq != 0)
        def _():
            dg_ref[...] += part

    row = pl.BlockSpec((tr, w), lambda i: (i, 0))
    per_b = pl.BlockSpec((None, 1, w), lambda i: ((i * tr) // seq, 0, 0))
    return pl.pallas_call(
        body, name=name, grid=(t // tr,),
        out_shape=(jax.ShapeDtypeStruct((t, w), BF), jax.ShapeDtypeStruct((nb, 1, w), F32)),
        in_specs=[row, row, per_b], out_specs=(row, per_b),
        compiler_params=_params(("arbitrary",)),
    )(dx, y, gate)


def _loss_head(x, gain, target, name):
    t, w = x.shape
    tr = ROW_BLOCK

    def body(x_ref, g_ref, t_ref, loss_ref, dx_ref, dg_ref):
        i = pl.program_id(0)
        xv = x_ref[...]
        g = g_ref[...]
        rstd = lax.rsqrt(jnp.mean(xv * xv, axis=-1, keepdims=True) + NORM_EPS)
        xhat = xv * rstd
        err = xhat * g - t_ref[...]
        row_loss = jnp.sum(err * err, axis=-1, keepdims=True) * (0.5 / w)
        loss_part = jnp.broadcast_to(jnp.sum(row_loss, axis=0, keepdims=True), (1, LANES))
        dy = err * (1.0 / w)
        dg_part = jnp.sum(dy * xhat, axis=0, keepdims=True)
        dxhat = dy * g
        proj = jnp.mean(dxhat * xhat, axis=-1, keepdims=True)
        dx_ref[...] = rstd * (dxhat - xhat * proj)

        @pl.when(i == 0)
        def _():
            loss_ref[...] = loss_part
            dg_ref[...] = dg_part

        @pl.when(i > 0)
        def _():
            loss_ref[...] += loss_part
            dg_ref[...] += dg_part

    row = pl.BlockSpec((tr, w), lambda i: (i, 0))
    vec = pl.BlockSpec((1, w), lambda i: (0, 0))
    return pl.pallas_call(
        body, name=name, grid=(t // tr,),
        out_shape=(jax.ShapeDtypeStruct((1, LANES), F32), jax.ShapeDtypeStruct((t, w), F32),
                   jax.ShapeDtypeStruct((1, w), F32)),
        in_specs=[row, vec, row],
        out_specs=(pl.BlockSpec((1, LANES), lambda i: (0, 0)), row, vec),
        compiler_params=_params(("arbitrary",)),
    )(x, gain, target)


def _rope_group(xg, cos_p, sin_a, sin_b):
    return (xg * cos_p + pltpu.roll(xg, LANES - ROPE_HALF, axis=1) * sin_a
            + pltpu.roll(xg, ROPE_HALF, axis=1) * sin_b)


def _rope(x, tables, name, out_dtype=BF):
    t, w = x.shape
    tr = ROW_BLOCK
    groups = w // LANES

    def body(x_ref, c_ref, a_ref, b_ref, out_ref):
        cos_p, sin_a, sin_b = c_ref[...], a_ref[...], b_ref[...]
        for g in range(groups):
            sl = slice(g * LANES, (g + 1) * LANES)
            out_ref[:, sl] = _rope_group(x_ref[:, sl].astype(F32), cos_p, sin_a, sin_b).astype(out_dtype)

    row = pl.BlockSpec((tr, w), lambda i: (i, 0))
    tab = pl.BlockSpec((tr, LANES), lambda i: (i, 0))
    return pl.pallas_call(
        body, name=name, grid=(t // tr,),
        out_shape=jax.ShapeDtypeStruct((t, w), out_dtype),
        in_specs=[row, tab, tab, tab], out_specs=row,
        compiler_params=_params(("parallel",)),
    )(x, *tables)


def _mla_mid(down, gq, gkv, tables, name):
    t = down.shape[0]
    tr = ROW_BLOCK

    def body(d_ref, gq_ref, gkv_ref, c_ref, a_ref, b_ref, cq_ref, ckr_ref):
        q = d_ref[:, 0:256]
        cq_ref[...] = (q * lax.rsqrt(jnp.mean(q * q, axis=-1, keepdims=True) + NORM_EPS) * gq_ref[...]).astype(BF)
        kv = d_ref[:, 256:384]
        ckr_ref[:, 0:128] = (kv * lax.rsqrt(jnp.mean(kv * kv, axis=-1, keepdims=True) + NORM_EPS)
                             * gkv_ref[...]).astype(BF)
        ckr_ref[:, 128:256] = _rope_group(d_ref[:, 384:512], c_ref[...], a_ref[...], b_ref[...]).astype(BF)

    tab = pl.BlockSpec((tr, LANES), lambda i: (i, 0))
    return pl.pallas_call(
        body, name=name, grid=(t // tr,),
        out_shape=(jax.ShapeDtypeStruct((t, 256), BF), jax.ShapeDtypeStruct((t, 256), BF)),
        in_specs=[pl.BlockSpec((tr, 512), lambda i: (i, 0)), pl.BlockSpec((1, 256), lambda i: (0, 0)),
                  pl.BlockSpec((1, 128), lambda i: (0, 0)), tab, tab, tab],
        out_specs=(pl.BlockSpec((tr, 256), lambda i: (i, 0)), pl.BlockSpec((tr, 256), lambda i: (i, 0))),
        compiler_params=_params(("parallel",)),
    )(down, gq, gkv, *tables)


def _mla_mid_bwd(down, dcq, dckr, gq, gkv, tables_t, name):
    t = down.shape[0]
    tr = ROW_BLOCK

    def norm_bwd(xv, g, dy):
        rstd = lax.rsqrt(jnp.mean(xv * xv, axis=-1, keepdims=True) + NORM_EPS)
        xhat = xv * rstd
        dxhat = dy * g
        proj = jnp.mean(dxhat * xhat, axis=-1, keepdims=True)
        return rstd * (dxhat - xhat * proj), jnp.sum(dy * xhat, axis=0, keepdims=True)

    def body(d_ref, dcq_ref, dckr_ref, gq_ref, gkv_ref, c_ref, a_ref, b_ref, dd_ref, dgq_ref, dgkv_ref):
        i = pl.program_id(0)
        dq, dgq_part = norm_bwd(d_ref[:, 0:256], gq_ref[...], dcq_ref[...].astype(F32))
        dd_ref[:, 0:256] = dq.astype(BF)
        dkv, dgkv_part = norm_bwd(d_ref[:, 256:384], gkv_ref[...], dckr_ref[:, 0:128].astype(F32))
        dd_ref[:, 256:384] = dkv.astype(BF)
        dd_ref[:, 384:512] = _rope_group(dckr_ref[:, 128:256].astype(F32), c_ref[...], a_ref[...],
                                         b_ref[...]).astype(BF)

        @pl.when(i == 0)
        def _():
            dgq_ref[...] = dgq_part
            dgkv_ref[...] = dgkv_part

        @pl.when(i > 0)
        def _():
            dgq_ref[...] += dgq_part
            dgkv_ref[...] += dgkv_part

    tab = pl.BlockSpec((tr, LANES), lambda i: (i, 0))
    r256 = pl.BlockSpec((tr, 256), lambda i: (i, 0))
    return pl.pallas_call(
        body, name=name, grid=(t // tr,),
        out_shape=(jax.ShapeDtypeStruct((t, 512), BF), jax.ShapeDtypeStruct((1, 256), F32),
                   jax.ShapeDtypeStruct((1, 128), F32)),
        in_specs=[pl.BlockSpec((tr, 512), lambda i: (i, 0)), r256, r256, pl.BlockSpec((1, 256), lambda i: (0, 0)),
                  pl.BlockSpec((1, 128), lambda i: (0, 0)), tab, tab, tab],
        out_specs=(pl.BlockSpec((tr, 512), lambda i: (i, 0)), pl.BlockSpec((1, 256), lambda i: (0, 0)),
                   pl.BlockSpec((1, 128), lambda i: (0, 0))),
        compiler_params=_params(("arbitrary",)),
    )(down, dcq, dckr, gq, gkv, *tables_t)


def _scan_rows(x, reverse):
    s = x.shape[0]
    row = lax.broadcasted_iota(jnp.int32, x.shape, 0)
    step = 1
    while step < s:
        if reverse:
            x = x + jnp.where(row < s - step, pltpu.roll(x, s - step, axis=0), 0.0)
        else:
            x = x + jnp.where(row >= step, pltpu.roll(x, step, axis=0), 0.0)
        step *= 2
    return x


def _fox_gate(fg, b_f, seq, name):
    t = fg.shape[0]

    def body(fg_ref, b_ref, out_ref):
        z = fg_ref[...] + b_ref[...]
        log_f = jnp.minimum(z, 0.0) - jnp.log(1.0 + jnp.exp(-jnp.abs(z)))
        out_ref[...] = _scan_rows(log_f, reverse=False)

    blk = pl.BlockSpec((seq, LANES), lambda b: (b, 0))
    return pl.pallas_call(
        body, name=name, grid=(t // seq,),
        out_shape=jax.ShapeDtypeStruct((t, LANES), F32),
        in_specs=[blk, pl.BlockSpec((1, LANES), lambda b: (0, 0))], out_specs=blk,
        compiler_params=_params(("parallel",)),
    )(fg, b_f)


def _fox_gate_bwd(d_cum, fg, b_f, seq, name):
    t = fg.shape[0]

    def body(dc_ref, fg_ref, b_ref, dfg_ref, db_ref):
        b = pl.program_id(0)
        z = fg_ref[...] + b_ref[...]
        d_log_f = _scan_rows(dc_ref[...], reverse=True)
        dz = d_log_f / (1.0 + jnp.exp(z))
        dfg_ref[...] = dz
        part = jnp.sum(dz, axis=0, keepdims=True)

        @pl.when(b == 0)
        def _():
            db_ref[...] = part

        @pl.when(b > 0)
        def _():
            db_ref[...] += part

    blk = pl.BlockSpec((seq, LANES), lambda b: (b, 0))
    vec = pl.BlockSpec((1, LANES), lambda b: (0, 0))
    return pl.pallas_call(
        body, name=name, grid=(t // seq,),
        out_shape=(jax.ShapeDtypeStruct((t, LANES), F32), jax.ShapeDtypeStruct((1, LANES), F32)),
        in_specs=[blk, blk, vec], out_specs=(blk, vec),
        compiler_params=_params(("arbitrary",)),
    )(d_cum, fg, b_f)


def _head_masks():
    lane = lax.broadcasted_iota(jnp.int32, (1, LANES), 1)
    return lane < HEAD_DIM, lane >= HEAD_DIM


def _attn_fwd(q_arr, q_off, k_arr, k_off, v_arr, v_off, bias, seq, name, ride=None):
    t = q_arr.shape[0]
    nb = t // seq
    blk = min(ATTN_BLOCK, seq)
    nq = seq // blk
    has_bias = bias is not None
    n_in, n_out = (4, 3) if has_bias else (3, 2)

    def body(*refs):
        step = pl.program_id(0) * HEAD_PAIRS + pl.program_id(1)
        refs, ride_end = _ride_steps(ride, refs, n_in, n_out, step, nb * HEAD_PAIRS)
        if has_bias:
            q_ref, k_ref, v_ref, bias_ref, o_ref, lse_ref, o32_ref = refs
        else:
            q_ref, k_ref, v_ref, o_ref, lse_ref = refs
        lo, hi = _head_masks()
        rows = lax.broadcasted_iota(jnp.int32, (blk, blk), 0)
        cols = lax.broadcasted_iota(jnp.int32, (blk, blk), 1)
        causal = cols <= rows

        def q_block(iq, _):
            q0 = pl.multiple_of(iq * blk, blk)
            qs = [q_ref[pl.ds(q0, blk), h * LANES:(h + 1) * LANES] for h in range(2)]

            def kv_block(j, carry, diag):
                k0 = pl.multiple_of(j * blk, blk)
                vv = v_ref[pl.ds(k0, blk), :]
                vs = [jnp.where(lo, vv, jnp.zeros_like(vv)), jnp.where(hi, vv, jnp.zeros_like(vv))]
                acc = carry[0]
                new = []
                pv = None
                alphas = []
                for h in range(2):
                    m, l = carry[1 + 2 * h], carry[2 + 2 * h]
                    kk = k_ref[pl.ds(k0, blk), h * LANES:(h + 1) * LANES]
                    s = lax.dot_general(qs[h], kk, _DIMS["nt"], preferred_element_type=F32)
                    if has_bias:
                        s = s + bias_ref[h, 0:1, pl.ds(k0, blk)]
                    if diag:
                        s = jnp.where(causal, s, -jnp.inf)
                    m_new = jnp.maximum(m, jnp.max(s, axis=-1, keepdims=True))
                    p = jnp.exp(s - m_new)
                    alpha = jnp.exp(m - m_new)
                    l_new = alpha * l + jnp.sum(p, axis=-1, keepdims=True)
                    p_hi = p.astype(BF)
                    d = jnp.dot(p_hi, vs[h], preferred_element_type=F32)
                    if has_bias:
                        p_lo = (p - p_hi.astype(F32)).astype(BF)
                        d = d + jnp.dot(p_lo, vs[h], preferred_element_type=F32)
                    pv = d if pv is None else pv + d
                    alphas.append(alpha)
                    new += [m_new, l_new]
                acc = acc * jnp.where(lo, alphas[0], alphas[1]) + pv
                return (acc, *new)

            init = (jnp.zeros((blk, LANES), F32),
                    jnp.full((blk, 1), -jnp.inf, F32), jnp.zeros((blk, 1), F32),
                    jnp.full((blk, 1), -jnp.inf, F32), jnp.zeros((blk, 1), F32))
            carry = lax.fori_loop(0, iq, functools.partial(kv_block, diag=False), init)
            acc, m0, l0, m1, l1 = kv_block(iq, carry, diag=True)
            o_val = acc / jnp.where(lo, l0, l1)
            o_ref[pl.ds(q0, blk), :] = o_val.astype(BF)
            if has_bias:
                o32_ref[pl.ds(q0, blk), :] = o_val
            lse_ref[pl.ds(q0, blk), :] = jnp.where(lo, m0 + jnp.log(l0), m1 + jnp.log(l1))
            return 0

        lax.fori_loop(0, nq, q_block, 0)
        ride_end()

    in_specs = [pl.BlockSpec((seq, 2 * LANES), lambda b, p: (b, q_off + p)),
                pl.BlockSpec((seq, 2 * LANES), lambda b, p: (b, k_off + p)),
                pl.BlockSpec((seq, LANES), lambda b, p: (b, v_off + p))]
    args = [q_arr, k_arr, v_arr]
    if has_bias:
        in_specs.append(pl.BlockSpec((None, 2, 8, seq), lambda b, p: (b, p, 0, 0)))
        args.append(bias)
    out_blk = pl.BlockSpec((seq, LANES), lambda b, p: (b, p))
    out_shape = [jax.ShapeDtypeStruct((t, HEAD_PAIRS * LANES), BF), jax.ShapeDtypeStruct((t, HEAD_PAIRS * LANES), F32)]
    if has_bias:
        out_shape.append(jax.ShapeDtypeStruct((t, HEAD_PAIRS * LANES), F32))
    out_specs, scratch = [out_blk] * len(out_shape), []
    if ride is not None:
        r_in, r_shape, r_out, scratch = _ride_specs(ride)
        in_specs, out_shape, out_specs = in_specs + r_in, out_shape + r_shape, out_specs + r_out
        args += list(ride[1])
    return pl.pallas_call(
        body, name=name, grid=(nb, HEAD_PAIRS),
        out_shape=out_shape, in_specs=in_specs, out_specs=out_specs, scratch_shapes=scratch,
        compiler_params=_params(("arbitrary", "arbitrary")),
    )(*args)


def _attn_bwd(q_arr, q_off, k_arr, k_off, v_arr, v_off, bias, o, do, lse, seq, name, ride=None):
    t = q_arr.shape[0]
    nb = t // seq
    blk = min(ATTN_BLOCK, seq)
    nq = seq // blk
    has_bias = bias is not None
    n_in, n_out = (7, 4) if has_bias else (6, 3)

    def body(*refs):
        step = pl.program_id(0) * HEAD_PAIRS + pl.program_id(1)
        refs, ride_end = _ride_steps(ride, refs, n_in, n_out, step, nb * HEAD_PAIRS)
        if has_bias:
            (q_ref, k_ref, v_ref, bias_ref, o_ref, do_ref, lse_ref,
             dq_ref, dk_ref, dv_ref, dbias_ref, dq_acc, dsum) = refs
        else:
            (q_ref, k_ref, v_ref, o_ref, do_ref, lse_ref, dq_ref, dk_ref, dv_ref, dq_acc, dsum) = refs
        lo, hi = _head_masks()
        rows = lax.broadcasted_iota(jnp.int32, (blk, blk), 0)
        cols = lax.broadcasted_iota(jnp.int32, (blk, blk), 1)
        causal = cols <= rows
        dq_acc[...] = jnp.zeros_like(dq_acc)

        def prep(iq, _):
            q0 = pl.multiple_of(iq * blk, blk)
            prod = do_ref[pl.ds(q0, blk), :].astype(F32) * o_ref[pl.ds(q0, blk), :].astype(F32)
            d0 = jnp.sum(jnp.where(lo, prod, 0.0), axis=-1, keepdims=True)
            d1 = jnp.sum(jnp.where(hi, prod, 0.0), axis=-1, keepdims=True)
            dsum[pl.ds(q0, blk), :] = jnp.where(lo, d0, d1)
            return 0

        lax.fori_loop(0, nq, prep, 0)

        def kv_block(j, _):
            k0 = pl.multiple_of(j * blk, blk)
            vv = v_ref[pl.ds(k0, blk), :]
            vs = [jnp.where(lo, vv, jnp.zeros_like(vv)), jnp.where(hi, vv, jnp.zeros_like(vv))]
            ks = [k_ref[pl.ds(k0, blk), h * LANES:(h + 1) * LANES] for h in range(2)]

            def q_block(iq, carry, diag):
                q0 = pl.multiple_of(iq * blk, blk)
                dov = do_ref[pl.ds(q0, blk), :]
                lse_v = lse_ref[pl.ds(q0, blk), :]
                dsum_v = dsum[pl.ds(q0, blk), :]
                dv_acc = carry[0]
                out = []
                for h in range(2):
                    dk_acc, db_acc = carry[1 + 2 * h], carry[2 + 2 * h]
                    qq = q_ref[pl.ds(q0, blk), h * LANES:(h + 1) * LANES]
                    s = lax.dot_general(qq, ks[h], _DIMS["nt"], preferred_element_type=F32)
                    if has_bias:
                        s = s + bias_ref[h, 0:1, pl.ds(k0, blk)]
                    p = jnp.exp(s - lse_v[:, h * HEAD_DIM:h * HEAD_DIM + 1])
                    if diag:
                        p = jnp.where(causal, p, 0.0)
                    dp = lax.dot_general(dov, vs[h], _DIMS["nt"], preferred_element_type=F32)
                    ds = p * (dp - dsum_v[:, h * HEAD_DIM:h * HEAD_DIM + 1])
                    ds_bf = ds.astype(BF)
                    pt_do = lax.dot_general(p.astype(BF), dov, _DIMS["tn"], preferred_element_type=F32)
                    dv_acc = dv_acc + jnp.where(hi if h else lo, pt_do, 0.0)
                    dk_acc = dk_acc + lax.dot_general(ds_bf, qq, _DIMS["tn"], preferred_element_type=F32)
                    dq_acc[pl.ds(q0, blk), h * LANES:(h + 1) * LANES] += jnp.dot(
                        ds_bf, ks[h], preferred_element_type=F32)
                    if has_bias:
                        db_acc = db_acc + jnp.sum(ds, axis=0, keepdims=True)
                    out += [dk_acc, db_acc]
                return (dv_acc, *out)

            init = (jnp.zeros((blk, LANES), F32),
                    jnp.zeros((blk, LANES), F32), jnp.zeros((1, blk), F32),
                    jnp.zeros((blk, LANES), F32), jnp.zeros((1, blk), F32))
            carry = q_block(j, init, diag=True)
            carry = lax.fori_loop(j + 1, nq, functools.partial(q_block, diag=False), carry)
            dv_ref[pl.ds(k0, blk), :] = carry[0].astype(BF)
            for h in range(2):
                dk_ref[pl.ds(k0, blk), h * LANES:(h + 1) * LANES] = carry[1 + 2 * h].astype(BF)
                if has_bias:
                    dbias_ref[h, :, pl.ds(k0, blk)] = jnp.broadcast_to(carry[2 + 2 * h], (8, blk))
            return 0

        lax.fori_loop(0, nq, kv_block, 0)
        dq_ref[...] = dq_acc[...].astype(BF)
        ride_end()

    pair256 = lambda off: pl.BlockSpec((seq, 2 * LANES), lambda b, p: (b, off + p))
    pair128 = lambda off: pl.BlockSpec((seq, LANES), lambda b, p: (b, off + p))
    bias_spec = pl.BlockSpec((None, 2, 8, seq), lambda b, p: (b, p, 0, 0))
    in_specs = [pair256(q_off), pair256(k_off), pair128(v_off)]
    args = [q_arr, k_arr, v_arr]
    if has_bias:
        in_specs.append(bias_spec)
        args.append(bias)
    in_specs += [pair128(0), pair128(0), pair128(0)]
    args += [o, do, lse]
    out_shape = [jax.ShapeDtypeStruct((t, HEAD_PAIRS * 2 * LANES), BF),
                 jax.ShapeDtypeStruct((t, HEAD_PAIRS * 2 * LANES), BF),
                 jax.ShapeDtypeStruct((t, HEAD_PAIRS * LANES), BF)]
    out_specs = [pair256(0), pair256(0), pair128(0)]
    if has_bias:
        out_shape.append(jax.ShapeDtypeStruct((nb, HEADS, 8, seq), F32))
        out_specs.append(bias_spec)
    scratch = [pltpu.VMEM((seq, 2 * LANES), F32), pltpu.VMEM((seq, LANES), F32)]
    if ride is not None:
        r_in, r_shape, r_out, r_scratch = _ride_specs(ride)
        in_specs, out_shape, out_specs = in_specs + r_in, out_shape + r_shape, out_specs + r_out
        args += list(ride[1])
        scratch += r_scratch
    return pl.pallas_call(
        body, name=name, grid=(nb, HEAD_PAIRS),
        out_shape=out_shape, in_specs=in_specs, out_specs=out_specs, scratch_shapes=scratch,
        compiler_params=_params(("arbitrary", "arbitrary")),
    )(*args)


def _adamw(w, g, m, v, name):
    shape = w.shape
    last = shape[-1]
    rows = int(np.prod(shape[:-1])) if len(shape) > 1 else 1
    tr = _rows(rows, 512)

    def body(w_ref, g_ref, m_ref, v_ref, d_ref, nm_ref, nv_ref):
        d_ref[...], nm_ref[...], nv_ref[...] = _adamw_math(w_ref[...], g_ref[...], m_ref[...], v_ref[...])

    blk = pl.BlockSpec((tr, last), lambda i: (i, 0))
    sds = jax.ShapeDtypeStruct((rows, last), F32)
    outs = pl.pallas_call(
        body, name=name, grid=(rows // tr,),
        out_shape=(sds, sds, sds), in_specs=[blk] * 4, out_specs=(blk,) * 3,
        compiler_params=_params(("parallel",)),
    )(*[a.reshape(rows, last) for a in (w, g, m, v)])
    return tuple(a.reshape(shape) for a in outs)


LOW_COLS = 256


def _low_pad(a):
    return jnp.pad(a, ((0, 0),) * (a.ndim - 1) + ((0, LOW_COLS - a.shape[-1]),))


def _layer_shards(w, i):
    j = i // 2
    bf = lambda a: a.astype(BF)
    if i % 2 == 0:
        mixer = [bf(w["fox_w_in"][j]), bf(w["fox_w_out"][j])]
    else:
        mixer = [jnp.concatenate([bf(w["mla_w_dq"][j]), bf(w["mla_w_ukv"][j]), _low_pad(bf(w["mla_w_uq"][j])),
                                  _low_pad(bf(w["mla_w_dkv"][j]))], axis=0), bf(w["mla_w_out"][j])]
    return mixer + [bf(w["mlp_w1"][i]), bf(w["mlp_w2"][i])]


def _side_by_side(stack, r0, rows, cols=None):
    return jnp.concatenate([stack[dd, r0:r0 + rows, :cols] for dd in range(N_DEV)], axis=1)


def _stacked(stack, r0, rows, cols=None):
    part = stack[:, r0:r0 + rows, :cols]
    return part.reshape(N_DEV * rows, part.shape[2])


def _layer_mixer_weights(i, first, out_all):
    full = dict(w_out=_stacked(out_all, 0, 128))
    if i % 2 == 0:
        full["fox_w_in"] = _side_by_side(first, 0, 1024)
    else:
        full.update(mla_w_dq=_stacked(first, 0, 128), mla_w_ukv=_side_by_side(first, 128, 128),
                    mla_w_uq=_side_by_side(first, 256, 256, 192), mla_w_dkv=_stacked(first, 512, 128, 160))
    return full


def _by_dest_rows(g):
    return g.reshape(N_DEV, g.shape[0] // N_DEV, g.shape[1]).astype(BF)


def _by_dest_cols(g):
    n = g.shape[1] // N_DEV
    return jnp.stack([g[:, dd * n:(dd + 1) * n] for dd in range(N_DEV)]).astype(BF)


def _layer_mixer_grad_bufs(i, g):
    if i % 2 == 0:
        return [_by_dest_cols(g["fox_w_in"]), _by_dest_rows(g["w_out"])]
    low = jnp.concatenate([_by_dest_rows(g["mla_w_dq"]), _by_dest_cols(g["mla_w_ukv"]),
                           _low_pad(_by_dest_cols(g["mla_w_uq"])), _low_pad(_by_dest_rows(g["mla_w_dkv"]))], axis=1)
    return [low, _by_dest_rows(g["w_out"])]


def _low_shard_grads(low):
    return dict(mla_w_dq=low[:128], mla_w_ukv=low[128:256], mla_w_uq=low[256:512, :192], mla_w_dkv=low[512:, :160])


def _pad_heads(w, width):
    k = w.shape[0]
    return jnp.pad(w.reshape(k, HEADS, width), ((0, 0), (0, 0), (0, LANES - width))).reshape(k, HEADS * LANES)


def _unpad_heads(w, width):
    k = w.shape[0]
    return w.reshape(k, HEADS, LANES)[:, :, :width].reshape(k, HEADS * width)


def _rope_tables(positions, scale):
    inv_freq = 10000.0 ** (-jnp.arange(0, 2 * ROPE_HALF, 2, dtype=F32) / (2 * ROPE_HALF))
    ang = positions.astype(F32)[:, None] * inv_freq
    cos, sin = jnp.cos(ang) * scale, jnp.sin(ang) * scale
    t = positions.shape[0]
    z = lambda n: jnp.zeros((t, n), F32)
    cos_p = jnp.concatenate([jnp.full((t, HEAD_DIM), scale, F32), cos, cos, z(32)], axis=1)
    sin_a = jnp.concatenate([z(64), -sin, z(48)], axis=1)
    sin_b = jnp.concatenate([z(80), sin, z(32)], axis=1)
    fwd = (cos_p, sin_a, sin_b)
    bwd = (cos_p, jnp.roll(sin_b, -ROPE_HALF, axis=1), jnp.roll(sin_a, ROPE_HALF, axis=1))
    return fwd, bwd


def _key_rows(cum, nb, seq):
    v = -cum.reshape(nb, seq, LANES)[:, :, :HEADS]
    return jnp.broadcast_to(jnp.transpose(v, (0, 2, 1))[:, :, None, :], (nb, HEADS, 8, seq))


def kernel(x, c, positions, ada_w, ada_b, norm_mix_g, norm_mlp_g, fox_w_in, fox_b_f, fox_w_out, mla_w_dq, mla_q_norm_g, mla_w_uq, mla_w_dkv, mla_kv_norm_g, mla_w_ukv, mla_w_out, mlp_w1, mlp_w2, final_norm_g, loss_target, m_ada_w, m_ada_b, m_norm_mix_g, m_norm_mlp_g, m_fox_w_in, m_fox_b_f, m_fox_w_out, m_mla_w_dq, m_mla_q_norm_g, m_mla_w_uq, m_mla_w_dkv, m_mla_kv_norm_g, m_mla_w_ukv, m_mla_w_out, m_mlp_w1, m_mlp_w2, m_final_norm_g, v_ada_w, v_ada_b, v_norm_mix_g, v_norm_mlp_g, v_fox_w_in, v_fox_b_f, v_fox_w_out, v_mla_w_dq, v_mla_q_norm_g, v_mla_w_uq, v_mla_w_dkv, v_mla_kv_norm_g, v_mla_w_ukv, v_mla_w_out, v_mlp_w1, v_mlp_w2, v_final_norm_g):
    args = dict(locals())
    weights = {n: args[n] for n in WEIGHTS}
    nb, seq, d = x.shape
    t = nb * seq
    depth = ada_w.shape[0]
    dev = 4 * lax.axis_index("x") + 2 * lax.axis_index("y") + lax.axis_index("c")
    n_mod_local = ada_w.shape[2]

    n_qg = mla_q_norm_g.shape[1]
    cond = jnp.concatenate([c, jnp.pad(mla_q_norm_g.reshape(1, -1), ((0, 7), (0, d - 2 * n_qg)))], axis=0)
    w1_rows, w2_rows = mlp_w1.shape[1], mlp_w2.shape[1]
    shards = [_layer_shards(weights, i) for i in range(depth)]
    stacks = [None] * depth
    *stacks[0], cond_all = _all_gather(shards[0] + [cond], "gather_first")
    c_all = cond_all[:, :nb].reshape(N_DEV * nb, d)
    q_gain = jnp.transpose(cond_all[:, nb, :2 * n_qg].reshape(N_DEV, 2, n_qg), (1, 0, 2)).reshape(2, N_DEV * n_qg)
    mod_local = jnp.stack([
        _matmul(c_all, ada_w[i], mode="nn", name="ada_mod", out_dtype=F32, a_act="silu", epi="bias",
                extras=(lax.dynamic_slice_in_dim(ada_b[i], dev * n_mod_local, n_mod_local)[None, :],))
        for i in range(depth)])
    mod_all, = _all_gather([mod_local.reshape(depth * N_DEV * nb, n_mod_local)], "gather_mod")
    mod_all = jnp.transpose(mod_all.reshape(N_DEV, depth, N_DEV * nb, n_mod_local), (1, 2, 0, 3))
    mod_all = mod_all.reshape(depth, N_DEV * nb, N_DEV * n_mod_local)
    mod = lax.dynamic_slice_in_dim(mod_all, dev * nb, nb, axis=1)
    mod = mod.reshape(depth, nb, 6, 1, d)

    pos = positions.reshape(t)
    rope_q, rope_q_t = _rope_tables(pos, MLA_SCALE)
    rope_k, rope_k_t = _rope_tables(pos, 1.0)

    def fox_weights(full):
        w_in = full["fox_w_in"]
        wq = _pad_heads(w_in[:, :d] * FOX_SCALE, HEAD_DIM)
        wk = _pad_heads(w_in[:, d:2 * d], HEAD_DIM)
        w_qkv = jnp.concatenate([wq, wk, w_in[:, 2 * d:3 * d]], axis=1)
        w_f = jnp.pad(w_in[:, 3 * d:], ((0, 0), (0, LANES - HEADS)))
        return w_qkv, w_f

    def mla_weights(full):
        w_dkv = full["mla_w_dkv"]
        w_down = jnp.concatenate([full["mla_w_dq"], w_dkv[:, :128], jnp.zeros((d, 64), BF),
                                  w_dkv[:, 128:160], jnp.zeros((d, 32), BF)], axis=1)
        w_uq = _pad_heads(full["mla_w_uq"], 96)
        w_ukv = full["mla_w_ukv"].reshape(128, HEADS, 2, HEAD_DIM)
        w_uk = jnp.pad(w_ukv[:, :, 0, :], ((0, 0), (0, 0), (0, 64))).reshape(128, HEADS * LANES)
        w_uv = w_ukv[:, :, 1, :].reshape(128, HEADS * HEAD_DIM)
        place = np.zeros((128, HEADS, LANES), np.float32)
        for i in range(2 * ROPE_HALF):
            place[64 + i, :, 64 + i] = 1.0
        bottom = jnp.concatenate([jnp.asarray(place.reshape(128, HEADS * LANES), BF),
                                  jnp.zeros((128, HEADS * HEAD_DIM), BF)], axis=1)
        w_kv = jnp.concatenate([jnp.concatenate([w_uk, w_uv], axis=1), bottom], axis=0)
        return w_down, w_uq, w_kv

    xs = x.reshape(t, d)
    saved = []
    for i in range(depth):
        j = i // 2
        sh_m, sc_m, g_m, sh_f, sc_f, g_f = (mod[i, :, q] for q in range(6))
        gain_mix = norm_mix_g[i][None, :]
        gain_mlp = norm_mlp_g[i][None, :]
        s = dict(x_in=xs)
        h = _norm_mod(xs, gain_mix, sc_m, sh_m, seq, "norm_mix")
        s["h"] = h
        full = _layer_mixer_weights(i, stacks[i][0], stacks[i][1])
        ride = ("gather", shards[i + 1]) if i + 1 < depth else None
        if i % 2 == 0:
            w_qkv, w_f = fox_weights(full)
            qkv = _matmul(h, w_qkv, mode="nn", name="fox_qkv")
            fg = _matmul(h, w_f, mode="nn", name="fox_gate_logits", out_dtype=F32)
            b_f = jnp.pad(fox_b_f[j], (0, LANES - HEADS))[None, :]
            cum = _fox_gate(fg, b_f, seq, "fox_gate")
            bias = _key_rows(cum, nb, seq)
            o, lse, o32, *rode = _attn_fwd(qkv, 0, qkv, 8, qkv, 32, bias, seq, "fox_attn", ride)
            s.update(qkv=qkv, fg=fg, b_f=b_f, bias=bias, w_qkv=w_qkv, w_f=w_f, o32=o32)
        else:
            w_down, w_uq, w_kv = mla_weights(full)
            down = _matmul(h, w_down, mode="nn", name="mla_down", out_dtype=F32)
            gq, gkv = q_gain[j][None, :], mla_kv_norm_g[j][None, :]
            cq, ckr = _mla_mid(down, gq, gkv, rope_k, "mla_mid")
            q_raw = _matmul(cq, w_uq, mode="nn", name="mla_uq", out_dtype=F32)
            q_rot = _rope(q_raw, rope_q, "mla_rope_q")
            kv = _matmul(ckr, w_kv, mode="nn", name="mla_ukv")
            o, lse, *rode = _attn_fwd(q_rot, 0, kv, 0, kv, 16, None, seq, "mla_attn", ride)
            s.update(down=down, gq=gq, gkv=gkv, cq=cq, ckr=ckr, q_rot=q_rot, kv=kv,
                     w_down=w_down, w_uq=w_uq, w_kv=w_kv)
        if ride is not None:
            stacks[i + 1] = rode
        w_out = full["w_out"]
        xs, y = _matmul(o, w_out, mode="nn", name="attn_out", epi="resid_gate", extras=(xs, g_m), seq=seq)
        s.update(o=o, lse=lse, y=y, w_out=w_out, x_mid=xs)
        h2 = _norm_mod(xs, gain_mlp, sc_f, sh_f, seq, "norm_mlp")
        a_pre = _matmul(h2, stacks[i][2], mode="nn", name="mlp_up", layer=("col", 0, w1_rows))
        xs, y2 = _matmul(a_pre, stacks[i][3], mode="nn", name="mlp_down", layer=("row", 0, w2_rows), a_act="relu2",
                         epi="resid_gate", extras=(xs, g_f), seq=seq)
        s.update(h2=h2, a_pre=a_pre, y2=y2)
        saved.append(s)

    loss_part, dx, dg_final = _loss_head(xs, final_norm_g[None, :], loss_target.reshape(t, d), "loss_head")

    w1_cols, w1_tm = mlp_w1.shape[2], _pick(w1_rows, 1024)
    dg_mix, dg_mlp, db_f, dg_kv, dg_q = [None] * depth, [None] * depth, [None] * 2, [None] * 2, [None] * 2
    dmod = [None] * depth
    rel = _relative_blocks()
    chains = dict(fox_w_in=None, fox_w_out=None, mla_w_out=None, mlp_w1=None, mlp_w2=None)
    low_grads = [None] * 2
    waiting = None

    def finish_layer(li, partial, from_chips):
        lj = li // 2

        def step(name, k, layer):
            chains[name] = _total_adamw(partial[k][0], from_chips[k], weights[name], args["m_" + name],
                                        args["v_" + name], layer, chains[name], "adamw_" + name)

        if li % 2 == 0:
            step("fox_w_in", 0, lj)
            step("fox_w_out", 1, lj)
        else:
            low_grads[lj] = _low_shard_grads(_add_parts(partial[0][0], from_chips[0], "grads_total"))
            step("mla_w_out", 1, lj)
        step("mlp_w1", 2, li)
        step("mlp_w2", 3, li)

    for i in reversed(range(depth)):
        j = i // 2
        s = saved[i]
        sh_m, sc_m, g_m, sh_f, sc_f, g_f = (mod[i, :, q] for q in range(6))
        dy2, dg_f = _gate_bwd(dx, s["y2"], g_f, seq, "gate_bwd")
        da_pre = _matmul(dy2, stacks[i][3], mode="nt", name="mlp_down_dx", layer=("row", 0, w2_rows), epi="mul_drelu",
                         extras=(s["a_pre"],))
        g_w2 = _matmul(s["a_pre"], dy2, mode="tn", name="mlp_down_dw", a_act="relu2", tm=w2_rows,
                       into=(jax.ShapeDtypeStruct((N_DEV, w2_rows, d), BF), (None, w2_rows, d),
                             lambda r, j, k: (r, 0, 0)))
        dh2 = _matmul(da_pre, stacks[i][2], mode="nt", name="mlp_up_dx", layer=("col", 0, w1_rows))
        g_w1 = _matmul(s["h2"], da_pre, mode="tn", name="mlp_up_dw", tm=w1_tm, tn=w1_cols,
                       into=(jax.ShapeDtypeStruct((N_DEV, w1_rows, w1_cols), BF), (None, w1_tm, w1_cols),
                             lambda r, j, k: (j, r, 0)))
        dx, dg_mlp[i], dsc_f, dsh_f = _norm_mod_bwd(dh2, s["x_mid"], norm_mlp_g[i][None, :], sc_f, dx, seq,
                                                    "norm_bwd")
        dy, dg_m = _gate_bwd(dx, s["y"], g_m, seq, "gate_bwd")
        do = _matmul(dy, s["w_out"], mode="nt", name="attn_out_dx")
        dw_out = _matmul(s["o"], dy, mode="tn", name="attn_out_dw", out_dtype=F32)
        ride = ("swap", [p[1] for p in waiting[1]]) if waiting is not None else None
        g_mixer = dict(w_out=dw_out)
        if i % 2 == 0:
            qkv = s["qkv"]
            dq, dk, dv, dbias, *rode = _attn_bwd(qkv, 0, qkv, 8, qkv, 32, s["bias"], s["o32"], do, s["lse"], seq,
                                                 "fox_attn_bwd", ride)
            dqkv = jnp.concatenate([dq, dk, dv], axis=1)
            d_cum = -jnp.transpose(dbias[:, :, 0, :], (0, 2, 1)).reshape(t, HEADS)
            d_cum = jnp.pad(d_cum, ((0, 0), (0, LANES - HEADS)))
            dfg, db = _fox_gate_bwd(d_cum, s["fg"], s["b_f"], seq, "fox_gate_bwd")
            db_f[j] = db
            dh = _matmul(dfg, s["w_f"], mode="nt", name="fox_gate_dx", out_dtype=F32)
            dh = _matmul(dqkv, s["w_qkv"], mode="nt", name="fox_qkv_dx", epi="add", extras=(dh,))
            dw_qkv = _matmul(s["h"], dqkv, mode="tn", name="fox_qkv_dw", out_dtype=F32)
            dw_f = _matmul(s["h"], dfg, mode="tn", name="fox_gate_dw", out_dtype=F32)
            g_mixer["fox_w_in"] = jnp.concatenate(
                [_unpad_heads(dw_qkv[:, :2048], HEAD_DIM) * FOX_SCALE, _unpad_heads(dw_qkv[:, 2048:4096], HEAD_DIM),
                 dw_qkv[:, 4096:], dw_f[:, :HEADS]], axis=1)
        else:
            kv = s["kv"]
            dq, dk, dv, *rode = _attn_bwd(s["q_rot"], 0, kv, 0, kv, 16, None, s["o"], do, s["lse"], seq,
                                          "mla_attn_bwd", ride)
            dq_raw = _rope(dq, rope_q_t, "mla_rope_q_bwd")
            dcq = _matmul(dq_raw, s["w_uq"], mode="nt", name="mla_uq_dx")
            dw_uq = _matmul(s["cq"], dq_raw, mode="tn", name="mla_uq_dw", out_dtype=F32)
            dkv = jnp.concatenate([dk, dv], axis=1)
            dckr = _matmul(dkv, s["w_kv"], mode="nt", name="mla_ukv_dx")
            dw_kv = _matmul(s["ckr"], dkv, mode="tn", name="mla_ukv_dw", out_dtype=F32)
            d_down, dgq, dgkv = _mla_mid_bwd(s["down"], dcq, dckr, s["gq"], s["gkv"], rope_k_t, "mla_mid_bwd")
            dg_q[j], dg_kv[j] = dgq, dgkv
            dh = _matmul(d_down, s["w_down"], mode="nt", name="mla_down_dx")
            dw_down = _matmul(s["h"], d_down, mode="tn", name="mla_down_dw", out_dtype=F32)
            g_mixer["mla_w_dq"] = dw_down[:, :256]
            g_mixer["mla_w_dkv"] = jnp.concatenate([dw_down[:, 256:384], dw_down[:, 448:480]], axis=1)
            g_mixer["mla_w_uq"] = _unpad_heads(dw_uq, 96)
            dk_nope = dw_kv[:128, :HEADS * LANES].reshape(128, HEADS, LANES)[:, :, :HEAD_DIM]
            dv_w = dw_kv[:128, HEADS * LANES:].reshape(128, HEADS, HEAD_DIM)
            g_mixer["mla_w_ukv"] = jnp.concatenate([dk_nope, dv_w], axis=2).reshape(128, HEADS * LANES)
        if waiting is not None:
            finish_layer(waiting[0], waiting[1], rode)
        dx, dg_mix[i], dsc_m, dsh_m = _norm_mod_bwd(dh, s["x_in"], norm_mix_g[i][None, :], sc_m, dx, seq,
                                                    "norm_bwd")
        dmod[i] = jnp.stack([dsh_m, dsc_m, dg_m, dsh_f, dsc_f, dg_f], axis=1).reshape(nb, 6 * d)
        bufs = _layer_mixer_grad_bufs(i, g_mixer) + [g_w1, g_w2]
        from_sibling = _swap_with_sibling(bufs, "grads_to_sibling")
        waiting = (i, [_chip_partial(b, fs, rel, "grads_chip_sum") for b, fs in zip(bufs, from_sibling)])

    grad_x = dx.reshape(nb, seq, d)
    finish_layer(waiting[0], waiting[1], _swap_with_chips([p[1] for p in waiting[1]], "grads_to_chips"))
    done = {n: tuple(a.reshape(weights[n].shape) for a in chain) for n, chain in chains.items()}
    shard_grads = {n: jnp.stack([low_grads[0][n], low_grads[1][n]]) for n in low_grads[0]}

    dmod_arr = jnp.stack(dmod)
    wide = lambda a: jnp.pad(a, ((0, 0), (0, d - a.shape[1])))
    pieces = [wide(loss_part), *dg_mix, *dg_mlp, *[wide(a) for a in db_f], *[wide(a) for a in dg_kv], dg_final,
              *[wide(a) for a in dg_q], jnp.sum(dmod_arr, axis=1).reshape(depth * 6, d)]
    n_small = sum(p.shape[0] for p in pieces)
    both = jnp.concatenate(pieces + [dmod_arr.reshape(depth * nb * 6, d)], axis=0)
    both = jnp.pad(both, ((0, (-both.shape[0]) % 8), (0, 0)))
    both_all, = _all_gather([both], "gather_small")
    total = _sum_leading(both_all, "sum_small")
    off = 0

    def take(rows):
        nonlocal off
        out = total[off:off + rows]
        off += rows
        return out

    loss = take(1)[0, 0]
    g_small = dict(
        norm_mix_g=take(depth), norm_mlp_g=take(depth), fox_b_f=take(2)[:, :HEADS], mla_kv_norm_g=take(2)[:, :128],
        final_norm_g=take(1)[0],
        mla_q_norm_g=lax.dynamic_slice_in_dim(take(2)[:, :N_DEV * n_qg], dev * n_qg, n_qg, axis=1),
        ada_b=take(depth * 6).reshape(depth, 6 * d))
    dmod_all = both_all[:, n_small:n_small + depth * nb * 6]
    dmod_all = jnp.transpose(dmod_all.reshape(N_DEV, depth, nb, 6 * d), (1, 0, 2, 3)).reshape(depth, N_DEV * nb, 6 * d)
    dmod_cols = lax.dynamic_slice_in_dim(dmod_all, dev * n_mod_local, n_mod_local, axis=2)
    g_ada_w = jnp.stack([_matmul(c_all, dmod_cols[i], mode="tn", name="ada_dw", out_dtype=F32, a_act="silu")
                         for i in range(depth)])

    all_grads = dict(shard_grads)
    all_grads.update(g_small)
    all_grads["ada_w"] = g_ada_w

    deltas, new_m, new_v = {}, {}, {}
    for n in WEIGHTS:
        if n in done:
            all_grads[n], deltas[n], new_m[n], new_v[n] = done[n]
        else:
            deltas[n], new_m[n], new_v[n] = _adamw(weights[n], all_grads[n], args["m_" + n], args["v_" + n], "adamw")

    return (loss, grad_x, *[all_grads[n] for n in WEIGHTS], *[deltas[n] for n in WEIGHTS],
            *[new_m[n] for n in WEIGHTS], *[new_v[n] for n in WEIGHTS])
```

```python
import functools
import math

import jax
import jax.numpy as jnp
import numpy as np
from jax import lax
from jax.experimental import pallas as pl
from jax.experimental.pallas import tpu as pltpu

F32 = jnp.float32
BF = jnp.bfloat16

N_DEV = 8
HEADS = 16
HEAD_PAIRS = HEADS // 2
HEAD_DIM = 64
LANES = 128
ROPE_HALF = 16
NORM_EPS = 1e-6
MLA_SCALE = 96.0 ** -0.5
FOX_SCALE = 0.125
ATTN_BLOCK = 512
ROW_BLOCK = 512
K_SPAN = 2
VMEM_LIMIT = 56 * 1024 * 1024
MESH = pl.DeviceIdType.MESH

ADAM_LR = 0.001
ADAM_B1 = 0.9
ADAM_B2 = 0.999
ADAM_EPS = 1e-08
ADAM_WD = 0.01
ADAM_STEP = 10

WEIGHTS = ("ada_w", "ada_b", "norm_mix_g", "norm_mlp_g", "fox_w_in", "fox_b_f", "fox_w_out", "mla_w_dq",
           "mla_q_norm_g", "mla_w_uq", "mla_w_dkv", "mla_kv_norm_g", "mla_w_ukv", "mla_w_out", "mlp_w1",
           "mlp_w2", "final_norm_g")


def _params(sem=None):
    return pltpu.CompilerParams(dimension_semantics=sem, vmem_limit_bytes=VMEM_LIMIT)


def _pick(n, target):
    if n <= target:
        return n
    for t in range(target, 127, -128):
        if n % t == 0:
            return t
    return n


def _rows(n, target=512):
    if n <= target:
        return n
    for t in range(target, 7, -8):
        if n % t == 0:
            return t
    return n


def _place():
    x, y, c = lax.axis_index("x"), lax.axis_index("y"), lax.axis_index("c")
    return x, y, c


def _adamw_math(w, g, m, v):
    nm = ADAM_B1 * m + (1.0 - ADAM_B1) * g
    nv = ADAM_B2 * v + (1.0 - ADAM_B2) * (g * g)
    m_hat = nm * (1.0 / (1.0 - ADAM_B1 ** ADAM_STEP))
    v_hat = nv * (1.0 / (1.0 - ADAM_B2 ** ADAM_STEP))
    return -ADAM_LR * (m_hat / (jnp.sqrt(v_hat) + ADAM_EPS) + ADAM_WD * w), nm, nv


def _all_gather(blocks, name):
    ride = ("gather", blocks)

    def body(*refs):
        start, mid, finish = _ride_phases(ride, *_ride_split(ride, refs, 0, 0)[:3])
        start()
        mid()
        finish()

    in_specs, out_shape, out_specs, scratch = _ride_specs(ride)
    return pl.pallas_call(
        body, name=name, out_shape=out_shape, in_specs=in_specs, out_specs=out_specs, scratch_shapes=scratch,
    )(*blocks)


def _ride_specs(ride):
    kind, arrays = ride
    n = len(arrays)
    any_spec = pl.BlockSpec(memory_space=pl.ANY)
    if kind == "gather":
        out_shape = [jax.ShapeDtypeStruct((N_DEV,) + b.shape, b.dtype) for b in arrays]
        scratch = [pltpu.SemaphoreType.DMA((7 * n,)), pltpu.SemaphoreType.DMA((7 * n,)), pltpu.SemaphoreType.DMA((n,))]
    else:
        out_shape = [jax.ShapeDtypeStruct(p.shape, p.dtype) for p in arrays]
        scratch = [pltpu.SemaphoreType.DMA((3 * n,)), pltpu.SemaphoreType.DMA((3 * n,))]
    return [any_spec] * n, out_shape, [any_spec] * n, scratch


def _ride_split(ride, refs, n_in, n_out):
    n = len(ride[1])
    n_sem = 3 if ride[0] == "gather" else 2
    src = refs[n_in:n_in + n]
    dst = refs[n_in + n + n_out:n_in + 2 * n + n_out]
    own = refs[:n_in] + refs[n_in + n:n_in + n + n_out] + refs[n_in + 2 * n + n_out:len(refs) - n_sem]
    return src, dst, refs[len(refs) - n_sem:], own


def _ride_phases(ride, src, dst, sems):
    n = len(src)
    x, y, c = _place()
    chips = [(1 - x, y), (x, 1 - y), (1 - x, 1 - y)]
    if ride[0] == "swap":
        send_sems, recv_sems = sems

        def copies():
            return [pltpu.make_async_remote_copy(
                src_ref=src[a].at[k], dst_ref=dst[a].at[k], send_sem=send_sems.at[3 * a + k],
                recv_sem=recv_sems.at[3 * a + k], device_id=(*chip, c), device_id_type=MESH)
                for k, chip in enumerate(chips) for a in range(n)]

        def start():
            for cp in copies():
                cp.start()

        def finish():
            for cp in copies():
                cp.wait()

        return start, lambda: None, finish

    send_sems, recv_sems, local_sems = sems
    me, sibling = (x, y, c), (x, y, 1 - c)

    def slot(a, px, py, pc):
        return dst[a].at[4 * px + 2 * py + pc]

    def copy(a, k, blk, to, from_src=False):
        return pltpu.make_async_remote_copy(
            src_ref=src[a] if from_src else slot(a, *blk), dst_ref=slot(a, *blk),
            send_sem=send_sems.at[7 * a + k], recv_sem=recv_sems.at[7 * a + k], device_id=to, device_id_type=MESH)

    def mine():
        return [pltpu.make_async_copy(src[a], slot(a, *me), local_sems.at[a]) for a in range(n)]

    def first():
        out = []
        for j, chip in enumerate(chips):
            out += [copy(a, 1 + j, me, (*chip, c), from_src=True) for a in range(n)]
        return out + [copy(a, 0, me, sibling, from_src=True) for a in range(n)]

    def passed():
        return [copy(a, 4 + j, (*chip, c), sibling) for j, chip in enumerate(chips) for a in range(n)]

    def start():
        for cp in mine() + first():
            cp.start()

    def mid():
        for j, chip in enumerate(chips):
            for a in range(n):
                copy(a, 1 + j, (*chip, c), me).wait_recv()
        for cp in passed():
            cp.start()

    def finish():
        for a in range(n):
            copy(a, 0, sibling, me).wait_recv()
        for j, chip in enumerate(chips):
            for a in range(n):
                copy(a, 4 + j, (*chip, 1 - c), me).wait_recv()
        for cp in first() + passed():
            cp.wait_send()
        for cp in mine():
            cp.wait()

    return start, mid, finish


def _ride_steps(ride, refs, n_in, n_out, step, n_steps):
    if ride is None:
        return refs, lambda: None
    src, dst, sems, own = _ride_split(ride, refs, n_in, n_out)
    start, mid, finish = _ride_phases(ride, src, dst, sems)
    pl.when(step == 0)(start)
    pl.when(step == (3 * n_steps) // 4)(mid)
    return own, lambda: pl.when(step == n_steps - 1)(finish)


def _swap_with_sibling(bufs, name):
    n = len(bufs)

    def body(*refs):
        src, dst = refs[:n], refs[n:2 * n]
        send_sems, recv_sems = refs[2 * n:]
        x, y, c = _place()
        cps = []
        for a in range(n):
            for k in range(4):
                cps.append(pltpu.make_async_remote_copy(
                    src_ref=src[a].at[2 * k + (1 - c)], dst_ref=dst[a].at[k],
                    send_sem=send_sems.at[4 * a + k], recv_sem=recv_sems.at[4 * a + k],
                    device_id=(x, y, 1 - c), device_id_type=MESH))
        for cp in cps:
            cp.start()
        for cp in cps:
            cp.wait()

    any_spec = pl.BlockSpec(memory_space=pl.ANY)
    return pl.pallas_call(
        body, name=name,
        out_shape=[jax.ShapeDtypeStruct((4,) + b.shape[1:], b.dtype) for b in bufs],
        in_specs=[any_spec] * n, out_specs=[any_spec] * n,
        scratch_shapes=[pltpu.SemaphoreType.DMA((4 * n,)), pltpu.SemaphoreType.DMA((4 * n,))],
    )(*bufs)


def _swap_with_chips(parts, name):
    ride = ("swap", parts)

    def body(*refs):
        start, _, finish = _ride_phases(ride, *_ride_split(ride, refs, 0, 0)[:3])
        start()
        finish()

    in_specs, out_shape, out_specs, scratch = _ride_specs(ride)
    return pl.pallas_call(
        body, name=name, out_shape=out_shape, in_specs=in_specs, out_specs=out_specs, scratch_shapes=scratch,
    )(*parts)


def _relative_blocks():
    x, y, c = _place()
    flips = ((0, 0), (1, 0), (0, 1), (1, 1))
    mine = [4 * (x ^ fx) + 2 * (y ^ fy) + c for fx, fy in flips]
    sib = [2 * (x ^ fx) + (y ^ fy) for fx, fy in flips]
    return jnp.stack(mine + sib).astype(jnp.int32)


def _chip_partial(buf, from_sibling, rel, name):
    _, r, cdim = buf.shape
    tr = _rows(r, 256)

    def body(rel_ref, m0, m1, m2, m3, s0, s1, s2, s3, own_ref, parts_ref):
        del rel_ref
        own_ref[...] = m0[...].astype(F32) + s0[...].astype(F32)
        for k, (m, s) in enumerate(((m1, s1), (m2, s2), (m3, s3))):
            parts_ref[k] = (m[...].astype(F32) + s[...].astype(F32)).astype(parts_ref.dtype)

    def pick(k):
        return pl.BlockSpec((None, tr, cdim), lambda i, rel_ref: (rel_ref[k], i, 0))

    return pl.pallas_call(
        body, name=name,
        grid_spec=pltpu.PrefetchScalarGridSpec(
            num_scalar_prefetch=1, grid=(r // tr,),
            in_specs=[pick(k) for k in range(8)],
            out_specs=(pl.BlockSpec((tr, cdim), lambda i, rel_ref: (i, 0)),
                       pl.BlockSpec((3, tr, cdim), lambda i, rel_ref: (0, i, 0)))),
        out_shape=(jax.ShapeDtypeStruct((r, cdim), F32), jax.ShapeDtypeStruct((3, r, cdim), buf.dtype)),
        compiler_params=_params(("parallel",)),
    )(rel, buf, buf, buf, buf, from_sibling, from_sibling, from_sibling, from_sibling)


def _total_adamw(own, parts, w, m, v, layer, carry, name):
    r, cdim = own.shape
    n_layers = w.shape[0]
    tr = _rows(r, 256)
    steps = r // tr

    def body(own_ref, parts_ref, w_ref, m_ref, v_ref, *rest):
        g_ref, d_ref, nm_ref, nv_ref = rest[-4:]
        g = own_ref[...]
        for k in range(3):
            g = g + parts_ref[k].astype(F32)
        g_ref[...] = g
        d_ref[...], nm_ref[...], nv_ref[...] = _adamw_math(w_ref[...], g, m_ref[...], v_ref[...])

    blk = pl.BlockSpec((tr, cdim), lambda i: (i, 0))
    lay = pl.BlockSpec((tr, cdim), lambda i: (layer * steps + i, 0))
    in_specs = [blk, pl.BlockSpec((3, tr, cdim), lambda i: (0, i, 0)), lay, lay, lay]
    operands = [own, parts, *[a.reshape(n_layers * r, cdim) for a in (w, m, v)]]
    aliases = {}
    if carry is not None:
        in_specs += [pl.BlockSpec(memory_space=pl.ANY)] * 4
        operands += list(carry)
        aliases = {5 + k: k for k in range(4)}
    sds = jax.ShapeDtypeStruct((n_layers * r, cdim), F32)
    return pl.pallas_call(
        body, name=name, grid=(steps,),
        out_shape=(sds,) * 4, in_specs=in_specs, out_specs=(lay,) * 4, input_output_aliases=aliases,
        compiler_params=_params(("parallel",)),
    )(*operands)


def _add_parts(own, parts, name):
    r, cdim = own.shape
    tr = _rows(r, 512)

    def body(own_ref, parts_ref, out_ref):
        acc = own_ref[...]
        for k in range(parts_ref.shape[0]):
            acc = acc + parts_ref[k].astype(F32)
        out_ref[...] = acc

    return pl.pallas_call(
        body, name=name, grid=(r // tr,),
        out_shape=jax.ShapeDtypeStruct((r, cdim), F32),
        in_specs=[pl.BlockSpec((tr, cdim), lambda i: (i, 0)),
                  pl.BlockSpec((parts.shape[0], tr, cdim), lambda i: (0, i, 0))],
        out_specs=pl.BlockSpec((tr, cdim), lambda i: (i, 0)),
        compiler_params=_params(("parallel",)),
    )(own, parts)


def _sum_leading(stack, name):
    n, r, cdim = stack.shape
    tr = _rows(r, 512)

    def body(in_ref, out_ref):
        acc = in_ref[0]
        for k in range(1, n):
            acc = acc + in_ref[k]
        out_ref[...] = acc

    return pl.pallas_call(
        body, name=name, grid=(r // tr,),
        out_shape=jax.ShapeDtypeStruct((r, cdim), F32),
        in_specs=[pl.BlockSpec((n, tr, cdim), lambda i: (0, i, 0))],
        out_specs=pl.BlockSpec((tr, cdim), lambda i: (i, 0)),
        compiler_params=_params(("parallel",)),
    )(stack)


_DIMS = {"nn": (((1,), (0,)), ((), ())), "nt": (((1,), (1,)), ((), ())), "tn": (((0,), (0,)), ((), ()))}


def _stack_spec(shape, mode, layer):
    cut, l, rows = layer
    cols = shape[2]
    by_n = pl.BlockSpec((1, rows, cols), lambda i, j, k: (j, l, 0))
    by_k = pl.BlockSpec((K_SPAN, rows, cols), lambda i, j, k: (k, l, 0))
    if cut == "col":
        return (by_n, N_DEV * cols, cols, rows) if mode == "nn" else (by_k, rows, rows, K_SPAN * cols)
    return (by_k, cols, cols, K_SPAN * rows) if mode == "nn" else (by_n, N_DEV * rows, rows, cols)


def _matmul(a, b, *, mode, name, out_dtype=BF, a_act=None, epi=None, extras=(), seq=None, layer=None, tm=None,
            tn=None, into=None):
    if mode == "tn":
        kdim, m = a.shape
    else:
        m, kdim = a.shape
    if tm is None:
        tm = _pick(m, 1024 if epi != "resid_gate" else min(1024, seq))
    tk = _pick(kdim, 2048 if mode == "tn" else 1024)
    b_spec = None
    if layer is not None:
        b_spec, n, tn, tk = _stack_spec(b.shape, mode, layer)
    else:
        n = b.shape[0] if mode == "nt" else b.shape[1]
        tn = _pick(n, 1024) if tn is None else tn
    nk = kdim // tk
    a_spec = (pl.BlockSpec((tk, tm), lambda i, j, k: (k, i)) if mode == "tn"
              else pl.BlockSpec((tm, tk), lambda i, j, k: (i, k)))
    if b_spec is None:
        b_spec = (pl.BlockSpec((tn, tk), lambda i, j, k: (j, k)) if mode == "nt"
                  else pl.BlockSpec((tk, tn), lambda i, j, k: (k, j)))
    tile = pl.BlockSpec((tm, tn), lambda i, j, k: (i, j))
    in_specs, out_specs = [a_spec, b_spec], [tile]
    out_shape = [jax.ShapeDtypeStruct((m, n), out_dtype)]
    if epi == "resid_gate":
        in_specs += [tile, pl.BlockSpec((None, 1, tn), lambda i, j, k: ((i * tm) // seq, 0, j))]
        out_shape = [jax.ShapeDtypeStruct((m, n), F32), jax.ShapeDtypeStruct((m, n), BF)]
        out_specs = [tile, tile]
    elif epi in ("mul_drelu", "add"):
        in_specs += [tile]
    elif epi == "bias":
        in_specs += [pl.BlockSpec((1, tn), lambda i, j, k: (0, j))]
    n_extra, n_out = len(in_specs) - 2, len(out_specs)
    aliases, n_kept = {}, 0
    if into is not None:
        buffer, block, index_map = into
        out_dtype = buffer.dtype
        if not isinstance(buffer, jax.ShapeDtypeStruct):
            in_specs.append(pl.BlockSpec(memory_space=pl.ANY))
            extras = tuple(extras) + (buffer,)
            aliases, n_kept = {len(in_specs) - 1: 0}, 1
        out_shape = [jax.ShapeDtypeStruct(buffer.shape, buffer.dtype)]
        out_specs = [pl.BlockSpec(block, index_map)]
    dims = _DIMS[mode]

    def body(*refs):
        a_ref, b_ref = refs[:2]
        ex = refs[2:2 + n_extra]
        outs = refs[2 + n_extra + n_kept:2 + n_extra + n_kept + n_out]
        av = a_ref[...]
        if a_act == "relu2":
            t = jnp.maximum(av.astype(F32), 0.0)
            av = t * t
        elif a_act == "silu":
            t = av.astype(F32)
            av = t / (1.0 + jnp.exp(-t))
        av = av.astype(BF)
        if layer is None:
            part = lax.dot_general(av, b_ref[...].astype(BF), dims, preferred_element_type=F32)
        else:
            span = b_ref.shape[0]
            wk = av.shape[1] // span
            part = None
            for u in range(span):
                p_u = lax.dot_general(av[:, u * wk:(u + 1) * wk], b_ref[u], dims, preferred_element_type=F32)
                part = p_u if part is None else part + p_u

        def finish(acc):
            if epi == "resid_gate":
                outs[0][...] = ex[0][...] + ex[1][...] * acc
                outs[1][...] = acc.astype(BF)
            elif epi == "mul_drelu":
                outs[0][...] = (acc * (2.0 * jnp.maximum(ex[0][...].astype(F32), 0.0))).astype(out_dtype)
            elif epi == "add":
                outs[0][...] = (acc + ex[0][...].astype(F32)).astype(out_dtype)
            elif epi == "bias":
                outs[0][...] = (acc + ex[0][...]).astype(out_dtype)
            else:
                outs[0][...] = acc.astype(out_dtype)

        if nk == 1:
            finish(part)
        else:
            acc_ref = refs[-1]
            k = pl.program_id(2)

            @pl.when(k == 0)
            def _():
                acc_ref[...] = part

            @pl.when(k > 0)
            def _():
                acc_ref[...] += part

            @pl.when(k == nk - 1)
            def _():
                finish(acc_ref[...])

    res = pl.pallas_call(
        body, name=name, grid=(m // tm, n // tn, nk),
        out_shape=out_shape, in_specs=in_specs, out_specs=out_specs,
        scratch_shapes=[pltpu.VMEM((tm, tn), F32)] if nk > 1 else [],
        input_output_aliases=aliases,
        compiler_params=_params(("parallel", "parallel", "arbitrary")),
    )(a, b, *extras)
    return res if n_out > 1 else res[0]


def _norm_mod(x, gain, scale, shift, seq, name):
    t, w = x.shape
    tr = ROW_BLOCK

    def body(x_ref, g_ref, sc_ref, sh_ref, out_ref):
        xv = x_ref[...]
        rstd = lax.rsqrt(jnp.mean(xv * xv, axis=-1, keepdims=True) + NORM_EPS)
        y = xv * rstd * g_ref[...]
        out_ref[...] = (y * (1.0 + sc_ref[...]) + sh_ref[...]).astype(BF)

    per_b = pl.BlockSpec((None, 1, w), lambda i: ((i * tr) // seq, 0, 0))
    return pl.pallas_call(
        body, name=name, grid=(t // tr,),
        out_shape=jax.ShapeDtypeStruct((t, w), BF),
        in_specs=[pl.BlockSpec((tr, w), lambda i: (i, 0)), pl.BlockSpec((1, w), lambda i: (0, 0)), per_b, per_b],
        out_specs=pl.BlockSpec((tr, w), lambda i: (i, 0)),
        compiler_params=_params(("parallel",)),
    )(x, gain, scale, shift)


def _norm_mod_bwd(dh, x, gain, scale, dres, seq, name, gate=None):
    t, w = x.shape
    tr = ROW_BLOCK
    steps_per_seq = seq // tr
    nb = t // seq
    gated = gate is not None

    def body(*refs):
        dh_ref, x_ref, g_ref, sc_ref, dres_ref = refs[:5]
        dx_ref, dg_ref, dsc_ref, dsh_ref = refs[-6:-2] if gated else refs[-4:]
        i = pl.program_id(0)
        xv = x_ref[...]
        dhv = dh_ref[...].astype(F32)
        rstd = lax.rsqrt(jnp.mean(xv * xv, axis=-1, keepdims=True) + NORM_EPS)
        xhat = xv * rstd
        one_sc = 1.0 + sc_ref[...]
        g = g_ref[...]
        dxhat = dhv * (g * one_sc)
        proj = jnp.mean(dxhat * xhat, axis=-1, keepdims=True)
        dxv = dres_ref[...] + rstd * (dxhat - xhat * proj)
        dx_ref[...] = dxv
        dhx = dhv * xhat
        first = [(dg_ref, jnp.sum(dhx * one_sc, axis=0, keepdims=True))]
        per_seq = [(dsc_ref, jnp.sum(dhx * g, axis=0, keepdims=True)), (dsh_ref, jnp.sum(dhv, axis=0, keepdims=True))]
        if gated:
            y_ref, gate_ref, dy_ref, dgate_ref = refs[5], refs[6], refs[-2], refs[-1]
            dy_ref[...] = (dxv * gate_ref[...]).astype(BF)
            per_seq.append((dgate_ref, jnp.sum(dxv * y_ref[...].astype(F32), axis=0, keepdims=True)))
        for cond_new, cond_add, group in ((i == 0, i > 0, first),
                                          (i % steps_per_seq == 0, i % steps_per_seq != 0, per_seq)):
            @pl.when(cond_new)
            def _(group=group):
                for ref, part in group:
                    ref[...] = part

            @pl.when(cond_add)
            def _(group=group):
                for ref, part in group:
                    ref[...] += part

    row = pl.BlockSpec((tr, w), lambda i: (i, 0))
    per_b = pl.BlockSpec((None, 1, w), lambda i: ((i * tr) // seq, 0, 0))
    vec = pl.BlockSpec((1, w), lambda i: (0, 0))
    out_shape = [jax.ShapeDtypeStruct((t, w), F32), jax.ShapeDtypeStruct((1, w), F32),
                 jax.ShapeDtypeStruct((nb, 1, w), F32), jax.ShapeDtypeStruct((nb, 1, w), F32)]
    in_specs, out_specs, operands = [row, row, vec, per_b, row], [row, vec, per_b, per_b], [dh, x, gain, scale, dres]
    if gated:
        in_specs += [row, per_b]
        operands += list(gate)
        out_shape += [jax.ShapeDtypeStruct((t, w), BF), jax.ShapeDtypeStruct((nb, 1, w), F32)]
        out_specs += [row, per_b]
    return pl.pallas_call(
        body, name=name, grid=(t // tr,),
        out_shape=out_shape, in_specs=in_specs, out_specs=out_specs,
        compiler_params=_params(("arbitrary",)),
    )(*operands)


def _gate_bwd(dx, y, gate, seq, name):
    t, w = dx.shape
    tr = ROW_BLOCK
    steps_per_seq = seq // tr
    nb = t // seq

    def body(dx_ref, y_ref, g_ref, dy_ref, dg_ref):
        i = pl.program_id(0)
        dxv = dx_ref[...]
        dy_ref[...] = (dxv * g_ref[...]).astype(BF)
        part = jnp.sum(dxv * y_ref[...].astype(F32), axis=0, keepdims=True)

        @pl.when(i % steps_per_seq == 0)
        def _():
            dg_ref[...] = part

        @pl.when(i % steps_per_seq != 0)
        def _():
            dg_ref[...] += part

    row = pl.BlockSpec((tr, w), lambda i: (i, 0))
    per_b = pl.BlockSpec((None, 1, w), lambda i: ((i * tr) // seq, 0, 0))
    return pl.pallas_call(
        body, name=name, grid=(t // tr,),
        out_shape=(jax.ShapeDtypeStruct((t, w), BF), jax.ShapeDtypeStruct((nb, 1, w), F32)),
        in_specs=[row, row, per_b], out_specs=(row, per_b),
        compiler_params=_params(("arbitrary",)),
    )(dx, y, gate)


def _loss_head(x, gain, target, name):
    t, w = x.shape
    tr = ROW_BLOCK

    def body(x_ref, g_ref, t_ref, loss_ref, dx_ref, dg_ref):
        i = pl.program_id(0)
        xv = x_ref[...]
        g = g_ref[...]
        rstd = lax.rsqrt(jnp.mean(xv * xv, axis=-1, keepdims=True) + NORM_EPS)
        xhat = xv * rstd
        err = xhat * g - t_ref[...]
        row_loss = jnp.sum(err * err, axis=-1, keepdims=True) * (0.5 / w)
        loss_part = jnp.broadcast_to(jnp.sum(row_loss, axis=0, keepdims=True), (1, LANES))
        dy = err * (1.0 / w)
        dg_part = jnp.sum(dy * xhat, axis=0, keepdims=True)
        dxhat = dy * g
        proj = jnp.mean(dxhat * xhat, axis=-1, keepdims=True)
        dx_ref[...] = rstd * (dxhat - xhat * proj)

        @pl.when(i == 0)
        def _():
            loss_ref[...] = loss_part
            dg_ref[...] = dg_part

        @pl.when(i > 0)
        def _():
            loss_ref[...] += loss_part
            dg_ref[...] += dg_part

    row = pl.BlockSpec((tr, w), lambda i: (i, 0))
    vec = pl.BlockSpec((1, w), lambda i: (0, 0))
    return pl.pallas_call(
        body, name=name, grid=(t // tr,),
        out_shape=(jax.ShapeDtypeStruct((1, LANES), F32), jax.ShapeDtypeStruct((t, w), F32),
                   jax.ShapeDtypeStruct((1, w), F32)),
        in_specs=[row, vec, row],
        out_specs=(pl.BlockSpec((1, LANES), lambda i: (0, 0)), row, vec),
        compiler_params=_params(("arbitrary",)),
    )(x, gain, target)


def _rope_group(xg, cos_p, sin_a, sin_b):
    return (xg * cos_p + pltpu.roll(xg, LANES - ROPE_HALF, axis=1) * sin_a
            + pltpu.roll(xg, ROPE_HALF, axis=1) * sin_b)


def _rope(x, tables, name, out_dtype=BF):
    t, w = x.shape
    tr = ROW_BLOCK
    groups = w // LANES

    def body(x_ref, c_ref, a_ref, b_ref, out_ref):
        cos_p, sin_a, sin_b = c_ref[...], a_ref[...], b_ref[...]
        for g in range(groups):
            sl = slice(g * LANES, (g + 1) * LANES)
            out_ref[:, sl] = _rope_group(x_ref[:, sl].astype(F32), cos_p, sin_a, sin_b).astype(out_dtype)

    row = pl.BlockSpec((tr, w), lambda i: (i, 0))
    tab = pl.BlockSpec((tr, LANES), lambda i: (i, 0))
    return pl.pallas_call(
        body, name=name, grid=(t // tr,),
        out_shape=jax.ShapeDtypeStruct((t, w), out_dtype),
        in_specs=[row, tab, tab, tab], out_specs=row,
        compiler_params=_params(("parallel",)),
    )(x, *tables)


def _mla_mid(down, gq, gkv, tables, name):
    t = down.shape[0]
    tr = ROW_BLOCK

    def body(d_ref, gq_ref, gkv_ref, c_ref, a_ref, b_ref, cq_ref, ckr_ref):
        q = d_ref[:, 0:256]
        cq_ref[...] = (q * lax.rsqrt(jnp.mean(q * q, axis=-1, keepdims=True) + NORM_EPS) * gq_ref[...]).astype(BF)
        kv = d_ref[:, 256:384]
        ckr_ref[:, 0:128] = (kv * lax.rsqrt(jnp.mean(kv * kv, axis=-1, keepdims=True) + NORM_EPS)
                             * gkv_ref[...]).astype(BF)
        ckr_ref[:, 128:256] = _rope_group(d_ref[:, 384:512], c_ref[...], a_ref[...], b_ref[...]).astype(BF)

    tab = pl.BlockSpec((tr, LANES), lambda i: (i, 0))
    return pl.pallas_call(
        body, name=name, grid=(t // tr,),
        out_shape=(jax.ShapeDtypeStruct((t, 256), BF), jax.ShapeDtypeStruct((t, 256), BF)),
        in_specs=[pl.BlockSpec((tr, 512), lambda i: (i, 0)), pl.BlockSpec((1, 256), lambda i: (0, 0)),
                  pl.BlockSpec((1, 128), lambda i: (0, 0)), tab, tab, tab],
        out_specs=(pl.BlockSpec((tr, 256), lambda i: (i, 0)), pl.BlockSpec((tr, 256), lambda i: (i, 0))),
        compiler_params=_params(("parallel",)),
    )(down, gq, gkv, *tables)


def _mla_mid_bwd(down, dcq, dckr, gq, gkv, tables_t, name):
    t = down.shape[0]
    tr = ROW_BLOCK

    def norm_bwd(xv, g, dy):
        rstd = lax.rsqrt(jnp.mean(xv * xv, axis=-1, keepdims=True) + NORM_EPS)
        xhat = xv * rstd
        dxhat = dy * g
        proj = jnp.mean(dxhat * xhat, axis=-1, keepdims=True)
        return rstd * (dxhat - xhat * proj), jnp.sum(dy * xhat, axis=0, keepdims=True)

    def body(d_ref, dcq_ref, dckr_ref, gq_ref, gkv_ref, c_ref, a_ref, b_ref, dd_ref, dgq_ref, dgkv_ref):
        i = pl.program_id(0)
        dq, dgq_part = norm_bwd(d_ref[:, 0:256], gq_ref[...], dcq_ref[...].astype(F32))
        dd_ref[:, 0:256] = dq.astype(BF)
        dkv, dgkv_part = norm_bwd(d_ref[:, 256:384], gkv_ref[...], dckr_ref[:, 0:128].astype(F32))
        dd_ref[:, 256:384] = dkv.astype(BF)
        dd_ref[:, 384:512] = _rope_group(dckr_ref[:, 128:256].astype(F32), c_ref[...], a_ref[...],
                                         b_ref[...]).astype(BF)

        @pl.when(i == 0)
        def _():
            dgq_ref[...] = dgq_part
            dgkv_ref[...] = dgkv_part

        @pl.when(i > 0)
        def _():
            dgq_ref[...] += dgq_part
            dgkv_ref[...] += dgkv_part

    tab = pl.BlockSpec((tr, LANES), lambda i: (i, 0))
    r256 = pl.BlockSpec((tr, 256), lambda i: (i, 0))
    return pl.pallas_call(
        body, name=name, grid=(t // tr,),
        out_shape=(jax.ShapeDtypeStruct((t, 512), BF), jax.ShapeDtypeStruct((1, 256), F32),
                   jax.ShapeDtypeStruct((1, 128), F32)),
        in_specs=[pl.BlockSpec((tr, 512), lambda i: (i, 0)), r256, r256, pl.BlockSpec((1, 256), lambda i: (0, 0)),
                  pl.BlockSpec((1, 128), lambda i: (0, 0)), tab, tab, tab],
        out_specs=(pl.BlockSpec((tr, 512), lambda i: (i, 0)), pl.BlockSpec((1, 256), lambda i: (0, 0)),
                   pl.BlockSpec((1, 128), lambda i: (0, 0))),
        compiler_params=_params(("arbitrary",)),
    )(down, dcq, dckr, gq, gkv, *tables_t)


def _scan_rows(x, reverse):
    s = x.shape[0]
    row = lax.broadcasted_iota(jnp.int32, x.shape, 0)
    step = 1
    while step < s:
        if reverse:
            x = x + jnp.where(row < s - step, pltpu.roll(x, s - step, axis=0), 0.0)
        else:
            x = x + jnp.where(row >= step, pltpu.roll(x, step, axis=0), 0.0)
        step *= 2
    return x


def _fox_gate(fg, b_f, seq, name):
    t = fg.shape[0]

    def body(fg_ref, b_ref, out_ref):
        z = fg_ref[...] + b_ref[...]
        log_f = jnp.minimum(z, 0.0) - jnp.log(1.0 + jnp.exp(-jnp.abs(z)))
        out_ref[...] = _scan_rows(log_f, reverse=False)

    blk = pl.BlockSpec((seq, LANES), lambda b: (b, 0))
    return pl.pallas_call(
        body, name=name, grid=(t // seq,),
        out_shape=jax.ShapeDtypeStruct((t, LANES), F32),
        in_specs=[blk, pl.BlockSpec((1, LANES), lambda b: (0, 0))], out_specs=blk,
        compiler_params=_params(("parallel",)),
    )(fg, b_f)


def _fox_gate_bwd(d_cum, fg, b_f, seq, name):
    t = fg.shape[0]

    def body(dc_ref, fg_ref, b_ref, dfg_ref, db_ref):
        b = pl.program_id(0)
        z = fg_ref[...] + b_ref[...]
        d_log_f = _scan_rows(dc_ref[...], reverse=True)
        dz = d_log_f / (1.0 + jnp.exp(z))
        dfg_ref[...] = dz
        part = jnp.sum(dz, axis=0, keepdims=True)

        @pl.when(b == 0)
        def _():
            db_ref[...] = part

        @pl.when(b > 0)
        def _():
            db_ref[...] += part

    blk = pl.BlockSpec((seq, LANES), lambda b: (b, 0))
    vec = pl.BlockSpec((1, LANES), lambda b: (0, 0))
    return pl.pallas_call(
        body, name=name, grid=(t // seq,),
        out_shape=(jax.ShapeDtypeStruct((t, LANES), F32), jax.ShapeDtypeStruct((1, LANES), F32)),
        in_specs=[blk, blk, vec], out_specs=(blk, vec),
        compiler_params=_params(("arbitrary",)),
    )(d_cum, fg, b_f)


def _head_masks():
    lane = lax.broadcasted_iota(jnp.int32, (1, LANES), 1)
    return lane < HEAD_DIM, lane >= HEAD_DIM


def _attn_fwd(q_arr, q_off, k_arr, k_off, v_arr, v_off, bias, seq, name, ride=None):
    t = q_arr.shape[0]
    nb = t // seq
    blk = min(ATTN_BLOCK, seq)
    nq = seq // blk
    has_bias = bias is not None
    n_in, n_out = (4, 3) if has_bias else (3, 2)

    def body(*refs):
        step = pl.program_id(0) * HEAD_PAIRS + pl.program_id(1)
        refs, ride_end = _ride_steps(ride, refs, n_in, n_out, step, nb * HEAD_PAIRS)
        if has_bias:
            q_ref, k_ref, v_ref, bias_ref, o_ref, lse_ref, o32_ref = refs
        else:
            q_ref, k_ref, v_ref, o_ref, lse_ref = refs
        lo, hi = _head_masks()
        rows = lax.broadcasted_iota(jnp.int32, (blk, blk), 0)
        cols = lax.broadcasted_iota(jnp.int32, (blk, blk), 1)
        causal = cols <= rows

        def q_block(iq, _):
            q0 = pl.multiple_of(iq * blk, blk)
            qs = [q_ref[pl.ds(q0, blk), h * LANES:(h + 1) * LANES] for h in range(2)]

            def kv_block(j, carry, diag):
                k0 = pl.multiple_of(j * blk, blk)
                vv = v_ref[pl.ds(k0, blk), :]
                vs = [jnp.where(lo, vv, jnp.zeros_like(vv)), jnp.where(hi, vv, jnp.zeros_like(vv))]
                acc = carry[0]
                new = []
                pv = None
                alphas = []
                for h in range(2):
                    m, l = carry[1 + 2 * h], carry[2 + 2 * h]
                    kk = k_ref[pl.ds(k0, blk), h * LANES:(h + 1) * LANES]
                    s = lax.dot_general(qs[h], kk, _DIMS["nt"], preferred_element_type=F32)
                    if has_bias:
                        s = s + bias_ref[h, 0:1, pl.ds(k0, blk)]
                    if diag:
                        s = jnp.where(causal, s, -jnp.inf)
                    m_new = jnp.maximum(m, jnp.max(s, axis=-1, keepdims=True))
                    p = jnp.exp(s - m_new)
                    alpha = jnp.exp(m - m_new)
                    l_new = alpha * l + jnp.sum(p, axis=-1, keepdims=True)
                    p_hi = p.astype(BF)
                    d = jnp.dot(p_hi, vs[h], preferred_element_type=F32)
                    if has_bias:
                        p_lo = (p - p_hi.astype(F32)).astype(BF)
                        d = d + jnp.dot(p_lo, vs[h], preferred_element_type=F32)
                    pv = d if pv is None else pv + d
                    alphas.append(alpha)
                    new += [m_new, l_new]
                acc = acc * jnp.where(lo, alphas[0], alphas[1]) + pv
                return (acc, *new)

            init = (jnp.zeros((blk, LANES), F32),
                    jnp.full((blk, 1), -jnp.inf, F32), jnp.zeros((blk, 1), F32),
                    jnp.full((blk, 1), -jnp.inf, F32), jnp.zeros((blk, 1), F32))
            carry = lax.fori_loop(0, iq, functools.partial(kv_block, diag=False), init)
            acc, m0, l0, m1, l1 = kv_block(iq, carry, diag=True)
            o_val = acc / jnp.where(lo, l0, l1)
            o_ref[pl.ds(q0, blk), :] = o_val.astype(BF)
            if has_bias:
                o32_ref[pl.ds(q0, blk), :] = o_val
            lse_ref[pl.ds(q0, blk), :] = jnp.where(lo, m0 + jnp.log(l0), m1 + jnp.log(l1))
            return 0

        lax.fori_loop(0, nq, q_block, 0)
        ride_end()

    in_specs = [pl.BlockSpec((seq, 2 * LANES), lambda b, p: (b, q_off + p)),
                pl.BlockSpec((seq, 2 * LANES), lambda b, p: (b, k_off + p)),
                pl.BlockSpec((seq, LANES), lambda b, p: (b, v_off + p))]
    args = [q_arr, k_arr, v_arr]
    if has_bias:
        in_specs.append(pl.BlockSpec((None, 2, 8, seq), lambda b, p: (b, p, 0, 0)))
        args.append(bias)
    out_blk = pl.BlockSpec((seq, LANES), lambda b, p: (b, p))
    out_shape = [jax.ShapeDtypeStruct((t, HEAD_PAIRS * LANES), BF), jax.ShapeDtypeStruct((t, HEAD_PAIRS * LANES), F32)]
    if has_bias:
        out_shape.append(jax.ShapeDtypeStruct((t, HEAD_PAIRS * LANES), F32))
    out_specs, scratch = [out_blk] * len(out_shape), []
    if ride is not None:
        r_in, r_shape, r_out, scratch = _ride_specs(ride)
        in_specs, out_shape, out_specs = in_specs + r_in, out_shape + r_shape, out_specs + r_out
        args += list(ride[1])
    return pl.pallas_call(
        body, name=name, grid=(nb, HEAD_PAIRS),
        out_shape=out_shape, in_specs=in_specs, out_specs=out_specs, scratch_shapes=scratch,
        compiler_params=_params(("arbitrary", "arbitrary")),
    )(*args)


def _attn_bwd(q_arr, q_off, k_arr, k_off, v_arr, v_off, bias, o, do, lse, seq, name, ride=None):
    t = q_arr.shape[0]
    nb = t // seq
    blk = min(ATTN_BLOCK, seq)
    nq = seq // blk
    has_bias = bias is not None
    n_in, n_out = (7, 4) if has_bias else (6, 3)

    def body(*refs):
        step = pl.program_id(0) * HEAD_PAIRS + pl.program_id(1)
        refs, ride_end = _ride_steps(ride, refs, n_in, n_out, step, nb * HEAD_PAIRS)
        if has_bias:
            (q_ref, k_ref, v_ref, bias_ref, o_ref, do_ref, lse_ref,
             dq_ref, dk_ref, dv_ref, dbias_ref, dq_acc, dsum) = refs
        else:
            (q_ref, k_ref, v_ref, o_ref, do_ref, lse_ref, dq_ref, dk_ref, dv_ref, dq_acc, dsum) = refs
        lo, hi = _head_masks()
        rows = lax.broadcasted_iota(jnp.int32, (blk, blk), 0)
        cols = lax.broadcasted_iota(jnp.int32, (blk, blk), 1)
        causal = cols <= rows
        dq_acc[...] = jnp.zeros_like(dq_acc)

        def prep(iq, _):
            q0 = pl.multiple_of(iq * blk, blk)
            prod = do_ref[pl.ds(q0, blk), :].astype(F32) * o_ref[pl.ds(q0, blk), :].astype(F32)
            d0 = jnp.sum(jnp.where(lo, prod, 0.0), axis=-1, keepdims=True)
            d1 = jnp.sum(jnp.where(hi, prod, 0.0), axis=-1, keepdims=True)
            dsum[pl.ds(q0, blk), :] = jnp.where(lo, d0, d1)
            return 0

        lax.fori_loop(0, nq, prep, 0)

        def kv_block(j, _):
            k0 = pl.multiple_of(j * blk, blk)
            vv = v_ref[pl.ds(k0, blk), :]
            vs = [jnp.where(lo, vv, jnp.zeros_like(vv)), jnp.where(hi, vv, jnp.zeros_like(vv))]
            ks = [k_ref[pl.ds(k0, blk), h * LANES:(h + 1) * LANES] for h in range(2)]

            def q_block(iq, carry, diag):
                q0 = pl.multiple_of(iq * blk, blk)
                dov = do_ref[pl.ds(q0, blk), :]
                lse_v = lse_ref[pl.ds(q0, blk), :]
                dsum_v = dsum[pl.ds(q0, blk), :]
                dv_acc = carry[0]
                out = []
                for h in range(2):
                    dk_acc, db_acc = carry[1 + 2 * h], carry[2 + 2 * h]
                    qq = q_ref[pl.ds(q0, blk), h * LANES:(h + 1) * LANES]
                    s = lax.dot_general(qq, ks[h], _DIMS["nt"], preferred_element_type=F32)
                    if has_bias:
                        s = s + bias_ref[h, 0:1, pl.ds(k0, blk)]
                    p = jnp.exp(s - lse_v[:, h * HEAD_DIM:h * HEAD_DIM + 1])
                    if diag:
                        p = jnp.where(causal, p, 0.0)
                    dp = lax.dot_general(dov, vs[h], _DIMS["nt"], preferred_element_type=F32)
                    ds = p * (dp - dsum_v[:, h * HEAD_DIM:h * HEAD_DIM + 1])
                    ds_bf = ds.astype(BF)
                    pt_do = lax.dot_general(p.astype(BF), dov, _DIMS["tn"], preferred_element_type=F32)
                    dv_acc = dv_acc + jnp.where(hi if h else lo, pt_do, 0.0)
                    dk_acc = dk_acc + lax.dot_general(ds_bf, qq, _DIMS["tn"], preferred_element_type=F32)
                    dq_acc[pl.ds(q0, blk), h * LANES:(h + 1) * LANES] += jnp.dot(
                        ds_bf, ks[h], preferred_element_type=F32)
                    if has_bias:
                        db_acc = db_acc + jnp.sum(ds, axis=0, keepdims=True)
                    out += [dk_acc, db_acc]
                return (dv_acc, *out)

            init = (jnp.zeros((blk, LANES), F32),
                    jnp.zeros((blk, LANES), F32), jnp.zeros((1, blk), F32),
                    jnp.zeros((blk, LANES), F32), jnp.zeros((1, blk), F32))
            carry = q_block(j, init, diag=True)
            carry = lax.fori_loop(j + 1, nq, functools.partial(q_block, diag=False), carry)
            dv_ref[pl.ds(k0, blk), :] = carry[0].astype(BF)
            for h in range(2):
                dk_ref[pl.ds(k0, blk), h * LANES:(h + 1) * LANES] = carry[1 + 2 * h].astype(BF)
                if has_bias:
                    dbias_ref[h, :, pl.ds(k0, blk)] = jnp.broadcast_to(carry[2 + 2 * h], (8, blk))
            return 0

        lax.fori_loop(0, nq, kv_block, 0)
        dq_ref[...] = dq_acc[...].astype(BF)
        ride_end()

    pair256 = lambda off: pl.BlockSpec((seq, 2 * LANES), lambda b, p: (b, off + p))
    pair128 = lambda off: pl.BlockSpec((seq, LANES), lambda b, p: (b, off + p))
    bias_spec = pl.BlockSpec((None, 2, 8, seq), lambda b, p: (b, p, 0, 0))
    in_specs = [pair256(q_off), pair256(k_off), pair128(v_off)]
    args = [q_arr, k_arr, v_arr]
    if has_bias:
        in_specs.append(bias_spec)
        args.append(bias)
    in_specs += [pair128(0), pair128(0), pair128(0)]
    args += [o, do, lse]
    out_shape = [jax.ShapeDtypeStruct((t, HEAD_PAIRS * 2 * LANES), BF),
                 jax.ShapeDtypeStruct((t, HEAD_PAIRS * 2 * LANES), BF),
                 jax.ShapeDtypeStruct((t, HEAD_PAIRS * LANES), BF)]
    out_specs = [pair256(0), pair256(0), pair128(0)]
    if has_bias:
        out_shape.append(jax.ShapeDtypeStruct((nb, HEADS, 8, seq), F32))
        out_specs.append(bias_spec)
    scratch = [pltpu.VMEM((seq, 2 * LANES), F32), pltpu.VMEM((seq, LANES), F32)]
    if ride is not None:
        r_in, r_shape, r_out, r_scratch = _ride_specs(ride)
        in_specs, out_shape, out_specs = in_specs + r_in, out_shape + r_shape, out_specs + r_out
        args += list(ride[1])
        scratch += r_scratch
    return pl.pallas_call(
        body, name=name, grid=(nb, HEAD_PAIRS),
        out_shape=out_shape, in_specs=in_specs, out_specs=out_specs, scratch_shapes=scratch,
        compiler_params=_params(("arbitrary", "arbitrary")),
    )(*args)


def _adamw(w, g, m, v, name):
    shape = w.shape
    last = shape[-1]
    rows = int(np.prod(shape[:-1])) if len(shape) > 1 else 1
    tr = _rows(rows, 512)

    def body(w_ref, g_ref, m_ref, v_ref, d_ref, nm_ref, nv_ref):
        d_ref[...], nm_ref[...], nv_ref[...] = _adamw_math(w_ref[...], g_ref[...], m_ref[...], v_ref[...])

    blk = pl.BlockSpec((tr, last), lambda i: (i, 0))
    sds = jax.ShapeDtypeStruct((rows, last), F32)
    outs = pl.pallas_call(
        body, name=name, grid=(rows // tr,),
        out_shape=(sds, sds, sds), in_specs=[blk] * 4, out_specs=(blk,) * 3,
        compiler_params=_params(("parallel",)),
    )(*[a.reshape(rows, last) for a in (w, g, m, v)])
    return tuple(a.reshape(shape) for a in outs)


LOW_COLS = 256


def _low_pad(a):
    return jnp.pad(a, ((0, 0),) * (a.ndim - 1) + ((0, LOW_COLS - a.shape[-1]),))


def _layer_shards(w, i):
    j = i // 2
    bf = lambda a: a.astype(BF)
    if i % 2 == 0:
        mixer = [bf(w["fox_w_in"][j]), bf(w["fox_w_out"][j])]
    else:
        mixer = [jnp.concatenate([bf(w["mla_w_dq"][j]), bf(w["mla_w_ukv"][j]), _low_pad(bf(w["mla_w_uq"][j])),
                                  _low_pad(bf(w["mla_w_dkv"][j]))], axis=0), bf(w["mla_w_out"][j])]
    return mixer + [bf(w["mlp_w1"][i]), bf(w["mlp_w2"][i])]


def _side_by_side(stack, r0, rows, cols=None):
    return jnp.concatenate([stack[dd, r0:r0 + rows, :cols] for dd in range(N_DEV)], axis=1)


def _stacked(stack, r0, rows, cols=None):
    part = stack[:, r0:r0 + rows, :cols]
    return part.reshape(N_DEV * rows, part.shape[2])


def _layer_mixer_weights(i, first, out_all):
    full = dict(w_out=_stacked(out_all, 0, 128))
    if i % 2 == 0:
        full["fox_w_in"] = _side_by_side(first, 0, 1024)
    else:
        full.update(mla_w_dq=_stacked(first, 0, 128), mla_w_ukv=_side_by_side(first, 128, 128),
                    mla_w_uq=_side_by_side(first, 256, 256, 192), mla_w_dkv=_stacked(first, 512, 128, 160))
    return full


def _by_dest_rows(g):
    return g.reshape(N_DEV, g.shape[0] // N_DEV, g.shape[1]).astype(BF)


def _by_dest_cols(g):
    n = g.shape[1] // N_DEV
    return jnp.stack([g[:, dd * n:(dd + 1) * n] for dd in range(N_DEV)]).astype(BF)


def _layer_mixer_grad_bufs(i, g):
    if i % 2 == 0:
        return [_by_dest_cols(g["fox_w_in"]), _by_dest_rows(g["w_out"])]
    low = jnp.concatenate([_by_dest_rows(g["mla_w_dq"]), _by_dest_cols(g["mla_w_ukv"]),
                           _low_pad(_by_dest_cols(g["mla_w_uq"])), _low_pad(_by_dest_rows(g["mla_w_dkv"]))], axis=1)
    return [low, _by_dest_rows(g["w_out"])]


def _low_shard_grads(low):
    return dict(mla_w_dq=low[:128], mla_w_ukv=low[128:256], mla_w_uq=low[256:512, :192], mla_w_dkv=low[512:, :160])


def _pad_heads(w, width):
    k = w.shape[0]
    return jnp.pad(w.reshape(k, HEADS, width), ((0, 0), (0, 0), (0, LANES - width))).reshape(k, HEADS * LANES)


def _unpad_heads(w, width):
    k = w.shape[0]
    return w.reshape(k, HEADS, LANES)[:, :, :width].reshape(k, HEADS * width)


def _rope_tables(positions, scale):
    inv_freq = 10000.0 ** (-jnp.arange(0, 2 * ROPE_HALF, 2, dtype=F32) / (2 * ROPE_HALF))
    ang = positions.astype(F32)[:, None] * inv_freq
    cos, sin = jnp.cos(ang) * scale, jnp.sin(ang) * scale
    t = positions.shape[0]
    z = lambda n: jnp.zeros((t, n), F32)
    cos_p = jnp.concatenate([jnp.full((t, HEAD_DIM), scale, F32), cos, cos, z(32)], axis=1)
    sin_a = jnp.concatenate([z(64), -sin, z(48)], axis=1)
    sin_b = jnp.concatenate([z(80), sin, z(32)], axis=1)
    fwd = (cos_p, sin_a, sin_b)
    bwd = (cos_p, jnp.roll(sin_b, -ROPE_HALF, axis=1), jnp.roll(sin_a, ROPE_HALF, axis=1))
    return fwd, bwd


def _key_rows(cum, nb, seq):
    v = -cum.reshape(nb, seq, LANES)[:, :, :HEADS]
    return jnp.broadcast_to(jnp.transpose(v, (0, 2, 1))[:, :, None, :], (nb, HEADS, 8, seq))


def kernel(x, c, positions, ada_w, ada_b, norm_mix_g, norm_mlp_g, fox_w_in, fox_b_f, fox_w_out, mla_w_dq, mla_q_norm_g, mla_w_uq, mla_w_dkv, mla_kv_norm_g, mla_w_ukv, mla_w_out, mlp_w1, mlp_w2, final_norm_g, loss_target, m_ada_w, m_ada_b, m_norm_mix_g, m_norm_mlp_g, m_fox_w_in, m_fox_b_f, m_fox_w_out, m_mla_w_dq, m_mla_q_norm_g, m_mla_w_uq, m_mla_w_dkv, m_mla_kv_norm_g, m_mla_w_ukv, m_mla_w_out, m_mlp_w1, m_mlp_w2, m_final_norm_g, v_ada_w, v_ada_b, v_norm_mix_g, v_norm_mlp_g, v_fox_w_in, v_fox_b_f, v_fox_w_out, v_mla_w_dq, v_mla_q_norm_g, v_mla_w_uq, v_mla_w_dkv, v_mla_kv_norm_g, v_mla_w_ukv, v_mla_w_out, v_mlp_w1, v_mlp_w2, v_final_norm_g):
    args = dict(locals())
    weights = {n: args[n] for n in WEIGHTS}
    nb, seq, d = x.shape
    t = nb * seq
    depth = ada_w.shape[0]
    dev = 4 * lax.axis_index("x") + 2 * lax.axis_index("y") + lax.axis_index("c")
    n_mod_local = ada_w.shape[2]

    n_qg = mla_q_norm_g.shape[1]
    cond = jnp.concatenate([c, jnp.pad(mla_q_norm_g.reshape(1, -1), ((0, 7), (0, d - 2 * n_qg)))], axis=0)
    w1_rows, w2_rows = mlp_w1.shape[1], mlp_w2.shape[1]
    shards = [_layer_shards(weights, i) for i in range(depth)]
    stacks = [None] * depth
    *stacks[0], cond_all = _all_gather(shards[0][:2] + [cond], "gather_first")
    c_all = cond_all[:, :nb].reshape(N_DEV * nb, d)
    q_gain = jnp.transpose(cond_all[:, nb, :2 * n_qg].reshape(N_DEV, 2, n_qg), (1, 0, 2)).reshape(2, N_DEV * n_qg)
    mod_local = jnp.stack([
        _matmul(c_all, ada_w[i], mode="nn", name="ada_mod", out_dtype=F32, a_act="silu", epi="bias",
                extras=(lax.dynamic_slice_in_dim(ada_b[i], dev * n_mod_local, n_mod_local)[None, :],))
        for i in range(depth)])
    mod_all, = _all_gather([mod_local.reshape(depth * N_DEV * nb, n_mod_local)], "gather_mod")
    mod_all = jnp.transpose(mod_all.reshape(N_DEV, depth, N_DEV * nb, n_mod_local), (1, 2, 0, 3))
    mod_all = mod_all.reshape(depth, N_DEV * nb, N_DEV * n_mod_local)
    mod = lax.dynamic_slice_in_dim(mod_all, dev * nb, nb, axis=1)
    mod = mod.reshape(depth, nb, 6, 1, d)

    pos = positions.reshape(t)
    rope_q, rope_q_t = _rope_tables(pos, MLA_SCALE)
    rope_k, rope_k_t = _rope_tables(pos, 1.0)

    def fox_weights(full):
        w_in = full["fox_w_in"]
        wq = _pad_heads(w_in[:, :d] * FOX_SCALE, HEAD_DIM)
        wk = _pad_heads(w_in[:, d:2 * d], HEAD_DIM)
        w_qkv = jnp.concatenate([wq, wk, w_in[:, 2 * d:3 * d]], axis=1)
        w_f = jnp.pad(w_in[:, 3 * d:], ((0, 0), (0, LANES - HEADS)))
        return w_qkv, w_f

    def mla_weights(full):
        w_dkv = full["mla_w_dkv"]
        w_down = jnp.concatenate([full["mla_w_dq"], w_dkv[:, :128], jnp.zeros((d, 64), BF),
                                  w_dkv[:, 128:160], jnp.zeros((d, 32), BF)], axis=1)
        w_uq = _pad_heads(full["mla_w_uq"], 96)
        w_ukv = full["mla_w_ukv"].reshape(128, HEADS, 2, HEAD_DIM)
        w_uk = jnp.pad(w_ukv[:, :, 0, :], ((0, 0), (0, 0), (0, 64))).reshape(128, HEADS * LANES)
        w_uv = w_ukv[:, :, 1, :].reshape(128, HEADS * HEAD_DIM)
        place = np.zeros((128, HEADS, LANES), np.float32)
        for i in range(2 * ROPE_HALF):
            place[64 + i, :, 64 + i] = 1.0
        bottom = jnp.concatenate([jnp.asarray(place.reshape(128, HEADS * LANES), BF),
                                  jnp.zeros((128, HEADS * HEAD_DIM), BF)], axis=1)
        w_kv = jnp.concatenate([jnp.concatenate([w_uk, w_uv], axis=1), bottom], axis=0)
        return w_down, w_uq, w_kv

    tm_big = min(2048, t)
    xs = x.reshape(t, d)
    saved = []
    for i in range(depth):
        j = i // 2
        sh_m, sc_m, g_m, sh_f, sc_f, g_f = (mod[i, :, q] for q in range(6))
        gain_mix = norm_mix_g[i][None, :]
        gain_mlp = norm_mlp_g[i][None, :]
        s = dict(x_in=xs)
        h = _norm_mod(xs, gain_mix, sc_m, sh_m, seq, "norm_mix")
        s["h"] = h
        full = _layer_mixer_weights(i, stacks[i][0], stacks[i][1])
        riders = (shards[0][2:] if i == 0 else []) + (shards[i + 1] if i + 1 < depth else [])
        ride = ("gather", riders) if riders else None
        if i % 2 == 0:
            w_qkv, w_f = fox_weights(full)
            qkv = _matmul(h, w_qkv, mode="nn", name="fox_qkv")
            fg = _matmul(h, w_f, mode="nn", name="fox_gate_logits", out_dtype=F32)
            b_f = jnp.pad(fox_b_f[j], (0, LANES - HEADS))[None, :]
            cum = _fox_gate(fg, b_f, seq, "fox_gate")
            bias = _key_rows(cum, nb, seq)
            o, lse, o32, *rode = _attn_fwd(qkv, 0, qkv, 8, qkv, 32, bias, seq, "fox_attn", ride)
            s.update(qkv=qkv, fg=fg, b_f=b_f, bias=bias, w_qkv=w_qkv, w_f=w_f, o32=o32)
        else:
            w_down, w_uq, w_kv = mla_weights(full)
            down = _matmul(h, w_down, mode="nn", name="mla_down", out_dtype=F32)
            gq, gkv = q_gain[j][None, :], mla_kv_norm_g[j][None, :]
            cq, ckr = _mla_mid(down, gq, gkv, rope_k, "mla_mid")
            q_raw = _matmul(cq, w_uq, mode="nn", name="mla_uq", out_dtype=F32)
            q_rot = _rope(q_raw, rope_q, "mla_rope_q")
            kv = _matmul(ckr, w_kv, mode="nn", name="mla_ukv")
            o, lse, *rode = _attn_fwd(q_rot, 0, kv, 0, kv, 16, None, seq, "mla_attn", ride)
            s.update(down=down, gq=gq, gkv=gkv, cq=cq, ckr=ckr, q_rot=q_rot, kv=kv,
                     w_down=w_down, w_uq=w_uq, w_kv=w_kv)
        if i == 0:
            stacks[0], rode = stacks[0] + rode[:2], rode[2:]
        if i + 1 < depth:
            stacks[i + 1] = rode
        w_out = full["w_out"]
        xs, y = _matmul(o, w_out, mode="nn", name="attn_out", epi="resid_gate", extras=(xs, g_m), seq=seq)
        s.update(o=o, lse=lse, y=y, w_out=w_out, x_mid=xs)
        h2 = _norm_mod(xs, gain_mlp, sc_f, sh_f, seq, "norm_mlp")
        a_pre = _matmul(h2, stacks[i][2], mode="nn", name="mlp_up", layer=("col", 0, w1_rows), tm=tm_big)
        xs, y2 = _matmul(a_pre, stacks[i][3], mode="nn", name="mlp_down", layer=("row", 0, w2_rows), a_act="relu2",
                         epi="resid_gate", extras=(xs, g_f), seq=seq)
        s.update(h2=h2, a_pre=a_pre, y2=y2)
        saved.append(s)

    loss_part, dx, dg_final = _loss_head(xs, final_norm_g[None, :], loss_target.reshape(t, d), "loss_head")

    w1_cols, w1_tm = mlp_w1.shape[2], _pick(w1_rows, 1024)
    dg_mix, dg_mlp, db_f, dg_kv, dg_q = [None] * depth, [None] * depth, [None] * 2, [None] * 2, [None] * 2
    dmod = [None] * depth
    rel = _relative_blocks()
    chains = dict(fox_w_in=None, fox_w_out=None, mla_w_out=None, mlp_w1=None, mlp_w2=None)
    low_grads = [None] * 2

    def adam_step(name, layer):
        def land_one(own, got):
            chains[name] = _total_adamw(own, got, weights[name], args["m_" + name], args["v_" + name], layer,
                                        chains[name], "adamw_" + name)
        return land_one

    def low_step(lj):
        def land_one(own, got):
            low_grads[lj] = _low_shard_grads(_add_parts(own, got, "grads_total"))
        return land_one

    def mixer_steps(li):
        lj = li // 2
        return ([adam_step("fox_w_in", lj), adam_step("fox_w_out", lj)] if li % 2 == 0
                else [low_step(lj), adam_step("mla_w_out", lj)])

    def mlp_steps(li):
        return [adam_step("mlp_w1", li), adam_step("mlp_w2", li)]

    def stage(bufs, steps):
        from_sibling = _swap_with_sibling(bufs, "grads_to_sibling")
        return [(step,) + tuple(_chip_partial(b, fs, rel, "grads_chip_sum"))
                for b, fs, step in zip(bufs, from_sibling, steps)]

    def land(staged, got):
        for (step, own, _), g in zip(staged, got):
            step(own, g)

    waiting = []
    dy2, dg_f = _gate_bwd(dx, saved[depth - 1]["y2"], mod[depth - 1, :, 5], seq, "gate_bwd")
    for i in reversed(range(depth)):
        j = i // 2
        s = saved[i]
        sh_m, sc_m, g_m, sh_f, sc_f, g_f = (mod[i, :, q] for q in range(6))
        da_pre = _matmul(dy2, stacks[i][3], mode="nt", name="mlp_down_dx", layer=("row", 0, w2_rows), epi="mul_drelu",
                         extras=(s["a_pre"],), tm=tm_big)
        g_w2 = _matmul(s["a_pre"], dy2, mode="tn", name="mlp_down_dw", a_act="relu2", tm=w2_rows,
                       into=(jax.ShapeDtypeStruct((N_DEV, w2_rows, d), BF), (None, w2_rows, d),
                             lambda r, j, k: (r, 0, 0)))
        dh2 = _matmul(da_pre, stacks[i][2], mode="nt", name="mlp_up_dx", layer=("col", 0, w1_rows), tm=tm_big)
        g_w1 = _matmul(s["h2"], da_pre, mode="tn", name="mlp_up_dw", tm=w1_tm, tn=w1_cols,
                       into=(jax.ShapeDtypeStruct((N_DEV, w1_rows, w1_cols), BF), (None, w1_tm, w1_cols),
                             lambda r, j, k: (j, r, 0)))
        if i == 0:
            waiting += stage([g_w1, g_w2], mlp_steps(0))
        dx, dg_mlp[i], dsc_f, dsh_f, dy, dg_m = _norm_mod_bwd(dh2, s["x_mid"], norm_mlp_g[i][None, :], sc_f, dx, seq,
                                                              "norm_bwd_gate", gate=(s["y"], g_m))
        do = _matmul(dy, s["w_out"], mode="nt", name="attn_out_dx")
        dw_out = _matmul(s["o"], dy, mode="tn", name="attn_out_dw", out_dtype=F32)
        ride = ("swap", [e[2] for e in waiting]) if waiting else None
        g_mixer = dict(w_out=dw_out)
        if i % 2 == 0:
            qkv = s["qkv"]
            dq, dk, dv, dbias, *rode = _attn_bwd(qkv, 0, qkv, 8, qkv, 32, s["bias"], s["o32"], do, s["lse"], seq,
                                                 "fox_attn_bwd", ride)
            dqkv = jnp.concatenate([dq, dk, dv], axis=1)
            d_cum = -jnp.transpose(dbias[:, :, 0, :], (0, 2, 1)).reshape(t, HEADS)
            d_cum = jnp.pad(d_cum, ((0, 0), (0, LANES - HEADS)))
            dfg, db = _fox_gate_bwd(d_cum, s["fg"], s["b_f"], seq, "fox_gate_bwd")
            db_f[j] = db
            dh = _matmul(dfg, s["w_f"], mode="nt", name="fox_gate_dx", out_dtype=F32)
            dh = _matmul(dqkv, s["w_qkv"], mode="nt", name="fox_qkv_dx", epi="add", extras=(dh,))
            dw_qkv = _matmul(s["h"], dqkv, mode="tn", name="fox_qkv_dw", out_dtype=F32)
            dw_f = _matmul(s["h"], dfg, mode="tn", name="fox_gate_dw", out_dtype=F32)
            g_mixer["fox_w_in"] = jnp.concatenate(
                [_unpad_heads(dw_qkv[:, :2048], HEAD_DIM) * FOX_SCALE, _unpad_heads(dw_qkv[:, 2048:4096], HEAD_DIM),
                 dw_qkv[:, 4096:], dw_f[:, :HEADS]], axis=1)
        else:
            kv = s["kv"]
            dq, dk, dv, *rode = _attn_bwd(s["q_rot"], 0, kv, 0, kv, 16, None, s["o"], do, s["lse"], seq,
                                          "mla_attn_bwd", ride)
            dq_raw = _rope(dq, rope_q_t, "mla_rope_q_bwd")
            dcq = _matmul(dq_raw, s["w_uq"], mode="nt", name="mla_uq_dx")
            dw_uq = _matmul(s["cq"], dq_raw, mode="tn", name="mla_uq_dw", out_dtype=F32)
            dkv = jnp.concatenate([dk, dv], axis=1)
            dckr = _matmul(dkv, s["w_kv"], mode="nt", name="mla_ukv_dx")
            dw_kv = _matmul(s["ckr"], dkv, mode="tn", name="mla_ukv_dw", out_dtype=F32)
            d_down, dgq, dgkv = _mla_mid_bwd(s["down"], dcq, dckr, s["gq"], s["gkv"], rope_k_t, "mla_mid_bwd")
            dg_q[j], dg_kv[j] = dgq, dgkv
            dh = _matmul(d_down, s["w_down"], mode="nt", name="mla_down_dx")
            dw_down = _matmul(s["h"], d_down, mode="tn", name="mla_down_dw", out_dtype=F32)
            g_mixer["mla_w_dq"] = dw_down[:, :256]
            g_mixer["mla_w_dkv"] = jnp.concatenate([dw_down[:, 256:384], dw_down[:, 448:480]], axis=1)
            g_mixer["mla_w_uq"] = _unpad_heads(dw_uq, 96)
            dk_nope = dw_kv[:128, :HEADS * LANES].reshape(128, HEADS, LANES)[:, :, :HEAD_DIM]
            dv_w = dw_kv[:128, HEADS * LANES:].reshape(128, HEADS, HEAD_DIM)
            g_mixer["mla_w_ukv"] = jnp.concatenate([dk_nope, dv_w], axis=2).reshape(128, HEADS * LANES)
        land(waiting, rode)
        this_dg_f = dg_f
        if i > 0:
            dx, dg_mix[i], dsc_m, dsh_m, dy2, dg_f = _norm_mod_bwd(
                dh, s["x_in"], norm_mix_g[i][None, :], sc_m, dx, seq, "norm_bwd_gate",
                gate=(saved[i - 1]["y2"], mod[i - 1, :, 5]))
            waiting = stage(_layer_mixer_grad_bufs(i, g_mixer) + [g_w1, g_w2], mixer_steps(i) + mlp_steps(i))
        else:
            dx, dg_mix[i], dsc_m, dsh_m = _norm_mod_bwd(dh, s["x_in"], norm_mix_g[i][None, :], sc_m, dx, seq, "norm_bwd")
            last = stage(_layer_mixer_grad_bufs(i, g_mixer), mixer_steps(i))
            land(last, _swap_with_chips([e[2] for e in last], "grads_to_chips"))
        dmod[i] = jnp.stack([dsh_m, dsc_m, dg_m, dsh_f, dsc_f, this_dg_f], axis=1).reshape(nb, 6 * d)

    grad_x = dx.reshape(nb, seq, d)
    done = {n: tuple(a.reshape(weights[n].shape) for a in chain) for n, chain in chains.items()}
    shard_grads = {n: jnp.stack([low_grads[0][n], low_grads[1][n]]) for n in low_grads[0]}

    dmod_arr = jnp.stack(dmod)
    wide = lambda a: jnp.pad(a, ((0, 0), (0, d - a.shape[1])))
    pieces = [wide(loss_part), *dg_mix, *dg_mlp, *[wide(a) for a in db_f], *[wide(a) for a in dg_kv], dg_final,
              *[wide(a) for a in dg_q], jnp.sum(dmod_arr, axis=1).reshape(depth * 6, d)]
    n_small = sum(p.shape[0] for p in pieces)
    both = jnp.concatenate(pieces + [dmod_arr.reshape(depth * nb * 6, d)], axis=0)
    both = jnp.pad(both, ((0, (-both.shape[0]) % 8), (0, 0)))
    both_all, = _all_gather([both], "gather_small")
    total = _sum_leading(both_all, "sum_small")
    off = 0

    def take(rows):
        nonlocal off
        out = total[off:off + rows]
        off += rows
        return out

    loss = take(1)[0, 0]
    g_small = dict(
        norm_mix_g=take(depth), norm_mlp_g=take(depth), fox_b_f=take(2)[:, :HEADS], mla_kv_norm_g=take(2)[:, :128],
        final_norm_g=take(1)[0],
        mla_q_norm_g=lax.dynamic_slice_in_dim(take(2)[:, :N_DEV * n_qg], dev * n_qg, n_qg, axis=1),
        ada_b=take(depth * 6).reshape(depth, 6 * d))
    dmod_all = both_all[:, n_small:n_small + depth * nb * 6]
    dmod_all = jnp.transpose(dmod_all.reshape(N_DEV, depth, nb, 6 * d), (1, 0, 2, 3)).reshape(depth, N_DEV * nb, 6 * d)
    dmod_cols = lax.dynamic_slice_in_dim(dmod_all, dev * n_mod_local, n_mod_local, axis=2)
    g_ada_w = jnp.stack([_matmul(c_all, dmod_cols[i], mode="tn", name="ada_dw", out_dtype=F32, a_act="silu")
                         for i in range(depth)])

    all_grads = dict(shard_grads)
    all_grads.update(g_small)
    all_grads["ada_w"] = g_ada_w

    deltas, new_m, new_v = {}, {}, {}
    for n in WEIGHTS:
        if n in done:
            all_grads[n], deltas[n], new_m[n], new_v[n] = done[n]
        else:
            deltas[n], new_m[n], new_v[n] = _adamw(weights[n], all_grads[n], args["m_" + n], args["v_" + n], "adamw")

    return (loss, grad_x, *[all_grads[n] for n in WEIGHTS], *[deltas[n] for n in WEIGHTS],
            *[new_m[n] for n in WEIGHTS], *[new_v[n] for n in WEIGHTS])
```

```python
import functools
import math

import jax
import jax.numpy as jnp
import numpy as np
from jax import lax
from jax.experimental import pallas as pl
from jax.experimental.pallas import tpu as pltpu

F32 = jnp.float32
BF = jnp.bfloat16

N_DEV = 8
HEADS = 16
HEAD_PAIRS = HEADS // 2
HEAD_DIM = 64
LANES = 128
ROPE_HALF = 16
NORM_EPS = 1e-6
MLA_SCALE = 96.0 ** -0.5
FOX_SCALE = 0.125
ATTN_BLOCK = 512
ROW_BLOCK = 512
K_SPAN = 2
VMEM_LIMIT = 56 * 1024 * 1024
MESH = pl.DeviceIdType.MESH

ADAM_LR = 0.001
ADAM_B1 = 0.9
ADAM_B2 = 0.999
ADAM_EPS = 1e-08
ADAM_WD = 0.01
ADAM_STEP = 10

WEIGHTS = ("ada_w", "ada_b", "norm_mix_g", "norm_mlp_g", "fox_w_in", "fox_b_f", "fox_w_out", "mla_w_dq",
           "mla_q_norm_g", "mla_w_uq", "mla_w_dkv", "mla_kv_norm_g", "mla_w_ukv", "mla_w_out", "mlp_w1",
           "mlp_w2", "final_norm_g")


def _params(sem=None):
    return pltpu.CompilerParams(dimension_semantics=sem, vmem_limit_bytes=VMEM_LIMIT)


def _pick(n, target):
    if n <= target:
        return n
    for t in range(target, 127, -128):
        if n % t == 0:
            return t
    return n


def _rows(n, target=512):
    if n <= target:
        return n
    for t in range(target, 7, -8):
        if n % t == 0:
            return t
    return n


def _place():
    x, y, c = lax.axis_index("x"), lax.axis_index("y"), lax.axis_index("c")
    return x, y, c


def _adamw_math(w, g, m, v):
    nm = ADAM_B1 * m + (1.0 - ADAM_B1) * g
    nv = ADAM_B2 * v + (1.0 - ADAM_B2) * (g * g)
    m_hat = nm * (1.0 / (1.0 - ADAM_B1 ** ADAM_STEP))
    v_hat = nv * (1.0 / (1.0 - ADAM_B2 ** ADAM_STEP))
    return -ADAM_LR * (m_hat / (jnp.sqrt(v_hat) + ADAM_EPS) + ADAM_WD * w), nm, nv


def _all_gather(blocks, name):
    ride = ("gather", blocks)

    def body(*refs):
        start, mid, finish = _ride_phases(ride, *_ride_split(ride, refs, 0, 0)[:3])
        start()
        mid()
        finish()

    in_specs, out_shape, out_specs, scratch = _ride_specs(ride)
    return pl.pallas_call(
        body, name=name, out_shape=out_shape, in_specs=in_specs, out_specs=out_specs, scratch_shapes=scratch,
    )(*blocks)


def _ride_specs(ride):
    kind, arrays = ride
    n = len(arrays)
    any_spec = pl.BlockSpec(memory_space=pl.ANY)
    if kind == "gather":
        out_shape = [jax.ShapeDtypeStruct((N_DEV,) + b.shape, b.dtype) for b in arrays]
        scratch = [pltpu.SemaphoreType.DMA((7 * n,)), pltpu.SemaphoreType.DMA((7 * n,)), pltpu.SemaphoreType.DMA((n,))]
    else:
        out_shape = [jax.ShapeDtypeStruct(p.shape, p.dtype) for p in arrays]
        scratch = [pltpu.SemaphoreType.DMA((3 * n,)), pltpu.SemaphoreType.DMA((3 * n,))]
    return [any_spec] * n, out_shape, [any_spec] * n, scratch


def _ride_split(ride, refs, n_in, n_out):
    n = len(ride[1])
    n_sem = 3 if ride[0] == "gather" else 2
    src = refs[n_in:n_in + n]
    dst = refs[n_in + n + n_out:n_in + 2 * n + n_out]
    own = refs[:n_in] + refs[n_in + n:n_in + n + n_out] + refs[n_in + 2 * n + n_out:len(refs) - n_sem]
    return src, dst, refs[len(refs) - n_sem:], own


def _ride_phases(ride, src, dst, sems):
    n = len(src)
    x, y, c = _place()
    chips = [(1 - x, y), (x, 1 - y), (1 - x, 1 - y)]
    if ride[0] == "swap":
        send_sems, recv_sems = sems

        def copies():
            return [pltpu.make_async_remote_copy(
                src_ref=src[a].at[k], dst_ref=dst[a].at[k], send_sem=send_sems.at[3 * a + k],
                recv_sem=recv_sems.at[3 * a + k], device_id=(*chip, c), device_id_type=MESH)
                for k, chip in enumerate(chips) for a in range(n)]

        def start():
            for cp in copies():
                cp.start()

        def finish():
            for cp in copies():
                cp.wait()

        return start, lambda: None, finish

    send_sems, recv_sems, local_sems = sems
    me, sibling = (x, y, c), (x, y, 1 - c)

    def slot(a, px, py, pc):
        return dst[a].at[4 * px + 2 * py + pc]

    def copy(a, k, blk, to, from_src=False):
        return pltpu.make_async_remote_copy(
            src_ref=src[a] if from_src else slot(a, *blk), dst_ref=slot(a, *blk),
            send_sem=send_sems.at[7 * a + k], recv_sem=recv_sems.at[7 * a + k], device_id=to, device_id_type=MESH)

    def mine():
        return [pltpu.make_async_copy(src[a], slot(a, *me), local_sems.at[a]) for a in range(n)]

    def first():
        out = []
        for j, chip in enumerate(chips):
            out += [copy(a, 1 + j, me, (*chip, c), from_src=True) for a in range(n)]
        return out + [copy(a, 0, me, sibling, from_src=True) for a in range(n)]

    def passed():
        return [copy(a, 4 + j, (*chip, c), sibling) for j, chip in enumerate(chips) for a in range(n)]

    def start():
        for cp in mine() + first():
            cp.start()

    def mid():
        for j, chip in enumerate(chips):
            for a in range(n):
                copy(a, 1 + j, (*chip, c), me).wait_recv()
        for cp in passed():
            cp.start()

    def finish():
        for a in range(n):
            copy(a, 0, sibling, me).wait_recv()
        for j, chip in enumerate(chips):
            for a in range(n):
                copy(a, 4 + j, (*chip, 1 - c), me).wait_recv()
        for cp in first() + passed():
            cp.wait_send()
        for cp in mine():
            cp.wait()

    return start, mid, finish


def _ride_steps(ride, refs, n_in, n_out, step, n_steps):
    if ride is None:
        return refs, lambda: None
    src, dst, sems, own = _ride_split(ride, refs, n_in, n_out)
    start, mid, finish = _ride_phases(ride, src, dst, sems)
    pl.when(step == 0)(start)
    pl.when(step == (3 * n_steps) // 4)(mid)
    return own, lambda: pl.when(step == n_steps - 1)(finish)


def _swap_with_sibling(bufs, name):
    n = len(bufs)

    def body(*refs):
        src, dst = refs[:n], refs[n:2 * n]
        send_sems, recv_sems = refs[2 * n:]
        x, y, c = _place()
        cps = []
        for a in range(n):
            for k in range(4):
                cps.append(pltpu.make_async_remote_copy(
                    src_ref=src[a].at[2 * k + (1 - c)], dst_ref=dst[a].at[k],
                    send_sem=send_sems.at[4 * a + k], recv_sem=recv_sems.at[4 * a + k],
                    device_id=(x, y, 1 - c), device_id_type=MESH))
        for cp in cps:
            cp.start()
        for cp in cps:
            cp.wait()

    any_spec = pl.BlockSpec(memory_space=pl.ANY)
    return pl.pallas_call(
        body, name=name,
        out_shape=[jax.ShapeDtypeStruct((4,) + b.shape[1:], b.dtype) for b in bufs],
        in_specs=[any_spec] * n, out_specs=[any_spec] * n,
        scratch_shapes=[pltpu.SemaphoreType.DMA((4 * n,)), pltpu.SemaphoreType.DMA((4 * n,))],
    )(*bufs)


def _swap_with_chips(parts, name):
    ride = ("swap", parts)

    def body(*refs):
        start, _, finish = _ride_phases(ride, *_ride_split(ride, refs, 0, 0)[:3])
        start()
        finish()

    in_specs, out_shape, out_specs, scratch = _ride_specs(ride)
    return pl.pallas_call(
        body, name=name, out_shape=out_shape, in_specs=in_specs, out_specs=out_specs, scratch_shapes=scratch,
    )(*parts)


def _relative_blocks():
    x, y, c = _place()
    flips = ((0, 0), (1, 0), (0, 1), (1, 1))
    mine = [4 * (x ^ fx) + 2 * (y ^ fy) + c for fx, fy in flips]
    sib = [2 * (x ^ fx) + (y ^ fy) for fx, fy in flips]
    return jnp.stack(mine + sib).astype(jnp.int32)


def _chip_partial(buf, from_sibling, rel, name):
    _, r, cdim = buf.shape
    tr = _rows(r, 256)

    def body(rel_ref, m0, m1, m2, m3, s0, s1, s2, s3, own_ref, parts_ref):
        del rel_ref
        own_ref[...] = m0[...].astype(F32) + s0[...].astype(F32)
        for k, (m, s) in enumerate(((m1, s1), (m2, s2), (m3, s3))):
            parts_ref[k] = (m[...].astype(F32) + s[...].astype(F32)).astype(parts_ref.dtype)

    def pick(k):
        return pl.BlockSpec((None, tr, cdim), lambda i, rel_ref: (rel_ref[k], i, 0))

    return pl.pallas_call(
        body, name=name,
        grid_spec=pltpu.PrefetchScalarGridSpec(
            num_scalar_prefetch=1, grid=(r // tr,),
            in_specs=[pick(k) for k in range(8)],
            out_specs=(pl.BlockSpec((tr, cdim), lambda i, rel_ref: (i, 0)),
                       pl.BlockSpec((3, tr, cdim), lambda i, rel_ref: (0, i, 0)))),
        out_shape=(jax.ShapeDtypeStruct((r, cdim), F32), jax.ShapeDtypeStruct((3, r, cdim), buf.dtype)),
        compiler_params=_params(("parallel",)),
    )(rel, buf, buf, buf, buf, from_sibling, from_sibling, from_sibling, from_sibling)


def _total_adamw(own, parts, w, m, v, layer, carry, name):
    r, cdim = own.shape
    n_layers = w.shape[0]
    tr = _rows(r, 256)
    steps = r // tr

    def body(own_ref, parts_ref, w_ref, m_ref, v_ref, *rest):
        g_ref, d_ref, nm_ref, nv_ref = rest[-4:]
        g = own_ref[...]
        for k in range(3):
            g = g + parts_ref[k].astype(F32)
        g_ref[...] = g
        d_ref[...], nm_ref[...], nv_ref[...] = _adamw_math(w_ref[...], g, m_ref[...], v_ref[...])

    blk = pl.BlockSpec((tr, cdim), lambda i: (i, 0))
    lay = pl.BlockSpec((tr, cdim), lambda i: (layer * steps + i, 0))
    in_specs = [blk, pl.BlockSpec((3, tr, cdim), lambda i: (0, i, 0)), lay, lay, lay]
    operands = [own, parts, *[a.reshape(n_layers * r, cdim) for a in (w, m, v)]]
    aliases = {}
    if carry is not None:
        in_specs += [pl.BlockSpec(memory_space=pl.ANY)] * 4
        operands += list(carry)
        aliases = {5 + k: k for k in range(4)}
    sds = jax.ShapeDtypeStruct((n_layers * r, cdim), F32)
    return pl.pallas_call(
        body, name=name, grid=(steps,),
        out_shape=(sds,) * 4, in_specs=in_specs, out_specs=(lay,) * 4, input_output_aliases=aliases,
        compiler_params=_params(("parallel",)),
    )(*operands)


def _add_parts(own, parts, name):
    r, cdim = own.shape
    tr = _rows(r, 512)

    def body(own_ref, parts_ref, out_ref):
        acc = own_ref[...]
        for k in range(parts_ref.shape[0]):
            acc = acc + parts_ref[k].astype(F32)
        out_ref[...] = acc

    return pl.pallas_call(
        body, name=name, grid=(r // tr,),
        out_shape=jax.ShapeDtypeStruct((r, cdim), F32),
        in_specs=[pl.BlockSpec((tr, cdim), lambda i: (i, 0)),
                  pl.BlockSpec((parts.shape[0], tr, cdim), lambda i: (0, i, 0))],
        out_specs=pl.BlockSpec((tr, cdim), lambda i: (i, 0)),
        compiler_params=_params(("parallel",)),
    )(own, parts)


def _sum_leading(stack, name):
    n, r, cdim = stack.shape
    tr = _rows(r, 512)

    def body(in_ref, out_ref):
        acc = in_ref[0]
        for k in range(1, n):
            acc = acc + in_ref[k]
        out_ref[...] = acc

    return pl.pallas_call(
        body, name=name, grid=(r // tr,),
        out_shape=jax.ShapeDtypeStruct((r, cdim), F32),
        in_specs=[pl.BlockSpec((n, tr, cdim), lambda i: (0, i, 0))],
        out_specs=pl.BlockSpec((tr, cdim), lambda i: (i, 0)),
        compiler_params=_params(("parallel",)),
    )(stack)


_DIMS = {"nn": (((1,), (0,)), ((), ())), "nt": (((1,), (1,)), ((), ())), "tn": (((0,), (0,)), ((), ()))}


def _stack_spec(shape, mode, layer):
    cut, l, rows = layer
    cols = shape[2]
    by_n = pl.BlockSpec((1, rows, cols), lambda i, j, k: (j, l, 0))
    by_k = pl.BlockSpec((K_SPAN, rows, cols), lambda i, j, k: (k, l, 0))
    if cut == "col":
        return (by_n, N_DEV * cols, cols, rows) if mode == "nn" else (by_k, rows, rows, K_SPAN * cols)
    return (by_k, cols, cols, K_SPAN * rows) if mode == "nn" else (by_n, N_DEV * rows, rows, cols)


def _matmul(a, b, *, mode, name, out_dtype=BF, a_act=None, epi=None, extras=(), seq=None, layer=None, tm=None,
            tn=None, into=None):
    if mode == "tn":
        kdim, m = a.shape
    else:
        m, kdim = a.shape
    if tm is None:
        tm = _pick(m, 1024 if epi != "resid_gate" else min(1024, seq))
    tk = _pick(kdim, 2048 if mode == "tn" else 1024)
    b_spec = None
    if layer is not None:
        b_spec, n, tn, tk = _stack_spec(b.shape, mode, layer)
    else:
        n = b.shape[0] if mode == "nt" else b.shape[1]
        tn = _pick(n, 1024) if tn is None else tn
    nk = kdim // tk
    a_spec = (pl.BlockSpec((tk, tm), lambda i, j, k: (k, i)) if mode == "tn"
              else pl.BlockSpec((tm, tk), lambda i, j, k: (i, k)))
    if b_spec is None:
        b_spec = (pl.BlockSpec((tn, tk), lambda i, j, k: (j, k)) if mode == "nt"
                  else pl.BlockSpec((tk, tn), lambda i, j, k: (k, j)))
    tile = pl.BlockSpec((tm, tn), lambda i, j, k: (i, j))
    in_specs, out_specs = [a_spec, b_spec], [tile]
    out_shape = [jax.ShapeDtypeStruct((m, n), out_dtype)]
    if epi == "resid_gate":
        in_specs += [tile, pl.BlockSpec((None, 1, tn), lambda i, j, k: ((i * tm) // seq, 0, j))]
        out_shape = [jax.ShapeDtypeStruct((m, n), F32), jax.ShapeDtypeStruct((m, n), BF)]
        out_specs = [tile, tile]
    elif epi in ("mul_drelu", "add"):
        in_specs += [tile]
    elif epi == "bias":
        in_specs += [pl.BlockSpec((1, tn), lambda i, j, k: (0, j))]
    n_extra, n_out = len(in_specs) - 2, len(out_specs)
    aliases, n_kept = {}, 0
    if into is not None:
        buffer, block, index_map = into
        out_dtype = buffer.dtype
        if not isinstance(buffer, jax.ShapeDtypeStruct):
            in_specs.append(pl.BlockSpec(memory_space=pl.ANY))
            extras = tuple(extras) + (buffer,)
            aliases, n_kept = {len(in_specs) - 1: 0}, 1
        out_shape = [jax.ShapeDtypeStruct(buffer.shape, buffer.dtype)]
        out_specs = [pl.BlockSpec(block, index_map)]
    dims = _DIMS[mode]

    def body(*refs):
        a_ref, b_ref = refs[:2]
        ex = refs[2:2 + n_extra]
        outs = refs[2 + n_extra + n_kept:2 + n_extra + n_kept + n_out]
        av = a_ref[...]
        if a_act == "relu2":
            t = jnp.maximum(av.astype(F32), 0.0)
            av = t * t
        elif a_act == "silu":
            t = av.astype(F32)
            av = t / (1.0 + jnp.exp(-t))
        av = av.astype(BF)
        if layer is None:
            part = lax.dot_general(av, b_ref[...].astype(BF), dims, preferred_element_type=F32)
        else:
            span = b_ref.shape[0]
            wk = av.shape[1] // span
            part = None
            for u in range(span):
                p_u = lax.dot_general(av[:, u * wk:(u + 1) * wk], b_ref[u], dims, preferred_element_type=F32)
                part = p_u if part is None else part + p_u

        def finish(acc):
            if epi == "resid_gate":
                outs[0][...] = ex[0][...] + ex[1][...] * acc
                outs[1][...] = acc.astype(BF)
            elif epi == "mul_drelu":
                outs[0][...] = (acc * (2.0 * jnp.maximum(ex[0][...].astype(F32), 0.0))).astype(out_dtype)
            elif epi == "add":
                outs[0][...] = (acc + ex[0][...].astype(F32)).astype(out_dtype)
            elif epi == "bias":
                outs[0][...] = (acc + ex[0][...]).astype(out_dtype)
            else:
                outs[0][...] = acc.astype(out_dtype)

        if nk == 1:
            finish(part)
        else:
            acc_ref = refs[-1]
            k = pl.program_id(2)

            @pl.when(k == 0)
            def _():
                acc_ref[...] = part

            @pl.when(k > 0)
            def _():
                acc_ref[...] += part

            @pl.when(k == nk - 1)
            def _():
                finish(acc_ref[...])

    res = pl.pallas_call(
        body, name=name, grid=(m // tm, n // tn, nk),
        out_shape=out_shape, in_specs=in_specs, out_specs=out_specs,
        scratch_shapes=[pltpu.VMEM((tm, tn), F32)] if nk > 1 else [],
        input_output_aliases=aliases,
        compiler_params=_params(("parallel", "parallel", "arbitrary")),
    )(a, b, *extras)
    return res if n_out > 1 else res[0]


def _norm_mod(x, gain, scale, shift, seq, name):
    t, w = x.shape
    tr = ROW_BLOCK

    def body(x_ref, g_ref, sc_ref, sh_ref, out_ref):
        xv = x_ref[...]
        rstd = lax.rsqrt(jnp.mean(xv * xv, axis=-1, keepdims=True) + NORM_EPS)
        y = xv * rstd * g_ref[...]
        out_ref[...] = (y * (1.0 + sc_ref[...]) + sh_ref[...]).astype(BF)

    per_b = pl.BlockSpec((None, 1, w), lambda i: ((i * tr) // seq, 0, 0))
    return pl.pallas_call(
        body, name=name, grid=(t // tr,),
        out_shape=jax.ShapeDtypeStruct((t, w), BF),
        in_specs=[pl.BlockSpec((tr, w), lambda i: (i, 0)), pl.BlockSpec((1, w), lambda i: (0, 0)), per_b, per_b],
        out_specs=pl.BlockSpec((tr, w), lambda i: (i, 0)),
        compiler_params=_params(("parallel",)),
    )(x, gain, scale, shift)


def _norm_mod_bwd(dh, x, gain, scale, dres, seq, name, gate=None):
    t, w = x.shape
    tr = ROW_BLOCK
    steps_per_seq = seq // tr
    nb = t // seq
    gated = gate is not None

    def body(*refs):
        dh_ref, x_ref, g_ref, sc_ref, dres_ref = refs[:5]
        dx_ref, dg_ref, dsc_ref, dsh_ref = refs[-6:-2] if gated else refs[-4:]
        i = pl.program_id(0)
        xv = x_ref[...]
        dhv = dh_ref[...].astype(F32)
        rstd = lax.rsqrt(jnp.mean(xv * xv, axis=-1, keepdims=True) + NORM_EPS)
        xhat = xv * rstd
        one_sc = 1.0 + sc_ref[...]
        g = g_ref[...]
        dxhat = dhv * (g * one_sc)
        proj = jnp.mean(dxhat * xhat, axis=-1, keepdims=True)
        dxv = dres_ref[...] + rstd * (dxhat - xhat * proj)
        dx_ref[...] = dxv
        dhx = dhv * xhat
        first = [(dg_ref, jnp.sum(dhx * one_sc, axis=0, keepdims=True))]
        per_seq = [(dsc_ref, jnp.sum(dhx * g, axis=0, keepdims=True)), (dsh_ref, jnp.sum(dhv, axis=0, keepdims=True))]
        if gated:
            y_ref, gate_ref, dy_ref, dgate_ref = refs[5], refs[6], refs[-2], refs[-1]
            dy_ref[...] = (dxv * gate_ref[...]).astype(BF)
            per_seq.append((dgate_ref, jnp.sum(dxv * y_ref[...].astype(F32), axis=0, keepdims=True)))
        for cond_new, cond_add, group in ((i == 0, i > 0, first),
                                          (i % steps_per_seq == 0, i % steps_per_seq != 0, per_seq)):
            @pl.when(cond_new)
            def _(group=group):
                for ref, part in group:
                    ref[...] = part

            @pl.when(cond_add)
            def _(group=group):
                for ref, part in group:
                    ref[...] += part

    row = pl.BlockSpec((tr, w), lambda i: (i, 0))
    per_b = pl.BlockSpec((None, 1, w), lambda i: ((i * tr) // seq, 0, 0))
    vec = pl.BlockSpec((1, w), lambda i: (0, 0))
    out_shape = [jax.ShapeDtypeStruct((t, w), F32), jax.ShapeDtypeStruct((1, w), F32),
                 jax.ShapeDtypeStruct((nb, 1, w), F32), jax.ShapeDtypeStruct((nb, 1, w), F32)]
    in_specs, out_specs, operands = [row, row, vec, per_b, row], [row, vec, per_b, per_b], [dh, x, gain, scale, dres]
    if gated:
        in_specs += [row, per_b]
        operands += list(gate)
        out_shape += [jax.ShapeDtypeStruct((t, w), BF), jax.ShapeDtypeStruct((nb, 1, w), F32)]
        out_specs += [row, per_b]
    return pl.pallas_call(
        body, name=name, grid=(t // tr,),
        out_shape=out_shape, in_specs=in_specs, out_specs=out_specs,
        compiler_params=_params(("arbitrary",)),
    )(*operands)


def _gate_bwd(dx, y, gate, seq, name):
    t, w = dx.shape
    tr = ROW_BLOCK
    steps_per_seq = seq // tr
    nb = t // seq

    def body(dx_ref, y_ref, g_ref, dy_ref, dg_ref):
        i = pl.program_id(0)
        dxv = dx_ref[...]
        dy_ref[...] = (dxv * g_ref[...]).astype(BF)
        part = jnp.sum(dxv * y_ref[...].astype(F32), axis=0, keepdims=True)

        @pl.when(i % steps_per_seq == 0)
        def _():
            dg_ref[...] = part

        @pl.when(i % steps_per_seq != 0)
        def _():
            dg_ref[...] += part

    row = pl.BlockSpec((tr, w), lambda i: (i, 0))
    per_b = pl.BlockSpec((None, 1, w), lambda i: ((i * tr) // seq, 0, 0))
    return pl.pallas_call(
        body, name=name, grid=(t // tr,),
        out_shape=(jax.ShapeDtypeStruct((t, w), BF), jax.ShapeDtypeStruct((nb, 1, w), F32)),
        in_specs=[row, row, per_b], out_specs=(row, per_b),
        compiler_params=_params(("arbitrary",)),
    )(dx, y, gate)


def _loss_head(x, gain, target, name):
    t, w = x.shape
    tr = ROW_BLOCK

    def body(x_ref, g_ref, t_ref, loss_ref, dx_ref, dg_ref):
        i = pl.program_id(0)
        xv = x_ref[...]
        g = g_ref[...]
        rstd = lax.rsqrt(jnp.mean(xv * xv, axis=-1, keepdims=True) + NORM_EPS)
        xhat = xv * rstd
        err = xhat * g - t_ref[...]
        row_loss = jnp.sum(err * err, axis=-1, keepdims=True) * (0.5 / w)
        loss_part = jnp.broadcast_to(jnp.sum(row_loss, axis=0, keepdims=True), (1, LANES))
        dy = err * (1.0 / w)
        dg_part = jnp.sum(dy * xhat, axis=0, keepdims=True)
        dxhat = dy * g
        proj = jnp.mean(dxhat * xhat, axis=-1, keepdims=True)
        dx_ref[...] = rstd * (dxhat - xhat * proj)

        @pl.when(i == 0)
        def _():
            loss_ref[...] = loss_part
            dg_ref[...] = dg_part

        @pl.when(i > 0)
        def _():
            loss_ref[...] += loss_part
            dg_ref[...] += dg_part

    row = pl.BlockSpec((tr, w), lambda i: (i, 0))
    vec = pl.BlockSpec((1, w), lambda i: (0, 0))
    return pl.pallas_call(
        body, name=name, grid=(t // tr,),
        out_shape=(jax.ShapeDtypeStruct((1, LANES), F32), jax.ShapeDtypeStruct((t, w), F32),
                   jax.ShapeDtypeStruct((1, w), F32)),
        in_specs=[row, vec, row],
        out_specs=(pl.BlockSpec((1, LANES), lambda i: (0, 0)), row, vec),
        compiler_params=_params(("arbitrary",)),
    )(x, gain, target)


def _rope_group(xg, cos_p, sin_a, sin_b):
    return (xg * cos_p + pltpu.roll(xg, LANES - ROPE_HALF, axis=1) * sin_a
            + pltpu.roll(xg, ROPE_HALF, axis=1) * sin_b)


def _rope(x, tables, name, out_dtype=BF):
    t, w = x.shape
    tr = ROW_BLOCK
    groups = w // LANES

    def body(x_ref, c_ref, a_ref, b_ref, out_ref):
        cos_p, sin_a, sin_b = c_ref[...], a_ref[...], b_ref[...]
        for g in range(groups):
            sl = slice(g * LANES, (g + 1) * LANES)
            out_ref[:, sl] = _rope_group(x_ref[:, sl].astype(F32), cos_p, sin_a, sin_b).astype(out_dtype)

    row = pl.BlockSpec((tr, w), lambda i: (i, 0))
    tab = pl.BlockSpec((tr, LANES), lambda i: (i, 0))
    return pl.pallas_call(
        body, name=name, grid=(t // tr,),
        out_shape=jax.ShapeDtypeStruct((t, w), out_dtype),
        in_specs=[row, tab, tab, tab], out_specs=row,
        compiler_params=_params(("parallel",)),
    )(x, *tables)


def _mla_mid(down, gq, gkv, tables, name):
    t = down.shape[0]
    tr = ROW_BLOCK

    def body(d_ref, gq_ref, gkv_ref, c_ref, a_ref, b_ref, cq_ref, ckr_ref):
        q = d_ref[:, 0:256]
        cq_ref[...] = (q * lax.rsqrt(jnp.mean(q * q, axis=-1, keepdims=True) + NORM_EPS) * gq_ref[...]).astype(BF)
        kv = d_ref[:, 256:384]
        ckr_ref[:, 0:128] = (kv * lax.rsqrt(jnp.mean(kv * kv, axis=-1, keepdims=True) + NORM_EPS)
                             * gkv_ref[...]).astype(BF)
        ckr_ref[:, 128:256] = _rope_group(d_ref[:, 384:512], c_ref[...], a_ref[...], b_ref[...]).astype(BF)

    tab = pl.BlockSpec((tr, LANES), lambda i: (i, 0))
    return pl.pallas_call(
        body, name=name, grid=(t // tr,),
        out_shape=(jax.ShapeDtypeStruct((t, 256), BF), jax.ShapeDtypeStruct((t, 256), BF)),
        in_specs=[pl.BlockSpec((tr, 512), lambda i: (i, 0)), pl.BlockSpec((1, 256), lambda i: (0, 0)),
                  pl.BlockSpec((1, 128), lambda i: (0, 0)), tab, tab, tab],
        out_specs=(pl.BlockSpec((tr, 256), lambda i: (i, 0)), pl.BlockSpec((tr, 256), lambda i: (i, 0))),
        compiler_params=_params(("parallel",)),
    )(down, gq, gkv, *tables)


def _mla_mid_bwd(down, dcq, dckr, gq, gkv, tables_t, name):
    t = down.shape[0]
    tr = ROW_BLOCK

    def norm_bwd(xv, g, dy):
        rstd = lax.rsqrt(jnp.mean(xv * xv, axis=-1, keepdims=True) + NORM_EPS)
        xhat = xv * rstd
        dxhat = dy * g
        proj = jnp.mean(dxhat * xhat, axis=-1, keepdims=True)
        return rstd * (dxhat - xhat * proj), jnp.sum(dy * xhat, axis=0, keepdims=True)

    def body(d_ref, dcq_ref, dckr_ref, gq_ref, gkv_ref, c_ref, a_ref, b_ref, dd_ref, dgq_ref, dgkv_ref):
        i = pl.program_id(0)
        dq, dgq_part = norm_bwd(d_ref[:, 0:256], gq_ref[...], dcq_ref[...].astype(F32))
        dd_ref[:, 0:256] = dq.astype(BF)
        dkv, dgkv_part = norm_bwd(d_ref[:, 256:384], gkv_ref[...], dckr_ref[:, 0:128].astype(F32))
        dd_ref[:, 256:384] = dkv.astype(BF)
        dd_ref[:, 384:512] = _rope_group(dckr_ref[:, 128:256].astype(F32), c_ref[...], a_ref[...],
                                         b_ref[...]).astype(BF)

        @pl.when(i == 0)
        def _():
            dgq_ref[...] = dgq_part
            dgkv_ref[...] = dgkv_part

        @pl.when(i > 0)
        def _():
            dgq_ref[...] += dgq_part
            dgkv_ref[...] += dgkv_part

    tab = pl.BlockSpec((tr, LANES), lambda i: (i, 0))
    r256 = pl.BlockSpec((tr, 256), lambda i: (i, 0))
    return pl.pallas_call(
        body, name=name, grid=(t // tr,),
        out_shape=(jax.ShapeDtypeStruct((t, 512), BF), jax.ShapeDtypeStruct((1, 256), F32),
                   jax.ShapeDtypeStruct((1, 128), F32)),
        in_specs=[pl.BlockSpec((tr, 512), lambda i: (i, 0)), r256, r256, pl.BlockSpec((1, 256), lambda i: (0, 0)),
                  pl.BlockSpec((1, 128), lambda i: (0, 0)), tab, tab, tab],
        out_specs=(pl.BlockSpec((tr, 512), lambda i: (i, 0)), pl.BlockSpec((1, 256), lambda i: (0, 0)),
                   pl.BlockSpec((1, 128), lambda i: (0, 0))),
        compiler_params=_params(("arbitrary",)),
    )(down, dcq, dckr, gq, gkv, *tables_t)


def _scan_rows(x, reverse):
    s = x.shape[0]
    row = lax.broadcasted_iota(jnp.int32, x.shape, 0)
    step = 1
    while step < s:
        if reverse:
            x = x + jnp.where(row < s - step, pltpu.roll(x, s - step, axis=0), 0.0)
        else:
            x = x + jnp.where(row >= step, pltpu.roll(x, step, axis=0), 0.0)
        step *= 2
    return x


def _fox_gate(fg, b_f, seq, name):
    t = fg.shape[0]

    def body(fg_ref, b_ref, out_ref):
        z = fg_ref[...] + b_ref[...]
        log_f = jnp.minimum(z, 0.0) - jnp.log(1.0 + jnp.exp(-jnp.abs(z)))
        out_ref[...] = _scan_rows(log_f, reverse=False)

    blk = pl.BlockSpec((seq, LANES), lambda b: (b, 0))
    return pl.pallas_call(
        body, name=name, grid=(t // seq,),
        out_shape=jax.ShapeDtypeStruct((t, LANES), F32),
        in_specs=[blk, pl.BlockSpec((1, LANES), lambda b: (0, 0))], out_specs=blk,
        compiler_params=_params(("parallel",)),
    )(fg, b_f)


def _fox_gate_bwd(d_cum, fg, b_f, seq, name):
    t = fg.shape[0]

    def body(dc_ref, fg_ref, b_ref, dfg_ref, db_ref):
        b = pl.program_id(0)
        z = fg_ref[...] + b_ref[...]
        d_log_f = _scan_rows(dc_ref[...], reverse=True)
        dz = d_log_f / (1.0 + jnp.exp(z))
        dfg_ref[...] = dz
        part = jnp.sum(dz, axis=0, keepdims=True)

        @pl.when(b == 0)
        def _():
            db_ref[...] = part

        @pl.when(b > 0)
        def _():
            db_ref[...] += part

    blk = pl.BlockSpec((seq, LANES), lambda b: (b, 0))
    vec = pl.BlockSpec((1, LANES), lambda b: (0, 0))
    return pl.pallas_call(
        body, name=name, grid=(t // seq,),
        out_shape=(jax.ShapeDtypeStruct((t, LANES), F32), jax.ShapeDtypeStruct((1, LANES), F32)),
        in_specs=[blk, blk, vec], out_specs=(blk, vec),
        compiler_params=_params(("arbitrary",)),
    )(d_cum, fg, b_f)


def _head_masks():
    lane = lax.broadcasted_iota(jnp.int32, (1, LANES), 1)
    return lane < HEAD_DIM, lane >= HEAD_DIM


def _pair_operands(ref, r0, n, compact, masks, masked):
    if not compact:
        return [ref[pl.ds(r0, n), h * LANES:(h + 1) * LANES] for h in range(2)]
    pair = ref[pl.ds(r0, n), :]
    return [jnp.where(mk, pair, jnp.zeros_like(pair)) for mk in masks] if masked else [pair, pair]


def _causal(n_rows, n_cols, shift):
    return (lax.broadcasted_iota(jnp.int32, (n_rows, n_cols), 1)
            <= lax.broadcasted_iota(jnp.int32, (n_rows, n_cols), 0) + shift)


def _attn_fwd(q_arr, q_off, k_arr, k_off, v_arr, v_off, bias, seq, name, ride=None, compact=False):
    t = q_arr.shape[0]
    nb = t // seq
    blk = min(ATTN_BLOCK, seq)
    nq = seq // blk
    qw = LANES if compact else 2 * LANES
    has_bias = bias is not None
    n_in, n_out = (4, 3) if has_bias else (3, 2)

    def body(*refs):
        step = pl.program_id(0) * HEAD_PAIRS + pl.program_id(1)
        refs, ride_end = _ride_steps(ride, refs, n_in, n_out, step, nb * HEAD_PAIRS)
        if has_bias:
            q_ref, k_ref, v_ref, bias_ref, o_ref, lse_ref, o32_ref = refs
        else:
            q_ref, k_ref, v_ref, o_ref, lse_ref = refs
        masks = _head_masks()
        lo = masks[0]

        def update(r0, n, carry, k0, nk, mask):
            qs = _pair_operands(q_ref, r0, n, compact, masks, True)
            ks = _pair_operands(k_ref, k0, nk, compact, masks, False)
            vv = v_ref[pl.ds(k0, nk), :]
            vs = [jnp.where(mk, vv, jnp.zeros_like(vv)) for mk in masks]
            new, alphas, pv = [], [], None
            for h in range(2):
                m, l = carry[1 + 2 * h], carry[2 + 2 * h]
                s = lax.dot_general(qs[h], ks[h], _DIMS["nt"], preferred_element_type=F32)
                if has_bias:
                    s = s + bias_ref[h, 0:1, pl.ds(k0, nk)]
                if mask is not None:
                    s = jnp.where(mask, s, -jnp.inf)
                m_new = jnp.maximum(m, jnp.max(s, axis=-1, keepdims=True))
                p = jnp.exp(s - m_new)
                alpha = jnp.exp(m - m_new)
                l_new = alpha * l + jnp.sum(p, axis=-1, keepdims=True)
                p_hi = p.astype(BF)
                d = jnp.dot(p_hi, vs[h], preferred_element_type=F32)
                if has_bias:
                    p_lo = (p - p_hi.astype(F32)).astype(BF)
                    d = d + jnp.dot(p_lo, vs[h], preferred_element_type=F32)
                pv = d if pv is None else pv + d
                alphas.append(alpha)
                new += [m_new, l_new]
            return (carry[0] * jnp.where(lo, alphas[0], alphas[1]) + pv, *new)

        def q_block(iq, _):
            q0 = pl.multiple_of(iq * blk, blk)
            init = (jnp.zeros((blk, LANES), F32),
                    jnp.full((blk, 1), -jnp.inf, F32), jnp.zeros((blk, 1), F32),
                    jnp.full((blk, 1), -jnp.inf, F32), jnp.zeros((blk, 1), F32))
            carry = lax.fori_loop(
                0, iq, lambda j, c: update(q0, blk, c, pl.multiple_of(j * blk, blk), blk, None), init)
            acc, m0, l0, m1, l1 = update(q0, blk, carry, q0, blk, _causal(blk, blk, 0))
            o_val = acc / jnp.where(lo, l0, l1)
            o_ref[pl.ds(q0, blk), :] = o_val.astype(BF)
            if has_bias:
                o32_ref[pl.ds(q0, blk), :] = o_val
            lse_ref[pl.ds(q0, blk), :] = jnp.where(lo, m0 + jnp.log(l0), m1 + jnp.log(l1))
            return 0

        lax.fori_loop(0, nq, q_block, 0)
        ride_end()

    in_specs = [pl.BlockSpec((seq, qw), lambda b, p: (b, q_off + p)),
                pl.BlockSpec((seq, qw), lambda b, p: (b, k_off + p)),
                pl.BlockSpec((seq, LANES), lambda b, p: (b, v_off + p))]
    args = [q_arr, k_arr, v_arr]
    if has_bias:
        in_specs.append(pl.BlockSpec((None, 2, 8, seq), lambda b, p: (b, p, 0, 0)))
        args.append(bias)
    out_blk = pl.BlockSpec((seq, LANES), lambda b, p: (b, p))
    out_shape = [jax.ShapeDtypeStruct((t, HEAD_PAIRS * LANES), BF), jax.ShapeDtypeStruct((t, HEAD_PAIRS * LANES), F32)]
    if has_bias:
        out_shape.append(jax.ShapeDtypeStruct((t, HEAD_PAIRS * LANES), F32))
    out_specs, scratch = [out_blk] * len(out_shape), []
    if ride is not None:
        r_in, r_shape, r_out, scratch = _ride_specs(ride)
        in_specs, out_shape, out_specs = in_specs + r_in, out_shape + r_shape, out_specs + r_out
        args += list(ride[1])
    return pl.pallas_call(
        body, name=name, grid=(nb, HEAD_PAIRS),
        out_shape=out_shape, in_specs=in_specs, out_specs=out_specs, scratch_shapes=scratch,
        compiler_params=_params(("arbitrary", "arbitrary")),
    )(*args)


def _attn_bwd(q_arr, q_off, k_arr, k_off, v_arr, v_off, bias, o, do, lse, seq, name, ride=None, compact=False):
    t = q_arr.shape[0]
    nb = t // seq
    blk = min(ATTN_BLOCK, seq)
    half = blk // 2
    nq = seq // blk
    qw = LANES if compact else 2 * LANES
    has_bias = bias is not None
    n_in, n_out = (7, 4) if has_bias else (6, 3)

    def body(*refs):
        step = pl.program_id(0) * HEAD_PAIRS + pl.program_id(1)
        refs, ride_end = _ride_steps(ride, refs, n_in, n_out, step, nb * HEAD_PAIRS)
        if has_bias:
            (q_ref, k_ref, v_ref, bias_ref, o_ref, do_ref, lse_ref,
             dq_ref, dk_ref, dv_ref, dbias_ref, dq_acc, dsum) = refs
        else:
            (q_ref, k_ref, v_ref, o_ref, do_ref, lse_ref, dq_ref, dk_ref, dv_ref, dq_acc, dsum) = refs
        masks = _head_masks()
        lo, hi = masks
        dq_acc[...] = jnp.zeros_like(dq_acc)

        def prep(iq, _):
            q0 = pl.multiple_of(iq * blk, blk)
            prod = do_ref[pl.ds(q0, blk), :].astype(F32) * o_ref[pl.ds(q0, blk), :].astype(F32)
            d0 = jnp.sum(jnp.where(lo, prod, 0.0), axis=-1, keepdims=True)
            d1 = jnp.sum(jnp.where(hi, prod, 0.0), axis=-1, keepdims=True)
            dsum[pl.ds(q0, blk), :] = jnp.where(lo, d0, d1)
            return 0

        lax.fori_loop(0, nq, prep, 0)

        def tile(r0, n, k0, nk, mask):
            qs = _pair_operands(q_ref, r0, n, compact, masks, True)
            ks = _pair_operands(k_ref, k0, nk, compact, masks, True)
            vv = v_ref[pl.ds(k0, nk), :]
            vs = [jnp.where(mk, vv, jnp.zeros_like(vv)) for mk in masks]
            dov = do_ref[pl.ds(r0, n), :]
            dos = [jnp.where(mk, dov, jnp.zeros_like(dov)) for mk in masks] if compact else None
            lse_v = lse_ref[pl.ds(r0, n), :]
            dsum_v = dsum[pl.ds(r0, n), :]
            dv_c, dks, dbs = None, [], []
            for h in range(2):
                s = lax.dot_general(qs[h], ks[h], _DIMS["nt"], preferred_element_type=F32)
                if has_bias:
                    s = s + bias_ref[h, 0:1, pl.ds(k0, nk)]
                p = jnp.exp(s - lse_v[:, h * HEAD_DIM:h * HEAD_DIM + 1])
                if mask is not None:
                    p = jnp.where(mask, p, 0.0)
                dp = lax.dot_general(dov, vs[h], _DIMS["nt"], preferred_element_type=F32)
                ds = p * (dp - dsum_v[:, h * HEAD_DIM:h * HEAD_DIM + 1])
                ds_bf = ds.astype(BF)
                if compact:
                    dv_h = lax.dot_general(p.astype(BF), dos[h], _DIMS["tn"], preferred_element_type=F32)
                else:
                    dv_h = jnp.where(masks[h], lax.dot_general(p.astype(BF), dov, _DIMS["tn"],
                                                               preferred_element_type=F32), 0.0)
                dv_c = dv_h if dv_c is None else dv_c + dv_h
                dk_h = lax.dot_general(ds_bf, qs[h], _DIMS["tn"], preferred_element_type=F32)
                dq_h = jnp.dot(ds_bf, ks[h], preferred_element_type=F32)
                if compact:
                    dks = [dk_h] if h == 0 else [dks[0] + dk_h, jnp.zeros((8, LANES), F32)]
                    if h == 0:
                        dq_first = dq_h
                    else:
                        dq_acc[pl.ds(r0, n), :] += dq_first + dq_h
                else:
                    dks.append(dk_h)
                    dq_acc[pl.ds(r0, n), h * LANES:(h + 1) * LANES] += dq_h
                dbs.append(jnp.sum(ds, axis=0, keepdims=True) if has_bias else jnp.zeros((1, nk), F32))
            return (dv_c, dks[0], dks[1], dbs[0], dbs[1])

        def kv_block(j, _):
            k0 = pl.multiple_of(j * blk, blk)
            if compact:
                dv_a, dk_a, dummy, db0_a, db1_a = tile(pl.multiple_of(k0 + half, half), half, k0, blk,
                                                       _causal(half, blk, half))
                top = tile(k0, half, k0, half, _causal(half, half, 0))
                head = lambda acc, x: jnp.concatenate([acc[:half] + x, acc[half:]], axis=0)
                lead = lambda acc, x: jnp.concatenate([acc[:, :half] + x, acc[:, half:]], axis=1)
                carry = (head(dv_a, top[0]), head(dk_a, top[1]), dummy, lead(db0_a, top[3]), lead(db1_a, top[4]))
            else:
                carry = tile(k0, blk, k0, blk, _causal(blk, blk, 0))

            def q_block(iq, c):
                part = tile(pl.multiple_of(iq * blk, blk), blk, k0, blk, None)
                return tuple(a + b for a, b in zip(c, part))

            carry = lax.fori_loop(j + 1, nq, q_block, carry)
            dv_ref[pl.ds(k0, blk), :] = carry[0].astype(BF)
            if compact:
                dk_ref[pl.ds(k0, blk), :] = carry[1].astype(BF)
            else:
                for h in range(2):
                    dk_ref[pl.ds(k0, blk), h * LANES:(h + 1) * LANES] = carry[1 + h].astype(BF)
            if has_bias:
                for h in range(2):
                    dbias_ref[h, :, pl.ds(k0, blk)] = jnp.broadcast_to(carry[3 + h], (8, blk))
            return 0

        lax.fori_loop(0, nq, kv_block, 0)
        dq_ref[...] = dq_acc[...].astype(BF)
        ride_end()

    pair256 = lambda off: pl.BlockSpec((seq, qw), lambda b, p: (b, off + p))
    pair128 = lambda off: pl.BlockSpec((seq, LANES), lambda b, p: (b, off + p))
    bias_spec = pl.BlockSpec((None, 2, 8, seq), lambda b, p: (b, p, 0, 0))
    in_specs = [pair256(q_off), pair256(k_off), pair128(v_off)]
    args = [q_arr, k_arr, v_arr]
    if has_bias:
        in_specs.append(bias_spec)
        args.append(bias)
    in_specs += [pair128(0), pair128(0), pair128(0)]
    args += [o, do, lse]
    out_shape = [jax.ShapeDtypeStruct((t, HEAD_PAIRS * qw), BF),
                 jax.ShapeDtypeStruct((t, HEAD_PAIRS * qw), BF),
                 jax.ShapeDtypeStruct((t, HEAD_PAIRS * LANES), BF)]
    out_specs = [pair256(0), pair256(0), pair128(0)]
    if has_bias:
        out_shape.append(jax.ShapeDtypeStruct((nb, HEADS, 8, seq), F32))
        out_specs.append(bias_spec)
    scratch = [pltpu.VMEM((seq, qw), F32), pltpu.VMEM((seq, LANES), F32)]
    if ride is not None:
        r_in, r_shape, r_out, r_scratch = _ride_specs(ride)
        in_specs, out_shape, out_specs = in_specs + r_in, out_shape + r_shape, out_specs + r_out
        args += list(ride[1])
        scratch += r_scratch
    return pl.pallas_call(
        body, name=name, grid=(nb, HEAD_PAIRS),
        out_shape=out_shape, in_specs=in_specs, out_specs=out_specs, scratch_shapes=scratch,
        compiler_params=_params(("arbitrary", "arbitrary")),
    )(*args)


def _adamw(w, g, m, v, name):
    shape = w.shape
    last = shape[-1]
    rows = int(np.prod(shape[:-1])) if len(shape) > 1 else 1
    tr = _rows(rows, 512)

    def body(w_ref, g_ref, m_ref, v_ref, d_ref, nm_ref, nv_ref):
        d_ref[...], nm_ref[...], nv_ref[...] = _adamw_math(w_ref[...], g_ref[...], m_ref[...], v_ref[...])

    blk = pl.BlockSpec((tr, last), lambda i: (i, 0))
    sds = jax.ShapeDtypeStruct((rows, last), F32)
    outs = pl.pallas_call(
        body, name=name, grid=(rows // tr,),
        out_shape=(sds, sds, sds), in_specs=[blk] * 4, out_specs=(blk,) * 3,
        compiler_params=_params(("parallel",)),
    )(*[a.reshape(rows, last) for a in (w, g, m, v)])
    return tuple(a.reshape(shape) for a in outs)


LOW_COLS = 256


def _low_pad(a):
    return jnp.pad(a, ((0, 0),) * (a.ndim - 1) + ((0, LOW_COLS - a.shape[-1]),))


def _layer_shards(w, i):
    j = i // 2
    bf = lambda a: a.astype(BF)
    if i % 2 == 0:
        mixer = [bf(w["fox_w_in"][j]), bf(w["fox_w_out"][j])]
    else:
        mixer = [jnp.concatenate([bf(w["mla_w_dq"][j]), bf(w["mla_w_ukv"][j]), _low_pad(bf(w["mla_w_uq"][j])),
                                  _low_pad(bf(w["mla_w_dkv"][j]))], axis=0), bf(w["mla_w_out"][j])]
    return mixer + [bf(w["mlp_w1"][i]), bf(w["mlp_w2"][i])]


def _side_by_side(stack, r0, rows, cols=None):
    return jnp.concatenate([stack[dd, r0:r0 + rows, :cols] for dd in range(N_DEV)], axis=1)


def _stacked(stack, r0, rows, cols=None):
    part = stack[:, r0:r0 + rows, :cols]
    return part.reshape(N_DEV * rows, part.shape[2])


def _layer_mixer_weights(i, first, out_all):
    full = dict(w_out=_stacked(out_all, 0, 128))
    if i % 2 == 0:
        full["fox_w_in"] = _side_by_side(first, 0, 1024)
    else:
        full.update(mla_w_dq=_stacked(first, 0, 128), mla_w_ukv=_side_by_side(first, 128, 128),
                    mla_w_uq=_side_by_side(first, 256, 256, 192), mla_w_dkv=_stacked(first, 512, 128, 160))
    return full


def _by_dest_rows(g):
    return g.reshape(N_DEV, g.shape[0] // N_DEV, g.shape[1]).astype(BF)


def _by_dest_cols(g):
    n = g.shape[1] // N_DEV
    return jnp.stack([g[:, dd * n:(dd + 1) * n] for dd in range(N_DEV)]).astype(BF)


def _layer_mixer_grad_bufs(i, g):
    if i % 2 == 0:
        return [_by_dest_cols(g["fox_w_in"]), _by_dest_rows(g["w_out"])]
    low = jnp.concatenate([_by_dest_rows(g["mla_w_dq"]), _by_dest_cols(g["mla_w_ukv"]),
                           _low_pad(_by_dest_cols(g["mla_w_uq"])), _low_pad(_by_dest_rows(g["mla_w_dkv"]))], axis=1)
    return [low, _by_dest_rows(g["w_out"])]


def _low_shard_grads(low):
    return dict(mla_w_dq=low[:128], mla_w_ukv=low[128:256], mla_w_uq=low[256:512, :192], mla_w_dkv=low[512:, :160])


def _pad_heads(w, width):
    k = w.shape[0]
    return jnp.pad(w.reshape(k, HEADS, width), ((0, 0), (0, 0), (0, LANES - width))).reshape(k, HEADS * LANES)


def _unpad_heads(w, width):
    k = w.shape[0]
    return w.reshape(k, HEADS, LANES)[:, :, :width].reshape(k, HEADS * width)


def _rope_tables(positions, scale):
    inv_freq = 10000.0 ** (-jnp.arange(0, 2 * ROPE_HALF, 2, dtype=F32) / (2 * ROPE_HALF))
    ang = positions.astype(F32)[:, None] * inv_freq
    cos, sin = jnp.cos(ang) * scale, jnp.sin(ang) * scale
    t = positions.shape[0]
    z = lambda n: jnp.zeros((t, n), F32)
    cos_p = jnp.concatenate([jnp.full((t, HEAD_DIM), scale, F32), cos, cos, z(32)], axis=1)
    sin_a = jnp.concatenate([z(64), -sin, z(48)], axis=1)
    sin_b = jnp.concatenate([z(80), sin, z(32)], axis=1)
    fwd = (cos_p, sin_a, sin_b)
    bwd = (cos_p, jnp.roll(sin_b, -ROPE_HALF, axis=1), jnp.roll(sin_a, ROPE_HALF, axis=1))
    return fwd, bwd


def _key_rows(cum, nb, seq):
    v = -cum.reshape(nb, seq, LANES)[:, :, :HEADS]
    return jnp.broadcast_to(jnp.transpose(v, (0, 2, 1))[:, :, None, :], (nb, HEADS, 8, seq))


def kernel(x, c, positions, ada_w, ada_b, norm_mix_g, norm_mlp_g, fox_w_in, fox_b_f, fox_w_out, mla_w_dq, mla_q_norm_g, mla_w_uq, mla_w_dkv, mla_kv_norm_g, mla_w_ukv, mla_w_out, mlp_w1, mlp_w2, final_norm_g, loss_target, m_ada_w, m_ada_b, m_norm_mix_g, m_norm_mlp_g, m_fox_w_in, m_fox_b_f, m_fox_w_out, m_mla_w_dq, m_mla_q_norm_g, m_mla_w_uq, m_mla_w_dkv, m_mla_kv_norm_g, m_mla_w_ukv, m_mla_w_out, m_mlp_w1, m_mlp_w2, m_final_norm_g, v_ada_w, v_ada_b, v_norm_mix_g, v_norm_mlp_g, v_fox_w_in, v_fox_b_f, v_fox_w_out, v_mla_w_dq, v_mla_q_norm_g, v_mla_w_uq, v_mla_w_dkv, v_mla_kv_norm_g, v_mla_w_ukv, v_mla_w_out, v_mlp_w1, v_mlp_w2, v_final_norm_g):
    args = dict(locals())
    weights = {n: args[n] for n in WEIGHTS}
    nb, seq, d = x.shape
    t = nb * seq
    depth = ada_w.shape[0]
    dev = 4 * lax.axis_index("x") + 2 * lax.axis_index("y") + lax.axis_index("c")
    n_mod_local = ada_w.shape[2]

    n_qg = mla_q_norm_g.shape[1]
    cond = jnp.concatenate([c, jnp.pad(mla_q_norm_g.reshape(1, -1), ((0, 7), (0, d - 2 * n_qg)))], axis=0)
    w1_rows, w2_rows = mlp_w1.shape[1], mlp_w2.shape[1]
    shards = [_layer_shards(weights, i) for i in range(depth)]
    stacks = [None] * depth
    *stacks[0], cond_all = _all_gather(shards[0][:2] + [cond], "gather_first")
    c_all = cond_all[:, :nb].reshape(N_DEV * nb, d)
    q_gain = jnp.transpose(cond_all[:, nb, :2 * n_qg].reshape(N_DEV, 2, n_qg), (1, 0, 2)).reshape(2, N_DEV * n_qg)
    mod_local = jnp.stack([
        _matmul(c_all, ada_w[i], mode="nn", name="ada_mod", out_dtype=F32, a_act="silu", epi="bias",
                extras=(lax.dynamic_slice_in_dim(ada_b[i], dev * n_mod_local, n_mod_local)[None, :],))
        for i in range(depth)])
    mod_all, = _all_gather([mod_local.reshape(depth * N_DEV * nb, n_mod_local)], "gather_mod")
    mod_all = jnp.transpose(mod_all.reshape(N_DEV, depth, N_DEV * nb, n_mod_local), (1, 2, 0, 3))
    mod_all = mod_all.reshape(depth, N_DEV * nb, N_DEV * n_mod_local)
    mod = lax.dynamic_slice_in_dim(mod_all, dev * nb, nb, axis=1)
    mod = mod.reshape(depth, nb, 6, 1, d)

    pos = positions.reshape(t)
    rope_q, rope_q_t = _rope_tables(pos, MLA_SCALE)
    rope_k, rope_k_t = _rope_tables(pos, 1.0)

    def fox_weights(full):
        w_in = full["fox_w_in"]
        w_qkv = jnp.concatenate([w_in[:, :d] * FOX_SCALE, w_in[:, d:3 * d]], axis=1)
        w_f = jnp.pad(w_in[:, 3 * d:], ((0, 0), (0, LANES - HEADS)))
        return w_qkv, w_f

    def mla_weights(full):
        w_dkv = full["mla_w_dkv"]
        w_down = jnp.concatenate([full["mla_w_dq"], w_dkv[:, :128], jnp.zeros((d, 64), BF),
                                  w_dkv[:, 128:160], jnp.zeros((d, 32), BF)], axis=1)
        w_uq = _pad_heads(full["mla_w_uq"], 96)
        w_ukv = full["mla_w_ukv"].reshape(128, HEADS, 2, HEAD_DIM)
        w_uk = jnp.pad(w_ukv[:, :, 0, :], ((0, 0), (0, 0), (0, 64))).reshape(128, HEADS * LANES)
        w_uv = w_ukv[:, :, 1, :].reshape(128, HEADS * HEAD_DIM)
        place = np.zeros((128, HEADS, LANES), np.float32)
        for i in range(2 * ROPE_HALF):
            place[64 + i, :, 64 + i] = 1.0
        bottom = jnp.concatenate([jnp.asarray(place.reshape(128, HEADS * LANES), BF),
                                  jnp.zeros((128, HEADS * HEAD_DIM), BF)], axis=1)
        w_kv = jnp.concatenate([jnp.concatenate([w_uk, w_uv], axis=1), bottom], axis=0)
        return w_down, w_uq, w_kv

    tm_big = min(2048, t)
    xs = x.reshape(t, d)
    saved = []
    for i in range(depth):
        j = i // 2
        sh_m, sc_m, g_m, sh_f, sc_f, g_f = (mod[i, :, q] for q in range(6))
        gain_mix = norm_mix_g[i][None, :]
        gain_mlp = norm_mlp_g[i][None, :]
        s = dict(x_in=xs)
        h = _norm_mod(xs, gain_mix, sc_m, sh_m, seq, "norm_mix")
        s["h"] = h
        full = _layer_mixer_weights(i, stacks[i][0], stacks[i][1])
        riders = (shards[0][2:] if i == 0 else []) + (shards[i + 1] if i + 1 < depth else [])
        ride = ("gather", riders) if riders else None
        if i % 2 == 0:
            w_qkv, w_f = fox_weights(full)
            qkv = _matmul(h, w_qkv, mode="nn", name="fox_qkv")
            fg = _matmul(h, w_f, mode="nn", name="fox_gate_logits", out_dtype=F32)
            b_f = jnp.pad(fox_b_f[j], (0, LANES - HEADS))[None, :]
            cum = _fox_gate(fg, b_f, seq, "fox_gate")
            bias = _key_rows(cum, nb, seq)
            o, lse, o32, *rode = _attn_fwd(qkv, 0, qkv, 8, qkv, 16, bias, seq, "fox_attn", ride, compact=True)
            s.update(qkv=qkv, fg=fg, b_f=b_f, bias=bias, w_qkv=w_qkv, w_f=w_f, o32=o32)
        else:
            w_down, w_uq, w_kv = mla_weights(full)
            down = _matmul(h, w_down, mode="nn", name="mla_down", out_dtype=F32)
            gq, gkv = q_gain[j][None, :], mla_kv_norm_g[j][None, :]
            cq, ckr = _mla_mid(down, gq, gkv, rope_k, "mla_mid")
            q_raw = _matmul(cq, w_uq, mode="nn", name="mla_uq", out_dtype=F32)
            q_rot = _rope(q_raw, rope_q, "mla_rope_q")
            kv = _matmul(ckr, w_kv, mode="nn", name="mla_ukv")
            o, lse, *rode = _attn_fwd(q_rot, 0, kv, 0, kv, 16, None, seq, "mla_attn", ride)
            s.update(down=down, gq=gq, gkv=gkv, cq=cq, ckr=ckr, q_rot=q_rot, kv=kv,
                     w_down=w_down, w_uq=w_uq, w_kv=w_kv)
        if i == 0:
            stacks[0], rode = stacks[0] + rode[:2], rode[2:]
        if i + 1 < depth:
            stacks[i + 1] = rode
        w_out = full["w_out"]
        xs, y = _matmul(o, w_out, mode="nn", name="attn_out", epi="resid_gate", extras=(xs, g_m), seq=seq)
        s.update(o=o, lse=lse, y=y, w_out=w_out, x_mid=xs)
        h2 = _norm_mod(xs, gain_mlp, sc_f, sh_f, seq, "norm_mlp")
        a_pre = _matmul(h2, stacks[i][2], mode="nn", name="mlp_up", layer=("col", 0, w1_rows), tm=tm_big)
        xs, y2 = _matmul(a_pre, stacks[i][3], mode="nn", name="mlp_down", layer=("row", 0, w2_rows), a_act="relu2",
                         epi="resid_gate", extras=(xs, g_f), seq=seq)
        s.update(h2=h2, a_pre=a_pre, y2=y2)
        saved.append(s)

    loss_part, dx, dg_final = _loss_head(xs, final_norm_g[None, :], loss_target.reshape(t, d), "loss_head")

    w1_cols, w1_tm = mlp_w1.shape[2], _pick(w1_rows, 1024)
    dg_mix, dg_mlp, db_f, dg_kv, dg_q = [None] * depth, [None] * depth, [None] * 2, [None] * 2, [None] * 2
    dmod = [None] * depth
    rel = _relative_blocks()
    chains = dict(fox_w_in=None, fox_w_out=None, mla_w_out=None, mlp_w1=None, mlp_w2=None)
    low_grads = [None] * 2

    def adam_step(name, layer):
        def land_one(own, got):
            chains[name] = _total_adamw(own, got, weights[name], args["m_" + name], args["v_" + name], layer,
                                        chains[name], "adamw_" + name)
        return land_one

    def low_step(lj):
        def land_one(own, got):
            low_grads[lj] = _low_shard_grads(_add_parts(own, got, "grads_total"))
        return land_one

    def mixer_steps(li):
        lj = li // 2
        return ([adam_step("fox_w_in", lj), adam_step("fox_w_out", lj)] if li % 2 == 0
                else [low_step(lj), adam_step("mla_w_out", lj)])

    def mlp_steps(li):
        return [adam_step("mlp_w1", li), adam_step("mlp_w2", li)]

    def stage(bufs, steps):
        from_sibling = _swap_with_sibling(bufs, "grads_to_sibling")
        return [(step,) + tuple(_chip_partial(b, fs, rel, "grads_chip_sum"))
                for b, fs, step in zip(bufs, from_sibling, steps)]

    def land(staged, got):
        for (step, own, _), g in zip(staged, got):
            step(own, g)

    waiting = []
    dy2, dg_f = _gate_bwd(dx, saved[depth - 1]["y2"], mod[depth - 1, :, 5], seq, "gate_bwd")
    for i in reversed(range(depth)):
        j = i // 2
        s = saved[i]
        sh_m, sc_m, g_m, sh_f, sc_f, g_f = (mod[i, :, q] for q in range(6))
        da_pre = _matmul(dy2, stacks[i][3], mode="nt", name="mlp_down_dx", layer=("row", 0, w2_rows), epi="mul_drelu",
                         extras=(s["a_pre"],), tm=tm_big)
        g_w2 = _matmul(s["a_pre"], dy2, mode="tn", name="mlp_down_dw", a_act="relu2", tm=w2_rows,
                       into=(jax.ShapeDtypeStruct((N_DEV, w2_rows, d), BF), (None, w2_rows, d),
                             lambda r, j, k: (r, 0, 0)))
        dh2 = _matmul(da_pre, stacks[i][2], mode="nt", name="mlp_up_dx", layer=("col", 0, w1_rows), tm=tm_big)
        g_w1 = _matmul(s["h2"], da_pre, mode="tn", name="mlp_up_dw", tm=w1_tm, tn=w1_cols,
                       into=(jax.ShapeDtypeStruct((N_DEV, w1_rows, w1_cols), BF), (None, w1_tm, w1_cols),
                             lambda r, j, k: (j, r, 0)))
        if i == 0:
            waiting += stage([g_w1, g_w2], mlp_steps(0))
        dx, dg_mlp[i], dsc_f, dsh_f, dy, dg_m = _norm_mod_bwd(dh2, s["x_mid"], norm_mlp_g[i][None, :], sc_f, dx, seq,
                                                              "norm_bwd_gate", gate=(s["y"], g_m))
        do = _matmul(dy, s["w_out"], mode="nt", name="attn_out_dx")
        dw_out = _matmul(s["o"], dy, mode="tn", name="attn_out_dw", out_dtype=F32)
        ride = ("swap", [e[2] for e in waiting]) if waiting else None
        g_mixer = dict(w_out=dw_out)
        if i % 2 == 0:
            qkv = s["qkv"]
            dq, dk, dv, dbias, *rode = _attn_bwd(qkv, 0, qkv, 8, qkv, 16, s["bias"], s["o32"], do, s["lse"], seq,
                                                 "fox_attn_bwd", ride, compact=True)
            dqkv = jnp.concatenate([dq, dk, dv], axis=1)
            d_cum = -jnp.transpose(dbias[:, :, 0, :], (0, 2, 1)).reshape(t, HEADS)
            d_cum = jnp.pad(d_cum, ((0, 0), (0, LANES - HEADS)))
            dfg, db = _fox_gate_bwd(d_cum, s["fg"], s["b_f"], seq, "fox_gate_bwd")
            db_f[j] = db
            dh = _matmul(dfg, s["w_f"], mode="nt", name="fox_gate_dx", out_dtype=F32)
            dh = _matmul(dqkv, s["w_qkv"], mode="nt", name="fox_qkv_dx", epi="add", extras=(dh,))
            dw_qkv = _matmul(s["h"], dqkv, mode="tn", name="fox_qkv_dw", out_dtype=F32)
            dw_f = _matmul(s["h"], dfg, mode="tn", name="fox_gate_dw", out_dtype=F32)
            g_mixer["fox_w_in"] = jnp.concatenate([dw_qkv[:, :d] * FOX_SCALE, dw_qkv[:, d:], dw_f[:, :HEADS]], axis=1)
        else:
            kv = s["kv"]
            dq, dk, dv, *rode = _attn_bwd(s["q_rot"], 0, kv, 0, kv, 16, None, s["o"], do, s["lse"], seq,
                                          "mla_attn_bwd", ride)
            dq_raw = _rope(dq, rope_q_t, "mla_rope_q_bwd")
            dcq = _matmul(dq_raw, s["w_uq"], mode="nt", name="mla_uq_dx")
            dw_uq = _matmul(s["cq"], dq_raw, mode="tn", name="mla_uq_dw", out_dtype=F32)
            dkv = jnp.concatenate([dk, dv], axis=1)
            dckr = _matmul(dkv, s["w_kv"], mode="nt", name="mla_ukv_dx")
            dw_kv = _matmul(s["ckr"], dkv, mode="tn", name="mla_ukv_dw", out_dtype=F32)
            d_down, dgq, dgkv = _mla_mid_bwd(s["down"], dcq, dckr, s["gq"], s["gkv"], rope_k_t, "mla_mid_bwd")
            dg_q[j], dg_kv[j] = dgq, dgkv
            dh = _matmul(d_down, s["w_down"], mode="nt", name="mla_down_dx")
            dw_down = _matmul(s["h"], d_down, mode="tn", name="mla_down_dw", out_dtype=F32)
            g_mixer["mla_w_dq"] = dw_down[:, :256]
            g_mixer["mla_w_dkv"] = jnp.concatenate([dw_down[:, 256:384], dw_down[:, 448:480]], axis=1)
            g_mixer["mla_w_uq"] = _unpad_heads(dw_uq, 96)
            dk_nope = dw_kv[:128, :HEADS * LANES].reshape(128, HEADS, LANES)[:, :, :HEAD_DIM]
            dv_w = dw_kv[:128, HEADS * LANES:].reshape(128, HEADS, HEAD_DIM)
            g_mixer["mla_w_ukv"] = jnp.concatenate([dk_nope, dv_w], axis=2).reshape(128, HEADS * LANES)
        land(waiting, rode)
        this_dg_f = dg_f
        if i > 0:
            dx, dg_mix[i], dsc_m, dsh_m, dy2, dg_f = _norm_mod_bwd(
                dh, s["x_in"], norm_mix_g[i][None, :], sc_m, dx, seq, "norm_bwd_gate",
                gate=(saved[i - 1]["y2"], mod[i - 1, :, 5]))
            waiting = stage(_layer_mixer_grad_bufs(i, g_mixer) + [g_w1, g_w2], mixer_steps(i) + mlp_steps(i))
        else:
            dx, dg_mix[i], dsc_m, dsh_m = _norm_mod_bwd(dh, s["x_in"], norm_mix_g[i][None, :], sc_m, dx, seq, "norm_bwd")
            last = stage(_layer_mixer_grad_bufs(i, g_mixer), mixer_steps(i))
            land(last, _swap_with_chips([e[2] for e in last], "grads_to_chips"))
        dmod[i] = jnp.stack([dsh_m, dsc_m, dg_m, dsh_f, dsc_f, this_dg_f], axis=1).reshape(nb, 6 * d)

    grad_x = dx.reshape(nb, seq, d)
    done = {n: tuple(a.reshape(weights[n].shape) for a in chain) for n, chain in chains.items()}
    shard_grads = {n: jnp.stack([low_grads[0][n], low_grads[1][n]]) for n in low_grads[0]}

    dmod_arr = jnp.stack(dmod)
    wide = lambda a: jnp.pad(a, ((0, 0), (0, d - a.shape[1])))
    pieces = [wide(loss_part), *dg_mix, *dg_mlp, *[wide(a) for a in db_f], *[wide(a) for a in dg_kv], dg_final,
              *[wide(a) for a in dg_q], jnp.sum(dmod_arr, axis=1).reshape(depth * 6, d)]
    n_small = sum(p.shape[0] for p in pieces)
    both = jnp.concatenate(pieces + [dmod_arr.reshape(depth * nb * 6, d)], axis=0)
    both = jnp.pad(both, ((0, (-both.shape[0]) % 8), (0, 0)))
    both_all, = _all_gather([both], "gather_small")
    total = _sum_leading(both_all, "sum_small")
    off = 0

    def take(rows):
        nonlocal off
        out = total[off:off + rows]
        off += rows
        return out

    loss = take(1)[0, 0]
    g_small = dict(
        norm_mix_g=take(depth), norm_mlp_g=take(depth), fox_b_f=take(2)[:, :HEADS], mla_kv_norm_g=take(2)[:, :128],
        final_norm_g=take(1)[0],
        mla_q_norm_g=lax.dynamic_slice_in_dim(take(2)[:, :N_DEV * n_qg], dev * n_qg, n_qg, axis=1),
        ada_b=take(depth * 6).reshape(depth, 6 * d))
    dmod_all = both_all[:, n_small:n_small + depth * nb * 6]
    dmod_all = jnp.transpose(dmod_all.reshape(N_DEV, depth, nb, 6 * d), (1, 0, 2, 3)).reshape(depth, N_DEV * nb, 6 * d)
    dmod_cols = lax.dynamic_slice_in_dim(dmod_all, dev * n_mod_local, n_mod_local, axis=2)
    g_ada_w = jnp.stack([_matmul(c_all, dmod_cols[i], mode="tn", name="ada_dw", out_dtype=F32, a_act="silu")
                         for i in range(depth)])

    all_grads = dict(shard_grads)
    all_grads.update(g_small)
    all_grads["ada_w"] = g_ada_w

    deltas, new_m, new_v = {}, {}, {}
    for n in WEIGHTS:
        if n in done:
            all_grads[n], deltas[n], new_m[n], new_v[n] = done[n]
        else:
            deltas[n], new_m[n], new_v[n] = _adamw(weights[n], all_grads[n], args["m_" + n], args["v_" + n], "adamw")

    return (loss, grad_x, *[all_grads[n] for n in WEIGHTS], *[deltas[n] for n in WEIGHTS],
            *[new_m[n] for n in WEIGHTS], *[new_v[n] for n in WEIGHTS])
```

```python
import functools
import math

import jax
import jax.numpy as jnp
import numpy as np
from jax import lax
from jax.experimental import pallas as pl
from jax.experimental.pallas import tpu as pltpu

F32 = jnp.float32
BF = jnp.bfloat16

N_DEV = 8
HEADS = 16
HEAD_PAIRS = HEADS // 2
HEAD_DIM = 64
LANES = 128
ROPE_HALF = 16
NORM_EPS = 1e-6
MLA_SCALE = 96.0 ** -0.5
FOX_SCALE = 0.125
ATTN_BLOCK = 512
ROW_BLOCK = 512
K_SPAN = 2
VMEM_LIMIT = 56 * 1024 * 1024
MESH = pl.DeviceIdType.MESH

ADAM_LR = 0.001
ADAM_B1 = 0.9
ADAM_B2 = 0.999
ADAM_EPS = 1e-08
ADAM_WD = 0.01
ADAM_STEP = 10

WEIGHTS = ("ada_w", "ada_b", "norm_mix_g", "norm_mlp_g", "fox_w_in", "fox_b_f", "fox_w_out", "mla_w_dq",
           "mla_q_norm_g", "mla_w_uq", "mla_w_dkv", "mla_kv_norm_g", "mla_w_ukv", "mla_w_out", "mlp_w1",
           "mlp_w2", "final_norm_g")


def _params(sem=None):
    return pltpu.CompilerParams(dimension_semantics=sem, vmem_limit_bytes=VMEM_LIMIT)


def _pick(n, target):
    if n <= target:
        return n
    for t in range(target, 127, -128):
        if n % t == 0:
            return t
    return n


def _rows(n, target=512):
    if n <= target:
        return n
    for t in range(target, 7, -8):
        if n % t == 0:
            return t
    return n


def _place():
    x, y, c = lax.axis_index("x"), lax.axis_index("y"), lax.axis_index("c")
    return x, y, c


def _adamw_math(w, g, m, v):
    nm = ADAM_B1 * m + (1.0 - ADAM_B1) * g
    nv = ADAM_B2 * v + (1.0 - ADAM_B2) * (g * g)
    m_hat = nm * (1.0 / (1.0 - ADAM_B1 ** ADAM_STEP))
    v_hat = nv * (1.0 / (1.0 - ADAM_B2 ** ADAM_STEP))
    return -ADAM_LR * (m_hat / (jnp.sqrt(v_hat) + ADAM_EPS) + ADAM_WD * w), nm, nv


def _all_gather(blocks, name):
    ride = ("gather", blocks)

    def body(*refs):
        start, mid, finish = _ride_phases(ride, *_ride_split(ride, refs, 0, 0)[:3])
        start()
        mid()
        finish()

    in_specs, out_shape, out_specs, scratch = _ride_specs(ride)
    return pl.pallas_call(
        body, name=name, out_shape=out_shape, in_specs=in_specs, out_specs=out_specs, scratch_shapes=scratch,
    )(*blocks)


def _ride_specs(ride):
    kind, arrays = ride
    n = len(arrays)
    any_spec = pl.BlockSpec(memory_space=pl.ANY)
    if kind == "gather":
        out_shape = [jax.ShapeDtypeStruct((N_DEV,) + b.shape, b.dtype) for b in arrays]
        scratch = [pltpu.SemaphoreType.DMA((7 * n,)), pltpu.SemaphoreType.DMA((7 * n,)), pltpu.SemaphoreType.DMA((n,))]
    else:
        out_shape = [jax.ShapeDtypeStruct((N_DEV - 1,) + p.shape[1:], p.dtype) for p in arrays]
        scratch = [pltpu.SemaphoreType.DMA((7 * n,)), pltpu.SemaphoreType.DMA((7 * n,))]
    return [any_spec] * n, out_shape, [any_spec] * n, scratch


def _ride_split(ride, refs, n_in, n_out):
    n = len(ride[1])
    n_sem = 3 if ride[0] == "gather" else 2
    src = refs[n_in:n_in + n]
    dst = refs[n_in + n + n_out:n_in + 2 * n + n_out]
    own = refs[:n_in] + refs[n_in + n:n_in + n + n_out] + refs[n_in + 2 * n + n_out:len(refs) - n_sem]
    return src, dst, refs[len(refs) - n_sem:], own


def _ride_phases(ride, src, dst, sems):
    n = len(src)
    x, y, c = _place()
    chips = [(1 - x, y), (x, 1 - y), (1 - x, 1 - y)]
    if ride[0] == "scatter":
        send_sems, recv_sems = sems

        def copies():
            out = []
            for f in (1, 2, 3, 5, 6, 7, 4):
                px, py, pc = x ^ (f & 1), y ^ ((f >> 1) & 1), c ^ (f >> 2)
                out += [pltpu.make_async_remote_copy(
                    src_ref=src[a].at[4 * px + 2 * py + pc], dst_ref=dst[a].at[f - 1],
                    send_sem=send_sems.at[7 * a + f - 1], recv_sem=recv_sems.at[7 * a + f - 1],
                    device_id=(px, py, pc), device_id_type=MESH) for a in range(n)]
            return out

        def start():
            for cp in copies():
                cp.start()

        def finish():
            for cp in copies():
                cp.wait()

        return start, lambda: None, finish

    send_sems, recv_sems, local_sems = sems
    me, sibling = (x, y, c), (x, y, 1 - c)

    def slot(a, px, py, pc):
        return dst[a].at[4 * px + 2 * py + pc]

    def copy(a, k, blk, to, from_src=False):
        return pltpu.make_async_remote_copy(
            src_ref=src[a] if from_src else slot(a, *blk), dst_ref=slot(a, *blk),
            send_sem=send_sems.at[7 * a + k], recv_sem=recv_sems.at[7 * a + k], device_id=to, device_id_type=MESH)

    def mine():
        return [pltpu.make_async_copy(src[a], slot(a, *me), local_sems.at[a]) for a in range(n)]

    def first():
        out = []
        for j, chip in enumerate(chips):
            out += [copy(a, 1 + j, me, (*chip, c), from_src=True) for a in range(n)]
        return out + [copy(a, 0, me, sibling, from_src=True) for a in range(n)]

    def passed():
        return [copy(a, 4 + j, (*chip, c), sibling) for j, chip in enumerate(chips) for a in range(n)]

    def start():
        for cp in mine() + first():
            cp.start()

    def mid():
        for j, chip in enumerate(chips):
            for a in range(n):
                copy(a, 1 + j, (*chip, c), me).wait_recv()
        for cp in passed():
            cp.start()

    def finish():
        for a in range(n):
            copy(a, 0, sibling, me).wait_recv()
        for j, chip in enumerate(chips):
            for a in range(n):
                copy(a, 4 + j, (*chip, 1 - c), me).wait_recv()
        for cp in first() + passed():
            cp.wait_send()
        for cp in mine():
            cp.wait()

    return start, mid, finish


def _ride_steps(ride, refs, n_in, n_out, step, n_steps):
    if ride is None:
        return refs, lambda: None
    src, dst, sems, own = _ride_split(ride, refs, n_in, n_out)
    start, mid, finish = _ride_phases(ride, src, dst, sems)
    pl.when(step == 0)(start)
    pl.when(step == (3 * n_steps) // 4)(mid)
    return own, lambda: pl.when(step == n_steps - 1)(finish)


def _scatter(bufs, name):
    ride = ("scatter", bufs)

    def body(*refs):
        start, _, finish = _ride_phases(ride, *_ride_split(ride, refs, 0, 0)[:3])
        start()
        finish()

    in_specs, out_shape, out_specs, scratch = _ride_specs(ride)
    return pl.pallas_call(
        body, name=name, out_shape=out_shape, in_specs=in_specs, out_specs=out_specs, scratch_shapes=scratch,
    )(*bufs)


def _total(own_ref, parts_ref):
    g = own_ref[...].astype(F32)
    for k in range(parts_ref.shape[0]):
        g = g + parts_ref[k].astype(F32)
    return g


def _total_adamw(buf, me, parts, w, m, v, layer, carry, name):
    _, r, cdim = buf.shape
    n_layers = w.shape[0]
    tr = _rows(r, 256)
    steps = r // tr

    def body(me_ref, own_ref, parts_ref, w_ref, m_ref, v_ref, *rest):
        del me_ref
        g_ref, d_ref, nm_ref, nv_ref = rest[-4:]
        g = _total(own_ref, parts_ref)
        g_ref[...] = g
        d_ref[...], nm_ref[...], nv_ref[...] = _adamw_math(w_ref[...], g, m_ref[...], v_ref[...])

    lay = pl.BlockSpec((tr, cdim), lambda i, me_ref: (layer * steps + i, 0))
    in_specs = [pl.BlockSpec((None, tr, cdim), lambda i, me_ref: (me_ref[0], i, 0)),
                pl.BlockSpec((N_DEV - 1, tr, cdim), lambda i, me_ref: (0, i, 0)), lay, lay, lay]
    operands = [me, buf, parts, *[a.reshape(n_layers * r, cdim) for a in (w, m, v)]]
    aliases = {}
    if carry is not None:
        in_specs += [pl.BlockSpec(memory_space=pl.ANY)] * 4
        operands += list(carry)
        aliases = {6 + k: k for k in range(4)}
    sds = jax.ShapeDtypeStruct((n_layers * r, cdim), F32)
    return pl.pallas_call(
        body, name=name,
        grid_spec=pltpu.PrefetchScalarGridSpec(num_scalar_prefetch=1, grid=(steps,), in_specs=in_specs,
                                               out_specs=(lay,) * 4),
        out_shape=(sds,) * 4, input_output_aliases=aliases,
        compiler_params=_params(("parallel",)),
    )(*operands)


def _add_parts(buf, me, parts, name):
    _, r, cdim = buf.shape
    tr = _rows(r, 512)

    def body(me_ref, own_ref, parts_ref, out_ref):
        del me_ref
        out_ref[...] = _total(own_ref, parts_ref)

    return pl.pallas_call(
        body, name=name,
        grid_spec=pltpu.PrefetchScalarGridSpec(
            num_scalar_prefetch=1, grid=(r // tr,),
            in_specs=[pl.BlockSpec((None, tr, cdim), lambda i, me_ref: (me_ref[0], i, 0)),
                      pl.BlockSpec((N_DEV - 1, tr, cdim), lambda i, me_ref: (0, i, 0))],
            out_specs=pl.BlockSpec((tr, cdim), lambda i, me_ref: (i, 0))),
        out_shape=jax.ShapeDtypeStruct((r, cdim), F32),
        compiler_params=_params(("parallel",)),
    )(me, buf, parts)


def _sum_leading(stack, name):
    n, r, cdim = stack.shape
    tr = _rows(r, 512)

    def body(in_ref, out_ref):
        acc = in_ref[0]
        for k in range(1, n):
            acc = acc + in_ref[k]
        out_ref[...] = acc

    return pl.pallas_call(
        body, name=name, grid=(r // tr,),
        out_shape=jax.ShapeDtypeStruct((r, cdim), F32),
        in_specs=[pl.BlockSpec((n, tr, cdim), lambda i: (0, i, 0))],
        out_specs=pl.BlockSpec((tr, cdim), lambda i: (i, 0)),
        compiler_params=_params(("parallel",)),
    )(stack)


_DIMS = {"nn": (((1,), (0,)), ((), ())), "nt": (((1,), (1,)), ((), ())), "tn": (((0,), (0,)), ((), ()))}


def _stack_spec(shape, mode, layer):
    cut, l, rows = layer
    cols = shape[2]
    by_n = pl.BlockSpec((1, rows, cols), lambda i, j, k: (j, l, 0))
    by_k = pl.BlockSpec((K_SPAN, rows, cols), lambda i, j, k: (k, l, 0))
    if cut == "col":
        return (by_n, N_DEV * cols, cols, rows) if mode == "nn" else (by_k, rows, rows, K_SPAN * cols)
    return (by_k, cols, cols, K_SPAN * rows) if mode == "nn" else (by_n, N_DEV * rows, rows, cols)


def _matmul(a, b, *, mode, name, out_dtype=BF, a_act=None, epi=None, extras=(), seq=None, layer=None, tm=None,
            tn=None, into=None):
    if mode == "tn":
        kdim, m = a.shape
    else:
        m, kdim = a.shape
    if tm is None:
        tm = _pick(m, 1024 if epi != "resid_gate" else min(1024, seq))
    tk = _pick(kdim, 4096 if mode == "tn" else 1024)
    b_spec = None
    if layer is not None:
        b_spec, n, tn, tk = _stack_spec(b.shape, mode, layer)
    else:
        n = b.shape[0] if mode == "nt" else b.shape[1]
        tn = _pick(n, 1024) if tn is None else tn
    nk = kdim // tk
    a_spec = (pl.BlockSpec((tk, tm), lambda i, j, k: (k, i)) if mode == "tn"
              else pl.BlockSpec((tm, tk), lambda i, j, k: (i, k)))
    if b_spec is None:
        b_spec = (pl.BlockSpec((tn, tk), lambda i, j, k: (j, k)) if mode == "nt"
                  else pl.BlockSpec((tk, tn), lambda i, j, k: (k, j)))
    tile = pl.BlockSpec((tm, tn), lambda i, j, k: (i, j))
    in_specs, out_specs = [a_spec, b_spec], [tile]
    out_shape = [jax.ShapeDtypeStruct((m, n), out_dtype)]
    if epi == "resid_gate":
        in_specs += [tile, pl.BlockSpec((None, 1, tn), lambda i, j, k: ((i * tm) // seq, 0, j))]
        out_shape = [jax.ShapeDtypeStruct((m, n), F32), jax.ShapeDtypeStruct((m, n), BF)]
        out_specs = [tile, tile]
    elif epi in ("mul_drelu", "add"):
        in_specs += [tile]
    elif epi == "bias":
        in_specs += [pl.BlockSpec((1, tn), lambda i, j, k: (0, j))]
    n_extra, n_out = len(in_specs) - 2, len(out_specs)
    aliases, n_kept = {}, 0
    if into is not None:
        buffer, block, index_map = into
        out_dtype = buffer.dtype
        if not isinstance(buffer, jax.ShapeDtypeStruct):
            in_specs.append(pl.BlockSpec(memory_space=pl.ANY))
            extras = tuple(extras) + (buffer,)
            aliases, n_kept = {len(in_specs) - 1: 0}, 1
        out_shape = [jax.ShapeDtypeStruct(buffer.shape, buffer.dtype)]
        out_specs = [pl.BlockSpec(block, index_map)]
    dims = _DIMS[mode]

    def body(*refs):
        a_ref, b_ref = refs[:2]
        ex = refs[2:2 + n_extra]
        outs = refs[2 + n_extra + n_kept:2 + n_extra + n_kept + n_out]
        av = a_ref[...]
        if a_act == "relu2":
            t = jnp.maximum(av.astype(F32), 0.0)
            av = t * t
        elif a_act == "silu":
            t = av.astype(F32)
            av = t / (1.0 + jnp.exp(-t))
        av = av.astype(BF)
        if layer is None:
            part = lax.dot_general(av, b_ref[...].astype(BF), dims, preferred_element_type=F32)
        else:
            span = b_ref.shape[0]
            wk = av.shape[1] // span
            part = None
            for u in range(span):
                p_u = lax.dot_general(av[:, u * wk:(u + 1) * wk], b_ref[u], dims, preferred_element_type=F32)
                part = p_u if part is None else part + p_u

        def finish(acc):
            if epi == "resid_gate":
                outs[0][...] = ex[0][...] + ex[1][...] * acc
                outs[1][...] = acc.astype(BF)
            elif epi == "mul_drelu":
                outs[0][...] = (acc * (2.0 * jnp.maximum(ex[0][...].astype(F32), 0.0))).astype(out_dtype)
            elif epi == "add":
                outs[0][...] = (acc + ex[0][...].astype(F32)).astype(out_dtype)
            elif epi == "bias":
                outs[0][...] = (acc + ex[0][...]).astype(out_dtype)
            else:
                outs[0][...] = acc.astype(out_dtype)

        if nk == 1:
            finish(part)
        else:
            acc_ref = refs[-1]
            k = pl.program_id(2)

            @pl.when(k == 0)
            def _():
                acc_ref[...] = part

            @pl.when(k > 0)
            def _():
                acc_ref[...] += part

            @pl.when(k == nk - 1)
            def _():
                finish(acc_ref[...])

    res = pl.pallas_call(
        body, name=name, grid=(m // tm, n // tn, nk),
        out_shape=out_shape, in_specs=in_specs, out_specs=out_specs,
        scratch_shapes=[pltpu.VMEM((tm, tn), F32)] if nk > 1 else [],
        input_output_aliases=aliases,
        compiler_params=_params(("parallel", "parallel", "arbitrary")),
    )(a, b, *extras)
    return res if n_out > 1 else res[0]


def _norm_mod(x, gain, scale, shift, seq, name):
    t, w = x.shape
    tr = ROW_BLOCK

    def body(x_ref, g_ref, sc_ref, sh_ref, out_ref):
        xv = x_ref[...]
        rstd = lax.rsqrt(jnp.mean(xv * xv, axis=-1, keepdims=True) + NORM_EPS)
        y = xv * rstd * g_ref[...]
        out_ref[...] = (y * (1.0 + sc_ref[...]) + sh_ref[...]).astype(BF)

    per_b = pl.BlockSpec((None, 1, w), lambda i: ((i * tr) // seq, 0, 0))
    return pl.pallas_call(
        body, name=name, grid=(t // tr,),
        out_shape=jax.ShapeDtypeStruct((t, w), BF),
        in_specs=[pl.BlockSpec((tr, w), lambda i: (i, 0)), pl.BlockSpec((1, w), lambda i: (0, 0)), per_b, per_b],
        out_specs=pl.BlockSpec((tr, w), lambda i: (i, 0)),
        compiler_params=_params(("parallel",)),
    )(x, gain, scale, shift)


def _norm_mod_bwd(dh, x, gain, scale, dres, seq, name, gate=None):
    t, w = x.shape
    tr = ROW_BLOCK
    steps_per_seq = seq // tr
    nb = t // seq
    gated = gate is not None

    def body(*refs):
        dh_ref, x_ref, g_ref, sc_ref, dres_ref = refs[:5]
        dx_ref, dg_ref, dsc_ref, dsh_ref = refs[-6:-2] if gated else refs[-4:]
        i = pl.program_id(0)
        xv = x_ref[...]
        dhv = dh_ref[...].astype(F32)
        rstd = lax.rsqrt(jnp.mean(xv * xv, axis=-1, keepdims=True) + NORM_EPS)
        xhat = xv * rstd
        one_sc = 1.0 + sc_ref[...]
        g = g_ref[...]
        dxhat = dhv * (g * one_sc)
        proj = jnp.mean(dxhat * xhat, axis=-1, keepdims=True)
        dxv = dres_ref[...] + rstd * (dxhat - xhat * proj)
        dx_ref[...] = dxv
        dhx = dhv * xhat
        first = [(dg_ref, jnp.sum(dhx * one_sc, axis=0, keepdims=True))]
        per_seq = [(dsc_ref, jnp.sum(dhx * g, axis=0, keepdims=True)), (dsh_ref, jnp.sum(dhv, axis=0, keepdims=True))]
        if gated:
            y_ref, gate_ref, dy_ref, dgate_ref = refs[5], refs[6], refs[-2], refs[-1]
            dy_ref[...] = (dxv * gate_ref[...]).astype(BF)
            per_seq.append((dgate_ref, jnp.sum(dxv * y_ref[...].astype(F32), axis=0, keepdims=True)))
        for cond_new, cond_add, group in ((i == 0, i > 0, first),
                                          (i % steps_per_seq == 0, i % steps_per_seq != 0, per_seq)):
            @pl.when(cond_new)
            def _(group=group):
                for ref, part in group:
                    ref[...] = part

            @pl.when(cond_add)
            def _(group=group):
                for ref, part in group:
                    ref[...] += part

    row = pl.BlockSpec((tr, w), lambda i: (i, 0))
    per_b = pl.BlockSpec((None, 1, w), lambda i: ((i * tr) // seq, 0, 0))
    vec = pl.BlockSpec((1, w), lambda i: (0, 0))
    out_shape = [jax.ShapeDtypeStruct((t, w), F32), jax.ShapeDtypeStruct((1, w), F32),
                 jax.ShapeDtypeStruct((nb, 1, w), F32), jax.ShapeDtypeStruct((nb, 1, w), F32)]
    in_specs, out_specs, operands = [row, row, vec, per_b, row], [row, vec, per_b, per_b], [dh, x, gain, scale, dres]
    if gated:
        in_specs += [row, per_b]
        operands += list(gate)
        out_shape += [jax.ShapeDtypeStruct((t, w), BF), jax.ShapeDtypeStruct((nb, 1, w), F32)]
        out_specs += [row, per_b]
    return pl.pallas_call(
        body, name=name, grid=(t // tr,),
        out_shape=out_shape, in_specs=in_specs, out_specs=out_specs,
        compiler_params=_params(("arbitrary",)),
    )(*operands)


def _gate_bwd(dx, y, gate, seq, name):
    t, w = dx.shape
    tr = ROW_BLOCK
    steps_per_seq = seq // tr
    nb = t // seq

    def body(dx_ref, y_ref, g_ref, dy_ref, dg_ref):
        i = pl.program_id(0)
        dxv = dx_ref[...]
        dy_ref[...] = (dxv * g_ref[...]).astype(BF)
        part = jnp.sum(dxv * y_ref[...].astype(F32), axis=0, keepdims=True)

        @pl.when(i % steps_per_seq == 0)
        def _():
            dg_ref[...] = part

        @pl.when(i % steps_per_seq != 0)
        def _():
            dg_ref[...] += part

    row = pl.BlockSpec((tr, w), lambda i: (i, 0))
    per_b = pl.BlockSpec((None, 1, w), lambda i: ((i * tr) // seq, 0, 0))
    return pl.pallas_call(
        body, name=name, grid=(t // tr,),
        out_shape=(jax.ShapeDtypeStruct((t, w), BF), jax.ShapeDtypeStruct((nb, 1, w), F32)),
        in_specs=[row, row, per_b], out_specs=(row, per_b),
        compiler_params=_params(("arbitrary",)),
    )(dx, y, gate)


def _loss_head(x, gain, target, name):
    t, w = x.shape
    tr = ROW_BLOCK

    def body(x_ref, g_ref, t_ref, loss_ref, dx_ref, dg_ref):
        i = pl.program_id(0)
        xv = x_ref[...]
        g = g_ref[...]
        rstd = lax.rsqrt(jnp.mean(xv * xv, axis=-1, keepdims=True) + NORM_EPS)
        xhat = xv * rstd
        err = xhat * g - t_ref[...]
        row_loss = jnp.sum(err * err, axis=-1, keepdims=True) * (0.5 / w)
        loss_part = jnp.broadcast_to(jnp.sum(row_loss, axis=0, keepdims=True), (1, LANES))
        dy = err * (1.0 / w)
        dg_part = jnp.sum(dy * xhat, axis=0, keepdims=True)
        dxhat = dy * g
        proj = jnp.mean(dxhat * xhat, axis=-1, keepdims=True)
        dx_ref[...] = rstd * (dxhat - xhat * proj)

        @pl.when(i == 0)
        def _():
            loss_ref[...] = loss_part
            dg_ref[...] = dg_part

        @pl.when(i > 0)
        def _():
            loss_ref[...] += loss_part
            dg_ref[...] += dg_part

    row = pl.BlockSpec((tr, w), lambda i: (i, 0))
    vec = pl.BlockSpec((1, w), lambda i: (0, 0))
    return pl.pallas_call(
        body, name=name, grid=(t // tr,),
        out_shape=(jax.ShapeDtypeStruct((1, LANES), F32), jax.ShapeDtypeStruct((t, w), F32),
                   jax.ShapeDtypeStruct((1, w), F32)),
        in_specs=[row, vec, row],
        out_specs=(pl.BlockSpec((1, LANES), lambda i: (0, 0)), row, vec),
        compiler_params=_params(("arbitrary",)),
    )(x, gain, target)


def _rope_group(xg, cos_p, sin_a, sin_b):
    return (xg * cos_p + pltpu.roll(xg, LANES - ROPE_HALF, axis=1) * sin_a
            + pltpu.roll(xg, ROPE_HALF, axis=1) * sin_b)


def _rope(x, tables, name, out_dtype=BF):
    t, w = x.shape
    tr = ROW_BLOCK
    groups = w // LANES

    def body(x_ref, c_ref, a_ref, b_ref, out_ref):
        cos_p, sin_a, sin_b = c_ref[...], a_ref[...], b_ref[...]
        for g in range(groups):
            sl = slice(g * LANES, (g + 1) * LANES)
            out_ref[:, sl] = _rope_group(x_ref[:, sl].astype(F32), cos_p, sin_a, sin_b).astype(out_dtype)

    row = pl.BlockSpec((tr, w), lambda i: (i, 0))
    tab = pl.BlockSpec((tr, LANES), lambda i: (i, 0))
    return pl.pallas_call(
        body, name=name, grid=(t // tr,),
        out_shape=jax.ShapeDtypeStruct((t, w), out_dtype),
        in_specs=[row, tab, tab, tab], out_specs=row,
        compiler_params=_params(("parallel",)),
    )(x, *tables)


def _mla_mid(down, gq, gkv, tables, name):
    t = down.shape[0]
    tr = ROW_BLOCK

    def body(d_ref, gq_ref, gkv_ref, c_ref, a_ref, b_ref, cq_ref, ckr_ref):
        q = d_ref[:, 0:256]
        cq_ref[...] = (q * lax.rsqrt(jnp.mean(q * q, axis=-1, keepdims=True) + NORM_EPS) * gq_ref[...]).astype(BF)
        kv = d_ref[:, 256:384]
        ckr_ref[:, 0:128] = (kv * lax.rsqrt(jnp.mean(kv * kv, axis=-1, keepdims=True) + NORM_EPS)
                             * gkv_ref[...]).astype(BF)
        ckr_ref[:, 128:256] = _rope_group(d_ref[:, 384:512], c_ref[...], a_ref[...], b_ref[...]).astype(BF)

    tab = pl.BlockSpec((tr, LANES), lambda i: (i, 0))
    return pl.pallas_call(
        body, name=name, grid=(t // tr,),
        out_shape=(jax.ShapeDtypeStruct((t, 256), BF), jax.ShapeDtypeStruct((t, 256), BF)),
        in_specs=[pl.BlockSpec((tr, 512), lambda i: (i, 0)), pl.BlockSpec((1, 256), lambda i: (0, 0)),
                  pl.BlockSpec((1, 128), lambda i: (0, 0)), tab, tab, tab],
        out_specs=(pl.BlockSpec((tr, 256), lambda i: (i, 0)), pl.BlockSpec((tr, 256), lambda i: (i, 0))),
        compiler_params=_params(("parallel",)),
    )(down, gq, gkv, *tables)


def _mla_mid_bwd(down, dcq, dckr, gq, gkv, tables_t, name):
    t = down.shape[0]
    tr = ROW_BLOCK

    def norm_bwd(xv, g, dy):
        rstd = lax.rsqrt(jnp.mean(xv * xv, axis=-1, keepdims=True) + NORM_EPS)
        xhat = xv * rstd
        dxhat = dy * g
        proj = jnp.mean(dxhat * xhat, axis=-1, keepdims=True)
        return rstd * (dxhat - xhat * proj), jnp.sum(dy * xhat, axis=0, keepdims=True)

    def body(d_ref, dcq_ref, dckr_ref, gq_ref, gkv_ref, c_ref, a_ref, b_ref, dd_ref, dgq_ref, dgkv_ref):
        i = pl.program_id(0)
        dq, dgq_part = norm_bwd(d_ref[:, 0:256], gq_ref[...], dcq_ref[...].astype(F32))
        dd_ref[:, 0:256] = dq.astype(BF)
        dkv, dgkv_part = norm_bwd(d_ref[:, 256:384], gkv_ref[...], dckr_ref[:, 0:128].astype(F32))
        dd_ref[:, 256:384] = dkv.astype(BF)
        dd_ref[:, 384:512] = _rope_group(dckr_ref[:, 128:256].astype(F32), c_ref[...], a_ref[...],
                                         b_ref[...]).astype(BF)

        @pl.when(i == 0)
        def _():
            dgq_ref[...] = dgq_part
            dgkv_ref[...] = dgkv_part

        @pl.when(i > 0)
        def _():
            dgq_ref[...] += dgq_part
            dgkv_ref[...] += dgkv_part

    tab = pl.BlockSpec((tr, LANES), lambda i: (i, 0))
    r256 = pl.BlockSpec((tr, 256), lambda i: (i, 0))
    return pl.pallas_call(
        body, name=name, grid=(t // tr,),
        out_shape=(jax.ShapeDtypeStruct((t, 512), BF), jax.ShapeDtypeStruct((1, 256), F32),
                   jax.ShapeDtypeStruct((1, 128), F32)),
        in_specs=[pl.BlockSpec((tr, 512), lambda i: (i, 0)), r256, r256, pl.BlockSpec((1, 256), lambda i: (0, 0)),
                  pl.BlockSpec((1, 128), lambda i: (0, 0)), tab, tab, tab],
        out_specs=(pl.BlockSpec((tr, 512), lambda i: (i, 0)), pl.BlockSpec((1, 256), lambda i: (0, 0)),
                   pl.BlockSpec((1, 128), lambda i: (0, 0))),
        compiler_params=_params(("arbitrary",)),
    )(down, dcq, dckr, gq, gkv, *tables_t)


def _scan_rows(x, reverse):
    s = x.shape[0]
    row = lax.broadcasted_iota(jnp.int32, x.shape, 0)
    step = 1
    while step < s:
        if reverse:
            x = x + jnp.where(row < s - step, pltpu.roll(x, s - step, axis=0), 0.0)
        else:
            x = x + jnp.where(row >= step, pltpu.roll(x, step, axis=0), 0.0)
        step *= 2
    return x


def _fox_gate(fg, b_f, seq, name):
    t = fg.shape[0]

    def body(fg_ref, b_ref, out_ref):
        z = fg_ref[...] + b_ref[...]
        log_f = jnp.minimum(z, 0.0) - jnp.log(1.0 + jnp.exp(-jnp.abs(z)))
        out_ref[...] = _scan_rows(log_f, reverse=False)

    blk = pl.BlockSpec((seq, LANES), lambda b: (b, 0))
    return pl.pallas_call(
        body, name=name, grid=(t // seq,),
        out_shape=jax.ShapeDtypeStruct((t, LANES), F32),
        in_specs=[blk, pl.BlockSpec((1, LANES), lambda b: (0, 0))], out_specs=blk,
        compiler_params=_params(("parallel",)),
    )(fg, b_f)


def _fox_gate_bwd(d_cum, fg, b_f, seq, name):
    t = fg.shape[0]

    def body(dc_ref, fg_ref, b_ref, dfg_ref, db_ref):
        b = pl.program_id(0)
        z = fg_ref[...] + b_ref[...]
        d_log_f = _scan_rows(dc_ref[...], reverse=True)
        dz = d_log_f / (1.0 + jnp.exp(z))
        dfg_ref[...] = dz
        part = jnp.sum(dz, axis=0, keepdims=True)

        @pl.when(b == 0)
        def _():
            db_ref[...] = part

        @pl.when(b > 0)
        def _():
            db_ref[...] += part

    blk = pl.BlockSpec((seq, LANES), lambda b: (b, 0))
    vec = pl.BlockSpec((1, LANES), lambda b: (0, 0))
    return pl.pallas_call(
        body, name=name, grid=(t // seq,),
        out_shape=(jax.ShapeDtypeStruct((t, LANES), F32), jax.ShapeDtypeStruct((1, LANES), F32)),
        in_specs=[blk, blk, vec], out_specs=(blk, vec),
        compiler_params=_params(("arbitrary",)),
    )(d_cum, fg, b_f)


def _head_masks():
    lane = lax.broadcasted_iota(jnp.int32, (1, LANES), 1)
    return lane < HEAD_DIM, lane >= HEAD_DIM


def _pair_operands(ref, r0, n, compact, masks, masked):
    if not compact:
        return [ref[pl.ds(r0, n), h * LANES:(h + 1) * LANES] for h in range(2)]
    pair = ref[pl.ds(r0, n), :]
    return [jnp.where(mk, pair, jnp.zeros_like(pair)) for mk in masks] if masked else [pair, pair]


def _causal(n_rows, n_cols, shift):
    return (lax.broadcasted_iota(jnp.int32, (n_rows, n_cols), 1)
            <= lax.broadcasted_iota(jnp.int32, (n_rows, n_cols), 0) + shift)


def _attn_fwd(q_arr, q_off, k_arr, k_off, v_arr, v_off, bias, seq, name, ride=None, compact=False):
    t = q_arr.shape[0]
    nb = t // seq
    blk = min(ATTN_BLOCK, seq)
    nq = seq // blk
    qw = LANES if compact else 2 * LANES
    has_bias = bias is not None
    n_in, n_out = (4, 3) if has_bias else (3, 2)

    def body(*refs):
        step = pl.program_id(0) * HEAD_PAIRS + pl.program_id(1)
        refs, ride_end = _ride_steps(ride, refs, n_in, n_out, step, nb * HEAD_PAIRS)
        if has_bias:
            q_ref, k_ref, v_ref, bias_ref, o_ref, lse_ref, o32_ref = refs
        else:
            q_ref, k_ref, v_ref, o_ref, lse_ref = refs
        masks = _head_masks()
        lo = masks[0]

        def update(r0, n, carry, k0, nk, mask):
            qs = _pair_operands(q_ref, r0, n, compact, masks, True)
            ks = _pair_operands(k_ref, k0, nk, compact, masks, False)
            vv = v_ref[pl.ds(k0, nk), :]
            vs = [jnp.where(mk, vv, jnp.zeros_like(vv)) for mk in masks]
            new, alphas, pv = [], [], None
            for h in range(2):
                m, l = carry[1 + 2 * h], carry[2 + 2 * h]
                s = lax.dot_general(qs[h], ks[h], _DIMS["nt"], preferred_element_type=F32)
                if has_bias:
                    s = s + bias_ref[h, 0:1, pl.ds(k0, nk)]
                if mask is not None:
                    s = jnp.where(mask, s, -jnp.inf)
                m_new = jnp.maximum(m, jnp.max(s, axis=-1, keepdims=True))
                p = jnp.exp(s - m_new)
                alpha = jnp.exp(m - m_new)
                l_new = alpha * l + jnp.sum(p, axis=-1, keepdims=True)
                p_hi = p.astype(BF)
                d = jnp.dot(p_hi, vs[h], preferred_element_type=F32)
                if has_bias:
                    p_lo = (p - p_hi.astype(F32)).astype(BF)
                    d = d + jnp.dot(p_lo, vs[h], preferred_element_type=F32)
                pv = d if pv is None else pv + d
                alphas.append(alpha)
                new += [m_new, l_new]
            return (carry[0] * jnp.where(lo, alphas[0], alphas[1]) + pv, *new)

        def q_block(iq, _):
            q0 = pl.multiple_of(iq * blk, blk)
            init = (jnp.zeros((blk, LANES), F32),
                    jnp.full((blk, 1), -jnp.inf, F32), jnp.zeros((blk, 1), F32),
                    jnp.full((blk, 1), -jnp.inf, F32), jnp.zeros((blk, 1), F32))
            carry = lax.fori_loop(
                0, iq, lambda j, c: update(q0, blk, c, pl.multiple_of(j * blk, blk), blk, None), init)
            acc, m0, l0, m1, l1 = update(q0, blk, carry, q0, blk, _causal(blk, blk, 0))
            o_val = acc / jnp.where(lo, l0, l1)
            o_ref[pl.ds(q0, blk), :] = o_val.astype(BF)
            if has_bias:
                o32_ref[pl.ds(q0, blk), :] = o_val
            lse_ref[pl.ds(q0, blk), :] = jnp.where(lo, m0 + jnp.log(l0), m1 + jnp.log(l1))
            return 0

        lax.fori_loop(0, nq, q_block, 0)
        ride_end()

    in_specs = [pl.BlockSpec((seq, qw), lambda b, p: (b, q_off + p)),
                pl.BlockSpec((seq, qw), lambda b, p: (b, k_off + p)),
                pl.BlockSpec((seq, LANES), lambda b, p: (b, v_off + p))]
    args = [q_arr, k_arr, v_arr]
    if has_bias:
        in_specs.append(pl.BlockSpec((None, 2, 8, seq), lambda b, p: (b, p, 0, 0)))
        args.append(bias)
    out_blk = pl.BlockSpec((seq, LANES), lambda b, p: (b, p))
    out_shape = [jax.ShapeDtypeStruct((t, HEAD_PAIRS * LANES), BF), jax.ShapeDtypeStruct((t, HEAD_PAIRS * LANES), F32)]
    if has_bias:
        out_shape.append(jax.ShapeDtypeStruct((t, HEAD_PAIRS * LANES), F32))
    out_specs, scratch = [out_blk] * len(out_shape), []
    if ride is not None:
        r_in, r_shape, r_out, scratch = _ride_specs(ride)
        in_specs, out_shape, out_specs = in_specs + r_in, out_shape + r_shape, out_specs + r_out
        args += list(ride[1])
    return pl.pallas_call(
        body, name=name, grid=(nb, HEAD_PAIRS),
        out_shape=out_shape, in_specs=in_specs, out_specs=out_specs, scratch_shapes=scratch,
        compiler_params=_params(("arbitrary", "arbitrary")),
    )(*args)


def _attn_bwd(q_arr, q_off, k_arr, k_off, v_arr, v_off, bias, o, do, lse, seq, name, ride=None, compact=False):
    t = q_arr.shape[0]
    nb = t // seq
    blk = min(ATTN_BLOCK, seq)
    half = blk // 2
    nq = seq // blk
    qw = LANES if compact else 2 * LANES
    has_bias = bias is not None
    n_in, n_out = (7, 4) if has_bias else (6, 3)

    def body(*refs):
        step = pl.program_id(0) * HEAD_PAIRS + pl.program_id(1)
        refs, ride_end = _ride_steps(ride, refs, n_in, n_out, step, nb * HEAD_PAIRS)
        if has_bias:
            (q_ref, k_ref, v_ref, bias_ref, o_ref, do_ref, lse_ref,
             dq_ref, dk_ref, dv_ref, dbias_ref, dq_acc, dsum) = refs
        else:
            (q_ref, k_ref, v_ref, o_ref, do_ref, lse_ref, dq_ref, dk_ref, dv_ref, dq_acc, dsum) = refs
        masks = _head_masks()
        lo, hi = masks
        dq_acc[...] = jnp.zeros_like(dq_acc)

        def prep(iq, _):
            q0 = pl.multiple_of(iq * blk, blk)
            prod = do_ref[pl.ds(q0, blk), :].astype(F32) * o_ref[pl.ds(q0, blk), :].astype(F32)
            d0 = jnp.sum(jnp.where(lo, prod, 0.0), axis=-1, keepdims=True)
            d1 = jnp.sum(jnp.where(hi, prod, 0.0), axis=-1, keepdims=True)
            dsum[pl.ds(q0, blk), :] = jnp.where(lo, d0, d1)
            return 0

        lax.fori_loop(0, nq, prep, 0)

        def tile(r0, n, k0, nk, mask):
            qs = _pair_operands(q_ref, r0, n, compact, masks, True)
            ks = _pair_operands(k_ref, k0, nk, compact, masks, True)
            vv = v_ref[pl.ds(k0, nk), :]
            vs = [jnp.where(mk, vv, jnp.zeros_like(vv)) for mk in masks]
            dov = do_ref[pl.ds(r0, n), :]
            dos = [jnp.where(mk, dov, jnp.zeros_like(dov)) for mk in masks] if compact else None
            lse_v = lse_ref[pl.ds(r0, n), :]
            dsum_v = dsum[pl.ds(r0, n), :]
            dv_c, dks, dbs = None, [], []
            for h in range(2):
                s = lax.dot_general(qs[h], ks[h], _DIMS["nt"], preferred_element_type=F32)
                if has_bias:
                    s = s + bias_ref[h, 0:1, pl.ds(k0, nk)]
                p = jnp.exp(s - lse_v[:, h * HEAD_DIM:h * HEAD_DIM + 1])
                if mask is not None:
                    p = jnp.where(mask, p, 0.0)
                dp = lax.dot_general(dov, vs[h], _DIMS["nt"], preferred_element_type=F32)
                ds = p * (dp - dsum_v[:, h * HEAD_DIM:h * HEAD_DIM + 1])
                ds_bf = ds.astype(BF)
                if compact:
                    dv_h = lax.dot_general(p.astype(BF), dos[h], _DIMS["tn"], preferred_element_type=F32)
                else:
                    dv_h = jnp.where(masks[h], lax.dot_general(p.astype(BF), dov, _DIMS["tn"],
                                                               preferred_element_type=F32), 0.0)
                dv_c = dv_h if dv_c is None else dv_c + dv_h
                dk_h = lax.dot_general(ds_bf, qs[h], _DIMS["tn"], preferred_element_type=F32)
                dq_h = jnp.dot(ds_bf, ks[h], preferred_element_type=F32)
                if compact:
                    dks = [dk_h] if h == 0 else [dks[0] + dk_h, jnp.zeros((8, LANES), F32)]
                    if h == 0:
                        dq_first = dq_h
                    else:
                        dq_acc[pl.ds(r0, n), :] += dq_first + dq_h
                else:
                    dks.append(dk_h)
                    dq_acc[pl.ds(r0, n), h * LANES:(h + 1) * LANES] += dq_h
                dbs.append(jnp.sum(ds, axis=0, keepdims=True) if has_bias else jnp.zeros((1, nk), F32))
            return (dv_c, dks[0], dks[1], dbs[0], dbs[1])

        def kv_block(j, _):
            k0 = pl.multiple_of(j * blk, blk)
            if compact:
                dv_a, dk_a, dummy, db0_a, db1_a = tile(pl.multiple_of(k0 + half, half), half, k0, blk,
                                                       _causal(half, blk, half))
                top = tile(k0, half, k0, half, _causal(half, half, 0))
                head = lambda acc, x: jnp.concatenate([acc[:half] + x, acc[half:]], axis=0)
                lead = lambda acc, x: jnp.concatenate([acc[:, :half] + x, acc[:, half:]], axis=1)
                carry = (head(dv_a, top[0]), head(dk_a, top[1]), dummy, lead(db0_a, top[3]), lead(db1_a, top[4]))
            else:
                carry = tile(k0, blk, k0, blk, _causal(blk, blk, 0))

            def q_block(iq, c):
                part = tile(pl.multiple_of(iq * blk, blk), blk, k0, blk, None)
                return tuple(a + b for a, b in zip(c, part))

            carry = lax.fori_loop(j + 1, nq, q_block, carry)
            dv_ref[pl.ds(k0, blk), :] = carry[0].astype(BF)
            if compact:
                dk_ref[pl.ds(k0, blk), :] = carry[1].astype(BF)
            else:
                for h in range(2):
                    dk_ref[pl.ds(k0, blk), h * LANES:(h + 1) * LANES] = carry[1 + h].astype(BF)
            if has_bias:
                for h in range(2):
                    dbias_ref[h, :, pl.ds(k0, blk)] = jnp.broadcast_to(carry[3 + h], (8, blk))
            return 0

        lax.fori_loop(0, nq, kv_block, 0)
        dq_ref[...] = dq_acc[...].astype(BF)
        ride_end()

    pair256 = lambda off: pl.BlockSpec((seq, qw), lambda b, p: (b, off + p))
    pair128 = lambda off: pl.BlockSpec((seq, LANES), lambda b, p: (b, off + p))
    bias_spec = pl.BlockSpec((None, 2, 8, seq), lambda b, p: (b, p, 0, 0))
    in_specs = [pair256(q_off), pair256(k_off), pair128(v_off)]
    args = [q_arr, k_arr, v_arr]
    if has_bias:
        in_specs.append(bias_spec)
        args.append(bias)
    in_specs += [pair128(0), pair128(0), pair128(0)]
    args += [o, do, lse]
    out_shape = [jax.ShapeDtypeStruct((t, HEAD_PAIRS * qw), BF),
                 jax.ShapeDtypeStruct((t, HEAD_PAIRS * qw), BF),
                 jax.ShapeDtypeStruct((t, HEAD_PAIRS * LANES), BF)]
    out_specs = [pair256(0), pair256(0), pair128(0)]
    if has_bias:
        out_shape.append(jax.ShapeDtypeStruct((nb, HEADS, 8, seq), F32))
        out_specs.append(bias_spec)
    scratch = [pltpu.VMEM((seq, qw), F32), pltpu.VMEM((seq, LANES), F32)]
    if ride is not None:
        r_in, r_shape, r_out, r_scratch = _ride_specs(ride)
        in_specs, out_shape, out_specs = in_specs + r_in, out_shape + r_shape, out_specs + r_out
        args += list(ride[1])
        scratch += r_scratch
    return pl.pallas_call(
        body, name=name, grid=(nb, HEAD_PAIRS),
        out_shape=out_shape, in_specs=in_specs, out_specs=out_specs, scratch_shapes=scratch,
        compiler_params=_params(("arbitrary", "arbitrary")),
    )(*args)


def _adamw(w, g, m, v, name):
    shape = w.shape
    last = shape[-1]
    rows = int(np.prod(shape[:-1])) if len(shape) > 1 else 1
    tr = _rows(rows, 512)

    def body(w_ref, g_ref, m_ref, v_ref, d_ref, nm_ref, nv_ref):
        d_ref[...], nm_ref[...], nv_ref[...] = _adamw_math(w_ref[...], g_ref[...], m_ref[...], v_ref[...])

    blk = pl.BlockSpec((tr, last), lambda i: (i, 0))
    sds = jax.ShapeDtypeStruct((rows, last), F32)
    outs = pl.pallas_call(
        body, name=name, grid=(rows // tr,),
        out_shape=(sds, sds, sds), in_specs=[blk] * 4, out_specs=(blk,) * 3,
        compiler_params=_params(("parallel",)),
    )(*[a.reshape(rows, last) for a in (w, g, m, v)])
    return tuple(a.reshape(shape) for a in outs)


LOW_COLS = 256


def _low_pad(a):
    return jnp.pad(a, ((0, 0),) * (a.ndim - 1) + ((0, LOW_COLS - a.shape[-1]),))


def _layer_shards(w, i):
    j = i // 2
    bf = lambda a: a.astype(BF)
    if i % 2 == 0:
        mixer = [bf(w["fox_w_in"][j]), bf(w["fox_w_out"][j])]
    else:
        mixer = [jnp.concatenate([bf(w["mla_w_dq"][j]), bf(w["mla_w_ukv"][j]), _low_pad(bf(w["mla_w_uq"][j])),
                                  _low_pad(bf(w["mla_w_dkv"][j]))], axis=0), bf(w["mla_w_out"][j])]
    return mixer + [bf(w["mlp_w1"][i]), bf(w["mlp_w2"][i])]


def _side_by_side(stack, r0, rows, cols=None):
    return jnp.concatenate([stack[dd, r0:r0 + rows, :cols] for dd in range(N_DEV)], axis=1)


def _stacked(stack, r0, rows, cols=None):
    part = stack[:, r0:r0 + rows, :cols]
    return part.reshape(N_DEV * rows, part.shape[2])


def _layer_mixer_weights(i, first, out_all):
    full = dict(w_out=_stacked(out_all, 0, 128))
    if i % 2 == 0:
        full["fox_w_in"] = _side_by_side(first, 0, 1024)
    else:
        full.update(mla_w_dq=_stacked(first, 0, 128), mla_w_ukv=_side_by_side(first, 128, 128),
                    mla_w_uq=_side_by_side(first, 256, 256, 192), mla_w_dkv=_stacked(first, 512, 128, 160))
    return full


def _by_dest_rows(g):
    return g.reshape(N_DEV, g.shape[0] // N_DEV, g.shape[1]).astype(BF)


def _by_dest_cols(g):
    n = g.shape[1] // N_DEV
    return jnp.stack([g[:, dd * n:(dd + 1) * n] for dd in range(N_DEV)]).astype(BF)


def _layer_mixer_grad_bufs(i, g):
    if i % 2 == 0:
        return [_by_dest_cols(g["fox_w_in"]), _by_dest_rows(g["w_out"])]
    low = jnp.concatenate([_by_dest_rows(g["mla_w_dq"]), _by_dest_cols(g["mla_w_ukv"]),
                           _low_pad(_by_dest_cols(g["mla_w_uq"])), _low_pad(_by_dest_rows(g["mla_w_dkv"]))], axis=1)
    return [low, _by_dest_rows(g["w_out"])]


def _low_shard_grads(low):
    return dict(mla_w_dq=low[:128], mla_w_ukv=low[128:256], mla_w_uq=low[256:512, :192], mla_w_dkv=low[512:, :160])


def _pad_heads(w, width):
    k = w.shape[0]
    return jnp.pad(w.reshape(k, HEADS, width), ((0, 0), (0, 0), (0, LANES - width))).reshape(k, HEADS * LANES)


def _unpad_heads(w, width):
    k = w.shape[0]
    return w.reshape(k, HEADS, LANES)[:, :, :width].reshape(k, HEADS * width)


def _rope_tables(positions, scale):
    inv_freq = 10000.0 ** (-jnp.arange(0, 2 * ROPE_HALF, 2, dtype=F32) / (2 * ROPE_HALF))
    ang = positions.astype(F32)[:, None] * inv_freq
    cos, sin = jnp.cos(ang) * scale, jnp.sin(ang) * scale
    t = positions.shape[0]
    z = lambda n: jnp.zeros((t, n), F32)
    cos_p = jnp.concatenate([jnp.full((t, HEAD_DIM), scale, F32), cos, cos, z(32)], axis=1)
    sin_a = jnp.concatenate([z(64), -sin, z(48)], axis=1)
    sin_b = jnp.concatenate([z(80), sin, z(32)], axis=1)
    fwd = (cos_p, sin_a, sin_b)
    bwd = (cos_p, jnp.roll(sin_b, -ROPE_HALF, axis=1), jnp.roll(sin_a, ROPE_HALF, axis=1))
    return fwd, bwd


def _key_rows(cum, nb, seq):
    v = -cum.reshape(nb, seq, LANES)[:, :, :HEADS]
    return jnp.broadcast_to(jnp.transpose(v, (0, 2, 1))[:, :, None, :], (nb, HEADS, 8, seq))


def kernel(x, c, positions, ada_w, ada_b, norm_mix_g, norm_mlp_g, fox_w_in, fox_b_f, fox_w_out, mla_w_dq, mla_q_norm_g, mla_w_uq, mla_w_dkv, mla_kv_norm_g, mla_w_ukv, mla_w_out, mlp_w1, mlp_w2, final_norm_g, loss_target, m_ada_w, m_ada_b, m_norm_mix_g, m_norm_mlp_g, m_fox_w_in, m_fox_b_f, m_fox_w_out, m_mla_w_dq, m_mla_q_norm_g, m_mla_w_uq, m_mla_w_dkv, m_mla_kv_norm_g, m_mla_w_ukv, m_mla_w_out, m_mlp_w1, m_mlp_w2, m_final_norm_g, v_ada_w, v_ada_b, v_norm_mix_g, v_norm_mlp_g, v_fox_w_in, v_fox_b_f, v_fox_w_out, v_mla_w_dq, v_mla_q_norm_g, v_mla_w_uq, v_mla_w_dkv, v_mla_kv_norm_g, v_mla_w_ukv, v_mla_w_out, v_mlp_w1, v_mlp_w2, v_final_norm_g):
    args = dict(locals())
    weights = {n: args[n] for n in WEIGHTS}
    nb, seq, d = x.shape
    t = nb * seq
    depth = ada_w.shape[0]
    dev = 4 * lax.axis_index("x") + 2 * lax.axis_index("y") + lax.axis_index("c")
    n_mod_local = ada_w.shape[2]

    n_qg = mla_q_norm_g.shape[1]
    cond = jnp.concatenate([c, jnp.pad(mla_q_norm_g.reshape(1, -1), ((0, 7), (0, d - 2 * n_qg)))], axis=0)
    w1_rows, w2_rows = mlp_w1.shape[1], mlp_w2.shape[1]
    shards = [_layer_shards(weights, i) for i in range(depth)]
    stacks = [None] * depth
    *stacks[0], cond_all = _all_gather(shards[0][:2] + [cond], "gather_first")
    c_all = cond_all[:, :nb].reshape(N_DEV * nb, d)
    q_gain = jnp.transpose(cond_all[:, nb, :2 * n_qg].reshape(N_DEV, 2, n_qg), (1, 0, 2)).reshape(2, N_DEV * n_qg)
    mod_local = jnp.stack([
        _matmul(c_all, ada_w[i], mode="nn", name="ada_mod", out_dtype=F32, a_act="silu", epi="bias",
                extras=(lax.dynamic_slice_in_dim(ada_b[i], dev * n_mod_local, n_mod_local)[None, :],))
        for i in range(depth)])
    mod_all, = _all_gather([mod_local.reshape(depth * N_DEV * nb, n_mod_local)], "gather_mod")
    mod_all = jnp.transpose(mod_all.reshape(N_DEV, depth, N_DEV * nb, n_mod_local), (1, 2, 0, 3))
    mod_all = mod_all.reshape(depth, N_DEV * nb, N_DEV * n_mod_local)
    mod = lax.dynamic_slice_in_dim(mod_all, dev * nb, nb, axis=1)
    mod = mod.reshape(depth, nb, 6, 1, d)

    pos = positions.reshape(t)
    rope_q, rope_q_t = _rope_tables(pos, MLA_SCALE)
    rope_k, rope_k_t = _rope_tables(pos, 1.0)

    def fox_weights(full):
        w_in = full["fox_w_in"]
        w_qkv = jnp.concatenate([w_in[:, :d] * FOX_SCALE, w_in[:, d:3 * d]], axis=1)
        w_f = jnp.pad(w_in[:, 3 * d:], ((0, 0), (0, LANES - HEADS)))
        return w_qkv, w_f

    def mla_weights(full):
        w_dkv = full["mla_w_dkv"]
        w_down = jnp.concatenate([full["mla_w_dq"], w_dkv[:, :128], jnp.zeros((d, 64), BF),
                                  w_dkv[:, 128:160], jnp.zeros((d, 32), BF)], axis=1)
        w_uq = _pad_heads(full["mla_w_uq"], 96)
        w_ukv = full["mla_w_ukv"].reshape(128, HEADS, 2, HEAD_DIM)
        w_uk = jnp.pad(w_ukv[:, :, 0, :], ((0, 0), (0, 0), (0, 64))).reshape(128, HEADS * LANES)
        w_uv = w_ukv[:, :, 1, :].reshape(128, HEADS * HEAD_DIM)
        place = np.zeros((128, HEADS, LANES), np.float32)
        for i in range(2 * ROPE_HALF):
            place[64 + i, :, 64 + i] = 1.0
        bottom = jnp.concatenate([jnp.asarray(place.reshape(128, HEADS * LANES), BF),
                                  jnp.zeros((128, HEADS * HEAD_DIM), BF)], axis=1)
        w_kv = jnp.concatenate([jnp.concatenate([w_uk, w_uv], axis=1), bottom], axis=0)
        return w_down, w_uq, w_kv

    tm_big = min(2048, t)
    xs = x.reshape(t, d)
    saved = []
    for i in range(depth):
        j = i // 2
        sh_m, sc_m, g_m, sh_f, sc_f, g_f = (mod[i, :, q] for q in range(6))
        gain_mix = norm_mix_g[i][None, :]
        gain_mlp = norm_mlp_g[i][None, :]
        s = dict(x_in=xs)
        h = _norm_mod(xs, gain_mix, sc_m, sh_m, seq, "norm_mix")
        s["h"] = h
        full = _layer_mixer_weights(i, stacks[i][0], stacks[i][1])
        riders = (shards[0][2:] if i == 0 else []) + (shards[i + 1] if i + 1 < depth else [])
        ride = ("gather", riders) if riders else None
        if i % 2 == 0:
            w_qkv, w_f = fox_weights(full)
            qkv = _matmul(h, w_qkv, mode="nn", name="fox_qkv")
            fg = _matmul(h, w_f, mode="nn", name="fox_gate_logits", out_dtype=F32)
            b_f = jnp.pad(fox_b_f[j], (0, LANES - HEADS))[None, :]
            cum = _fox_gate(fg, b_f, seq, "fox_gate")
            bias = _key_rows(cum, nb, seq)
            o, lse, o32, *rode = _attn_fwd(qkv, 0, qkv, 8, qkv, 16, bias, seq, "fox_attn", ride, compact=True)
            s.update(qkv=qkv, fg=fg, b_f=b_f, bias=bias, w_qkv=w_qkv, w_f=w_f, o32=o32)
        else:
            w_down, w_uq, w_kv = mla_weights(full)
            down = _matmul(h, w_down, mode="nn", name="mla_down", out_dtype=F32)
            gq, gkv = q_gain[j][None, :], mla_kv_norm_g[j][None, :]
            cq, ckr = _mla_mid(down, gq, gkv, rope_k, "mla_mid")
            q_raw = _matmul(cq, w_uq, mode="nn", name="mla_uq", out_dtype=F32)
            q_rot = _rope(q_raw, rope_q, "mla_rope_q")
            kv = _matmul(ckr, w_kv, mode="nn", name="mla_ukv")
            o, lse, *rode = _attn_fwd(q_rot, 0, kv, 0, kv, 16, None, seq, "mla_attn", ride)
            s.update(down=down, gq=gq, gkv=gkv, cq=cq, ckr=ckr, q_rot=q_rot, kv=kv,
                     w_down=w_down, w_uq=w_uq, w_kv=w_kv)
        if i == 0:
            stacks[0], rode = stacks[0] + rode[:2], rode[2:]
        if i + 1 < depth:
            stacks[i + 1] = rode
        w_out = full["w_out"]
        xs, y = _matmul(o, w_out, mode="nn", name="attn_out", epi="resid_gate", extras=(xs, g_m), seq=seq)
        s.update(o=o, lse=lse, y=y, w_out=w_out, x_mid=xs)
        h2 = _norm_mod(xs, gain_mlp, sc_f, sh_f, seq, "norm_mlp")
        a_pre = _matmul(h2, stacks[i][2], mode="nn", name="mlp_up", layer=("col", 0, w1_rows), tm=tm_big)
        xs, y2 = _matmul(a_pre, stacks[i][3], mode="nn", name="mlp_down", layer=("row", 0, w2_rows), a_act="relu2",
                         epi="resid_gate", extras=(xs, g_f), seq=seq)
        s.update(h2=h2, a_pre=a_pre, y2=y2)
        saved.append(s)

    loss_part, dx, dg_final = _loss_head(xs, final_norm_g[None, :], loss_target.reshape(t, d), "loss_head")

    w1_cols, w1_tm = mlp_w1.shape[2], _pick(w1_rows, 1024)
    dg_mix, dg_mlp, db_f, dg_kv, dg_q = [None] * depth, [None] * depth, [None] * 2, [None] * 2, [None] * 2
    dmod = [None] * depth
    me = dev.astype(jnp.int32).reshape(1)
    chains = dict(fox_w_in=None, fox_w_out=None, mla_w_out=None, mlp_w1=None, mlp_w2=None)
    low_grads = [None] * 2

    def adam_step(name, layer):
        def land_one(buf, got):
            chains[name] = _total_adamw(buf, me, got, weights[name], args["m_" + name], args["v_" + name], layer,
                                        chains[name], "adamw_" + name)
        return land_one

    def low_step(lj):
        def land_one(buf, got):
            low_grads[lj] = _low_shard_grads(_add_parts(buf, me, got, "grads_total"))
        return land_one

    def mixer_steps(li):
        lj = li // 2
        return ([adam_step("fox_w_in", lj), adam_step("fox_w_out", lj)] if li % 2 == 0
                else [low_step(lj), adam_step("mla_w_out", lj)])

    def mlp_steps(li):
        return [adam_step("mlp_w1", li), adam_step("mlp_w2", li)]

    def land(staged, got):
        for (step, buf), g in zip(staged, got):
            step(buf, g)

    waiting = []
    dy2, dg_f = _gate_bwd(dx, saved[depth - 1]["y2"], mod[depth - 1, :, 5], seq, "gate_bwd")
    for i in reversed(range(depth)):
        j = i // 2
        s = saved[i]
        sh_m, sc_m, g_m, sh_f, sc_f, g_f = (mod[i, :, q] for q in range(6))
        da_pre = _matmul(dy2, stacks[i][3], mode="nt", name="mlp_down_dx", layer=("row", 0, w2_rows), epi="mul_drelu",
                         extras=(s["a_pre"],), tm=tm_big)
        g_w2 = _matmul(s["a_pre"], dy2, mode="tn", name="mlp_down_dw", a_act="relu2", tm=w2_rows,
                       into=(jax.ShapeDtypeStruct((N_DEV, w2_rows, d), BF), (None, w2_rows, d),
                             lambda r, j, k: (r, 0, 0)))
        dh2 = _matmul(da_pre, stacks[i][2], mode="nt", name="mlp_up_dx", layer=("col", 0, w1_rows), tm=tm_big)
        g_w1 = _matmul(s["h2"], da_pre, mode="tn", name="mlp_up_dw", tm=w1_tm, tn=w1_cols,
                       into=(jax.ShapeDtypeStruct((N_DEV, w1_rows, w1_cols), BF), (None, w1_tm, w1_cols),
                             lambda r, j, k: (j, r, 0)))
        if i == 0:
            waiting += list(zip(mlp_steps(0), [g_w1, g_w2]))
        dx, dg_mlp[i], dsc_f, dsh_f, dy, dg_m = _norm_mod_bwd(dh2, s["x_mid"], norm_mlp_g[i][None, :], sc_f, dx, seq,
                                                              "norm_bwd_gate", gate=(s["y"], g_m))
        do = _matmul(dy, s["w_out"], mode="nt", name="attn_out_dx")
        dw_out = _matmul(s["o"], dy, mode="tn", name="attn_out_dw", out_dtype=F32)
        ride = ("scatter", [e[1] for e in waiting]) if waiting else None
        g_mixer = dict(w_out=dw_out)
        if i % 2 == 0:
            qkv = s["qkv"]
            dq, dk, dv, dbias, *rode = _attn_bwd(qkv, 0, qkv, 8, qkv, 16, s["bias"], s["o32"], do, s["lse"], seq,
                                                 "fox_attn_bwd", ride, compact=True)
            dqkv = jnp.concatenate([dq, dk, dv], axis=1)
            d_cum = -jnp.transpose(dbias[:, :, 0, :], (0, 2, 1)).reshape(t, HEADS)
            d_cum = jnp.pad(d_cum, ((0, 0), (0, LANES - HEADS)))
            dfg, db = _fox_gate_bwd(d_cum, s["fg"], s["b_f"], seq, "fox_gate_bwd")
            db_f[j] = db
            dh = _matmul(dfg, s["w_f"], mode="nt", name="fox_gate_dx", out_dtype=F32)
            dh = _matmul(dqkv, s["w_qkv"], mode="nt", name="fox_qkv_dx", epi="add", extras=(dh,))
            dw_qkv = _matmul(s["h"], dqkv, mode="tn", name="fox_qkv_dw", out_dtype=F32)
            dw_f = _matmul(s["h"], dfg, mode="tn", name="fox_gate_dw", out_dtype=F32)
            g_mixer["fox_w_in"] = jnp.concatenate([dw_qkv[:, :d] * FOX_SCALE, dw_qkv[:, d:], dw_f[:, :HEADS]], axis=1)
        else:
            kv = s["kv"]
            dq, dk, dv, *rode = _attn_bwd(s["q_rot"], 0, kv, 0, kv, 16, None, s["o"], do, s["lse"], seq,
                                          "mla_attn_bwd", ride)
            dq_raw = _rope(dq, rope_q_t, "mla_rope_q_bwd")
            dcq = _matmul(dq_raw, s["w_uq"], mode="nt", name="mla_uq_dx")
            dw_uq = _matmul(s["cq"], dq_raw, mode="tn", name="mla_uq_dw", out_dtype=F32)
            dkv = jnp.concatenate([dk, dv], axis=1)
            dckr = _matmul(dkv, s["w_kv"], mode="nt", name="mla_ukv_dx")
            dw_kv = _matmul(s["ckr"], dkv, mode="tn", name="mla_ukv_dw", out_dtype=F32)
            d_down, dgq, dgkv = _mla_mid_bwd(s["down"], dcq, dckr, s["gq"], s["gkv"], rope_k_t, "mla_mid_bwd")
            dg_q[j], dg_kv[j] = dgq, dgkv
            dh = _matmul(d_down, s["w_down"], mode="nt", name="mla_down_dx")
            dw_down = _matmul(s["h"], d_down, mode="tn", name="mla_down_dw", out_dtype=F32)
            g_mixer["mla_w_dq"] = dw_down[:, :256]
            g_mixer["mla_w_dkv"] = jnp.concatenate([dw_down[:, 256:384], dw_down[:, 448:480]], axis=1)
            g_mixer["mla_w_uq"] = _unpad_heads(dw_uq, 96)
            dk_nope = dw_kv[:128, :HEADS * LANES].reshape(128, HEADS, LANES)[:, :, :HEAD_DIM]
            dv_w = dw_kv[:128, HEADS * LANES:].reshape(128, HEADS, HEAD_DIM)
            g_mixer["mla_w_ukv"] = jnp.concatenate([dk_nope, dv_w], axis=2).reshape(128, HEADS * LANES)
        land(waiting, rode)
        this_dg_f = dg_f
        if i > 0:
            dx, dg_mix[i], dsc_m, dsh_m, dy2, dg_f = _norm_mod_bwd(
                dh, s["x_in"], norm_mix_g[i][None, :], sc_m, dx, seq, "norm_bwd_gate",
                gate=(saved[i - 1]["y2"], mod[i - 1, :, 5]))
            waiting = list(zip(mixer_steps(i) + mlp_steps(i), _layer_mixer_grad_bufs(i, g_mixer) + [g_w1, g_w2]))
        else:
            dx, dg_mix[i], dsc_m, dsh_m = _norm_mod_bwd(dh, s["x_in"], norm_mix_g[i][None, :], sc_m, dx, seq, "norm_bwd")
            last = list(zip(mixer_steps(i), _layer_mixer_grad_bufs(i, g_mixer)))
            land(last, _scatter([e[1] for e in last], "grads_scatter"))
        dmod[i] = jnp.stack([dsh_m, dsc_m, dg_m, dsh_f, dsc_f, this_dg_f], axis=1).reshape(nb, 6 * d)

    grad_x = dx.reshape(nb, seq, d)
    done = {n: tuple(a.reshape(weights[n].shape) for a in chain) for n, chain in chains.items()}
    shard_grads = {n: jnp.stack([low_grads[0][n], low_grads[1][n]]) for n in low_grads[0]}

    dmod_arr = jnp.stack(dmod)
    wide = lambda a: jnp.pad(a, ((0, 0), (0, d - a.shape[1])))
    pieces = [wide(loss_part), *dg_mix, *dg_mlp, *[wide(a) for a in db_f], *[wide(a) for a in dg_kv], dg_final,
              *[wide(a) for a in dg_q], jnp.sum(dmod_arr, axis=1).reshape(depth * 6, d)]
    n_small = sum(p.shape[0] for p in pieces)
    both = jnp.concatenate(pieces + [dmod_arr.reshape(depth * nb * 6, d)], axis=0)
    both = jnp.pad(both, ((0, (-both.shape[0]) % 8), (0, 0)))
    both_all, = _all_gather([both], "gather_small")
    total = _sum_leading(both_all, "sum_small")
    off = 0

    def take(rows):
        nonlocal off
        out = total[off:off + rows]
        off += rows
        return out

    loss = take(1)[0, 0]
    g_small = dict(
        norm_mix_g=take(depth), norm_mlp_g=take(depth), fox_b_f=take(2)[:, :HEADS], mla_kv_norm_g=take(2)[:, :128],
        final_norm_g=take(1)[0],
        mla_q_norm_g=lax.dynamic_slice_in_dim(take(2)[:, :N_DEV * n_qg], dev * n_qg, n_qg, axis=1),
        ada_b=take(depth * 6).reshape(depth, 6 * d))
    dmod_all = both_all[:, n_small:n_small + depth * nb * 6]
    dmod_all = jnp.transpose(dmod_all.reshape(N_DEV, depth, nb, 6 * d), (1, 0, 2, 3)).reshape(depth, N_DEV * nb, 6 * d)
    dmod_cols = lax.dynamic_slice_in_dim(dmod_all, dev * n_mod_local, n_mod_local, axis=2)
    g_ada_w = jnp.stack([_matmul(c_all, dmod_cols[i], mode="tn", name="ada_dw", out_dtype=F32, a_act="silu")
                         for i in range(depth)])

    all_grads = dict(shard_grads)
    all_grads.update(g_small)
    all_grads["ada_w"] = g_ada_w

    deltas, new_m, new_v = {}, {}, {}
    for n in WEIGHTS:
        if n in done:
            all_grads[n], deltas[n], new_m[n], new_v[n] = done[n]
        else:
            deltas[n], new_m[n], new_v[n] = _adamw(weights[n], all_grads[n], args["m_" + n], args["v_" + n], "adamw")

    return (loss, grad_x, *[all_grads[n] for n in WEIGHTS], *[deltas[n] for n in WEIGHTS],
            *[new_m[n] for n in WEIGHTS], *[new_v[n] for n in WEIGHTS])
```

```python
import functools
import math

import jax
import jax.numpy as jnp
import numpy as np
from jax import lax
from jax.experimental import pallas as pl
from jax.experimental.pallas import tpu as pltpu

F32 = jnp.float32
BF = jnp.bfloat16

N_DEV = 8
HEADS = 16
HEAD_PAIRS = HEADS // 2
HEAD_DIM = 64
LANES = 128
ROPE_HALF = 16
NORM_EPS = 1e-6
MLA_SCALE = 96.0 ** -0.5
FOX_SCALE = 0.125
ATTN_BLOCK = 512
ROW_BLOCK = 512
K_SPAN = 4
VMEM_LIMIT = 56 * 1024 * 1024
MESH = pl.DeviceIdType.MESH

ADAM_LR = 0.001
ADAM_B1 = 0.9
ADAM_B2 = 0.999
ADAM_EPS = 1e-08
ADAM_WD = 0.01
ADAM_STEP = 10

WEIGHTS = ("ada_w", "ada_b", "norm_mix_g", "norm_mlp_g", "fox_w_in", "fox_b_f", "fox_w_out", "mla_w_dq",
           "mla_q_norm_g", "mla_w_uq", "mla_w_dkv", "mla_kv_norm_g", "mla_w_ukv", "mla_w_out", "mlp_w1",
           "mlp_w2", "final_norm_g")


def _params(sem=None):
    return pltpu.CompilerParams(dimension_semantics=sem, vmem_limit_bytes=VMEM_LIMIT)


def _pick(n, target):
    if n <= target:
        return n
    for t in range(target, 127, -128):
        if n % t == 0:
            return t
    return n


def _rows(n, target=512):
    if n <= target:
        return n
    for t in range(target, 7, -8):
        if n % t == 0:
            return t
    return n


def _place():
    x, y, c = lax.axis_index("x"), lax.axis_index("y"), lax.axis_index("c")
    return x, y, c


def _adamw_math(w, g, m, v):
    nm = ADAM_B1 * m + (1.0 - ADAM_B1) * g
    nv = ADAM_B2 * v + (1.0 - ADAM_B2) * (g * g)
    m_hat = nm * (1.0 / (1.0 - ADAM_B1 ** ADAM_STEP))
    v_hat = nv * (1.0 / (1.0 - ADAM_B2 ** ADAM_STEP))
    return -ADAM_LR * (m_hat / (jnp.sqrt(v_hat) + ADAM_EPS) + ADAM_WD * w), nm, nv


def _all_gather(blocks, name):
    ride = ("gather", blocks)

    def body(*refs):
        start, mid, finish = _ride_phases(ride, *_ride_split(ride, refs, 0, 0)[:3])
        start()
        mid()
        finish()

    in_specs, out_shape, out_specs, scratch = _ride_specs(ride)
    return pl.pallas_call(
        body, name=name, out_shape=out_shape, in_specs=in_specs, out_specs=out_specs, scratch_shapes=scratch,
    )(*blocks)


def _ride_specs(ride):
    kind, arrays = ride
    n = len(arrays)
    any_spec = pl.BlockSpec(memory_space=pl.ANY)
    if kind == "gather":
        out_shape = [jax.ShapeDtypeStruct((N_DEV,) + b.shape, b.dtype) for b in arrays]
        scratch = [pltpu.SemaphoreType.DMA((7 * n,)), pltpu.SemaphoreType.DMA((7 * n,)), pltpu.SemaphoreType.DMA((n,))]
    else:
        out_shape = [jax.ShapeDtypeStruct((N_DEV - 1,) + p.shape[1:], p.dtype) for p in arrays]
        scratch = [pltpu.SemaphoreType.DMA((7 * n,)), pltpu.SemaphoreType.DMA((7 * n,))]
    return [any_spec] * n, out_shape, [any_spec] * n, scratch


def _ride_split(ride, refs, n_in, n_out):
    n = len(ride[1])
    n_sem = 3 if ride[0] == "gather" else 2
    src = refs[n_in:n_in + n]
    dst = refs[n_in + n + n_out:n_in + 2 * n + n_out]
    own = refs[:n_in] + refs[n_in + n:n_in + n + n_out] + refs[n_in + 2 * n + n_out:len(refs) - n_sem]
    return src, dst, refs[len(refs) - n_sem:], own


def _ride_phases(ride, src, dst, sems):
    n = len(src)
    x, y, c = _place()
    chips = [(1 - x, y), (x, 1 - y), (1 - x, 1 - y)]
    if ride[0] == "scatter":
        send_sems, recv_sems = sems

        def copies():
            out = []
            for f in (1, 2, 3, 5, 6, 7, 4):
                px, py, pc = x ^ (f & 1), y ^ ((f >> 1) & 1), c ^ (f >> 2)
                out += [pltpu.make_async_remote_copy(
                    src_ref=src[a].at[4 * px + 2 * py + pc], dst_ref=dst[a].at[f - 1],
                    send_sem=send_sems.at[7 * a + f - 1], recv_sem=recv_sems.at[7 * a + f - 1],
                    device_id=(px, py, pc), device_id_type=MESH) for a in range(n)]
            return out

        def start():
            for cp in copies():
                cp.start()

        def finish():
            for cp in copies():
                cp.wait()

        return start, lambda: None, finish

    send_sems, recv_sems, local_sems = sems
    me, sibling = (x, y, c), (x, y, 1 - c)

    def slot(a, px, py, pc):
        return dst[a].at[4 * px + 2 * py + pc]

    def copy(a, k, blk, to, from_src=False):
        return pltpu.make_async_remote_copy(
            src_ref=src[a] if from_src else slot(a, *blk), dst_ref=slot(a, *blk),
            send_sem=send_sems.at[7 * a + k], recv_sem=recv_sems.at[7 * a + k], device_id=to, device_id_type=MESH)

    def mine():
        return [pltpu.make_async_copy(src[a], slot(a, *me), local_sems.at[a]) for a in range(n)]

    def first():
        out = []
        for j, chip in enumerate(chips):
            out += [copy(a, 1 + j, me, (*chip, c), from_src=True) for a in range(n)]
        return out + [copy(a, 0, me, sibling, from_src=True) for a in range(n)]

    def passed():
        return [copy(a, 4 + j, (*chip, c), sibling) for j, chip in enumerate(chips) for a in range(n)]

    def start():
        for cp in mine() + first():
            cp.start()

    def mid():
        for j, chip in enumerate(chips):
            for a in range(n):
                copy(a, 1 + j, (*chip, c), me).wait_recv()
        for cp in passed():
            cp.start()

    def finish():
        for a in range(n):
            copy(a, 0, sibling, me).wait_recv()
        for j, chip in enumerate(chips):
            for a in range(n):
                copy(a, 4 + j, (*chip, 1 - c), me).wait_recv()
        for cp in first() + passed():
            cp.wait_send()
        for cp in mine():
            cp.wait()

    return start, mid, finish


def _ride_steps(ride, refs, n_in, n_out, step, n_steps):
    if ride is None:
        return refs, lambda: None
    src, dst, sems, own = _ride_split(ride, refs, n_in, n_out)
    start, mid, finish = _ride_phases(ride, src, dst, sems)
    pl.when(step == 0)(start)
    pl.when(step == (3 * n_steps) // 4)(mid)
    return own, lambda: pl.when(step == n_steps - 1)(finish)


def _scatter(bufs, name):
    ride = ("scatter", bufs)

    def body(*refs):
        start, _, finish = _ride_phases(ride, *_ride_split(ride, refs, 0, 0)[:3])
        start()
        finish()

    in_specs, out_shape, out_specs, scratch = _ride_specs(ride)
    return pl.pallas_call(
        body, name=name, out_shape=out_shape, in_specs=in_specs, out_specs=out_specs, scratch_shapes=scratch,
    )(*bufs)


def _total(own_ref, parts_ref):
    g = own_ref[...].astype(F32)
    for k in range(parts_ref.shape[0]):
        g = g + parts_ref[k].astype(F32)
    return g


def _total_adamw(buf, me, parts, w, m, v, layer, carry, name):
    _, r, cdim = buf.shape
    n_layers = w.shape[0]
    tr = _rows(r, 256)
    steps = r // tr

    def body(me_ref, own_ref, parts_ref, w_ref, m_ref, v_ref, *rest):
        del me_ref
        g_ref, d_ref, nm_ref, nv_ref = rest[-4:]
        g = _total(own_ref, parts_ref)
        g_ref[...] = g
        d_ref[...], nm_ref[...], nv_ref[...] = _adamw_math(w_ref[...], g, m_ref[...], v_ref[...])

    lay = pl.BlockSpec((tr, cdim), lambda i, me_ref: (layer * steps + i, 0))
    in_specs = [pl.BlockSpec((None, tr, cdim), lambda i, me_ref: (me_ref[0], i, 0)),
                pl.BlockSpec((N_DEV - 1, tr, cdim), lambda i, me_ref: (0, i, 0)), lay, lay, lay]
    operands = [me, buf, parts, *[a.reshape(n_layers * r, cdim) for a in (w, m, v)]]
    aliases = {}
    if carry is not None:
        in_specs += [pl.BlockSpec(memory_space=pl.ANY)] * 4
        operands += list(carry)
        aliases = {6 + k: k for k in range(4)}
    sds = jax.ShapeDtypeStruct((n_layers * r, cdim), F32)
    return pl.pallas_call(
        body, name=name,
        grid_spec=pltpu.PrefetchScalarGridSpec(num_scalar_prefetch=1, grid=(steps,), in_specs=in_specs,
                                               out_specs=(lay,) * 4),
        out_shape=(sds,) * 4, input_output_aliases=aliases,
        compiler_params=_params(("parallel",)),
    )(*operands)


def _add_parts(buf, me, parts, name):
    _, r, cdim = buf.shape
    tr = _rows(r, 512)

    def body(me_ref, own_ref, parts_ref, out_ref):
        del me_ref
        out_ref[...] = _total(own_ref, parts_ref)

    return pl.pallas_call(
        body, name=name,
        grid_spec=pltpu.PrefetchScalarGridSpec(
            num_scalar_prefetch=1, grid=(r // tr,),
            in_specs=[pl.BlockSpec((None, tr, cdim), lambda i, me_ref: (me_ref[0], i, 0)),
                      pl.BlockSpec((N_DEV - 1, tr, cdim), lambda i, me_ref: (0, i, 0))],
            out_specs=pl.BlockSpec((tr, cdim), lambda i, me_ref: (i, 0))),
        out_shape=jax.ShapeDtypeStruct((r, cdim), F32),
        compiler_params=_params(("parallel",)),
    )(me, buf, parts)


def _sum_leading(stack, name):
    n, r, cdim = stack.shape
    tr = _rows(r, 512)

    def body(in_ref, out_ref):
        acc = in_ref[0]
        for k in range(1, n):
            acc = acc + in_ref[k]
        out_ref[...] = acc

    return pl.pallas_call(
        body, name=name, grid=(r // tr,),
        out_shape=jax.ShapeDtypeStruct((r, cdim), F32),
        in_specs=[pl.BlockSpec((n, tr, cdim), lambda i: (0, i, 0))],
        out_specs=pl.BlockSpec((tr, cdim), lambda i: (i, 0)),
        compiler_params=_params(("parallel",)),
    )(stack)


_DIMS = {"nn": (((1,), (0,)), ((), ())), "nt": (((1,), (1,)), ((), ())), "tn": (((0,), (0,)), ((), ()))}


def _stack_spec(shape, mode, layer):
    cut, l, rows = layer
    cols = shape[2]
    by_n = pl.BlockSpec((1, rows, cols), lambda i, j, k: (j, l, 0))
    by_k = pl.BlockSpec((K_SPAN, rows, cols), lambda i, j, k: (k, l, 0))
    if cut == "col":
        return (by_n, N_DEV * cols, cols, rows) if mode == "nn" else (by_k, rows, rows, K_SPAN * cols)
    return (by_k, cols, cols, K_SPAN * rows) if mode == "nn" else (by_n, N_DEV * rows, rows, cols)


def _matmul(a, b, *, mode, name, out_dtype=BF, a_act=None, epi=None, extras=(), seq=None, layer=None, tm=None,
            tn=None, into=None):
    if mode == "tn":
        kdim, m = a.shape
    else:
        m, kdim = a.shape
    if tm is None:
        tm = _pick(m, 1024 if epi != "resid_gate" else min(1024, seq))
    tk = _pick(kdim, 4096 if mode == "tn" else 1024)
    b_spec = None
    if layer is not None:
        b_spec, n, tn, tk = _stack_spec(b.shape, mode, layer)
    else:
        n = b.shape[0] if mode == "nt" else b.shape[1]
        tn = _pick(n, 1024) if tn is None else tn
    nk = kdim // tk
    a_spec = (pl.BlockSpec((tk, tm), lambda i, j, k: (k, i)) if mode == "tn"
              else pl.BlockSpec((tm, tk), lambda i, j, k: (i, k)))
    if b_spec is None:
        b_spec = (pl.BlockSpec((tn, tk), lambda i, j, k: (j, k)) if mode == "nt"
                  else pl.BlockSpec((tk, tn), lambda i, j, k: (k, j)))
    tile = pl.BlockSpec((tm, tn), lambda i, j, k: (i, j))
    in_specs, out_specs = [a_spec, b_spec], [tile]
    out_shape = [jax.ShapeDtypeStruct((m, n), out_dtype)]
    if epi == "resid_gate":
        in_specs += [tile, pl.BlockSpec((None, 1, tn), lambda i, j, k: ((i * tm) // seq, 0, j))]
        out_shape = [jax.ShapeDtypeStruct((m, n), F32), jax.ShapeDtypeStruct((m, n), BF)]
        out_specs = [tile, tile]
    elif epi in ("mul_drelu", "add"):
        in_specs += [tile]
    elif epi == "bias":
        in_specs += [pl.BlockSpec((1, tn), lambda i, j, k: (0, j))]
    n_extra, n_out = len(in_specs) - 2, len(out_specs)
    aliases, n_kept = {}, 0
    if into is not None:
        buffer, block, index_map = into
        out_dtype = buffer.dtype
        if not isinstance(buffer, jax.ShapeDtypeStruct):
            in_specs.append(pl.BlockSpec(memory_space=pl.ANY))
            extras = tuple(extras) + (buffer,)
            aliases, n_kept = {len(in_specs) - 1: 0}, 1
        out_shape = [jax.ShapeDtypeStruct(buffer.shape, buffer.dtype)]
        out_specs = [pl.BlockSpec(block, index_map)]
    dims = _DIMS[mode]

    def body(*refs):
        a_ref, b_ref = refs[:2]
        ex = refs[2:2 + n_extra]
        outs = refs[2 + n_extra + n_kept:2 + n_extra + n_kept + n_out]
        av = a_ref[...]
        if a_act == "relu2":
            t = jnp.maximum(av.astype(F32), 0.0)
            av = t * t
        elif a_act == "silu":
            t = av.astype(F32)
            av = t / (1.0 + jnp.exp(-t))
        av = av.astype(BF)
        if layer is None:
            part = lax.dot_general(av, b_ref[...].astype(BF), dims, preferred_element_type=F32)
        else:
            span = b_ref.shape[0]
            wk = av.shape[1] // span
            part = None
            for u in range(span):
                p_u = lax.dot_general(av[:, u * wk:(u + 1) * wk], b_ref[u], dims, preferred_element_type=F32)
                part = p_u if part is None else part + p_u

        def finish(acc):
            if epi == "resid_gate":
                outs[0][...] = ex[0][...] + ex[1][...] * acc
                outs[1][...] = acc.astype(BF)
            elif epi == "mul_drelu":
                outs[0][...] = (acc * (2.0 * jnp.maximum(ex[0][...].astype(F32), 0.0))).astype(out_dtype)
            elif epi == "add":
                outs[0][...] = (acc + ex[0][...].astype(F32)).astype(out_dtype)
            elif epi == "bias":
                outs[0][...] = (acc + ex[0][...]).astype(out_dtype)
            else:
                outs[0][...] = acc.astype(out_dtype)

        if nk == 1:
            finish(part)
        else:
            acc_ref = refs[-1]
            k = pl.program_id(2)

            @pl.when(k == 0)
            def _():
                acc_ref[...] = part

            @pl.when(k > 0)
            def _():
                acc_ref[...] += part

            @pl.when(k == nk - 1)
            def _():
                finish(acc_ref[...])

    res = pl.pallas_call(
        body, name=name, grid=(m // tm, n // tn, nk),
        out_shape=out_shape, in_specs=in_specs, out_specs=out_specs,
        scratch_shapes=[pltpu.VMEM((tm, tn), F32)] if nk > 1 else [],
        input_output_aliases=aliases,
        compiler_params=_params(("parallel", "parallel", "arbitrary")),
    )(a, b, *extras)
    return res if n_out > 1 else res[0]


def _norm_mod(x, gain, scale, shift, seq, name):
    t, w = x.shape
    tr = ROW_BLOCK

    def body(x_ref, g_ref, sc_ref, sh_ref, out_ref):
        xv = x_ref[...]
        rstd = lax.rsqrt(jnp.mean(xv * xv, axis=-1, keepdims=True) + NORM_EPS)
        y = xv * rstd * g_ref[...]
        out_ref[...] = (y * (1.0 + sc_ref[...]) + sh_ref[...]).astype(BF)

    per_b = pl.BlockSpec((None, 1, w), lambda i: ((i * tr) // seq, 0, 0))
    return pl.pallas_call(
        body, name=name, grid=(t // tr,),
        out_shape=jax.ShapeDtypeStruct((t, w), BF),
        in_specs=[pl.BlockSpec((tr, w), lambda i: (i, 0)), pl.BlockSpec((1, w), lambda i: (0, 0)), per_b, per_b],
        out_specs=pl.BlockSpec((tr, w), lambda i: (i, 0)),
        compiler_params=_params(("parallel",)),
    )(x, gain, scale, shift)


def _norm_mod_bwd(dh, x, gain, scale, dres, seq, name, gate=None):
    t, w = x.shape
    tr = ROW_BLOCK
    steps_per_seq = seq // tr
    nb = t // seq
    gated = gate is not None

    def body(*refs):
        dh_ref, x_ref, g_ref, sc_ref, dres_ref = refs[:5]
        dx_ref, dg_ref, dsc_ref, dsh_ref = refs[-6:-2] if gated else refs[-4:]
        i = pl.program_id(0)
        xv = x_ref[...]
        dhv = dh_ref[...].astype(F32)
        rstd = lax.rsqrt(jnp.mean(xv * xv, axis=-1, keepdims=True) + NORM_EPS)
        xhat = xv * rstd
        one_sc = 1.0 + sc_ref[...]
        g = g_ref[...]
        dxhat = dhv * (g * one_sc)
        proj = jnp.mean(dxhat * xhat, axis=-1, keepdims=True)
        dxv = dres_ref[...] + rstd * (dxhat - xhat * proj)
        dx_ref[...] = dxv
        dhx = dhv * xhat
        first = [(dg_ref, jnp.sum(dhx * one_sc, axis=0, keepdims=True))]
        per_seq = [(dsc_ref, jnp.sum(dhx * g, axis=0, keepdims=True)), (dsh_ref, jnp.sum(dhv, axis=0, keepdims=True))]
        if gated:
            y_ref, gate_ref, dy_ref, dgate_ref = refs[5], refs[6], refs[-2], refs[-1]
            dy_ref[...] = (dxv * gate_ref[...]).astype(BF)
            per_seq.append((dgate_ref, jnp.sum(dxv * y_ref[...].astype(F32), axis=0, keepdims=True)))
        for cond_new, cond_add, group in ((i == 0, i > 0, first),
                                          (i % steps_per_seq == 0, i % steps_per_seq != 0, per_seq)):
            @pl.when(cond_new)
            def _(group=group):
                for ref, part in group:
                    ref[...] = part

            @pl.when(cond_add)
            def _(group=group):
                for ref, part in group:
                    ref[...] += part

    row = pl.BlockSpec((tr, w), lambda i: (i, 0))
    per_b = pl.BlockSpec((None, 1, w), lambda i: ((i * tr) // seq, 0, 0))
    vec = pl.BlockSpec((1, w), lambda i: (0, 0))
    out_shape = [jax.ShapeDtypeStruct((t, w), F32), jax.ShapeDtypeStruct((1, w), F32),
                 jax.ShapeDtypeStruct((nb, 1, w), F32), jax.ShapeDtypeStruct((nb, 1, w), F32)]
    in_specs, out_specs, operands = [row, row, vec, per_b, row], [row, vec, per_b, per_b], [dh, x, gain, scale, dres]
    if gated:
        in_specs += [row, per_b]
        operands += list(gate)
        out_shape += [jax.ShapeDtypeStruct((t, w), BF), jax.ShapeDtypeStruct((nb, 1, w), F32)]
        out_specs += [row, per_b]
    return pl.pallas_call(
        body, name=name, grid=(t // tr,),
        out_shape=out_shape, in_specs=in_specs, out_specs=out_specs,
        compiler_params=_params(("arbitrary",)),
    )(*operands)


def _gate_bwd(dx, y, gate, seq, name):
    t, w = dx.shape
    tr = ROW_BLOCK
    steps_per_seq = seq // tr
    nb = t // seq

    def body(dx_ref, y_ref, g_ref, dy_ref, dg_ref):
        i = pl.program_id(0)
        dxv = dx_ref[...]
        dy_ref[...] = (dxv * g_ref[...]).astype(BF)
        part = jnp.sum(dxv * y_ref[...].astype(F32), axis=0, keepdims=True)

        @pl.when(i % steps_per_seq == 0)
        def _():
            dg_ref[...] = part

        @pl.when(i % steps_per_seq != 0)
        def _():
            dg_ref[...] += part

    row = pl.BlockSpec((tr, w), lambda i: (i, 0))
    per_b = pl.BlockSpec((None, 1, w), lambda i: ((i * tr) // seq, 0, 0))
    return pl.pallas_call(
        body, name=name, grid=(t // tr,),
        out_shape=(jax.ShapeDtypeStruct((t, w), BF), jax.ShapeDtypeStruct((nb, 1, w), F32)),
        in_specs=[row, row, per_b], out_specs=(row, per_b),
        compiler_params=_params(("arbitrary",)),
    )(dx, y, gate)


def _loss_head(x, gain, target, name):
    t, w = x.shape
    tr = ROW_BLOCK

    def body(x_ref, g_ref, t_ref, loss_ref, dx_ref, dg_ref):
        i = pl.program_id(0)
        xv = x_ref[...]
        g = g_ref[...]
        rstd = lax.rsqrt(jnp.mean(xv * xv, axis=-1, keepdims=True) + NORM_EPS)
        xhat = xv * rstd
        err = xhat * g - t_ref[...]
        row_loss = jnp.sum(err * err, axis=-1, keepdims=True) * (0.5 / w)
        loss_part = jnp.broadcast_to(jnp.sum(row_loss, axis=0, keepdims=True), (1, LANES))
        dy = err * (1.0 / w)
        dg_part = jnp.sum(dy * xhat, axis=0, keepdims=True)
        dxhat = dy * g
        proj = jnp.mean(dxhat * xhat, axis=-1, keepdims=True)
        dx_ref[...] = rstd * (dxhat - xhat * proj)

        @pl.when(i == 0)
        def _():
            loss_ref[...] = loss_part
            dg_ref[...] = dg_part

        @pl.when(i > 0)
        def _():
            loss_ref[...] += loss_part
            dg_ref[...] += dg_part

    row = pl.BlockSpec((tr, w), lambda i: (i, 0))
    vec = pl.BlockSpec((1, w), lambda i: (0, 0))
    return pl.pallas_call(
        body, name=name, grid=(t // tr,),
        out_shape=(jax.ShapeDtypeStruct((1, LANES), F32), jax.ShapeDtypeStruct((t, w), F32),
                   jax.ShapeDtypeStruct((1, w), F32)),
        in_specs=[row, vec, row],
        out_specs=(pl.BlockSpec((1, LANES), lambda i: (0, 0)), row, vec),
        compiler_params=_params(("arbitrary",)),
    )(x, gain, target)


def _rope_group(xg, cos_p, sin_a, sin_b):
    return (xg * cos_p + pltpu.roll(xg, LANES - ROPE_HALF, axis=1) * sin_a
            + pltpu.roll(xg, ROPE_HALF, axis=1) * sin_b)


def _rope(x, tables, name, out_dtype=BF):
    t, w = x.shape
    tr = ROW_BLOCK
    groups = w // LANES

    def body(x_ref, c_ref, a_ref, b_ref, out_ref):
        cos_p, sin_a, sin_b = c_ref[...], a_ref[...], b_ref[...]
        for g in range(groups):
            sl = slice(g * LANES, (g + 1) * LANES)
            out_ref[:, sl] = _rope_group(x_ref[:, sl].astype(F32), cos_p, sin_a, sin_b).astype(out_dtype)

    row = pl.BlockSpec((tr, w), lambda i: (i, 0))
    tab = pl.BlockSpec((tr, LANES), lambda i: (i, 0))
    return pl.pallas_call(
        body, name=name, grid=(t // tr,),
        out_shape=jax.ShapeDtypeStruct((t, w), out_dtype),
        in_specs=[row, tab, tab, tab], out_specs=row,
        compiler_params=_params(("parallel",)),
    )(x, *tables)


def _mla_mid(down, gq, gkv, tables, name):
    t = down.shape[0]
    tr = ROW_BLOCK

    def body(d_ref, gq_ref, gkv_ref, c_ref, a_ref, b_ref, cq_ref, ckr_ref):
        q = d_ref[:, 0:256]
        cq_ref[...] = (q * lax.rsqrt(jnp.mean(q * q, axis=-1, keepdims=True) + NORM_EPS) * gq_ref[...]).astype(BF)
        kv = d_ref[:, 256:384]
        ckr_ref[:, 0:128] = (kv * lax.rsqrt(jnp.mean(kv * kv, axis=-1, keepdims=True) + NORM_EPS)
                             * gkv_ref[...]).astype(BF)
        ckr_ref[:, 128:256] = _rope_group(d_ref[:, 384:512], c_ref[...], a_ref[...], b_ref[...]).astype(BF)

    tab = pl.BlockSpec((tr, LANES), lambda i: (i, 0))
    return pl.pallas_call(
        body, name=name, grid=(t // tr,),
        out_shape=(jax.ShapeDtypeStruct((t, 256), BF), jax.ShapeDtypeStruct((t, 256), BF)),
        in_specs=[pl.BlockSpec((tr, 512), lambda i: (i, 0)), pl.BlockSpec((1, 256), lambda i: (0, 0)),
                  pl.BlockSpec((1, 128), lambda i: (0, 0)), tab, tab, tab],
        out_specs=(pl.BlockSpec((tr, 256), lambda i: (i, 0)), pl.BlockSpec((tr, 256), lambda i: (i, 0))),
        compiler_params=_params(("parallel",)),
    )(down, gq, gkv, *tables)


def _mla_mid_bwd(down, dcq, dckr, gq, gkv, tables_t, name):
    t = down.shape[0]
    tr = ROW_BLOCK

    def norm_bwd(xv, g, dy):
        rstd = lax.rsqrt(jnp.mean(xv * xv, axis=-1, keepdims=True) + NORM_EPS)
        xhat = xv * rstd
        dxhat = dy * g
        proj = jnp.mean(dxhat * xhat, axis=-1, keepdims=True)
        return rstd * (dxhat - xhat * proj), jnp.sum(dy * xhat, axis=0, keepdims=True)

    def body(d_ref, dcq_ref, dckr_ref, gq_ref, gkv_ref, c_ref, a_ref, b_ref, dd_ref, dgq_ref, dgkv_ref):
        i = pl.program_id(0)
        dq, dgq_part = norm_bwd(d_ref[:, 0:256], gq_ref[...], dcq_ref[...].astype(F32))
        dd_ref[:, 0:256] = dq.astype(BF)
        dkv, dgkv_part = norm_bwd(d_ref[:, 256:384], gkv_ref[...], dckr_ref[:, 0:128].astype(F32))
        dd_ref[:, 256:384] = dkv.astype(BF)
        dd_ref[:, 384:512] = _rope_group(dckr_ref[:, 128:256].astype(F32), c_ref[...], a_ref[...],
                                         b_ref[...]).astype(BF)

        @pl.when(i == 0)
        def _():
            dgq_ref[...] = dgq_part
            dgkv_ref[...] = dgkv_part

        @pl.when(i > 0)
        def _():
            dgq_ref[...] += dgq_part
            dgkv_ref[...] += dgkv_part

    tab = pl.BlockSpec((tr, LANES), lambda i: (i, 0))
    r256 = pl.BlockSpec((tr, 256), lambda i: (i, 0))
    return pl.pallas_call(
        body, name=name, grid=(t // tr,),
        out_shape=(jax.ShapeDtypeStruct((t, 512), BF), jax.ShapeDtypeStruct((1, 256), F32),
                   jax.ShapeDtypeStruct((1, 128), F32)),
        in_specs=[pl.BlockSpec((tr, 512), lambda i: (i, 0)), r256, r256, pl.BlockSpec((1, 256), lambda i: (0, 0)),
                  pl.BlockSpec((1, 128), lambda i: (0, 0)), tab, tab, tab],
        out_specs=(pl.BlockSpec((tr, 512), lambda i: (i, 0)), pl.BlockSpec((1, 256), lambda i: (0, 0)),
                   pl.BlockSpec((1, 128), lambda i: (0, 0))),
        compiler_params=_params(("arbitrary",)),
    )(down, dcq, dckr, gq, gkv, *tables_t)


def _scan_rows(x, reverse):
    s = x.shape[0]
    row = lax.broadcasted_iota(jnp.int32, x.shape, 0)
    step = 1
    while step < s:
        if reverse:
            x = x + jnp.where(row < s - step, pltpu.roll(x, s - step, axis=0), 0.0)
        else:
            x = x + jnp.where(row >= step, pltpu.roll(x, step, axis=0), 0.0)
        step *= 2
    return x


def _fox_gate(fg, b_f, seq, name):
    t = fg.shape[0]

    def body(fg_ref, b_ref, out_ref):
        z = fg_ref[...] + b_ref[...]
        log_f = jnp.minimum(z, 0.0) - jnp.log(1.0 + jnp.exp(-jnp.abs(z)))
        out_ref[...] = _scan_rows(log_f, reverse=False)

    blk = pl.BlockSpec((seq, LANES), lambda b: (b, 0))
    return pl.pallas_call(
        body, name=name, grid=(t // seq,),
        out_shape=jax.ShapeDtypeStruct((t, LANES), F32),
        in_specs=[blk, pl.BlockSpec((1, LANES), lambda b: (0, 0))], out_specs=blk,
        compiler_params=_params(("parallel",)),
    )(fg, b_f)


def _fox_gate_bwd(d_cum, fg, b_f, seq, name):
    t = fg.shape[0]

    def body(dc_ref, fg_ref, b_ref, dfg_ref, db_ref):
        b = pl.program_id(0)
        z = fg_ref[...] + b_ref[...]
        d_log_f = _scan_rows(dc_ref[...], reverse=True)
        dz = d_log_f / (1.0 + jnp.exp(z))
        dfg_ref[...] = dz
        part = jnp.sum(dz, axis=0, keepdims=True)

        @pl.when(b == 0)
        def _():
            db_ref[...] = part

        @pl.when(b > 0)
        def _():
            db_ref[...] += part

    blk = pl.BlockSpec((seq, LANES), lambda b: (b, 0))
    vec = pl.BlockSpec((1, LANES), lambda b: (0, 0))
    return pl.pallas_call(
        body, name=name, grid=(t // seq,),
        out_shape=(jax.ShapeDtypeStruct((t, LANES), F32), jax.ShapeDtypeStruct((1, LANES), F32)),
        in_specs=[blk, blk, vec], out_specs=(blk, vec),
        compiler_params=_params(("arbitrary",)),
    )(d_cum, fg, b_f)


def _head_masks():
    lane = lax.broadcasted_iota(jnp.int32, (1, LANES), 1)
    return lane < HEAD_DIM, lane >= HEAD_DIM


def _pair_operands(ref, r0, n, compact, masks, masked):
    if not compact:
        return [ref[pl.ds(r0, n), h * LANES:(h + 1) * LANES] for h in range(2)]
    pair = ref[pl.ds(r0, n), :]
    return [jnp.where(mk, pair, jnp.zeros_like(pair)) for mk in masks] if masked else [pair, pair]


def _causal(n_rows, n_cols, shift):
    return (lax.broadcasted_iota(jnp.int32, (n_rows, n_cols), 1)
            <= lax.broadcasted_iota(jnp.int32, (n_rows, n_cols), 0) + shift)


def _attn_fwd(q_arr, q_off, k_arr, k_off, v_arr, v_off, bias, seq, name, ride=None, compact=False):
    t = q_arr.shape[0]
    nb = t // seq
    blk = min(ATTN_BLOCK, seq)
    nq = seq // blk
    qw = LANES if compact else 2 * LANES
    has_bias = bias is not None
    n_in, n_out = (4, 3) if has_bias else (3, 2)

    def body(*refs):
        step = pl.program_id(0) * HEAD_PAIRS + pl.program_id(1)
        refs, ride_end = _ride_steps(ride, refs, n_in, n_out, step, nb * HEAD_PAIRS)
        if has_bias:
            q_ref, k_ref, v_ref, bias_ref, o_ref, lse_ref, o32_ref = refs
        else:
            q_ref, k_ref, v_ref, o_ref, lse_ref = refs
        masks = _head_masks()
        lo = masks[0]

        def update(r0, n, carry, k0, nk, mask):
            qs = _pair_operands(q_ref, r0, n, compact, masks, True)
            ks = _pair_operands(k_ref, k0, nk, compact, masks, False)
            vv = v_ref[pl.ds(k0, nk), :]
            vs = [jnp.where(mk, vv, jnp.zeros_like(vv)) for mk in masks]
            new, alphas, pv = [], [], None
            for h in range(2):
                m, l = carry[1 + 2 * h], carry[2 + 2 * h]
                s = lax.dot_general(qs[h], ks[h], _DIMS["nt"], preferred_element_type=F32)
                if has_bias:
                    s = s + bias_ref[h, 0:1, pl.ds(k0, nk)]
                if mask is not None:
                    s = jnp.where(mask, s, -jnp.inf)
                m_new = jnp.maximum(m, jnp.max(s, axis=-1, keepdims=True))
                p = jnp.exp(s - m_new)
                alpha = jnp.exp(m - m_new)
                l_new = alpha * l + jnp.sum(p, axis=-1, keepdims=True)
                p_hi = p.astype(BF)
                d = jnp.dot(p_hi, vs[h], preferred_element_type=F32)
                if has_bias:
                    p_lo = (p - p_hi.astype(F32)).astype(BF)
                    d = d + jnp.dot(p_lo, vs[h], preferred_element_type=F32)
                pv = d if pv is None else pv + d
                alphas.append(alpha)
                new += [m_new, l_new]
            return (carry[0] * jnp.where(lo, alphas[0], alphas[1]) + pv, *new)

        def q_block(iq, _):
            q0 = pl.multiple_of(iq * blk, blk)
            init = (jnp.zeros((blk, LANES), F32),
                    jnp.full((blk, 1), -jnp.inf, F32), jnp.zeros((blk, 1), F32),
                    jnp.full((blk, 1), -jnp.inf, F32), jnp.zeros((blk, 1), F32))
            carry = lax.fori_loop(
                0, iq, lambda j, c: update(q0, blk, c, pl.multiple_of(j * blk, blk), blk, None), init)
            acc, m0, l0, m1, l1 = update(q0, blk, carry, q0, blk, _causal(blk, blk, 0))
            o_val = acc / jnp.where(lo, l0, l1)
            o_ref[pl.ds(q0, blk), :] = o_val.astype(BF)
            if has_bias:
                o32_ref[pl.ds(q0, blk), :] = o_val
            lse_ref[pl.ds(q0, blk), :] = jnp.where(lo, m0 + jnp.log(l0), m1 + jnp.log(l1))
            return 0

        lax.fori_loop(0, nq, q_block, 0)
        ride_end()

    in_specs = [pl.BlockSpec((seq, qw), lambda b, p: (b, q_off + p)),
                pl.BlockSpec((seq, qw), lambda b, p: (b, k_off + p)),
                pl.BlockSpec((seq, LANES), lambda b, p: (b, v_off + p))]
    args = [q_arr, k_arr, v_arr]
    if has_bias:
        in_specs.append(pl.BlockSpec((None, 2, 8, seq), lambda b, p: (b, p, 0, 0)))
        args.append(bias)
    out_blk = pl.BlockSpec((seq, LANES), lambda b, p: (b, p))
    out_shape = [jax.ShapeDtypeStruct((t, HEAD_PAIRS * LANES), BF), jax.ShapeDtypeStruct((t, HEAD_PAIRS * LANES), F32)]
    if has_bias:
        out_shape.append(jax.ShapeDtypeStruct((t, HEAD_PAIRS * LANES), F32))
    out_specs, scratch = [out_blk] * len(out_shape), []
    if ride is not None:
        r_in, r_shape, r_out, scratch = _ride_specs(ride)
        in_specs, out_shape, out_specs = in_specs + r_in, out_shape + r_shape, out_specs + r_out
        args += list(ride[1])
    return pl.pallas_call(
        body, name=name, grid=(nb, HEAD_PAIRS),
        out_shape=out_shape, in_specs=in_specs, out_specs=out_specs, scratch_shapes=scratch,
        compiler_params=_params(("arbitrary", "arbitrary")),
    )(*args)


def _attn_bwd(q_arr, q_off, k_arr, k_off, v_arr, v_off, bias, o, do, lse, seq, name, ride=None, compact=False):
    t = q_arr.shape[0]
    nb = t // seq
    blk = min(ATTN_BLOCK, seq)
    half = blk // 2
    nq = seq // blk
    qw = LANES if compact else 2 * LANES
    has_bias = bias is not None
    n_in, n_out = (7, 4) if has_bias else (6, 3)

    def body(*refs):
        step = pl.program_id(0) * HEAD_PAIRS + pl.program_id(1)
        refs, ride_end = _ride_steps(ride, refs, n_in, n_out, step, nb * HEAD_PAIRS)
        if has_bias:
            (q_ref, k_ref, v_ref, bias_ref, o_ref, do_ref, lse_ref,
             dq_ref, dk_ref, dv_ref, dbias_ref, dq_acc, dsum) = refs
        else:
            (q_ref, k_ref, v_ref, o_ref, do_ref, lse_ref, dq_ref, dk_ref, dv_ref, dq_acc, dsum) = refs
        masks = _head_masks()
        lo, hi = masks
        dq_acc[...] = jnp.zeros_like(dq_acc)

        def prep(iq, _):
            q0 = pl.multiple_of(iq * blk, blk)
            prod = do_ref[pl.ds(q0, blk), :].astype(F32) * o_ref[pl.ds(q0, blk), :].astype(F32)
            d0 = jnp.sum(jnp.where(lo, prod, 0.0), axis=-1, keepdims=True)
            d1 = jnp.sum(jnp.where(hi, prod, 0.0), axis=-1, keepdims=True)
            dsum[pl.ds(q0, blk), :] = jnp.where(lo, d0, d1)
            return 0

        lax.fori_loop(0, nq, prep, 0)

        def tile(r0, n, k0, nk, mask):
            qs = _pair_operands(q_ref, r0, n, compact, masks, True)
            ks = _pair_operands(k_ref, k0, nk, compact, masks, True)
            vv = v_ref[pl.ds(k0, nk), :]
            vs = [jnp.where(mk, vv, jnp.zeros_like(vv)) for mk in masks]
            dov = do_ref[pl.ds(r0, n), :]
            dos = [jnp.where(mk, dov, jnp.zeros_like(dov)) for mk in masks] if compact else None
            lse_v = lse_ref[pl.ds(r0, n), :]
            dsum_v = dsum[pl.ds(r0, n), :]
            dv_c, dks, dbs = None, [], []
            for h in range(2):
                s = lax.dot_general(qs[h], ks[h], _DIMS["nt"], preferred_element_type=F32)
                if has_bias:
                    s = s + bias_ref[h, 0:1, pl.ds(k0, nk)]
                p = jnp.exp(s - lse_v[:, h * HEAD_DIM:h * HEAD_DIM + 1])
                if mask is not None:
                    p = jnp.where(mask, p, 0.0)
                dp = lax.dot_general(dov, vs[h], _DIMS["nt"], preferred_element_type=F32)
                ds = p * (dp - dsum_v[:, h * HEAD_DIM:h * HEAD_DIM + 1])
                ds_bf = ds.astype(BF)
                if compact:
                    dv_h = lax.dot_general(p.astype(BF), dos[h], _DIMS["tn"], preferred_element_type=F32)
                else:
                    dv_h = jnp.where(masks[h], lax.dot_general(p.astype(BF), dov, _DIMS["tn"],
                                                               preferred_element_type=F32), 0.0)
                dv_c = dv_h if dv_c is None else dv_c + dv_h
                dk_h = lax.dot_general(ds_bf, qs[h], _DIMS["tn"], preferred_element_type=F32)
                dq_h = jnp.dot(ds_bf, ks[h], preferred_element_type=F32)
                if compact:
                    dks = [dk_h] if h == 0 else [dks[0] + dk_h, jnp.zeros((8, LANES), F32)]
                    if h == 0:
                        dq_first = dq_h
                    else:
                        dq_acc[pl.ds(r0, n), :] += dq_first + dq_h
                else:
                    dks.append(dk_h)
                    dq_acc[pl.ds(r0, n), h * LANES:(h + 1) * LANES] += dq_h
                dbs.append(jnp.sum(ds, axis=0, keepdims=True) if has_bias else jnp.zeros((1, nk), F32))
            return (dv_c, dks[0], dks[1], dbs[0], dbs[1])

        def kv_block(j, _):
            k0 = pl.multiple_of(j * blk, blk)
            if compact:
                dv_a, dk_a, dummy, db0_a, db1_a = tile(pl.multiple_of(k0 + half, half), half, k0, blk,
                                                       _causal(half, blk, half))
                top = tile(k0, half, k0, half, _causal(half, half, 0))
                head = lambda acc, x: jnp.concatenate([acc[:half] + x, acc[half:]], axis=0)
                lead = lambda acc, x: jnp.concatenate([acc[:, :half] + x, acc[:, half:]], axis=1)
                carry = (head(dv_a, top[0]), head(dk_a, top[1]), dummy, lead(db0_a, top[3]), lead(db1_a, top[4]))
            else:
                carry = tile(k0, blk, k0, blk, _causal(blk, blk, 0))

            def q_block(iq, c):
                part = tile(pl.multiple_of(iq * blk, blk), blk, k0, blk, None)
                return tuple(a + b for a, b in zip(c, part))

            carry = lax.fori_loop(j + 1, nq, q_block, carry)
            dv_ref[pl.ds(k0, blk), :] = carry[0].astype(BF)
            if compact:
                dk_ref[pl.ds(k0, blk), :] = carry[1].astype(BF)
            else:
                for h in range(2):
                    dk_ref[pl.ds(k0, blk), h * LANES:(h + 1) * LANES] = carry[1 + h].astype(BF)
            if has_bias:
                for h in range(2):
                    dbias_ref[h, :, pl.ds(k0, blk)] = jnp.broadcast_to(carry[3 + h], (8, blk))
            return 0

        lax.fori_loop(0, nq, kv_block, 0)
        dq_ref[...] = dq_acc[...].astype(BF)
        ride_end()

    pair256 = lambda off: pl.BlockSpec((seq, qw), lambda b, p: (b, off + p))
    pair128 = lambda off: pl.BlockSpec((seq, LANES), lambda b, p: (b, off + p))
    bias_spec = pl.BlockSpec((None, 2, 8, seq), lambda b, p: (b, p, 0, 0))
    in_specs = [pair256(q_off), pair256(k_off), pair128(v_off)]
    args = [q_arr, k_arr, v_arr]
    if has_bias:
        in_specs.append(bias_spec)
        args.append(bias)
    in_specs += [pair128(0), pair128(0), pair128(0)]
    args += [o, do, lse]
    out_shape = [jax.ShapeDtypeStruct((t, HEAD_PAIRS * qw), BF),
                 jax.ShapeDtypeStruct((t, HEAD_PAIRS * qw), BF),
                 jax.ShapeDtypeStruct((t, HEAD_PAIRS * LANES), BF)]
    out_specs = [pair256(0), pair256(0), pair128(0)]
    if has_bias:
        out_shape.append(jax.ShapeDtypeStruct((nb, HEADS, 8, seq), F32))
        out_specs.append(bias_spec)
    scratch = [pltpu.VMEM((seq, qw), F32), pltpu.VMEM((seq, LANES), F32)]
    if ride is not None:
        r_in, r_shape, r_out, r_scratch = _ride_specs(ride)
        in_specs, out_shape, out_specs = in_specs + r_in, out_shape + r_shape, out_specs + r_out
        args += list(ride[1])
        scratch += r_scratch
    return pl.pallas_call(
        body, name=name, grid=(nb, HEAD_PAIRS),
        out_shape=out_shape, in_specs=in_specs, out_specs=out_specs, scratch_shapes=scratch,
        compiler_params=_params(("arbitrary", "arbitrary")),
    )(*args)


def _adamw(w, g, m, v, name):
    shape = w.shape
    last = shape[-1]
    rows = int(np.prod(shape[:-1])) if len(shape) > 1 else 1
    tr = _rows(rows, 512)

    def body(w_ref, g_ref, m_ref, v_ref, d_ref, nm_ref, nv_ref):
        d_ref[...], nm_ref[...], nv_ref[...] = _adamw_math(w_ref[...], g_ref[...], m_ref[...], v_ref[...])

    blk = pl.BlockSpec((tr, last), lambda i: (i, 0))
    sds = jax.ShapeDtypeStruct((rows, last), F32)
    outs = pl.pallas_call(
        body, name=name, grid=(rows // tr,),
        out_shape=(sds, sds, sds), in_specs=[blk] * 4, out_specs=(blk,) * 3,
        compiler_params=_params(("parallel",)),
    )(*[a.reshape(rows, last) for a in (w, g, m, v)])
    return tuple(a.reshape(shape) for a in outs)


LOW_COLS = 256


def _low_pad(a):
    return jnp.pad(a, ((0, 0),) * (a.ndim - 1) + ((0, LOW_COLS - a.shape[-1]),))


def _layer_shards(w, i):
    j = i // 2
    bf = lambda a: a.astype(BF)
    if i % 2 == 0:
        mixer = [bf(w["fox_w_in"][j]), bf(w["fox_w_out"][j])]
    else:
        mixer = [jnp.concatenate([bf(w["mla_w_dq"][j]), bf(w["mla_w_ukv"][j]), _low_pad(bf(w["mla_w_uq"][j])),
                                  _low_pad(bf(w["mla_w_dkv"][j]))], axis=0), bf(w["mla_w_out"][j])]
    return mixer + [bf(w["mlp_w1"][i]), bf(w["mlp_w2"][i])]


def _side_by_side(stack, r0, rows, cols=None):
    return jnp.concatenate([stack[dd, r0:r0 + rows, :cols] for dd in range(N_DEV)], axis=1)


def _stacked(stack, r0, rows, cols=None):
    part = stack[:, r0:r0 + rows, :cols]
    return part.reshape(N_DEV * rows, part.shape[2])


def _layer_mixer_weights(i, first, out_all):
    full = dict(w_out=_stacked(out_all, 0, 128))
    if i % 2 == 0:
        full["fox_w_in"] = _side_by_side(first, 0, 1024)
    else:
        full.update(mla_w_dq=_stacked(first, 0, 128), mla_w_ukv=_side_by_side(first, 128, 128),
                    mla_w_uq=_side_by_side(first, 256, 256, 192), mla_w_dkv=_stacked(first, 512, 128, 160))
    return full


def _by_dest_rows(g):
    return g.reshape(N_DEV, g.shape[0] // N_DEV, g.shape[1]).astype(BF)


def _by_dest_cols(g):
    n = g.shape[1] // N_DEV
    return jnp.stack([g[:, dd * n:(dd + 1) * n] for dd in range(N_DEV)]).astype(BF)


def _layer_mixer_grad_bufs(i, g):
    if i % 2 == 0:
        return [_by_dest_cols(g["fox_w_in"]), _by_dest_rows(g["w_out"])]
    low = jnp.concatenate([_by_dest_rows(g["mla_w_dq"]), _by_dest_cols(g["mla_w_ukv"]),
                           _low_pad(_by_dest_cols(g["mla_w_uq"])), _low_pad(_by_dest_rows(g["mla_w_dkv"]))], axis=1)
    return [low, _by_dest_rows(g["w_out"])]


def _low_shard_grads(low):
    return dict(mla_w_dq=low[:128], mla_w_ukv=low[128:256], mla_w_uq=low[256:512, :192], mla_w_dkv=low[512:, :160])


def _pad_heads(w, width):
    k = w.shape[0]
    return jnp.pad(w.reshape(k, HEADS, width), ((0, 0), (0, 0), (0, LANES - width))).reshape(k, HEADS * LANES)


def _unpad_heads(w, width):
    k = w.shape[0]
    return w.reshape(k, HEADS, LANES)[:, :, :width].reshape(k, HEADS * width)


def _rope_tables(positions, scale):
    inv_freq = 10000.0 ** (-jnp.arange(0, 2 * ROPE_HALF, 2, dtype=F32) / (2 * ROPE_HALF))
    ang = positions.astype(F32)[:, None] * inv_freq
    cos, sin = jnp.cos(ang) * scale, jnp.sin(ang) * scale
    t = positions.shape[0]
    z = lambda n: jnp.zeros((t, n), F32)
    cos_p = jnp.concatenate([jnp.full((t, HEAD_DIM), scale, F32), cos, cos, z(32)], axis=1)
    sin_a = jnp.concatenate([z(64), -sin, z(48)], axis=1)
    sin_b = jnp.concatenate([z(80), sin, z(32)], axis=1)
    fwd = (cos_p, sin_a, sin_b)
    bwd = (cos_p, jnp.roll(sin_b, -ROPE_HALF, axis=1), jnp.roll(sin_a, ROPE_HALF, axis=1))
    return fwd, bwd


def _key_rows(cum, nb, seq):
    v = -cum.reshape(nb, seq, LANES)[:, :, :HEADS]
    return jnp.broadcast_to(jnp.transpose(v, (0, 2, 1))[:, :, None, :], (nb, HEADS, 8, seq))


def kernel(x, c, positions, ada_w, ada_b, norm_mix_g, norm_mlp_g, fox_w_in, fox_b_f, fox_w_out, mla_w_dq, mla_q_norm_g, mla_w_uq, mla_w_dkv, mla_kv_norm_g, mla_w_ukv, mla_w_out, mlp_w1, mlp_w2, final_norm_g, loss_target, m_ada_w, m_ada_b, m_norm_mix_g, m_norm_mlp_g, m_fox_w_in, m_fox_b_f, m_fox_w_out, m_mla_w_dq, m_mla_q_norm_g, m_mla_w_uq, m_mla_w_dkv, m_mla_kv_norm_g, m_mla_w_ukv, m_mla_w_out, m_mlp_w1, m_mlp_w2, m_final_norm_g, v_ada_w, v_ada_b, v_norm_mix_g, v_norm_mlp_g, v_fox_w_in, v_fox_b_f, v_fox_w_out, v_mla_w_dq, v_mla_q_norm_g, v_mla_w_uq, v_mla_w_dkv, v_mla_kv_norm_g, v_mla_w_ukv, v_mla_w_out, v_mlp_w1, v_mlp_w2, v_final_norm_g):
    args = dict(locals())
    weights = {n: args[n] for n in WEIGHTS}
    nb, seq, d = x.shape
    t = nb * seq
    depth = ada_w.shape[0]
    dev = 4 * lax.axis_index("x") + 2 * lax.axis_index("y") + lax.axis_index("c")
    n_mod_local = ada_w.shape[2]

    n_qg = mla_q_norm_g.shape[1]
    cond = jnp.concatenate([c, jnp.pad(mla_q_norm_g.reshape(1, -1), ((0, 7), (0, d - 2 * n_qg)))], axis=0)
    w1_rows, w2_rows = mlp_w1.shape[1], mlp_w2.shape[1]
    shards = [_layer_shards(weights, i) for i in range(depth)]
    stacks = [None] * depth
    *stacks[0], cond_all = _all_gather(shards[0][:2] + [cond], "gather_first")
    c_all = cond_all[:, :nb].reshape(N_DEV * nb, d)
    q_gain = jnp.transpose(cond_all[:, nb, :2 * n_qg].reshape(N_DEV, 2, n_qg), (1, 0, 2)).reshape(2, N_DEV * n_qg)
    mod_local = jnp.stack([
        _matmul(c_all, ada_w[i], mode="nn", name="ada_mod", out_dtype=F32, a_act="silu", epi="bias",
                extras=(lax.dynamic_slice_in_dim(ada_b[i], dev * n_mod_local, n_mod_local)[None, :],))
        for i in range(depth)])
    mod_all, = _all_gather([mod_local.reshape(depth * N_DEV * nb, n_mod_local)], "gather_mod")
    mod_all = jnp.transpose(mod_all.reshape(N_DEV, depth, N_DEV * nb, n_mod_local), (1, 2, 0, 3))
    mod_all = mod_all.reshape(depth, N_DEV * nb, N_DEV * n_mod_local)
    mod = lax.dynamic_slice_in_dim(mod_all, dev * nb, nb, axis=1)
    mod = mod.reshape(depth, nb, 6, 1, d)

    pos = positions.reshape(t)
    rope_q, rope_q_t = _rope_tables(pos, MLA_SCALE)
    rope_k, rope_k_t = _rope_tables(pos, 1.0)

    def fox_weights(full):
        w_in = full["fox_w_in"]
        w_qkv = jnp.concatenate([w_in[:, :d] * FOX_SCALE, w_in[:, d:3 * d]], axis=1)
        w_f = jnp.pad(w_in[:, 3 * d:], ((0, 0), (0, LANES - HEADS)))
        return w_qkv, w_f

    def mla_weights(full):
        w_dkv = full["mla_w_dkv"]
        w_down = jnp.concatenate([full["mla_w_dq"], w_dkv[:, :128], jnp.zeros((d, 64), BF),
                                  w_dkv[:, 128:160], jnp.zeros((d, 32), BF)], axis=1)
        w_uq = _pad_heads(full["mla_w_uq"], 96)
        w_ukv = full["mla_w_ukv"].reshape(128, HEADS, 2, HEAD_DIM)
        w_uk = jnp.pad(w_ukv[:, :, 0, :], ((0, 0), (0, 0), (0, 64))).reshape(128, HEADS * LANES)
        w_uv = w_ukv[:, :, 1, :].reshape(128, HEADS * HEAD_DIM)
        place = np.zeros((128, HEADS, LANES), np.float32)
        for i in range(2 * ROPE_HALF):
            place[64 + i, :, 64 + i] = 1.0
        bottom = jnp.concatenate([jnp.asarray(place.reshape(128, HEADS * LANES), BF),
                                  jnp.zeros((128, HEADS * HEAD_DIM), BF)], axis=1)
        w_kv = jnp.concatenate([jnp.concatenate([w_uk, w_uv], axis=1), bottom], axis=0)
        return w_down, w_uq, w_kv

    tm_big = min(2048, t)
    xs = x.reshape(t, d)
    saved = []
    for i in range(depth):
        j = i // 2
        sh_m, sc_m, g_m, sh_f, sc_f, g_f = (mod[i, :, q] for q in range(6))
        gain_mix = norm_mix_g[i][None, :]
        gain_mlp = norm_mlp_g[i][None, :]
        s = dict(x_in=xs)
        h = _norm_mod(xs, gain_mix, sc_m, sh_m, seq, "norm_mix")
        s["h"] = h
        full = _layer_mixer_weights(i, stacks[i][0], stacks[i][1])
        riders = (shards[0][2:] if i == 0 else []) + (shards[i + 1] if i + 1 < depth else [])
        ride = ("gather", riders) if riders else None
        if i % 2 == 0:
            w_qkv, w_f = fox_weights(full)
            qkv = _matmul(h, w_qkv, mode="nn", name="fox_qkv")
            fg = _matmul(h, w_f, mode="nn", name="fox_gate_logits", out_dtype=F32)
            b_f = jnp.pad(fox_b_f[j], (0, LANES - HEADS))[None, :]
            cum = _fox_gate(fg, b_f, seq, "fox_gate")
            bias = _key_rows(cum, nb, seq)
            o, lse, o32, *rode = _attn_fwd(qkv, 0, qkv, 8, qkv, 16, bias, seq, "fox_attn", ride, compact=True)
            s.update(qkv=qkv, fg=fg, b_f=b_f, bias=bias, w_qkv=w_qkv, w_f=w_f, o32=o32)
        else:
            w_down, w_uq, w_kv = mla_weights(full)
            down = _matmul(h, w_down, mode="nn", name="mla_down", out_dtype=F32)
            gq, gkv = q_gain[j][None, :], mla_kv_norm_g[j][None, :]
            cq, ckr = _mla_mid(down, gq, gkv, rope_k, "mla_mid")
            q_raw = _matmul(cq, w_uq, mode="nn", name="mla_uq", out_dtype=F32)
            q_rot = _rope(q_raw, rope_q, "mla_rope_q")
            kv = _matmul(ckr, w_kv, mode="nn", name="mla_ukv")
            o, lse, *rode = _attn_fwd(q_rot, 0, kv, 0, kv, 16, None, seq, "mla_attn", ride)
            s.update(down=down, gq=gq, gkv=gkv, cq=cq, ckr=ckr, q_rot=q_rot, kv=kv,
                     w_down=w_down, w_uq=w_uq, w_kv=w_kv)
        if i == 0:
            stacks[0], rode = stacks[0] + rode[:2], rode[2:]
        if i + 1 < depth:
            stacks[i + 1] = rode
        w_out = full["w_out"]
        xs, y = _matmul(o, w_out, mode="nn", name="attn_out", epi="resid_gate", extras=(xs, g_m), seq=seq)
        s.update(o=o, lse=lse, y=y, w_out=w_out, x_mid=xs)
        h2 = _norm_mod(xs, gain_mlp, sc_f, sh_f, seq, "norm_mlp")
        a_pre = _matmul(h2, stacks[i][2], mode="nn", name="mlp_up", layer=("col", 0, w1_rows), tm=tm_big)
        xs, y2 = _matmul(a_pre, stacks[i][3], mode="nn", name="mlp_down", layer=("row", 0, w2_rows), a_act="relu2",
                         epi="resid_gate", extras=(xs, g_f), seq=seq)
        s.update(h2=h2, a_pre=a_pre, y2=y2)
        saved.append(s)

    loss_part, dx, dg_final = _loss_head(xs, final_norm_g[None, :], loss_target.reshape(t, d), "loss_head")

    w1_cols, w1_tm = mlp_w1.shape[2], _pick(w1_rows, 1024)
    dg_mix, dg_mlp, db_f, dg_kv, dg_q = [None] * depth, [None] * depth, [None] * 2, [None] * 2, [None] * 2
    dmod = [None] * depth
    me = dev.astype(jnp.int32).reshape(1)
    chains = dict(fox_w_in=None, fox_w_out=None, mla_w_out=None, mlp_w1=None, mlp_w2=None)
    low_grads = [None] * 2

    def adam_step(name, layer):
        def land_one(buf, got):
            chains[name] = _total_adamw(buf, me, got, weights[name], args["m_" + name], args["v_" + name], layer,
                                        chains[name], "adamw_" + name)
        return land_one

    def low_step(lj):
        def land_one(buf, got):
            low_grads[lj] = _low_shard_grads(_add_parts(buf, me, got, "grads_total"))
        return land_one

    def mixer_steps(li):
        lj = li // 2
        return ([adam_step("fox_w_in", lj), adam_step("fox_w_out", lj)] if li % 2 == 0
                else [low_step(lj), adam_step("mla_w_out", lj)])

    def mlp_steps(li):
        return [adam_step("mlp_w1", li), adam_step("mlp_w2", li)]

    def land(staged, got):
        for (step, buf), g in zip(staged, got):
            step(buf, g)

    waiting = []
    dy2, dg_f = _gate_bwd(dx, saved[depth - 1]["y2"], mod[depth - 1, :, 5], seq, "gate_bwd")
    for i in reversed(range(depth)):
        j = i // 2
        s = saved[i]
        sh_m, sc_m, g_m, sh_f, sc_f, g_f = (mod[i, :, q] for q in range(6))
        da_pre = _matmul(dy2, stacks[i][3], mode="nt", name="mlp_down_dx", layer=("row", 0, w2_rows), epi="mul_drelu",
                         extras=(s["a_pre"],), tm=tm_big)
        g_w2 = _matmul(s["a_pre"], dy2, mode="tn", name="mlp_down_dw", a_act="relu2", tm=w2_rows,
                       into=(jax.ShapeDtypeStruct((N_DEV, w2_rows, d), BF), (None, w2_rows, d),
                             lambda r, j, k: (r, 0, 0)))
        dh2 = _matmul(da_pre, stacks[i][2], mode="nt", name="mlp_up_dx", layer=("col", 0, w1_rows), tm=tm_big)
        g_w1 = _matmul(s["h2"], da_pre, mode="tn", name="mlp_up_dw", tm=w1_tm, tn=w1_cols,
                       into=(jax.ShapeDtypeStruct((N_DEV, w1_rows, w1_cols), BF), (None, w1_tm, w1_cols),
                             lambda r, j, k: (j, r, 0)))
        waiting += list(zip(mlp_steps(i), [g_w1, g_w2]))
        dx, dg_mlp[i], dsc_f, dsh_f, dy, dg_m = _norm_mod_bwd(dh2, s["x_mid"], norm_mlp_g[i][None, :], sc_f, dx, seq,
                                                              "norm_bwd_gate", gate=(s["y"], g_m))
        do = _matmul(dy, s["w_out"], mode="nt", name="attn_out_dx")
        dw_out = _matmul(s["o"], dy, mode="tn", name="attn_out_dw", out_dtype=F32)
        ride = ("scatter", [e[1] for e in waiting]) if waiting else None
        g_mixer = dict(w_out=dw_out)
        if i % 2 == 0:
            qkv = s["qkv"]
            dq, dk, dv, dbias, *rode = _attn_bwd(qkv, 0, qkv, 8, qkv, 16, s["bias"], s["o32"], do, s["lse"], seq,
                                                 "fox_attn_bwd", ride, compact=True)
            dqkv = jnp.concatenate([dq, dk, dv], axis=1)
            d_cum = -jnp.transpose(dbias[:, :, 0, :], (0, 2, 1)).reshape(t, HEADS)
            d_cum = jnp.pad(d_cum, ((0, 0), (0, LANES - HEADS)))
            dfg, db = _fox_gate_bwd(d_cum, s["fg"], s["b_f"], seq, "fox_gate_bwd")
            db_f[j] = db
            dh = _matmul(dfg, s["w_f"], mode="nt", name="fox_gate_dx", out_dtype=F32)
            dh = _matmul(dqkv, s["w_qkv"], mode="nt", name="fox_qkv_dx", epi="add", extras=(dh,))
            dw_qkv = _matmul(s["h"], dqkv, mode="tn", name="fox_qkv_dw", out_dtype=F32)
            dw_f = _matmul(s["h"], dfg, mode="tn", name="fox_gate_dw", out_dtype=F32)
            g_mixer["fox_w_in"] = jnp.concatenate([dw_qkv[:, :d] * FOX_SCALE, dw_qkv[:, d:], dw_f[:, :HEADS]], axis=1)
        else:
            kv = s["kv"]
            dq, dk, dv, *rode = _attn_bwd(s["q_rot"], 0, kv, 0, kv, 16, None, s["o"], do, s["lse"], seq,
                                          "mla_attn_bwd", ride)
            dq_raw = _rope(dq, rope_q_t, "mla_rope_q_bwd")
            dcq = _matmul(dq_raw, s["w_uq"], mode="nt", name="mla_uq_dx")
            dw_uq = _matmul(s["cq"], dq_raw, mode="tn", name="mla_uq_dw", out_dtype=F32)
            dkv = jnp.concatenate([dk, dv], axis=1)
            dckr = _matmul(dkv, s["w_kv"], mode="nt", name="mla_ukv_dx")
            dw_kv = _matmul(s["ckr"], dkv, mode="tn", name="mla_ukv_dw", out_dtype=F32)
            d_down, dgq, dgkv = _mla_mid_bwd(s["down"], dcq, dckr, s["gq"], s["gkv"], rope_k_t, "mla_mid_bwd")
            dg_q[j], dg_kv[j] = dgq, dgkv
            dh = _matmul(d_down, s["w_down"], mode="nt", name="mla_down_dx")
            dw_down = _matmul(s["h"], d_down, mode="tn", name="mla_down_dw", out_dtype=F32)
            g_mixer["mla_w_dq"] = dw_down[:, :256]
            g_mixer["mla_w_dkv"] = jnp.concatenate([dw_down[:, 256:384], dw_down[:, 448:480]], axis=1)
            g_mixer["mla_w_uq"] = _unpad_heads(dw_uq, 96)
            dk_nope = dw_kv[:128, :HEADS * LANES].reshape(128, HEADS, LANES)[:, :, :HEAD_DIM]
            dv_w = dw_kv[:128, HEADS * LANES:].reshape(128, HEADS, HEAD_DIM)
            g_mixer["mla_w_ukv"] = jnp.concatenate([dk_nope, dv_w], axis=2).reshape(128, HEADS * LANES)
        land(waiting, rode)
        this_dg_f = dg_f
        if i > 0:
            dx, dg_mix[i], dsc_m, dsh_m, dy2, dg_f = _norm_mod_bwd(
                dh, s["x_in"], norm_mix_g[i][None, :], sc_m, dx, seq, "norm_bwd_gate",
                gate=(saved[i - 1]["y2"], mod[i - 1, :, 5]))
            waiting = list(zip(mixer_steps(i), _layer_mixer_grad_bufs(i, g_mixer)))
        else:
            dx, dg_mix[i], dsc_m, dsh_m = _norm_mod_bwd(dh, s["x_in"], norm_mix_g[i][None, :], sc_m, dx, seq, "norm_bwd")
            last = list(zip(mixer_steps(i), _layer_mixer_grad_bufs(i, g_mixer)))
            land(last, _scatter([e[1] for e in last], "grads_scatter"))
        dmod[i] = jnp.stack([dsh_m, dsc_m, dg_m, dsh_f, dsc_f, this_dg_f], axis=1).reshape(nb, 6 * d)

    grad_x = dx.reshape(nb, seq, d)
    done = {n: tuple(a.reshape(weights[n].shape) for a in chain) for n, chain in chains.items()}
    shard_grads = {n: jnp.stack([low_grads[0][n], low_grads[1][n]]) for n in low_grads[0]}

    dmod_arr = jnp.stack(dmod)
    wide = lambda a: jnp.pad(a, ((0, 0), (0, d - a.shape[1])))
    pieces = [wide(loss_part), *dg_mix, *dg_mlp, *[wide(a) for a in db_f], *[wide(a) for a in dg_kv], dg_final,
              *[wide(a) for a in dg_q], jnp.sum(dmod_arr, axis=1).reshape(depth * 6, d)]
    n_small = sum(p.shape[0] for p in pieces)
    both = jnp.concatenate(pieces + [dmod_arr.reshape(depth * nb * 6, d)], axis=0)
    both = jnp.pad(both, ((0, (-both.shape[0]) % 8), (0, 0)))
    both_all, = _all_gather([both], "gather_small")
    total = _sum_leading(both_all, "sum_small")
    off = 0

    def take(rows):
        nonlocal off
        out = total[off:off + rows]
        off += rows
        return out

    loss = take(1)[0, 0]
    g_small = dict(
        norm_mix_g=take(depth), norm_mlp_g=take(depth), fox_b_f=take(2)[:, :HEADS], mla_kv_norm_g=take(2)[:, :128],
        final_norm_g=take(1)[0],
        mla_q_norm_g=lax.dynamic_slice_in_dim(take(2)[:, :N_DEV * n_qg], dev * n_qg, n_qg, axis=1),
        ada_b=take(depth * 6).reshape(depth, 6 * d))
    dmod_all = both_all[:, n_small:n_small + depth * nb * 6]
    dmod_all = jnp.transpose(dmod_all.reshape(N_DEV, depth, nb, 6 * d), (1, 0, 2, 3)).reshape(depth, N_DEV * nb, 6 * d)
    dmod_cols = lax.dynamic_slice_in_dim(dmod_all, dev * n_mod_local, n_mod_local, axis=2)
    g_ada_w = jnp.stack([_matmul(c_all, dmod_cols[i], mode="tn", name="ada_dw", out_dtype=F32, a_act="silu")
                         for i in range(depth)])

    all_grads = dict(shard_grads)
    all_grads.update(g_small)
    all_grads["ada_w"] = g_ada_w

    deltas, new_m, new_v = {}, {}, {}
    for n in WEIGHTS:
        if n in done:
            all_grads[n], deltas[n], new_m[n], new_v[n] = done[n]
        else:
            deltas[n], new_m[n], new_v[n] = _adamw(weights[n], all_grads[n], args["m_" + n], args["v_" + n], "adamw")

    return (loss, grad_x, *[all_grads[n] for n in WEIGHTS], *[deltas[n] for n in WEIGHTS],
            *[new_m[n] for n in WEIGHTS], *[new_v[n] for n in WEIGHTS])
```

```python
import functools
import math

import jax
import jax.numpy as jnp
import numpy as np
from jax import lax
from jax.experimental import pallas as pl
from jax.experimental.pallas import tpu as pltpu

F32 = jnp.float32
BF = jnp.bfloat16

N_DEV = 8
HEADS = 16
HEAD_PAIRS = HEADS // 2
HEAD_DIM = 64
LANES = 128
ROPE_HALF = 16
NORM_EPS = 1e-6
MLA_SCALE = 96.0 ** -0.5
FOX_SCALE = 0.125
ATTN_BLOCK = 512
ROW_BLOCK = 512
K_SPAN = 4
VMEM_LIMIT = 56 * 1024 * 1024
MESH = pl.DeviceIdType.MESH

ADAM_LR = 0.001
ADAM_B1 = 0.9
ADAM_B2 = 0.999
ADAM_EPS = 1e-08
ADAM_WD = 0.01
ADAM_STEP = 10

WEIGHTS = ("ada_w", "ada_b", "norm_mix_g", "norm_mlp_g", "fox_w_in", "fox_b_f", "fox_w_out", "mla_w_dq",
           "mla_q_norm_g", "mla_w_uq", "mla_w_dkv", "mla_kv_norm_g", "mla_w_ukv", "mla_w_out", "mlp_w1",
           "mlp_w2", "final_norm_g")


def _params(sem=None):
    return pltpu.CompilerParams(dimension_semantics=sem, vmem_limit_bytes=VMEM_LIMIT)


def _pick(n, target):
    if n <= target:
        return n
    for t in range(target, 127, -128):
        if n % t == 0:
            return t
    return n


def _rows(n, target=512):
    if n <= target:
        return n
    for t in range(target, 7, -8):
        if n % t == 0:
            return t
    return n


def _place():
    x, y, c = lax.axis_index("x"), lax.axis_index("y"), lax.axis_index("c")
    return x, y, c


def _adamw_math(w, g, m, v):
    nm = ADAM_B1 * m + (1.0 - ADAM_B1) * g
    nv = ADAM_B2 * v + (1.0 - ADAM_B2) * (g * g)
    m_hat = nm * (1.0 / (1.0 - ADAM_B1 ** ADAM_STEP))
    v_hat = nv * (1.0 / (1.0 - ADAM_B2 ** ADAM_STEP))
    return -ADAM_LR * (m_hat / (jnp.sqrt(v_hat) + ADAM_EPS) + ADAM_WD * w), nm, nv


def _all_gather(blocks, name):
    ride = ("gather", blocks)

    def body(*refs):
        start, mid, finish = _ride_phases(ride, *_ride_split(ride, refs, 0, 0)[:3])
        start()
        mid()
        finish()

    in_specs, out_shape, out_specs, scratch = _ride_specs(ride)
    return pl.pallas_call(
        body, name=name, out_shape=out_shape, in_specs=in_specs, out_specs=out_specs, scratch_shapes=scratch,
    )(*blocks)


def _ride_specs(ride):
    kind, arrays = ride
    n = len(arrays)
    any_spec = pl.BlockSpec(memory_space=pl.ANY)
    if kind == "gather":
        out_shape = [jax.ShapeDtypeStruct((N_DEV,) + b.shape, b.dtype) for b in arrays]
        scratch = [pltpu.SemaphoreType.DMA((7 * n,)), pltpu.SemaphoreType.DMA((7 * n,)), pltpu.SemaphoreType.DMA((n,))]
    else:
        out_shape = [jax.ShapeDtypeStruct((N_DEV - 1,) + p.shape[1:], p.dtype) for p in arrays]
        scratch = [pltpu.SemaphoreType.DMA((7 * n,)), pltpu.SemaphoreType.DMA((7 * n,))]
    return [any_spec] * n, out_shape, [any_spec] * n, scratch


def _ride_split(ride, refs, n_in, n_out):
    n = len(ride[1])
    n_sem = 3 if ride[0] == "gather" else 2
    src = refs[n_in:n_in + n]
    dst = refs[n_in + n + n_out:n_in + 2 * n + n_out]
    own = refs[:n_in] + refs[n_in + n:n_in + n + n_out] + refs[n_in + 2 * n + n_out:len(refs) - n_sem]
    return src, dst, refs[len(refs) - n_sem:], own


def _ride_phases(ride, src, dst, sems):
    n = len(src)
    x, y, c = _place()
    chips = [(1 - x, y), (x, 1 - y), (1 - x, 1 - y)]
    if ride[0] == "scatter":
        send_sems, recv_sems = sems

        def copies():
            out = []
            for f in (1, 2, 3, 5, 6, 7, 4):
                px, py, pc = x ^ (f & 1), y ^ ((f >> 1) & 1), c ^ (f >> 2)
                out += [pltpu.make_async_remote_copy(
                    src_ref=src[a].at[4 * px + 2 * py + pc], dst_ref=dst[a].at[f - 1],
                    send_sem=send_sems.at[7 * a + f - 1], recv_sem=recv_sems.at[7 * a + f - 1],
                    device_id=(px, py, pc), device_id_type=MESH) for a in range(n)]
            return out

        def start():
            for cp in copies():
                cp.start()

        def finish():
            for cp in copies():
                cp.wait()

        return start, lambda: None, finish

    send_sems, recv_sems, local_sems = sems
    me, sibling = (x, y, c), (x, y, 1 - c)

    def slot(a, px, py, pc):
        return dst[a].at[4 * px + 2 * py + pc]

    def copy(a, k, blk, to, from_src=False):
        return pltpu.make_async_remote_copy(
            src_ref=src[a] if from_src else slot(a, *blk), dst_ref=slot(a, *blk),
            send_sem=send_sems.at[7 * a + k], recv_sem=recv_sems.at[7 * a + k], device_id=to, device_id_type=MESH)

    def mine():
        return [pltpu.make_async_copy(src[a], slot(a, *me), local_sems.at[a]) for a in range(n)]

    def first():
        out = []
        for j, chip in enumerate(chips):
            out += [copy(a, 1 + j, me, (*chip, c), from_src=True) for a in range(n)]
        return out + [copy(a, 0, me, sibling, from_src=True) for a in range(n)]

    def passed():
        return [copy(a, 4 + j, (*chip, c), sibling) for j, chip in enumerate(chips) for a in range(n)]

    def start():
        for cp in mine() + first():
            cp.start()

    def mid():
        for j, chip in enumerate(chips):
            for a in range(n):
                copy(a, 1 + j, (*chip, c), me).wait_recv()
        for cp in passed():
            cp.start()

    def finish():
        for a in range(n):
            copy(a, 0, sibling, me).wait_recv()
        for j, chip in enumerate(chips):
            for a in range(n):
                copy(a, 4 + j, (*chip, 1 - c), me).wait_recv()
        for cp in first() + passed():
            cp.wait_send()
        for cp in mine():
            cp.wait()

    return start, mid, finish


def _ride_steps(ride, refs, n_in, n_out, step, n_steps):
    if ride is None:
        return refs, lambda: None
    src, dst, sems, own = _ride_split(ride, refs, n_in, n_out)
    start, mid, finish = _ride_phases(ride, src, dst, sems)
    pl.when(step == 0)(start)
    pl.when(step == (3 * n_steps) // 4)(mid)
    return own, lambda: pl.when(step == n_steps - 1)(finish)


def _scatter_and_gather(bufs, blocks, name):
    rides = (("scatter", bufs), ("gather", blocks))
    ns, ng = len(bufs), len(blocks)

    def body(*refs):
        s_src, g_src = refs[:ns], refs[ns:ns + ng]
        s_dst, g_dst = refs[ns + ng:2 * ns + ng], refs[2 * ns + ng:2 * (ns + ng)]
        sems = refs[2 * (ns + ng):]
        s_start, _, s_finish = _ride_phases(rides[0], s_src, s_dst, sems[:2])
        g_start, g_mid, g_finish = _ride_phases(rides[1], g_src, g_dst, sems[2:])
        s_start()
        g_start()
        g_mid()
        g_finish()
        s_finish()

    s_in, s_shape, s_out, s_scratch = _ride_specs(rides[0])
    g_in, g_shape, g_out, g_scratch = _ride_specs(rides[1])
    outs = pl.pallas_call(
        body, name=name, out_shape=s_shape + g_shape, in_specs=s_in + g_in, out_specs=s_out + g_out,
        scratch_shapes=s_scratch + g_scratch,
    )(*bufs, *blocks)
    return outs[:ns], outs[ns:]


def _total(own_ref, parts_ref):
    g = own_ref[...].astype(F32)
    for k in range(parts_ref.shape[0]):
        g = g + parts_ref[k].astype(F32)
    return g


def _total_adamw(buf, me, parts, w, m, v, layer, carry, name):
    _, r, cdim = buf.shape
    n_layers = w.shape[0]
    tr = _rows(r, 256)
    steps = r // tr

    def body(me_ref, own_ref, parts_ref, w_ref, m_ref, v_ref, *rest):
        del me_ref
        g_ref, d_ref, nm_ref, nv_ref = rest[-4:]
        g = _total(own_ref, parts_ref)
        g_ref[...] = g
        d_ref[...], nm_ref[...], nv_ref[...] = _adamw_math(w_ref[...], g, m_ref[...], v_ref[...])

    lay = pl.BlockSpec((tr, cdim), lambda i, me_ref: (layer * steps + i, 0))
    in_specs = [pl.BlockSpec((None, tr, cdim), lambda i, me_ref: (me_ref[0], i, 0)),
                pl.BlockSpec((N_DEV - 1, tr, cdim), lambda i, me_ref: (0, i, 0)), lay, lay, lay]
    operands = [me, buf, parts, *[a.reshape(n_layers * r, cdim) for a in (w, m, v)]]
    aliases = {}
    if carry is not None:
        in_specs += [pl.BlockSpec(memory_space=pl.ANY)] * 4
        operands += list(carry)
        aliases = {6 + k: k for k in range(4)}
    sds = jax.ShapeDtypeStruct((n_layers * r, cdim), F32)
    return pl.pallas_call(
        body, name=name,
        grid_spec=pltpu.PrefetchScalarGridSpec(num_scalar_prefetch=1, grid=(steps,), in_specs=in_specs,
                                               out_specs=(lay,) * 4),
        out_shape=(sds,) * 4, input_output_aliases=aliases,
        compiler_params=_params(("parallel",)),
    )(*operands)


def _add_parts(buf, me, parts, name):
    _, r, cdim = buf.shape
    tr = _rows(r, 512)

    def body(me_ref, own_ref, parts_ref, out_ref):
        del me_ref
        out_ref[...] = _total(own_ref, parts_ref)

    return pl.pallas_call(
        body, name=name,
        grid_spec=pltpu.PrefetchScalarGridSpec(
            num_scalar_prefetch=1, grid=(r // tr,),
            in_specs=[pl.BlockSpec((None, tr, cdim), lambda i, me_ref: (me_ref[0], i, 0)),
                      pl.BlockSpec((N_DEV - 1, tr, cdim), lambda i, me_ref: (0, i, 0))],
            out_specs=pl.BlockSpec((tr, cdim), lambda i, me_ref: (i, 0))),
        out_shape=jax.ShapeDtypeStruct((r, cdim), F32),
        compiler_params=_params(("parallel",)),
    )(me, buf, parts)


def _sum_leading(stack, name):
    n, r, cdim = stack.shape
    tr = _rows(r, 512)

    def body(in_ref, out_ref):
        acc = in_ref[0]
        for k in range(1, n):
            acc = acc + in_ref[k]
        out_ref[...] = acc

    return pl.pallas_call(
        body, name=name, grid=(r // tr,),
        out_shape=jax.ShapeDtypeStruct((r, cdim), F32),
        in_specs=[pl.BlockSpec((n, tr, cdim), lambda i: (0, i, 0))],
        out_specs=pl.BlockSpec((tr, cdim), lambda i: (i, 0)),
        compiler_params=_params(("parallel",)),
    )(stack)


_DIMS = {"nn": (((1,), (0,)), ((), ())), "nt": (((1,), (1,)), ((), ())), "tn": (((0,), (0,)), ((), ()))}


def _stack_spec(shape, mode, layer):
    cut, l, rows = layer
    cols = shape[2]
    by_n = pl.BlockSpec((1, rows, cols), lambda i, j, k: (j, l, 0))
    by_k = pl.BlockSpec((K_SPAN, rows, cols), lambda i, j, k: (k, l, 0))
    if cut == "col":
        return (by_n, N_DEV * cols, cols, rows) if mode == "nn" else (by_k, rows, rows, K_SPAN * cols)
    return (by_k, cols, cols, K_SPAN * rows) if mode == "nn" else (by_n, N_DEV * rows, rows, cols)


def _matmul(a, b, *, mode, name, out_dtype=BF, a_act=None, epi=None, extras=(), seq=None, layer=None, tm=None,
            tn=None, into=None):
    if mode == "tn":
        kdim, m = a.shape
    else:
        m, kdim = a.shape
    if tm is None:
        tm = _pick(m, 1024 if epi != "resid_gate" else min(1024, seq))
    tk = _pick(kdim, 4096 if mode == "tn" else 1024)
    b_spec = None
    if layer is not None:
        b_spec, n, tn, tk = _stack_spec(b.shape, mode, layer)
    else:
        n = b.shape[0] if mode == "nt" else b.shape[1]
        tn = _pick(n, 1024) if tn is None else tn
    nk = kdim // tk
    a_spec = (pl.BlockSpec((tk, tm), lambda i, j, k: (k, i)) if mode == "tn"
              else pl.BlockSpec((tm, tk), lambda i, j, k: (i, k)))
    if b_spec is None:
        b_spec = (pl.BlockSpec((tn, tk), lambda i, j, k: (j, k)) if mode == "nt"
                  else pl.BlockSpec((tk, tn), lambda i, j, k: (k, j)))
    tile = pl.BlockSpec((tm, tn), lambda i, j, k: (i, j))
    in_specs, out_specs = [a_spec, b_spec], [tile]
    out_shape = [jax.ShapeDtypeStruct((m, n), out_dtype)]
    if epi == "resid_gate":
        in_specs += [tile, pl.BlockSpec((None, 1, tn), lambda i, j, k: ((i * tm) // seq, 0, j))]
        out_shape = [jax.ShapeDtypeStruct((m, n), F32), jax.ShapeDtypeStruct((m, n), BF)]
        out_specs = [tile, tile]
    elif epi in ("mul_drelu", "add"):
        in_specs += [tile]
    elif epi == "bias":
        in_specs += [pl.BlockSpec((1, tn), lambda i, j, k: (0, j))]
    n_extra, n_out = len(in_specs) - 2, len(out_specs)
    aliases, n_kept = {}, 0
    if into is not None:
        buffer, block, index_map = into
        out_dtype = buffer.dtype
        if not isinstance(buffer, jax.ShapeDtypeStruct):
            in_specs.append(pl.BlockSpec(memory_space=pl.ANY))
            extras = tuple(extras) + (buffer,)
            aliases, n_kept = {len(in_specs) - 1: 0}, 1
        out_shape = [jax.ShapeDtypeStruct(buffer.shape, buffer.dtype)]
        out_specs = [pl.BlockSpec(block, index_map)]
    dims = _DIMS[mode]

    def body(*refs):
        a_ref, b_ref = refs[:2]
        ex = refs[2:2 + n_extra]
        outs = refs[2 + n_extra + n_kept:2 + n_extra + n_kept + n_out]
        av = a_ref[...]
        if a_act == "relu2":
            t = jnp.maximum(av.astype(F32), 0.0)
            av = t * t
        elif a_act == "silu":
            t = av.astype(F32)
            av = t / (1.0 + jnp.exp(-t))
        av = av.astype(BF)
        if layer is None:
            part = lax.dot_general(av, b_ref[...].astype(BF), dims, preferred_element_type=F32)
        else:
            span = b_ref.shape[0]
            wk = av.shape[1] // span
            part = None
            for u in range(span):
                p_u = lax.dot_general(av[:, u * wk:(u + 1) * wk], b_ref[u], dims, preferred_element_type=F32)
                part = p_u if part is None else part + p_u

        def finish(acc):
            if epi == "resid_gate":
                outs[0][...] = ex[0][...] + ex[1][...] * acc
                outs[1][...] = acc.astype(BF)
            elif epi == "mul_drelu":
                outs[0][...] = (acc * (2.0 * jnp.maximum(ex[0][...].astype(F32), 0.0))).astype(out_dtype)
            elif epi == "add":
                outs[0][...] = (acc + ex[0][...].astype(F32)).astype(out_dtype)
            elif epi == "bias":
                outs[0][...] = (acc + ex[0][...]).astype(out_dtype)
            else:
                outs[0][...] = acc.astype(out_dtype)

        if nk == 1:
            finish(part)
        else:
            acc_ref = refs[-1]
            k = pl.program_id(2)

            @pl.when(k == 0)
            def _():
                acc_ref[...] = part

            @pl.when(k > 0)
            def _():
                acc_ref[...] += part

            @pl.when(k == nk - 1)
            def _():
                finish(acc_ref[...])

    res = pl.pallas_call(
        body, name=name, grid=(m // tm, n // tn, nk),
        out_shape=out_shape, in_specs=in_specs, out_specs=out_specs,
        scratch_shapes=[pltpu.VMEM((tm, tn), F32)] if nk > 1 else [],
        input_output_aliases=aliases,
        compiler_params=_params(("parallel", "parallel", "arbitrary")),
    )(a, b, *extras)
    return res if n_out > 1 else res[0]


def _norm_mod(x, gain, scale, shift, seq, name):
    t, w = x.shape
    tr = ROW_BLOCK

    def body(x_ref, g_ref, sc_ref, sh_ref, out_ref):
        xv = x_ref[...]
        rstd = lax.rsqrt(jnp.mean(xv * xv, axis=-1, keepdims=True) + NORM_EPS)
        y = xv * rstd * g_ref[...]
        out_ref[...] = (y * (1.0 + sc_ref[...]) + sh_ref[...]).astype(BF)

    per_b = pl.BlockSpec((None, 1, w), lambda i: ((i * tr) // seq, 0, 0))
    return pl.pallas_call(
        body, name=name, grid=(t // tr,),
        out_shape=jax.ShapeDtypeStruct((t, w), BF),
        in_specs=[pl.BlockSpec((tr, w), lambda i: (i, 0)), pl.BlockSpec((1, w), lambda i: (0, 0)), per_b, per_b],
        out_specs=pl.BlockSpec((tr, w), lambda i: (i, 0)),
        compiler_params=_params(("parallel",)),
    )(x, gain, scale, shift)


def _norm_mod_bwd(dh, x, gain, scale, dres, seq, name, gate=None):
    t, w = x.shape
    tr = ROW_BLOCK
    steps_per_seq = seq // tr
    nb = t // seq
    gated = gate is not None

    def body(*refs):
        dh_ref, x_ref, g_ref, sc_ref, dres_ref = refs[:5]
        dx_ref, dg_ref, dsc_ref, dsh_ref = refs[-6:-2] if gated else refs[-4:]
        i = pl.program_id(0)
        xv = x_ref[...]
        dhv = dh_ref[...].astype(F32)
        rstd = lax.rsqrt(jnp.mean(xv * xv, axis=-1, keepdims=True) + NORM_EPS)
        xhat = xv * rstd
        one_sc = 1.0 + sc_ref[...]
        g = g_ref[...]
        dxhat = dhv * (g * one_sc)
        proj = jnp.mean(dxhat * xhat, axis=-1, keepdims=True)
        dxv = dres_ref[...] + rstd * (dxhat - xhat * proj)
        dx_ref[...] = dxv
        dhx = dhv * xhat
        first = [(dg_ref, jnp.sum(dhx * one_sc, axis=0, keepdims=True))]
        per_seq = [(dsc_ref, jnp.sum(dhx * g, axis=0, keepdims=True)), (dsh_ref, jnp.sum(dhv, axis=0, keepdims=True))]
        if gated:
            y_ref, gate_ref, dy_ref, dgate_ref = refs[5], refs[6], refs[-2], refs[-1]
            dy_ref[...] = (dxv * gate_ref[...]).astype(BF)
            per_seq.append((dgate_ref, jnp.sum(dxv * y_ref[...].astype(F32), axis=0, keepdims=True)))
        for cond_new, cond_add, group in ((i == 0, i > 0, first),
                                          (i % steps_per_seq == 0, i % steps_per_seq != 0, per_seq)):
            @pl.when(cond_new)
            def _(group=group):
                for ref, part in group:
                    ref[...] = part

            @pl.when(cond_add)
            def _(group=group):
                for ref, part in group:
                    ref[...] += part

    row = pl.BlockSpec((tr, w), lambda i: (i, 0))
    per_b = pl.BlockSpec((None, 1, w), lambda i: ((i * tr) // seq, 0, 0))
    vec = pl.BlockSpec((1, w), lambda i: (0, 0))
    out_shape = [jax.ShapeDtypeStruct((t, w), F32), jax.ShapeDtypeStruct((1, w), F32),
                 jax.ShapeDtypeStruct((nb, 1, w), F32), jax.ShapeDtypeStruct((nb, 1, w), F32)]
    in_specs, out_specs, operands = [row, row, vec, per_b, row], [row, vec, per_b, per_b], [dh, x, gain, scale, dres]
    if gated:
        in_specs += [row, per_b]
        operands += list(gate)
        out_shape += [jax.ShapeDtypeStruct((t, w), BF), jax.ShapeDtypeStruct((nb, 1, w), F32)]
        out_specs += [row, per_b]
    return pl.pallas_call(
        body, name=name, grid=(t // tr,),
        out_shape=out_shape, in_specs=in_specs, out_specs=out_specs,
        compiler_params=_params(("arbitrary",)),
    )(*operands)


def _gate_bwd(dx, y, gate, seq, name):
    t, w = dx.shape
    tr = ROW_BLOCK
    steps_per_seq = seq // tr
    nb = t // seq

    def body(dx_ref, y_ref, g_ref, dy_ref, dg_ref):
        i = pl.program_id(0)
        dxv = dx_ref[...]
        dy_ref[...] = (dxv * g_ref[...]).astype(BF)
        part = jnp.sum(dxv * y_ref[...].astype(F32), axis=0, keepdims=True)

        @pl.when(i % steps_per_seq == 0)
        def _():
            dg_ref[...] = part

        @pl.when(i % steps_per_seq != 0)
        def _():
            dg_ref[...] += part

    row = pl.BlockSpec((tr, w), lambda i: (i, 0))
    per_b = pl.BlockSpec((None, 1, w), lambda i: ((i * tr) // seq, 0, 0))
    return pl.pallas_call(
        body, name=name, grid=(t // tr,),
        out_shape=(jax.ShapeDtypeStruct((t, w), BF), jax.ShapeDtypeStruct((nb, 1, w), F32)),
        in_specs=[row, row, per_b], out_specs=(row, per_b),
        compiler_params=_params(("arbitrary",)),
    )(dx, y, gate)


def _loss_head(x, gain, target, name):
    t, w = x.shape
    tr = ROW_BLOCK

    def body(x_ref, g_ref, t_ref, loss_ref, dx_ref, dg_ref):
        i = pl.program_id(0)
        xv = x_ref[...]
        g = g_ref[...]
        rstd = lax.rsqrt(jnp.mean(xv * xv, axis=-1, keepdims=True) + NORM_EPS)
        xhat = xv * rstd
        err = xhat * g - t_ref[...]
        row_loss = jnp.sum(err * err, axis=-1, keepdims=True) * (0.5 / w)
        loss_part = jnp.broadcast_to(jnp.sum(row_loss, axis=0, keepdims=True), (1, LANES))
        dy = err * (1.0 / w)
        dg_part = jnp.sum(dy * xhat, axis=0, keepdims=True)
        dxhat = dy * g
        proj = jnp.mean(dxhat * xhat, axis=-1, keepdims=True)
        dx_ref[...] = rstd * (dxhat - xhat * proj)

        @pl.when(i == 0)
        def _():
            loss_ref[...] = loss_part
            dg_ref[...] = dg_part

        @pl.when(i > 0)
        def _():
            loss_ref[...] += loss_part
            dg_ref[...] += dg_part

    row = pl.BlockSpec((tr, w), lambda i: (i, 0))
    vec = pl.BlockSpec((1, w), lambda i: (0, 0))
    return pl.pallas_call(
        body, name=name, grid=(t // tr,),
        out_shape=(jax.ShapeDtypeStruct((1, LANES), F32), jax.ShapeDtypeStruct((t, w), F32),
                   jax.ShapeDtypeStruct((1, w), F32)),
        in_specs=[row, vec, row],
        out_specs=(pl.BlockSpec((1, LANES), lambda i: (0, 0)), row, vec),
        compiler_params=_params(("arbitrary",)),
    )(x, gain, target)


def _rope_group(xg, cos_p, sin_a, sin_b):
    return (xg * cos_p + pltpu.roll(xg, LANES - ROPE_HALF, axis=1) * sin_a
            + pltpu.roll(xg, ROPE_HALF, axis=1) * sin_b)


def _rope(x, tables, name, out_dtype=BF):
    t, w = x.shape
    tr = ROW_BLOCK
    groups = w // LANES

    def body(x_ref, c_ref, a_ref, b_ref, out_ref):
        cos_p, sin_a, sin_b = c_ref[...], a_ref[...], b_ref[...]
        for g in range(groups):
            sl = slice(g * LANES, (g + 1) * LANES)
            out_ref[:, sl] = _rope_group(x_ref[:, sl].astype(F32), cos_p, sin_a, sin_b).astype(out_dtype)

    row = pl.BlockSpec((tr, w), lambda i: (i, 0))
    tab = pl.BlockSpec((tr, LANES), lambda i: (i, 0))
    return pl.pallas_call(
        body, name=name, grid=(t // tr,),
        out_shape=jax.ShapeDtypeStruct((t, w), out_dtype),
        in_specs=[row, tab, tab, tab], out_specs=row,
        compiler_params=_params(("parallel",)),
    )(x, *tables)


def _mla_mid(down, gq, gkv, tables, name):
    t = down.shape[0]
    tr = ROW_BLOCK

    def body(d_ref, gq_ref, gkv_ref, c_ref, a_ref, b_ref, cq_ref, ckr_ref):
        q = d_ref[:, 0:256]
        cq_ref[...] = (q * lax.rsqrt(jnp.mean(q * q, axis=-1, keepdims=True) + NORM_EPS) * gq_ref[...]).astype(BF)
        kv = d_ref[:, 256:384]
        ckr_ref[:, 0:128] = (kv * lax.rsqrt(jnp.mean(kv * kv, axis=-1, keepdims=True) + NORM_EPS)
                             * gkv_ref[...]).astype(BF)
        ckr_ref[:, 128:256] = _rope_group(d_ref[:, 384:512], c_ref[...], a_ref[...], b_ref[...]).astype(BF)

    tab = pl.BlockSpec((tr, LANES), lambda i: (i, 0))
    return pl.pallas_call(
        body, name=name, grid=(t // tr,),
        out_shape=(jax.ShapeDtypeStruct((t, 256), BF), jax.ShapeDtypeStruct((t, 256), BF)),
        in_specs=[pl.BlockSpec((tr, 512), lambda i: (i, 0)), pl.BlockSpec((1, 256), lambda i: (0, 0)),
                  pl.BlockSpec((1, 128), lambda i: (0, 0)), tab, tab, tab],
        out_specs=(pl.BlockSpec((tr, 256), lambda i: (i, 0)), pl.BlockSpec((tr, 256), lambda i: (i, 0))),
        compiler_params=_params(("parallel",)),
    )(down, gq, gkv, *tables)


def _mla_mid_bwd(down, dcq, dckr, gq, gkv, tables_t, name):
    t = down.shape[0]
    tr = ROW_BLOCK

    def norm_bwd(xv, g, dy):
        rstd = lax.rsqrt(jnp.mean(xv * xv, axis=-1, keepdims=True) + NORM_EPS)
        xhat = xv * rstd
        dxhat = dy * g
        proj = jnp.mean(dxhat * xhat, axis=-1, keepdims=True)
        return rstd * (dxhat - xhat * proj), jnp.sum(dy * xhat, axis=0, keepdims=True)

    def body(d_ref, dcq_ref, dckr_ref, gq_ref, gkv_ref, c_ref, a_ref, b_ref, dd_ref, dgq_ref, dgkv_ref):
        i = pl.program_id(0)
        dq, dgq_part = norm_bwd(d_ref[:, 0:256], gq_ref[...], dcq_ref[...].astype(F32))
        dd_ref[:, 0:256] = dq.astype(BF)
        dkv, dgkv_part = norm_bwd(d_ref[:, 256:384], gkv_ref[...], dckr_ref[:, 0:128].astype(F32))
        dd_ref[:, 256:384] = dkv.astype(BF)
        dd_ref[:, 384:512] = _rope_group(dckr_ref[:, 128:256].astype(F32), c_ref[...], a_ref[...],
                                         b_ref[...]).astype(BF)

        @pl.when(i == 0)
        def _():
            dgq_ref[...] = dgq_part
            dgkv_ref[...] = dgkv_part

        @pl.when(i > 0)
        def _():
            dgq_ref[...] += dgq_part
            dgkv_ref[...] += dgkv_part

    tab = pl.BlockSpec((tr, LANES), lambda i: (i, 0))
    r256 = pl.BlockSpec((tr, 256), lambda i: (i, 0))
    return pl.pallas_call(
        body, name=name, grid=(t // tr,),
        out_shape=(jax.ShapeDtypeStruct((t, 512), BF), jax.ShapeDtypeStruct((1, 256), F32),
                   jax.ShapeDtypeStruct((1, 128), F32)),
        in_specs=[pl.BlockSpec((tr, 512), lambda i: (i, 0)), r256, r256, pl.BlockSpec((1, 256), lambda i: (0, 0)),
                  pl.BlockSpec((1, 128), lambda i: (0, 0)), tab, tab, tab],
        out_specs=(pl.BlockSpec((tr, 512), lambda i: (i, 0)), pl.BlockSpec((1, 256), lambda i: (0, 0)),
                   pl.BlockSpec((1, 128), lambda i: (0, 0))),
        compiler_params=_params(("arbitrary",)),
    )(down, dcq, dckr, gq, gkv, *tables_t)


def _scan_rows(x, reverse):
    s = x.shape[0]
    row = lax.broadcasted_iota(jnp.int32, x.shape, 0)
    step = 1
    while step < s:
        if reverse:
            x = x + jnp.where(row < s - step, pltpu.roll(x, s - step, axis=0), 0.0)
        else:
            x = x + jnp.where(row >= step, pltpu.roll(x, step, axis=0), 0.0)
        step *= 2
    return x


def _fox_gate(fg, b_f, seq, name):
    t = fg.shape[0]

    def body(fg_ref, b_ref, out_ref):
        z = fg_ref[...] + b_ref[...]
        log_f = jnp.minimum(z, 0.0) - jnp.log(1.0 + jnp.exp(-jnp.abs(z)))
        out_ref[...] = _scan_rows(log_f, reverse=False)

    blk = pl.BlockSpec((seq, LANES), lambda b: (b, 0))
    return pl.pallas_call(
        body, name=name, grid=(t // seq,),
        out_shape=jax.ShapeDtypeStruct((t, LANES), F32),
        in_specs=[blk, pl.BlockSpec((1, LANES), lambda b: (0, 0))], out_specs=blk,
        compiler_params=_params(("parallel",)),
    )(fg, b_f)


def _fox_gate_bwd(d_cum, fg, b_f, seq, name):
    t = fg.shape[0]

    def body(dc_ref, fg_ref, b_ref, dfg_ref, db_ref):
        b = pl.program_id(0)
        z = fg_ref[...] + b_ref[...]
        d_log_f = _scan_rows(dc_ref[...], reverse=True)
        dz = d_log_f / (1.0 + jnp.exp(z))
        dfg_ref[...] = dz
        part = jnp.sum(dz, axis=0, keepdims=True)

        @pl.when(b == 0)
        def _():
            db_ref[...] = part

        @pl.when(b > 0)
        def _():
            db_ref[...] += part

    blk = pl.BlockSpec((seq, LANES), lambda b: (b, 0))
    vec = pl.BlockSpec((1, LANES), lambda b: (0, 0))
    return pl.pallas_call(
        body, name=name, grid=(t // seq,),
        out_shape=(jax.ShapeDtypeStruct((t, LANES), F32), jax.ShapeDtypeStruct((1, LANES), F32)),
        in_specs=[blk, blk, vec], out_specs=(blk, vec),
        compiler_params=_params(("arbitrary",)),
    )(d_cum, fg, b_f)


def _head_masks():
    lane = lax.broadcasted_iota(jnp.int32, (1, LANES), 1)
    return lane < HEAD_DIM, lane >= HEAD_DIM


def _pair_operands(ref, r0, n, compact, masks, masked):
    if not compact:
        return [ref[pl.ds(r0, n), h * LANES:(h + 1) * LANES] for h in range(2)]
    pair = ref[pl.ds(r0, n), :]
    return [jnp.where(mk, pair, jnp.zeros_like(pair)) for mk in masks] if masked else [pair, pair]


def _causal(n_rows, n_cols, shift):
    return (lax.broadcasted_iota(jnp.int32, (n_rows, n_cols), 1)
            <= lax.broadcasted_iota(jnp.int32, (n_rows, n_cols), 0) + shift)


def _attn_fwd(q_arr, q_off, k_arr, k_off, v_arr, v_off, bias, seq, name, ride=None, compact=False):
    t = q_arr.shape[0]
    nb = t // seq
    blk = min(ATTN_BLOCK, seq)
    nq = seq // blk
    qw = LANES if compact else 2 * LANES
    has_bias = bias is not None
    n_in, n_out = (4, 3) if has_bias else (3, 2)

    def body(*refs):
        step = pl.program_id(0) * HEAD_PAIRS + pl.program_id(1)
        refs, ride_end = _ride_steps(ride, refs, n_in, n_out, step, nb * HEAD_PAIRS)
        if has_bias:
            q_ref, k_ref, v_ref, bias_ref, o_ref, lse_ref, o32_ref = refs
        else:
            q_ref, k_ref, v_ref, o_ref, lse_ref = refs
        masks = _head_masks()
        lo = masks[0]

        def update(r0, n, carry, k0, nk, mask):
            qs = _pair_operands(q_ref, r0, n, compact, masks, True)
            ks = _pair_operands(k_ref, k0, nk, compact, masks, False)
            vv = v_ref[pl.ds(k0, nk), :]
            vs = [jnp.where(mk, vv, jnp.zeros_like(vv)) for mk in masks]
            new, alphas, pv = [], [], None
            for h in range(2):
                m, l = carry[1 + 2 * h], carry[2 + 2 * h]
                s = lax.dot_general(qs[h], ks[h], _DIMS["nt"], preferred_element_type=F32)
                if has_bias:
                    s = s + bias_ref[h, 0:1, pl.ds(k0, nk)]
                if mask is not None:
                    s = jnp.where(mask, s, -jnp.inf)
                m_new = jnp.maximum(m, jnp.max(s, axis=-1, keepdims=True))
                p = jnp.exp(s - m_new)
                alpha = jnp.exp(m - m_new)
                l_new = alpha * l + jnp.sum(p, axis=-1, keepdims=True)
                p_hi = p.astype(BF)
                d = jnp.dot(p_hi, vs[h], preferred_element_type=F32)
                if has_bias:
                    p_lo = (p - p_hi.astype(F32)).astype(BF)
                    d = d + jnp.dot(p_lo, vs[h], preferred_element_type=F32)
                pv = d if pv is None else pv + d
                alphas.append(alpha)
                new += [m_new, l_new]
            return (carry[0] * jnp.where(lo, alphas[0], alphas[1]) + pv, *new)

        def q_block(iq, _):
            q0 = pl.multiple_of(iq * blk, blk)
            init = (jnp.zeros((blk, LANES), F32),
                    jnp.full((blk, 1), -jnp.inf, F32), jnp.zeros((blk, 1), F32),
                    jnp.full((blk, 1), -jnp.inf, F32), jnp.zeros((blk, 1), F32))
            carry = lax.fori_loop(
                0, iq, lambda j, c: update(q0, blk, c, pl.multiple_of(j * blk, blk), blk, None), init)
            acc, m0, l0, m1, l1 = update(q0, blk, carry, q0, blk, _causal(blk, blk, 0))
            o_val = acc / jnp.where(lo, l0, l1)
            o_ref[pl.ds(q0, blk), :] = o_val.astype(BF)
            if has_bias:
                o32_ref[pl.ds(q0, blk), :] = o_val
            lse_ref[pl.ds(q0, blk), :] = jnp.where(lo, m0 + jnp.log(l0), m1 + jnp.log(l1))
            return 0

        lax.fori_loop(0, nq, q_block, 0)
        ride_end()

    in_specs = [pl.BlockSpec((seq, qw), lambda b, p: (b, q_off + p)),
                pl.BlockSpec((seq, qw), lambda b, p: (b, k_off + p)),
                pl.BlockSpec((seq, LANES), lambda b, p: (b, v_off + p))]
    args = [q_arr, k_arr, v_arr]
    if has_bias:
        in_specs.append(pl.BlockSpec((None, 2, 8, seq), lambda b, p: (b, p, 0, 0)))
        args.append(bias)
    out_blk = pl.BlockSpec((seq, LANES), lambda b, p: (b, p))
    out_shape = [jax.ShapeDtypeStruct((t, HEAD_PAIRS * LANES), BF), jax.ShapeDtypeStruct((t, HEAD_PAIRS * LANES), F32)]
    if has_bias:
        out_shape.append(jax.ShapeDtypeStruct((t, HEAD_PAIRS * LANES), F32))
    out_specs, scratch = [out_blk] * len(out_shape), []
    if ride is not None:
        r_in, r_shape, r_out, scratch = _ride_specs(ride)
        in_specs, out_shape, out_specs = in_specs + r_in, out_shape + r_shape, out_specs + r_out
        args += list(ride[1])
    return pl.pallas_call(
        body, name=name, grid=(nb, HEAD_PAIRS),
        out_shape=out_shape, in_specs=in_specs, out_specs=out_specs, scratch_shapes=scratch,
        compiler_params=_params(("arbitrary", "arbitrary")),
    )(*args)


def _attn_bwd(q_arr, q_off, k_arr, k_off, v_arr, v_off, bias, o, do, lse, seq, name, ride=None, compact=False):
    t = q_arr.shape[0]
    nb = t // seq
    blk = min(ATTN_BLOCK, seq)
    half = blk // 2
    nq = seq // blk
    qw = LANES if compact else 2 * LANES
    has_bias = bias is not None
    n_in, n_out = (7, 4) if has_bias else (6, 3)

    def body(*refs):
        step = pl.program_id(0) * HEAD_PAIRS + pl.program_id(1)
        refs, ride_end = _ride_steps(ride, refs, n_in, n_out, step, nb * HEAD_PAIRS)
        if has_bias:
            (q_ref, k_ref, v_ref, bias_ref, o_ref, do_ref, lse_ref,
             dq_ref, dk_ref, dv_ref, dbias_ref, dq_acc, dsum) = refs
        else:
            (q_ref, k_ref, v_ref, o_ref, do_ref, lse_ref, dq_ref, dk_ref, dv_ref, dq_acc, dsum) = refs
        masks = _head_masks()
        lo, hi = masks
        dq_acc[...] = jnp.zeros_like(dq_acc)

        def prep(iq, _):
            q0 = pl.multiple_of(iq * blk, blk)
            prod = do_ref[pl.ds(q0, blk), :].astype(F32) * o_ref[pl.ds(q0, blk), :].astype(F32)
            d0 = jnp.sum(jnp.where(lo, prod, 0.0), axis=-1, keepdims=True)
            d1 = jnp.sum(jnp.where(hi, prod, 0.0), axis=-1, keepdims=True)
            dsum[pl.ds(q0, blk), :] = jnp.where(lo, d0, d1)
            return 0

        lax.fori_loop(0, nq, prep, 0)

        def tile(r0, n, k0, nk, mask):
            qs = _pair_operands(q_ref, r0, n, compact, masks, True)
            ks = _pair_operands(k_ref, k0, nk, compact, masks, True)
            vv = v_ref[pl.ds(k0, nk), :]
            vs = [jnp.where(mk, vv, jnp.zeros_like(vv)) for mk in masks]
            dov = do_ref[pl.ds(r0, n), :]
            dos = [jnp.where(mk, dov, jnp.zeros_like(dov)) for mk in masks] if compact else None
            lse_v = lse_ref[pl.ds(r0, n), :]
            dsum_v = dsum[pl.ds(r0, n), :]
            dv_c, dks, dbs = None, [], []
            for h in range(2):
                s = lax.dot_general(qs[h], ks[h], _DIMS["nt"], preferred_element_type=F32)
                if has_bias:
                    s = s + bias_ref[h, 0:1, pl.ds(k0, nk)]
                p = jnp.exp(s - lse_v[:, h * HEAD_DIM:h * HEAD_DIM + 1])
                if mask is not None:
                    p = jnp.where(mask, p, 0.0)
                dp = lax.dot_general(dov, vs[h], _DIMS["nt"], preferred_element_type=F32)
                ds = p * (dp - dsum_v[:, h * HEAD_DIM:h * HEAD_DIM + 1])
                ds_bf = ds.astype(BF)
                if compact:
                    dv_h = lax.dot_general(p.astype(BF), dos[h], _DIMS["tn"], preferred_element_type=F32)
                else:
                    dv_h = jnp.where(masks[h], lax.dot_general(p.astype(BF), dov, _DIMS["tn"],
                                                               preferred_element_type=F32), 0.0)
                dv_c = dv_h if dv_c is None else dv_c + dv_h
                dk_h = lax.dot_general(ds_bf, qs[h], _DIMS["tn"], preferred_element_type=F32)
                dq_h = jnp.dot(ds_bf, ks[h], preferred_element_type=F32)
                if compact:
                    dks = [dk_h] if h == 0 else [dks[0] + dk_h, jnp.zeros((8, LANES), F32)]
                    if h == 0:
                        dq_first = dq_h
                    else:
                        dq_acc[pl.ds(r0, n), :] += dq_first + dq_h
                else:
                    dks.append(dk_h)
                    dq_acc[pl.ds(r0, n), h * LANES:(h + 1) * LANES] += dq_h
                dbs.append(jnp.sum(ds, axis=0, keepdims=True) if has_bias else jnp.zeros((1, nk), F32))
            return (dv_c, dks[0], dks[1], dbs[0], dbs[1])

        def kv_block(j, _):
            k0 = pl.multiple_of(j * blk, blk)
            if compact:
                dv_a, dk_a, dummy, db0_a, db1_a = tile(pl.multiple_of(k0 + half, half), half, k0, blk,
                                                       _causal(half, blk, half))
                top = tile(k0, half, k0, half, _causal(half, half, 0))
                head = lambda acc, x: jnp.concatenate([acc[:half] + x, acc[half:]], axis=0)
                lead = lambda acc, x: jnp.concatenate([acc[:, :half] + x, acc[:, half:]], axis=1)
                carry = (head(dv_a, top[0]), head(dk_a, top[1]), dummy, lead(db0_a, top[3]), lead(db1_a, top[4]))
            else:
                carry = tile(k0, blk, k0, blk, _causal(blk, blk, 0))

            def q_block(iq, c):
                part = tile(pl.multiple_of(iq * blk, blk), blk, k0, blk, None)
                return tuple(a + b for a, b in zip(c, part))

            carry = lax.fori_loop(j + 1, nq, q_block, carry)
            dv_ref[pl.ds(k0, blk), :] = carry[0].astype(BF)
            if compact:
                dk_ref[pl.ds(k0, blk), :] = carry[1].astype(BF)
            else:
                for h in range(2):
                    dk_ref[pl.ds(k0, blk), h * LANES:(h + 1) * LANES] = carry[1 + h].astype(BF)
            if has_bias:
                for h in range(2):
                    dbias_ref[h, :, pl.ds(k0, blk)] = jnp.broadcast_to(carry[3 + h], (8, blk))
            return 0

        lax.fori_loop(0, nq, kv_block, 0)
        dq_ref[...] = dq_acc[...].astype(BF)
        ride_end()

    pair256 = lambda off: pl.BlockSpec((seq, qw), lambda b, p: (b, off + p))
    pair128 = lambda off: pl.BlockSpec((seq, LANES), lambda b, p: (b, off + p))
    bias_spec = pl.BlockSpec((None, 2, 8, seq), lambda b, p: (b, p, 0, 0))
    in_specs = [pair256(q_off), pair256(k_off), pair128(v_off)]
    args = [q_arr, k_arr, v_arr]
    if has_bias:
        in_specs.append(bias_spec)
        args.append(bias)
    in_specs += [pair128(0), pair128(0), pair128(0)]
    args += [o, do, lse]
    out_shape = [jax.ShapeDtypeStruct((t, HEAD_PAIRS * qw), BF),
                 jax.ShapeDtypeStruct((t, HEAD_PAIRS * qw), BF),
                 jax.ShapeDtypeStruct((t, HEAD_PAIRS * LANES), BF)]
    out_specs = [pair256(0), pair256(0), pair128(0)]
    if has_bias:
        out_shape.append(jax.ShapeDtypeStruct((nb, HEADS, 8, seq), F32))
        out_specs.append(bias_spec)
    scratch = [pltpu.VMEM((seq, qw), F32), pltpu.VMEM((seq, LANES), F32)]
    if ride is not None:
        r_in, r_shape, r_out, r_scratch = _ride_specs(ride)
        in_specs, out_shape, out_specs = in_specs + r_in, out_shape + r_shape, out_specs + r_out
        args += list(ride[1])
        scratch += r_scratch
    return pl.pallas_call(
        body, name=name, grid=(nb, HEAD_PAIRS),
        out_shape=out_shape, in_specs=in_specs, out_specs=out_specs, scratch_shapes=scratch,
        compiler_params=_params(("arbitrary", "arbitrary")),
    )(*args)


def _adamw(w, g, m, v, name):
    shape = w.shape
    last = shape[-1]
    rows = int(np.prod(shape[:-1])) if len(shape) > 1 else 1
    tr = _rows(rows, 512)

    def body(w_ref, g_ref, m_ref, v_ref, d_ref, nm_ref, nv_ref):
        d_ref[...], nm_ref[...], nv_ref[...] = _adamw_math(w_ref[...], g_ref[...], m_ref[...], v_ref[...])

    blk = pl.BlockSpec((tr, last), lambda i: (i, 0))
    sds = jax.ShapeDtypeStruct((rows, last), F32)
    outs = pl.pallas_call(
        body, name=name, grid=(rows // tr,),
        out_shape=(sds, sds, sds), in_specs=[blk] * 4, out_specs=(blk,) * 3,
        compiler_params=_params(("parallel",)),
    )(*[a.reshape(rows, last) for a in (w, g, m, v)])
    return tuple(a.reshape(shape) for a in outs)


LOW_COLS = 256


def _low_pad(a):
    return jnp.pad(a, ((0, 0),) * (a.ndim - 1) + ((0, LOW_COLS - a.shape[-1]),))


def _layer_shards(w, i):
    j = i // 2
    bf = lambda a: a.astype(BF)
    if i % 2 == 0:
        mixer = [bf(w["fox_w_in"][j]), bf(w["fox_w_out"][j])]
    else:
        mixer = [jnp.concatenate([bf(w["mla_w_dq"][j]), bf(w["mla_w_ukv"][j]), _low_pad(bf(w["mla_w_uq"][j])),
                                  _low_pad(bf(w["mla_w_dkv"][j]))], axis=0), bf(w["mla_w_out"][j])]
    return mixer + [bf(w["mlp_w1"][i]), bf(w["mlp_w2"][i])]


def _side_by_side(stack, r0, rows, cols=None):
    return jnp.concatenate([stack[dd, r0:r0 + rows, :cols] for dd in range(N_DEV)], axis=1)


def _stacked(stack, r0, rows, cols=None):
    part = stack[:, r0:r0 + rows, :cols]
    return part.reshape(N_DEV * rows, part.shape[2])


def _layer_mixer_weights(i, first):
    if i % 2 == 0:
        return dict(fox_w_in=_side_by_side(first, 0, 1024))
    return dict(mla_w_dq=_stacked(first, 0, 128), mla_w_ukv=_side_by_side(first, 128, 128),
                mla_w_uq=_side_by_side(first, 256, 256, 192), mla_w_dkv=_stacked(first, 512, 128, 160))


def _by_dest_rows(g):
    return g.reshape(N_DEV, g.shape[0] // N_DEV, g.shape[1]).astype(BF)


def _by_dest_cols(g):
    n = g.shape[1] // N_DEV
    return jnp.stack([g[:, dd * n:(dd + 1) * n] for dd in range(N_DEV)]).astype(BF)


def _layer_mixer_grad_bufs(i, g):
    if i % 2 == 0:
        return [_by_dest_cols(g["fox_w_in"]), _by_dest_rows(g["w_out"])]
    low = jnp.concatenate([_by_dest_rows(g["mla_w_dq"]), _by_dest_cols(g["mla_w_ukv"]),
                           _low_pad(_by_dest_cols(g["mla_w_uq"])), _low_pad(_by_dest_rows(g["mla_w_dkv"]))], axis=1)
    return [low, _by_dest_rows(g["w_out"])]


def _low_shard_grads(low):
    return dict(mla_w_dq=low[:128], mla_w_ukv=low[128:256], mla_w_uq=low[256:512, :192], mla_w_dkv=low[512:, :160])


def _pad_heads(w, width):
    k = w.shape[0]
    return jnp.pad(w.reshape(k, HEADS, width), ((0, 0), (0, 0), (0, LANES - width))).reshape(k, HEADS * LANES)


def _unpad_heads(w, width):
    k = w.shape[0]
    return w.reshape(k, HEADS, LANES)[:, :, :width].reshape(k, HEADS * width)


def _rope_tables(positions, scale):
    inv_freq = 10000.0 ** (-jnp.arange(0, 2 * ROPE_HALF, 2, dtype=F32) / (2 * ROPE_HALF))
    ang = positions.astype(F32)[:, None] * inv_freq
    cos, sin = jnp.cos(ang) * scale, jnp.sin(ang) * scale
    t = positions.shape[0]
    z = lambda n: jnp.zeros((t, n), F32)
    cos_p = jnp.concatenate([jnp.full((t, HEAD_DIM), scale, F32), cos, cos, z(32)], axis=1)
    sin_a = jnp.concatenate([z(64), -sin, z(48)], axis=1)
    sin_b = jnp.concatenate([z(80), sin, z(32)], axis=1)
    fwd = (cos_p, sin_a, sin_b)
    bwd = (cos_p, jnp.roll(sin_b, -ROPE_HALF, axis=1), jnp.roll(sin_a, ROPE_HALF, axis=1))
    return fwd, bwd


def _key_rows(cum, nb, seq):
    v = -cum.reshape(nb, seq, LANES)[:, :, :HEADS]
    return jnp.broadcast_to(jnp.transpose(v, (0, 2, 1))[:, :, None, :], (nb, HEADS, 8, seq))


def kernel(x, c, positions, ada_w, ada_b, norm_mix_g, norm_mlp_g, fox_w_in, fox_b_f, fox_w_out, mla_w_dq, mla_q_norm_g, mla_w_uq, mla_w_dkv, mla_kv_norm_g, mla_w_ukv, mla_w_out, mlp_w1, mlp_w2, final_norm_g, loss_target, m_ada_w, m_ada_b, m_norm_mix_g, m_norm_mlp_g, m_fox_w_in, m_fox_b_f, m_fox_w_out, m_mla_w_dq, m_mla_q_norm_g, m_mla_w_uq, m_mla_w_dkv, m_mla_kv_norm_g, m_mla_w_ukv, m_mla_w_out, m_mlp_w1, m_mlp_w2, m_final_norm_g, v_ada_w, v_ada_b, v_norm_mix_g, v_norm_mlp_g, v_fox_w_in, v_fox_b_f, v_fox_w_out, v_mla_w_dq, v_mla_q_norm_g, v_mla_w_uq, v_mla_w_dkv, v_mla_kv_norm_g, v_mla_w_ukv, v_mla_w_out, v_mlp_w1, v_mlp_w2, v_final_norm_g):
    args = dict(locals())
    weights = {n: args[n] for n in WEIGHTS}
    nb, seq, d = x.shape
    t = nb * seq
    depth = ada_w.shape[0]
    dev = 4 * lax.axis_index("x") + 2 * lax.axis_index("y") + lax.axis_index("c")
    n_mod_local = ada_w.shape[2]

    n_qg = mla_q_norm_g.shape[1]
    cond = jnp.concatenate([c, jnp.pad(mla_q_norm_g.reshape(1, -1), ((0, 7), (0, d - 2 * n_qg)))], axis=0)
    w1_rows, w2_rows = mlp_w1.shape[1], mlp_w2.shape[1]
    shards = [_layer_shards(weights, i) for i in range(depth)]
    stacks = [None] * depth
    cond_all, = _all_gather([cond], "gather_cond")
    c_all = cond_all[:, :nb].reshape(N_DEV * nb, d)
    q_gain = jnp.transpose(cond_all[:, nb, :2 * n_qg].reshape(N_DEV, 2, n_qg), (1, 0, 2)).reshape(2, N_DEV * n_qg)
    mod_local = jnp.stack([
        _matmul(c_all, ada_w[i], mode="nn", name="ada_mod", out_dtype=F32, a_act="silu", epi="bias",
                extras=(lax.dynamic_slice_in_dim(ada_b[i], dev * n_mod_local, n_mod_local)[None, :],))
        for i in range(depth)])
    mod_all, first_in = _all_gather([mod_local.reshape(depth * N_DEV * nb, n_mod_local), shards[0][0]], "gather_first")
    stacks[0] = [first_in]
    mod_all = jnp.transpose(mod_all.reshape(N_DEV, depth, N_DEV * nb, n_mod_local), (1, 2, 0, 3))
    mod_all = mod_all.reshape(depth, N_DEV * nb, N_DEV * n_mod_local)
    mod = lax.dynamic_slice_in_dim(mod_all, dev * nb, nb, axis=1)
    mod = mod.reshape(depth, nb, 6, 1, d)

    pos = positions.reshape(t)
    rope_q, rope_q_t = _rope_tables(pos, MLA_SCALE)
    rope_k, rope_k_t = _rope_tables(pos, 1.0)

    def fox_weights(full):
        w_in = full["fox_w_in"]
        w_qkv = jnp.concatenate([w_in[:, :d] * FOX_SCALE, w_in[:, d:3 * d]], axis=1)
        w_f = jnp.pad(w_in[:, 3 * d:], ((0, 0), (0, LANES - HEADS)))
        return w_qkv, w_f

    def mla_weights(full):
        w_dkv = full["mla_w_dkv"]
        w_down = jnp.concatenate([full["mla_w_dq"], w_dkv[:, :128], jnp.zeros((d, 64), BF),
                                  w_dkv[:, 128:160], jnp.zeros((d, 32), BF)], axis=1)
        w_uq = _pad_heads(full["mla_w_uq"], 96)
        w_ukv = full["mla_w_ukv"].reshape(128, HEADS, 2, HEAD_DIM)
        w_uk = jnp.pad(w_ukv[:, :, 0, :], ((0, 0), (0, 0), (0, 64))).reshape(128, HEADS * LANES)
        w_uv = w_ukv[:, :, 1, :].reshape(128, HEADS * HEAD_DIM)
        place = np.zeros((128, HEADS, LANES), np.float32)
        for i in range(2 * ROPE_HALF):
            place[64 + i, :, 64 + i] = 1.0
        bottom = jnp.concatenate([jnp.asarray(place.reshape(128, HEADS * LANES), BF),
                                  jnp.zeros((128, HEADS * HEAD_DIM), BF)], axis=1)
        w_kv = jnp.concatenate([jnp.concatenate([w_uk, w_uv], axis=1), bottom], axis=0)
        return w_down, w_uq, w_kv

    tm_big = min(2048, t)
    xs = x.reshape(t, d)
    saved = []
    for i in range(depth):
        j = i // 2
        sh_m, sc_m, g_m, sh_f, sc_f, g_f = (mod[i, :, q] for q in range(6))
        gain_mix = norm_mix_g[i][None, :]
        gain_mlp = norm_mlp_g[i][None, :]
        s = dict(x_in=xs)
        h = _norm_mod(xs, gain_mix, sc_m, sh_m, seq, "norm_mix")
        s["h"] = h
        full = _layer_mixer_weights(i, stacks[i][0])
        riders = (shards[0][1:] if i == 0 else []) + (shards[i + 1] if i + 1 < depth else [])
        ride = ("gather", riders) if riders else None
        if i % 2 == 0:
            w_qkv, w_f = fox_weights(full)
            qkv = _matmul(h, w_qkv, mode="nn", name="fox_qkv")
            fg = _matmul(h, w_f, mode="nn", name="fox_gate_logits", out_dtype=F32)
            b_f = jnp.pad(fox_b_f[j], (0, LANES - HEADS))[None, :]
            cum = _fox_gate(fg, b_f, seq, "fox_gate")
            bias = _key_rows(cum, nb, seq)
            o, lse, o32, *rode = _attn_fwd(qkv, 0, qkv, 8, qkv, 16, bias, seq, "fox_attn", ride, compact=True)
            s.update(qkv=qkv, fg=fg, b_f=b_f, bias=bias, w_qkv=w_qkv, w_f=w_f, o32=o32)
        else:
            w_down, w_uq, w_kv = mla_weights(full)
            down = _matmul(h, w_down, mode="nn", name="mla_down", out_dtype=F32)
            gq, gkv = q_gain[j][None, :], mla_kv_norm_g[j][None, :]
            cq, ckr = _mla_mid(down, gq, gkv, rope_k, "mla_mid")
            q_raw = _matmul(cq, w_uq, mode="nn", name="mla_uq", out_dtype=F32)
            q_rot = _rope(q_raw, rope_q, "mla_rope_q")
            kv = _matmul(ckr, w_kv, mode="nn", name="mla_ukv")
            o, lse, *rode = _attn_fwd(q_rot, 0, kv, 0, kv, 16, None, seq, "mla_attn", ride)
            s.update(down=down, gq=gq, gkv=gkv, cq=cq, ckr=ckr, q_rot=q_rot, kv=kv,
                     w_down=w_down, w_uq=w_uq, w_kv=w_kv)
        if i == 0:
            stacks[0], rode = stacks[0] + rode[:3], rode[3:]
        if i + 1 < depth:
            stacks[i + 1] = rode
        w_out = _stacked(stacks[i][1], 0, 128)
        xs, y = _matmul(o, w_out, mode="nn", name="attn_out", epi="resid_gate", extras=(xs, g_m), seq=seq)
        s.update(o=o, lse=lse, y=y, w_out=w_out, x_mid=xs)
        h2 = _norm_mod(xs, gain_mlp, sc_f, sh_f, seq, "norm_mlp")
        a_pre = _matmul(h2, stacks[i][2], mode="nn", name="mlp_up", layer=("col", 0, w1_rows), tm=tm_big)
        xs, y2 = _matmul(a_pre, stacks[i][3], mode="nn", name="mlp_down", layer=("row", 0, w2_rows), a_act="relu2",
                         epi="resid_gate", extras=(xs, g_f), seq=seq)
        s.update(h2=h2, a_pre=a_pre, y2=y2)
        saved.append(s)

    loss_part, dx, dg_final = _loss_head(xs, final_norm_g[None, :], loss_target.reshape(t, d), "loss_head")

    w1_cols, w1_tm = mlp_w1.shape[2], _pick(w1_rows, 1024)
    dg_mix, dg_mlp, db_f, dg_kv, dg_q = [None] * depth, [None] * depth, [None] * 2, [None] * 2, [None] * 2
    dmod = [None] * depth
    me = dev.astype(jnp.int32).reshape(1)
    chains = dict(fox_w_in=None, fox_w_out=None, mla_w_out=None, mlp_w1=None, mlp_w2=None)
    low_grads = [None] * 2

    def adam_step(name, layer):
        def land_one(buf, got):
            chains[name] = _total_adamw(buf, me, got, weights[name], args["m_" + name], args["v_" + name], layer,
                                        chains[name], "adamw_" + name)
        return land_one

    def low_step(lj):
        def land_one(buf, got):
            low_grads[lj] = _low_shard_grads(_add_parts(buf, me, got, "grads_total"))
        return land_one

    def mixer_steps(li):
        lj = li // 2
        return ([adam_step("fox_w_in", lj), adam_step("fox_w_out", lj)] if li % 2 == 0
                else [low_step(lj), adam_step("mla_w_out", lj)])

    def mlp_steps(li):
        return [adam_step("mlp_w1", li), adam_step("mlp_w2", li)]

    def land(staged, got):
        for (step, buf), g in zip(staged, got):
            step(buf, g)

    waiting = []
    dy2, dg_f = _gate_bwd(dx, saved[depth - 1]["y2"], mod[depth - 1, :, 5], seq, "gate_bwd")
    for i in reversed(range(depth)):
        j = i // 2
        s = saved[i]
        sh_m, sc_m, g_m, sh_f, sc_f, g_f = (mod[i, :, q] for q in range(6))
        da_pre = _matmul(dy2, stacks[i][3], mode="nt", name="mlp_down_dx", layer=("row", 0, w2_rows), epi="mul_drelu",
                         extras=(s["a_pre"],), tm=tm_big)
        g_w2 = _matmul(s["a_pre"], dy2, mode="tn", name="mlp_down_dw", a_act="relu2", tm=w2_rows,
                       into=(jax.ShapeDtypeStruct((N_DEV, w2_rows, d), BF), (None, w2_rows, d),
                             lambda r, j, k: (r, 0, 0)))
        dh2 = _matmul(da_pre, stacks[i][2], mode="nt", name="mlp_up_dx", layer=("col", 0, w1_rows), tm=tm_big)
        g_w1 = _matmul(s["h2"], da_pre, mode="tn", name="mlp_up_dw", tm=w1_tm, tn=w1_cols,
                       into=(jax.ShapeDtypeStruct((N_DEV, w1_rows, w1_cols), BF), (None, w1_tm, w1_cols),
                             lambda r, j, k: (j, r, 0)))
        waiting += list(zip(mlp_steps(i), [g_w1, g_w2]))
        dx, dg_mlp[i], dsc_f, dsh_f, dy, dg_m = _norm_mod_bwd(dh2, s["x_mid"], norm_mlp_g[i][None, :], sc_f, dx, seq,
                                                              "norm_bwd_gate", gate=(s["y"], g_m))
        do = _matmul(dy, s["w_out"], mode="nt", name="attn_out_dx")
        dw_out = _matmul(s["o"], dy, mode="tn", name="attn_out_dw", out_dtype=F32)
        if i == 0:
            waiting.append((adam_step("fox_w_out", 0), _by_dest_rows(dw_out)))
        ride = ("scatter", [e[1] for e in waiting]) if waiting else None
        g_mixer = dict(w_out=dw_out)
        if i % 2 == 0:
            qkv = s["qkv"]
            dq, dk, dv, dbias, *rode = _attn_bwd(qkv, 0, qkv, 8, qkv, 16, s["bias"], s["o32"], do, s["lse"], seq,
                                                 "fox_attn_bwd", ride, compact=True)
            dqkv = jnp.concatenate([dq, dk, dv], axis=1)
            d_cum = -jnp.transpose(dbias[:, :, 0, :], (0, 2, 1)).reshape(t, HEADS)
            d_cum = jnp.pad(d_cum, ((0, 0), (0, LANES - HEADS)))
            dfg, db = _fox_gate_bwd(d_cum, s["fg"], s["b_f"], seq, "fox_gate_bwd")
            db_f[j] = db
            dh = _matmul(dfg, s["w_f"], mode="nt", name="fox_gate_dx", out_dtype=F32)
            dh = _matmul(dqkv, s["w_qkv"], mode="nt", name="fox_qkv_dx", epi="add", extras=(dh,))
            dw_qkv = _matmul(s["h"], dqkv, mode="tn", name="fox_qkv_dw", out_dtype=F32)
            dw_f = _matmul(s["h"], dfg, mode="tn", name="fox_gate_dw", out_dtype=F32)
            g_mixer["fox_w_in"] = jnp.concatenate([dw_qkv[:, :d] * FOX_SCALE, dw_qkv[:, d:], dw_f[:, :HEADS]], axis=1)
        else:
            kv = s["kv"]
            dq, dk, dv, *rode = _attn_bwd(s["q_rot"], 0, kv, 0, kv, 16, None, s["o"], do, s["lse"], seq,
                                          "mla_attn_bwd", ride)
            dq_raw = _rope(dq, rope_q_t, "mla_rope_q_bwd")
            dcq = _matmul(dq_raw, s["w_uq"], mode="nt", name="mla_uq_dx")
            dw_uq = _matmul(s["cq"], dq_raw, mode="tn", name="mla_uq_dw", out_dtype=F32)
            dkv = jnp.concatenate([dk, dv], axis=1)
            dckr = _matmul(dkv, s["w_kv"], mode="nt", name="mla_ukv_dx")
            dw_kv = _matmul(s["ckr"], dkv, mode="tn", name="mla_ukv_dw", out_dtype=F32)
            d_down, dgq, dgkv = _mla_mid_bwd(s["down"], dcq, dckr, s["gq"], s["gkv"], rope_k_t, "mla_mid_bwd")
            dg_q[j], dg_kv[j] = dgq, dgkv
            dh = _matmul(d_down, s["w_down"], mode="nt", name="mla_down_dx")
            dw_down = _matmul(s["h"], d_down, mode="tn", name="mla_down_dw", out_dtype=F32)
            g_mixer["mla_w_dq"] = dw_down[:, :256]
            g_mixer["mla_w_dkv"] = jnp.concatenate([dw_down[:, 256:384], dw_down[:, 448:480]], axis=1)
            g_mixer["mla_w_uq"] = _unpad_heads(dw_uq, 96)
            dk_nope = dw_kv[:128, :HEADS * LANES].reshape(128, HEADS, LANES)[:, :, :HEAD_DIM]
            dv_w = dw_kv[:128, HEADS * LANES:].reshape(128, HEADS, HEAD_DIM)
            g_mixer["mla_w_ukv"] = jnp.concatenate([dk_nope, dv_w], axis=2).reshape(128, HEADS * LANES)
        land(waiting, rode)
        this_dg_f = dg_f
        if i > 0:
            dx, dg_mix[i], dsc_m, dsh_m, dy2, dg_f = _norm_mod_bwd(
                dh, s["x_in"], norm_mix_g[i][None, :], sc_m, dx, seq, "norm_bwd_gate",
                gate=(saved[i - 1]["y2"], mod[i - 1, :, 5]))
            waiting = list(zip(mixer_steps(i), _layer_mixer_grad_bufs(i, g_mixer)))
        else:
            dx, dg_mix[i], dsc_m, dsh_m = _norm_mod_bwd(dh, s["x_in"], norm_mix_g[i][None, :], sc_m, dx, seq, "norm_bwd")
            last = [(adam_step("fox_w_in", 0), _by_dest_cols(g_mixer["fox_w_in"]))]
        dmod[i] = jnp.stack([dsh_m, dsc_m, dg_m, dsh_f, dsc_f, this_dg_f], axis=1).reshape(nb, 6 * d)

    grad_x = dx.reshape(nb, seq, d)
    shard_grads = {n: jnp.stack([low_grads[0][n], low_grads[1][n]]) for n in low_grads[0]}

    dmod_arr = jnp.stack(dmod)
    wide = lambda a: jnp.pad(a, ((0, 0), (0, d - a.shape[1])))
    pieces = [wide(loss_part), *dg_mix, *dg_mlp, *[wide(a) for a in db_f], *[wide(a) for a in dg_kv], dg_final,
              *[wide(a) for a in dg_q], jnp.sum(dmod_arr, axis=1).reshape(depth * 6, d)]
    n_small = sum(p.shape[0] for p in pieces)
    both = jnp.concatenate(pieces + [dmod_arr.reshape(depth * nb * 6, d)], axis=0)
    both = jnp.pad(both, ((0, (-both.shape[0]) % 8), (0, 0)))
    scattered, (both_all,) = _scatter_and_gather([e[1] for e in last], [both], "last_exchange")
    land(last, scattered)
    done = {n: tuple(a.reshape(weights[n].shape) for a in chain) for n, chain in chains.items()}
    total = _sum_leading(both_all, "sum_small")
    off = 0

    def take(rows):
        nonlocal off
        out = total[off:off + rows]
        off += rows
        return out

    loss = take(1)[0, 0]
    g_small = dict(
        norm_mix_g=take(depth), norm_mlp_g=take(depth), fox_b_f=take(2)[:, :HEADS], mla_kv_norm_g=take(2)[:, :128],
        final_norm_g=take(1)[0],
        mla_q_norm_g=lax.dynamic_slice_in_dim(take(2)[:, :N_DEV * n_qg], dev * n_qg, n_qg, axis=1),
        ada_b=take(depth * 6).reshape(depth, 6 * d))
    dmod_all = both_all[:, n_small:n_small + depth * nb * 6]
    dmod_all = jnp.transpose(dmod_all.reshape(N_DEV, depth, nb, 6 * d), (1, 0, 2, 3)).reshape(depth, N_DEV * nb, 6 * d)
    dmod_cols = lax.dynamic_slice_in_dim(dmod_all, dev * n_mod_local, n_mod_local, axis=2)
    g_ada_w = jnp.stack([_matmul(c_all, dmod_cols[i], mode="tn", name="ada_dw", out_dtype=F32, a_act="silu")
                         for i in range(depth)])

    all_grads = dict(shard_grads)
    all_grads.update(g_small)
    all_grads["ada_w"] = g_ada_w

    deltas, new_m, new_v = {}, {}, {}
    for n in WEIGHTS:
        if n in done:
            all_grads[n], deltas[n], new_m[n], new_v[n] = done[n]
        else:
            deltas[n], new_m[n], new_v[n] = _adamw(weights[n], all_grads[n], args["m_" + n], args["v_" + n], "adamw")

    return (loss, grad_x, *[all_grads[n] for n in WEIGHTS], *[deltas[n] for n in WEIGHTS],
            *[new_m[n] for n in WEIGHTS], *[new_v[n] for n in WEIGHTS])
```

```python
import functools
import math

import jax
import jax.numpy as jnp
import numpy as np
from jax import lax
from jax.experimental import pallas as pl
from jax.experimental.pallas import tpu as pltpu

F32 = jnp.float32
BF = jnp.bfloat16

N_DEV = 8
HEADS = 16
HEAD_PAIRS = HEADS // 2
HEAD_DIM = 64
LANES = 128
ROPE_HALF = 16
NORM_EPS = 1e-6
MLA_SCALE = 96.0 ** -0.5
FOX_SCALE = 0.125
ATTN_BLOCK = 512
ROW_BLOCK = 512
K_SPAN = 4
VMEM_LIMIT = 56 * 1024 * 1024
MESH = pl.DeviceIdType.MESH

ADAM_LR = 0.001
ADAM_B1 = 0.9
ADAM_B2 = 0.999
ADAM_EPS = 1e-08
ADAM_WD = 0.01
ADAM_STEP = 10

WEIGHTS = ("ada_w", "ada_b", "norm_mix_g", "norm_mlp_g", "fox_w_in", "fox_b_f", "fox_w_out", "mla_w_dq",
           "mla_q_norm_g", "mla_w_uq", "mla_w_dkv", "mla_kv_norm_g", "mla_w_ukv", "mla_w_out", "mlp_w1",
           "mlp_w2", "final_norm_g")


def _params(sem=None):
    return pltpu.CompilerParams(dimension_semantics=sem, vmem_limit_bytes=VMEM_LIMIT)


def _pick(n, target):
    if n <= target:
        return n
    for t in range(target, 127, -128):
        if n % t == 0:
            return t
    return n


def _rows(n, target=512):
    if n <= target:
        return n
    for t in range(target, 7, -8):
        if n % t == 0:
            return t
    return n


def _place():
    x, y, c = lax.axis_index("x"), lax.axis_index("y"), lax.axis_index("c")
    return x, y, c


def _adamw_math(w, g, m, v):
    nm = ADAM_B1 * m + (1.0 - ADAM_B1) * g
    nv = ADAM_B2 * v + (1.0 - ADAM_B2) * (g * g)
    m_hat = nm * (1.0 / (1.0 - ADAM_B1 ** ADAM_STEP))
    v_hat = nv * (1.0 / (1.0 - ADAM_B2 ** ADAM_STEP))
    return -ADAM_LR * (m_hat / (jnp.sqrt(v_hat) + ADAM_EPS) + ADAM_WD * w), nm, nv


def _all_gather(blocks, name):
    ride = ("gather", blocks)

    def body(*refs):
        start, mid, finish = _ride_phases(ride, *_ride_split(ride, refs, 0, 0)[:3])
        start()
        mid()
        finish()

    in_specs, out_shape, out_specs, scratch = _ride_specs(ride)
    return pl.pallas_call(
        body, name=name, out_shape=out_shape, in_specs=in_specs, out_specs=out_specs, scratch_shapes=scratch,
    )(*blocks)


def _ride_specs(ride):
    kind, arrays = ride
    n = len(arrays)
    any_spec = pl.BlockSpec(memory_space=pl.ANY)
    if kind == "gather":
        out_shape = [jax.ShapeDtypeStruct((N_DEV,) + b.shape, b.dtype) for b in arrays]
        scratch = [pltpu.SemaphoreType.DMA((7 * n,)), pltpu.SemaphoreType.DMA((7 * n,)), pltpu.SemaphoreType.DMA((n,))]
    else:
        out_shape = [jax.ShapeDtypeStruct((N_DEV - 1,) + p.shape[1:], p.dtype) for p in arrays]
        scratch = [pltpu.SemaphoreType.DMA((7 * n,)), pltpu.SemaphoreType.DMA((7 * n,))]
    return [any_spec] * n, out_shape, [any_spec] * n, scratch


def _ride_split(ride, refs, n_in, n_out):
    n = len(ride[1])
    n_sem = 3 if ride[0] == "gather" else 2
    src = refs[n_in:n_in + n]
    dst = refs[n_in + n + n_out:n_in + 2 * n + n_out]
    own = refs[:n_in] + refs[n_in + n:n_in + n + n_out] + refs[n_in + 2 * n + n_out:len(refs) - n_sem]
    return src, dst, refs[len(refs) - n_sem:], own


def _ride_phases(ride, src, dst, sems):
    n = len(src)
    x, y, c = _place()
    chips = [(1 - x, y), (x, 1 - y), (1 - x, 1 - y)]
    if ride[0] == "scatter":
        send_sems, recv_sems = sems

        def copies():
            out = []
            for f in (1, 2, 3, 5, 6, 7, 4):
                px, py, pc = x ^ (f & 1), y ^ ((f >> 1) & 1), c ^ (f >> 2)
                out += [pltpu.make_async_remote_copy(
                    src_ref=src[a].at[4 * px + 2 * py + pc], dst_ref=dst[a].at[f - 1],
                    send_sem=send_sems.at[7 * a + f - 1], recv_sem=recv_sems.at[7 * a + f - 1],
                    device_id=(px, py, pc), device_id_type=MESH) for a in range(n)]
            return out

        def start():
            for cp in copies():
                cp.start()

        def finish():
            for cp in copies():
                cp.wait()

        return start, lambda: None, finish

    send_sems, recv_sems, local_sems = sems
    me, sibling = (x, y, c), (x, y, 1 - c)

    def slot(a, px, py, pc):
        return dst[a].at[4 * px + 2 * py + pc]

    def copy(a, k, blk, to, from_src=False):
        return pltpu.make_async_remote_copy(
            src_ref=src[a] if from_src else slot(a, *blk), dst_ref=slot(a, *blk),
            send_sem=send_sems.at[7 * a + k], recv_sem=recv_sems.at[7 * a + k], device_id=to, device_id_type=MESH)

    def mine():
        return [pltpu.make_async_copy(src[a], slot(a, *me), local_sems.at[a]) for a in range(n)]

    def first():
        out = []
        for j, chip in enumerate(chips):
            out += [copy(a, 1 + j, me, (*chip, c), from_src=True) for a in range(n)]
        return out + [copy(a, 0, me, sibling, from_src=True) for a in range(n)]

    def passed():
        return [copy(a, 4 + j, (*chip, c), sibling) for j, chip in enumerate(chips) for a in range(n)]

    def start():
        for cp in mine() + first():
            cp.start()

    def mid():
        for j, chip in enumerate(chips):
            for a in range(n):
                copy(a, 1 + j, (*chip, c), me).wait_recv()
        for cp in passed():
            cp.start()

    def finish():
        for a in range(n):
            copy(a, 0, sibling, me).wait_recv()
        for j, chip in enumerate(chips):
            for a in range(n):
                copy(a, 4 + j, (*chip, 1 - c), me).wait_recv()
        for cp in first() + passed():
            cp.wait_send()
        for cp in mine():
            cp.wait()

    return start, mid, finish


def _ride_steps(ride, refs, n_in, n_out, step, n_steps):
    if ride is None:
        return refs, lambda: None
    src, dst, sems, own = _ride_split(ride, refs, n_in, n_out)
    start, mid, finish = _ride_phases(ride, src, dst, sems)
    pl.when(step == 0)(start)
    pl.when(step == (3 * n_steps) // 4)(mid)
    return own, lambda: pl.when(step == n_steps - 1)(finish)


def _scatter_and_gather(bufs, blocks, name):
    rides = (("scatter", bufs), ("gather", blocks))
    ns, ng = len(bufs), len(blocks)

    def body(*refs):
        s_src, g_src = refs[:ns], refs[ns:ns + ng]
        s_dst, g_dst = refs[ns + ng:2 * ns + ng], refs[2 * ns + ng:2 * (ns + ng)]
        sems = refs[2 * (ns + ng):]
        s_start, _, s_finish = _ride_phases(rides[0], s_src, s_dst, sems[:2])
        g_start, g_mid, g_finish = _ride_phases(rides[1], g_src, g_dst, sems[2:])
        s_start()
        g_start()
        g_mid()
        g_finish()
        s_finish()

    s_in, s_shape, s_out, s_scratch = _ride_specs(rides[0])
    g_in, g_shape, g_out, g_scratch = _ride_specs(rides[1])
    outs = pl.pallas_call(
        body, name=name, out_shape=s_shape + g_shape, in_specs=s_in + g_in, out_specs=s_out + g_out,
        scratch_shapes=s_scratch + g_scratch,
    )(*bufs, *blocks)
    return outs[:ns], outs[ns:]


def _total(own_ref, parts_ref):
    g = own_ref[...].astype(F32)
    for k in range(parts_ref.shape[0]):
        g = g + parts_ref[k].astype(F32)
    return g


def _total_adamw(buf, me, parts, w, m, v, layer, carry, name):
    _, r, cdim = buf.shape
    n_layers = w.shape[0]
    tr = _rows(r, 256)
    steps = r // tr

    def body(me_ref, own_ref, parts_ref, w_ref, m_ref, v_ref, *rest):
        del me_ref
        g_ref, d_ref, nm_ref, nv_ref = rest[-4:]
        g = _total(own_ref, parts_ref)
        g_ref[...] = g
        d_ref[...], nm_ref[...], nv_ref[...] = _adamw_math(w_ref[...], g, m_ref[...], v_ref[...])

    lay = pl.BlockSpec((tr, cdim), lambda i, me_ref: (layer * steps + i, 0))
    in_specs = [pl.BlockSpec((None, tr, cdim), lambda i, me_ref: (me_ref[0], i, 0)),
                pl.BlockSpec((N_DEV - 1, tr, cdim), lambda i, me_ref: (0, i, 0)), lay, lay, lay]
    operands = [me, buf, parts, *[a.reshape(n_layers * r, cdim) for a in (w, m, v)]]
    aliases = {}
    if carry is not None:
        in_specs += [pl.BlockSpec(memory_space=pl.ANY)] * 4
        operands += list(carry)
        aliases = {6 + k: k for k in range(4)}
    sds = jax.ShapeDtypeStruct((n_layers * r, cdim), F32)
    return pl.pallas_call(
        body, name=name,
        grid_spec=pltpu.PrefetchScalarGridSpec(num_scalar_prefetch=1, grid=(steps,), in_specs=in_specs,
                                               out_specs=(lay,) * 4),
        out_shape=(sds,) * 4, input_output_aliases=aliases,
        compiler_params=_params(("parallel",)),
    )(*operands)


def _add_parts(buf, me, parts, name):
    _, r, cdim = buf.shape
    tr = _rows(r, 512)

    def body(me_ref, own_ref, parts_ref, out_ref):
        del me_ref
        out_ref[...] = _total(own_ref, parts_ref)

    return pl.pallas_call(
        body, name=name,
        grid_spec=pltpu.PrefetchScalarGridSpec(
            num_scalar_prefetch=1, grid=(r // tr,),
            in_specs=[pl.BlockSpec((None, tr, cdim), lambda i, me_ref: (me_ref[0], i, 0)),
                      pl.BlockSpec((N_DEV - 1, tr, cdim), lambda i, me_ref: (0, i, 0))],
            out_specs=pl.BlockSpec((tr, cdim), lambda i, me_ref: (i, 0))),
        out_shape=jax.ShapeDtypeStruct((r, cdim), F32),
        compiler_params=_params(("parallel",)),
    )(me, buf, parts)


def _sum_leading(stack, name):
    n, r, cdim = stack.shape
    tr = _rows(r, 512)

    def body(in_ref, out_ref):
        acc = in_ref[0]
        for k in range(1, n):
            acc = acc + in_ref[k]
        out_ref[...] = acc

    return pl.pallas_call(
        body, name=name, grid=(r // tr,),
        out_shape=jax.ShapeDtypeStruct((r, cdim), F32),
        in_specs=[pl.BlockSpec((n, tr, cdim), lambda i: (0, i, 0))],
        out_specs=pl.BlockSpec((tr, cdim), lambda i: (i, 0)),
        compiler_params=_params(("parallel",)),
    )(stack)


_DIMS = {"nn": (((1,), (0,)), ((), ())), "nt": (((1,), (1,)), ((), ())), "tn": (((0,), (0,)), ((), ()))}


def _stack_spec(shape, mode, layer):
    cut, l, rows = layer
    cols = shape[2]
    by_n = pl.BlockSpec((1, rows, cols), lambda i, j, k: (j, l, 0))
    by_k = pl.BlockSpec((K_SPAN, rows, cols), lambda i, j, k: (k, l, 0))
    if cut == "col":
        return (by_n, N_DEV * cols, cols, rows) if mode == "nn" else (by_k, rows, rows, K_SPAN * cols)
    return (by_k, cols, cols, K_SPAN * rows) if mode == "nn" else (by_n, N_DEV * rows, rows, cols)


def _matmul(a, b, *, mode, name, out_dtype=BF, a_act=None, epi=None, extras=(), seq=None, layer=None, tm=None,
            tn=None, into=None):
    if mode == "tn":
        kdim, m = a.shape
    else:
        m, kdim = a.shape
    if tm is None:
        tm = _pick(m, 1024 if epi != "resid_gate" else min(1024, seq))
    tk = _pick(kdim, 4096 if mode == "tn" else 1024)
    b_spec = None
    if layer is not None:
        b_spec, n, tn, tk = _stack_spec(b.shape, mode, layer)
    else:
        n = b.shape[0] if mode == "nt" else b.shape[1]
        tn = _pick(n, 1024) if tn is None else tn
    nk = kdim // tk
    a_spec = (pl.BlockSpec((tk, tm), lambda i, j, k: (k, i)) if mode == "tn"
              else pl.BlockSpec((tm, tk), lambda i, j, k: (i, k)))
    if b_spec is None:
        b_spec = (pl.BlockSpec((tn, tk), lambda i, j, k: (j, k)) if mode == "nt"
                  else pl.BlockSpec((tk, tn), lambda i, j, k: (k, j)))
    tile = pl.BlockSpec((tm, tn), lambda i, j, k: (i, j))
    in_specs, out_specs = [a_spec, b_spec], [tile]
    out_shape = [jax.ShapeDtypeStruct((m, n), out_dtype)]
    if epi == "resid_gate":
        in_specs += [tile, pl.BlockSpec((None, 1, tn), lambda i, j, k: ((i * tm) // seq, 0, j))]
        out_shape = [jax.ShapeDtypeStruct((m, n), F32), jax.ShapeDtypeStruct((m, n), BF)]
        out_specs = [tile, tile]
    elif epi in ("mul_drelu", "add"):
        in_specs += [tile]
    elif epi == "bias":
        in_specs += [pl.BlockSpec((1, tn), lambda i, j, k: (0, j))]
    n_extra, n_out = len(in_specs) - 2, len(out_specs)
    aliases, n_kept = {}, 0
    if into is not None:
        buffer, block, index_map = into
        out_dtype = buffer.dtype
        if not isinstance(buffer, jax.ShapeDtypeStruct):
            in_specs.append(pl.BlockSpec(memory_space=pl.ANY))
            extras = tuple(extras) + (buffer,)
            aliases, n_kept = {len(in_specs) - 1: 0}, 1
        out_shape = [jax.ShapeDtypeStruct(buffer.shape, buffer.dtype)]
        out_specs = [pl.BlockSpec(block, index_map)]
    dims = _DIMS[mode]

    def body(*refs):
        a_ref, b_ref = refs[:2]
        ex = refs[2:2 + n_extra]
        outs = refs[2 + n_extra + n_kept:2 + n_extra + n_kept + n_out]
        av = a_ref[...]
        if a_act == "relu2":
            t = jnp.maximum(av.astype(F32), 0.0)
            av = t * t
        elif a_act == "silu":
            t = av.astype(F32)
            av = t / (1.0 + jnp.exp(-t))
        av = av.astype(BF)
        if layer is None:
            part = lax.dot_general(av, b_ref[...].astype(BF), dims, preferred_element_type=F32)
        else:
            span = b_ref.shape[0]
            wk = av.shape[1] // span
            part = None
            for u in range(span):
                p_u = lax.dot_general(av[:, u * wk:(u + 1) * wk], b_ref[u], dims, preferred_element_type=F32)
                part = p_u if part is None else part + p_u

        def finish(acc):
            if epi == "resid_gate":
                outs[0][...] = ex[0][...] + ex[1][...] * acc
                outs[1][...] = acc.astype(BF)
            elif epi == "mul_drelu":
                outs[0][...] = (acc * (2.0 * jnp.maximum(ex[0][...].astype(F32), 0.0))).astype(out_dtype)
            elif epi == "add":
                outs[0][...] = (acc + ex[0][...].astype(F32)).astype(out_dtype)
            elif epi == "bias":
                outs[0][...] = (acc + ex[0][...]).astype(out_dtype)
            else:
                outs[0][...] = acc.astype(out_dtype)

        if nk == 1:
            finish(part)
        else:
            acc_ref = refs[-1]
            k = pl.program_id(2)

            @pl.when(k == 0)
            def _():
                acc_ref[...] = part

            @pl.when(k > 0)
            def _():
                acc_ref[...] += part

            @pl.when(k == nk - 1)
            def _():
                finish(acc_ref[...])

    res = pl.pallas_call(
        body, name=name, grid=(m // tm, n // tn, nk),
        out_shape=out_shape, in_specs=in_specs, out_specs=out_specs,
        scratch_shapes=[pltpu.VMEM((tm, tn), F32)] if nk > 1 else [],
        input_output_aliases=aliases,
        compiler_params=_params(("parallel", "parallel", "arbitrary")),
    )(a, b, *extras)
    return res if n_out > 1 else res[0]


def _norm_mod(x, gain, scale, shift, seq, name):
    t, w = x.shape
    tr = ROW_BLOCK

    def body(x_ref, g_ref, sc_ref, sh_ref, out_ref):
        xv = x_ref[...]
        rstd = lax.rsqrt(jnp.mean(xv * xv, axis=-1, keepdims=True) + NORM_EPS)
        y = xv * rstd * g_ref[...]
        out_ref[...] = (y * (1.0 + sc_ref[...]) + sh_ref[...]).astype(BF)

    per_b = pl.BlockSpec((None, 1, w), lambda i: ((i * tr) // seq, 0, 0))
    return pl.pallas_call(
        body, name=name, grid=(t // tr,),
        out_shape=jax.ShapeDtypeStruct((t, w), BF),
        in_specs=[pl.BlockSpec((tr, w), lambda i: (i, 0)), pl.BlockSpec((1, w), lambda i: (0, 0)), per_b, per_b],
        out_specs=pl.BlockSpec((tr, w), lambda i: (i, 0)),
        compiler_params=_params(("parallel",)),
    )(x, gain, scale, shift)


def _norm_mod_bwd(dh, x, gain, scale, dres, seq, name, gate=None):
    t, w = x.shape
    tr = ROW_BLOCK
    steps_per_seq = seq // tr
    nb = t // seq
    gated = gate is not None

    def body(*refs):
        dh_ref, x_ref, g_ref, sc_ref, dres_ref = refs[:5]
        dx_ref, dg_ref, dsc_ref, dsh_ref = refs[-6:-2] if gated else refs[-4:]
        i = pl.program_id(0)
        xv = x_ref[...]
        dhv = dh_ref[...].astype(F32)
        rstd = lax.rsqrt(jnp.mean(xv * xv, axis=-1, keepdims=True) + NORM_EPS)
        xhat = xv * rstd
        one_sc = 1.0 + sc_ref[...]
        g = g_ref[...]
        dxhat = dhv * (g * one_sc)
        proj = jnp.mean(dxhat * xhat, axis=-1, keepdims=True)
        dxv = dres_ref[...] + rstd * (dxhat - xhat * proj)
        dx_ref[...] = dxv
        dhx = dhv * xhat
        first = [(dg_ref, jnp.sum(dhx * one_sc, axis=0, keepdims=True))]
        per_seq = [(dsc_ref, jnp.sum(dhx * g, axis=0, keepdims=True)), (dsh_ref, jnp.sum(dhv, axis=0, keepdims=True))]
        if gated:
            y_ref, gate_ref, dy_ref, dgate_ref = refs[5], refs[6], refs[-2], refs[-1]
            dy_ref[...] = (dxv * gate_ref[...]).astype(BF)
            per_seq.append((dgate_ref, jnp.sum(dxv * y_ref[...].astype(F32), axis=0, keepdims=True)))
        for cond_new, cond_add, group in ((i == 0, i > 0, first),
                                          (i % steps_per_seq == 0, i % steps_per_seq != 0, per_seq)):
            @pl.when(cond_new)
            def _(group=group):
                for ref, part in group:
                    ref[...] = part

            @pl.when(cond_add)
            def _(group=group):
                for ref, part in group:
                    ref[...] += part

    row = pl.BlockSpec((tr, w), lambda i: (i, 0))
    per_b = pl.BlockSpec((None, 1, w), lambda i: ((i * tr) // seq, 0, 0))
    vec = pl.BlockSpec((1, w), lambda i: (0, 0))
    out_shape = [jax.ShapeDtypeStruct((t, w), F32), jax.ShapeDtypeStruct((1, w), F32),
                 jax.ShapeDtypeStruct((nb, 1, w), F32), jax.ShapeDtypeStruct((nb, 1, w), F32)]
    in_specs, out_specs, operands = [row, row, vec, per_b, row], [row, vec, per_b, per_b], [dh, x, gain, scale, dres]
    if gated:
        in_specs += [row, per_b]
        operands += list(gate)
        out_shape += [jax.ShapeDtypeStruct((t, w), BF), jax.ShapeDtypeStruct((nb, 1, w), F32)]
        out_specs += [row, per_b]
    return pl.pallas_call(
        body, name=name, grid=(t // tr,),
        out_shape=out_shape, in_specs=in_specs, out_specs=out_specs,
        compiler_params=_params(("arbitrary",)),
    )(*operands)


def _gate_bwd(dx, y, gate, seq, name):
    t, w = dx.shape
    tr = ROW_BLOCK
    steps_per_seq = seq // tr
    nb = t // seq

    def body(dx_ref, y_ref, g_ref, dy_ref, dg_ref):
        i = pl.program_id(0)
        dxv = dx_ref[...]
        dy_ref[...] = (dxv * g_ref[...]).astype(BF)
        part = jnp.sum(dxv * y_ref[...].astype(F32), axis=0, keepdims=True)

        @pl.when(i % steps_per_seq == 0)
        def _():
            dg_ref[...] = part

        @pl.when(i % steps_per_seq != 0)
        def _():
            dg_ref[...] += part

    row = pl.BlockSpec((tr, w), lambda i: (i, 0))
    per_b = pl.BlockSpec((None, 1, w), lambda i: ((i * tr) // seq, 0, 0))
    return pl.pallas_call(
        body, name=name, grid=(t // tr,),
        out_shape=(jax.ShapeDtypeStruct((t, w), BF), jax.ShapeDtypeStruct((nb, 1, w), F32)),
        in_specs=[row, row, per_b], out_specs=(row, per_b),
        compiler_params=_params(("arbitrary",)),
    )(dx, y, gate)


def _loss_head(x, gain, target, name):
    t, w = x.shape
    tr = ROW_BLOCK

    def body(x_ref, g_ref, t_ref, loss_ref, dx_ref, dg_ref):
        i = pl.program_id(0)
        xv = x_ref[...]
        g = g_ref[...]
        rstd = lax.rsqrt(jnp.mean(xv * xv, axis=-1, keepdims=True) + NORM_EPS)
        xhat = xv * rstd
        err = xhat * g - t_ref[...]
        row_loss = jnp.sum(err * err, axis=-1, keepdims=True) * (0.5 / w)
        loss_part = jnp.broadcast_to(jnp.sum(row_loss, axis=0, keepdims=True), (1, LANES))
        dy = err * (1.0 / w)
        dg_part = jnp.sum(dy * xhat, axis=0, keepdims=True)
        dxhat = dy * g
        proj = jnp.mean(dxhat * xhat, axis=-1, keepdims=True)
        dx_ref[...] = rstd * (dxhat - xhat * proj)

        @pl.when(i == 0)
        def _():
            loss_ref[...] = loss_part
            dg_ref[...] = dg_part

        @pl.when(i > 0)
        def _():
            loss_ref[...] += loss_part
            dg_ref[...] += dg_part

    row = pl.BlockSpec((tr, w), lambda i: (i, 0))
    vec = pl.BlockSpec((1, w), lambda i: (0, 0))
    return pl.pallas_call(
        body, name=name, grid=(t // tr,),
        out_shape=(jax.ShapeDtypeStruct((1, LANES), F32), jax.ShapeDtypeStruct((t, w), F32),
                   jax.ShapeDtypeStruct((1, w), F32)),
        in_specs=[row, vec, row],
        out_specs=(pl.BlockSpec((1, LANES), lambda i: (0, 0)), row, vec),
        compiler_params=_params(("arbitrary",)),
    )(x, gain, target)


def _rope_group(xg, cos_p, sin_a, sin_b):
    return (xg * cos_p + pltpu.roll(xg, LANES - ROPE_HALF, axis=1) * sin_a
            + pltpu.roll(xg, ROPE_HALF, axis=1) * sin_b)


def _rope(x, tables, name, out_dtype=BF):
    t, w = x.shape
    tr = ROW_BLOCK
    groups = w // LANES

    def body(x_ref, c_ref, a_ref, b_ref, out_ref):
        cos_p, sin_a, sin_b = c_ref[...], a_ref[...], b_ref[...]
        for g in range(groups):
            sl = slice(g * LANES, (g + 1) * LANES)
            out_ref[:, sl] = _rope_group(x_ref[:, sl].astype(F32), cos_p, sin_a, sin_b).astype(out_dtype)

    row = pl.BlockSpec((tr, w), lambda i: (i, 0))
    tab = pl.BlockSpec((tr, LANES), lambda i: (i, 0))
    return pl.pallas_call(
        body, name=name, grid=(t // tr,),
        out_shape=jax.ShapeDtypeStruct((t, w), out_dtype),
        in_specs=[row, tab, tab, tab], out_specs=row,
        compiler_params=_params(("parallel",)),
    )(x, *tables)


def _mla_mid(down, gq, gkv, tables, name):
    t = down.shape[0]
    tr = ROW_BLOCK

    def body(d_ref, gq_ref, gkv_ref, c_ref, a_ref, b_ref, cq_ref, ckr_ref):
        q = d_ref[:, 0:256]
        cq_ref[...] = (q * lax.rsqrt(jnp.mean(q * q, axis=-1, keepdims=True) + NORM_EPS) * gq_ref[...]).astype(BF)
        kv = d_ref[:, 256:384]
        ckr_ref[:, 0:128] = (kv * lax.rsqrt(jnp.mean(kv * kv, axis=-1, keepdims=True) + NORM_EPS)
                             * gkv_ref[...]).astype(BF)
        ckr_ref[:, 128:256] = _rope_group(d_ref[:, 384:512], c_ref[...], a_ref[...], b_ref[...]).astype(BF)

    tab = pl.BlockSpec((tr, LANES), lambda i: (i, 0))
    return pl.pallas_call(
        body, name=name, grid=(t // tr,),
        out_shape=(jax.ShapeDtypeStruct((t, 256), BF), jax.ShapeDtypeStruct((t, 256), BF)),
        in_specs=[pl.BlockSpec((tr, 512), lambda i: (i, 0)), pl.BlockSpec((1, 256), lambda i: (0, 0)),
                  pl.BlockSpec((1, 128), lambda i: (0, 0)), tab, tab, tab],
        out_specs=(pl.BlockSpec((tr, 256), lambda i: (i, 0)), pl.BlockSpec((tr, 256), lambda i: (i, 0))),
        compiler_params=_params(("parallel",)),
    )(down, gq, gkv, *tables)


def _mla_mid_bwd(down, dcq, dckr, gq, gkv, tables_t, name):
    t = down.shape[0]
    tr = ROW_BLOCK

    def norm_bwd(xv, g, dy):
        rstd = lax.rsqrt(jnp.mean(xv * xv, axis=-1, keepdims=True) + NORM_EPS)
        xhat = xv * rstd
        dxhat = dy * g
        proj = jnp.mean(dxhat * xhat, axis=-1, keepdims=True)
        return rstd * (dxhat - xhat * proj), jnp.sum(dy * xhat, axis=0, keepdims=True)

    def body(d_ref, dcq_ref, dckr_ref, gq_ref, gkv_ref, c_ref, a_ref, b_ref, dd_ref, dgq_ref, dgkv_ref):
        i = pl.program_id(0)
        dq, dgq_part = norm_bwd(d_ref[:, 0:256], gq_ref[...], dcq_ref[...].astype(F32))
        dd_ref[:, 0:256] = dq.astype(BF)
        dkv, dgkv_part = norm_bwd(d_ref[:, 256:384], gkv_ref[...], dckr_ref[:, 0:128].astype(F32))
        dd_ref[:, 256:384] = dkv.astype(BF)
        dd_ref[:, 384:512] = _rope_group(dckr_ref[:, 128:256].astype(F32), c_ref[...], a_ref[...],
                                         b_ref[...]).astype(BF)

        @pl.when(i == 0)
        def _():
            dgq_ref[...] = dgq_part
            dgkv_ref[...] = dgkv_part

        @pl.when(i > 0)
        def _():
            dgq_ref[...] += dgq_part
            dgkv_ref[...] += dgkv_part

    tab = pl.BlockSpec((tr, LANES), lambda i: (i, 0))
    r256 = pl.BlockSpec((tr, 256), lambda i: (i, 0))
    return pl.pallas_call(
        body, name=name, grid=(t // tr,),
        out_shape=(jax.ShapeDtypeStruct((t, 512), BF), jax.ShapeDtypeStruct((1, 256), F32),
                   jax.ShapeDtypeStruct((1, 128), F32)),
        in_specs=[pl.BlockSpec((tr, 512), lambda i: (i, 0)), r256, r256, pl.BlockSpec((1, 256), lambda i: (0, 0)),
                  pl.BlockSpec((1, 128), lambda i: (0, 0)), tab, tab, tab],
        out_specs=(pl.BlockSpec((tr, 512), lambda i: (i, 0)), pl.BlockSpec((1, 256), lambda i: (0, 0)),
                   pl.BlockSpec((1, 128), lambda i: (0, 0))),
        compiler_params=_params(("arbitrary",)),
    )(down, dcq, dckr, gq, gkv, *tables_t)


def _scan_rows(x, reverse):
    s = x.shape[0]
    row = lax.broadcasted_iota(jnp.int32, x.shape, 0)
    step = 1
    while step < s:
        if reverse:
            x = x + jnp.where(row < s - step, pltpu.roll(x, s - step, axis=0), 0.0)
        else:
            x = x + jnp.where(row >= step, pltpu.roll(x, step, axis=0), 0.0)
        step *= 2
    return x


def _fox_gate(fg, b_f, seq, name):
    t = fg.shape[0]

    def body(fg_ref, b_ref, out_ref):
        z = fg_ref[...] + b_ref[...]
        log_f = jnp.minimum(z, 0.0) - jnp.log(1.0 + jnp.exp(-jnp.abs(z)))
        out_ref[...] = _scan_rows(log_f, reverse=False)

    blk = pl.BlockSpec((seq, LANES), lambda b: (b, 0))
    return pl.pallas_call(
        body, name=name, grid=(t // seq,),
        out_shape=jax.ShapeDtypeStruct((t, LANES), F32),
        in_specs=[blk, pl.BlockSpec((1, LANES), lambda b: (0, 0))], out_specs=blk,
        compiler_params=_params(("parallel",)),
    )(fg, b_f)


def _fox_gate_bwd(d_cum, fg, b_f, seq, name):
    t = fg.shape[0]

    def body(dc_ref, fg_ref, b_ref, dfg_ref, db_ref):
        b = pl.program_id(0)
        z = fg_ref[...] + b_ref[...]
        d_log_f = _scan_rows(dc_ref[...], reverse=True)
        dz = d_log_f / (1.0 + jnp.exp(z))
        dfg_ref[...] = dz
        part = jnp.sum(dz, axis=0, keepdims=True)

        @pl.when(b == 0)
        def _():
            db_ref[...] = part

        @pl.when(b > 0)
        def _():
            db_ref[...] += part

    blk = pl.BlockSpec((seq, LANES), lambda b: (b, 0))
    vec = pl.BlockSpec((1, LANES), lambda b: (0, 0))
    return pl.pallas_call(
        body, name=name, grid=(t // seq,),
        out_shape=(jax.ShapeDtypeStruct((t, LANES), F32), jax.ShapeDtypeStruct((1, LANES), F32)),
        in_specs=[blk, blk, vec], out_specs=(blk, vec),
        compiler_params=_params(("arbitrary",)),
    )(d_cum, fg, b_f)


def _head_masks():
    lane = lax.broadcasted_iota(jnp.int32, (1, LANES), 1)
    return lane < HEAD_DIM, lane >= HEAD_DIM


def _pair_operands(ref, r0, n, compact, masks, masked):
    if not compact:
        return [ref[pl.ds(r0, n), h * LANES:(h + 1) * LANES] for h in range(2)]
    pair = ref[pl.ds(r0, n), :]
    return [jnp.where(mk, pair, jnp.zeros_like(pair)) for mk in masks] if masked else [pair, pair]


def _causal(n_rows, n_cols, shift):
    return (lax.broadcasted_iota(jnp.int32, (n_rows, n_cols), 1)
            <= lax.broadcasted_iota(jnp.int32, (n_rows, n_cols), 0) + shift)


def _attn_fwd(q_arr, q_off, k_arr, k_off, v_arr, v_off, bias, seq, name, ride=None, compact=False):
    t = q_arr.shape[0]
    nb = t // seq
    blk = min(ATTN_BLOCK, seq)
    nq = seq // blk
    qw = LANES if compact else 2 * LANES
    has_bias = bias is not None
    n_in, n_out = (4, 3) if has_bias else (3, 2)

    def body(*refs):
        step = pl.program_id(0) * HEAD_PAIRS + pl.program_id(1)
        refs, ride_end = _ride_steps(ride, refs, n_in, n_out, step, nb * HEAD_PAIRS)
        if has_bias:
            q_ref, k_ref, v_ref, bias_ref, o_ref, lse_ref, o32_ref = refs
        else:
            q_ref, k_ref, v_ref, o_ref, lse_ref = refs
        masks = _head_masks()
        lo = masks[0]

        def update(r0, n, carry, k0, nk, mask):
            qs = _pair_operands(q_ref, r0, n, compact, masks, True)
            ks = _pair_operands(k_ref, k0, nk, compact, masks, False)
            vv = v_ref[pl.ds(k0, nk), :]
            vs = [jnp.where(mk, vv, jnp.zeros_like(vv)) for mk in masks]
            new, alphas, pv = [], [], None
            for h in range(2):
                m, l = carry[1 + 2 * h], carry[2 + 2 * h]
                s = lax.dot_general(qs[h], ks[h], _DIMS["nt"], preferred_element_type=F32)
                if has_bias:
                    s = s + bias_ref[h, 0:1, pl.ds(k0, nk)]
                if mask is not None:
                    s = jnp.where(mask, s, -jnp.inf)
                m_new = jnp.maximum(m, jnp.max(s, axis=-1, keepdims=True))
                p = jnp.exp(s - m_new)
                alpha = jnp.exp(m - m_new)
                l_new = alpha * l + jnp.sum(p, axis=-1, keepdims=True)
                p_hi = p.astype(BF)
                d = jnp.dot(p_hi, vs[h], preferred_element_type=F32)
                if has_bias:
                    p_lo = (p - p_hi.astype(F32)).astype(BF)
                    d = d + jnp.dot(p_lo, vs[h], preferred_element_type=F32)
                pv = d if pv is None else pv + d
                alphas.append(alpha)
                new += [m_new, l_new]
            return (carry[0] * jnp.where(lo, alphas[0], alphas[1]) + pv, *new)

        def q_block(iq, _):
            q0 = pl.multiple_of(iq * blk, blk)
            init = (jnp.zeros((blk, LANES), F32),
                    jnp.full((blk, 1), -jnp.inf, F32), jnp.zeros((blk, 1), F32),
                    jnp.full((blk, 1), -jnp.inf, F32), jnp.zeros((blk, 1), F32))
            full = lambda j, c: update(q0, blk, c, pl.multiple_of(j * blk, blk), blk, None)
            carry = lax.fori_loop(0, iq // 2, lambda jj, c: full(2 * jj + 1, full(2 * jj, c)), init)

            def last(c, odd):
                if odd:
                    c = full(iq - 1, c)
                acc, m0, l0, m1, l1 = update(q0, blk, c, q0, blk, _causal(blk, blk, 0))
                o_val = acc / jnp.where(lo, l0, l1)
                o_ref[pl.ds(q0, blk), :] = o_val.astype(BF)
                if has_bias:
                    o32_ref[pl.ds(q0, blk), :] = o_val
                lse_ref[pl.ds(q0, blk), :] = jnp.where(lo, m0 + jnp.log(l0), m1 + jnp.log(l1))
                return 0

            return lax.cond(iq % 2 == 1, functools.partial(last, odd=True), functools.partial(last, odd=False), carry)

        lax.fori_loop(0, nq, q_block, 0)
        ride_end()

    in_specs = [pl.BlockSpec((seq, qw), lambda b, p: (b, q_off + p)),
                pl.BlockSpec((seq, qw), lambda b, p: (b, k_off + p)),
                pl.BlockSpec((seq, LANES), lambda b, p: (b, v_off + p))]
    args = [q_arr, k_arr, v_arr]
    if has_bias:
        in_specs.append(pl.BlockSpec((None, 2, 8, seq), lambda b, p: (b, p, 0, 0)))
        args.append(bias)
    out_blk = pl.BlockSpec((seq, LANES), lambda b, p: (b, p))
    out_shape = [jax.ShapeDtypeStruct((t, HEAD_PAIRS * LANES), BF), jax.ShapeDtypeStruct((t, HEAD_PAIRS * LANES), F32)]
    if has_bias:
        out_shape.append(jax.ShapeDtypeStruct((t, HEAD_PAIRS * LANES), F32))
    out_specs, scratch = [out_blk] * len(out_shape), []
    if ride is not None:
        r_in, r_shape, r_out, scratch = _ride_specs(ride)
        in_specs, out_shape, out_specs = in_specs + r_in, out_shape + r_shape, out_specs + r_out
        args += list(ride[1])
    return pl.pallas_call(
        body, name=name, grid=(nb, HEAD_PAIRS),
        out_shape=out_shape, in_specs=in_specs, out_specs=out_specs, scratch_shapes=scratch,
        compiler_params=_params(("arbitrary", "arbitrary")),
    )(*args)


def _attn_bwd(q_arr, q_off, k_arr, k_off, v_arr, v_off, bias, o, do, lse, seq, name, ride=None, compact=False):
    t = q_arr.shape[0]
    nb = t // seq
    blk = min(ATTN_BLOCK, seq)
    half = blk // 2
    nq = seq // blk
    qw = LANES if compact else 2 * LANES
    has_bias = bias is not None
    n_in, n_out = (7, 4) if has_bias else (6, 3)

    def body(*refs):
        step = pl.program_id(0) * HEAD_PAIRS + pl.program_id(1)
        refs, ride_end = _ride_steps(ride, refs, n_in, n_out, step, nb * HEAD_PAIRS)
        if has_bias:
            (q_ref, k_ref, v_ref, bias_ref, o_ref, do_ref, lse_ref,
             dq_ref, dk_ref, dv_ref, dbias_ref, dq_acc, dsum) = refs
        else:
            (q_ref, k_ref, v_ref, o_ref, do_ref, lse_ref, dq_ref, dk_ref, dv_ref, dq_acc, dsum) = refs
        masks = _head_masks()
        lo, hi = masks
        dq_acc[...] = jnp.zeros_like(dq_acc)

        def prep(iq, _):
            q0 = pl.multiple_of(iq * blk, blk)
            prod = do_ref[pl.ds(q0, blk), :].astype(F32) * o_ref[pl.ds(q0, blk), :].astype(F32)
            d0 = jnp.sum(jnp.where(lo, prod, 0.0), axis=-1, keepdims=True)
            d1 = jnp.sum(jnp.where(hi, prod, 0.0), axis=-1, keepdims=True)
            dsum[pl.ds(q0, blk), :] = jnp.where(lo, d0, d1)
            return 0

        lax.fori_loop(0, nq, prep, 0)

        def tile(r0, n, k0, nk, mask):
            qs = _pair_operands(q_ref, r0, n, compact, masks, True)
            ks = _pair_operands(k_ref, k0, nk, compact, masks, True)
            vv = v_ref[pl.ds(k0, nk), :]
            vs = [jnp.where(mk, vv, jnp.zeros_like(vv)) for mk in masks]
            dov = do_ref[pl.ds(r0, n), :]
            dos = [jnp.where(mk, dov, jnp.zeros_like(dov)) for mk in masks] if compact else None
            lse_v = lse_ref[pl.ds(r0, n), :]
            dsum_v = dsum[pl.ds(r0, n), :]
            dv_c, dks, dbs = None, [], []
            for h in range(2):
                s = lax.dot_general(qs[h], ks[h], _DIMS["nt"], preferred_element_type=F32)
                if has_bias:
                    s = s + bias_ref[h, 0:1, pl.ds(k0, nk)]
                p = jnp.exp(s - lse_v[:, h * HEAD_DIM:h * HEAD_DIM + 1])
                if mask is not None:
                    p = jnp.where(mask, p, 0.0)
                dp = lax.dot_general(dov, vs[h], _DIMS["nt"], preferred_element_type=F32)
                ds = p * (dp - dsum_v[:, h * HEAD_DIM:h * HEAD_DIM + 1])
                ds_bf = ds.astype(BF)
                if compact:
                    dv_h = lax.dot_general(p.astype(BF), dos[h], _DIMS["tn"], preferred_element_type=F32)
                else:
                    dv_h = jnp.where(masks[h], lax.dot_general(p.astype(BF), dov, _DIMS["tn"],
                                                               preferred_element_type=F32), 0.0)
                dv_c = dv_h if dv_c is None else dv_c + dv_h
                dk_h = lax.dot_general(ds_bf, qs[h], _DIMS["tn"], preferred_element_type=F32)
                dq_h = jnp.dot(ds_bf, ks[h], preferred_element_type=F32)
                if compact:
                    dks = [dk_h] if h == 0 else [dks[0] + dk_h, jnp.zeros((8, LANES), F32)]
                    if h == 0:
                        dq_first = dq_h
                    else:
                        dq_acc[pl.ds(r0, n), :] += dq_first + dq_h
                else:
                    dks.append(dk_h)
                    dq_acc[pl.ds(r0, n), h * LANES:(h + 1) * LANES] += dq_h
                dbs.append(jnp.sum(ds, axis=0, keepdims=True) if has_bias else jnp.zeros((1, nk), F32))
            return (dv_c, dks[0], dks[1], dbs[0], dbs[1])

        def kv_block(j, _):
            k0 = pl.multiple_of(j * blk, blk)
            if compact:
                dv_a, dk_a, dummy, db0_a, db1_a = tile(pl.multiple_of(k0 + half, half), half, k0, blk,
                                                       _causal(half, blk, half))
                top = tile(k0, half, k0, half, _causal(half, half, 0))
                head = lambda acc, x: jnp.concatenate([acc[:half] + x, acc[half:]], axis=0)
                lead = lambda acc, x: jnp.concatenate([acc[:, :half] + x, acc[:, half:]], axis=1)
                carry = (head(dv_a, top[0]), head(dk_a, top[1]), dummy, lead(db0_a, top[3]), lead(db1_a, top[4]))
            else:
                carry = tile(k0, blk, k0, blk, _causal(blk, blk, 0))

            def q_block(iq, c):
                part = tile(pl.multiple_of(iq * blk, blk), blk, k0, blk, None)
                return tuple(a + b for a, b in zip(c, part))

            carry = lax.fori_loop(j + 1, nq, q_block, carry)
            dv_ref[pl.ds(k0, blk), :] = carry[0].astype(BF)
            if compact:
                dk_ref[pl.ds(k0, blk), :] = carry[1].astype(BF)
            else:
                for h in range(2):
                    dk_ref[pl.ds(k0, blk), h * LANES:(h + 1) * LANES] = carry[1 + h].astype(BF)
            if has_bias:
                for h in range(2):
                    dbias_ref[h, :, pl.ds(k0, blk)] = jnp.broadcast_to(carry[3 + h], (8, blk))
            return 0

        lax.fori_loop(0, nq, kv_block, 0)
        dq_ref[...] = dq_acc[...].astype(BF)
        ride_end()

    pair256 = lambda off: pl.BlockSpec((seq, qw), lambda b, p: (b, off + p))
    pair128 = lambda off: pl.BlockSpec((seq, LANES), lambda b, p: (b, off + p))
    bias_spec = pl.BlockSpec((None, 2, 8, seq), lambda b, p: (b, p, 0, 0))
    in_specs = [pair256(q_off), pair256(k_off), pair128(v_off)]
    args = [q_arr, k_arr, v_arr]
    if has_bias:
        in_specs.append(bias_spec)
        args.append(bias)
    in_specs += [pair128(0), pair128(0), pair128(0)]
    args += [o, do, lse]
    out_shape = [jax.ShapeDtypeStruct((t, HEAD_PAIRS * qw), BF),
                 jax.ShapeDtypeStruct((t, HEAD_PAIRS * qw), BF),
                 jax.ShapeDtypeStruct((t, HEAD_PAIRS * LANES), BF)]
    out_specs = [pair256(0), pair256(0), pair128(0)]
    if has_bias:
        out_shape.append(jax.ShapeDtypeStruct((nb, HEADS, 8, seq), F32))
        out_specs.append(bias_spec)
    scratch = [pltpu.VMEM((seq, qw), F32), pltpu.VMEM((seq, LANES), F32)]
    if ride is not None:
        r_in, r_shape, r_out, r_scratch = _ride_specs(ride)
        in_specs, out_shape, out_specs = in_specs + r_in, out_shape + r_shape, out_specs + r_out
        args += list(ride[1])
        scratch += r_scratch
    return pl.pallas_call(
        body, name=name, grid=(nb, HEAD_PAIRS),
        out_shape=out_shape, in_specs=in_specs, out_specs=out_specs, scratch_shapes=scratch,
        compiler_params=_params(("arbitrary", "arbitrary")),
    )(*args)


def _adamw(w, g, m, v, name):
    shape = w.shape
    last = shape[-1]
    rows = int(np.prod(shape[:-1])) if len(shape) > 1 else 1
    tr = _rows(rows, 512)

    def body(w_ref, g_ref, m_ref, v_ref, d_ref, nm_ref, nv_ref):
        d_ref[...], nm_ref[...], nv_ref[...] = _adamw_math(w_ref[...], g_ref[...], m_ref[...], v_ref[...])

    blk = pl.BlockSpec((tr, last), lambda i: (i, 0))
    sds = jax.ShapeDtypeStruct((rows, last), F32)
    outs = pl.pallas_call(
        body, name=name, grid=(rows // tr,),
        out_shape=(sds, sds, sds), in_specs=[blk] * 4, out_specs=(blk,) * 3,
        compiler_params=_params(("parallel",)),
    )(*[a.reshape(rows, last) for a in (w, g, m, v)])
    return tuple(a.reshape(shape) for a in outs)


LOW_COLS = 256


def _low_pad(a):
    return jnp.pad(a, ((0, 0),) * (a.ndim - 1) + ((0, LOW_COLS - a.shape[-1]),))


def _layer_shards(w, i):
    j = i // 2
    bf = lambda a: a.astype(BF)
    if i % 2 == 0:
        mixer = [bf(w["fox_w_in"][j]), bf(w["fox_w_out"][j])]
    else:
        mixer = [jnp.concatenate([bf(w["mla_w_dq"][j]), bf(w["mla_w_ukv"][j]), _low_pad(bf(w["mla_w_uq"][j])),
                                  _low_pad(bf(w["mla_w_dkv"][j]))], axis=0), bf(w["mla_w_out"][j])]
    return mixer + [bf(w["mlp_w1"][i]), bf(w["mlp_w2"][i])]


def _side_by_side(stack, r0, rows, cols=None):
    return jnp.concatenate([stack[dd, r0:r0 + rows, :cols] for dd in range(N_DEV)], axis=1)


def _stacked(stack, r0, rows, cols=None):
    part = stack[:, r0:r0 + rows, :cols]
    return part.reshape(N_DEV * rows, part.shape[2])


def _layer_mixer_weights(i, first):
    if i % 2 == 0:
        return dict(fox_w_in=_side_by_side(first, 0, 1024))
    return dict(mla_w_dq=_stacked(first, 0, 128), mla_w_ukv=_side_by_side(first, 128, 128),
                mla_w_uq=_side_by_side(first, 256, 256, 192), mla_w_dkv=_stacked(first, 512, 128, 160))


def _by_dest_rows(g):
    return g.reshape(N_DEV, g.shape[0] // N_DEV, g.shape[1]).astype(BF)


def _by_dest_cols(g):
    n = g.shape[1] // N_DEV
    return jnp.stack([g[:, dd * n:(dd + 1) * n] for dd in range(N_DEV)]).astype(BF)


def _layer_mixer_grad_bufs(i, g):
    if i % 2 == 0:
        return [_by_dest_cols(g["fox_w_in"]), _by_dest_rows(g["w_out"])]
    low = jnp.concatenate([_by_dest_rows(g["mla_w_dq"]), _by_dest_cols(g["mla_w_ukv"]),
                           _low_pad(_by_dest_cols(g["mla_w_uq"])), _low_pad(_by_dest_rows(g["mla_w_dkv"]))], axis=1)
    return [low, _by_dest_rows(g["w_out"])]


def _low_shard_grads(low):
    return dict(mla_w_dq=low[:128], mla_w_ukv=low[128:256], mla_w_uq=low[256:512, :192], mla_w_dkv=low[512:, :160])


def _pad_heads(w, width):
    k = w.shape[0]
    return jnp.pad(w.reshape(k, HEADS, width), ((0, 0), (0, 0), (0, LANES - width))).reshape(k, HEADS * LANES)


def _unpad_heads(w, width):
    k = w.shape[0]
    return w.reshape(k, HEADS, LANES)[:, :, :width].reshape(k, HEADS * width)


def _rope_tables(positions, scale):
    inv_freq = 10000.0 ** (-jnp.arange(0, 2 * ROPE_HALF, 2, dtype=F32) / (2 * ROPE_HALF))
    ang = positions.astype(F32)[:, None] * inv_freq
    cos, sin = jnp.cos(ang) * scale, jnp.sin(ang) * scale
    t = positions.shape[0]
    z = lambda n: jnp.zeros((t, n), F32)
    cos_p = jnp.concatenate([jnp.full((t, HEAD_DIM), scale, F32), cos, cos, z(32)], axis=1)
    sin_a = jnp.concatenate([z(64), -sin, z(48)], axis=1)
    sin_b = jnp.concatenate([z(80), sin, z(32)], axis=1)
    fwd = (cos_p, sin_a, sin_b)
    bwd = (cos_p, jnp.roll(sin_b, -ROPE_HALF, axis=1), jnp.roll(sin_a, ROPE_HALF, axis=1))
    return fwd, bwd


def _key_rows(cum, nb, seq):
    v = -cum.reshape(nb, seq, LANES)[:, :, :HEADS]
    return jnp.broadcast_to(jnp.transpose(v, (0, 2, 1))[:, :, None, :], (nb, HEADS, 8, seq))


def kernel(x, c, positions, ada_w, ada_b, norm_mix_g, norm_mlp_g, fox_w_in, fox_b_f, fox_w_out, mla_w_dq, mla_q_norm_g, mla_w_uq, mla_w_dkv, mla_kv_norm_g, mla_w_ukv, mla_w_out, mlp_w1, mlp_w2, final_norm_g, loss_target, m_ada_w, m_ada_b, m_norm_mix_g, m_norm_mlp_g, m_fox_w_in, m_fox_b_f, m_fox_w_out, m_mla_w_dq, m_mla_q_norm_g, m_mla_w_uq, m_mla_w_dkv, m_mla_kv_norm_g, m_mla_w_ukv, m_mla_w_out, m_mlp_w1, m_mlp_w2, m_final_norm_g, v_ada_w, v_ada_b, v_norm_mix_g, v_norm_mlp_g, v_fox_w_in, v_fox_b_f, v_fox_w_out, v_mla_w_dq, v_mla_q_norm_g, v_mla_w_uq, v_mla_w_dkv, v_mla_kv_norm_g, v_mla_w_ukv, v_mla_w_out, v_mlp_w1, v_mlp_w2, v_final_norm_g):
    args = dict(locals())
    weights = {n: args[n] for n in WEIGHTS}
    nb, seq, d = x.shape
    t = nb * seq
    depth = ada_w.shape[0]
    dev = 4 * lax.axis_index("x") + 2 * lax.axis_index("y") + lax.axis_index("c")
    n_mod_local = ada_w.shape[2]

    n_qg = mla_q_norm_g.shape[1]
    cond = jnp.concatenate([c, jnp.pad(mla_q_norm_g.reshape(1, -1), ((0, 7), (0, d - 2 * n_qg)))], axis=0)
    w1_rows, w2_rows = mlp_w1.shape[1], mlp_w2.shape[1]
    shards = [_layer_shards(weights, i) for i in range(depth)]
    stacks = [None] * depth
    cond_all, = _all_gather([cond], "gather_cond")
    c_all = cond_all[:, :nb].reshape(N_DEV * nb, d)
    q_gain = jnp.transpose(cond_all[:, nb, :2 * n_qg].reshape(N_DEV, 2, n_qg), (1, 0, 2)).reshape(2, N_DEV * n_qg)
    mod_local = jnp.stack([
        _matmul(c_all, ada_w[i], mode="nn", name="ada_mod", out_dtype=F32, a_act="silu", epi="bias",
                extras=(lax.dynamic_slice_in_dim(ada_b[i], dev * n_mod_local, n_mod_local)[None, :],))
        for i in range(depth)])
    mod_all, first_in = _all_gather([mod_local.reshape(depth * N_DEV * nb, n_mod_local), shards[0][0]], "gather_first")
    stacks[0] = [first_in]
    mod_all = jnp.transpose(mod_all.reshape(N_DEV, depth, N_DEV * nb, n_mod_local), (1, 2, 0, 3))
    mod_all = mod_all.reshape(depth, N_DEV * nb, N_DEV * n_mod_local)
    mod = lax.dynamic_slice_in_dim(mod_all, dev * nb, nb, axis=1)
    mod = mod.reshape(depth, nb, 6, 1, d)

    pos = positions.reshape(t)
    rope_q, rope_q_t = _rope_tables(pos, MLA_SCALE)
    rope_k, rope_k_t = _rope_tables(pos, 1.0)

    def fox_weights(full):
        w_in = full["fox_w_in"]
        w_qkv = jnp.concatenate([w_in[:, :d] * FOX_SCALE, w_in[:, d:3 * d]], axis=1)
        w_f = jnp.pad(w_in[:, 3 * d:], ((0, 0), (0, LANES - HEADS)))
        return w_qkv, w_f

    def mla_weights(full):
        w_dkv = full["mla_w_dkv"]
        w_down = jnp.concatenate([full["mla_w_dq"], w_dkv[:, :128], jnp.zeros((d, 64), BF),
                                  w_dkv[:, 128:160], jnp.zeros((d, 32), BF)], axis=1)
        w_uq = _pad_heads(full["mla_w_uq"], 96)
        w_ukv = full["mla_w_ukv"].reshape(128, HEADS, 2, HEAD_DIM)
        w_uk = jnp.pad(w_ukv[:, :, 0, :], ((0, 0), (0, 0), (0, 64))).reshape(128, HEADS * LANES)
        w_uv = w_ukv[:, :, 1, :].reshape(128, HEADS * HEAD_DIM)
        place = np.zeros((128, HEADS, LANES), np.float32)
        for i in range(2 * ROPE_HALF):
            place[64 + i, :, 64 + i] = 1.0
        bottom = jnp.concatenate([jnp.asarray(place.reshape(128, HEADS * LANES), BF),
                                  jnp.zeros((128, HEADS * HEAD_DIM), BF)], axis=1)
        w_kv = jnp.concatenate([jnp.concatenate([w_uk, w_uv], axis=1), bottom], axis=0)
        return w_down, w_uq, w_kv

    tm_big = min(2048, t)
    xs = x.reshape(t, d)
    saved = []
    for i in range(depth):
        j = i // 2
        sh_m, sc_m, g_m, sh_f, sc_f, g_f = (mod[i, :, q] for q in range(6))
        gain_mix = norm_mix_g[i][None, :]
        gain_mlp = norm_mlp_g[i][None, :]
        s = dict(x_in=xs)
        h = _norm_mod(xs, gain_mix, sc_m, sh_m, seq, "norm_mix")
        s["h"] = h
        full = _layer_mixer_weights(i, stacks[i][0])
        riders = (shards[0][1:] if i == 0 else []) + (shards[i + 1] if i + 1 < depth else [])
        ride = ("gather", riders) if riders else None
        if i % 2 == 0:
            w_qkv, w_f = fox_weights(full)
            qkv = _matmul(h, w_qkv, mode="nn", name="fox_qkv")
            fg = _matmul(h, w_f, mode="nn", name="fox_gate_logits", out_dtype=F32)
            b_f = jnp.pad(fox_b_f[j], (0, LANES - HEADS))[None, :]
            cum = _fox_gate(fg, b_f, seq, "fox_gate")
            bias = _key_rows(cum, nb, seq)
            o, lse, o32, *rode = _attn_fwd(qkv, 0, qkv, 8, qkv, 16, bias, seq, "fox_attn", ride, compact=True)
            s.update(qkv=qkv, fg=fg, b_f=b_f, bias=bias, w_qkv=w_qkv, w_f=w_f, o32=o32)
        else:
            w_down, w_uq, w_kv = mla_weights(full)
            down = _matmul(h, w_down, mode="nn", name="mla_down", out_dtype=F32)
            gq, gkv = q_gain[j][None, :], mla_kv_norm_g[j][None, :]
            cq, ckr = _mla_mid(down, gq, gkv, rope_k, "mla_mid")
            q_raw = _matmul(cq, w_uq, mode="nn", name="mla_uq", out_dtype=F32)
            q_rot = _rope(q_raw, rope_q, "mla_rope_q")
            kv = _matmul(ckr, w_kv, mode="nn", name="mla_ukv")
            o, lse, *rode = _attn_fwd(q_rot, 0, kv, 0, kv, 16, None, seq, "mla_attn", ride)
            s.update(down=down, gq=gq, gkv=gkv, cq=cq, ckr=ckr, q_rot=q_rot, kv=kv,
                     w_down=w_down, w_uq=w_uq, w_kv=w_kv)
        if i == 0:
            stacks[0], rode = stacks[0] + rode[:3], rode[3:]
        if i + 1 < depth:
            stacks[i + 1] = rode
        w_out = _stacked(stacks[i][1], 0, 128)
        xs, y = _matmul(o, w_out, mode="nn", name="attn_out", epi="resid_gate", extras=(xs, g_m), seq=seq)
        s.update(o=o, lse=lse, y=y, w_out=w_out, x_mid=xs)
        h2 = _norm_mod(xs, gain_mlp, sc_f, sh_f, seq, "norm_mlp")
        a_pre = _matmul(h2, stacks[i][2], mode="nn", name="mlp_up", layer=("col", 0, w1_rows), tm=tm_big)
        xs, y2 = _matmul(a_pre, stacks[i][3], mode="nn", name="mlp_down", layer=("row", 0, w2_rows), a_act="relu2",
                         epi="resid_gate", extras=(xs, g_f), seq=seq)
        s.update(h2=h2, a_pre=a_pre, y2=y2)
        saved.append(s)

    loss_part, dx, dg_final = _loss_head(xs, final_norm_g[None, :], loss_target.reshape(t, d), "loss_head")

    w1_cols, w1_tm = mlp_w1.shape[2], _pick(w1_rows, 1024)
    dg_mix, dg_mlp, db_f, dg_kv, dg_q = [None] * depth, [None] * depth, [None] * 2, [None] * 2, [None] * 2
    dmod = [None] * depth
    me = dev.astype(jnp.int32).reshape(1)
    chains = dict(fox_w_in=None, fox_w_out=None, mla_w_out=None, mlp_w1=None, mlp_w2=None)
    low_grads = [None] * 2

    def adam_step(name, layer):
        def land_one(buf, got):
            chains[name] = _total_adamw(buf, me, got, weights[name], args["m_" + name], args["v_" + name], layer,
                                        chains[name], "adamw_" + name)
        return land_one

    def low_step(lj):
        def land_one(buf, got):
            low_grads[lj] = _low_shard_grads(_add_parts(buf, me, got, "grads_total"))
        return land_one

    def mixer_steps(li):
        lj = li // 2
        return ([adam_step("fox_w_in", lj), adam_step("fox_w_out", lj)] if li % 2 == 0
                else [low_step(lj), adam_step("mla_w_out", lj)])

    def mlp_steps(li):
        return [adam_step("mlp_w1", li), adam_step("mlp_w2", li)]

    def land(staged, got):
        for (step, buf), g in zip(staged, got):
            step(buf, g)

    waiting = []
    dy2, dg_f = _gate_bwd(dx, saved[depth - 1]["y2"], mod[depth - 1, :, 5], seq, "gate_bwd")
    for i in reversed(range(depth)):
        j = i // 2
        s = saved[i]
        sh_m, sc_m, g_m, sh_f, sc_f, g_f = (mod[i, :, q] for q in range(6))
        da_pre = _matmul(dy2, stacks[i][3], mode="nt", name="mlp_down_dx", layer=("row", 0, w2_rows), epi="mul_drelu",
                         extras=(s["a_pre"],), tm=tm_big)
        g_w2 = _matmul(s["a_pre"], dy2, mode="tn", name="mlp_down_dw", a_act="relu2", tm=w2_rows,
                       into=(jax.ShapeDtypeStruct((N_DEV, w2_rows, d), BF), (None, w2_rows, d),
                             lambda r, j, k: (r, 0, 0)))
        dh2 = _matmul(da_pre, stacks[i][2], mode="nt", name="mlp_up_dx", layer=("col", 0, w1_rows), tm=tm_big)
        g_w1 = _matmul(s["h2"], da_pre, mode="tn", name="mlp_up_dw", tm=w1_tm, tn=w1_cols,
                       into=(jax.ShapeDtypeStruct((N_DEV, w1_rows, w1_cols), BF), (None, w1_tm, w1_cols),
                             lambda r, j, k: (j, r, 0)))
        waiting += list(zip(mlp_steps(i), [g_w1, g_w2]))
        dx, dg_mlp[i], dsc_f, dsh_f, dy, dg_m = _norm_mod_bwd(dh2, s["x_mid"], norm_mlp_g[i][None, :], sc_f, dx, seq,
                                                              "norm_bwd_gate", gate=(s["y"], g_m))
        do = _matmul(dy, s["w_out"], mode="nt", name="attn_out_dx")
        dw_out = _matmul(s["o"], dy, mode="tn", name="attn_out_dw", out_dtype=F32)
        if i == 0:
            waiting.append((adam_step("fox_w_out", 0), _by_dest_rows(dw_out)))
        ride = ("scatter", [e[1] for e in waiting]) if waiting else None
        g_mixer = dict(w_out=dw_out)
        if i % 2 == 0:
            qkv = s["qkv"]
            dq, dk, dv, dbias, *rode = _attn_bwd(qkv, 0, qkv, 8, qkv, 16, s["bias"], s["o32"], do, s["lse"], seq,
                                                 "fox_attn_bwd", ride, compact=True)
            dqkv = jnp.concatenate([dq, dk, dv], axis=1)
            d_cum = -jnp.transpose(dbias[:, :, 0, :], (0, 2, 1)).reshape(t, HEADS)
            d_cum = jnp.pad(d_cum, ((0, 0), (0, LANES - HEADS)))
            dfg, db = _fox_gate_bwd(d_cum, s["fg"], s["b_f"], seq, "fox_gate_bwd")
            db_f[j] = db
            dh = _matmul(dfg, s["w_f"], mode="nt", name="fox_gate_dx", out_dtype=F32)
            dh = _matmul(dqkv, s["w_qkv"], mode="nt", name="fox_qkv_dx", epi="add", extras=(dh,))
            dw_qkv = _matmul(s["h"], dqkv, mode="tn", name="fox_qkv_dw", out_dtype=F32)
            dw_f = _matmul(s["h"], dfg, mode="tn", name="fox_gate_dw", out_dtype=F32)
            g_mixer["fox_w_in"] = jnp.concatenate([dw_qkv[:, :d] * FOX_SCALE, dw_qkv[:, d:], dw_f[:, :HEADS]], axis=1)
        else:
            kv = s["kv"]
            dq, dk, dv, *rode = _attn_bwd(s["q_rot"], 0, kv, 0, kv, 16, None, s["o"], do, s["lse"], seq,
                                          "mla_attn_bwd", ride)
            dq_raw = _rope(dq, rope_q_t, "mla_rope_q_bwd")
            dcq = _matmul(dq_raw, s["w_uq"], mode="nt", name="mla_uq_dx")
            dw_uq = _matmul(s["cq"], dq_raw, mode="tn", name="mla_uq_dw", out_dtype=F32)
            dkv = jnp.concatenate([dk, dv], axis=1)
            dckr = _matmul(dkv, s["w_kv"], mode="nt", name="mla_ukv_dx")
            dw_kv = _matmul(s["ckr"], dkv, mode="tn", name="mla_ukv_dw", out_dtype=F32)
            d_down, dgq, dgkv = _mla_mid_bwd(s["down"], dcq, dckr, s["gq"], s["gkv"], rope_k_t, "mla_mid_bwd")
            dg_q[j], dg_kv[j] = dgq, dgkv
            dh = _matmul(d_down, s["w_down"], mode="nt", name="mla_down_dx")
            dw_down = _matmul(s["h"], d_down, mode="tn", name="mla_down_dw", out_dtype=F32)
            g_mixer["mla_w_dq"] = dw_down[:, :256]
            g_mixer["mla_w_dkv"] = jnp.concatenate([dw_down[:, 256:384], dw_down[:, 448:480]], axis=1)
            g_mixer["mla_w_uq"] = _unpad_heads(dw_uq, 96)
            dk_nope = dw_kv[:128, :HEADS * LANES].reshape(128, HEADS, LANES)[:, :, :HEAD_DIM]
            dv_w = dw_kv[:128, HEADS * LANES:].reshape(128, HEADS, HEAD_DIM)
            g_mixer["mla_w_ukv"] = jnp.concatenate([dk_nope, dv_w], axis=2).reshape(128, HEADS * LANES)
        land(waiting, rode)
        this_dg_f = dg_f
        if i > 0:
            dx, dg_mix[i], dsc_m, dsh_m, dy2, dg_f = _norm_mod_bwd(
                dh, s["x_in"], norm_mix_g[i][None, :], sc_m, dx, seq, "norm_bwd_gate",
                gate=(saved[i - 1]["y2"], mod[i - 1, :, 5]))
            waiting = list(zip(mixer_steps(i), _layer_mixer_grad_bufs(i, g_mixer)))
        else:
            dx, dg_mix[i], dsc_m, dsh_m = _norm_mod_bwd(dh, s["x_in"], norm_mix_g[i][None, :], sc_m, dx, seq, "norm_bwd")
            last = [(adam_step("fox_w_in", 0), _by_dest_cols(g_mixer["fox_w_in"]))]
        dmod[i] = jnp.stack([dsh_m, dsc_m, dg_m, dsh_f, dsc_f, this_dg_f], axis=1).reshape(nb, 6 * d)

    grad_x = dx.reshape(nb, seq, d)
    shard_grads = {n: jnp.stack([low_grads[0][n], low_grads[1][n]]) for n in low_grads[0]}

    dmod_arr = jnp.stack(dmod)
    wide = lambda a: jnp.pad(a, ((0, 0), (0, d - a.shape[1])))
    pieces = [wide(loss_part), *dg_mix, *dg_mlp, *[wide(a) for a in db_f], *[wide(a) for a in dg_kv], dg_final,
              *[wide(a) for a in dg_q], jnp.sum(dmod_arr, axis=1).reshape(depth * 6, d)]
    n_small = sum(p.shape[0] for p in pieces)
    both = jnp.concatenate(pieces + [dmod_arr.reshape(depth * nb * 6, d)], axis=0)
    both = jnp.pad(both, ((0, (-both.shape[0]) % 8), (0, 0)))
    scattered, (both_all,) = _scatter_and_gather([e[1] for e in last], [both], "last_exchange")
    land(last, scattered)
    done = {n: tuple(a.reshape(weights[n].shape) for a in chain) for n, chain in chains.items()}
    total = _sum_leading(both_all, "sum_small")
    off = 0

    def take(rows):
        nonlocal off
        out = total[off:off + rows]
        off += rows
        return out

    loss = take(1)[0, 0]
    g_small = dict(
        norm_mix_g=take(depth), norm_mlp_g=take(depth), fox_b_f=take(2)[:, :HEADS], mla_kv_norm_g=take(2)[:, :128],
        final_norm_g=take(1)[0],
        mla_q_norm_g=lax.dynamic_slice_in_dim(take(2)[:, :N_DEV * n_qg], dev * n_qg, n_qg, axis=1),
        ada_b=take(depth * 6).reshape(depth, 6 * d))
    dmod_all = both_all[:, n_small:n_small + depth * nb * 6]
    dmod_all = jnp.transpose(dmod_all.reshape(N_DEV, depth, nb, 6 * d), (1, 0, 2, 3)).reshape(depth, N_DEV * nb, 6 * d)
    dmod_cols = lax.dynamic_slice_in_dim(dmod_all, dev * n_mod_local, n_mod_local, axis=2)
    g_ada_w = jnp.stack([_matmul(c_all, dmod_cols[i], mode="tn", name="ada_dw", out_dtype=F32, a_act="silu")
                         for i in range(depth)])

    all_grads = dict(shard_grads)
    all_grads.update(g_small)
    all_grads["ada_w"] = g_ada_w

    deltas, new_m, new_v = {}, {}, {}
    for n in WEIGHTS:
        if n in done:
            all_grads[n], deltas[n], new_m[n], new_v[n] = done[n]
        else:
            deltas[n], new_m[n], new_v[n] = _adamw(weights[n], all_grads[n], args["m_" + n], args["v_" + n], "adamw")

    return (loss, grad_x, *[all_grads[n] for n in WEIGHTS], *[deltas[n] for n in WEIGHTS],
            *[new_m[n] for n in WEIGHTS], *[new_v[n] for n in WEIGHTS])
```

```python
import functools
import math

import jax
import jax.numpy as jnp
import numpy as np
from jax import lax
from jax.experimental import pallas as pl
from jax.experimental.pallas import tpu as pltpu

F32 = jnp.float32
BF = jnp.bfloat16

N_DEV = 8
HEADS = 16
HEAD_PAIRS = HEADS // 2
HEAD_DIM = 64
LANES = 128
ROPE_HALF = 16
NORM_EPS = 1e-6
MLA_SCALE = 96.0 ** -0.5
FOX_SCALE = 0.125
ATTN_BLOCK = 512
ROW_BLOCK = 512
K_SPAN = 4
VMEM_LIMIT = 56 * 1024 * 1024
MESH = pl.DeviceIdType.MESH

ADAM_LR = 0.001
ADAM_B1 = 0.9
ADAM_B2 = 0.999
ADAM_EPS = 1e-08
ADAM_WD = 0.01
ADAM_STEP = 10

WEIGHTS = ("ada_w", "ada_b", "norm_mix_g", "norm_mlp_g", "fox_w_in", "fox_b_f", "fox_w_out", "mla_w_dq",
           "mla_q_norm_g", "mla_w_uq", "mla_w_dkv", "mla_kv_norm_g", "mla_w_ukv", "mla_w_out", "mlp_w1",
           "mlp_w2", "final_norm_g")


def _params(sem=None):
    return pltpu.CompilerParams(dimension_semantics=sem, vmem_limit_bytes=VMEM_LIMIT)


def _pick(n, target):
    if n <= target:
        return n
    for t in range(target, 127, -128):
        if n % t == 0:
            return t
    return n


def _rows(n, target=512):
    if n <= target:
        return n
    for t in range(target, 7, -8):
        if n % t == 0:
            return t
    return n


def _place():
    x, y, c = lax.axis_index("x"), lax.axis_index("y"), lax.axis_index("c")
    return x, y, c


def _adamw_math(w, g, m, v):
    nm = ADAM_B1 * m + (1.0 - ADAM_B1) * g
    nv = ADAM_B2 * v + (1.0 - ADAM_B2) * (g * g)
    m_hat = nm * (1.0 / (1.0 - ADAM_B1 ** ADAM_STEP))
    v_hat = nv * (1.0 / (1.0 - ADAM_B2 ** ADAM_STEP))
    return -ADAM_LR * (m_hat / (jnp.sqrt(v_hat) + ADAM_EPS) + ADAM_WD * w), nm, nv


def _all_gather(blocks, name):
    ride = ("gather", blocks)

    def body(*refs):
        start, mid, finish = _ride_phases(ride, *_ride_split(ride, refs, 0, 0)[:3])
        start()
        mid()
        finish()

    in_specs, out_shape, out_specs, scratch = _ride_specs(ride)
    return pl.pallas_call(
        body, name=name, out_shape=out_shape, in_specs=in_specs, out_specs=out_specs, scratch_shapes=scratch,
    )(*blocks)


def _ride_specs(ride):
    kind, arrays = ride
    n = len(arrays)
    any_spec = pl.BlockSpec(memory_space=pl.ANY)
    if kind == "gather":
        out_shape = [jax.ShapeDtypeStruct((N_DEV,) + b.shape, b.dtype) for b in arrays]
        scratch = [pltpu.SemaphoreType.DMA((7 * n,)), pltpu.SemaphoreType.DMA((7 * n,)), pltpu.SemaphoreType.DMA((n,))]
    else:
        out_shape = [jax.ShapeDtypeStruct((N_DEV - 1,) + p.shape[1:], p.dtype) for p in arrays]
        scratch = [pltpu.SemaphoreType.DMA((7 * n,)), pltpu.SemaphoreType.DMA((7 * n,))]
    return [any_spec] * n, out_shape, [any_spec] * n, scratch


def _ride_split(ride, refs, n_in, n_out):
    n = len(ride[1])
    n_sem = 3 if ride[0] == "gather" else 2
    src = refs[n_in:n_in + n]
    dst = refs[n_in + n + n_out:n_in + 2 * n + n_out]
    own = refs[:n_in] + refs[n_in + n:n_in + n + n_out] + refs[n_in + 2 * n + n_out:len(refs) - n_sem]
    return src, dst, refs[len(refs) - n_sem:], own


def _ride_phases(ride, src, dst, sems):
    n = len(src)
    x, y, c = _place()
    chips = [(1 - x, y), (x, 1 - y), (1 - x, 1 - y)]
    if ride[0] == "scatter":
        send_sems, recv_sems = sems

        def copies():
            out = []
            for f in (1, 2, 3, 5, 6, 7, 4):
                px, py, pc = x ^ (f & 1), y ^ ((f >> 1) & 1), c ^ (f >> 2)
                out += [pltpu.make_async_remote_copy(
                    src_ref=src[a].at[4 * px + 2 * py + pc], dst_ref=dst[a].at[f - 1],
                    send_sem=send_sems.at[7 * a + f - 1], recv_sem=recv_sems.at[7 * a + f - 1],
                    device_id=(px, py, pc), device_id_type=MESH) for a in range(n)]
            return out

        def start():
            for cp in copies():
                cp.start()

        def finish():
            for cp in copies():
                cp.wait()

        return start, lambda: None, finish

    send_sems, recv_sems, local_sems = sems
    me, sibling = (x, y, c), (x, y, 1 - c)

    def slot(a, px, py, pc):
        return dst[a].at[4 * px + 2 * py + pc]

    def copy(a, k, blk, to, from_src=False):
        return pltpu.make_async_remote_copy(
            src_ref=src[a] if from_src else slot(a, *blk), dst_ref=slot(a, *blk),
            send_sem=send_sems.at[7 * a + k], recv_sem=recv_sems.at[7 * a + k], device_id=to, device_id_type=MESH)

    def mine():
        return [pltpu.make_async_copy(src[a], slot(a, *me), local_sems.at[a]) for a in range(n)]

    def first():
        out = []
        for j, chip in enumerate(chips):
            out += [copy(a, 1 + j, me, (*chip, c), from_src=True) for a in range(n)]
        return out + [copy(a, 0, me, sibling, from_src=True) for a in range(n)]

    def passed():
        return [copy(a, 4 + j, (*chip, c), sibling) for j, chip in enumerate(chips) for a in range(n)]

    def start():
        for cp in mine() + first():
            cp.start()

    def mid():
        for j, chip in enumerate(chips):
            for a in range(n):
                copy(a, 1 + j, (*chip, c), me).wait_recv()
        for cp in passed():
            cp.start()

    def finish():
        for a in range(n):
            copy(a, 0, sibling, me).wait_recv()
        for j, chip in enumerate(chips):
            for a in range(n):
                copy(a, 4 + j, (*chip, 1 - c), me).wait_recv()
        for cp in first() + passed():
            cp.wait_send()
        for cp in mine():
            cp.wait()

    return start, mid, finish


def _ride_steps(ride, refs, n_in, n_out, step, n_steps):
    if ride is None:
        return refs, lambda: None
    src, dst, sems, own = _ride_split(ride, refs, n_in, n_out)
    start, mid, finish = _ride_phases(ride, src, dst, sems)
    pl.when(step == 0)(start)
    pl.when(step == (3 * n_steps) // 4)(mid)
    return own, lambda: pl.when(step == n_steps - 1)(finish)


def _scatter_and_gather(bufs, blocks, name):
    rides = (("scatter", bufs), ("gather", blocks))
    ns, ng = len(bufs), len(blocks)

    def body(*refs):
        s_src, g_src = refs[:ns], refs[ns:ns + ng]
        s_dst, g_dst = refs[ns + ng:2 * ns + ng], refs[2 * ns + ng:2 * (ns + ng)]
        sems = refs[2 * (ns + ng):]
        s_start, _, s_finish = _ride_phases(rides[0], s_src, s_dst, sems[:2])
        g_start, g_mid, g_finish = _ride_phases(rides[1], g_src, g_dst, sems[2:])
        s_start()
        g_start()
        g_mid()
        g_finish()
        s_finish()

    s_in, s_shape, s_out, s_scratch = _ride_specs(rides[0])
    g_in, g_shape, g_out, g_scratch = _ride_specs(rides[1])
    outs = pl.pallas_call(
        body, name=name, out_shape=s_shape + g_shape, in_specs=s_in + g_in, out_specs=s_out + g_out,
        scratch_shapes=s_scratch + g_scratch,
    )(*bufs, *blocks)
    return outs[:ns], outs[ns:]


def _total(own_ref, parts_ref):
    g = own_ref[...].astype(F32)
    for k in range(parts_ref.shape[0]):
        g = g + parts_ref[k].astype(F32)
    return g


def _total_adamw(buf, me, parts, w, m, v, layer, carry, name):
    _, r, cdim = buf.shape
    n_layers = w.shape[0]
    tr = _rows(r, 256)
    steps = r // tr

    def body(me_ref, own_ref, parts_ref, w_ref, m_ref, v_ref, *rest):
        del me_ref
        g_ref, d_ref, nm_ref, nv_ref = rest[-4:]
        g = _total(own_ref, parts_ref)
        g_ref[...] = g
        d_ref[...], nm_ref[...], nv_ref[...] = _adamw_math(w_ref[...], g, m_ref[...], v_ref[...])

    lay = pl.BlockSpec((tr, cdim), lambda i, me_ref: (layer * steps + i, 0))
    in_specs = [pl.BlockSpec((None, tr, cdim), lambda i, me_ref: (me_ref[0], i, 0)),
                pl.BlockSpec((N_DEV - 1, tr, cdim), lambda i, me_ref: (0, i, 0)), lay, lay, lay]
    operands = [me, buf, parts, *[a.reshape(n_layers * r, cdim) for a in (w, m, v)]]
    aliases = {}
    if carry is not None:
        in_specs += [pl.BlockSpec(memory_space=pl.ANY)] * 4
        operands += list(carry)
        aliases = {6 + k: k for k in range(4)}
    sds = jax.ShapeDtypeStruct((n_layers * r, cdim), F32)
    return pl.pallas_call(
        body, name=name,
        grid_spec=pltpu.PrefetchScalarGridSpec(num_scalar_prefetch=1, grid=(steps,), in_specs=in_specs,
                                               out_specs=(lay,) * 4),
        out_shape=(sds,) * 4, input_output_aliases=aliases,
        compiler_params=_params(("parallel",)),
    )(*operands)


def _add_parts(buf, me, parts, name):
    _, r, cdim = buf.shape
    tr = _rows(r, 512)

    def body(me_ref, own_ref, parts_ref, out_ref):
        del me_ref
        out_ref[...] = _total(own_ref, parts_ref)

    return pl.pallas_call(
        body, name=name,
        grid_spec=pltpu.PrefetchScalarGridSpec(
            num_scalar_prefetch=1, grid=(r // tr,),
            in_specs=[pl.BlockSpec((None, tr, cdim), lambda i, me_ref: (me_ref[0], i, 0)),
                      pl.BlockSpec((N_DEV - 1, tr, cdim), lambda i, me_ref: (0, i, 0))],
            out_specs=pl.BlockSpec((tr, cdim), lambda i, me_ref: (i, 0))),
        out_shape=jax.ShapeDtypeStruct((r, cdim), F32),
        compiler_params=_params(("parallel",)),
    )(me, buf, parts)


def _sum_leading(stack, name):
    n, r, cdim = stack.shape
    tr = _rows(r, 512)

    def body(in_ref, out_ref):
        acc = in_ref[0]
        for k in range(1, n):
            acc = acc + in_ref[k]
        out_ref[...] = acc

    return pl.pallas_call(
        body, name=name, grid=(r // tr,),
        out_shape=jax.ShapeDtypeStruct((r, cdim), F32),
        in_specs=[pl.BlockSpec((n, tr, cdim), lambda i: (0, i, 0))],
        out_specs=pl.BlockSpec((tr, cdim), lambda i: (i, 0)),
        compiler_params=_params(("parallel",)),
    )(stack)


_DIMS = {"nn": (((1,), (0,)), ((), ())), "nt": (((1,), (1,)), ((), ())), "tn": (((0,), (0,)), ((), ()))}


def _stack_spec(shape, mode, layer):
    cut, l, rows = layer
    cols = shape[2]
    by_n = pl.BlockSpec((1, rows, cols), lambda i, j, k: (j, l, 0))
    by_k = pl.BlockSpec((K_SPAN, rows, cols), lambda i, j, k: (k, l, 0))
    if cut == "col":
        return (by_n, N_DEV * cols, cols, rows) if mode == "nn" else (by_k, rows, rows, K_SPAN * cols)
    return (by_k, cols, cols, K_SPAN * rows) if mode == "nn" else (by_n, N_DEV * rows, rows, cols)


def _matmul(a, b, *, mode, name, out_dtype=BF, a_act=None, epi=None, extras=(), seq=None, layer=None, tm=None,
            tn=None, into=None):
    if mode == "tn":
        kdim, m = a.shape
    else:
        m, kdim = a.shape
    if tm is None:
        tm = _pick(m, 1024 if epi != "resid_gate" else min(1024, seq))
    tk = _pick(kdim, 4096 if mode == "tn" else 1024)
    b_spec = None
    if layer is not None:
        b_spec, n, tn, tk = _stack_spec(b.shape, mode, layer)
    else:
        n = b.shape[0] if mode == "nt" else b.shape[1]
        tn = _pick(n, 1024) if tn is None else tn
    nk = kdim // tk
    a_spec = (pl.BlockSpec((tk, tm), lambda i, j, k: (k, i)) if mode == "tn"
              else pl.BlockSpec((tm, tk), lambda i, j, k: (i, k)))
    if b_spec is None:
        b_spec = (pl.BlockSpec((tn, tk), lambda i, j, k: (j, k)) if mode == "nt"
                  else pl.BlockSpec((tk, tn), lambda i, j, k: (k, j)))
    tile = pl.BlockSpec((tm, tn), lambda i, j, k: (i, j))
    in_specs, out_specs = [a_spec, b_spec], [tile]
    out_shape = [jax.ShapeDtypeStruct((m, n), out_dtype)]
    if epi == "resid_gate":
        in_specs += [tile, pl.BlockSpec((None, 1, tn), lambda i, j, k: ((i * tm) // seq, 0, j))]
        out_shape = [jax.ShapeDtypeStruct((m, n), F32), jax.ShapeDtypeStruct((m, n), BF)]
        out_specs = [tile, tile]
    elif epi in ("mul_drelu", "add"):
        in_specs += [tile]
    elif epi == "bias":
        in_specs += [pl.BlockSpec((1, tn), lambda i, j, k: (0, j))]
    n_extra, n_out = len(in_specs) - 2, len(out_specs)
    aliases, n_kept = {}, 0
    if into is not None:
        buffer, block, index_map = into
        out_dtype = buffer.dtype
        if not isinstance(buffer, jax.ShapeDtypeStruct):
            in_specs.append(pl.BlockSpec(memory_space=pl.ANY))
            extras = tuple(extras) + (buffer,)
            aliases, n_kept = {len(in_specs) - 1: 0}, 1
        out_shape = [jax.ShapeDtypeStruct(buffer.shape, buffer.dtype)]
        out_specs = [pl.BlockSpec(block, index_map)]
    dims = _DIMS[mode]

    def body(*refs):
        a_ref, b_ref = refs[:2]
        ex = refs[2:2 + n_extra]
        outs = refs[2 + n_extra + n_kept:2 + n_extra + n_kept + n_out]
        av = a_ref[...]
        if a_act == "relu2":
            t = jnp.maximum(av.astype(F32), 0.0)
            av = t * t
        elif a_act == "silu":
            t = av.astype(F32)
            av = t / (1.0 + jnp.exp(-t))
        av = av.astype(BF)
        if layer is None:
            part = lax.dot_general(av, b_ref[...].astype(BF), dims, preferred_element_type=F32)
        else:
            span = b_ref.shape[0]
            wk = av.shape[1] // span
            part = None
            for u in range(span):
                p_u = lax.dot_general(av[:, u * wk:(u + 1) * wk], b_ref[u], dims, preferred_element_type=F32)
                part = p_u if part is None else part + p_u

        def finish(acc):
            if epi == "resid_gate":
                outs[0][...] = ex[0][...] + ex[1][...] * acc
                outs[1][...] = acc.astype(BF)
            elif epi == "mul_drelu":
                outs[0][...] = (acc * (2.0 * jnp.maximum(ex[0][...].astype(F32), 0.0))).astype(out_dtype)
            elif epi == "add":
                outs[0][...] = (acc + ex[0][...].astype(F32)).astype(out_dtype)
            elif epi == "bias":
                outs[0][...] = (acc + ex[0][...]).astype(out_dtype)
            else:
                outs[0][...] = acc.astype(out_dtype)

        if nk == 1:
            finish(part)
        else:
            acc_ref = refs[-1]
            k = pl.program_id(2)

            @pl.when(k == 0)
            def _():
                acc_ref[...] = part

            @pl.when(k > 0)
            def _():
                acc_ref[...] += part

            @pl.when(k == nk - 1)
            def _():
                finish(acc_ref[...])

    res = pl.pallas_call(
        body, name=name, grid=(m // tm, n // tn, nk),
        out_shape=out_shape, in_specs=in_specs, out_specs=out_specs,
        scratch_shapes=[pltpu.VMEM((tm, tn), F32)] if nk > 1 else [],
        input_output_aliases=aliases,
        compiler_params=_params(("parallel", "parallel", "arbitrary")),
    )(a, b, *extras)
    return res if n_out > 1 else res[0]


def _norm_mod(x, gain, scale, shift, seq, name):
    t, w = x.shape
    tr = ROW_BLOCK

    def body(x_ref, g_ref, sc_ref, sh_ref, out_ref):
        xv = x_ref[...]
        rstd = lax.rsqrt(jnp.mean(xv * xv, axis=-1, keepdims=True) + NORM_EPS)
        y = xv * rstd * g_ref[...]
        out_ref[...] = (y * (1.0 + sc_ref[...]) + sh_ref[...]).astype(BF)

    per_b = pl.BlockSpec((None, 1, w), lambda i: ((i * tr) // seq, 0, 0))
    return pl.pallas_call(
        body, name=name, grid=(t // tr,),
        out_shape=jax.ShapeDtypeStruct((t, w), BF),
        in_specs=[pl.BlockSpec((tr, w), lambda i: (i, 0)), pl.BlockSpec((1, w), lambda i: (0, 0)), per_b, per_b],
        out_specs=pl.BlockSpec((tr, w), lambda i: (i, 0)),
        compiler_params=_params(("parallel",)),
    )(x, gain, scale, shift)


def _norm_mod_bwd(dh, x, gain, scale, dres, seq, name, gate=None):
    t, w = x.shape
    tr = ROW_BLOCK
    steps_per_seq = seq // tr
    nb = t // seq
    gated = gate is not None

    def body(*refs):
        dh_ref, x_ref, g_ref, sc_ref, dres_ref = refs[:5]
        dx_ref, dg_ref, dsc_ref, dsh_ref = refs[-6:-2] if gated else refs[-4:]
        i = pl.program_id(0)
        xv = x_ref[...]
        dhv = dh_ref[...].astype(F32)
        rstd = lax.rsqrt(jnp.mean(xv * xv, axis=-1, keepdims=True) + NORM_EPS)
        xhat = xv * rstd
        one_sc = 1.0 + sc_ref[...]
        g = g_ref[...]
        dxhat = dhv * (g * one_sc)
        proj = jnp.mean(dxhat * xhat, axis=-1, keepdims=True)
        dxv = dres_ref[...] + rstd * (dxhat - xhat * proj)
        dx_ref[...] = dxv
        dhx = dhv * xhat
        first = [(dg_ref, jnp.sum(dhx * one_sc, axis=0, keepdims=True))]
        per_seq = [(dsc_ref, jnp.sum(dhx * g, axis=0, keepdims=True)), (dsh_ref, jnp.sum(dhv, axis=0, keepdims=True))]
        if gated:
            y_ref, gate_ref, dy_ref, dgate_ref = refs[5], refs[6], refs[-2], refs[-1]
            dy_ref[...] = (dxv * gate_ref[...]).astype(BF)
            per_seq.append((dgate_ref, jnp.sum(dxv * y_ref[...].astype(F32), axis=0, keepdims=True)))
        for cond_new, cond_add, group in ((i == 0, i > 0, first),
                                          (i % steps_per_seq == 0, i % steps_per_seq != 0, per_seq)):
            @pl.when(cond_new)
            def _(group=group):
                for ref, part in group:
                    ref[...] = part

            @pl.when(cond_add)
            def _(group=group):
                for ref, part in group:
                    ref[...] += part

    row = pl.BlockSpec((tr, w), lambda i: (i, 0))
    per_b = pl.BlockSpec((None, 1, w), lambda i: ((i * tr) // seq, 0, 0))
    vec = pl.BlockSpec((1, w), lambda i: (0, 0))
    out_shape = [jax.ShapeDtypeStruct((t, w), F32), jax.ShapeDtypeStruct((1, w), F32),
                 jax.ShapeDtypeStruct((nb, 1, w), F32), jax.ShapeDtypeStruct((nb, 1, w), F32)]
    in_specs, out_specs, operands = [row, row, vec, per_b, row], [row, vec, per_b, per_b], [dh, x, gain, scale, dres]
    if gated:
        in_specs += [row, per_b]
        operands += list(gate)
        out_shape += [jax.ShapeDtypeStruct((t, w), BF), jax.ShapeDtypeStruct((nb, 1, w), F32)]
        out_specs += [row, per_b]
    return pl.pallas_call(
        body, name=name, grid=(t // tr,),
        out_shape=out_shape, in_specs=in_specs, out_specs=out_specs,
        compiler_params=_params(("arbitrary",)),
    )(*operands)


def _gate_bwd(dx, y, gate, seq, name):
    t, w = dx.shape
    tr = ROW_BLOCK
    steps_per_seq = seq // tr
    nb = t // seq

    def body(dx_ref, y_ref, g_ref, dy_ref, dg_ref):
        i = pl.program_id(0)
        dxv = dx_ref[...]
        dy_ref[...] = (dxv * g_ref[...]).astype(BF)
        part = jnp.sum(dxv * y_ref[...].astype(F32), axis=0, keepdims=True)

        @pl.when(i % steps_per_seq == 0)
        def _():
            dg_ref[...] = part

        @pl.when(i % steps_per_seq != 0)
        def _():
            dg_ref[...] += part

    row = pl.BlockSpec((tr, w), lambda i: (i, 0))
    per_b = pl.BlockSpec((None, 1, w), lambda i: ((i * tr) // seq, 0, 0))
    return pl.pallas_call(
        body, name=name, grid=(t // tr,),
        out_shape=(jax.ShapeDtypeStruct((t, w), BF), jax.ShapeDtypeStruct((nb, 1, w), F32)),
        in_specs=[row, row, per_b], out_specs=(row, per_b),
        compiler_params=_params(("arbitrary",)),
    )(dx, y, gate)


def _loss_head(x, gain, target, name):
    t, w = x.shape
    tr = ROW_BLOCK

    def body(x_ref, g_ref, t_ref, loss_ref, dx_ref, dg_ref):
        i = pl.program_id(0)
        xv = x_ref[...]
        g = g_ref[...]
        rstd = lax.rsqrt(jnp.mean(xv * xv, axis=-1, keepdims=True) + NORM_EPS)
        xhat = xv * rstd
        err = xhat * g - t_ref[...]
        row_loss = jnp.sum(err * err, axis=-1, keepdims=True) * (0.5 / w)
        loss_part = jnp.broadcast_to(jnp.sum(row_loss, axis=0, keepdims=True), (1, LANES))
        dy = err * (1.0 / w)
        dg_part = jnp.sum(dy * xhat, axis=0, keepdims=True)
        dxhat = dy * g
        proj = jnp.mean(dxhat * xhat, axis=-1, keepdims=True)
        dx_ref[...] = rstd * (dxhat - xhat * proj)

        @pl.when(i == 0)
        def _():
            loss_ref[...] = loss_part
            dg_ref[...] = dg_part

        @pl.when(i > 0)
        def _():
            loss_ref[...] += loss_part
            dg_ref[...] += dg_part

    row = pl.BlockSpec((tr, w), lambda i: (i, 0))
    vec = pl.BlockSpec((1, w), lambda i: (0, 0))
    return pl.pallas_call(
        body, name=name, grid=(t // tr,),
        out_shape=(jax.ShapeDtypeStruct((1, LANES), F32), jax.ShapeDtypeStruct((t, w), F32),
                   jax.ShapeDtypeStruct((1, w), F32)),
        in_specs=[row, vec, row],
        out_specs=(pl.BlockSpec((1, LANES), lambda i: (0, 0)), row, vec),
        compiler_params=_params(("arbitrary",)),
    )(x, gain, target)


def _rope_group(xg, cos_p, sin_a, sin_b):
    return (xg * cos_p + pltpu.roll(xg, LANES - ROPE_HALF, axis=1) * sin_a
            + pltpu.roll(xg, ROPE_HALF, axis=1) * sin_b)


def _rope(x, tables, name, out_dtype=BF):
    t, w = x.shape
    tr = ROW_BLOCK
    groups = w // LANES

    def body(x_ref, c_ref, a_ref, b_ref, out_ref):
        cos_p, sin_a, sin_b = c_ref[...], a_ref[...], b_ref[...]
        for g in range(groups):
            sl = slice(g * LANES, (g + 1) * LANES)
            out_ref[:, sl] = _rope_group(x_ref[:, sl].astype(F32), cos_p, sin_a, sin_b).astype(out_dtype)

    row = pl.BlockSpec((tr, w), lambda i: (i, 0))
    tab = pl.BlockSpec((tr, LANES), lambda i: (i, 0))
    return pl.pallas_call(
        body, name=name, grid=(t // tr,),
        out_shape=jax.ShapeDtypeStruct((t, w), out_dtype),
        in_specs=[row, tab, tab, tab], out_specs=row,
        compiler_params=_params(("parallel",)),
    )(x, *tables)


def _mla_mid(down, gq, gkv, tables, name):
    t = down.shape[0]
    tr = ROW_BLOCK

    def body(d_ref, gq_ref, gkv_ref, c_ref, a_ref, b_ref, cq_ref, ckr_ref):
        q = d_ref[:, 0:256]
        cq_ref[...] = (q * lax.rsqrt(jnp.mean(q * q, axis=-1, keepdims=True) + NORM_EPS) * gq_ref[...]).astype(BF)
        kv = d_ref[:, 256:384]
        ckr_ref[:, 0:128] = (kv * lax.rsqrt(jnp.mean(kv * kv, axis=-1, keepdims=True) + NORM_EPS)
                             * gkv_ref[...]).astype(BF)
        ckr_ref[:, 128:256] = _rope_group(d_ref[:, 384:512], c_ref[...], a_ref[...], b_ref[...]).astype(BF)

    tab = pl.BlockSpec((tr, LANES), lambda i: (i, 0))
    return pl.pallas_call(
        body, name=name, grid=(t // tr,),
        out_shape=(jax.ShapeDtypeStruct((t, 256), BF), jax.ShapeDtypeStruct((t, 256), BF)),
        in_specs=[pl.BlockSpec((tr, 512), lambda i: (i, 0)), pl.BlockSpec((1, 256), lambda i: (0, 0)),
                  pl.BlockSpec((1, 128), lambda i: (0, 0)), tab, tab, tab],
        out_specs=(pl.BlockSpec((tr, 256), lambda i: (i, 0)), pl.BlockSpec((tr, 256), lambda i: (i, 0))),
        compiler_params=_params(("parallel",)),
    )(down, gq, gkv, *tables)


def _mla_mid_bwd(down, dcq, dckr, gq, gkv, tables_t, name):
    t = down.shape[0]
    tr = ROW_BLOCK

    def norm_bwd(xv, g, dy):
        rstd = lax.rsqrt(jnp.mean(xv * xv, axis=-1, keepdims=True) + NORM_EPS)
        xhat = xv * rstd
        dxhat = dy * g
        proj = jnp.mean(dxhat * xhat, axis=-1, keepdims=True)
        return rstd * (dxhat - xhat * proj), jnp.sum(dy * xhat, axis=0, keepdims=True)

    def body(d_ref, dcq_ref, dckr_ref, gq_ref, gkv_ref, c_ref, a_ref, b_ref, dd_ref, dgq_ref, dgkv_ref):
        i = pl.program_id(0)
        dq, dgq_part = norm_bwd(d_ref[:, 0:256], gq_ref[...], dcq_ref[...].astype(F32))
        dd_ref[:, 0:256] = dq.astype(BF)
        dkv, dgkv_part = norm_bwd(d_ref[:, 256:384], gkv_ref[...], dckr_ref[:, 0:128].astype(F32))
        dd_ref[:, 256:384] = dkv.astype(BF)
        dd_ref[:, 384:512] = _rope_group(dckr_ref[:, 128:256].astype(F32), c_ref[...], a_ref[...],
                                         b_ref[...]).astype(BF)

        @pl.when(i == 0)
        def _():
            dgq_ref[...] = dgq_part
            dgkv_ref[...] = dgkv_part

        @pl.when(i > 0)
        def _():
            dgq_ref[...] += dgq_part
            dgkv_ref[...] += dgkv_part

    tab = pl.BlockSpec((tr, LANES), lambda i: (i, 0))
    r256 = pl.BlockSpec((tr, 256), lambda i: (i, 0))
    return pl.pallas_call(
        body, name=name, grid=(t // tr,),
        out_shape=(jax.ShapeDtypeStruct((t, 512), BF), jax.ShapeDtypeStruct((1, 256), F32),
                   jax.ShapeDtypeStruct((1, 128), F32)),
        in_specs=[pl.BlockSpec((tr, 512), lambda i: (i, 0)), r256, r256, pl.BlockSpec((1, 256), lambda i: (0, 0)),
                  pl.BlockSpec((1, 128), lambda i: (0, 0)), tab, tab, tab],
        out_specs=(pl.BlockSpec((tr, 512), lambda i: (i, 0)), pl.BlockSpec((1, 256), lambda i: (0, 0)),
                   pl.BlockSpec((1, 128), lambda i: (0, 0))),
        compiler_params=_params(("arbitrary",)),
    )(down, dcq, dckr, gq, gkv, *tables_t)


def _scan_rows(x, reverse):
    s = x.shape[0]
    row = lax.broadcasted_iota(jnp.int32, x.shape, 0)
    step = 1
    while step < s:
        if reverse:
            x = x + jnp.where(row < s - step, pltpu.roll(x, s - step, axis=0), 0.0)
        else:
            x = x + jnp.where(row >= step, pltpu.roll(x, step, axis=0), 0.0)
        step *= 2
    return x


def _fox_gate(fg, b_f, seq, name):
    t = fg.shape[0]

    def body(fg_ref, b_ref, out_ref):
        z = fg_ref[...] + b_ref[...]
        log_f = jnp.minimum(z, 0.0) - jnp.log(1.0 + jnp.exp(-jnp.abs(z)))
        out_ref[...] = _scan_rows(log_f, reverse=False)

    blk = pl.BlockSpec((seq, LANES), lambda b: (b, 0))
    return pl.pallas_call(
        body, name=name, grid=(t // seq,),
        out_shape=jax.ShapeDtypeStruct((t, LANES), F32),
        in_specs=[blk, pl.BlockSpec((1, LANES), lambda b: (0, 0))], out_specs=blk,
        compiler_params=_params(("parallel",)),
    )(fg, b_f)


def _fox_gate_bwd(d_cum, fg, b_f, seq, name):
    t = fg.shape[0]

    def body(dc_ref, fg_ref, b_ref, dfg_ref, db_ref):
        b = pl.program_id(0)
        z = fg_ref[...] + b_ref[...]
        d_log_f = _scan_rows(dc_ref[...], reverse=True)
        dz = d_log_f / (1.0 + jnp.exp(z))
        dfg_ref[...] = dz
        part = jnp.sum(dz, axis=0, keepdims=True)

        @pl.when(b == 0)
        def _():
            db_ref[...] = part

        @pl.when(b > 0)
        def _():
            db_ref[...] += part

    blk = pl.BlockSpec((seq, LANES), lambda b: (b, 0))
    vec = pl.BlockSpec((1, LANES), lambda b: (0, 0))
    return pl.pallas_call(
        body, name=name, grid=(t // seq,),
        out_shape=(jax.ShapeDtypeStruct((t, LANES), F32), jax.ShapeDtypeStruct((1, LANES), F32)),
        in_specs=[blk, blk, vec], out_specs=(blk, vec),
        compiler_params=_params(("arbitrary",)),
    )(d_cum, fg, b_f)


def _head_masks():
    lane = lax.broadcasted_iota(jnp.int32, (1, LANES), 1)
    return lane < HEAD_DIM, lane >= HEAD_DIM


def _pair_operands(ref, r0, n, compact, masks, masked):
    if not compact:
        return [ref[pl.ds(r0, n), h * LANES:(h + 1) * LANES] for h in range(2)]
    pair = ref[pl.ds(r0, n), :]
    return [jnp.where(mk, pair, jnp.zeros_like(pair)) for mk in masks] if masked else [pair, pair]


def _causal(n_rows, n_cols, shift):
    return (lax.broadcasted_iota(jnp.int32, (n_rows, n_cols), 1)
            <= lax.broadcasted_iota(jnp.int32, (n_rows, n_cols), 0) + shift)


def _attn_fwd(q_arr, q_off, k_arr, k_off, v_arr, v_off, bias, seq, name, ride=None, compact=False):
    t = q_arr.shape[0]
    nb = t // seq
    blk = min(ATTN_BLOCK, seq)
    nq = seq // blk
    qw = LANES if compact else 2 * LANES
    has_bias = bias is not None
    n_in, n_out = (4, 3) if has_bias else (3, 2)

    def body(*refs):
        step = pl.program_id(0) * HEAD_PAIRS + pl.program_id(1)
        refs, ride_end = _ride_steps(ride, refs, n_in, n_out, step, nb * HEAD_PAIRS)
        if has_bias:
            q_ref, k_ref, v_ref, bias_ref, o_ref, lse_ref, o32_ref = refs
        else:
            q_ref, k_ref, v_ref, o_ref, lse_ref = refs
        masks = _head_masks()
        lo = masks[0]

        def update(r0, n, carry, k0, nk, mask):
            qs = _pair_operands(q_ref, r0, n, compact, masks, True)
            ks = _pair_operands(k_ref, k0, nk, compact, masks, False)
            vv = v_ref[pl.ds(k0, nk), :]
            vs = [jnp.where(mk, vv, jnp.zeros_like(vv)) for mk in masks]
            new, alphas, pv = [], [], None
            for h in range(2):
                m, l = carry[1 + 2 * h], carry[2 + 2 * h]
                s = lax.dot_general(qs[h], ks[h], _DIMS["nt"], preferred_element_type=F32)
                if has_bias:
                    s = s + bias_ref[h, 0:1, pl.ds(k0, nk)]
                if mask is not None:
                    s = jnp.where(mask, s, -jnp.inf)
                m_new = jnp.maximum(m, jnp.max(s, axis=-1, keepdims=True))
                p = jnp.exp(s - m_new)
                alpha = jnp.exp(m - m_new)
                l_new = alpha * l + jnp.sum(p, axis=-1, keepdims=True)
                p_hi = p.astype(BF)
                d = jnp.dot(p_hi, vs[h], preferred_element_type=F32)
                if has_bias:
                    p_lo = (p - p_hi.astype(F32)).astype(BF)
                    d = d + jnp.dot(p_lo, vs[h], preferred_element_type=F32)
                pv = d if pv is None else pv + d
                alphas.append(alpha)
                new += [m_new, l_new]
            return (carry[0] * jnp.where(lo, alphas[0], alphas[1]) + pv, *new)

        def q_block(iq, _):
            q0 = pl.multiple_of(iq * blk, blk)
            init = (jnp.zeros((blk, LANES), F32),
                    jnp.full((blk, 1), -jnp.inf, F32), jnp.zeros((blk, 1), F32),
                    jnp.full((blk, 1), -jnp.inf, F32), jnp.zeros((blk, 1), F32))
            full = lambda j, c: update(q0, blk, c, pl.multiple_of(j * blk, blk), blk, None)
            carry = lax.fori_loop(0, iq // 2, lambda jj, c: full(2 * jj + 1, full(2 * jj, c)), init)

            def last(c, odd):
                if odd:
                    c = full(iq - 1, c)
                acc, m0, l0, m1, l1 = update(q0, blk, c, q0, blk, _causal(blk, blk, 0))
                o_val = acc / jnp.where(lo, l0, l1)
                o_ref[pl.ds(q0, blk), :] = o_val.astype(BF)
                if has_bias:
                    o32_ref[pl.ds(q0, blk), :] = o_val
                lse_ref[pl.ds(q0, blk), :] = jnp.where(lo, m0 + jnp.log(l0), m1 + jnp.log(l1))
                return 0

            return lax.cond(iq % 2 == 1, functools.partial(last, odd=True), functools.partial(last, odd=False), carry)

        lax.fori_loop(0, nq, q_block, 0)
        ride_end()

    in_specs = [pl.BlockSpec((seq, qw), lambda b, p: (b, q_off + p)),
                pl.BlockSpec((seq, qw), lambda b, p: (b, k_off + p)),
                pl.BlockSpec((seq, LANES), lambda b, p: (b, v_off + p))]
    args = [q_arr, k_arr, v_arr]
    if has_bias:
        in_specs.append(pl.BlockSpec((None, 2, 8, seq), lambda b, p: (b, p, 0, 0)))
        args.append(bias)
    out_blk = pl.BlockSpec((seq, LANES), lambda b, p: (b, p))
    out_shape = [jax.ShapeDtypeStruct((t, HEAD_PAIRS * LANES), BF), jax.ShapeDtypeStruct((t, HEAD_PAIRS * LANES), F32)]
    if has_bias:
        out_shape.append(jax.ShapeDtypeStruct((t, HEAD_PAIRS * LANES), F32))
    out_specs, scratch = [out_blk] * len(out_shape), []
    if ride is not None:
        r_in, r_shape, r_out, scratch = _ride_specs(ride)
        in_specs, out_shape, out_specs = in_specs + r_in, out_shape + r_shape, out_specs + r_out
        args += list(ride[1])
    return pl.pallas_call(
        body, name=name, grid=(nb, HEAD_PAIRS),
        out_shape=out_shape, in_specs=in_specs, out_specs=out_specs, scratch_shapes=scratch,
        compiler_params=_params(("arbitrary", "arbitrary")),
    )(*args)


def _attn_bwd(q_arr, q_off, k_arr, k_off, v_arr, v_off, bias, o, do, lse, seq, name, ride=None, compact=False):
    t = q_arr.shape[0]
    nb = t // seq
    blk = min(ATTN_BLOCK, seq)
    half = blk // 2
    nq = seq // blk
    qw = LANES if compact else 2 * LANES
    has_bias = bias is not None
    n_in, n_out = (7, 4) if has_bias else (6, 3)

    def body(*refs):
        step = pl.program_id(0) * HEAD_PAIRS + pl.program_id(1)
        refs, ride_end = _ride_steps(ride, refs, n_in, n_out, step, nb * HEAD_PAIRS)
        if has_bias:
            (q_ref, k_ref, v_ref, bias_ref, o_ref, do_ref, lse_ref,
             dq_ref, dk_ref, dv_ref, dbias_ref, dq_acc, dsum) = refs
        else:
            (q_ref, k_ref, v_ref, o_ref, do_ref, lse_ref, dq_ref, dk_ref, dv_ref, dq_acc, dsum) = refs
        masks = _head_masks()
        lo, hi = masks
        dq_acc[...] = jnp.zeros_like(dq_acc)

        def prep(iq, _):
            q0 = pl.multiple_of(iq * blk, blk)
            prod = do_ref[pl.ds(q0, blk), :].astype(F32) * o_ref[pl.ds(q0, blk), :].astype(F32)
            d0 = jnp.sum(jnp.where(lo, prod, 0.0), axis=-1, keepdims=True)
            d1 = jnp.sum(jnp.where(hi, prod, 0.0), axis=-1, keepdims=True)
            dsum[pl.ds(q0, blk), :] = jnp.where(lo, d0, d1)
            return 0

        lax.fori_loop(0, nq, prep, 0)

        def tile(r0, n, k0, nk, mask):
            qs = _pair_operands(q_ref, r0, n, compact, masks, True)
            ks = _pair_operands(k_ref, k0, nk, compact, masks, True)
            vv = v_ref[pl.ds(k0, nk), :]
            vs = [jnp.where(mk, vv, jnp.zeros_like(vv)) for mk in masks]
            dov = do_ref[pl.ds(r0, n), :]
            dos = [jnp.where(mk, dov, jnp.zeros_like(dov)) for mk in masks] if compact else None
            lse_v = lse_ref[pl.ds(r0, n), :]
            dsum_v = dsum[pl.ds(r0, n), :]
            dv_c, dks, dbs = None, [], []
            for h in range(2):
                s = lax.dot_general(qs[h], ks[h], _DIMS["nt"], preferred_element_type=F32)
                if has_bias:
                    s = s + bias_ref[h, 0:1, pl.ds(k0, nk)]
                p = jnp.exp(s - lse_v[:, h * HEAD_DIM:h * HEAD_DIM + 1])
                if mask is not None:
                    p = jnp.where(mask, p, 0.0)
                dp = lax.dot_general(dov, vs[h], _DIMS["nt"], preferred_element_type=F32)
                ds = p * (dp - dsum_v[:, h * HEAD_DIM:h * HEAD_DIM + 1])
                ds_bf = ds.astype(BF)
                if compact:
                    dv_h = lax.dot_general(p.astype(BF), dos[h], _DIMS["tn"], preferred_element_type=F32)
                else:
                    dv_h = jnp.where(masks[h], lax.dot_general(p.astype(BF), dov, _DIMS["tn"],
                                                               preferred_element_type=F32), 0.0)
                dv_c = dv_h if dv_c is None else dv_c + dv_h
                dk_h = lax.dot_general(ds_bf, qs[h], _DIMS["tn"], preferred_element_type=F32)
                dq_h = jnp.dot(ds_bf, ks[h], preferred_element_type=F32)
                if compact:
                    dks = [dk_h] if h == 0 else [dks[0] + dk_h, jnp.zeros((8, LANES), F32)]
                    if h == 0:
                        dq_first = dq_h
                    else:
                        dq_acc[pl.ds(r0, n), :] += dq_first + dq_h
                else:
                    dks.append(dk_h)
                    dq_acc[pl.ds(r0, n), h * LANES:(h + 1) * LANES] += dq_h
                dbs.append(jnp.sum(ds, axis=0, keepdims=True) if has_bias else jnp.zeros((1, nk), F32))
            return (dv_c, dks[0], dks[1], dbs[0], dbs[1])

        def kv_block(j, _):
            k0 = pl.multiple_of(j * blk, blk)
            if compact:
                dv_a, dk_a, dummy, db0_a, db1_a = tile(pl.multiple_of(k0 + half, half), half, k0, blk,
                                                       _causal(half, blk, half))
                top = tile(k0, half, k0, half, _causal(half, half, 0))
                head = lambda acc, x: jnp.concatenate([acc[:half] + x, acc[half:]], axis=0)
                lead = lambda acc, x: jnp.concatenate([acc[:, :half] + x, acc[:, half:]], axis=1)
                carry = (head(dv_a, top[0]), head(dk_a, top[1]), dummy, lead(db0_a, top[3]), lead(db1_a, top[4]))
            else:
                carry = tile(k0, blk, k0, blk, _causal(blk, blk, 0))

            def q_block(iq, c):
                part = tile(pl.multiple_of(iq * blk, blk), blk, k0, blk, None)
                return tuple(a + b for a, b in zip(c, part))

            carry = lax.fori_loop(j + 1, nq, q_block, carry)
            dv_ref[pl.ds(k0, blk), :] = carry[0].astype(BF)
            if compact:
                dk_ref[pl.ds(k0, blk), :] = carry[1].astype(BF)
            else:
                for h in range(2):
                    dk_ref[pl.ds(k0, blk), h * LANES:(h + 1) * LANES] = carry[1 + h].astype(BF)
            if has_bias:
                for h in range(2):
                    dbias_ref[h, :, pl.ds(k0, blk)] = jnp.broadcast_to(carry[3 + h], (8, blk))
            return 0

        lax.fori_loop(0, nq, kv_block, 0)
        dq_ref[...] = dq_acc[...].astype(BF)
        ride_end()

    pair256 = lambda off: pl.BlockSpec((seq, qw), lambda b, p: (b, off + p))
    pair128 = lambda off: pl.BlockSpec((seq, LANES), lambda b, p: (b, off + p))
    bias_spec = pl.BlockSpec((None, 2, 8, seq), lambda b, p: (b, p, 0, 0))
    in_specs = [pair256(q_off), pair256(k_off), pair128(v_off)]
    args = [q_arr, k_arr, v_arr]
    if has_bias:
        in_specs.append(bias_spec)
        args.append(bias)
    in_specs += [pair128(0), pair128(0), pair128(0)]
    args += [o, do, lse]
    out_shape = [jax.ShapeDtypeStruct((t, HEAD_PAIRS * qw), BF),
                 jax.ShapeDtypeStruct((t, HEAD_PAIRS * qw), BF),
                 jax.ShapeDtypeStruct((t, HEAD_PAIRS * LANES), BF)]
    out_specs = [pair256(0), pair256(0), pair128(0)]
    if has_bias:
        out_shape.append(jax.ShapeDtypeStruct((nb, HEADS, 8, seq), F32))
        out_specs.append(bias_spec)
    scratch = [pltpu.VMEM((seq, qw), F32), pltpu.VMEM((seq, LANES), F32)]
    if ride is not None:
        r_in, r_shape, r_out, r_scratch = _ride_specs(ride)
        in_specs, out_shape, out_specs = in_specs + r_in, out_shape + r_shape, out_specs + r_out
        args += list(ride[1])
        scratch += r_scratch
    return pl.pallas_call(
        body, name=name, grid=(nb, HEAD_PAIRS),
        out_shape=out_shape, in_specs=in_specs, out_specs=out_specs, scratch_shapes=scratch,
        compiler_params=_params(("arbitrary", "arbitrary")),
    )(*args)


def _adamw(w, g, m, v, name):
    shape = w.shape
    last = shape[-1]
    rows = int(np.prod(shape[:-1])) if len(shape) > 1 else 1
    tr = _rows(rows, 512)

    def body(w_ref, g_ref, m_ref, v_ref, d_ref, nm_ref, nv_ref):
        d_ref[...], nm_ref[...], nv_ref[...] = _adamw_math(w_ref[...], g_ref[...], m_ref[...], v_ref[...])

    blk = pl.BlockSpec((tr, last), lambda i: (i, 0))
    sds = jax.ShapeDtypeStruct((rows, last), F32)
    outs = pl.pallas_call(
        body, name=name, grid=(rows // tr,),
        out_shape=(sds, sds, sds), in_specs=[blk] * 4, out_specs=(blk,) * 3,
        compiler_params=_params(("parallel",)),
    )(*[a.reshape(rows, last) for a in (w, g, m, v)])
    return tuple(a.reshape(shape) for a in outs)


LOW_COLS = 256


def _low_pad(a):
    return jnp.pad(a, ((0, 0),) * (a.ndim - 1) + ((0, LOW_COLS - a.shape[-1]),))


def _layer_shards(w, i):
    j = i // 2
    bf = lambda a: a.astype(BF)
    if i % 2 == 0:
        mixer = [bf(w["fox_w_in"][j]), bf(w["fox_w_out"][j])]
    else:
        mixer = [jnp.concatenate([bf(w["mla_w_dq"][j]), bf(w["mla_w_ukv"][j]), _low_pad(bf(w["mla_w_uq"][j])),
                                  _low_pad(bf(w["mla_w_dkv"][j]))], axis=0), bf(w["mla_w_out"][j])]
    return mixer + [bf(w["mlp_w1"][i]), bf(w["mlp_w2"][i])]


def _side_by_side(stack, r0, rows, cols=None):
    return jnp.concatenate([stack[dd, r0:r0 + rows, :cols] for dd in range(N_DEV)], axis=1)


def _stacked(stack, r0, rows, cols=None):
    part = stack[:, r0:r0 + rows, :cols]
    return part.reshape(N_DEV * rows, part.shape[2])


def _layer_mixer_weights(i, first):
    if i % 2 == 0:
        return dict(fox_w_in=_side_by_side(first, 0, 1024))
    return dict(mla_w_dq=_stacked(first, 0, 128), mla_w_ukv=_side_by_side(first, 128, 128),
                mla_w_uq=_side_by_side(first, 256, 256, 192), mla_w_dkv=_stacked(first, 512, 128, 160))


def _by_dest_rows(g):
    return g.reshape(N_DEV, g.shape[0] // N_DEV, g.shape[1]).astype(BF)


def _by_dest_cols(g):
    n = g.shape[1] // N_DEV
    return jnp.stack([g[:, dd * n:(dd + 1) * n] for dd in range(N_DEV)]).astype(BF)


def _layer_mixer_grad_bufs(i, g):
    if i % 2 == 0:
        return [_by_dest_cols(g["fox_w_in"]), _by_dest_rows(g["w_out"])]
    low = jnp.concatenate([_by_dest_rows(g["mla_w_dq"]), _by_dest_cols(g["mla_w_ukv"]),
                           _low_pad(_by_dest_cols(g["mla_w_uq"])), _low_pad(_by_dest_rows(g["mla_w_dkv"]))], axis=1)
    return [low, _by_dest_rows(g["w_out"])]


def _low_shard_grads(low):
    return dict(mla_w_dq=low[:128], mla_w_ukv=low[128:256], mla_w_uq=low[256:512, :192], mla_w_dkv=low[512:, :160])


def _pad_heads(w, width):
    k = w.shape[0]
    return jnp.pad(w.reshape(k, HEADS, width), ((0, 0), (0, 0), (0, LANES - width))).reshape(k, HEADS * LANES)


def _unpad_heads(w, width):
    k = w.shape[0]
    return w.reshape(k, HEADS, LANES)[:, :, :width].reshape(k, HEADS * width)


def _rope_tables(positions, scale):
    inv_freq = 10000.0 ** (-jnp.arange(0, 2 * ROPE_HALF, 2, dtype=F32) / (2 * ROPE_HALF))
    ang = positions.astype(F32)[:, None] * inv_freq
    cos, sin = jnp.cos(ang) * scale, jnp.sin(ang) * scale
    t = positions.shape[0]
    z = lambda n: jnp.zeros((t, n), F32)
    cos_p = jnp.concatenate([jnp.full((t, HEAD_DIM), scale, F32), cos, cos, z(32)], axis=1)
    sin_a = jnp.concatenate([z(64), -sin, z(48)], axis=1)
    sin_b = jnp.concatenate([z(80), sin, z(32)], axis=1)
    fwd = (cos_p, sin_a, sin_b)
    bwd = (cos_p, jnp.roll(sin_b, -ROPE_HALF, axis=1), jnp.roll(sin_a, ROPE_HALF, axis=1))
    return fwd, bwd


def _key_rows(cum, nb, seq):
    v = -cum.reshape(nb, seq, LANES)[:, :, :HEADS]
    return jnp.broadcast_to(jnp.transpose(v, (0, 2, 1))[:, :, None, :], (nb, HEADS, 8, seq))


def kernel(x, c, positions, ada_w, ada_b, norm_mix_g, norm_mlp_g, fox_w_in, fox_b_f, fox_w_out, mla_w_dq, mla_q_norm_g, mla_w_uq, mla_w_dkv, mla_kv_norm_g, mla_w_ukv, mla_w_out, mlp_w1, mlp_w2, final_norm_g, loss_target, m_ada_w, m_ada_b, m_norm_mix_g, m_norm_mlp_g, m_fox_w_in, m_fox_b_f, m_fox_w_out, m_mla_w_dq, m_mla_q_norm_g, m_mla_w_uq, m_mla_w_dkv, m_mla_kv_norm_g, m_mla_w_ukv, m_mla_w_out, m_mlp_w1, m_mlp_w2, m_final_norm_g, v_ada_w, v_ada_b, v_norm_mix_g, v_norm_mlp_g, v_fox_w_in, v_fox_b_f, v_fox_w_out, v_mla_w_dq, v_mla_q_norm_g, v_mla_w_uq, v_mla_w_dkv, v_mla_kv_norm_g, v_mla_w_ukv, v_mla_w_out, v_mlp_w1, v_mlp_w2, v_final_norm_g):
    args = dict(locals())
    weights = {n: args[n] for n in WEIGHTS}
    nb, seq, d = x.shape
    t = nb * seq
    depth = ada_w.shape[0]
    dev = 4 * lax.axis_index("x") + 2 * lax.axis_index("y") + lax.axis_index("c")
    n_mod_local = ada_w.shape[2]

    n_qg = mla_q_norm_g.shape[1]
    cond = jnp.concatenate([c, jnp.pad(mla_q_norm_g.reshape(1, -1), ((0, 7), (0, d - 2 * n_qg)))], axis=0)
    w1_rows, w2_rows = mlp_w1.shape[1], mlp_w2.shape[1]
    shards = [_layer_shards(weights, i) for i in range(depth)]
    stacks = [None] * depth
    cond_all, = _all_gather([cond], "gather_cond")
    c_all = cond_all[:, :nb].reshape(N_DEV * nb, d)
    q_gain = jnp.transpose(cond_all[:, nb, :2 * n_qg].reshape(N_DEV, 2, n_qg), (1, 0, 2)).reshape(2, N_DEV * n_qg)
    mod_local = jnp.stack([
        _matmul(c_all, ada_w[i], mode="nn", name="ada_mod", out_dtype=F32, a_act="silu", epi="bias",
                extras=(lax.dynamic_slice_in_dim(ada_b[i], dev * n_mod_local, n_mod_local)[None, :],))
        for i in range(depth)])
    mod_all, first_in = _all_gather([mod_local.reshape(depth * N_DEV * nb, n_mod_local), shards[0][0]], "gather_first")
    stacks[0] = [first_in]
    mod_all = jnp.transpose(mod_all.reshape(N_DEV, depth, N_DEV * nb, n_mod_local), (1, 2, 0, 3))
    mod_all = mod_all.reshape(depth, N_DEV * nb, N_DEV * n_mod_local)
    mod = lax.dynamic_slice_in_dim(mod_all, dev * nb, nb, axis=1)
    mod = mod.reshape(depth, nb, 6, 1, d)

    pos = positions.reshape(t)
    rope_q, rope_q_t = _rope_tables(pos, MLA_SCALE)
    rope_k, rope_k_t = _rope_tables(pos, 1.0)

    def fox_weights(full):
        w_in = full["fox_w_in"]
        w_qkv = jnp.concatenate([w_in[:, :d] * FOX_SCALE, w_in[:, d:3 * d]], axis=1)
        w_f = jnp.pad(w_in[:, 3 * d:], ((0, 0), (0, LANES - HEADS)))
        return w_qkv, w_f

    def mla_weights(full):
        w_dkv = full["mla_w_dkv"]
        w_down = jnp.concatenate([full["mla_w_dq"], w_dkv[:, :128], jnp.zeros((d, 64), BF),
                                  w_dkv[:, 128:160], jnp.zeros((d, 32), BF)], axis=1)
        w_uq = _pad_heads(full["mla_w_uq"], 96)
        w_ukv = full["mla_w_ukv"].reshape(128, HEADS, 2, HEAD_DIM)
        w_uk = jnp.pad(w_ukv[:, :, 0, :], ((0, 0), (0, 0), (0, 64))).reshape(128, HEADS * LANES)
        w_uv = w_ukv[:, :, 1, :].reshape(128, HEADS * HEAD_DIM)
        place = np.zeros((128, HEADS, LANES), np.float32)
        for i in range(2 * ROPE_HALF):
            place[64 + i, :, 64 + i] = 1.0
        bottom = jnp.concatenate([jnp.asarray(place.reshape(128, HEADS * LANES), BF),
                                  jnp.zeros((128, HEADS * HEAD_DIM), BF)], axis=1)
        w_kv = jnp.concatenate([jnp.concatenate([w_uk, w_uv], axis=1), bottom], axis=0)
        return w_down, w_uq, w_kv

    tm_big = min(2048, t)
    xs = x.reshape(t, d)
    saved = []
    for i in range(depth):
        j = i // 2
        sh_m, sc_m, g_m, sh_f, sc_f, g_f = (mod[i, :, q] for q in range(6))
        gain_mix = norm_mix_g[i][None, :]
        gain_mlp = norm_mlp_g[i][None, :]
        s = dict(x_in=xs)
        h = _norm_mod(xs, gain_mix, sc_m, sh_m, seq, "norm_mix")
        s["h"] = h
        full = _layer_mixer_weights(i, stacks[i][0])
        riders = shards[i][1 if i == 0 else 2:] + (shards[i + 1][:2] if i + 1 < depth else [])
        ride = ("gather", riders)
        if i % 2 == 0:
            w_qkv, w_f = fox_weights(full)
            qkv = _matmul(h, w_qkv, mode="nn", name="fox_qkv")
            fg = _matmul(h, w_f, mode="nn", name="fox_gate_logits", out_dtype=F32)
            b_f = jnp.pad(fox_b_f[j], (0, LANES - HEADS))[None, :]
            cum = _fox_gate(fg, b_f, seq, "fox_gate")
            bias = _key_rows(cum, nb, seq)
            o, lse, o32, *rode = _attn_fwd(qkv, 0, qkv, 8, qkv, 16, bias, seq, "fox_attn", ride, compact=True)
            s.update(qkv=qkv, fg=fg, b_f=b_f, bias=bias, w_qkv=w_qkv, w_f=w_f, o32=o32)
        else:
            w_down, w_uq, w_kv = mla_weights(full)
            down = _matmul(h, w_down, mode="nn", name="mla_down", out_dtype=F32)
            gq, gkv = q_gain[j][None, :], mla_kv_norm_g[j][None, :]
            cq, ckr = _mla_mid(down, gq, gkv, rope_k, "mla_mid")
            q_raw = _matmul(cq, w_uq, mode="nn", name="mla_uq", out_dtype=F32)
            q_rot = _rope(q_raw, rope_q, "mla_rope_q")
            kv = _matmul(ckr, w_kv, mode="nn", name="mla_ukv")
            o, lse, *rode = _attn_fwd(q_rot, 0, kv, 0, kv, 16, None, seq, "mla_attn", ride)
            s.update(down=down, gq=gq, gkv=gkv, cq=cq, ckr=ckr, q_rot=q_rot, kv=kv,
                     w_down=w_down, w_uq=w_uq, w_kv=w_kv)
        n_own = 3 if i == 0 else 2
        stacks[i], rode = stacks[i] + rode[:n_own], rode[n_own:]
        if i + 1 < depth:
            stacks[i + 1] = rode
        w_out = _stacked(stacks[i][1], 0, 128)
        xs, y = _matmul(o, w_out, mode="nn", name="attn_out", epi="resid_gate", extras=(xs, g_m), seq=seq)
        s.update(o=o, lse=lse, y=y, w_out=w_out, x_mid=xs)
        h2 = _norm_mod(xs, gain_mlp, sc_f, sh_f, seq, "norm_mlp")
        a_pre = _matmul(h2, stacks[i][2], mode="nn", name="mlp_up", layer=("col", 0, w1_rows), tm=tm_big)
        xs, y2 = _matmul(a_pre, stacks[i][3], mode="nn", name="mlp_down", layer=("row", 0, w2_rows), a_act="relu2",
                         epi="resid_gate", extras=(xs, g_f), seq=seq)
        s.update(h2=h2, a_pre=a_pre, y2=y2)
        saved.append(s)

    loss_part, dx, dg_final = _loss_head(xs, final_norm_g[None, :], loss_target.reshape(t, d), "loss_head")

    w1_cols, w1_tm = mlp_w1.shape[2], _pick(w1_rows, 1024)
    dg_mix, dg_mlp, db_f, dg_kv, dg_q = [None] * depth, [None] * depth, [None] * 2, [None] * 2, [None] * 2
    dmod = [None] * depth
    me = dev.astype(jnp.int32).reshape(1)
    chains = dict(fox_w_in=None, fox_w_out=None, mla_w_out=None, mlp_w1=None, mlp_w2=None)
    low_grads = [None] * 2

    def adam_step(name, layer):
        def land_one(buf, got):
            chains[name] = _total_adamw(buf, me, got, weights[name], args["m_" + name], args["v_" + name], layer,
                                        chains[name], "adamw_" + name)
        return land_one

    def low_step(lj):
        def land_one(buf, got):
            low_grads[lj] = _low_shard_grads(_add_parts(buf, me, got, "grads_total"))
        return land_one

    def mixer_steps(li):
        lj = li // 2
        return ([adam_step("fox_w_in", lj), adam_step("fox_w_out", lj)] if li % 2 == 0
                else [low_step(lj), adam_step("mla_w_out", lj)])

    def mlp_steps(li):
        return [adam_step("mlp_w1", li), adam_step("mlp_w2", li)]

    def land(staged, got):
        for (step, buf), g in zip(staged, got):
            step(buf, g)

    waiting = []
    dy2, dg_f = _gate_bwd(dx, saved[depth - 1]["y2"], mod[depth - 1, :, 5], seq, "gate_bwd")
    for i in reversed(range(depth)):
        j = i // 2
        s = saved[i]
        sh_m, sc_m, g_m, sh_f, sc_f, g_f = (mod[i, :, q] for q in range(6))
        da_pre = _matmul(dy2, stacks[i][3], mode="nt", name="mlp_down_dx", layer=("row", 0, w2_rows), epi="mul_drelu",
                         extras=(s["a_pre"],), tm=tm_big)
        g_w2 = _matmul(s["a_pre"], dy2, mode="tn", name="mlp_down_dw", a_act="relu2", tm=w2_rows,
                       into=(jax.ShapeDtypeStruct((N_DEV, w2_rows, d), BF), (None, w2_rows, d),
                             lambda r, j, k: (r, 0, 0)))
        dh2 = _matmul(da_pre, stacks[i][2], mode="nt", name="mlp_up_dx", layer=("col", 0, w1_rows), tm=tm_big)
        g_w1 = _matmul(s["h2"], da_pre, mode="tn", name="mlp_up_dw", tm=w1_tm, tn=w1_cols,
                       into=(jax.ShapeDtypeStruct((N_DEV, w1_rows, w1_cols), BF), (None, w1_tm, w1_cols),
                             lambda r, j, k: (j, r, 0)))
        waiting += list(zip(mlp_steps(i), [g_w1, g_w2]))
        dx, dg_mlp[i], dsc_f, dsh_f, dy, dg_m = _norm_mod_bwd(dh2, s["x_mid"], norm_mlp_g[i][None, :], sc_f, dx, seq,
                                                              "norm_bwd_gate", gate=(s["y"], g_m))
        do = _matmul(dy, s["w_out"], mode="nt", name="attn_out_dx")
        dw_out = _matmul(s["o"], dy, mode="tn", name="attn_out_dw", out_dtype=F32)
        if i == 0:
            waiting.append((adam_step("fox_w_out", 0), _by_dest_rows(dw_out)))
        ride = ("scatter", [e[1] for e in waiting]) if waiting else None
        g_mixer = dict(w_out=dw_out)
        if i % 2 == 0:
            qkv = s["qkv"]
            dq, dk, dv, dbias, *rode = _attn_bwd(qkv, 0, qkv, 8, qkv, 16, s["bias"], s["o32"], do, s["lse"], seq,
                                                 "fox_attn_bwd", ride, compact=True)
            dqkv = jnp.concatenate([dq, dk, dv], axis=1)
            d_cum = -jnp.transpose(dbias[:, :, 0, :], (0, 2, 1)).reshape(t, HEADS)
            d_cum = jnp.pad(d_cum, ((0, 0), (0, LANES - HEADS)))
            dfg, db = _fox_gate_bwd(d_cum, s["fg"], s["b_f"], seq, "fox_gate_bwd")
            db_f[j] = db
            dh = _matmul(dfg, s["w_f"], mode="nt", name="fox_gate_dx", out_dtype=F32)
            dh = _matmul(dqkv, s["w_qkv"], mode="nt", name="fox_qkv_dx", epi="add", extras=(dh,))
            dw_qkv = _matmul(s["h"], dqkv, mode="tn", name="fox_qkv_dw", out_dtype=F32)
            dw_f = _matmul(s["h"], dfg, mode="tn", name="fox_gate_dw", out_dtype=F32)
            g_mixer["fox_w_in"] = jnp.concatenate([dw_qkv[:, :d] * FOX_SCALE, dw_qkv[:, d:], dw_f[:, :HEADS]], axis=1)
        else:
            kv = s["kv"]
            dq, dk, dv, *rode = _attn_bwd(s["q_rot"], 0, kv, 0, kv, 16, None, s["o"], do, s["lse"], seq,
                                          "mla_attn_bwd", ride)
            dq_raw = _rope(dq, rope_q_t, "mla_rope_q_bwd")
            dcq = _matmul(dq_raw, s["w_uq"], mode="nt", name="mla_uq_dx")
            dw_uq = _matmul(s["cq"], dq_raw, mode="tn", name="mla_uq_dw", out_dtype=F32)
            dkv = jnp.concatenate([dk, dv], axis=1)
            dckr = _matmul(dkv, s["w_kv"], mode="nt", name="mla_ukv_dx")
            dw_kv = _matmul(s["ckr"], dkv, mode="tn", name="mla_ukv_dw", out_dtype=F32)
            d_down, dgq, dgkv = _mla_mid_bwd(s["down"], dcq, dckr, s["gq"], s["gkv"], rope_k_t, "mla_mid_bwd")
            dg_q[j], dg_kv[j] = dgq, dgkv
            dh = _matmul(d_down, s["w_down"], mode="nt", name="mla_down_dx")
            dw_down = _matmul(s["h"], d_down, mode="tn", name="mla_down_dw", out_dtype=F32)
            g_mixer["mla_w_dq"] = dw_down[:, :256]
            g_mixer["mla_w_dkv"] = jnp.concatenate([dw_down[:, 256:384], dw_down[:, 448:480]], axis=1)
            g_mixer["mla_w_uq"] = _unpad_heads(dw_uq, 96)
            dk_nope = dw_kv[:128, :HEADS * LANES].reshape(128, HEADS, LANES)[:, :, :HEAD_DIM]
            dv_w = dw_kv[:128, HEADS * LANES:].reshape(128, HEADS, HEAD_DIM)
            g_mixer["mla_w_ukv"] = jnp.concatenate([dk_nope, dv_w], axis=2).reshape(128, HEADS * LANES)
        land(waiting, rode)
        this_dg_f = dg_f
        if i > 0:
            dx, dg_mix[i], dsc_m, dsh_m, dy2, dg_f = _norm_mod_bwd(
                dh, s["x_in"], norm_mix_g[i][None, :], sc_m, dx, seq, "norm_bwd_gate",
                gate=(saved[i - 1]["y2"], mod[i - 1, :, 5]))
            waiting = list(zip(mixer_steps(i), _layer_mixer_grad_bufs(i, g_mixer)))
        else:
            dx, dg_mix[i], dsc_m, dsh_m = _norm_mod_bwd(dh, s["x_in"], norm_mix_g[i][None, :], sc_m, dx, seq, "norm_bwd")
            last = [(adam_step("fox_w_in", 0), _by_dest_cols(g_mixer["fox_w_in"]))]
        dmod[i] = jnp.stack([dsh_m, dsc_m, dg_m, dsh_f, dsc_f, this_dg_f], axis=1).reshape(nb, 6 * d)

    grad_x = dx.reshape(nb, seq, d)
    shard_grads = {n: jnp.stack([low_grads[0][n], low_grads[1][n]]) for n in low_grads[0]}

    dmod_arr = jnp.stack(dmod)
    wide = lambda a: jnp.pad(a, ((0, 0), (0, d - a.shape[1])))
    pieces = [wide(loss_part), *dg_mix, *dg_mlp, *[wide(a) for a in db_f], *[wide(a) for a in dg_kv], dg_final,
              *[wide(a) for a in dg_q], jnp.sum(dmod_arr, axis=1).reshape(depth * 6, d)]
    n_small = sum(p.shape[0] for p in pieces)
    both = jnp.concatenate(pieces + [dmod_arr.reshape(depth * nb * 6, d)], axis=0)
    both = jnp.pad(both, ((0, (-both.shape[0]) % 8), (0, 0)))
    scattered, (both_all,) = _scatter_and_gather([e[1] for e in last], [both], "last_exchange")
    land(last, scattered)
    done = {n: tuple(a.reshape(weights[n].shape) for a in chain) for n, chain in chains.items()}
    total = _sum_leading(both_all, "sum_small")
    off = 0

    def take(rows):
        nonlocal off
        out = total[off:off + rows]
        off += rows
        return out

    loss = take(1)[0, 0]
    g_small = dict(
        norm_mix_g=take(depth), norm_mlp_g=take(depth), fox_b_f=take(2)[:, :HEADS], mla_kv_norm_g=take(2)[:, :128],
        final_norm_g=take(1)[0],
        mla_q_norm_g=lax.dynamic_slice_in_dim(take(2)[:, :N_DEV * n_qg], dev * n_qg, n_qg, axis=1),
        ada_b=take(depth * 6).reshape(depth, 6 * d))
    dmod_all = both_all[:, n_small:n_small + depth * nb * 6]
    dmod_all = jnp.transpose(dmod_all.reshape(N_DEV, depth, nb, 6 * d), (1, 0, 2, 3)).reshape(depth, N_DEV * nb, 6 * d)
    dmod_cols = lax.dynamic_slice_in_dim(dmod_all, dev * n_mod_local, n_mod_local, axis=2)
    g_ada_w = jnp.stack([_matmul(c_all, dmod_cols[i], mode="tn", name="ada_dw", out_dtype=F32, a_act="silu")
                         for i in range(depth)])

    all_grads = dict(shard_grads)
    all_grads.update(g_small)
    all_grads["ada_w"] = g_ada_w

    deltas, new_m, new_v = {}, {}, {}
    for n in WEIGHTS:
        if n in done:
            all_grads[n], deltas[n], new_m[n], new_v[n] = done[n]
        else:
            deltas[n], new_m[n], new_v[n] = _adamw(weights[n], all_grads[n], args["m_" + n], args["v_" + n], "adamw")

    return (loss, grad_x, *[all_grads[n] for n in WEIGHTS], *[deltas[n] for n in WEIGHTS],
            *[new_m[n] for n in WEIGHTS], *[new_v[n] for n in WEIGHTS])
```

```python
import functools
import math

import jax
import jax.numpy as jnp
import numpy as np
from jax import lax
from jax.experimental import pallas as pl
from jax.experimental.pallas import tpu as pltpu

F32 = jnp.float32
BF = jnp.bfloat16

N_DEV = 8
HEADS = 16
HEAD_PAIRS = HEADS // 2
HEAD_DIM = 64
LANES = 128
ROPE_HALF = 16
NORM_EPS = 1e-6
MLA_SCALE = 96.0 ** -0.5
FOX_SCALE = 0.125
ATTN_BLOCK = 512
ROW_BLOCK = 512
K_SPAN = 4
VMEM_LIMIT = 56 * 1024 * 1024
MESH = pl.DeviceIdType.MESH

ADAM_LR = 0.001
ADAM_B1 = 0.9
ADAM_B2 = 0.999
ADAM_EPS = 1e-08
ADAM_WD = 0.01
ADAM_STEP = 10

WEIGHTS = ("ada_w", "ada_b", "norm_mix_g", "norm_mlp_g", "fox_w_in", "fox_b_f", "fox_w_out", "mla_w_dq",
           "mla_q_norm_g", "mla_w_uq", "mla_w_dkv", "mla_kv_norm_g", "mla_w_ukv", "mla_w_out", "mlp_w1",
           "mlp_w2", "final_norm_g")


def _params(sem=None):
    return pltpu.CompilerParams(dimension_semantics=sem, vmem_limit_bytes=VMEM_LIMIT)


def _pick(n, target):
    if n <= target:
        return n
    for t in range(target, 127, -128):
        if n % t == 0:
            return t
    return n


def _rows(n, target=512):
    if n <= target:
        return n
    for t in range(target, 7, -8):
        if n % t == 0:
            return t
    return n


def _place():
    x, y, c = lax.axis_index("x"), lax.axis_index("y"), lax.axis_index("c")
    return x, y, c


def _adamw_math(w, g, m, v):
    nm = ADAM_B1 * m + (1.0 - ADAM_B1) * g
    nv = ADAM_B2 * v + (1.0 - ADAM_B2) * (g * g)
    m_hat = nm * (1.0 / (1.0 - ADAM_B1 ** ADAM_STEP))
    v_hat = nv * (1.0 / (1.0 - ADAM_B2 ** ADAM_STEP))
    return -ADAM_LR * (m_hat / (jnp.sqrt(v_hat) + ADAM_EPS) + ADAM_WD * w), nm, nv


def _all_gather(blocks, name):
    ride = ("gather", blocks)

    def body(*refs):
        start, mid, finish = _ride_phases(ride, *_ride_split(ride, refs, 0, 0)[:3])
        start()
        mid()
        finish()

    in_specs, out_shape, out_specs, scratch = _ride_specs(ride)
    return pl.pallas_call(
        body, name=name, out_shape=out_shape, in_specs=in_specs, out_specs=out_specs, scratch_shapes=scratch,
    )(*blocks)


def _ride_specs(ride):
    kind, arrays = ride
    n = len(arrays)
    any_spec = pl.BlockSpec(memory_space=pl.ANY)
    if kind == "gather":
        out_shape = [jax.ShapeDtypeStruct((N_DEV,) + b.shape, b.dtype) for b in arrays]
        scratch = [pltpu.SemaphoreType.DMA((7 * n,)), pltpu.SemaphoreType.DMA((7 * n,)), pltpu.SemaphoreType.DMA((n,))]
    else:
        out_shape = [jax.ShapeDtypeStruct((N_DEV - 1,) + p.shape[1:], p.dtype) for p in arrays]
        scratch = [pltpu.SemaphoreType.DMA((7 * n,)), pltpu.SemaphoreType.DMA((7 * n,))]
    return [any_spec] * n, out_shape, [any_spec] * n, scratch


def _ride_split(ride, refs, n_in, n_out):
    n = len(ride[1])
    n_sem = 3 if ride[0] == "gather" else 2
    src = refs[n_in:n_in + n]
    dst = refs[n_in + n + n_out:n_in + 2 * n + n_out]
    own = refs[:n_in] + refs[n_in + n:n_in + n + n_out] + refs[n_in + 2 * n + n_out:len(refs) - n_sem]
    return src, dst, refs[len(refs) - n_sem:], own


def _ride_phases(ride, src, dst, sems):
    n = len(src)
    x, y, c = _place()
    chips = [(1 - x, y), (x, 1 - y), (1 - x, 1 - y)]
    if ride[0] == "scatter":
        send_sems, recv_sems = sems

        def copies():
            out = []
            for f in (1, 2, 3, 5, 6, 7, 4):
                px, py, pc = x ^ (f & 1), y ^ ((f >> 1) & 1), c ^ (f >> 2)
                out += [pltpu.make_async_remote_copy(
                    src_ref=src[a].at[4 * px + 2 * py + pc], dst_ref=dst[a].at[f - 1],
                    send_sem=send_sems.at[7 * a + f - 1], recv_sem=recv_sems.at[7 * a + f - 1],
                    device_id=(px, py, pc), device_id_type=MESH) for a in range(n)]
            return out

        def start():
            for cp in copies():
                cp.start()

        def finish():
            for cp in copies():
                cp.wait()

        return start, lambda: None, finish

    send_sems, recv_sems, local_sems = sems
    me, sibling = (x, y, c), (x, y, 1 - c)

    def slot(a, px, py, pc):
        return dst[a].at[4 * px + 2 * py + pc]

    def copy(a, k, blk, to, from_src=False):
        return pltpu.make_async_remote_copy(
            src_ref=src[a] if from_src else slot(a, *blk), dst_ref=slot(a, *blk),
            send_sem=send_sems.at[7 * a + k], recv_sem=recv_sems.at[7 * a + k], device_id=to, device_id_type=MESH)

    def mine():
        return [pltpu.make_async_copy(src[a], slot(a, *me), local_sems.at[a]) for a in range(n)]

    def first():
        out = []
        for j, chip in enumerate(chips):
            out += [copy(a, 1 + j, me, (*chip, c), from_src=True) for a in range(n)]
        return out + [copy(a, 0, me, sibling, from_src=True) for a in range(n)]

    def passed():
        return [copy(a, 4 + j, (*chip, c), sibling) for j, chip in enumerate(chips) for a in range(n)]

    def start():
        for cp in mine() + first():
            cp.start()

    def mid():
        for j, chip in enumerate(chips):
            for a in range(n):
                copy(a, 1 + j, (*chip, c), me).wait_recv()
        for cp in passed():
            cp.start()

    def finish():
        for a in range(n):
            copy(a, 0, sibling, me).wait_recv()
        for j, chip in enumerate(chips):
            for a in range(n):
                copy(a, 4 + j, (*chip, 1 - c), me).wait_recv()
        for cp in first() + passed():
            cp.wait_send()
        for cp in mine():
            cp.wait()

    return start, mid, finish


def _ride_steps(ride, refs, n_in, n_out, step, n_steps):
    if ride is None:
        return refs, lambda: None
    src, dst, sems, own = _ride_split(ride, refs, n_in, n_out)
    start, mid, finish = _ride_phases(ride, src, dst, sems)
    pl.when(step == 0)(start)
    pl.when(step == (3 * n_steps) // 4)(mid)
    return own, lambda: pl.when(step == n_steps - 1)(finish)


def _scatter_and_gather(bufs, blocks, name):
    rides = (("scatter", bufs), ("gather", blocks))
    ns, ng = len(bufs), len(blocks)

    def body(*refs):
        s_src, g_src = refs[:ns], refs[ns:ns + ng]
        s_dst, g_dst = refs[ns + ng:2 * ns + ng], refs[2 * ns + ng:2 * (ns + ng)]
        sems = refs[2 * (ns + ng):]
        s_start, _, s_finish = _ride_phases(rides[0], s_src, s_dst, sems[:2])
        g_start, g_mid, g_finish = _ride_phases(rides[1], g_src, g_dst, sems[2:])
        s_start()
        g_start()
        g_mid()
        g_finish()
        s_finish()

    s_in, s_shape, s_out, s_scratch = _ride_specs(rides[0])
    g_in, g_shape, g_out, g_scratch = _ride_specs(rides[1])
    outs = pl.pallas_call(
        body, name=name, out_shape=s_shape + g_shape, in_specs=s_in + g_in, out_specs=s_out + g_out,
        scratch_shapes=s_scratch + g_scratch,
    )(*bufs, *blocks)
    return outs[:ns], outs[ns:]


def _total(own_ref, parts_ref):
    g = own_ref[...].astype(F32)
    for k in range(parts_ref.shape[0]):
        g = g + parts_ref[k].astype(F32)
    return g


def _total_adamw(buf, me, parts, w, m, v, layer, carry, name):
    _, r, cdim = buf.shape
    n_layers = w.shape[0]
    tr = _rows(r, 256)
    steps = r // tr

    def body(me_ref, own_ref, parts_ref, w_ref, m_ref, v_ref, *rest):
        del me_ref
        g_ref, d_ref, nm_ref, nv_ref = rest[-4:]
        g = _total(own_ref, parts_ref)
        g_ref[...] = g
        d_ref[...], nm_ref[...], nv_ref[...] = _adamw_math(w_ref[...], g, m_ref[...], v_ref[...])

    lay = pl.BlockSpec((tr, cdim), lambda i, me_ref: (layer * steps + i, 0))
    in_specs = [pl.BlockSpec((None, tr, cdim), lambda i, me_ref: (me_ref[0], i, 0)),
                pl.BlockSpec((N_DEV - 1, tr, cdim), lambda i, me_ref: (0, i, 0)), lay, lay, lay]
    operands = [me, buf, parts, *[a.reshape(n_layers * r, cdim) for a in (w, m, v)]]
    aliases = {}
    if carry is not None:
        in_specs += [pl.BlockSpec(memory_space=pl.ANY)] * 4
        operands += list(carry)
        aliases = {6 + k: k for k in range(4)}
    sds = jax.ShapeDtypeStruct((n_layers * r, cdim), F32)
    return pl.pallas_call(
        body, name=name,
        grid_spec=pltpu.PrefetchScalarGridSpec(num_scalar_prefetch=1, grid=(steps,), in_specs=in_specs,
                                               out_specs=(lay,) * 4),
        out_shape=(sds,) * 4, input_output_aliases=aliases,
        compiler_params=_params(("parallel",)),
    )(*operands)


def _add_parts(buf, me, parts, name):
    _, r, cdim = buf.shape
    tr = _rows(r, 512)

    def body(me_ref, own_ref, parts_ref, out_ref):
        del me_ref
        out_ref[...] = _total(own_ref, parts_ref)

    return pl.pallas_call(
        body, name=name,
        grid_spec=pltpu.PrefetchScalarGridSpec(
            num_scalar_prefetch=1, grid=(r // tr,),
            in_specs=[pl.BlockSpec((None, tr, cdim), lambda i, me_ref: (me_ref[0], i, 0)),
                      pl.BlockSpec((N_DEV - 1, tr, cdim), lambda i, me_ref: (0, i, 0))],
            out_specs=pl.BlockSpec((tr, cdim), lambda i, me_ref: (i, 0))),
        out_shape=jax.ShapeDtypeStruct((r, cdim), F32),
        compiler_params=_params(("parallel",)),
    )(me, buf, parts)


def _sum_leading(stack, name):
    n, r, cdim = stack.shape
    tr = _rows(r, 512)

    def body(in_ref, out_ref):
        acc = in_ref[0]
        for k in range(1, n):
            acc = acc + in_ref[k]
        out_ref[...] = acc

    return pl.pallas_call(
        body, name=name, grid=(r // tr,),
        out_shape=jax.ShapeDtypeStruct((r, cdim), F32),
        in_specs=[pl.BlockSpec((n, tr, cdim), lambda i: (0, i, 0))],
        out_specs=pl.BlockSpec((tr, cdim), lambda i: (i, 0)),
        compiler_params=_params(("parallel",)),
    )(stack)


_DIMS = {"nn": (((1,), (0,)), ((), ())), "nt": (((1,), (1,)), ((), ())), "tn": (((0,), (0,)), ((), ()))}


def _stack_spec(shape, mode, layer):
    cut, l, rows = layer
    cols = shape[2]
    by_n = pl.BlockSpec((1, rows, cols), lambda i, j, k: (j, l, 0))
    by_k = pl.BlockSpec((K_SPAN, rows, cols), lambda i, j, k: (k, l, 0))
    if cut == "col":
        return (by_n, N_DEV * cols, cols, rows) if mode == "nn" else (by_k, rows, rows, K_SPAN * cols)
    return (by_k, cols, cols, K_SPAN * rows) if mode == "nn" else (by_n, N_DEV * rows, rows, cols)


def _matmul(a, b, *, mode, name, out_dtype=BF, a_act=None, epi=None, extras=(), seq=None, layer=None, tm=None,
            tn=None, into=None):
    if mode == "tn":
        kdim, m = a.shape
    else:
        m, kdim = a.shape
    if tm is None:
        tm = _pick(m, 1024 if epi != "resid_gate" else min(1024, seq))
    tk = _pick(kdim, 4096 if mode == "tn" else 1024)
    b_spec = None
    if layer is not None:
        b_spec, n, tn, tk = _stack_spec(b.shape, mode, layer)
    else:
        n = b.shape[0] if mode == "nt" else b.shape[1]
        tn = _pick(n, 1024) if tn is None else tn
    nk = kdim // tk
    a_spec = (pl.BlockSpec((tk, tm), lambda i, j, k: (k, i)) if mode == "tn"
              else pl.BlockSpec((tm, tk), lambda i, j, k: (i, k)))
    if b_spec is None:
        b_spec = (pl.BlockSpec((tn, tk), lambda i, j, k: (j, k)) if mode == "nt"
                  else pl.BlockSpec((tk, tn), lambda i, j, k: (k, j)))
    tile = pl.BlockSpec((tm, tn), lambda i, j, k: (i, j))
    in_specs, out_specs = [a_spec, b_spec], [tile]
    out_shape = [jax.ShapeDtypeStruct((m, n), out_dtype)]
    if epi == "resid_gate":
        in_specs += [tile, pl.BlockSpec((None, 1, tn), lambda i, j, k: ((i * tm) // seq, 0, j))]
        out_shape = [jax.ShapeDtypeStruct((m, n), F32), jax.ShapeDtypeStruct((m, n), BF)]
        out_specs = [tile, tile]
    elif epi in ("mul_drelu", "add"):
        in_specs += [tile]
    elif epi == "bias":
        in_specs += [pl.BlockSpec((1, tn), lambda i, j, k: (0, j))]
    n_extra, n_out = len(in_specs) - 2, len(out_specs)
    aliases, n_kept = {}, 0
    if into is not None:
        buffer, block, index_map = into
        out_dtype = buffer.dtype
        if not isinstance(buffer, jax.ShapeDtypeStruct):
            in_specs.append(pl.BlockSpec(memory_space=pl.ANY))
            extras = tuple(extras) + (buffer,)
            aliases, n_kept = {len(in_specs) - 1: 0}, 1
        out_shape = [jax.ShapeDtypeStruct(buffer.shape, buffer.dtype)]
        out_specs = [pl.BlockSpec(block, index_map)]
    dims = _DIMS[mode]

    def body(*refs):
        a_ref, b_ref = refs[:2]
        ex = refs[2:2 + n_extra]
        outs = refs[2 + n_extra + n_kept:2 + n_extra + n_kept + n_out]
        av = a_ref[...]
        if a_act == "relu2":
            t = jnp.maximum(av.astype(F32), 0.0)
            av = t * t
        elif a_act == "silu":
            t = av.astype(F32)
            av = t / (1.0 + jnp.exp(-t))
        av = av.astype(BF)
        if layer is None:
            part = lax.dot_general(av, b_ref[...].astype(BF), dims, preferred_element_type=F32)
        else:
            span = b_ref.shape[0]
            wk = av.shape[1] // span
            part = None
            for u in range(span):
                p_u = lax.dot_general(av[:, u * wk:(u + 1) * wk], b_ref[u], dims, preferred_element_type=F32)
                part = p_u if part is None else part + p_u

        def finish(acc):
            if epi == "resid_gate":
                outs[0][...] = ex[0][...] + ex[1][...] * acc
                outs[1][...] = acc.astype(BF)
            elif epi == "mul_drelu":
                outs[0][...] = (acc * (2.0 * jnp.maximum(ex[0][...].astype(F32), 0.0))).astype(out_dtype)
            elif epi == "add":
                outs[0][...] = (acc + ex[0][...].astype(F32)).astype(out_dtype)
            elif epi == "bias":
                outs[0][...] = (acc + ex[0][...]).astype(out_dtype)
            else:
                outs[0][...] = acc.astype(out_dtype)

        if nk == 1:
            finish(part)
        else:
            acc_ref = refs[-1]
            k = pl.program_id(2)

            @pl.when(k == 0)
            def _():
                acc_ref[...] = part

            @pl.when(k > 0)
            def _():
                acc_ref[...] += part

            @pl.when(k == nk - 1)
            def _():
                finish(acc_ref[...])

    res = pl.pallas_call(
        body, name=name, grid=(m // tm, n // tn, nk),
        out_shape=out_shape, in_specs=in_specs, out_specs=out_specs,
        scratch_shapes=[pltpu.VMEM((tm, tn), F32)] if nk > 1 else [],
        input_output_aliases=aliases,
        compiler_params=_params(("parallel", "parallel", "arbitrary")),
    )(a, b, *extras)
    return res if n_out > 1 else res[0]


def _norm_mod(x, gain, scale, shift, seq, name):
    t, w = x.shape
    tr = ROW_BLOCK

    def body(x_ref, g_ref, sc_ref, sh_ref, out_ref):
        xv = x_ref[...]
        rstd = lax.rsqrt(jnp.mean(xv * xv, axis=-1, keepdims=True) + NORM_EPS)
        y = xv * rstd * g_ref[...]
        out_ref[...] = (y * (1.0 + sc_ref[...]) + sh_ref[...]).astype(BF)

    per_b = pl.BlockSpec((None, 1, w), lambda i: ((i * tr) // seq, 0, 0))
    return pl.pallas_call(
        body, name=name, grid=(t // tr,),
        out_shape=jax.ShapeDtypeStruct((t, w), BF),
        in_specs=[pl.BlockSpec((tr, w), lambda i: (i, 0)), pl.BlockSpec((1, w), lambda i: (0, 0)), per_b, per_b],
        out_specs=pl.BlockSpec((tr, w), lambda i: (i, 0)),
        compiler_params=_params(("parallel",)),
    )(x, gain, scale, shift)


def _norm_mod_bwd(dh, x, gain, scale, dres, seq, name, gate=None):
    t, w = x.shape
    tr = ROW_BLOCK
    steps_per_seq = seq // tr
    nb = t // seq
    gated = gate is not None

    def body(*refs):
        dh_ref, x_ref, g_ref, sc_ref, dres_ref = refs[:5]
        dx_ref, dg_ref, dsc_ref, dsh_ref = refs[-6:-2] if gated else refs[-4:]
        i = pl.program_id(0)
        xv = x_ref[...]
        dhv = dh_ref[...].astype(F32)
        rstd = lax.rsqrt(jnp.mean(xv * xv, axis=-1, keepdims=True) + NORM_EPS)
        xhat = xv * rstd
        one_sc = 1.0 + sc_ref[...]
        g = g_ref[...]
        dxhat = dhv * (g * one_sc)
        proj = jnp.mean(dxhat * xhat, axis=-1, keepdims=True)
        dxv = dres_ref[...] + rstd * (dxhat - xhat * proj)
        dx_ref[...] = dxv
        dhx = dhv * xhat
        first = [(dg_ref, jnp.sum(dhx * one_sc, axis=0, keepdims=True))]
        per_seq = [(dsc_ref, jnp.sum(dhx * g, axis=0, keepdims=True)), (dsh_ref, jnp.sum(dhv, axis=0, keepdims=True))]
        if gated:
            y_ref, gate_ref, dy_ref, dgate_ref = refs[5], refs[6], refs[-2], refs[-1]
            dy_ref[...] = (dxv * gate_ref[...]).astype(BF)
            per_seq.append((dgate_ref, jnp.sum(dxv * y_ref[...].astype(F32), axis=0, keepdims=True)))
        for cond_new, cond_add, group in ((i == 0, i > 0, first),
                                          (i % steps_per_seq == 0, i % steps_per_seq != 0, per_seq)):
            @pl.when(cond_new)
            def _(group=group):
                for ref, part in group:
                    ref[...] = part

            @pl.when(cond_add)
            def _(group=group):
                for ref, part in group:
                    ref[...] += part

    row = pl.BlockSpec((tr, w), lambda i: (i, 0))
    per_b = pl.BlockSpec((None, 1, w), lambda i: ((i * tr) // seq, 0, 0))
    vec = pl.BlockSpec((1, w), lambda i: (0, 0))
    out_shape = [jax.ShapeDtypeStruct((t, w), F32), jax.ShapeDtypeStruct((1, w), F32),
                 jax.ShapeDtypeStruct((nb, 1, w), F32), jax.ShapeDtypeStruct((nb, 1, w), F32)]
    in_specs, out_specs, operands = [row, row, vec, per_b, row], [row, vec, per_b, per_b], [dh, x, gain, scale, dres]
    if gated:
        in_specs += [row, per_b]
        operands += list(gate)
        out_shape += [jax.ShapeDtypeStruct((t, w), BF), jax.ShapeDtypeStruct((nb, 1, w), F32)]
        out_specs += [row, per_b]
    return pl.pallas_call(
        body, name=name, grid=(t // tr,),
        out_shape=out_shape, in_specs=in_specs, out_specs=out_specs,
        compiler_params=_params(("arbitrary",)),
    )(*operands)


def _gate_bwd(dx, y, gate, seq, name):
    t, w = dx.shape
    tr = ROW_BLOCK
    steps_per_seq = seq // tr
    nb = t // seq

    def body(dx_ref, y_ref, g_ref, dy_ref, dg_ref):
        i = pl.program_id(0)
        dxv = dx_ref[...]
        dy_ref[...] = (dxv * g_ref[...]).astype(BF)
        part = jnp.sum(dxv * y_ref[...].astype(F32), axis=0, keepdims=True)

        @pl.when(i % steps_per_seq == 0)
        def _():
            dg_ref[...] = part

        @pl.when(i % steps_per_seq != 0)
        def _():
            dg_ref[...] += part

    row = pl.BlockSpec((tr, w), lambda i: (i, 0))
    per_b = pl.BlockSpec((None, 1, w), lambda i: ((i * tr) // seq, 0, 0))
    return pl.pallas_call(
        body, name=name, grid=(t // tr,),
        out_shape=(jax.ShapeDtypeStruct((t, w), BF), jax.ShapeDtypeStruct((nb, 1, w), F32)),
        in_specs=[row, row, per_b], out_specs=(row, per_b),
        compiler_params=_params(("arbitrary",)),
    )(dx, y, gate)


def _loss_head(x, gain, target, name):
    t, w = x.shape
    tr = ROW_BLOCK

    def body(x_ref, g_ref, t_ref, loss_ref, dx_ref, dg_ref):
        i = pl.program_id(0)
        xv = x_ref[...]
        g = g_ref[...]
        rstd = lax.rsqrt(jnp.mean(xv * xv, axis=-1, keepdims=True) + NORM_EPS)
        xhat = xv * rstd
        err = xhat * g - t_ref[...]
        row_loss = jnp.sum(err * err, axis=-1, keepdims=True) * (0.5 / w)
        loss_part = jnp.broadcast_to(jnp.sum(row_loss, axis=0, keepdims=True), (1, LANES))
        dy = err * (1.0 / w)
        dg_part = jnp.sum(dy * xhat, axis=0, keepdims=True)
        dxhat = dy * g
        proj = jnp.mean(dxhat * xhat, axis=-1, keepdims=True)
        dx_ref[...] = rstd * (dxhat - xhat * proj)

        @pl.when(i == 0)
        def _():
            loss_ref[...] = loss_part
            dg_ref[...] = dg_part

        @pl.when(i > 0)
        def _():
            loss_ref[...] += loss_part
            dg_ref[...] += dg_part

    row = pl.BlockSpec((tr, w), lambda i: (i, 0))
    vec = pl.BlockSpec((1, w), lambda i: (0, 0))
    return pl.pallas_call(
        body, name=name, grid=(t // tr,),
        out_shape=(jax.ShapeDtypeStruct((1, LANES), F32), jax.ShapeDtypeStruct((t, w), F32),
                   jax.ShapeDtypeStruct((1, w), F32)),
        in_specs=[row, vec, row],
        out_specs=(pl.BlockSpec((1, LANES), lambda i: (0, 0)), row, vec),
        compiler_params=_params(("arbitrary",)),
    )(x, gain, target)


def _rope_group(xg, cos_p, sin_a, sin_b):
    return (xg * cos_p + pltpu.roll(xg, LANES - ROPE_HALF, axis=1) * sin_a
            + pltpu.roll(xg, ROPE_HALF, axis=1) * sin_b)


def _rope(x, tables, name, out_dtype=BF):
    t, w = x.shape
    tr = ROW_BLOCK
    groups = w // LANES

    def body(x_ref, c_ref, a_ref, b_ref, out_ref):
        cos_p, sin_a, sin_b = c_ref[...], a_ref[...], b_ref[...]
        for g in range(groups):
            sl = slice(g * LANES, (g + 1) * LANES)
            out_ref[:, sl] = _rope_group(x_ref[:, sl].astype(F32), cos_p, sin_a, sin_b).astype(out_dtype)

    row = pl.BlockSpec((tr, w), lambda i: (i, 0))
    tab = pl.BlockSpec((tr, LANES), lambda i: (i, 0))
    return pl.pallas_call(
        body, name=name, grid=(t // tr,),
        out_shape=jax.ShapeDtypeStruct((t, w), out_dtype),
        in_specs=[row, tab, tab, tab], out_specs=row,
        compiler_params=_params(("parallel",)),
    )(x, *tables)


def _mla_mid(down, gq, gkv, tables, name):
    t = down.shape[0]
    tr = ROW_BLOCK

    def body(d_ref, gq_ref, gkv_ref, c_ref, a_ref, b_ref, cq_ref, ckr_ref):
        q = d_ref[:, 0:256]
        cq_ref[...] = (q * lax.rsqrt(jnp.mean(q * q, axis=-1, keepdims=True) + NORM_EPS) * gq_ref[...]).astype(BF)
        kv = d_ref[:, 256:384]
        ckr_ref[:, 0:128] = (kv * lax.rsqrt(jnp.mean(kv * kv, axis=-1, keepdims=True) + NORM_EPS)
                             * gkv_ref[...]).astype(BF)
        ckr_ref[:, 128:256] = _rope_group(d_ref[:, 384:512], c_ref[...], a_ref[...], b_ref[...]).astype(BF)

    tab = pl.BlockSpec((tr, LANES), lambda i: (i, 0))
    return pl.pallas_call(
        body, name=name, grid=(t // tr,),
        out_shape=(jax.ShapeDtypeStruct((t, 256), BF), jax.ShapeDtypeStruct((t, 256), BF)),
        in_specs=[pl.BlockSpec((tr, 512), lambda i: (i, 0)), pl.BlockSpec((1, 256), lambda i: (0, 0)),
                  pl.BlockSpec((1, 128), lambda i: (0, 0)), tab, tab, tab],
        out_specs=(pl.BlockSpec((tr, 256), lambda i: (i, 0)), pl.BlockSpec((tr, 256), lambda i: (i, 0))),
        compiler_params=_params(("parallel",)),
    )(down, gq, gkv, *tables)


def _mla_mid_bwd(down, dcq, dckr, gq, gkv, tables_t, name):
    t = down.shape[0]
    tr = ROW_BLOCK

    def norm_bwd(xv, g, dy):
        rstd = lax.rsqrt(jnp.mean(xv * xv, axis=-1, keepdims=True) + NORM_EPS)
        xhat = xv * rstd
        dxhat = dy * g
        proj = jnp.mean(dxhat * xhat, axis=-1, keepdims=True)
        return rstd * (dxhat - xhat * proj), jnp.sum(dy * xhat, axis=0, keepdims=True)

    def body(d_ref, dcq_ref, dckr_ref, gq_ref, gkv_ref, c_ref, a_ref, b_ref, dd_ref, dgq_ref, dgkv_ref):
        i = pl.program_id(0)
        dq, dgq_part = norm_bwd(d_ref[:, 0:256], gq_ref[...], dcq_ref[...].astype(F32))
        dd_ref[:, 0:256] = dq.astype(BF)
        dkv, dgkv_part = norm_bwd(d_ref[:, 256:384], gkv_ref[...], dckr_ref[:, 0:128].astype(F32))
        dd_ref[:, 256:384] = dkv.astype(BF)
        dd_ref[:, 384:512] = _rope_group(dckr_ref[:, 128:256].astype(F32), c_ref[...], a_ref[...],
                                         b_ref[...]).astype(BF)

        @pl.when(i == 0)
        def _():
            dgq_ref[...] = dgq_part
            dgkv_ref[...] = dgkv_part

        @pl.when(i > 0)
        def _():
            dgq_ref[...] += dgq_part
            dgkv_ref[...] += dgkv_part

    tab = pl.BlockSpec((tr, LANES), lambda i: (i, 0))
    r256 = pl.BlockSpec((tr, 256), lambda i: (i, 0))
    return pl.pallas_call(
        body, name=name, grid=(t // tr,),
        out_shape=(jax.ShapeDtypeStruct((t, 512), BF), jax.ShapeDtypeStruct((1, 256), F32),
                   jax.ShapeDtypeStruct((1, 128), F32)),
        in_specs=[pl.BlockSpec((tr, 512), lambda i: (i, 0)), r256, r256, pl.BlockSpec((1, 256), lambda i: (0, 0)),
                  pl.BlockSpec((1, 128), lambda i: (0, 0)), tab, tab, tab],
        out_specs=(pl.BlockSpec((tr, 512), lambda i: (i, 0)), pl.BlockSpec((1, 256), lambda i: (0, 0)),
                   pl.BlockSpec((1, 128), lambda i: (0, 0))),
        compiler_params=_params(("arbitrary",)),
    )(down, dcq, dckr, gq, gkv, *tables_t)


def _scan_rows(x, reverse):
    s = x.shape[0]
    row = lax.broadcasted_iota(jnp.int32, x.shape, 0)
    step = 1
    while step < s:
        if reverse:
            x = x + jnp.where(row < s - step, pltpu.roll(x, s - step, axis=0), 0.0)
        else:
            x = x + jnp.where(row >= step, pltpu.roll(x, step, axis=0), 0.0)
        step *= 2
    return x


def _fox_gate(fg, b_f, seq, name):
    t = fg.shape[0]

    def body(fg_ref, b_ref, out_ref):
        z = fg_ref[...] + b_ref[...]
        log_f = jnp.minimum(z, 0.0) - jnp.log(1.0 + jnp.exp(-jnp.abs(z)))
        out_ref[...] = _scan_rows(log_f, reverse=False)

    blk = pl.BlockSpec((seq, LANES), lambda b: (b, 0))
    return pl.pallas_call(
        body, name=name, grid=(t // seq,),
        out_shape=jax.ShapeDtypeStruct((t, LANES), F32),
        in_specs=[blk, pl.BlockSpec((1, LANES), lambda b: (0, 0))], out_specs=blk,
        compiler_params=_params(("parallel",)),
    )(fg, b_f)


def _fox_gate_bwd(d_cum, fg, b_f, seq, name):
    t = fg.shape[0]

    def body(dc_ref, fg_ref, b_ref, dfg_ref, db_ref):
        b = pl.program_id(0)
        z = fg_ref[...] + b_ref[...]
        d_log_f = _scan_rows(dc_ref[...], reverse=True)
        dz = d_log_f / (1.0 + jnp.exp(z))
        dfg_ref[...] = dz
        part = jnp.sum(dz, axis=0, keepdims=True)

        @pl.when(b == 0)
        def _():
            db_ref[...] = part

        @pl.when(b > 0)
        def _():
            db_ref[...] += part

    blk = pl.BlockSpec((seq, LANES), lambda b: (b, 0))
    vec = pl.BlockSpec((1, LANES), lambda b: (0, 0))
    return pl.pallas_call(
        body, name=name, grid=(t // seq,),
        out_shape=(jax.ShapeDtypeStruct((t, LANES), F32), jax.ShapeDtypeStruct((1, LANES), F32)),
        in_specs=[blk, blk, vec], out_specs=(blk, vec),
        compiler_params=_params(("arbitrary",)),
    )(d_cum, fg, b_f)


def _head_masks():
    lane = lax.broadcasted_iota(jnp.int32, (1, LANES), 1)
    return lane < HEAD_DIM, lane >= HEAD_DIM


def _pair_operands(ref, r0, n, compact, masks, masked):
    if not compact:
        return [ref[pl.ds(r0, n), h * LANES:(h + 1) * LANES] for h in range(2)]
    pair = ref[pl.ds(r0, n), :]
    return [jnp.where(mk, pair, jnp.zeros_like(pair)) for mk in masks] if masked else [pair, pair]


def _causal(n_rows, n_cols, shift):
    return (lax.broadcasted_iota(jnp.int32, (n_rows, n_cols), 1)
            <= lax.broadcasted_iota(jnp.int32, (n_rows, n_cols), 0) + shift)


def _attn_fwd(q_arr, q_off, k_arr, k_off, v_arr, v_off, bias, seq, name, ride=None, compact=False):
    t = q_arr.shape[0]
    nb = t // seq
    blk = min(ATTN_BLOCK, seq)
    nq = seq // blk
    qw = LANES if compact else 2 * LANES
    has_bias = bias is not None
    n_in, n_out = (4, 3) if has_bias else (3, 2)

    def body(*refs):
        step = pl.program_id(0) * HEAD_PAIRS + pl.program_id(1)
        refs, ride_end = _ride_steps(ride, refs, n_in, n_out, step, nb * HEAD_PAIRS)
        if has_bias:
            q_ref, k_ref, v_ref, bias_ref, o_ref, lse_ref, o32_ref = refs
        else:
            q_ref, k_ref, v_ref, o_ref, lse_ref = refs
        masks = _head_masks()
        lo = masks[0]

        def update(r0, n, carry, k0, nk, mask):
            qs = _pair_operands(q_ref, r0, n, compact, masks, True)
            ks = _pair_operands(k_ref, k0, nk, compact, masks, False)
            vv = v_ref[pl.ds(k0, nk), :]
            vs = [jnp.where(mk, vv, jnp.zeros_like(vv)) for mk in masks]
            new, alphas, pv = [], [], None
            for h in range(2):
                m, l = carry[1 + 2 * h], carry[2 + 2 * h]
                s = lax.dot_general(qs[h], ks[h], _DIMS["nt"], preferred_element_type=F32)
                if has_bias:
                    s = s + bias_ref[h, 0:1, pl.ds(k0, nk)]
                if mask is not None:
                    s = jnp.where(mask, s, -jnp.inf)
                m_new = jnp.maximum(m, jnp.max(s, axis=-1, keepdims=True))
                p = jnp.exp(s - m_new)
                alpha = jnp.exp(m - m_new)
                l_new = alpha * l + jnp.sum(p, axis=-1, keepdims=True)
                p_hi = p.astype(BF)
                d = jnp.dot(p_hi, vs[h], preferred_element_type=F32)
                if has_bias:
                    p_lo = (p - p_hi.astype(F32)).astype(BF)
                    d = d + jnp.dot(p_lo, vs[h], preferred_element_type=F32)
                pv = d if pv is None else pv + d
                alphas.append(alpha)
                new += [m_new, l_new]
            return (carry[0] * jnp.where(lo, alphas[0], alphas[1]) + pv, *new)

        def q_block(iq, _):
            q0 = pl.multiple_of(iq * blk, blk)
            init = (jnp.zeros((blk, LANES), F32),
                    jnp.full((blk, 1), -jnp.inf, F32), jnp.zeros((blk, 1), F32),
                    jnp.full((blk, 1), -jnp.inf, F32), jnp.zeros((blk, 1), F32))
            full = lambda j, c: update(q0, blk, c, pl.multiple_of(j * blk, blk), blk, None)
            carry = lax.fori_loop(0, iq // 2, lambda jj, c: full(2 * jj + 1, full(2 * jj, c)), init)

            def last(c, odd):
                if odd:
                    c = full(iq - 1, c)
                acc, m0, l0, m1, l1 = update(q0, blk, c, q0, blk, _causal(blk, blk, 0))
                o_val = acc / jnp.where(lo, l0, l1)
                o_ref[pl.ds(q0, blk), :] = o_val.astype(BF)
                if has_bias:
                    o32_ref[pl.ds(q0, blk), :] = o_val
                lse_ref[pl.ds(q0, blk), :] = jnp.where(lo, m0 + jnp.log(l0), m1 + jnp.log(l1))
                return 0

            return lax.cond(iq % 2 == 1, functools.partial(last, odd=True), functools.partial(last, odd=False), carry)

        lax.fori_loop(0, nq, q_block, 0)
        ride_end()

    in_specs = [pl.BlockSpec((seq, qw), lambda b, p: (b, q_off + p)),
                pl.BlockSpec((seq, qw), lambda b, p: (b, k_off + p)),
                pl.BlockSpec((seq, LANES), lambda b, p: (b, v_off + p))]
    args = [q_arr, k_arr, v_arr]
    if has_bias:
        in_specs.append(pl.BlockSpec((None, 2, 8, seq), lambda b, p: (b, p, 0, 0)))
        args.append(bias)
    out_blk = pl.BlockSpec((seq, LANES), lambda b, p: (b, p))
    out_shape = [jax.ShapeDtypeStruct((t, HEAD_PAIRS * LANES), BF), jax.ShapeDtypeStruct((t, HEAD_PAIRS * LANES), F32)]
    if has_bias:
        out_shape.append(jax.ShapeDtypeStruct((t, HEAD_PAIRS * LANES), F32))
    out_specs, scratch = [out_blk] * len(out_shape), []
    if ride is not None:
        r_in, r_shape, r_out, scratch = _ride_specs(ride)
        in_specs, out_shape, out_specs = in_specs + r_in, out_shape + r_shape, out_specs + r_out
        args += list(ride[1])
    return pl.pallas_call(
        body, name=name, grid=(nb, HEAD_PAIRS),
        out_shape=out_shape, in_specs=in_specs, out_specs=out_specs, scratch_shapes=scratch,
        compiler_params=_params(("arbitrary", "arbitrary")),
    )(*args)


def _attn_bwd(q_arr, q_off, k_arr, k_off, v_arr, v_off, bias, o, do, lse, seq, name, ride=None, compact=False):
    t = q_arr.shape[0]
    nb = t // seq
    blk = min(ATTN_BLOCK, seq)
    half = blk // 2
    nq = seq // blk
    qw = LANES if compact else 2 * LANES
    has_bias = bias is not None
    n_in, n_out = (7, 4) if has_bias else (6, 3)

    def body(*refs):
        step = pl.program_id(0) * HEAD_PAIRS + pl.program_id(1)
        refs, ride_end = _ride_steps(ride, refs, n_in, n_out, step, nb * HEAD_PAIRS)
        if has_bias:
            (q_ref, k_ref, v_ref, bias_ref, o_ref, do_ref, lse_ref,
             dq_ref, dk_ref, dv_ref, dbias_ref, dq_acc, dsum) = refs
        else:
            (q_ref, k_ref, v_ref, o_ref, do_ref, lse_ref, dq_ref, dk_ref, dv_ref, dq_acc, dsum) = refs
        masks = _head_masks()
        lo, hi = masks
        dq_acc[...] = jnp.zeros_like(dq_acc)

        def prep(iq, _):
            q0 = pl.multiple_of(iq * blk, blk)
            prod = do_ref[pl.ds(q0, blk), :].astype(F32) * o_ref[pl.ds(q0, blk), :].astype(F32)
            d0 = jnp.sum(jnp.where(lo, prod, 0.0), axis=-1, keepdims=True)
            d1 = jnp.sum(jnp.where(hi, prod, 0.0), axis=-1, keepdims=True)
            dsum[pl.ds(q0, blk), :] = jnp.where(lo, d0, d1)
            return 0

        lax.fori_loop(0, nq, prep, 0)

        def tile(r0, n, k0, nk, mask):
            qs = _pair_operands(q_ref, r0, n, compact, masks, True)
            ks = _pair_operands(k_ref, k0, nk, compact, masks, True)
            vv = v_ref[pl.ds(k0, nk), :]
            vs = [jnp.where(mk, vv, jnp.zeros_like(vv)) for mk in masks]
            dov = do_ref[pl.ds(r0, n), :]
            dos = [jnp.where(mk, dov, jnp.zeros_like(dov)) for mk in masks] if compact else None
            lse_v = lse_ref[pl.ds(r0, n), :]
            dsum_v = dsum[pl.ds(r0, n), :]
            dv_c, dks, dbs = None, [], []
            for h in range(2):
                s = lax.dot_general(qs[h], ks[h], _DIMS["nt"], preferred_element_type=F32)
                if has_bias:
                    s = s + bias_ref[h, 0:1, pl.ds(k0, nk)]
                p = jnp.exp(s - lse_v[:, h * HEAD_DIM:h * HEAD_DIM + 1])
                if mask is not None:
                    p = jnp.where(mask, p, 0.0)
                dp = lax.dot_general(dov, vs[h], _DIMS["nt"], preferred_element_type=F32)
                ds = p * (dp - dsum_v[:, h * HEAD_DIM:h * HEAD_DIM + 1])
                ds_bf = ds.astype(BF)
                if compact:
                    dv_h = lax.dot_general(p.astype(BF), dos[h], _DIMS["tn"], preferred_element_type=F32)
                else:
                    dv_h = jnp.where(masks[h], lax.dot_general(p.astype(BF), dov, _DIMS["tn"],
                                                               preferred_element_type=F32), 0.0)
                dv_c = dv_h if dv_c is None else dv_c + dv_h
                dk_h = lax.dot_general(ds_bf, qs[h], _DIMS["tn"], preferred_element_type=F32)
                dq_h = jnp.dot(ds_bf, ks[h], preferred_element_type=F32)
                if compact:
                    dks = [dk_h] if h == 0 else [dks[0] + dk_h, jnp.zeros((8, LANES), F32)]
                    if h == 0:
                        dq_first = dq_h
                    else:
                        dq_acc[pl.ds(r0, n), :] += dq_first + dq_h
                else:
                    dks.append(dk_h)
                    dq_acc[pl.ds(r0, n), h * LANES:(h + 1) * LANES] += dq_h
                dbs.append(jnp.sum(ds, axis=0, keepdims=True) if has_bias else jnp.zeros((1, nk), F32))
            return (dv_c, dks[0], dks[1], dbs[0], dbs[1])

        def kv_block(j, _):
            k0 = pl.multiple_of(j * blk, blk)
            if compact:
                dv_a, dk_a, dummy, db0_a, db1_a = tile(pl.multiple_of(k0 + half, half), half, k0, blk,
                                                       _causal(half, blk, half))
                top = tile(k0, half, k0, half, _causal(half, half, 0))
                head = lambda acc, x: jnp.concatenate([acc[:half] + x, acc[half:]], axis=0)
                lead = lambda acc, x: jnp.concatenate([acc[:, :half] + x, acc[:, half:]], axis=1)
                carry = (head(dv_a, top[0]), head(dk_a, top[1]), dummy, lead(db0_a, top[3]), lead(db1_a, top[4]))
            else:
                carry = tile(k0, blk, k0, blk, _causal(blk, blk, 0))

            def q_block(iq, c):
                part = tile(pl.multiple_of(iq * blk, blk), blk, k0, blk, None)
                return tuple(a + b for a, b in zip(c, part))

            n_full = nq - 1 - j
            carry = lax.fori_loop(0, n_full // 2, lambda jj, c: q_block(j + 2 + 2 * jj, q_block(j + 1 + 2 * jj, c)),
                                  carry)

            def last(c, odd):
                if odd:
                    c = q_block(nq - 1, c)
                dv_ref[pl.ds(k0, blk), :] = c[0].astype(BF)
                if compact:
                    dk_ref[pl.ds(k0, blk), :] = c[1].astype(BF)
                else:
                    for h in range(2):
                        dk_ref[pl.ds(k0, blk), h * LANES:(h + 1) * LANES] = c[1 + h].astype(BF)
                if has_bias:
                    for h in range(2):
                        dbias_ref[h, :, pl.ds(k0, blk)] = jnp.broadcast_to(c[3 + h], (8, blk))
                return 0

            return lax.cond(n_full % 2 == 1, functools.partial(last, odd=True), functools.partial(last, odd=False),
                            carry)

        lax.fori_loop(0, nq, kv_block, 0)
        dq_ref[...] = dq_acc[...].astype(BF)
        ride_end()

    pair256 = lambda off: pl.BlockSpec((seq, qw), lambda b, p: (b, off + p))
    pair128 = lambda off: pl.BlockSpec((seq, LANES), lambda b, p: (b, off + p))
    bias_spec = pl.BlockSpec((None, 2, 8, seq), lambda b, p: (b, p, 0, 0))
    in_specs = [pair256(q_off), pair256(k_off), pair128(v_off)]
    args = [q_arr, k_arr, v_arr]
    if has_bias:
        in_specs.append(bias_spec)
        args.append(bias)
    in_specs += [pair128(0), pair128(0), pair128(0)]
    args += [o, do, lse]
    out_shape = [jax.ShapeDtypeStruct((t, HEAD_PAIRS * qw), BF),
                 jax.ShapeDtypeStruct((t, HEAD_PAIRS * qw), BF),
                 jax.ShapeDtypeStruct((t, HEAD_PAIRS * LANES), BF)]
    out_specs = [pair256(0), pair256(0), pair128(0)]
    if has_bias:
        out_shape.append(jax.ShapeDtypeStruct((nb, HEADS, 8, seq), F32))
        out_specs.append(bias_spec)
    scratch = [pltpu.VMEM((seq, qw), F32), pltpu.VMEM((seq, LANES), F32)]
    if ride is not None:
        r_in, r_shape, r_out, r_scratch = _ride_specs(ride)
        in_specs, out_shape, out_specs = in_specs + r_in, out_shape + r_shape, out_specs + r_out
        args += list(ride[1])
        scratch += r_scratch
    return pl.pallas_call(
        body, name=name, grid=(nb, HEAD_PAIRS),
        out_shape=out_shape, in_specs=in_specs, out_specs=out_specs, scratch_shapes=scratch,
        compiler_params=_params(("arbitrary", "arbitrary")),
    )(*args)


def _adamw(w, g, m, v, name):
    shape = w.shape
    last = shape[-1]
    rows = int(np.prod(shape[:-1])) if len(shape) > 1 else 1
    tr = _rows(rows, 512)

    def body(w_ref, g_ref, m_ref, v_ref, d_ref, nm_ref, nv_ref):
        d_ref[...], nm_ref[...], nv_ref[...] = _adamw_math(w_ref[...], g_ref[...], m_ref[...], v_ref[...])

    blk = pl.BlockSpec((tr, last), lambda i: (i, 0))
    sds = jax.ShapeDtypeStruct((rows, last), F32)
    outs = pl.pallas_call(
        body, name=name, grid=(rows // tr,),
        out_shape=(sds, sds, sds), in_specs=[blk] * 4, out_specs=(blk,) * 3,
        compiler_params=_params(("parallel",)),
    )(*[a.reshape(rows, last) for a in (w, g, m, v)])
    return tuple(a.reshape(shape) for a in outs)


LOW_COLS = 256


def _low_pad(a):
    return jnp.pad(a, ((0, 0),) * (a.ndim - 1) + ((0, LOW_COLS - a.shape[-1]),))


def _layer_shards(w, i):
    j = i // 2
    bf = lambda a: a.astype(BF)
    if i % 2 == 0:
        mixer = [bf(w["fox_w_in"][j]), bf(w["fox_w_out"][j])]
    else:
        mixer = [jnp.concatenate([bf(w["mla_w_dq"][j]), bf(w["mla_w_ukv"][j]), _low_pad(bf(w["mla_w_uq"][j])),
                                  _low_pad(bf(w["mla_w_dkv"][j]))], axis=0), bf(w["mla_w_out"][j])]
    return mixer + [bf(w["mlp_w1"][i]), bf(w["mlp_w2"][i])]


def _side_by_side(stack, r0, rows, cols=None):
    return jnp.concatenate([stack[dd, r0:r0 + rows, :cols] for dd in range(N_DEV)], axis=1)


def _stacked(stack, r0, rows, cols=None):
    part = stack[:, r0:r0 + rows, :cols]
    return part.reshape(N_DEV * rows, part.shape[2])


def _layer_mixer_weights(i, first):
    if i % 2 == 0:
        return dict(fox_w_in=_side_by_side(first, 0, 1024))
    return dict(mla_w_dq=_stacked(first, 0, 128), mla_w_ukv=_side_by_side(first, 128, 128),
                mla_w_uq=_side_by_side(first, 256, 256, 192), mla_w_dkv=_stacked(first, 512, 128, 160))


def _by_dest_rows(g):
    return g.reshape(N_DEV, g.shape[0] // N_DEV, g.shape[1]).astype(BF)


def _by_dest_cols(g):
    n = g.shape[1] // N_DEV
    return jnp.stack([g[:, dd * n:(dd + 1) * n] for dd in range(N_DEV)]).astype(BF)


def _layer_mixer_grad_bufs(i, g):
    if i % 2 == 0:
        return [_by_dest_cols(g["fox_w_in"]), _by_dest_rows(g["w_out"])]
    low = jnp.concatenate([_by_dest_rows(g["mla_w_dq"]), _by_dest_cols(g["mla_w_ukv"]),
                           _low_pad(_by_dest_cols(g["mla_w_uq"])), _low_pad(_by_dest_rows(g["mla_w_dkv"]))], axis=1)
    return [low, _by_dest_rows(g["w_out"])]


def _low_shard_grads(low):
    return dict(mla_w_dq=low[:128], mla_w_ukv=low[128:256], mla_w_uq=low[256:512, :192], mla_w_dkv=low[512:, :160])


def _pad_heads(w, width):
    k = w.shape[0]
    return jnp.pad(w.reshape(k, HEADS, width), ((0, 0), (0, 0), (0, LANES - width))).reshape(k, HEADS * LANES)


def _unpad_heads(w, width):
    k = w.shape[0]
    return w.reshape(k, HEADS, LANES)[:, :, :width].reshape(k, HEADS * width)


def _rope_tables(positions, scale):
    inv_freq = 10000.0 ** (-jnp.arange(0, 2 * ROPE_HALF, 2, dtype=F32) / (2 * ROPE_HALF))
    ang = positions.astype(F32)[:, None] * inv_freq
    cos, sin = jnp.cos(ang) * scale, jnp.sin(ang) * scale
    t = positions.shape[0]
    z = lambda n: jnp.zeros((t, n), F32)
    cos_p = jnp.concatenate([jnp.full((t, HEAD_DIM), scale, F32), cos, cos, z(32)], axis=1)
    sin_a = jnp.concatenate([z(64), -sin, z(48)], axis=1)
    sin_b = jnp.concatenate([z(80), sin, z(32)], axis=1)
    fwd = (cos_p, sin_a, sin_b)
    bwd = (cos_p, jnp.roll(sin_b, -ROPE_HALF, axis=1), jnp.roll(sin_a, ROPE_HALF, axis=1))
    return fwd, bwd


def _key_rows(cum, nb, seq):
    v = -cum.reshape(nb, seq, LANES)[:, :, :HEADS]
    return jnp.broadcast_to(jnp.transpose(v, (0, 2, 1))[:, :, None, :], (nb, HEADS, 8, seq))


def kernel(x, c, positions, ada_w, ada_b, norm_mix_g, norm_mlp_g, fox_w_in, fox_b_f, fox_w_out, mla_w_dq, mla_q_norm_g, mla_w_uq, mla_w_dkv, mla_kv_norm_g, mla_w_ukv, mla_w_out, mlp_w1, mlp_w2, final_norm_g, loss_target, m_ada_w, m_ada_b, m_norm_mix_g, m_norm_mlp_g, m_fox_w_in, m_fox_b_f, m_fox_w_out, m_mla_w_dq, m_mla_q_norm_g, m_mla_w_uq, m_mla_w_dkv, m_mla_kv_norm_g, m_mla_w_ukv, m_mla_w_out, m_mlp_w1, m_mlp_w2, m_final_norm_g, v_ada_w, v_ada_b, v_norm_mix_g, v_norm_mlp_g, v_fox_w_in, v_fox_b_f, v_fox_w_out, v_mla_w_dq, v_mla_q_norm_g, v_mla_w_uq, v_mla_w_dkv, v_mla_kv_norm_g, v_mla_w_ukv, v_mla_w_out, v_mlp_w1, v_mlp_w2, v_final_norm_g):
    args = dict(locals())
    weights = {n: args[n] for n in WEIGHTS}
    nb, seq, d = x.shape
    t = nb * seq
    depth = ada_w.shape[0]
    dev = 4 * lax.axis_index("x") + 2 * lax.axis_index("y") + lax.axis_index("c")
    n_mod_local = ada_w.shape[2]

    n_qg = mla_q_norm_g.shape[1]
    cond = jnp.concatenate([c, jnp.pad(mla_q_norm_g.reshape(1, -1), ((0, 7), (0, d - 2 * n_qg)))], axis=0)
    w1_rows, w2_rows = mlp_w1.shape[1], mlp_w2.shape[1]
    shards = [_layer_shards(weights, i) for i in range(depth)]
    stacks = [None] * depth
    cond_all, = _all_gather([cond], "gather_cond")
    c_all = cond_all[:, :nb].reshape(N_DEV * nb, d)
    q_gain = jnp.transpose(cond_all[:, nb, :2 * n_qg].reshape(N_DEV, 2, n_qg), (1, 0, 2)).reshape(2, N_DEV * n_qg)
    mod_local = jnp.stack([
        _matmul(c_all, ada_w[i], mode="nn", name="ada_mod", out_dtype=F32, a_act="silu", epi="bias",
                extras=(lax.dynamic_slice_in_dim(ada_b[i], dev * n_mod_local, n_mod_local)[None, :],))
        for i in range(depth)])
    mod_all, first_in = _all_gather([mod_local.reshape(depth * N_DEV * nb, n_mod_local), shards[0][0]], "gather_first")
    stacks[0] = [first_in]
    mod_all = jnp.transpose(mod_all.reshape(N_DEV, depth, N_DEV * nb, n_mod_local), (1, 2, 0, 3))
    mod_all = mod_all.reshape(depth, N_DEV * nb, N_DEV * n_mod_local)
    mod = lax.dynamic_slice_in_dim(mod_all, dev * nb, nb, axis=1)
    mod = mod.reshape(depth, nb, 6, 1, d)

    pos = positions.reshape(t)
    rope_q, rope_q_t = _rope_tables(pos, MLA_SCALE)
    rope_k, rope_k_t = _rope_tables(pos, 1.0)

    def fox_weights(full):
        w_in = full["fox_w_in"]
        w_qkv = jnp.concatenate([w_in[:, :d] * FOX_SCALE, w_in[:, d:3 * d]], axis=1)
        w_f = jnp.pad(w_in[:, 3 * d:], ((0, 0), (0, LANES - HEADS)))
        return w_qkv, w_f

    def mla_weights(full):
        w_dkv = full["mla_w_dkv"]
        w_down = jnp.concatenate([full["mla_w_dq"], w_dkv[:, :128], jnp.zeros((d, 64), BF),
                                  w_dkv[:, 128:160], jnp.zeros((d, 32), BF)], axis=1)
        w_uq = _pad_heads(full["mla_w_uq"], 96)
        w_ukv = full["mla_w_ukv"].reshape(128, HEADS, 2, HEAD_DIM)
        w_uk = jnp.pad(w_ukv[:, :, 0, :], ((0, 0), (0, 0), (0, 64))).reshape(128, HEADS * LANES)
        w_uv = w_ukv[:, :, 1, :].reshape(128, HEADS * HEAD_DIM)
        place = np.zeros((128, HEADS, LANES), np.float32)
        for i in range(2 * ROPE_HALF):
            place[64 + i, :, 64 + i] = 1.0
        bottom = jnp.concatenate([jnp.asarray(place.reshape(128, HEADS * LANES), BF),
                                  jnp.zeros((128, HEADS * HEAD_DIM), BF)], axis=1)
        w_kv = jnp.concatenate([jnp.concatenate([w_uk, w_uv], axis=1), bottom], axis=0)
        return w_down, w_uq, w_kv

    tm_big = min(2048, t)
    xs = x.reshape(t, d)
    saved = []
    for i in range(depth):
        j = i // 2
        sh_m, sc_m, g_m, sh_f, sc_f, g_f = (mod[i, :, q] for q in range(6))
        gain_mix = norm_mix_g[i][None, :]
        gain_mlp = norm_mlp_g[i][None, :]
        s = dict(x_in=xs)
        h = _norm_mod(xs, gain_mix, sc_m, sh_m, seq, "norm_mix")
        s["h"] = h
        full = _layer_mixer_weights(i, stacks[i][0])
        riders = shards[i][1 if i == 0 else 2:] + (shards[i + 1][:2] if i + 1 < depth else [])
        ride = ("gather", riders)
        if i % 2 == 0:
            w_qkv, w_f = fox_weights(full)
            qkv = _matmul(h, w_qkv, mode="nn", name="fox_qkv")
            fg = _matmul(h, w_f, mode="nn", name="fox_gate_logits", out_dtype=F32)
            b_f = jnp.pad(fox_b_f[j], (0, LANES - HEADS))[None, :]
            cum = _fox_gate(fg, b_f, seq, "fox_gate")
            bias = _key_rows(cum, nb, seq)
            o, lse, o32, *rode = _attn_fwd(qkv, 0, qkv, 8, qkv, 16, bias, seq, "fox_attn", ride, compact=True)
            s.update(qkv=qkv, fg=fg, b_f=b_f, bias=bias, w_qkv=w_qkv, w_f=w_f, o32=o32)
        else:
            w_down, w_uq, w_kv = mla_weights(full)
            down = _matmul(h, w_down, mode="nn", name="mla_down", out_dtype=F32)
            gq, gkv = q_gain[j][None, :], mla_kv_norm_g[j][None, :]
            cq, ckr = _mla_mid(down, gq, gkv, rope_k, "mla_mid")
            q_raw = _matmul(cq, w_uq, mode="nn", name="mla_uq", out_dtype=F32)
            q_rot = _rope(q_raw, rope_q, "mla_rope_q")
            kv = _matmul(ckr, w_kv, mode="nn", name="mla_ukv")
            o, lse, *rode = _attn_fwd(q_rot, 0, kv, 0, kv, 16, None, seq, "mla_attn", ride)
            s.update(down=down, gq=gq, gkv=gkv, cq=cq, ckr=ckr, q_rot=q_rot, kv=kv,
                     w_down=w_down, w_uq=w_uq, w_kv=w_kv)
        n_own = 3 if i == 0 else 2
        stacks[i], rode = stacks[i] + rode[:n_own], rode[n_own:]
        if i + 1 < depth:
            stacks[i + 1] = rode
        w_out = _stacked(stacks[i][1], 0, 128)
        xs, y = _matmul(o, w_out, mode="nn", name="attn_out", epi="resid_gate", extras=(xs, g_m), seq=seq)
        s.update(o=o, lse=lse, y=y, w_out=w_out, x_mid=xs)
        h2 = _norm_mod(xs, gain_mlp, sc_f, sh_f, seq, "norm_mlp")
        a_pre = _matmul(h2, stacks[i][2], mode="nn", name="mlp_up", layer=("col", 0, w1_rows), tm=tm_big)
        xs, y2 = _matmul(a_pre, stacks[i][3], mode="nn", name="mlp_down", layer=("row", 0, w2_rows), a_act="relu2",
                         epi="resid_gate", extras=(xs, g_f), seq=seq)
        s.update(h2=h2, a_pre=a_pre, y2=y2)
        saved.append(s)

    loss_part, dx, dg_final = _loss_head(xs, final_norm_g[None, :], loss_target.reshape(t, d), "loss_head")

    w1_cols, w1_tm = mlp_w1.shape[2], _pick(w1_rows, 1024)
    dg_mix, dg_mlp, db_f, dg_kv, dg_q = [None] * depth, [None] * depth, [None] * 2, [None] * 2, [None] * 2
    dmod = [None] * depth
    me = dev.astype(jnp.int32).reshape(1)
    chains = dict(fox_w_in=None, fox_w_out=None, mla_w_out=None, mlp_w1=None, mlp_w2=None)
    low_grads = [None] * 2

    def adam_step(name, layer):
        def land_one(buf, got):
            chains[name] = _total_adamw(buf, me, got, weights[name], args["m_" + name], args["v_" + name], layer,
                                        chains[name], "adamw_" + name)
        return land_one

    def low_step(lj):
        def land_one(buf, got):
            low_grads[lj] = _low_shard_grads(_add_parts(buf, me, got, "grads_total"))
        return land_one

    def mixer_steps(li):
        lj = li // 2
        return ([adam_step("fox_w_in", lj), adam_step("fox_w_out", lj)] if li % 2 == 0
                else [low_step(lj), adam_step("mla_w_out", lj)])

    def mlp_steps(li):
        return [adam_step("mlp_w1", li), adam_step("mlp_w2", li)]

    def land(staged, got):
        for (step, buf), g in zip(staged, got):
            step(buf, g)

    waiting = []
    dy2, dg_f = _gate_bwd(dx, saved[depth - 1]["y2"], mod[depth - 1, :, 5], seq, "gate_bwd")
    for i in reversed(range(depth)):
        j = i // 2
        s = saved[i]
        sh_m, sc_m, g_m, sh_f, sc_f, g_f = (mod[i, :, q] for q in range(6))
        da_pre = _matmul(dy2, stacks[i][3], mode="nt", name="mlp_down_dx", layer=("row", 0, w2_rows), epi="mul_drelu",
                         extras=(s["a_pre"],), tm=tm_big)
        g_w2 = _matmul(s["a_pre"], dy2, mode="tn", name="mlp_down_dw", a_act="relu2", tm=w2_rows,
                       into=(jax.ShapeDtypeStruct((N_DEV, w2_rows, d), BF), (None, w2_rows, d),
                             lambda r, j, k: (r, 0, 0)))
        dh2 = _matmul(da_pre, stacks[i][2], mode="nt", name="mlp_up_dx", layer=("col", 0, w1_rows), tm=tm_big)
        g_w1 = _matmul(s["h2"], da_pre, mode="tn", name="mlp_up_dw", tm=w1_tm, tn=w1_cols,
                       into=(jax.ShapeDtypeStruct((N_DEV, w1_rows, w1_cols), BF), (None, w1_tm, w1_cols),
                             lambda r, j, k: (j, r, 0)))
        waiting += list(zip(mlp_steps(i), [g_w1, g_w2]))
        dx, dg_mlp[i], dsc_f, dsh_f, dy, dg_m = _norm_mod_bwd(dh2, s["x_mid"], norm_mlp_g[i][None, :], sc_f, dx, seq,
                                                              "norm_bwd_gate", gate=(s["y"], g_m))
        do = _matmul(dy, s["w_out"], mode="nt", name="attn_out_dx")
        dw_out = _matmul(s["o"], dy, mode="tn", name="attn_out_dw", out_dtype=F32)
        if i == 0:
            waiting.append((adam_step("fox_w_out", 0), _by_dest_rows(dw_out)))
        ride = ("scatter", [e[1] for e in waiting]) if waiting else None
        g_mixer = dict(w_out=dw_out)
        if i % 2 == 0:
            qkv = s["qkv"]
            dq, dk, dv, dbias, *rode = _attn_bwd(qkv, 0, qkv, 8, qkv, 16, s["bias"], s["o32"], do, s["lse"], seq,
                                                 "fox_attn_bwd", ride, compact=True)
            dqkv = jnp.concatenate([dq, dk, dv], axis=1)
            d_cum = -jnp.transpose(dbias[:, :, 0, :], (0, 2, 1)).reshape(t, HEADS)
            d_cum = jnp.pad(d_cum, ((0, 0), (0, LANES - HEADS)))
            dfg, db = _fox_gate_bwd(d_cum, s["fg"], s["b_f"], seq, "fox_gate_bwd")
            db_f[j] = db
            dh = _matmul(dfg, s["w_f"], mode="nt", name="fox_gate_dx", out_dtype=F32)
            dh = _matmul(dqkv, s["w_qkv"], mode="nt", name="fox_qkv_dx", epi="add", extras=(dh,))
            dw_qkv = _matmul(s["h"], dqkv, mode="tn", name="fox_qkv_dw", out_dtype=F32)
            dw_f = _matmul(s["h"], dfg, mode="tn", name="fox_gate_dw", out_dtype=F32)
            g_mixer["fox_w_in"] = jnp.concatenate([dw_qkv[:, :d] * FOX_SCALE, dw_qkv[:, d:], dw_f[:, :HEADS]], axis=1)
        else:
            kv = s["kv"]
            dq, dk, dv, *rode = _attn_bwd(s["q_rot"], 0, kv, 0, kv, 16, None, s["o"], do, s["lse"], seq,
                                          "mla_attn_bwd", ride)
            dq_raw = _rope(dq, rope_q_t, "mla_rope_q_bwd")
            dcq = _matmul(dq_raw, s["w_uq"], mode="nt", name="mla_uq_dx")
            dw_uq = _matmul(s["cq"], dq_raw, mode="tn", name="mla_uq_dw", out_dtype=F32)
            dkv = jnp.concatenate([dk, dv], axis=1)
            dckr = _matmul(dkv, s["w_kv"], mode="nt", name="mla_ukv_dx")
            dw_kv = _matmul(s["ckr"], dkv, mode="tn", name="mla_ukv_dw", out_dtype=F32)
            d_down, dgq, dgkv = _mla_mid_bwd(s["down"], dcq, dckr, s["gq"], s["gkv"], rope_k_t, "mla_mid_bwd")
            dg_q[j], dg_kv[j] = dgq, dgkv
            dh = _matmul(d_down, s["w_down"], mode="nt", name="mla_down_dx")
            dw_down = _matmul(s["h"], d_down, mode="tn", name="mla_down_dw", out_dtype=F32)
            g_mixer["mla_w_dq"] = dw_down[:, :256]
            g_mixer["mla_w_dkv"] = jnp.concatenate([dw_down[:, 256:384], dw_down[:, 448:480]], axis=1)
            g_mixer["mla_w_uq"] = _unpad_heads(dw_uq, 96)
            dk_nope = dw_kv[:128, :HEADS * LANES].reshape(128, HEADS, LANES)[:, :, :HEAD_DIM]
            dv_w = dw_kv[:128, HEADS * LANES:].reshape(128, HEADS, HEAD_DIM)
            g_mixer["mla_w_ukv"] = jnp.concatenate([dk_nope, dv_w], axis=2).reshape(128, HEADS * LANES)
        land(waiting, rode)
        this_dg_f = dg_f
        if i > 0:
            dx, dg_mix[i], dsc_m, dsh_m, dy2, dg_f = _norm_mod_bwd(
                dh, s["x_in"], norm_mix_g[i][None, :], sc_m, dx, seq, "norm_bwd_gate",
                gate=(saved[i - 1]["y2"], mod[i - 1, :, 5]))
            waiting = list(zip(mixer_steps(i), _layer_mixer_grad_bufs(i, g_mixer)))
        else:
            dx, dg_mix[i], dsc_m, dsh_m = _norm_mod_bwd(dh, s["x_in"], norm_mix_g[i][None, :], sc_m, dx, seq, "norm_bwd")
            last = [(adam_step("fox_w_in", 0), _by_dest_cols(g_mixer["fox_w_in"]))]
        dmod[i] = jnp.stack([dsh_m, dsc_m, dg_m, dsh_f, dsc_f, this_dg_f], axis=1).reshape(nb, 6 * d)

    grad_x = dx.reshape(nb, seq, d)
    shard_grads = {n: jnp.stack([low_grads[0][n], low_grads[1][n]]) for n in low_grads[0]}

    dmod_arr = jnp.stack(dmod)
    wide = lambda a: jnp.pad(a, ((0, 0), (0, d - a.shape[1])))
    pieces = [wide(loss_part), *dg_mix, *dg_mlp, *[wide(a) for a in db_f], *[wide(a) for a in dg_kv], dg_final,
              *[wide(a) for a in dg_q], jnp.sum(dmod_arr, axis=1).reshape(depth * 6, d)]
    n_small = sum(p.shape[0] for p in pieces)
    both = jnp.concatenate(pieces + [dmod_arr.reshape(depth * nb * 6, d)], axis=0)
    both = jnp.pad(both, ((0, (-both.shape[0]) % 8), (0, 0)))
    scattered, (both_all,) = _scatter_and_gather([e[1] for e in last], [both], "last_exchange")
    land(last, scattered)
    done = {n: tuple(a.reshape(weights[n].shape) for a in chain) for n, chain in chains.items()}
    total = _sum_leading(both_all, "sum_small")
    off = 0

    def take(rows):
        nonlocal off
        out = total[off:off + rows]
        off += rows
        return out

    loss = take(1)[0, 0]
    g_small = dict(
        norm_mix_g=take(depth), norm_mlp_g=take(depth), fox_b_f=take(2)[:, :HEADS], mla_kv_norm_g=take(2)[:, :128],
        final_norm_g=take(1)[0],
        mla_q_norm_g=lax.dynamic_slice_in_dim(take(2)[:, :N_DEV * n_qg], dev * n_qg, n_qg, axis=1),
        ada_b=take(depth * 6).reshape(depth, 6 * d))
    dmod_all = both_all[:, n_small:n_small + depth * nb * 6]
    dmod_all = jnp.transpose(dmod_all.reshape(N_DEV, depth, nb, 6 * d), (1, 0, 2, 3)).reshape(depth, N_DEV * nb, 6 * d)
    dmod_cols = lax.dynamic_slice_in_dim(dmod_all, dev * n_mod_local, n_mod_local, axis=2)
    g_ada_w = jnp.stack([_matmul(c_all, dmod_cols[i], mode="tn", name="ada_dw", out_dtype=F32, a_act="silu")
                         for i in range(depth)])

    all_grads = dict(shard_grads)
    all_grads.update(g_small)
    all_grads["ada_w"] = g_ada_w

    deltas, new_m, new_v = {}, {}, {}
    for n in WEIGHTS:
        if n in done:
            all_grads[n], deltas[n], new_m[n], new_v[n] = done[n]
        else:
            deltas[n], new_m[n], new_v[n] = _adamw(weights[n], all_grads[n], args["m_" + n], args["v_" + n], "adamw")

    return (loss, grad_x, *[all_grads[n] for n in WEIGHTS], *[deltas[n] for n in WEIGHTS],
            *[new_m[n] for n in WEIGHTS], *[new_v[n] for n in WEIGHTS])
```

```python
import functools
import math

import jax
import jax.numpy as jnp
import numpy as np
from jax import lax
from jax.experimental import pallas as pl
from jax.experimental.pallas import tpu as pltpu

F32 = jnp.float32
BF = jnp.bfloat16

N_DEV = 8
HEADS = 16
HEAD_PAIRS = HEADS // 2
HEAD_DIM = 64
LANES = 128
ROPE_HALF = 16
NORM_EPS = 1e-6
MLA_SCALE = 96.0 ** -0.5
FOX_SCALE = 0.125
ATTN_BLOCK = 512
ROW_BLOCK = 1024
K_SPAN = 4
VMEM_LIMIT = 56 * 1024 * 1024
MESH = pl.DeviceIdType.MESH

ADAM_LR = 0.001
ADAM_B1 = 0.9
ADAM_B2 = 0.999
ADAM_EPS = 1e-08
ADAM_WD = 0.01
ADAM_STEP = 10

WEIGHTS = ("ada_w", "ada_b", "norm_mix_g", "norm_mlp_g", "fox_w_in", "fox_b_f", "fox_w_out", "mla_w_dq",
           "mla_q_norm_g", "mla_w_uq", "mla_w_dkv", "mla_kv_norm_g", "mla_w_ukv", "mla_w_out", "mlp_w1",
           "mlp_w2", "final_norm_g")


def _params(sem=None):
    return pltpu.CompilerParams(dimension_semantics=sem, vmem_limit_bytes=VMEM_LIMIT)


def _pick(n, target):
    if n <= target:
        return n
    for t in range(target, 127, -128):
        if n % t == 0:
            return t
    return n


def _rows(n, target=512):
    if n <= target:
        return n
    for t in range(target, 7, -8):
        if n % t == 0:
            return t
    return n


def _place():
    x, y, c = lax.axis_index("x"), lax.axis_index("y"), lax.axis_index("c")
    return x, y, c


def _adamw_math(w, g, m, v):
    nm = ADAM_B1 * m + (1.0 - ADAM_B1) * g
    nv = ADAM_B2 * v + (1.0 - ADAM_B2) * (g * g)
    m_hat = nm * (1.0 / (1.0 - ADAM_B1 ** ADAM_STEP))
    v_hat = nv * (1.0 / (1.0 - ADAM_B2 ** ADAM_STEP))
    return -ADAM_LR * (m_hat / (jnp.sqrt(v_hat) + ADAM_EPS) + ADAM_WD * w), nm, nv


def _all_gather(blocks, name):
    ride = ("gather", blocks)

    def body(*refs):
        start, mid, finish = _ride_phases(ride, *_ride_split(ride, refs, 0, 0)[:3])
        start()
        mid()
        finish()

    in_specs, out_shape, out_specs, scratch = _ride_specs(ride)
    return pl.pallas_call(
        body, name=name, out_shape=out_shape, in_specs=in_specs, out_specs=out_specs, scratch_shapes=scratch,
    )(*blocks)


def _ride_specs(ride):
    kind, arrays = ride
    n = len(arrays)
    any_spec = pl.BlockSpec(memory_space=pl.ANY)
    if kind == "gather":
        out_shape = [jax.ShapeDtypeStruct((N_DEV,) + b.shape, b.dtype) for b in arrays]
        scratch = [pltpu.SemaphoreType.DMA((7 * n,)), pltpu.SemaphoreType.DMA((7 * n,)), pltpu.SemaphoreType.DMA((n,))]
    else:
        out_shape = [jax.ShapeDtypeStruct((N_DEV - 1,) + p.shape[1:], p.dtype) for p in arrays]
        scratch = [pltpu.SemaphoreType.DMA((7 * n,)), pltpu.SemaphoreType.DMA((7 * n,))]
    return [any_spec] * n, out_shape, [any_spec] * n, scratch


def _ride_split(ride, refs, n_in, n_out):
    n = len(ride[1])
    n_sem = 3 if ride[0] == "gather" else 2
    src = refs[n_in:n_in + n]
    dst = refs[n_in + n + n_out:n_in + 2 * n + n_out]
    own = refs[:n_in] + refs[n_in + n:n_in + n + n_out] + refs[n_in + 2 * n + n_out:len(refs) - n_sem]
    return src, dst, refs[len(refs) - n_sem:], own


def _ride_phases(ride, src, dst, sems):
    n = len(src)
    x, y, c = _place()
    chips = [(1 - x, y), (x, 1 - y), (1 - x, 1 - y)]
    if ride[0] == "scatter":
        send_sems, recv_sems = sems

        def copies():
            out = []
            for f in (1, 2, 3, 5, 6, 7, 4):
                px, py, pc = x ^ (f & 1), y ^ ((f >> 1) & 1), c ^ (f >> 2)
                out += [pltpu.make_async_remote_copy(
                    src_ref=src[a].at[4 * px + 2 * py + pc], dst_ref=dst[a].at[f - 1],
                    send_sem=send_sems.at[7 * a + f - 1], recv_sem=recv_sems.at[7 * a + f - 1],
                    device_id=(px, py, pc), device_id_type=MESH) for a in range(n)]
            return out

        def start():
            for cp in copies():
                cp.start()

        def finish():
            for cp in copies():
                cp.wait()

        return start, lambda: None, finish

    send_sems, recv_sems, local_sems = sems
    me, sibling = (x, y, c), (x, y, 1 - c)

    def slot(a, px, py, pc):
        return dst[a].at[4 * px + 2 * py + pc]

    def copy(a, k, blk, to, from_src=False):
        return pltpu.make_async_remote_copy(
            src_ref=src[a] if from_src else slot(a, *blk), dst_ref=slot(a, *blk),
            send_sem=send_sems.at[7 * a + k], recv_sem=recv_sems.at[7 * a + k], device_id=to, device_id_type=MESH)

    def mine():
        return [pltpu.make_async_copy(src[a], slot(a, *me), local_sems.at[a]) for a in range(n)]

    def first():
        out = []
        for j, chip in enumerate(chips):
            out += [copy(a, 1 + j, me, (*chip, c), from_src=True) for a in range(n)]
        return out + [copy(a, 0, me, sibling, from_src=True) for a in range(n)]

    def passed():
        return [copy(a, 4 + j, (*chip, c), sibling) for j, chip in enumerate(chips) for a in range(n)]

    def start():
        for cp in mine() + first():
            cp.start()

    def mid():
        for j, chip in enumerate(chips):
            for a in range(n):
                copy(a, 1 + j, (*chip, c), me).wait_recv()
        for cp in passed():
            cp.start()

    def finish():
        for a in range(n):
            copy(a, 0, sibling, me).wait_recv()
        for j, chip in enumerate(chips):
            for a in range(n):
                copy(a, 4 + j, (*chip, 1 - c), me).wait_recv()
        for cp in first() + passed():
            cp.wait_send()
        for cp in mine():
            cp.wait()

    return start, mid, finish


def _ride_steps(ride, refs, n_in, n_out, step, n_steps):
    if ride is None:
        return refs, lambda: None
    src, dst, sems, own = _ride_split(ride, refs, n_in, n_out)
    start, mid, finish = _ride_phases(ride, src, dst, sems)
    pl.when(step == 0)(start)
    pl.when(step == (3 * n_steps) // 4)(mid)
    return own, lambda: pl.when(step == n_steps - 1)(finish)


def _scatter_and_gather(bufs, blocks, name):
    rides = (("scatter", bufs), ("gather", blocks))
    ns, ng = len(bufs), len(blocks)

    def body(*refs):
        s_src, g_src = refs[:ns], refs[ns:ns + ng]
        s_dst, g_dst = refs[ns + ng:2 * ns + ng], refs[2 * ns + ng:2 * (ns + ng)]
        sems = refs[2 * (ns + ng):]
        s_start, _, s_finish = _ride_phases(rides[0], s_src, s_dst, sems[:2])
        g_start, g_mid, g_finish = _ride_phases(rides[1], g_src, g_dst, sems[2:])
        s_start()
        g_start()
        g_mid()
        g_finish()
        s_finish()

    s_in, s_shape, s_out, s_scratch = _ride_specs(rides[0])
    g_in, g_shape, g_out, g_scratch = _ride_specs(rides[1])
    outs = pl.pallas_call(
        body, name=name, out_shape=s_shape + g_shape, in_specs=s_in + g_in, out_specs=s_out + g_out,
        scratch_shapes=s_scratch + g_scratch,
    )(*bufs, *blocks)
    return outs[:ns], outs[ns:]


def _total(own_ref, parts_ref):
    g = own_ref[...].astype(F32)
    for k in range(parts_ref.shape[0]):
        g = g + parts_ref[k].astype(F32)
    return g


def _total_adamw(buf, me, parts, w, m, v, layer, carry, name):
    _, r, cdim = buf.shape
    n_layers = w.shape[0]
    tr = _rows(r, 256)
    steps = r // tr

    def body(me_ref, own_ref, parts_ref, w_ref, m_ref, v_ref, *rest):
        del me_ref
        g_ref, d_ref, nm_ref, nv_ref = rest[-4:]
        g = _total(own_ref, parts_ref)
        g_ref[...] = g
        d_ref[...], nm_ref[...], nv_ref[...] = _adamw_math(w_ref[...], g, m_ref[...], v_ref[...])

    lay = pl.BlockSpec((tr, cdim), lambda i, me_ref: (layer * steps + i, 0))
    in_specs = [pl.BlockSpec((None, tr, cdim), lambda i, me_ref: (me_ref[0], i, 0)),
                pl.BlockSpec((N_DEV - 1, tr, cdim), lambda i, me_ref: (0, i, 0)), lay, lay, lay]
    operands = [me, buf, parts, *[a.reshape(n_layers * r, cdim) for a in (w, m, v)]]
    aliases = {}
    if carry is not None:
        in_specs += [pl.BlockSpec(memory_space=pl.ANY)] * 4
        operands += list(carry)
        aliases = {6 + k: k for k in range(4)}
    sds = jax.ShapeDtypeStruct((n_layers * r, cdim), F32)
    return pl.pallas_call(
        body, name=name,
        grid_spec=pltpu.PrefetchScalarGridSpec(num_scalar_prefetch=1, grid=(steps,), in_specs=in_specs,
                                               out_specs=(lay,) * 4),
        out_shape=(sds,) * 4, input_output_aliases=aliases,
        compiler_params=_params(("parallel",)),
    )(*operands)


def _add_parts(buf, me, parts, name):
    _, r, cdim = buf.shape
    tr = _rows(r, 512)

    def body(me_ref, own_ref, parts_ref, out_ref):
        del me_ref
        out_ref[...] = _total(own_ref, parts_ref)

    return pl.pallas_call(
        body, name=name,
        grid_spec=pltpu.PrefetchScalarGridSpec(
            num_scalar_prefetch=1, grid=(r // tr,),
            in_specs=[pl.BlockSpec((None, tr, cdim), lambda i, me_ref: (me_ref[0], i, 0)),
                      pl.BlockSpec((N_DEV - 1, tr, cdim), lambda i, me_ref: (0, i, 0))],
            out_specs=pl.BlockSpec((tr, cdim), lambda i, me_ref: (i, 0))),
        out_shape=jax.ShapeDtypeStruct((r, cdim), F32),
        compiler_params=_params(("parallel",)),
    )(me, buf, parts)


def _sum_leading(stack, name):
    n, r, cdim = stack.shape
    tr = _rows(r, 512)

    def body(in_ref, out_ref):
        acc = in_ref[0]
        for k in range(1, n):
            acc = acc + in_ref[k]
        out_ref[...] = acc

    return pl.pallas_call(
        body, name=name, grid=(r // tr,),
        out_shape=jax.ShapeDtypeStruct((r, cdim), F32),
        in_specs=[pl.BlockSpec((n, tr, cdim), lambda i: (0, i, 0))],
        out_specs=pl.BlockSpec((tr, cdim), lambda i: (i, 0)),
        compiler_params=_params(("parallel",)),
    )(stack)


_DIMS = {"nn": (((1,), (0,)), ((), ())), "nt": (((1,), (1,)), ((), ())), "tn": (((0,), (0,)), ((), ()))}


def _stack_spec(shape, mode, layer):
    cut, l, rows = layer
    cols = shape[2]
    by_n = pl.BlockSpec((1, rows, cols), lambda i, j, k: (j, l, 0))
    by_k = pl.BlockSpec((K_SPAN, rows, cols), lambda i, j, k: (k, l, 0))
    if cut == "col":
        return (by_n, N_DEV * cols, cols, rows) if mode == "nn" else (by_k, rows, rows, K_SPAN * cols)
    return (by_k, cols, cols, K_SPAN * rows) if mode == "nn" else (by_n, N_DEV * rows, rows, cols)


def _matmul(a, b, *, mode, name, out_dtype=BF, a_act=None, epi=None, extras=(), seq=None, layer=None, tm=None,
            tn=None, into=None):
    if mode == "tn":
        kdim, m = a.shape
    else:
        m, kdim = a.shape
    if tm is None:
        tm = _pick(m, 1024 if epi != "resid_gate" else min(1024, seq))
    tk = _pick(kdim, 4096 if mode == "tn" else 1024)
    b_spec = None
    if layer is not None:
        b_spec, n, tn, tk = _stack_spec(b.shape, mode, layer)
    else:
        n = b.shape[0] if mode == "nt" else b.shape[1]
        tn = _pick(n, 1024) if tn is None else tn
    nk = kdim // tk
    a_spec = (pl.BlockSpec((tk, tm), lambda i, j, k: (k, i)) if mode == "tn"
              else pl.BlockSpec((tm, tk), lambda i, j, k: (i, k)))
    if b_spec is None:
        b_spec = (pl.BlockSpec((tn, tk), lambda i, j, k: (j, k)) if mode == "nt"
                  else pl.BlockSpec((tk, tn), lambda i, j, k: (k, j)))
    tile = pl.BlockSpec((tm, tn), lambda i, j, k: (i, j))
    in_specs, out_specs = [a_spec, b_spec], [tile]
    out_shape = [jax.ShapeDtypeStruct((m, n), out_dtype)]
    if epi == "resid_gate":
        in_specs += [tile, pl.BlockSpec((None, 1, tn), lambda i, j, k: ((i * tm) // seq, 0, j))]
        out_shape = [jax.ShapeDtypeStruct((m, n), F32), jax.ShapeDtypeStruct((m, n), BF)]
        out_specs = [tile, tile]
    elif epi in ("mul_drelu", "add"):
        in_specs += [tile]
    elif epi == "bias":
        in_specs += [pl.BlockSpec((1, tn), lambda i, j, k: (0, j))]
    n_extra, n_out = len(in_specs) - 2, len(out_specs)
    aliases, n_kept = {}, 0
    if into is not None:
        buffer, block, index_map = into
        out_dtype = buffer.dtype
        if not isinstance(buffer, jax.ShapeDtypeStruct):
            in_specs.append(pl.BlockSpec(memory_space=pl.ANY))
            extras = tuple(extras) + (buffer,)
            aliases, n_kept = {len(in_specs) - 1: 0}, 1
        out_shape = [jax.ShapeDtypeStruct(buffer.shape, buffer.dtype)]
        out_specs = [pl.BlockSpec(block, index_map)]
    dims = _DIMS[mode]

    def body(*refs):
        a_ref, b_ref = refs[:2]
        ex = refs[2:2 + n_extra]
        outs = refs[2 + n_extra + n_kept:2 + n_extra + n_kept + n_out]
        av = a_ref[...]
        if a_act == "relu2":
            t = jnp.maximum(av.astype(F32), 0.0)
            av = t * t
        elif a_act == "silu":
            t = av.astype(F32)
            av = t / (1.0 + jnp.exp(-t))
        av = av.astype(BF)
        if layer is None:
            part = lax.dot_general(av, b_ref[...].astype(BF), dims, preferred_element_type=F32)
        else:
            span = b_ref.shape[0]
            wk = av.shape[1] // span
            part = None
            for u in range(span):
                p_u = lax.dot_general(av[:, u * wk:(u + 1) * wk], b_ref[u], dims, preferred_element_type=F32)
                part = p_u if part is None else part + p_u

        def finish(acc):
            if epi == "resid_gate":
                outs[0][...] = ex[0][...] + ex[1][...] * acc
                outs[1][...] = acc.astype(BF)
            elif epi == "mul_drelu":
                outs[0][...] = (acc * (2.0 * jnp.maximum(ex[0][...].astype(F32), 0.0))).astype(out_dtype)
            elif epi == "add":
                outs[0][...] = (acc + ex[0][...].astype(F32)).astype(out_dtype)
            elif epi == "bias":
                outs[0][...] = (acc + ex[0][...]).astype(out_dtype)
            else:
                outs[0][...] = acc.astype(out_dtype)

        if nk == 1:
            finish(part)
        else:
            acc_ref = refs[-1]
            k = pl.program_id(2)

            @pl.when(k == 0)
            def _():
                acc_ref[...] = part

            @pl.when(k > 0)
            def _():
                acc_ref[...] += part

            @pl.when(k == nk - 1)
            def _():
                finish(acc_ref[...])

    res = pl.pallas_call(
        body, name=name, grid=(m // tm, n // tn, nk),
        out_shape=out_shape, in_specs=in_specs, out_specs=out_specs,
        scratch_shapes=[pltpu.VMEM((tm, tn), F32)] if nk > 1 else [],
        input_output_aliases=aliases,
        compiler_params=_params(("parallel", "parallel", "arbitrary")),
    )(a, b, *extras)
    return res if n_out > 1 else res[0]


def _norm_mod(x, gain, scale, shift, seq, name):
    t, w = x.shape
    tr = ROW_BLOCK

    def body(x_ref, g_ref, sc_ref, sh_ref, out_ref):
        xv = x_ref[...]
        rstd = lax.rsqrt(jnp.mean(xv * xv, axis=-1, keepdims=True) + NORM_EPS)
        y = xv * rstd * g_ref[...]
        out_ref[...] = (y * (1.0 + sc_ref[...]) + sh_ref[...]).astype(BF)

    per_b = pl.BlockSpec((None, 1, w), lambda i: ((i * tr) // seq, 0, 0))
    return pl.pallas_call(
        body, name=name, grid=(t // tr,),
        out_shape=jax.ShapeDtypeStruct((t, w), BF),
        in_specs=[pl.BlockSpec((tr, w), lambda i: (i, 0)), pl.BlockSpec((1, w), lambda i: (0, 0)), per_b, per_b],
        out_specs=pl.BlockSpec((tr, w), lambda i: (i, 0)),
        compiler_params=_params(("parallel",)),
    )(x, gain, scale, shift)


def _norm_mod_bwd(dh, x, gain, scale, dres, seq, name, gate=None):
    t, w = x.shape
    tr = ROW_BLOCK
    steps_per_seq = seq // tr
    nb = t // seq
    gated = gate is not None

    def body(*refs):
        dh_ref, x_ref, g_ref, sc_ref, dres_ref = refs[:5]
        dx_ref, dg_ref, dsc_ref, dsh_ref = refs[-6:-2] if gated else refs[-4:]
        i = pl.program_id(0)
        xv = x_ref[...]
        dhv = dh_ref[...].astype(F32)
        rstd = lax.rsqrt(jnp.mean(xv * xv, axis=-1, keepdims=True) + NORM_EPS)
        xhat = xv * rstd
        one_sc = 1.0 + sc_ref[...]
        g = g_ref[...]
        dxhat = dhv * (g * one_sc)
        proj = jnp.mean(dxhat * xhat, axis=-1, keepdims=True)
        dxv = dres_ref[...] + rstd * (dxhat - xhat * proj)
        dx_ref[...] = dxv
        dhx = dhv * xhat
        first = [(dg_ref, jnp.sum(dhx * one_sc, axis=0, keepdims=True))]
        per_seq = [(dsc_ref, jnp.sum(dhx * g, axis=0, keepdims=True)), (dsh_ref, jnp.sum(dhv, axis=0, keepdims=True))]
        if gated:
            y_ref, gate_ref, dy_ref, dgate_ref = refs[5], refs[6], refs[-2], refs[-1]
            dy_ref[...] = (dxv * gate_ref[...]).astype(BF)
            per_seq.append((dgate_ref, jnp.sum(dxv * y_ref[...].astype(F32), axis=0, keepdims=True)))
        for cond_new, cond_add, group in ((i == 0, i > 0, first),
                                          (i % steps_per_seq == 0, i % steps_per_seq != 0, per_seq)):
            @pl.when(cond_new)
            def _(group=group):
                for ref, part in group:
                    ref[...] = part

            @pl.when(cond_add)
            def _(group=group):
                for ref, part in group:
                    ref[...] += part

    row = pl.BlockSpec((tr, w), lambda i: (i, 0))
    per_b = pl.BlockSpec((None, 1, w), lambda i: ((i * tr) // seq, 0, 0))
    vec = pl.BlockSpec((1, w), lambda i: (0, 0))
    out_shape = [jax.ShapeDtypeStruct((t, w), F32), jax.ShapeDtypeStruct((1, w), F32),
                 jax.ShapeDtypeStruct((nb, 1, w), F32), jax.ShapeDtypeStruct((nb, 1, w), F32)]
    in_specs, out_specs, operands = [row, row, vec, per_b, row], [row, vec, per_b, per_b], [dh, x, gain, scale, dres]
    if gated:
        in_specs += [row, per_b]
        operands += list(gate)
        out_shape += [jax.ShapeDtypeStruct((t, w), BF), jax.ShapeDtypeStruct((nb, 1, w), F32)]
        out_specs += [row, per_b]
    return pl.pallas_call(
        body, name=name, grid=(t // tr,),
        out_shape=out_shape, in_specs=in_specs, out_specs=out_specs,
        compiler_params=_params(("arbitrary",)),
    )(*operands)


def _gate_bwd(dx, y, gate, seq, name):
    t, w = dx.shape
    tr = ROW_BLOCK
    steps_per_seq = seq // tr
    nb = t // seq

    def body(dx_ref, y_ref, g_ref, dy_ref, dg_ref):
        i = pl.program_id(0)
        dxv = dx_ref[...]
        dy_ref[...] = (dxv * g_ref[...]).astype(BF)
        part = jnp.sum(dxv * y_ref[...].astype(F32), axis=0, keepdims=True)

        @pl.when(i % steps_per_seq == 0)
        def _():
            dg_ref[...] = part

        @pl.when(i % steps_per_seq != 0)
        def _():
            dg_ref[...] += part

    row = pl.BlockSpec((tr, w), lambda i: (i, 0))
    per_b = pl.BlockSpec((None, 1, w), lambda i: ((i * tr) // seq, 0, 0))
    return pl.pallas_call(
        body, name=name, grid=(t // tr,),
        out_shape=(jax.ShapeDtypeStruct((t, w), BF), jax.ShapeDtypeStruct((nb, 1, w), F32)),
        in_specs=[row, row, per_b], out_specs=(row, per_b),
        compiler_params=_params(("arbitrary",)),
    )(dx, y, gate)


def _loss_head(x, gain, target, name):
    t, w = x.shape
    tr = ROW_BLOCK

    def body(x_ref, g_ref, t_ref, loss_ref, dx_ref, dg_ref):
        i = pl.program_id(0)
        xv = x_ref[...]
        g = g_ref[...]
        rstd = lax.rsqrt(jnp.mean(xv * xv, axis=-1, keepdims=True) + NORM_EPS)
        xhat = xv * rstd
        err = xhat * g - t_ref[...]
        row_loss = jnp.sum(err * err, axis=-1, keepdims=True) * (0.5 / w)
        loss_part = jnp.broadcast_to(jnp.sum(row_loss, axis=0, keepdims=True), (1, LANES))
        dy = err * (1.0 / w)
        dg_part = jnp.sum(dy * xhat, axis=0, keepdims=True)
        dxhat = dy * g
        proj = jnp.mean(dxhat * xhat, axis=-1, keepdims=True)
        dx_ref[...] = rstd * (dxhat - xhat * proj)

        @pl.when(i == 0)
        def _():
            loss_ref[...] = loss_part
            dg_ref[...] = dg_part

        @pl.when(i > 0)
        def _():
            loss_ref[...] += loss_part
            dg_ref[...] += dg_part

    row = pl.BlockSpec((tr, w), lambda i: (i, 0))
    vec = pl.BlockSpec((1, w), lambda i: (0, 0))
    return pl.pallas_call(
        body, name=name, grid=(t // tr,),
        out_shape=(jax.ShapeDtypeStruct((1, LANES), F32), jax.ShapeDtypeStruct((t, w), F32),
                   jax.ShapeDtypeStruct((1, w), F32)),
        in_specs=[row, vec, row],
        out_specs=(pl.BlockSpec((1, LANES), lambda i: (0, 0)), row, vec),
        compiler_params=_params(("arbitrary",)),
    )(x, gain, target)


def _rope_group(xg, cos_p, sin_a, sin_b):
    return (xg * cos_p + pltpu.roll(xg, LANES - ROPE_HALF, axis=1) * sin_a
            + pltpu.roll(xg, ROPE_HALF, axis=1) * sin_b)


def _rope(x, tables, name, out_dtype=BF):
    t, w = x.shape
    tr = ROW_BLOCK
    groups = w // LANES

    def body(x_ref, c_ref, a_ref, b_ref, out_ref):
        cos_p, sin_a, sin_b = c_ref[...], a_ref[...], b_ref[...]
        for g in range(groups):
            sl = slice(g * LANES, (g + 1) * LANES)
            out_ref[:, sl] = _rope_group(x_ref[:, sl].astype(F32), cos_p, sin_a, sin_b).astype(out_dtype)

    row = pl.BlockSpec((tr, w), lambda i: (i, 0))
    tab = pl.BlockSpec((tr, LANES), lambda i: (i, 0))
    return pl.pallas_call(
        body, name=name, grid=(t // tr,),
        out_shape=jax.ShapeDtypeStruct((t, w), out_dtype),
        in_specs=[row, tab, tab, tab], out_specs=row,
        compiler_params=_params(("parallel",)),
    )(x, *tables)


def _mla_mid(down, gq, gkv, tables, name):
    t = down.shape[0]
    tr = ROW_BLOCK

    def body(d_ref, gq_ref, gkv_ref, c_ref, a_ref, b_ref, cq_ref, ckr_ref):
        q = d_ref[:, 0:256]
        cq_ref[...] = (q * lax.rsqrt(jnp.mean(q * q, axis=-1, keepdims=True) + NORM_EPS) * gq_ref[...]).astype(BF)
        kv = d_ref[:, 256:384]
        ckr_ref[:, 0:128] = (kv * lax.rsqrt(jnp.mean(kv * kv, axis=-1, keepdims=True) + NORM_EPS)
                             * gkv_ref[...]).astype(BF)
        ckr_ref[:, 128:256] = _rope_group(d_ref[:, 384:512], c_ref[...], a_ref[...], b_ref[...]).astype(BF)

    tab = pl.BlockSpec((tr, LANES), lambda i: (i, 0))
    return pl.pallas_call(
        body, name=name, grid=(t // tr,),
        out_shape=(jax.ShapeDtypeStruct((t, 256), BF), jax.ShapeDtypeStruct((t, 256), BF)),
        in_specs=[pl.BlockSpec((tr, 512), lambda i: (i, 0)), pl.BlockSpec((1, 256), lambda i: (0, 0)),
                  pl.BlockSpec((1, 128), lambda i: (0, 0)), tab, tab, tab],
        out_specs=(pl.BlockSpec((tr, 256), lambda i: (i, 0)), pl.BlockSpec((tr, 256), lambda i: (i, 0))),
        compiler_params=_params(("parallel",)),
    )(down, gq, gkv, *tables)


def _mla_mid_bwd(down, dcq, dckr, gq, gkv, tables_t, name):
    t = down.shape[0]
    tr = ROW_BLOCK

    def norm_bwd(xv, g, dy):
        rstd = lax.rsqrt(jnp.mean(xv * xv, axis=-1, keepdims=True) + NORM_EPS)
        xhat = xv * rstd
        dxhat = dy * g
        proj = jnp.mean(dxhat * xhat, axis=-1, keepdims=True)
        return rstd * (dxhat - xhat * proj), jnp.sum(dy * xhat, axis=0, keepdims=True)

    def body(d_ref, dcq_ref, dckr_ref, gq_ref, gkv_ref, c_ref, a_ref, b_ref, dd_ref, dgq_ref, dgkv_ref):
        i = pl.program_id(0)
        dq, dgq_part = norm_bwd(d_ref[:, 0:256], gq_ref[...], dcq_ref[...].astype(F32))
        dd_ref[:, 0:256] = dq.astype(BF)
        dkv, dgkv_part = norm_bwd(d_ref[:, 256:384], gkv_ref[...], dckr_ref[:, 0:128].astype(F32))
        dd_ref[:, 256:384] = dkv.astype(BF)
        dd_ref[:, 384:512] = _rope_group(dckr_ref[:, 128:256].astype(F32), c_ref[...], a_ref[...],
                                         b_ref[...]).astype(BF)

        @pl.when(i == 0)
        def _():
            dgq_ref[...] = dgq_part
            dgkv_ref[...] = dgkv_part

        @pl.when(i > 0)
        def _():
            dgq_ref[...] += dgq_part
            dgkv_ref[...] += dgkv_part

    tab = pl.BlockSpec((tr, LANES), lambda i: (i, 0))
    r256 = pl.BlockSpec((tr, 256), lambda i: (i, 0))
    return pl.pallas_call(
        body, name=name, grid=(t // tr,),
        out_shape=(jax.ShapeDtypeStruct((t, 512), BF), jax.ShapeDtypeStruct((1, 256), F32),
                   jax.ShapeDtypeStruct((1, 128), F32)),
        in_specs=[pl.BlockSpec((tr, 512), lambda i: (i, 0)), r256, r256, pl.BlockSpec((1, 256), lambda i: (0, 0)),
                  pl.BlockSpec((1, 128), lambda i: (0, 0)), tab, tab, tab],
        out_specs=(pl.BlockSpec((tr, 512), lambda i: (i, 0)), pl.BlockSpec((1, 256), lambda i: (0, 0)),
                   pl.BlockSpec((1, 128), lambda i: (0, 0))),
        compiler_params=_params(("arbitrary",)),
    )(down, dcq, dckr, gq, gkv, *tables_t)


def _scan_rows(x, reverse):
    s = x.shape[0]
    row = lax.broadcasted_iota(jnp.int32, x.shape, 0)
    step = 1
    while step < s:
        if reverse:
            x = x + jnp.where(row < s - step, pltpu.roll(x, s - step, axis=0), 0.0)
        else:
            x = x + jnp.where(row >= step, pltpu.roll(x, step, axis=0), 0.0)
        step *= 2
    return x


def _fox_gate(fg, b_f, seq, name):
    t = fg.shape[0]

    def body(fg_ref, b_ref, out_ref):
        z = fg_ref[...] + b_ref[...]
        log_f = jnp.minimum(z, 0.0) - jnp.log(1.0 + jnp.exp(-jnp.abs(z)))
        out_ref[...] = _scan_rows(log_f, reverse=False)

    blk = pl.BlockSpec((seq, LANES), lambda b: (b, 0))
    return pl.pallas_call(
        body, name=name, grid=(t // seq,),
        out_shape=jax.ShapeDtypeStruct((t, LANES), F32),
        in_specs=[blk, pl.BlockSpec((1, LANES), lambda b: (0, 0))], out_specs=blk,
        compiler_params=_params(("parallel",)),
    )(fg, b_f)


def _fox_gate_bwd(d_cum, fg, b_f, seq, name):
    t = fg.shape[0]

    def body(dc_ref, fg_ref, b_ref, dfg_ref, db_ref):
        b = pl.program_id(0)
        z = fg_ref[...] + b_ref[...]
        d_log_f = _scan_rows(dc_ref[...], reverse=True)
        dz = d_log_f / (1.0 + jnp.exp(z))
        dfg_ref[...] = dz
        part = jnp.sum(dz, axis=0, keepdims=True)

        @pl.when(b == 0)
        def _():
            db_ref[...] = part

        @pl.when(b > 0)
        def _():
            db_ref[...] += part

    blk = pl.BlockSpec((seq, LANES), lambda b: (b, 0))
    vec = pl.BlockSpec((1, LANES), lambda b: (0, 0))
    return pl.pallas_call(
        body, name=name, grid=(t // seq,),
        out_shape=(jax.ShapeDtypeStruct((t, LANES), F32), jax.ShapeDtypeStruct((1, LANES), F32)),
        in_specs=[blk, blk, vec], out_specs=(blk, vec),
        compiler_params=_params(("arbitrary",)),
    )(d_cum, fg, b_f)


def _head_masks():
    lane = lax.broadcasted_iota(jnp.int32, (1, LANES), 1)
    return lane < HEAD_DIM, lane >= HEAD_DIM


def _pair_operands(ref, r0, n, compact, masks, masked):
    if not compact:
        return [ref[pl.ds(r0, n), h * LANES:(h + 1) * LANES] for h in range(2)]
    pair = ref[pl.ds(r0, n), :]
    return [jnp.where(mk, pair, jnp.zeros_like(pair)) for mk in masks] if masked else [pair, pair]


def _causal(n_rows, n_cols, shift):
    return (lax.broadcasted_iota(jnp.int32, (n_rows, n_cols), 1)
            <= lax.broadcasted_iota(jnp.int32, (n_rows, n_cols), 0) + shift)


def _attn_fwd(q_arr, q_off, k_arr, k_off, v_arr, v_off, bias, seq, name, ride=None, compact=False):
    t = q_arr.shape[0]
    nb = t // seq
    blk = min(ATTN_BLOCK, seq)
    nq = seq // blk
    qw = LANES if compact else 2 * LANES
    has_bias = bias is not None
    n_in, n_out = (4, 3) if has_bias else (3, 2)

    def body(*refs):
        step = pl.program_id(0) * HEAD_PAIRS + pl.program_id(1)
        refs, ride_end = _ride_steps(ride, refs, n_in, n_out, step, nb * HEAD_PAIRS)
        if has_bias:
            q_ref, k_ref, v_ref, bias_ref, o_ref, lse_ref, o32_ref = refs
        else:
            q_ref, k_ref, v_ref, o_ref, lse_ref = refs
        masks = _head_masks()
        lo = masks[0]

        def update(r0, n, carry, k0, nk, mask):
            qs = _pair_operands(q_ref, r0, n, compact, masks, True)
            ks = _pair_operands(k_ref, k0, nk, compact, masks, False)
            vv = v_ref[pl.ds(k0, nk), :]
            vs = [jnp.where(mk, vv, jnp.zeros_like(vv)) for mk in masks]
            new, alphas, pv = [], [], None
            for h in range(2):
                m, l = carry[1 + 2 * h], carry[2 + 2 * h]
                s = lax.dot_general(qs[h], ks[h], _DIMS["nt"], preferred_element_type=F32)
                if has_bias:
                    s = s + bias_ref[h, 0:1, pl.ds(k0, nk)]
                if mask is not None:
                    s = jnp.where(mask, s, -jnp.inf)
                m_new = jnp.maximum(m, jnp.max(s, axis=-1, keepdims=True))
                p = jnp.exp(s - m_new)
                alpha = jnp.exp(m - m_new)
                l_new = alpha * l + jnp.sum(p, axis=-1, keepdims=True)
                p_hi = p.astype(BF)
                d = jnp.dot(p_hi, vs[h], preferred_element_type=F32)
                if has_bias:
                    p_lo = (p - p_hi.astype(F32)).astype(BF)
                    d = d + jnp.dot(p_lo, vs[h], preferred_element_type=F32)
                pv = d if pv is None else pv + d
                alphas.append(alpha)
                new += [m_new, l_new]
            return (carry[0] * jnp.where(lo, alphas[0], alphas[1]) + pv, *new)

        def q_block(iq, _):
            q0 = pl.multiple_of(iq * blk, blk)
            init = (jnp.zeros((blk, LANES), F32),
                    jnp.full((blk, 1), -jnp.inf, F32), jnp.zeros((blk, 1), F32),
                    jnp.full((blk, 1), -jnp.inf, F32), jnp.zeros((blk, 1), F32))
            full = lambda j, c: update(q0, blk, c, pl.multiple_of(j * blk, blk), blk, None)
            carry = lax.fori_loop(0, iq // 2, lambda jj, c: full(2 * jj + 1, full(2 * jj, c)), init)

            def last(c, odd):
                if odd:
                    c = full(iq - 1, c)
                acc, m0, l0, m1, l1 = update(q0, blk, c, q0, blk, _causal(blk, blk, 0))
                o_val = acc / jnp.where(lo, l0, l1)
                o_ref[pl.ds(q0, blk), :] = o_val.astype(BF)
                if has_bias:
                    o32_ref[pl.ds(q0, blk), :] = o_val
                lse_ref[pl.ds(q0, blk), :] = jnp.where(lo, m0 + jnp.log(l0), m1 + jnp.log(l1))
                return 0

            return lax.cond(iq % 2 == 1, functools.partial(last, odd=True), functools.partial(last, odd=False), carry)

        lax.fori_loop(0, nq, q_block, 0)
        ride_end()

    in_specs = [pl.BlockSpec((seq, qw), lambda b, p: (b, q_off + p)),
                pl.BlockSpec((seq, qw), lambda b, p: (b, k_off + p)),
                pl.BlockSpec((seq, LANES), lambda b, p: (b, v_off + p))]
    args = [q_arr, k_arr, v_arr]
    if has_bias:
        in_specs.append(pl.BlockSpec((None, 2, 8, seq), lambda b, p: (b, p, 0, 0)))
        args.append(bias)
    out_blk = pl.BlockSpec((seq, LANES), lambda b, p: (b, p))
    out_shape = [jax.ShapeDtypeStruct((t, HEAD_PAIRS * LANES), BF), jax.ShapeDtypeStruct((t, HEAD_PAIRS * LANES), F32)]
    if has_bias:
        out_shape.append(jax.ShapeDtypeStruct((t, HEAD_PAIRS * LANES), F32))
    out_specs, scratch = [out_blk] * len(out_shape), []
    if ride is not None:
        r_in, r_shape, r_out, scratch = _ride_specs(ride)
        in_specs, out_shape, out_specs = in_specs + r_in, out_shape + r_shape, out_specs + r_out
        args += list(ride[1])
    return pl.pallas_call(
        body, name=name, grid=(nb, HEAD_PAIRS),
        out_shape=out_shape, in_specs=in_specs, out_specs=out_specs, scratch_shapes=scratch,
        compiler_params=_params(("arbitrary", "arbitrary")),
    )(*args)


def _attn_bwd(q_arr, q_off, k_arr, k_off, v_arr, v_off, bias, o, do, lse, seq, name, ride=None, compact=False):
    t = q_arr.shape[0]
    nb = t // seq
    blk = min(ATTN_BLOCK, seq)
    half = blk // 2
    nq = seq // blk
    qw = LANES if compact else 2 * LANES
    has_bias = bias is not None
    n_in, n_out = (7, 4) if has_bias else (6, 3)

    def body(*refs):
        step = pl.program_id(0) * HEAD_PAIRS + pl.program_id(1)
        refs, ride_end = _ride_steps(ride, refs, n_in, n_out, step, nb * HEAD_PAIRS)
        if has_bias:
            (q_ref, k_ref, v_ref, bias_ref, o_ref, do_ref, lse_ref,
             dq_ref, dk_ref, dv_ref, dbias_ref, dq_acc, dsum) = refs
        else:
            (q_ref, k_ref, v_ref, o_ref, do_ref, lse_ref, dq_ref, dk_ref, dv_ref, dq_acc, dsum) = refs
        masks = _head_masks()
        lo, hi = masks
        dq_acc[...] = jnp.zeros_like(dq_acc)

        def prep(iq, _):
            q0 = pl.multiple_of(iq * blk, blk)
            prod = do_ref[pl.ds(q0, blk), :].astype(F32) * o_ref[pl.ds(q0, blk), :].astype(F32)
            d0 = jnp.sum(jnp.where(lo, prod, 0.0), axis=-1, keepdims=True)
            d1 = jnp.sum(jnp.where(hi, prod, 0.0), axis=-1, keepdims=True)
            dsum[pl.ds(q0, blk), :] = jnp.where(lo, d0, d1)
            return 0

        lax.fori_loop(0, nq, prep, 0)

        def tile(r0, n, k0, nk, mask):
            qs = _pair_operands(q_ref, r0, n, compact, masks, True)
            ks = _pair_operands(k_ref, k0, nk, compact, masks, True)
            vv = v_ref[pl.ds(k0, nk), :]
            vs = [jnp.where(mk, vv, jnp.zeros_like(vv)) for mk in masks]
            dov = do_ref[pl.ds(r0, n), :]
            dos = [jnp.where(mk, dov, jnp.zeros_like(dov)) for mk in masks] if compact else None
            lse_v = lse_ref[pl.ds(r0, n), :]
            dsum_v = dsum[pl.ds(r0, n), :]
            dv_c, dks, dbs = None, [], []
            for h in range(2):
                s = lax.dot_general(qs[h], ks[h], _DIMS["nt"], preferred_element_type=F32)
                if has_bias:
                    s = s + bias_ref[h, 0:1, pl.ds(k0, nk)]
                p = jnp.exp(s - lse_v[:, h * HEAD_DIM:h * HEAD_DIM + 1])
                if mask is not None:
                    p = jnp.where(mask, p, 0.0)
                dp = lax.dot_general(dov, vs[h], _DIMS["nt"], preferred_element_type=F32)
                ds = p * (dp - dsum_v[:, h * HEAD_DIM:h * HEAD_DIM + 1])
                ds_bf = ds.astype(BF)
                if compact:
                    dv_h = lax.dot_general(p.astype(BF), dos[h], _DIMS["tn"], preferred_element_type=F32)
                else:
                    dv_h = jnp.where(masks[h], lax.dot_general(p.astype(BF), dov, _DIMS["tn"],
                                                               preferred_element_type=F32), 0.0)
                dv_c = dv_h if dv_c is None else dv_c + dv_h
                dk_h = lax.dot_general(ds_bf, qs[h], _DIMS["tn"], preferred_element_type=F32)
                dq_h = jnp.dot(ds_bf, ks[h], preferred_element_type=F32)
                if compact:
                    dks = [dk_h] if h == 0 else [dks[0] + dk_h, jnp.zeros((8, LANES), F32)]
                    if h == 0:
                        dq_first = dq_h
                    else:
                        dq_acc[pl.ds(r0, n), :] += dq_first + dq_h
                else:
                    dks.append(dk_h)
                    dq_acc[pl.ds(r0, n), h * LANES:(h + 1) * LANES] += dq_h
                dbs.append(jnp.sum(ds, axis=0, keepdims=True) if has_bias else jnp.zeros((1, nk), F32))
            return (dv_c, dks[0], dks[1], dbs[0], dbs[1])

        def kv_block(j, _):
            k0 = pl.multiple_of(j * blk, blk)
            if compact:
                dv_a, dk_a, dummy, db0_a, db1_a = tile(pl.multiple_of(k0 + half, half), half, k0, blk,
                                                       _causal(half, blk, half))
                top = tile(k0, half, k0, half, _causal(half, half, 0))
                head = lambda acc, x: jnp.concatenate([acc[:half] + x, acc[half:]], axis=0)
                lead = lambda acc, x: jnp.concatenate([acc[:, :half] + x, acc[:, half:]], axis=1)
                carry = (head(dv_a, top[0]), head(dk_a, top[1]), dummy, lead(db0_a, top[3]), lead(db1_a, top[4]))
            else:
                carry = tile(k0, blk, k0, blk, _causal(blk, blk, 0))

            def q_block(iq, c):
                part = tile(pl.multiple_of(iq * blk, blk), blk, k0, blk, None)
                return tuple(a + b for a, b in zip(c, part))

            n_full = nq - 1 - j
            carry = lax.fori_loop(0, n_full // 2, lambda jj, c: q_block(j + 2 + 2 * jj, q_block(j + 1 + 2 * jj, c)),
                                  carry)

            def last(c, odd):
                if odd:
                    c = q_block(nq - 1, c)
                dv_ref[pl.ds(k0, blk), :] = c[0].astype(BF)
                if compact:
                    dk_ref[pl.ds(k0, blk), :] = c[1].astype(BF)
                else:
                    for h in range(2):
                        dk_ref[pl.ds(k0, blk), h * LANES:(h + 1) * LANES] = c[1 + h].astype(BF)
                if has_bias:
                    for h in range(2):
                        dbias_ref[h, :, pl.ds(k0, blk)] = jnp.broadcast_to(c[3 + h], (8, blk))
                return 0

            return lax.cond(n_full % 2 == 1, functools.partial(last, odd=True), functools.partial(last, odd=False),
                            carry)

        lax.fori_loop(0, nq, kv_block, 0)
        dq_ref[...] = dq_acc[...].astype(BF)
        ride_end()

    pair256 = lambda off: pl.BlockSpec((seq, qw), lambda b, p: (b, off + p))
    pair128 = lambda off: pl.BlockSpec((seq, LANES), lambda b, p: (b, off + p))
    bias_spec = pl.BlockSpec((None, 2, 8, seq), lambda b, p: (b, p, 0, 0))
    in_specs = [pair256(q_off), pair256(k_off), pair128(v_off)]
    args = [q_arr, k_arr, v_arr]
    if has_bias:
        in_specs.append(bias_spec)
        args.append(bias)
    in_specs += [pair128(0), pair128(0), pair128(0)]
    args += [o, do, lse]
    out_shape = [jax.ShapeDtypeStruct((t, HEAD_PAIRS * qw), BF),
                 jax.ShapeDtypeStruct((t, HEAD_PAIRS * qw), BF),
                 jax.ShapeDtypeStruct((t, HEAD_PAIRS * LANES), BF)]
    out_specs = [pair256(0), pair256(0), pair128(0)]
    if has_bias:
        out_shape.append(jax.ShapeDtypeStruct((nb, HEADS, 8, seq), F32))
        out_specs.append(bias_spec)
    scratch = [pltpu.VMEM((seq, qw), F32), pltpu.VMEM((seq, LANES), F32)]
    if ride is not None:
        r_in, r_shape, r_out, r_scratch = _ride_specs(ride)
        in_specs, out_shape, out_specs = in_specs + r_in, out_shape + r_shape, out_specs + r_out
        args += list(ride[1])
        scratch += r_scratch
    return pl.pallas_call(
        body, name=name, grid=(nb, HEAD_PAIRS),
        out_shape=out_shape, in_specs=in_specs, out_specs=out_specs, scratch_shapes=scratch,
        compiler_params=_params(("arbitrary", "arbitrary")),
    )(*args)


def _adamw(w, g, m, v, name):
    shape = w.shape
    last = shape[-1]
    rows = int(np.prod(shape[:-1])) if len(shape) > 1 else 1
    tr = _rows(rows, 512)

    def body(w_ref, g_ref, m_ref, v_ref, d_ref, nm_ref, nv_ref):
        d_ref[...], nm_ref[...], nv_ref[...] = _adamw_math(w_ref[...], g_ref[...], m_ref[...], v_ref[...])

    blk = pl.BlockSpec((tr, last), lambda i: (i, 0))
    sds = jax.ShapeDtypeStruct((rows, last), F32)
    outs = pl.pallas_call(
        body, name=name, grid=(rows // tr,),
        out_shape=(sds, sds, sds), in_specs=[blk] * 4, out_specs=(blk,) * 3,
        compiler_params=_params(("parallel",)),
    )(*[a.reshape(rows, last) for a in (w, g, m, v)])
    return tuple(a.reshape(shape) for a in outs)


LOW_COLS = 256


def _low_pad(a):
    return jnp.pad(a, ((0, 0),) * (a.ndim - 1) + ((0, LOW_COLS - a.shape[-1]),))


def _layer_shards(w, i):
    j = i // 2
    bf = lambda a: a.astype(BF)
    if i % 2 == 0:
        mixer = [bf(w["fox_w_in"][j]), bf(w["fox_w_out"][j])]
    else:
        mixer = [jnp.concatenate([bf(w["mla_w_dq"][j]), bf(w["mla_w_ukv"][j]), _low_pad(bf(w["mla_w_uq"][j])),
                                  _low_pad(bf(w["mla_w_dkv"][j]))], axis=0), bf(w["mla_w_out"][j])]
    return mixer + [bf(w["mlp_w1"][i]), bf(w["mlp_w2"][i])]


def _side_by_side(stack, r0, rows, cols=None):
    return jnp.concatenate([stack[dd, r0:r0 + rows, :cols] for dd in range(N_DEV)], axis=1)


def _stacked(stack, r0, rows, cols=None):
    part = stack[:, r0:r0 + rows, :cols]
    return part.reshape(N_DEV * rows, part.shape[2])


def _layer_mixer_weights(i, first):
    if i % 2 == 0:
        return dict(fox_w_in=_side_by_side(first, 0, 1024))
    return dict(mla_w_dq=_stacked(first, 0, 128), mla_w_ukv=_side_by_side(first, 128, 128),
                mla_w_uq=_side_by_side(first, 256, 256, 192), mla_w_dkv=_stacked(first, 512, 128, 160))


def _by_dest_rows(g):
    return g.reshape(N_DEV, g.shape[0] // N_DEV, g.shape[1]).astype(BF)


def _by_dest_cols(g):
    n = g.shape[1] // N_DEV
    return jnp.stack([g[:, dd * n:(dd + 1) * n] for dd in range(N_DEV)]).astype(BF)


def _layer_mixer_grad_bufs(i, g):
    if i % 2 == 0:
        return [_by_dest_cols(g["fox_w_in"]), _by_dest_rows(g["w_out"])]
    low = jnp.concatenate([_by_dest_rows(g["mla_w_dq"]), _by_dest_cols(g["mla_w_ukv"]),
                           _low_pad(_by_dest_cols(g["mla_w_uq"])), _low_pad(_by_dest_rows(g["mla_w_dkv"]))], axis=1)
    return [low, _by_dest_rows(g["w_out"])]


def _low_shard_grads(low):
    return dict(mla_w_dq=low[:128], mla_w_ukv=low[128:256], mla_w_uq=low[256:512, :192], mla_w_dkv=low[512:, :160])


def _pad_heads(w, width):
    k = w.shape[0]
    return jnp.pad(w.reshape(k, HEADS, width), ((0, 0), (0, 0), (0, LANES - width))).reshape(k, HEADS * LANES)


def _unpad_heads(w, width):
    k = w.shape[0]
    return w.reshape(k, HEADS, LANES)[:, :, :width].reshape(k, HEADS * width)


def _rope_tables(positions, scale):
    inv_freq = 10000.0 ** (-jnp.arange(0, 2 * ROPE_HALF, 2, dtype=F32) / (2 * ROPE_HALF))
    ang = positions.astype(F32)[:, None] * inv_freq
    cos, sin = jnp.cos(ang) * scale, jnp.sin(ang) * scale
    t = positions.shape[0]
    z = lambda n: jnp.zeros((t, n), F32)
    cos_p = jnp.concatenate([jnp.full((t, HEAD_DIM), scale, F32), cos, cos, z(32)], axis=1)
    sin_a = jnp.concatenate([z(64), -sin, z(48)], axis=1)
    sin_b = jnp.concatenate([z(80), sin, z(32)], axis=1)
    fwd = (cos_p, sin_a, sin_b)
    bwd = (cos_p, jnp.roll(sin_b, -ROPE_HALF, axis=1), jnp.roll(sin_a, ROPE_HALF, axis=1))
    return fwd, bwd


def _key_rows(cum, nb, seq):
    v = -cum.reshape(nb, seq, LANES)[:, :, :HEADS]
    return jnp.broadcast_to(jnp.transpose(v, (0, 2, 1))[:, :, None, :], (nb, HEADS, 8, seq))


def kernel(x, c, positions, ada_w, ada_b, norm_mix_g, norm_mlp_g, fox_w_in, fox_b_f, fox_w_out, mla_w_dq, mla_q_norm_g, mla_w_uq, mla_w_dkv, mla_kv_norm_g, mla_w_ukv, mla_w_out, mlp_w1, mlp_w2, final_norm_g, loss_target, m_ada_w, m_ada_b, m_norm_mix_g, m_norm_mlp_g, m_fox_w_in, m_fox_b_f, m_fox_w_out, m_mla_w_dq, m_mla_q_norm_g, m_mla_w_uq, m_mla_w_dkv, m_mla_kv_norm_g, m_mla_w_ukv, m_mla_w_out, m_mlp_w1, m_mlp_w2, m_final_norm_g, v_ada_w, v_ada_b, v_norm_mix_g, v_norm_mlp_g, v_fox_w_in, v_fox_b_f, v_fox_w_out, v_mla_w_dq, v_mla_q_norm_g, v_mla_w_uq, v_mla_w_dkv, v_mla_kv_norm_g, v_mla_w_ukv, v_mla_w_out, v_mlp_w1, v_mlp_w2, v_final_norm_g):
    args = dict(locals())
    weights = {n: args[n] for n in WEIGHTS}
    nb, seq, d = x.shape
    t = nb * seq
    depth = ada_w.shape[0]
    dev = 4 * lax.axis_index("x") + 2 * lax.axis_index("y") + lax.axis_index("c")
    n_mod_local = ada_w.shape[2]

    n_qg = mla_q_norm_g.shape[1]
    cond = jnp.concatenate([c, jnp.pad(mla_q_norm_g.reshape(1, -1), ((0, 7), (0, d - 2 * n_qg)))], axis=0)
    w1_rows, w2_rows = mlp_w1.shape[1], mlp_w2.shape[1]
    shards = [_layer_shards(weights, i) for i in range(depth)]
    stacks = [None] * depth
    cond_all, = _all_gather([cond], "gather_cond")
    c_all = cond_all[:, :nb].reshape(N_DEV * nb, d)
    q_gain = jnp.transpose(cond_all[:, nb, :2 * n_qg].reshape(N_DEV, 2, n_qg), (1, 0, 2)).reshape(2, N_DEV * n_qg)
    mod_local = jnp.stack([
        _matmul(c_all, ada_w[i], mode="nn", name="ada_mod", out_dtype=F32, a_act="silu", epi="bias",
                extras=(lax.dynamic_slice_in_dim(ada_b[i], dev * n_mod_local, n_mod_local)[None, :],))
        for i in range(depth)])
    mod_all, first_in = _all_gather([mod_local.reshape(depth * N_DEV * nb, n_mod_local), shards[0][0]], "gather_first")
    stacks[0] = [first_in]
    mod_all = jnp.transpose(mod_all.reshape(N_DEV, depth, N_DEV * nb, n_mod_local), (1, 2, 0, 3))
    mod_all = mod_all.reshape(depth, N_DEV * nb, N_DEV * n_mod_local)
    mod = lax.dynamic_slice_in_dim(mod_all, dev * nb, nb, axis=1)
    mod = mod.reshape(depth, nb, 6, 1, d)

    pos = positions.reshape(t)
    rope_q, rope_q_t = _rope_tables(pos, MLA_SCALE)
    rope_k, rope_k_t = _rope_tables(pos, 1.0)

    def fox_weights(full):
        w_in = full["fox_w_in"]
        w_qkv = jnp.concatenate([w_in[:, :d] * FOX_SCALE, w_in[:, d:3 * d]], axis=1)
        w_f = jnp.pad(w_in[:, 3 * d:], ((0, 0), (0, LANES - HEADS)))
        return w_qkv, w_f

    def mla_weights(full):
        w_dkv = full["mla_w_dkv"]
        w_down = jnp.concatenate([full["mla_w_dq"], w_dkv[:, :128], jnp.zeros((d, 64), BF),
                                  w_dkv[:, 128:160], jnp.zeros((d, 32), BF)], axis=1)
        w_uq = _pad_heads(full["mla_w_uq"], 96)
        w_ukv = full["mla_w_ukv"].reshape(128, HEADS, 2, HEAD_DIM)
        w_uk = jnp.pad(w_ukv[:, :, 0, :], ((0, 0), (0, 0), (0, 64))).reshape(128, HEADS * LANES)
        w_uv = w_ukv[:, :, 1, :].reshape(128, HEADS * HEAD_DIM)
        place = np.zeros((128, HEADS, LANES), np.float32)
        for i in range(2 * ROPE_HALF):
            place[64 + i, :, 64 + i] = 1.0
        bottom = jnp.concatenate([jnp.asarray(place.reshape(128, HEADS * LANES), BF),
                                  jnp.zeros((128, HEADS * HEAD_DIM), BF)], axis=1)
        w_kv = jnp.concatenate([jnp.concatenate([w_uk, w_uv], axis=1), bottom], axis=0)
        return w_down, w_uq, w_kv

    tm_big = min(2048, t)
    xs = x.reshape(t, d)
    saved = []
    for i in range(depth):
        j = i // 2
        sh_m, sc_m, g_m, sh_f, sc_f, g_f = (mod[i, :, q] for q in range(6))
        gain_mix = norm_mix_g[i][None, :]
        gain_mlp = norm_mlp_g[i][None, :]
        s = dict(x_in=xs)
        h = _norm_mod(xs, gain_mix, sc_m, sh_m, seq, "norm_mix")
        s["h"] = h
        full = _layer_mixer_weights(i, stacks[i][0])
        riders = shards[i][1 if i == 0 else 2:] + (shards[i + 1][:2] if i + 1 < depth else [])
        ride = ("gather", riders)
        if i % 2 == 0:
            w_qkv, w_f = fox_weights(full)
            qkv = _matmul(h, w_qkv, mode="nn", name="fox_qkv")
            fg = _matmul(h, w_f, mode="nn", name="fox_gate_logits", out_dtype=F32)
            b_f = jnp.pad(fox_b_f[j], (0, LANES - HEADS))[None, :]
            cum = _fox_gate(fg, b_f, seq, "fox_gate")
            bias = _key_rows(cum, nb, seq)
            o, lse, o32, *rode = _attn_fwd(qkv, 0, qkv, 8, qkv, 16, bias, seq, "fox_attn", ride, compact=True)
            s.update(qkv=qkv, fg=fg, b_f=b_f, bias=bias, w_qkv=w_qkv, w_f=w_f, o32=o32)
        else:
            w_down, w_uq, w_kv = mla_weights(full)
            down = _matmul(h, w_down, mode="nn", name="mla_down", out_dtype=F32)
            gq, gkv = q_gain[j][None, :], mla_kv_norm_g[j][None, :]
            cq, ckr = _mla_mid(down, gq, gkv, rope_k, "mla_mid")
            q_raw = _matmul(cq, w_uq, mode="nn", name="mla_uq", out_dtype=F32)
            q_rot = _rope(q_raw, rope_q, "mla_rope_q")
            kv = _matmul(ckr, w_kv, mode="nn", name="mla_ukv")
            o, lse, *rode = _attn_fwd(q_rot, 0, kv, 0, kv, 16, None, seq, "mla_attn", ride)
            s.update(down=down, gq=gq, gkv=gkv, cq=cq, ckr=ckr, q_rot=q_rot, kv=kv,
                     w_down=w_down, w_uq=w_uq, w_kv=w_kv)
        n_own = 3 if i == 0 else 2
        stacks[i], rode = stacks[i] + rode[:n_own], rode[n_own:]
        if i + 1 < depth:
            stacks[i + 1] = rode
        w_out = _stacked(stacks[i][1], 0, 128)
        xs, y = _matmul(o, w_out, mode="nn", name="attn_out", epi="resid_gate", extras=(xs, g_m), seq=seq)
        s.update(o=o, lse=lse, y=y, w_out=w_out, x_mid=xs)
        h2 = _norm_mod(xs, gain_mlp, sc_f, sh_f, seq, "norm_mlp")
        a_pre = _matmul(h2, stacks[i][2], mode="nn", name="mlp_up", layer=("col", 0, w1_rows), tm=tm_big)
        xs, y2 = _matmul(a_pre, stacks[i][3], mode="nn", name="mlp_down", layer=("row", 0, w2_rows), a_act="relu2",
                         epi="resid_gate", extras=(xs, g_f), seq=seq)
        s.update(h2=h2, a_pre=a_pre, y2=y2)
        saved.append(s)

    loss_part, dx, dg_final = _loss_head(xs, final_norm_g[None, :], loss_target.reshape(t, d), "loss_head")

    w1_cols, w1_tm = mlp_w1.shape[2], _pick(w1_rows, 1024)
    dg_mix, dg_mlp, db_f, dg_kv, dg_q = [None] * depth, [None] * depth, [None] * 2, [None] * 2, [None] * 2
    dmod = [None] * depth
    me = dev.astype(jnp.int32).reshape(1)
    chains = dict(fox_w_in=None, fox_w_out=None, mla_w_out=None, mlp_w1=None, mlp_w2=None)
    low_grads = [None] * 2

    def adam_step(name, layer):
        def land_one(buf, got):
            chains[name] = _total_adamw(buf, me, got, weights[name], args["m_" + name], args["v_" + name], layer,
                                        chains[name], "adamw_" + name)
        return land_one

    def low_step(lj):
        def land_one(buf, got):
            low_grads[lj] = _low_shard_grads(_add_parts(buf, me, got, "grads_total"))
        return land_one

    def mixer_steps(li):
        lj = li // 2
        return ([adam_step("fox_w_in", lj), adam_step("fox_w_out", lj)] if li % 2 == 0
                else [low_step(lj), adam_step("mla_w_out", lj)])

    def mlp_steps(li):
        return [adam_step("mlp_w1", li), adam_step("mlp_w2", li)]

    def land(staged, got):
        for (step, buf), g in zip(staged, got):
            step(buf, g)

    waiting = []
    dy2, dg_f = _gate_bwd(dx, saved[depth - 1]["y2"], mod[depth - 1, :, 5], seq, "gate_bwd")
    for i in reversed(range(depth)):
        j = i // 2
        s = saved[i]
        sh_m, sc_m, g_m, sh_f, sc_f, g_f = (mod[i, :, q] for q in range(6))
        da_pre = _matmul(dy2, stacks[i][3], mode="nt", name="mlp_down_dx", layer=("row", 0, w2_rows), epi="mul_drelu",
                         extras=(s["a_pre"],), tm=tm_big)
        g_w2 = _matmul(s["a_pre"], dy2, mode="tn", name="mlp_down_dw", a_act="relu2", tm=w2_rows,
                       into=(jax.ShapeDtypeStruct((N_DEV, w2_rows, d), BF), (None, w2_rows, d),
                             lambda r, j, k: (r, 0, 0)))
        dh2 = _matmul(da_pre, stacks[i][2], mode="nt", name="mlp_up_dx", layer=("col", 0, w1_rows), tm=tm_big)
        g_w1 = _matmul(s["h2"], da_pre, mode="tn", name="mlp_up_dw", tm=w1_tm, tn=w1_cols,
                       into=(jax.ShapeDtypeStruct((N_DEV, w1_rows, w1_cols), BF), (None, w1_tm, w1_cols),
                             lambda r, j, k: (j, r, 0)))
        waiting += list(zip(mlp_steps(i), [g_w1, g_w2]))
        dx, dg_mlp[i], dsc_f, dsh_f, dy, dg_m = _norm_mod_bwd(dh2, s["x_mid"], norm_mlp_g[i][None, :], sc_f, dx, seq,
                                                              "norm_bwd_gate", gate=(s["y"], g_m))
        do = _matmul(dy, s["w_out"], mode="nt", name="attn_out_dx")
        dw_out = _matmul(s["o"], dy, mode="tn", name="attn_out_dw", out_dtype=F32)
        if i == 0:
            waiting.append((adam_step("fox_w_out", 0), _by_dest_rows(dw_out)))
        ride = ("scatter", [e[1] for e in waiting]) if waiting else None
        g_mixer = dict(w_out=dw_out)
        if i % 2 == 0:
            qkv = s["qkv"]
            dq, dk, dv, dbias, *rode = _attn_bwd(qkv, 0, qkv, 8, qkv, 16, s["bias"], s["o32"], do, s["lse"], seq,
                                                 "fox_attn_bwd", ride, compact=True)
            dqkv = jnp.concatenate([dq, dk, dv], axis=1)
            d_cum = -jnp.transpose(dbias[:, :, 0, :], (0, 2, 1)).reshape(t, HEADS)
            d_cum = jnp.pad(d_cum, ((0, 0), (0, LANES - HEADS)))
            dfg, db = _fox_gate_bwd(d_cum, s["fg"], s["b_f"], seq, "fox_gate_bwd")
            db_f[j] = db
            dh = _matmul(dfg, s["w_f"], mode="nt", name="fox_gate_dx", out_dtype=F32)
            dh = _matmul(dqkv, s["w_qkv"], mode="nt", name="fox_qkv_dx", epi="add", extras=(dh,))
            dw_qkv = _matmul(s["h"], dqkv, mode="tn", name="fox_qkv_dw", out_dtype=F32)
            dw_f = _matmul(s["h"], dfg, mode="tn", name="fox_gate_dw", out_dtype=F32)
            g_mixer["fox_w_in"] = jnp.concatenate([dw_qkv[:, :d] * FOX_SCALE, dw_qkv[:, d:], dw_f[:, :HEADS]], axis=1)
        else:
            kv = s["kv"]
            dq, dk, dv, *rode = _attn_bwd(s["q_rot"], 0, kv, 0, kv, 16, None, s["o"], do, s["lse"], seq,
                                          "mla_attn_bwd", ride)
            dq_raw = _rope(dq, rope_q_t, "mla_rope_q_bwd")
            dcq = _matmul(dq_raw, s["w_uq"], mode="nt", name="mla_uq_dx")
            dw_uq = _matmul(s["cq"], dq_raw, mode="tn", name="mla_uq_dw", out_dtype=F32)
            dkv = jnp.concatenate([dk, dv], axis=1)
            dckr = _matmul(dkv, s["w_kv"], mode="nt", name="mla_ukv_dx")
            dw_kv = _matmul(s["ckr"], dkv, mode="tn", name="mla_ukv_dw", out_dtype=F32)
            d_down, dgq, dgkv = _mla_mid_bwd(s["down"], dcq, dckr, s["gq"], s["gkv"], rope_k_t, "mla_mid_bwd")
            dg_q[j], dg_kv[j] = dgq, dgkv
            dh = _matmul(d_down, s["w_down"], mode="nt", name="mla_down_dx")
            dw_down = _matmul(s["h"], d_down, mode="tn", name="mla_down_dw", out_dtype=F32)
            g_mixer["mla_w_dq"] = dw_down[:, :256]
            g_mixer["mla_w_dkv"] = jnp.concatenate([dw_down[:, 256:384], dw_down[:, 448:480]], axis=1)
            g_mixer["mla_w_uq"] = _unpad_heads(dw_uq, 96)
            dk_nope = dw_kv[:128, :HEADS * LANES].reshape(128, HEADS, LANES)[:, :, :HEAD_DIM]
            dv_w = dw_kv[:128, HEADS * LANES:].reshape(128, HEADS, HEAD_DIM)
            g_mixer["mla_w_ukv"] = jnp.concatenate([dk_nope, dv_w], axis=2).reshape(128, HEADS * LANES)
        land(waiting, rode)
        this_dg_f = dg_f
        if i > 0:
            dx, dg_mix[i], dsc_m, dsh_m, dy2, dg_f = _norm_mod_bwd(
                dh, s["x_in"], norm_mix_g[i][None, :], sc_m, dx, seq, "norm_bwd_gate",
                gate=(saved[i - 1]["y2"], mod[i - 1, :, 5]))
            waiting = list(zip(mixer_steps(i), _layer_mixer_grad_bufs(i, g_mixer)))
        else:
            dx, dg_mix[i], dsc_m, dsh_m = _norm_mod_bwd(dh, s["x_in"], norm_mix_g[i][None, :], sc_m, dx, seq, "norm_bwd")
            last = [(adam_step("fox_w_in", 0), _by_dest_cols(g_mixer["fox_w_in"]))]
        dmod[i] = jnp.stack([dsh_m, dsc_m, dg_m, dsh_f, dsc_f, this_dg_f], axis=1).reshape(nb, 6 * d)

    grad_x = dx.reshape(nb, seq, d)
    shard_grads = {n: jnp.stack([low_grads[0][n], low_grads[1][n]]) for n in low_grads[0]}

    dmod_arr = jnp.stack(dmod)
    wide = lambda a: jnp.pad(a, ((0, 0), (0, d - a.shape[1])))
    pieces = [wide(loss_part), *dg_mix, *dg_mlp, *[wide(a) for a in db_f], *[wide(a) for a in dg_kv], dg_final,
              *[wide(a) for a in dg_q], jnp.sum(dmod_arr, axis=1).reshape(depth * 6, d)]
    n_small = sum(p.shape[0] for p in pieces)
    both = jnp.concatenate(pieces + [dmod_arr.reshape(depth * nb * 6, d)], axis=0)
    both = jnp.pad(both, ((0, (-both.shape[0]) % 8), (0, 0)))
    scattered, (both_all,) = _scatter_and_gather([e[1] for e in last], [both], "last_exchange")
    land(last, scattered)
    done = {n: tuple(a.reshape(weights[n].shape) for a in chain) for n, chain in chains.items()}
    total = _sum_leading(both_all, "sum_small")
    off = 0

    def take(rows):
        nonlocal off
        out = total[off:off + rows]
        off += rows
        return out

    loss = take(1)[0, 0]
    g_small = dict(
        norm_mix_g=take(depth), norm_mlp_g=take(depth), fox_b_f=take(2)[:, :HEADS], mla_kv_norm_g=take(2)[:, :128],
        final_norm_g=take(1)[0],
        mla_q_norm_g=lax.dynamic_slice_in_dim(take(2)[:, :N_DEV * n_qg], dev * n_qg, n_qg, axis=1),
        ada_b=take(depth * 6).reshape(depth, 6 * d))
    dmod_all = both_all[:, n_small:n_small + depth * nb * 6]
    dmod_all = jnp.transpose(dmod_all.reshape(N_DEV, depth, nb, 6 * d), (1, 0, 2, 3)).reshape(depth, N_DEV * nb, 6 * d)
    dmod_cols = lax.dynamic_slice_in_dim(dmod_all, dev * n_mod_local, n_mod_local, axis=2)
    g_ada_w = jnp.stack([_matmul(c_all, dmod_cols[i], mode="tn", name="ada_dw", out_dtype=F32, a_act="silu")
                         for i in range(depth)])

    all_grads = dict(shard_grads)
    all_grads.update(g_small)
    all_grads["ada_w"] = g_ada_w

    deltas, new_m, new_v = {}, {}, {}
    for n in WEIGHTS:
        if n in done:
            all_grads[n], deltas[n], new_m[n], new_v[n] = done[n]
        else:
            deltas[n], new_m[n], new_v[n] = _adamw(weights[n], all_grads[n], args["m_" + n], args["v_" + n], "adamw")

    return (loss, grad_x, *[all_grads[n] for n in WEIGHTS], *[deltas[n] for n in WEIGHTS],
            *[new_m[n] for n in WEIGHTS], *[new_v[n] for n in WEIGHTS])
```

```python
import functools
import math

import jax
import jax.numpy as jnp
import numpy as np
from jax import lax
from jax.experimental import pallas as pl
from jax.experimental.pallas import tpu as pltpu

F32 = jnp.float32
BF = jnp.bfloat16

N_DEV = 8
HEADS = 16
HEAD_PAIRS = HEADS // 2
HEAD_DIM = 64
LANES = 128
ROPE_HALF = 16
NORM_EPS = 1e-6
MLA_SCALE = 96.0 ** -0.5
FOX_SCALE = 0.125
ATTN_BLOCK = 512
ROW_BLOCK = 1024
K_SPAN = 4
VMEM_LIMIT = 56 * 1024 * 1024
MESH = pl.DeviceIdType.MESH

ADAM_LR = 0.001
ADAM_B1 = 0.9
ADAM_B2 = 0.999
ADAM_EPS = 1e-08
ADAM_WD = 0.01
ADAM_STEP = 10

WEIGHTS = ("ada_w", "ada_b", "norm_mix_g", "norm_mlp_g", "fox_w_in", "fox_b_f", "fox_w_out", "mla_w_dq",
           "mla_q_norm_g", "mla_w_uq", "mla_w_dkv", "mla_kv_norm_g", "mla_w_ukv", "mla_w_out", "mlp_w1",
           "mlp_w2", "final_norm_g")


def _params(sem=None):
    return pltpu.CompilerParams(dimension_semantics=sem, vmem_limit_bytes=VMEM_LIMIT)


def _pick(n, target):
    if n <= target:
        return n
    for t in range(target, 127, -128):
        if n % t == 0:
            return t
    return n


def _rows(n, target=512):
    if n <= target:
        return n
    for t in range(target, 7, -8):
        if n % t == 0:
            return t
    return n


def _place():
    x, y, c = lax.axis_index("x"), lax.axis_index("y"), lax.axis_index("c")
    return x, y, c


def _adamw_math(w, g, m, v):
    nm = ADAM_B1 * m + (1.0 - ADAM_B1) * g
    nv = ADAM_B2 * v + (1.0 - ADAM_B2) * (g * g)
    m_hat = nm * (1.0 / (1.0 - ADAM_B1 ** ADAM_STEP))
    v_hat = nv * (1.0 / (1.0 - ADAM_B2 ** ADAM_STEP))
    return -ADAM_LR * (m_hat / (jnp.sqrt(v_hat) + ADAM_EPS) + ADAM_WD * w), nm, nv


def _all_gather(blocks, name):
    ride = ("gather", blocks)

    def body(*refs):
        start, mid, finish = _ride_phases(ride, *_ride_split(ride, refs, 0, 0)[:3])
        start()
        mid()
        finish()

    in_specs, out_shape, out_specs, scratch = _ride_specs(ride)
    return pl.pallas_call(
        body, name=name, out_shape=out_shape, in_specs=in_specs, out_specs=out_specs, scratch_shapes=scratch,
    )(*blocks)


def _ride_specs(ride):
    kind, arrays = ride
    n = len(arrays)
    any_spec = pl.BlockSpec(memory_space=pl.ANY)
    if kind == "gather":
        out_shape = [jax.ShapeDtypeStruct((N_DEV,) + b.shape, b.dtype) for b in arrays]
        scratch = [pltpu.SemaphoreType.DMA((7 * n,)), pltpu.SemaphoreType.DMA((7 * n,)), pltpu.SemaphoreType.DMA((n,))]
    else:
        out_shape = [jax.ShapeDtypeStruct((N_DEV - 1,) + p.shape[1:], p.dtype) for p in arrays]
        scratch = [pltpu.SemaphoreType.DMA((7 * n,)), pltpu.SemaphoreType.DMA((7 * n,))]
    return [any_spec] * n, out_shape, [any_spec] * n, scratch


def _ride_split(ride, refs, n_in, n_out):
    n = len(ride[1])
    n_sem = 3 if ride[0] == "gather" else 2
    src = refs[n_in:n_in + n]
    dst = refs[n_in + n + n_out:n_in + 2 * n + n_out]
    own = refs[:n_in] + refs[n_in + n:n_in + n + n_out] + refs[n_in + 2 * n + n_out:len(refs) - n_sem]
    return src, dst, refs[len(refs) - n_sem:], own


def _ride_phases(ride, src, dst, sems):
    n = len(src)
    x, y, c = _place()
    chips = [(1 - x, y), (x, 1 - y), (1 - x, 1 - y)]
    if ride[0] == "scatter":
        send_sems, recv_sems = sems

        def copies():
            out = []
            for f in (1, 2, 3, 5, 6, 7, 4):
                px, py, pc = x ^ (f & 1), y ^ ((f >> 1) & 1), c ^ (f >> 2)
                out += [pltpu.make_async_remote_copy(
                    src_ref=src[a].at[4 * px + 2 * py + pc], dst_ref=dst[a].at[f - 1],
                    send_sem=send_sems.at[7 * a + f - 1], recv_sem=recv_sems.at[7 * a + f - 1],
                    device_id=(px, py, pc), device_id_type=MESH) for a in range(n)]
            return out

        def start():
            for cp in copies():
                cp.start()

        def finish():
            for cp in copies():
                cp.wait()

        return start, lambda: None, finish

    send_sems, recv_sems, local_sems = sems
    me, sibling = (x, y, c), (x, y, 1 - c)

    def slot(a, px, py, pc):
        return dst[a].at[4 * px + 2 * py + pc]

    def copy(a, k, blk, to, from_src=False):
        return pltpu.make_async_remote_copy(
            src_ref=src[a] if from_src else slot(a, *blk), dst_ref=slot(a, *blk),
            send_sem=send_sems.at[7 * a + k], recv_sem=recv_sems.at[7 * a + k], device_id=to, device_id_type=MESH)

    def mine():
        return [pltpu.make_async_copy(src[a], slot(a, *me), local_sems.at[a]) for a in range(n)]

    def first():
        out = []
        for j, chip in enumerate(chips):
            out += [copy(a, 1 + j, me, (*chip, c), from_src=True) for a in range(n)]
        return out + [copy(a, 0, me, sibling, from_src=True) for a in range(n)]

    def passed():
        return [copy(a, 4 + j, (*chip, c), sibling) for j, chip in enumerate(chips) for a in range(n)]

    def start():
        for cp in mine() + first():
            cp.start()

    def mid():
        for j, chip in enumerate(chips):
            for a in range(n):
                copy(a, 1 + j, (*chip, c), me).wait_recv()
        for cp in passed():
            cp.start()

    def finish():
        for a in range(n):
            copy(a, 0, sibling, me).wait_recv()
        for j, chip in enumerate(chips):
            for a in range(n):
                copy(a, 4 + j, (*chip, 1 - c), me).wait_recv()
        for cp in first() + passed():
            cp.wait_send()
        for cp in mine():
            cp.wait()

    return start, mid, finish


def _ride_steps(ride, refs, n_in, n_out, step, n_steps):
    if ride is None:
        return refs, lambda: None
    src, dst, sems, own = _ride_split(ride, refs, n_in, n_out)
    start, mid, finish = _ride_phases(ride, src, dst, sems)
    pl.when(step == 0)(start)
    pl.when(step == (3 * n_steps) // 4)(mid)
    return own, lambda: pl.when(step == n_steps - 1)(finish)


def _scatter_and_gather(bufs, blocks, name):
    rides = (("scatter", bufs), ("gather", blocks))
    ns, ng = len(bufs), len(blocks)

    def body(*refs):
        s_src, g_src = refs[:ns], refs[ns:ns + ng]
        s_dst, g_dst = refs[ns + ng:2 * ns + ng], refs[2 * ns + ng:2 * (ns + ng)]
        sems = refs[2 * (ns + ng):]
        s_start, _, s_finish = _ride_phases(rides[0], s_src, s_dst, sems[:2])
        g_start, g_mid, g_finish = _ride_phases(rides[1], g_src, g_dst, sems[2:])
        s_start()
        g_start()
        g_mid()
        g_finish()
        s_finish()

    s_in, s_shape, s_out, s_scratch = _ride_specs(rides[0])
    g_in, g_shape, g_out, g_scratch = _ride_specs(rides[1])
    outs = pl.pallas_call(
        body, name=name, out_shape=s_shape + g_shape, in_specs=s_in + g_in, out_specs=s_out + g_out,
        scratch_shapes=s_scratch + g_scratch,
    )(*bufs, *blocks)
    return outs[:ns], outs[ns:]


def _total(own_ref, parts_ref):
    g = own_ref[...].astype(F32)
    for k in range(parts_ref.shape[0]):
        g = g + parts_ref[k].astype(F32)
    return g


def _total_adamw(buf, me, parts, w, m, v, layer, carry, name):
    _, r, cdim = buf.shape
    n_layers = w.shape[0]
    tr = _rows(r, 512)
    steps = r // tr

    def body(me_ref, own_ref, parts_ref, w_ref, m_ref, v_ref, *rest):
        del me_ref
        g_ref, d_ref, nm_ref, nv_ref = rest[-4:]
        g = _total(own_ref, parts_ref)
        g_ref[...] = g
        d_ref[...], nm_ref[...], nv_ref[...] = _adamw_math(w_ref[...], g, m_ref[...], v_ref[...])

    lay = pl.BlockSpec((tr, cdim), lambda i, me_ref: (layer * steps + i, 0))
    in_specs = [pl.BlockSpec((None, tr, cdim), lambda i, me_ref: (me_ref[0], i, 0)),
                pl.BlockSpec((N_DEV - 1, tr, cdim), lambda i, me_ref: (0, i, 0)), lay, lay, lay]
    operands = [me, buf, parts, *[a.reshape(n_layers * r, cdim) for a in (w, m, v)]]
    aliases = {}
    if carry is not None:
        in_specs += [pl.BlockSpec(memory_space=pl.ANY)] * 4
        operands += list(carry)
        aliases = {6 + k: k for k in range(4)}
    sds = jax.ShapeDtypeStruct((n_layers * r, cdim), F32)
    return pl.pallas_call(
        body, name=name,
        grid_spec=pltpu.PrefetchScalarGridSpec(num_scalar_prefetch=1, grid=(steps,), in_specs=in_specs,
                                               out_specs=(lay,) * 4),
        out_shape=(sds,) * 4, input_output_aliases=aliases,
        compiler_params=_params(("parallel",)),
    )(*operands)


def _add_parts(buf, me, parts, name):
    _, r, cdim = buf.shape
    tr = _rows(r, 512)

    def body(me_ref, own_ref, parts_ref, out_ref):
        del me_ref
        out_ref[...] = _total(own_ref, parts_ref)

    return pl.pallas_call(
        body, name=name,
        grid_spec=pltpu.PrefetchScalarGridSpec(
            num_scalar_prefetch=1, grid=(r // tr,),
            in_specs=[pl.BlockSpec((None, tr, cdim), lambda i, me_ref: (me_ref[0], i, 0)),
                      pl.BlockSpec((N_DEV - 1, tr, cdim), lambda i, me_ref: (0, i, 0))],
            out_specs=pl.BlockSpec((tr, cdim), lambda i, me_ref: (i, 0))),
        out_shape=jax.ShapeDtypeStruct((r, cdim), F32),
        compiler_params=_params(("parallel",)),
    )(me, buf, parts)


def _sum_leading(stack, name):
    n, r, cdim = stack.shape
    tr = _rows(r, 512)

    def body(in_ref, out_ref):
        acc = in_ref[0]
        for k in range(1, n):
            acc = acc + in_ref[k]
        out_ref[...] = acc

    return pl.pallas_call(
        body, name=name, grid=(r // tr,),
        out_shape=jax.ShapeDtypeStruct((r, cdim), F32),
        in_specs=[pl.BlockSpec((n, tr, cdim), lambda i: (0, i, 0))],
        out_specs=pl.BlockSpec((tr, cdim), lambda i: (i, 0)),
        compiler_params=_params(("parallel",)),
    )(stack)


_DIMS = {"nn": (((1,), (0,)), ((), ())), "nt": (((1,), (1,)), ((), ())), "tn": (((0,), (0,)), ((), ()))}


def _stack_spec(shape, mode, layer):
    cut, l, rows = layer
    cols = shape[2]
    by_n = pl.BlockSpec((1, rows, cols), lambda i, j, k: (j, l, 0))
    by_k = pl.BlockSpec((K_SPAN, rows, cols), lambda i, j, k: (k, l, 0))
    if cut == "col":
        return (by_n, N_DEV * cols, cols, rows) if mode == "nn" else (by_k, rows, rows, K_SPAN * cols)
    return (by_k, cols, cols, K_SPAN * rows) if mode == "nn" else (by_n, N_DEV * rows, rows, cols)


def _matmul(a, b, *, mode, name, out_dtype=BF, a_act=None, epi=None, extras=(), seq=None, layer=None, tm=None,
            tn=None, into=None):
    if mode == "tn":
        kdim, m = a.shape
    else:
        m, kdim = a.shape
    if tm is None:
        tm = _pick(m, 1024 if epi != "resid_gate" else min(1024, seq))
    tk = _pick(kdim, 4096 if mode == "tn" else 1024)
    b_spec = None
    if layer is not None:
        b_spec, n, tn, tk = _stack_spec(b.shape, mode, layer)
    else:
        n = b.shape[0] if mode == "nt" else b.shape[1]
        tn = _pick(n, 1024) if tn is None else tn
    nk = kdim // tk
    a_spec = (pl.BlockSpec((tk, tm), lambda i, j, k: (k, i)) if mode == "tn"
              else pl.BlockSpec((tm, tk), lambda i, j, k: (i, k)))
    if b_spec is None:
        b_spec = (pl.BlockSpec((tn, tk), lambda i, j, k: (j, k)) if mode == "nt"
                  else pl.BlockSpec((tk, tn), lambda i, j, k: (k, j)))
    tile = pl.BlockSpec((tm, tn), lambda i, j, k: (i, j))
    in_specs, out_specs = [a_spec, b_spec], [tile]
    out_shape = [jax.ShapeDtypeStruct((m, n), out_dtype)]
    if epi == "resid_gate":
        in_specs += [tile, pl.BlockSpec((None, 1, tn), lambda i, j, k: ((i * tm) // seq, 0, j))]
        out_shape = [jax.ShapeDtypeStruct((m, n), F32), jax.ShapeDtypeStruct((m, n), BF)]
        out_specs = [tile, tile]
    elif epi in ("mul_drelu", "add"):
        in_specs += [tile]
    elif epi == "bias":
        in_specs += [pl.BlockSpec((1, tn), lambda i, j, k: (0, j))]
    n_extra, n_out = len(in_specs) - 2, len(out_specs)
    aliases, n_kept = {}, 0
    if into is not None:
        buffer, block, index_map = into
        out_dtype = buffer.dtype
        if not isinstance(buffer, jax.ShapeDtypeStruct):
            in_specs.append(pl.BlockSpec(memory_space=pl.ANY))
            extras = tuple(extras) + (buffer,)
            aliases, n_kept = {len(in_specs) - 1: 0}, 1
        out_shape = [jax.ShapeDtypeStruct(buffer.shape, buffer.dtype)]
        out_specs = [pl.BlockSpec(block, index_map)]
    dims = _DIMS[mode]

    def body(*refs):
        a_ref, b_ref = refs[:2]
        ex = refs[2:2 + n_extra]
        outs = refs[2 + n_extra + n_kept:2 + n_extra + n_kept + n_out]
        av = a_ref[...]
        if a_act == "relu2":
            t = jnp.maximum(av.astype(F32), 0.0)
            av = t * t
        elif a_act == "silu":
            t = av.astype(F32)
            av = t / (1.0 + jnp.exp(-t))
        av = av.astype(BF)
        if layer is None:
            part = lax.dot_general(av, b_ref[...].astype(BF), dims, preferred_element_type=F32)
        else:
            span = b_ref.shape[0]
            wk = av.shape[1] // span
            part = None
            for u in range(span):
                p_u = lax.dot_general(av[:, u * wk:(u + 1) * wk], b_ref[u], dims, preferred_element_type=F32)
                part = p_u if part is None else part + p_u

        def finish(acc):
            if epi == "resid_gate":
                outs[0][...] = ex[0][...] + ex[1][...] * acc
                outs[1][...] = acc.astype(BF)
            elif epi == "mul_drelu":
                outs[0][...] = (acc * (2.0 * jnp.maximum(ex[0][...].astype(F32), 0.0))).astype(out_dtype)
            elif epi == "add":
                outs[0][...] = (acc + ex[0][...].astype(F32)).astype(out_dtype)
            elif epi == "bias":
                outs[0][...] = (acc + ex[0][...]).astype(out_dtype)
            else:
                outs[0][...] = acc.astype(out_dtype)

        if nk == 1:
            finish(part)
        else:
            acc_ref = refs[-1]
            k = pl.program_id(2)

            @pl.when(k == 0)
            def _():
                acc_ref[...] = part

            @pl.when(k > 0)
            def _():
                acc_ref[...] += part

            @pl.when(k == nk - 1)
            def _():
                finish(acc_ref[...])

    res = pl.pallas_call(
        body, name=name, grid=(m // tm, n // tn, nk),
        out_shape=out_shape, in_specs=in_specs, out_specs=out_specs,
        scratch_shapes=[pltpu.VMEM((tm, tn), F32)] if nk > 1 else [],
        input_output_aliases=aliases,
        compiler_params=_params(("parallel", "parallel", "arbitrary")),
    )(a, b, *extras)
    return res if n_out > 1 else res[0]


def _norm_mod(x, gain, scale, shift, seq, name):
    t, w = x.shape
    tr = ROW_BLOCK

    def body(x_ref, g_ref, sc_ref, sh_ref, out_ref):
        xv = x_ref[...]
        rstd = lax.rsqrt(jnp.mean(xv * xv, axis=-1, keepdims=True) + NORM_EPS)
        y = xv * rstd * g_ref[...]
        out_ref[...] = (y * (1.0 + sc_ref[...]) + sh_ref[...]).astype(BF)

    per_b = pl.BlockSpec((None, 1, w), lambda i: ((i * tr) // seq, 0, 0))
    return pl.pallas_call(
        body, name=name, grid=(t // tr,),
        out_shape=jax.ShapeDtypeStruct((t, w), BF),
        in_specs=[pl.BlockSpec((tr, w), lambda i: (i, 0)), pl.BlockSpec((1, w), lambda i: (0, 0)), per_b, per_b],
        out_specs=pl.BlockSpec((tr, w), lambda i: (i, 0)),
        compiler_params=_params(("parallel",)),
    )(x, gain, scale, shift)


def _norm_mod_bwd(dh, x, gain, scale, dres, seq, name, gate=None):
    t, w = x.shape
    tr = ROW_BLOCK
    steps_per_seq = seq // tr
    nb = t // seq
    gated = gate is not None

    def body(*refs):
        dh_ref, x_ref, g_ref, sc_ref, dres_ref = refs[:5]
        dx_ref, dg_ref, dsc_ref, dsh_ref = refs[-6:-2] if gated else refs[-4:]
        i = pl.program_id(0)
        xv = x_ref[...]
        dhv = dh_ref[...].astype(F32)
        rstd = lax.rsqrt(jnp.mean(xv * xv, axis=-1, keepdims=True) + NORM_EPS)
        xhat = xv * rstd
        one_sc = 1.0 + sc_ref[...]
        g = g_ref[...]
        dxhat = dhv * (g * one_sc)
        proj = jnp.mean(dxhat * xhat, axis=-1, keepdims=True)
        dxv = dres_ref[...] + rstd * (dxhat - xhat * proj)
        dx_ref[...] = dxv
        dhx = dhv * xhat
        first = [(dg_ref, jnp.sum(dhx * one_sc, axis=0, keepdims=True))]
        per_seq = [(dsc_ref, jnp.sum(dhx * g, axis=0, keepdims=True)), (dsh_ref, jnp.sum(dhv, axis=0, keepdims=True))]
        if gated:
            y_ref, gate_ref, dy_ref, dgate_ref = refs[5], refs[6], refs[-2], refs[-1]
            dy_ref[...] = (dxv * gate_ref[...]).astype(BF)
            per_seq.append((dgate_ref, jnp.sum(dxv * y_ref[...].astype(F32), axis=0, keepdims=True)))
        for cond_new, cond_add, group in ((i == 0, i > 0, first),
                                          (i % steps_per_seq == 0, i % steps_per_seq != 0, per_seq)):
            @pl.when(cond_new)
            def _(group=group):
                for ref, part in group:
                    ref[...] = part

            @pl.when(cond_add)
            def _(group=group):
                for ref, part in group:
                    ref[...] += part

    row = pl.BlockSpec((tr, w), lambda i: (i, 0))
    per_b = pl.BlockSpec((None, 1, w), lambda i: ((i * tr) // seq, 0, 0))
    vec = pl.BlockSpec((1, w), lambda i: (0, 0))
    out_shape = [jax.ShapeDtypeStruct((t, w), F32), jax.ShapeDtypeStruct((1, w), F32),
                 jax.ShapeDtypeStruct((nb, 1, w), F32), jax.ShapeDtypeStruct((nb, 1, w), F32)]
    in_specs, out_specs, operands = [row, row, vec, per_b, row], [row, vec, per_b, per_b], [dh, x, gain, scale, dres]
    if gated:
        in_specs += [row, per_b]
        operands += list(gate)
        out_shape += [jax.ShapeDtypeStruct((t, w), BF), jax.ShapeDtypeStruct((nb, 1, w), F32)]
        out_specs += [row, per_b]
    return pl.pallas_call(
        body, name=name, grid=(t // tr,),
        out_shape=out_shape, in_specs=in_specs, out_specs=out_specs,
        compiler_params=_params(("arbitrary",)),
    )(*operands)


def _gate_bwd(dx, y, gate, seq, name):
    t, w = dx.shape
    tr = ROW_BLOCK
    steps_per_seq = seq // tr
    nb = t // seq

    def body(dx_ref, y_ref, g_ref, dy_ref, dg_ref):
        i = pl.program_id(0)
        dxv = dx_ref[...]
        dy_ref[...] = (dxv * g_ref[...]).astype(BF)
        part = jnp.sum(dxv * y_ref[...].astype(F32), axis=0, keepdims=True)

        @pl.when(i % steps_per_seq == 0)
        def _():
            dg_ref[...] = part

        @pl.when(i % steps_per_seq != 0)
        def _():
            dg_ref[...] += part

    row = pl.BlockSpec((tr, w), lambda i: (i, 0))
    per_b = pl.BlockSpec((None, 1, w), lambda i: ((i * tr) // seq, 0, 0))
    return pl.pallas_call(
        body, name=name, grid=(t // tr,),
        out_shape=(jax.ShapeDtypeStruct((t, w), BF), jax.ShapeDtypeStruct((nb, 1, w), F32)),
        in_specs=[row, row, per_b], out_specs=(row, per_b),
        compiler_params=_params(("arbitrary",)),
    )(dx, y, gate)


def _loss_head(x, gain, target, name):
    t, w = x.shape
    tr = ROW_BLOCK

    def body(x_ref, g_ref, t_ref, loss_ref, dx_ref, dg_ref):
        i = pl.program_id(0)
        xv = x_ref[...]
        g = g_ref[...]
        rstd = lax.rsqrt(jnp.mean(xv * xv, axis=-1, keepdims=True) + NORM_EPS)
        xhat = xv * rstd
        err = xhat * g - t_ref[...]
        row_loss = jnp.sum(err * err, axis=-1, keepdims=True) * (0.5 / w)
        loss_part = jnp.broadcast_to(jnp.sum(row_loss, axis=0, keepdims=True), (1, LANES))
        dy = err * (1.0 / w)
        dg_part = jnp.sum(dy * xhat, axis=0, keepdims=True)
        dxhat = dy * g
        proj = jnp.mean(dxhat * xhat, axis=-1, keepdims=True)
        dx_ref[...] = rstd * (dxhat - xhat * proj)

        @pl.when(i == 0)
        def _():
            loss_ref[...] = loss_part
            dg_ref[...] = dg_part

        @pl.when(i > 0)
        def _():
            loss_ref[...] += loss_part
            dg_ref[...] += dg_part

    row = pl.BlockSpec((tr, w), lambda i: (i, 0))
    vec = pl.BlockSpec((1, w), lambda i: (0, 0))
    return pl.pallas_call(
        body, name=name, grid=(t // tr,),
        out_shape=(jax.ShapeDtypeStruct((1, LANES), F32), jax.ShapeDtypeStruct((t, w), F32),
                   jax.ShapeDtypeStruct((1, w), F32)),
        in_specs=[row, vec, row],
        out_specs=(pl.BlockSpec((1, LANES), lambda i: (0, 0)), row, vec),
        compiler_params=_params(("arbitrary",)),
    )(x, gain, target)


def _rope_group(xg, cos_p, sin_a, sin_b):
    return (xg * cos_p + pltpu.roll(xg, LANES - ROPE_HALF, axis=1) * sin_a
            + pltpu.roll(xg, ROPE_HALF, axis=1) * sin_b)


def _rope(x, tables, name, out_dtype=BF):
    t, w = x.shape
    tr = ROW_BLOCK
    groups = w // LANES

    def body(x_ref, c_ref, a_ref, b_ref, out_ref):
        cos_p, sin_a, sin_b = c_ref[...], a_ref[...], b_ref[...]
        for g in range(groups):
            sl = slice(g * LANES, (g + 1) * LANES)
            out_ref[:, sl] = _rope_group(x_ref[:, sl].astype(F32), cos_p, sin_a, sin_b).astype(out_dtype)

    row = pl.BlockSpec((tr, w), lambda i: (i, 0))
    tab = pl.BlockSpec((tr, LANES), lambda i: (i, 0))
    return pl.pallas_call(
        body, name=name, grid=(t // tr,),
        out_shape=jax.ShapeDtypeStruct((t, w), out_dtype),
        in_specs=[row, tab, tab, tab], out_specs=row,
        compiler_params=_params(("parallel",)),
    )(x, *tables)


def _mla_mid(down, gq, gkv, tables, name):
    t = down.shape[0]
    tr = ROW_BLOCK

    def body(d_ref, gq_ref, gkv_ref, c_ref, a_ref, b_ref, cq_ref, ckr_ref):
        q = d_ref[:, 0:256]
        cq_ref[...] = (q * lax.rsqrt(jnp.mean(q * q, axis=-1, keepdims=True) + NORM_EPS) * gq_ref[...]).astype(BF)
        kv = d_ref[:, 256:384]
        ckr_ref[:, 0:128] = (kv * lax.rsqrt(jnp.mean(kv * kv, axis=-1, keepdims=True) + NORM_EPS)
                             * gkv_ref[...]).astype(BF)
        ckr_ref[:, 128:256] = _rope_group(d_ref[:, 384:512], c_ref[...], a_ref[...], b_ref[...]).astype(BF)

    tab = pl.BlockSpec((tr, LANES), lambda i: (i, 0))
    return pl.pallas_call(
        body, name=name, grid=(t // tr,),
        out_shape=(jax.ShapeDtypeStruct((t, 256), BF), jax.ShapeDtypeStruct((t, 256), BF)),
        in_specs=[pl.BlockSpec((tr, 512), lambda i: (i, 0)), pl.BlockSpec((1, 256), lambda i: (0, 0)),
                  pl.BlockSpec((1, 128), lambda i: (0, 0)), tab, tab, tab],
        out_specs=(pl.BlockSpec((tr, 256), lambda i: (i, 0)), pl.BlockSpec((tr, 256), lambda i: (i, 0))),
        compiler_params=_params(("parallel",)),
    )(down, gq, gkv, *tables)


def _mla_mid_bwd(down, dcq, dckr, gq, gkv, tables_t, name):
    t = down.shape[0]
    tr = ROW_BLOCK

    def norm_bwd(xv, g, dy):
        rstd = lax.rsqrt(jnp.mean(xv * xv, axis=-1, keepdims=True) + NORM_EPS)
        xhat = xv * rstd
        dxhat = dy * g
        proj = jnp.mean(dxhat * xhat, axis=-1, keepdims=True)
        return rstd * (dxhat - xhat * proj), jnp.sum(dy * xhat, axis=0, keepdims=True)

    def body(d_ref, dcq_ref, dckr_ref, gq_ref, gkv_ref, c_ref, a_ref, b_ref, dd_ref, dgq_ref, dgkv_ref):
        i = pl.program_id(0)
        dq, dgq_part = norm_bwd(d_ref[:, 0:256], gq_ref[...], dcq_ref[...].astype(F32))
        dd_ref[:, 0:256] = dq.astype(BF)
        dkv, dgkv_part = norm_bwd(d_ref[:, 256:384], gkv_ref[...], dckr_ref[:, 0:128].astype(F32))
        dd_ref[:, 256:384] = dkv.astype(BF)
        dd_ref[:, 384:512] = _rope_group(dckr_ref[:, 128:256].astype(F32), c_ref[...], a_ref[...],
                                         b_ref[...]).astype(BF)

        @pl.when(i == 0)
        def _():
            dgq_ref[...] = dgq_part
            dgkv_ref[...] = dgkv_part

        @pl.when(i > 0)
        def _():
            dgq_ref[...] += dgq_part
            dgkv_ref[...] += dgkv_part

    tab = pl.BlockSpec((tr, LANES), lambda i: (i, 0))
    r256 = pl.BlockSpec((tr, 256), lambda i: (i, 0))
    return pl.pallas_call(
        body, name=name, grid=(t // tr,),
        out_shape=(jax.ShapeDtypeStruct((t, 512), BF), jax.ShapeDtypeStruct((1, 256), F32),
                   jax.ShapeDtypeStruct((1, 128), F32)),
        in_specs=[pl.BlockSpec((tr, 512), lambda i: (i, 0)), r256, r256, pl.BlockSpec((1, 256), lambda i: (0, 0)),
                  pl.BlockSpec((1, 128), lambda i: (0, 0)), tab, tab, tab],
        out_specs=(pl.BlockSpec((tr, 512), lambda i: (i, 0)), pl.BlockSpec((1, 256), lambda i: (0, 0)),
                   pl.BlockSpec((1, 128), lambda i: (0, 0))),
        compiler_params=_params(("arbitrary",)),
    )(down, dcq, dckr, gq, gkv, *tables_t)


def _scan_rows(x, reverse):
    s = x.shape[0]
    row = lax.broadcasted_iota(jnp.int32, x.shape, 0)
    step = 1
    while step < s:
        if reverse:
            x = x + jnp.where(row < s - step, pltpu.roll(x, s - step, axis=0), 0.0)
        else:
            x = x + jnp.where(row >= step, pltpu.roll(x, step, axis=0), 0.0)
        step *= 2
    return x


def _fox_gate(fg, b_f, seq, name):
    t = fg.shape[0]

    def body(fg_ref, b_ref, out_ref):
        z = fg_ref[...] + b_ref[...]
        log_f = jnp.minimum(z, 0.0) - jnp.log(1.0 + jnp.exp(-jnp.abs(z)))
        out_ref[...] = _scan_rows(log_f, reverse=False)

    blk = pl.BlockSpec((seq, LANES), lambda b: (b, 0))
    return pl.pallas_call(
        body, name=name, grid=(t // seq,),
        out_shape=jax.ShapeDtypeStruct((t, LANES), F32),
        in_specs=[blk, pl.BlockSpec((1, LANES), lambda b: (0, 0))], out_specs=blk,
        compiler_params=_params(("parallel",)),
    )(fg, b_f)


def _fox_gate_bwd(d_cum, fg, b_f, seq, name):
    t = fg.shape[0]

    def body(dc_ref, fg_ref, b_ref, dfg_ref, db_ref):
        b = pl.program_id(0)
        z = fg_ref[...] + b_ref[...]
        d_log_f = _scan_rows(dc_ref[...], reverse=True)
        dz = d_log_f / (1.0 + jnp.exp(z))
        dfg_ref[...] = dz
        part = jnp.sum(dz, axis=0, keepdims=True)

        @pl.when(b == 0)
        def _():
            db_ref[...] = part

        @pl.when(b > 0)
        def _():
            db_ref[...] += part

    blk = pl.BlockSpec((seq, LANES), lambda b: (b, 0))
    vec = pl.BlockSpec((1, LANES), lambda b: (0, 0))
    return pl.pallas_call(
        body, name=name, grid=(t // seq,),
        out_shape=(jax.ShapeDtypeStruct((t, LANES), F32), jax.ShapeDtypeStruct((1, LANES), F32)),
        in_specs=[blk, blk, vec], out_specs=(blk, vec),
        compiler_params=_params(("arbitrary",)),
    )(d_cum, fg, b_f)


def _head_masks():
    lane = lax.broadcasted_iota(jnp.int32, (1, LANES), 1)
    return lane < HEAD_DIM, lane >= HEAD_DIM


def _pair_operands(ref, r0, n, compact, masks, masked):
    if not compact:
        return [ref[pl.ds(r0, n), h * LANES:(h + 1) * LANES] for h in range(2)]
    pair = ref[pl.ds(r0, n), :]
    return [jnp.where(mk, pair, jnp.zeros_like(pair)) for mk in masks] if masked else [pair, pair]


def _causal(n_rows, n_cols, shift):
    return (lax.broadcasted_iota(jnp.int32, (n_rows, n_cols), 1)
            <= lax.broadcasted_iota(jnp.int32, (n_rows, n_cols), 0) + shift)


def _attn_fwd(q_arr, q_off, k_arr, k_off, v_arr, v_off, bias, seq, name, ride=None, compact=False):
    t = q_arr.shape[0]
    nb = t // seq
    blk = min(ATTN_BLOCK, seq)
    nq = seq // blk
    qw = LANES if compact else 2 * LANES
    has_bias = bias is not None
    n_in, n_out = (4, 3) if has_bias else (3, 2)

    def body(*refs):
        step = pl.program_id(0) * HEAD_PAIRS + pl.program_id(1)
        refs, ride_end = _ride_steps(ride, refs, n_in, n_out, step, nb * HEAD_PAIRS)
        if has_bias:
            q_ref, k_ref, v_ref, bias_ref, o_ref, lse_ref, o32_ref = refs
        else:
            q_ref, k_ref, v_ref, o_ref, lse_ref = refs
        masks = _head_masks()
        lo = masks[0]

        def update(r0, n, carry, k0, nk, mask):
            qs = _pair_operands(q_ref, r0, n, compact, masks, True)
            ks = _pair_operands(k_ref, k0, nk, compact, masks, False)
            vv = v_ref[pl.ds(k0, nk), :]
            vs = [jnp.where(mk, vv, jnp.zeros_like(vv)) for mk in masks]
            new, alphas, pv = [], [], None
            for h in range(2):
                m, l = carry[1 + 2 * h], carry[2 + 2 * h]
                s = lax.dot_general(qs[h], ks[h], _DIMS["nt"], preferred_element_type=F32)
                if has_bias:
                    s = s + bias_ref[h, 0:1, pl.ds(k0, nk)]
                if mask is not None:
                    s = jnp.where(mask, s, -jnp.inf)
                m_new = jnp.maximum(m, jnp.max(s, axis=-1, keepdims=True))
                p = jnp.exp(s - m_new)
                alpha = jnp.exp(m - m_new)
                l_new = alpha * l + jnp.sum(p, axis=-1, keepdims=True)
                p_hi = p.astype(BF)
                d = jnp.dot(p_hi, vs[h], preferred_element_type=F32)
                if has_bias:
                    p_lo = (p - p_hi.astype(F32)).astype(BF)
                    d = d + jnp.dot(p_lo, vs[h], preferred_element_type=F32)
                pv = d if pv is None else pv + d
                alphas.append(alpha)
                new += [m_new, l_new]
            return (carry[0] * jnp.where(lo, alphas[0], alphas[1]) + pv, *new)

        def q_block(iq, _):
            q0 = pl.multiple_of(iq * blk, blk)
            init = (jnp.zeros((blk, LANES), F32),
                    jnp.full((blk, 1), -jnp.inf, F32), jnp.zeros((blk, 1), F32),
                    jnp.full((blk, 1), -jnp.inf, F32), jnp.zeros((blk, 1), F32))
            full = lambda j, c: update(q0, blk, c, pl.multiple_of(j * blk, blk), blk, None)
            carry = lax.fori_loop(0, iq // 2, lambda jj, c: full(2 * jj + 1, full(2 * jj, c)), init)

            def last(c, odd):
                if odd:
                    c = full(iq - 1, c)
                acc, m0, l0, m1, l1 = update(q0, blk, c, q0, blk, _causal(blk, blk, 0))
                o_val = acc / jnp.where(lo, l0, l1)
                o_ref[pl.ds(q0, blk), :] = o_val.astype(BF)
                if has_bias:
                    o32_ref[pl.ds(q0, blk), :] = o_val
                lse_ref[pl.ds(q0, blk), :] = jnp.where(lo, m0 + jnp.log(l0), m1 + jnp.log(l1))
                return 0

            return lax.cond(iq % 2 == 1, functools.partial(last, odd=True), functools.partial(last, odd=False), carry)

        lax.fori_loop(0, nq, q_block, 0)
        ride_end()

    in_specs = [pl.BlockSpec((seq, qw), lambda b, p: (b, q_off + p)),
                pl.BlockSpec((seq, qw), lambda b, p: (b, k_off + p)),
                pl.BlockSpec((seq, LANES), lambda b, p: (b, v_off + p))]
    args = [q_arr, k_arr, v_arr]
    if has_bias:
        in_specs.append(pl.BlockSpec((None, 2, 8, seq), lambda b, p: (b, p, 0, 0)))
        args.append(bias)
    out_blk = pl.BlockSpec((seq, LANES), lambda b, p: (b, p))
    out_shape = [jax.ShapeDtypeStruct((t, HEAD_PAIRS * LANES), BF), jax.ShapeDtypeStruct((t, HEAD_PAIRS * LANES), F32)]
    if has_bias:
        out_shape.append(jax.ShapeDtypeStruct((t, HEAD_PAIRS * LANES), F32))
    out_specs, scratch = [out_blk] * len(out_shape), []
    if ride is not None:
        r_in, r_shape, r_out, scratch = _ride_specs(ride)
        in_specs, out_shape, out_specs = in_specs + r_in, out_shape + r_shape, out_specs + r_out
        args += list(ride[1])
    return pl.pallas_call(
        body, name=name, grid=(nb, HEAD_PAIRS),
        out_shape=out_shape, in_specs=in_specs, out_specs=out_specs, scratch_shapes=scratch,
        compiler_params=_params(("arbitrary", "arbitrary")),
    )(*args)


def _attn_bwd(q_arr, q_off, k_arr, k_off, v_arr, v_off, bias, o, do, lse, seq, name, ride=None, compact=False):
    t = q_arr.shape[0]
    nb = t // seq
    blk = min(ATTN_BLOCK, seq)
    half = blk // 2
    nq = seq // blk
    qw = LANES if compact else 2 * LANES
    has_bias = bias is not None
    n_in, n_out = (7, 4) if has_bias else (6, 3)

    def body(*refs):
        step = pl.program_id(0) * HEAD_PAIRS + pl.program_id(1)
        refs, ride_end = _ride_steps(ride, refs, n_in, n_out, step, nb * HEAD_PAIRS)
        if has_bias:
            (q_ref, k_ref, v_ref, bias_ref, o_ref, do_ref, lse_ref,
             dq_ref, dk_ref, dv_ref, dbias_ref, dq_acc, dsum) = refs
        else:
            (q_ref, k_ref, v_ref, o_ref, do_ref, lse_ref, dq_ref, dk_ref, dv_ref, dq_acc, dsum) = refs
        masks = _head_masks()
        lo, hi = masks
        dq_acc[...] = jnp.zeros_like(dq_acc)

        def prep(iq, _):
            q0 = pl.multiple_of(iq * blk, blk)
            prod = do_ref[pl.ds(q0, blk), :].astype(F32) * o_ref[pl.ds(q0, blk), :].astype(F32)
            d0 = jnp.sum(jnp.where(lo, prod, 0.0), axis=-1, keepdims=True)
            d1 = jnp.sum(jnp.where(hi, prod, 0.0), axis=-1, keepdims=True)
            dsum[pl.ds(q0, blk), :] = jnp.where(lo, d0, d1)
            return 0

        lax.fori_loop(0, nq, prep, 0)

        def tile(r0, n, k0, nk, mask):
            qs = _pair_operands(q_ref, r0, n, compact, masks, True)
            ks = _pair_operands(k_ref, k0, nk, compact, masks, True)
            vv = v_ref[pl.ds(k0, nk), :]
            vs = [jnp.where(mk, vv, jnp.zeros_like(vv)) for mk in masks]
            dov = do_ref[pl.ds(r0, n), :]
            dos = [jnp.where(mk, dov, jnp.zeros_like(dov)) for mk in masks] if compact else None
            lse_v = lse_ref[pl.ds(r0, n), :]
            dsum_v = dsum[pl.ds(r0, n), :]
            dv_c, dks, dbs = None, [], []
            for h in range(2):
                s = lax.dot_general(qs[h], ks[h], _DIMS["nt"], preferred_element_type=F32)
                if has_bias:
                    s = s + bias_ref[h, 0:1, pl.ds(k0, nk)]
                p = jnp.exp(s - lse_v[:, h * HEAD_DIM:h * HEAD_DIM + 1])
                if mask is not None:
                    p = jnp.where(mask, p, 0.0)
                dp = lax.dot_general(dov, vs[h], _DIMS["nt"], preferred_element_type=F32)
                ds = p * (dp - dsum_v[:, h * HEAD_DIM:h * HEAD_DIM + 1])
                ds_bf = ds.astype(BF)
                if compact:
                    dv_h = lax.dot_general(p.astype(BF), dos[h], _DIMS["tn"], preferred_element_type=F32)
                else:
                    dv_h = jnp.where(masks[h], lax.dot_general(p.astype(BF), dov, _DIMS["tn"],
                                                               preferred_element_type=F32), 0.0)
                dv_c = dv_h if dv_c is None else dv_c + dv_h
                dk_h = lax.dot_general(ds_bf, qs[h], _DIMS["tn"], preferred_element_type=F32)
                dq_h = jnp.dot(ds_bf, ks[h], preferred_element_type=F32)
                if compact:
                    dks = [dk_h] if h == 0 else [dks[0] + dk_h, jnp.zeros((8, LANES), F32)]
                    if h == 0:
                        dq_first = dq_h
                    else:
                        dq_acc[pl.ds(r0, n), :] += dq_first + dq_h
                else:
                    dks.append(dk_h)
                    dq_acc[pl.ds(r0, n), h * LANES:(h + 1) * LANES] += dq_h
                dbs.append(jnp.sum(ds, axis=0, keepdims=True) if has_bias else jnp.zeros((1, nk), F32))
            return (dv_c, dks[0], dks[1], dbs[0], dbs[1])

        def kv_block(j, _):
            k0 = pl.multiple_of(j * blk, blk)
            if compact:
                dv_a, dk_a, dummy, db0_a, db1_a = tile(pl.multiple_of(k0 + half, half), half, k0, blk,
                                                       _causal(half, blk, half))
                top = tile(k0, half, k0, half, _causal(half, half, 0))
                head = lambda acc, x: jnp.concatenate([acc[:half] + x, acc[half:]], axis=0)
                lead = lambda acc, x: jnp.concatenate([acc[:, :half] + x, acc[:, half:]], axis=1)
                carry = (head(dv_a, top[0]), head(dk_a, top[1]), dummy, lead(db0_a, top[3]), lead(db1_a, top[4]))
            else:
                carry = tile(k0, blk, k0, blk, _causal(blk, blk, 0))

            def q_block(iq, c):
                part = tile(pl.multiple_of(iq * blk, blk), blk, k0, blk, None)
                return tuple(a + b for a, b in zip(c, part))

            n_full = nq - 1 - j
            carry = lax.fori_loop(0, n_full // 2, lambda jj, c: q_block(j + 2 + 2 * jj, q_block(j + 1 + 2 * jj, c)),
                                  carry)

            def last(c, odd):
                if odd:
                    c = q_block(nq - 1, c)
                dv_ref[pl.ds(k0, blk), :] = c[0].astype(BF)
                if compact:
                    dk_ref[pl.ds(k0, blk), :] = c[1].astype(BF)
                else:
                    for h in range(2):
                        dk_ref[pl.ds(k0, blk), h * LANES:(h + 1) * LANES] = c[1 + h].astype(BF)
                if has_bias:
                    for h in range(2):
                        dbias_ref[h, :, pl.ds(k0, blk)] = jnp.broadcast_to(c[3 + h], (8, blk))
                return 0

            return lax.cond(n_full % 2 == 1, functools.partial(last, odd=True), functools.partial(last, odd=False),
                            carry)

        lax.fori_loop(0, nq, kv_block, 0)
        dq_ref[...] = dq_acc[...].astype(BF)
        ride_end()

    pair256 = lambda off: pl.BlockSpec((seq, qw), lambda b, p: (b, off + p))
    pair128 = lambda off: pl.BlockSpec((seq, LANES), lambda b, p: (b, off + p))
    bias_spec = pl.BlockSpec((None, 2, 8, seq), lambda b, p: (b, p, 0, 0))
    in_specs = [pair256(q_off), pair256(k_off), pair128(v_off)]
    args = [q_arr, k_arr, v_arr]
    if has_bias:
        in_specs.append(bias_spec)
        args.append(bias)
    in_specs += [pair128(0), pair128(0), pair128(0)]
    args += [o, do, lse]
    out_shape = [jax.ShapeDtypeStruct((t, HEAD_PAIRS * qw), BF),
                 jax.ShapeDtypeStruct((t, HEAD_PAIRS * qw), BF),
                 jax.ShapeDtypeStruct((t, HEAD_PAIRS * LANES), BF)]
    out_specs = [pair256(0), pair256(0), pair128(0)]
    if has_bias:
        out_shape.append(jax.ShapeDtypeStruct((nb, HEADS, 8, seq), F32))
        out_specs.append(bias_spec)
    scratch = [pltpu.VMEM((seq, qw), F32), pltpu.VMEM((seq, LANES), F32)]
    if ride is not None:
        r_in, r_shape, r_out, r_scratch = _ride_specs(ride)
        in_specs, out_shape, out_specs = in_specs + r_in, out_shape + r_shape, out_specs + r_out
        args += list(ride[1])
        scratch += r_scratch
    return pl.pallas_call(
        body, name=name, grid=(nb, HEAD_PAIRS),
        out_shape=out_shape, in_specs=in_specs, out_specs=out_specs, scratch_shapes=scratch,
        compiler_params=_params(("arbitrary", "arbitrary")),
    )(*args)


def _adamw(w, g, m, v, name):
    shape = w.shape
    last = shape[-1]
    rows = int(np.prod(shape[:-1])) if len(shape) > 1 else 1
    tr = _rows(rows, 512)

    def body(w_ref, g_ref, m_ref, v_ref, d_ref, nm_ref, nv_ref):
        d_ref[...], nm_ref[...], nv_ref[...] = _adamw_math(w_ref[...], g_ref[...], m_ref[...], v_ref[...])

    blk = pl.BlockSpec((tr, last), lambda i: (i, 0))
    sds = jax.ShapeDtypeStruct((rows, last), F32)
    outs = pl.pallas_call(
        body, name=name, grid=(rows // tr,),
        out_shape=(sds, sds, sds), in_specs=[blk] * 4, out_specs=(blk,) * 3,
        compiler_params=_params(("parallel",)),
    )(*[a.reshape(rows, last) for a in (w, g, m, v)])
    return tuple(a.reshape(shape) for a in outs)


LOW_COLS = 256


def _low_pad(a):
    return jnp.pad(a, ((0, 0),) * (a.ndim - 1) + ((0, LOW_COLS - a.shape[-1]),))


def _layer_shards(w, i):
    j = i // 2
    bf = lambda a: a.astype(BF)
    if i % 2 == 0:
        mixer = [bf(w["fox_w_in"][j]), bf(w["fox_w_out"][j])]
    else:
        mixer = [jnp.concatenate([bf(w["mla_w_dq"][j]), bf(w["mla_w_ukv"][j]), _low_pad(bf(w["mla_w_uq"][j])),
                                  _low_pad(bf(w["mla_w_dkv"][j]))], axis=0), bf(w["mla_w_out"][j])]
    return mixer + [bf(w["mlp_w1"][i]), bf(w["mlp_w2"][i])]


def _side_by_side(stack, r0, rows, cols=None):
    return jnp.concatenate([stack[dd, r0:r0 + rows, :cols] for dd in range(N_DEV)], axis=1)


def _stacked(stack, r0, rows, cols=None):
    part = stack[:, r0:r0 + rows, :cols]
    return part.reshape(N_DEV * rows, part.shape[2])


def _layer_mixer_weights(i, first):
    if i % 2 == 0:
        return dict(fox_w_in=_side_by_side(first, 0, 1024))
    return dict(mla_w_dq=_stacked(first, 0, 128), mla_w_ukv=_side_by_side(first, 128, 128),
                mla_w_uq=_side_by_side(first, 256, 256, 192), mla_w_dkv=_stacked(first, 512, 128, 160))


def _by_dest_rows(g):
    return g.reshape(N_DEV, g.shape[0] // N_DEV, g.shape[1]).astype(BF)


def _by_dest_cols(g):
    n = g.shape[1] // N_DEV
    return jnp.stack([g[:, dd * n:(dd + 1) * n] for dd in range(N_DEV)]).astype(BF)


def _layer_mixer_grad_bufs(i, g):
    if i % 2 == 0:
        return [_by_dest_cols(g["fox_w_in"]), _by_dest_rows(g["w_out"])]
    low = jnp.concatenate([_by_dest_rows(g["mla_w_dq"]), _by_dest_cols(g["mla_w_ukv"]),
                           _low_pad(_by_dest_cols(g["mla_w_uq"])), _low_pad(_by_dest_rows(g["mla_w_dkv"]))], axis=1)
    return [low, _by_dest_rows(g["w_out"])]


def _low_shard_grads(low):
    return dict(mla_w_dq=low[:128], mla_w_ukv=low[128:256], mla_w_uq=low[256:512, :192], mla_w_dkv=low[512:, :160])


def _pad_heads(w, width):
    k = w.shape[0]
    return jnp.pad(w.reshape(k, HEADS, width), ((0, 0), (0, 0), (0, LANES - width))).reshape(k, HEADS * LANES)


def _unpad_heads(w, width):
    k = w.shape[0]
    return w.reshape(k, HEADS, LANES)[:, :, :width].reshape(k, HEADS * width)


def _rope_tables(positions, scale):
    inv_freq = 10000.0 ** (-jnp.arange(0, 2 * ROPE_HALF, 2, dtype=F32) / (2 * ROPE_HALF))
    ang = positions.astype(F32)[:, None] * inv_freq
    cos, sin = jnp.cos(ang) * scale, jnp.sin(ang) * scale
    t = positions.shape[0]
    z = lambda n: jnp.zeros((t, n), F32)
    cos_p = jnp.concatenate([jnp.full((t, HEAD_DIM), scale, F32), cos, cos, z(32)], axis=1)
    sin_a = jnp.concatenate([z(64), -sin, z(48)], axis=1)
    sin_b = jnp.concatenate([z(80), sin, z(32)], axis=1)
    fwd = (cos_p, sin_a, sin_b)
    bwd = (cos_p, jnp.roll(sin_b, -ROPE_HALF, axis=1), jnp.roll(sin_a, ROPE_HALF, axis=1))
    return fwd, bwd


def _key_rows(cum, nb, seq):
    v = -cum.reshape(nb, seq, LANES)[:, :, :HEADS]
    return jnp.broadcast_to(jnp.transpose(v, (0, 2, 1))[:, :, None, :], (nb, HEADS, 8, seq))


def kernel(x, c, positions, ada_w, ada_b, norm_mix_g, norm_mlp_g, fox_w_in, fox_b_f, fox_w_out, mla_w_dq, mla_q_norm_g, mla_w_uq, mla_w_dkv, mla_kv_norm_g, mla_w_ukv, mla_w_out, mlp_w1, mlp_w2, final_norm_g, loss_target, m_ada_w, m_ada_b, m_norm_mix_g, m_norm_mlp_g, m_fox_w_in, m_fox_b_f, m_fox_w_out, m_mla_w_dq, m_mla_q_norm_g, m_mla_w_uq, m_mla_w_dkv, m_mla_kv_norm_g, m_mla_w_ukv, m_mla_w_out, m_mlp_w1, m_mlp_w2, m_final_norm_g, v_ada_w, v_ada_b, v_norm_mix_g, v_norm_mlp_g, v_fox_w_in, v_fox_b_f, v_fox_w_out, v_mla_w_dq, v_mla_q_norm_g, v_mla_w_uq, v_mla_w_dkv, v_mla_kv_norm_g, v_mla_w_ukv, v_mla_w_out, v_mlp_w1, v_mlp_w2, v_final_norm_g):
    args = dict(locals())
    weights = {n: args[n] for n in WEIGHTS}
    nb, seq, d = x.shape
    t = nb * seq
    depth = ada_w.shape[0]
    dev = 4 * lax.axis_index("x") + 2 * lax.axis_index("y") + lax.axis_index("c")
    n_mod_local = ada_w.shape[2]

    n_qg = mla_q_norm_g.shape[1]
    cond = jnp.concatenate([c, jnp.pad(mla_q_norm_g.reshape(1, -1), ((0, 7), (0, d - 2 * n_qg)))], axis=0)
    w1_rows, w2_rows = mlp_w1.shape[1], mlp_w2.shape[1]
    shards = [_layer_shards(weights, i) for i in range(depth)]
    stacks = [None] * depth
    cond_all, = _all_gather([cond], "gather_cond")
    c_all = cond_all[:, :nb].reshape(N_DEV * nb, d)
    q_gain = jnp.transpose(cond_all[:, nb, :2 * n_qg].reshape(N_DEV, 2, n_qg), (1, 0, 2)).reshape(2, N_DEV * n_qg)
    mod_local = jnp.stack([
        _matmul(c_all, ada_w[i], mode="nn", name="ada_mod", out_dtype=F32, a_act="silu", epi="bias",
                extras=(lax.dynamic_slice_in_dim(ada_b[i], dev * n_mod_local, n_mod_local)[None, :],))
        for i in range(depth)])
    mod_all, first_in = _all_gather([mod_local.reshape(depth * N_DEV * nb, n_mod_local), shards[0][0]], "gather_first")
    stacks[0] = [first_in]
    mod_all = jnp.transpose(mod_all.reshape(N_DEV, depth, N_DEV * nb, n_mod_local), (1, 2, 0, 3))
    mod_all = mod_all.reshape(depth, N_DEV * nb, N_DEV * n_mod_local)
    mod = lax.dynamic_slice_in_dim(mod_all, dev * nb, nb, axis=1)
    mod = mod.reshape(depth, nb, 6, 1, d)

    pos = positions.reshape(t)
    rope_q, rope_q_t = _rope_tables(pos, MLA_SCALE)
    rope_k, rope_k_t = _rope_tables(pos, 1.0)

    def fox_weights(full):
        w_in = full["fox_w_in"]
        w_qkv = jnp.concatenate([w_in[:, :d] * FOX_SCALE, w_in[:, d:3 * d]], axis=1)
        w_f = jnp.pad(w_in[:, 3 * d:], ((0, 0), (0, LANES - HEADS)))
        return w_qkv, w_f

    def mla_weights(full):
        w_dkv = full["mla_w_dkv"]
        w_down = jnp.concatenate([full["mla_w_dq"], w_dkv[:, :128], jnp.zeros((d, 64), BF),
                                  w_dkv[:, 128:160], jnp.zeros((d, 32), BF)], axis=1)
        w_uq = _pad_heads(full["mla_w_uq"], 96)
        w_ukv = full["mla_w_ukv"].reshape(128, HEADS, 2, HEAD_DIM)
        w_uk = jnp.pad(w_ukv[:, :, 0, :], ((0, 0), (0, 0), (0, 64))).reshape(128, HEADS * LANES)
        w_uv = w_ukv[:, :, 1, :].reshape(128, HEADS * HEAD_DIM)
        place = np.zeros((128, HEADS, LANES), np.float32)
        for i in range(2 * ROPE_HALF):
            place[64 + i, :, 64 + i] = 1.0
        bottom = jnp.concatenate([jnp.asarray(place.reshape(128, HEADS * LANES), BF),
                                  jnp.zeros((128, HEADS * HEAD_DIM), BF)], axis=1)
        w_kv = jnp.concatenate([jnp.concatenate([w_uk, w_uv], axis=1), bottom], axis=0)
        return w_down, w_uq, w_kv

    tm_big = min(2048, t)
    xs = x.reshape(t, d)
    saved = []
    for i in range(depth):
        j = i // 2
        sh_m, sc_m, g_m, sh_f, sc_f, g_f = (mod[i, :, q] for q in range(6))
        gain_mix = norm_mix_g[i][None, :]
        gain_mlp = norm_mlp_g[i][None, :]
        s = dict(x_in=xs)
        h = _norm_mod(xs, gain_mix, sc_m, sh_m, seq, "norm_mix")
        s["h"] = h
        full = _layer_mixer_weights(i, stacks[i][0])
        riders = shards[i][1 if i == 0 else 2:] + (shards[i + 1][:2] if i + 1 < depth else [])
        ride = ("gather", riders)
        if i % 2 == 0:
            w_qkv, w_f = fox_weights(full)
            qkv = _matmul(h, w_qkv, mode="nn", name="fox_qkv")
            fg = _matmul(h, w_f, mode="nn", name="fox_gate_logits", out_dtype=F32)
            b_f = jnp.pad(fox_b_f[j], (0, LANES - HEADS))[None, :]
            cum = _fox_gate(fg, b_f, seq, "fox_gate")
            bias = _key_rows(cum, nb, seq)
            o, lse, o32, *rode = _attn_fwd(qkv, 0, qkv, 8, qkv, 16, bias, seq, "fox_attn", ride, compact=True)
            s.update(qkv=qkv, fg=fg, b_f=b_f, bias=bias, w_qkv=w_qkv, w_f=w_f, o32=o32)
        else:
            w_down, w_uq, w_kv = mla_weights(full)
            down = _matmul(h, w_down, mode="nn", name="mla_down", out_dtype=F32)
            gq, gkv = q_gain[j][None, :], mla_kv_norm_g[j][None, :]
            cq, ckr = _mla_mid(down, gq, gkv, rope_k, "mla_mid")
            q_raw = _matmul(cq, w_uq, mode="nn", name="mla_uq", out_dtype=F32)
            q_rot = _rope(q_raw, rope_q, "mla_rope_q")
            kv = _matmul(ckr, w_kv, mode="nn", name="mla_ukv")
            o, lse, *rode = _attn_fwd(q_rot, 0, kv, 0, kv, 16, None, seq, "mla_attn", ride)
            s.update(down=down, gq=gq, gkv=gkv, cq=cq, ckr=ckr, q_rot=q_rot, kv=kv,
                     w_down=w_down, w_uq=w_uq, w_kv=w_kv)
        n_own = 3 if i == 0 else 2
        stacks[i], rode = stacks[i] + rode[:n_own], rode[n_own:]
        if i + 1 < depth:
            stacks[i + 1] = rode
        w_out = _stacked(stacks[i][1], 0, 128)
        xs, y = _matmul(o, w_out, mode="nn", name="attn_out", epi="resid_gate", extras=(xs, g_m), seq=seq)
        s.update(o=o, lse=lse, y=y, w_out=w_out, x_mid=xs)
        h2 = _norm_mod(xs, gain_mlp, sc_f, sh_f, seq, "norm_mlp")
        a_pre = _matmul(h2, stacks[i][2], mode="nn", name="mlp_up", layer=("col", 0, w1_rows), tm=tm_big)
        xs, y2 = _matmul(a_pre, stacks[i][3], mode="nn", name="mlp_down", layer=("row", 0, w2_rows), a_act="relu2",
                         epi="resid_gate", extras=(xs, g_f), seq=seq)
        s.update(h2=h2, a_pre=a_pre, y2=y2)
        saved.append(s)

    loss_part, dx, dg_final = _loss_head(xs, final_norm_g[None, :], loss_target.reshape(t, d), "loss_head")

    w1_cols, w1_tm = mlp_w1.shape[2], _pick(w1_rows, 1024)
    dg_mix, dg_mlp, db_f, dg_kv, dg_q = [None] * depth, [None] * depth, [None] * 2, [None] * 2, [None] * 2
    dmod = [None] * depth
    me = dev.astype(jnp.int32).reshape(1)
    chains = dict(fox_w_in=None, fox_w_out=None, mla_w_out=None, mlp_w1=None, mlp_w2=None)
    low_grads = [None] * 2

    def adam_step(name, layer):
        def land_one(buf, got):
            chains[name] = _total_adamw(buf, me, got, weights[name], args["m_" + name], args["v_" + name], layer,
                                        chains[name], "adamw_" + name)
        return land_one

    def low_step(lj):
        def land_one(buf, got):
            low_grads[lj] = _low_shard_grads(_add_parts(buf, me, got, "grads_total"))
        return land_one

    def mixer_steps(li):
        lj = li // 2
        return ([adam_step("fox_w_in", lj), adam_step("fox_w_out", lj)] if li % 2 == 0
                else [low_step(lj), adam_step("mla_w_out", lj)])

    def mlp_steps(li):
        return [adam_step("mlp_w1", li), adam_step("mlp_w2", li)]

    def land(staged, got):
        for (step, buf), g in zip(staged, got):
            step(buf, g)

    waiting = []
    dy2, dg_f = _gate_bwd(dx, saved[depth - 1]["y2"], mod[depth - 1, :, 5], seq, "gate_bwd")
    for i in reversed(range(depth)):
        j = i // 2
        s = saved[i]
        sh_m, sc_m, g_m, sh_f, sc_f, g_f = (mod[i, :, q] for q in range(6))
        da_pre = _matmul(dy2, stacks[i][3], mode="nt", name="mlp_down_dx", layer=("row", 0, w2_rows), epi="mul_drelu",
                         extras=(s["a_pre"],), tm=tm_big)
        g_w2 = _matmul(s["a_pre"], dy2, mode="tn", name="mlp_down_dw", a_act="relu2", tm=w2_rows,
                       into=(jax.ShapeDtypeStruct((N_DEV, w2_rows, d), BF), (None, w2_rows, d),
                             lambda r, j, k: (r, 0, 0)))
        dh2 = _matmul(da_pre, stacks[i][2], mode="nt", name="mlp_up_dx", layer=("col", 0, w1_rows), tm=tm_big)
        g_w1 = _matmul(s["h2"], da_pre, mode="tn", name="mlp_up_dw", tm=w1_tm, tn=w1_cols,
                       into=(jax.ShapeDtypeStruct((N_DEV, w1_rows, w1_cols), BF), (None, w1_tm, w1_cols),
                             lambda r, j, k: (j, r, 0)))
        waiting += list(zip(mlp_steps(i), [g_w1, g_w2]))
        dx, dg_mlp[i], dsc_f, dsh_f, dy, dg_m = _norm_mod_bwd(dh2, s["x_mid"], norm_mlp_g[i][None, :], sc_f, dx, seq,
                                                              "norm_bwd_gate", gate=(s["y"], g_m))
        do = _matmul(dy, s["w_out"], mode="nt", name="attn_out_dx")
        dw_out = _matmul(s["o"], dy, mode="tn", name="attn_out_dw", out_dtype=F32)
        if i == 0:
            waiting.append((adam_step("fox_w_out", 0), _by_dest_rows(dw_out)))
        ride = ("scatter", [e[1] for e in waiting]) if waiting else None
        g_mixer = dict(w_out=dw_out)
        if i % 2 == 0:
            qkv = s["qkv"]
            dq, dk, dv, dbias, *rode = _attn_bwd(qkv, 0, qkv, 8, qkv, 16, s["bias"], s["o32"], do, s["lse"], seq,
                                                 "fox_attn_bwd", ride, compact=True)
            dqkv = jnp.concatenate([dq, dk, dv], axis=1)
            d_cum = -jnp.transpose(dbias[:, :, 0, :], (0, 2, 1)).reshape(t, HEADS)
            d_cum = jnp.pad(d_cum, ((0, 0), (0, LANES - HEADS)))
            dfg, db = _fox_gate_bwd(d_cum, s["fg"], s["b_f"], seq, "fox_gate_bwd")
            db_f[j] = db
            dh = _matmul(dfg, s["w_f"], mode="nt", name="fox_gate_dx", out_dtype=F32)
            dh = _matmul(dqkv, s["w_qkv"], mode="nt", name="fox_qkv_dx", epi="add", extras=(dh,))
            dw_qkv = _matmul(s["h"], dqkv, mode="tn", name="fox_qkv_dw", out_dtype=F32)
            dw_f = _matmul(s["h"], dfg, mode="tn", name="fox_gate_dw", out_dtype=F32)
            g_mixer["fox_w_in"] = jnp.concatenate([dw_qkv[:, :d] * FOX_SCALE, dw_qkv[:, d:], dw_f[:, :HEADS]], axis=1)
        else:
            kv = s["kv"]
            dq, dk, dv, *rode = _attn_bwd(s["q_rot"], 0, kv, 0, kv, 16, None, s["o"], do, s["lse"], seq,
                                          "mla_attn_bwd", ride)
            dq_raw = _rope(dq, rope_q_t, "mla_rope_q_bwd")
            dcq = _matmul(dq_raw, s["w_uq"], mode="nt", name="mla_uq_dx")
            dw_uq = _matmul(s["cq"], dq_raw, mode="tn", name="mla_uq_dw", out_dtype=F32)
            dkv = jnp.concatenate([dk, dv], axis=1)
            dckr = _matmul(dkv, s["w_kv"], mode="nt", name="mla_ukv_dx")
            dw_kv = _matmul(s["ckr"], dkv, mode="tn", name="mla_ukv_dw", out_dtype=F32)
            d_down, dgq, dgkv = _mla_mid_bwd(s["down"], dcq, dckr, s["gq"], s["gkv"], rope_k_t, "mla_mid_bwd")
            dg_q[j], dg_kv[j] = dgq, dgkv
            dh = _matmul(d_down, s["w_down"], mode="nt", name="mla_down_dx")
            dw_down = _matmul(s["h"], d_down, mode="tn", name="mla_down_dw", out_dtype=F32)
            g_mixer["mla_w_dq"] = dw_down[:, :256]
            g_mixer["mla_w_dkv"] = jnp.concatenate([dw_down[:, 256:384], dw_down[:, 448:480]], axis=1)
            g_mixer["mla_w_uq"] = _unpad_heads(dw_uq, 96)
            dk_nope = dw_kv[:128, :HEADS * LANES].reshape(128, HEADS, LANES)[:, :, :HEAD_DIM]
            dv_w = dw_kv[:128, HEADS * LANES:].reshape(128, HEADS, HEAD_DIM)
            g_mixer["mla_w_ukv"] = jnp.concatenate([dk_nope, dv_w], axis=2).reshape(128, HEADS * LANES)
        land(waiting, rode)
        this_dg_f = dg_f
        if i > 0:
            dx, dg_mix[i], dsc_m, dsh_m, dy2, dg_f = _norm_mod_bwd(
                dh, s["x_in"], norm_mix_g[i][None, :], sc_m, dx, seq, "norm_bwd_gate",
                gate=(saved[i - 1]["y2"], mod[i - 1, :, 5]))
            waiting = list(zip(mixer_steps(i), _layer_mixer_grad_bufs(i, g_mixer)))
        else:
            dx, dg_mix[i], dsc_m, dsh_m = _norm_mod_bwd(dh, s["x_in"], norm_mix_g[i][None, :], sc_m, dx, seq, "norm_bwd")
            last = [(adam_step("fox_w_in", 0), _by_dest_cols(g_mixer["fox_w_in"]))]
        dmod[i] = jnp.stack([dsh_m, dsc_m, dg_m, dsh_f, dsc_f, this_dg_f], axis=1).reshape(nb, 6 * d)

    grad_x = dx.reshape(nb, seq, d)
    shard_grads = {n: jnp.stack([low_grads[0][n], low_grads[1][n]]) for n in low_grads[0]}

    dmod_arr = jnp.stack(dmod)
    wide = lambda a: jnp.pad(a, ((0, 0), (0, d - a.shape[1])))
    pieces = [wide(loss_part), *dg_mix, *dg_mlp, *[wide(a) for a in db_f], *[wide(a) for a in dg_kv], dg_final,
              *[wide(a) for a in dg_q], jnp.sum(dmod_arr, axis=1).reshape(depth * 6, d)]
    n_small = sum(p.shape[0] for p in pieces)
    both = jnp.concatenate(pieces + [dmod_arr.reshape(depth * nb * 6, d)], axis=0)
    both = jnp.pad(both, ((0, (-both.shape[0]) % 8), (0, 0)))
    scattered, (both_all,) = _scatter_and_gather([e[1] for e in last], [both], "last_exchange")
    land(last, scattered)
    done = {n: tuple(a.reshape(weights[n].shape) for a in chain) for n, chain in chains.items()}
    total = _sum_leading(both_all, "sum_small")
    off = 0

    def take(rows):
        nonlocal off
        out = total[off:off + rows]
        off += rows
        return out

    loss = take(1)[0, 0]
    g_small = dict(
        norm_mix_g=take(depth), norm_mlp_g=take(depth), fox_b_f=take(2)[:, :HEADS], mla_kv_norm_g=take(2)[:, :128],
        final_norm_g=take(1)[0],
        mla_q_norm_g=lax.dynamic_slice_in_dim(take(2)[:, :N_DEV * n_qg], dev * n_qg, n_qg, axis=1),
        ada_b=take(depth * 6).reshape(depth, 6 * d))
    dmod_all = both_all[:, n_small:n_small + depth * nb * 6]
    dmod_all = jnp.transpose(dmod_all.reshape(N_DEV, depth, nb, 6 * d), (1, 0, 2, 3)).reshape(depth, N_DEV * nb, 6 * d)
    dmod_cols = lax.dynamic_slice_in_dim(dmod_all, dev * n_mod_local, n_mod_local, axis=2)
    g_ada_w = jnp.stack([_matmul(c_all, dmod_cols[i], mode="tn", name="ada_dw", out_dtype=F32, a_act="silu")
                         for i in range(depth)])

    all_grads = dict(shard_grads)
    all_grads.update(g_small)
    all_grads["ada_w"] = g_ada_w

    deltas, new_m, new_v = {}, {}, {}
    for n in WEIGHTS:
        if n in done:
            all_grads[n], deltas[n], new_m[n], new_v[n] = done[n]
        else:
            deltas[n], new_m[n], new_v[n] = _adamw(weights[n], all_grads[n], args["m_" + n], args["v_" + n], "adamw")

    return (loss, grad_x, *[all_grads[n] for n in WEIGHTS], *[deltas[n] for n in WEIGHTS],
            *[new_m[n] for n in WEIGHTS], *[new_v[n] for n in WEIGHTS])
```
